```python
import jax, jax.numpy as jnp
from jax import lax
import numpy as np

D_MODEL = 2048
BATCH = 8
SEQ = 8192
DEPTH = 2

N_META = 16
BLOCK = 128
PAD = BLOCK - N_META
EPS = 1e-6
NEG = -1e30

FOX_HEADS = 8
FOX_HEAD_DIM = D_MODEL // 16
FOX_WIDTH = FOX_HEADS * FOX_HEAD_DIM

CONV_CH = D_MODEL // 2
CONV_K = 3

GLA_HEADS = 4
GLA_DK = D_MODEL // 16
GLA_DV = D_MODEL // 8
GLA_RANK = 16
GLA_TAU = 16.0

D_FF = D_MODEL * 11 // 4
MLP_CONV_K = 3

SPLIT_SIZES = (FOX_WIDTH, FOX_WIDTH, FOX_WIDTH, FOX_HEADS,
               CONV_CH, CONV_CH, CONV_CH,
               GLA_HEADS * GLA_DK, GLA_HEADS * GLA_DK, GLA_HEADS * GLA_DV, GLA_HEADS * GLA_DV, GLA_RANK,
               D_MODEL, D_MODEL, D_MODEL)
N_IN = sum(SPLIT_SIZES)
SPLIT_POINTS = tuple(sum(SPLIT_SIZES[:i + 1]) for i in range(len(SPLIT_SIZES) - 1))
FOX_F_START = 3 * FOX_WIDTH

kernel_name = 'hybrid_fox_shortconv_gla_block'


def _rmsnorm(x, g):
    xf = x.astype(jnp.float32)
    y = xf * lax.rsqrt(jnp.mean(xf * xf, axis=-1, keepdims=True) + EPS)
    return (y * g.astype(jnp.float32)).astype(x.dtype)


def _causal_dwconv(x, w):
    k_width, ch = w.shape
    return lax.conv_general_dilated(x, w[:, None, :].astype(x.dtype), window_strides=(1,),
                                    padding=[(k_width - 1, 0)],
                                    dimension_numbers=('NWC', 'WIO', 'NWC'),
                                    feature_group_count=ch)


def _to_heads_padded(t, n_heads):
    b, l, w = t.shape
    t = t.reshape(b, l, n_heads, w // n_heads).transpose(0, 2, 1, 3)
    return jnp.pad(t, ((0, 0), (0, 0), (PAD, 0), (0, 0)))


def _from_heads_padded(t):
    b, h, lp, dh = t.shape
    return t[:, :, PAD:].transpose(0, 2, 1, 3).reshape(b, lp - PAD, h * dh)


def _fox_attention(q, k, v, logf):
    bn, nh, lp, dh = q.shape
    nb = lp // BLOCK
    c = jnp.cumsum(logf, axis=-1)
    pos = jnp.arange(lp)
    key_valid = pos >= PAD
    qb = jnp.moveaxis(q.reshape(bn, nh, nb, BLOCK, dh), 2, 0)
    cb = jnp.moveaxis(c.reshape(bn, nh, nb, BLOCK), 2, 0)
    scale = dh ** -0.5

    def one_block(args):
        qi, ci, i = args
        s = jnp.einsum('bhqd,bhkd->bhqk', qi, k, preferred_element_type=jnp.float32) * scale
        s = s + ci[..., None] - c[:, :, None, :]
        qpos = i * BLOCK + jnp.arange(BLOCK)
        mask = (pos[None, :] <= qpos[:, None]) & key_valid[None, :]
        p = jax.nn.softmax(jnp.where(mask, s, NEG), axis=-1)
        return jnp.einsum('bhqk,bhkd->bhqd', p.astype(v.dtype), v)

    out = lax.map(one_block, (qb, cb, jnp.arange(nb)))
    return jnp.moveaxis(out, 0, 2).reshape(bn, nh, lp, dh)


def _gla_chunked(q, k, v, logg):
    bn, nh, lp, dk = q.shape
    dv = v.shape[-1]
    nc = lp // BLOCK
    f32 = jnp.float32

    def to_chunks(t):
        return jnp.moveaxis(t.astype(f32).reshape(bn, nh, nc, BLOCK, t.shape[-1]), 2, 0)

    causal = jnp.tril(jnp.ones((BLOCK, BLOCK), dtype=bool))

    def step(state, inp):
        qc, kc, vc, gc = inp
        b = jnp.cumsum(gc, axis=2)
        diff = b[:, :, :, None, :] - b[:, :, None, :, :]
        decay = jnp.exp(jnp.where(causal[None, None, :, :, None], diff, -jnp.inf))
        att = jnp.einsum('bhtd,bhsd,bhtsd->bhts', qc, kc, decay)
        o = (jnp.einsum('bhts,bhsv->bhtv', att, vc)
             + jnp.einsum('bhtd,bhdv->bhtv', qc * jnp.exp(b), state))
        b_last = b[:, :, -1:, :]
        state = (jnp.exp(b_last[:, :, 0, :])[..., None] * state
                 + jnp.einsum('bhsd,bhsv->bhdv', kc * jnp.exp(b_last - b), vc))
        return state, o

    s0 = jnp.zeros((bn, nh, dk, dv), f32)
    _, o = lax.scan(step, s0, (to_chunks(q), to_chunks(k), to_chunks(v), to_chunks(logg)))
    return jnp.moveaxis(o, 0, 2).reshape(bn, nh, lp, dv)


def _fwd_setup_inputs(seed: int = 0) -> dict:
    key = jax.random.key(seed)
    ks = jax.random.split(key, 20)

    def nrm(k, shape, scale):
        return jax.random.normal(k, shape, jnp.float32) * scale

    w_in = nrm(ks[3], (DEPTH, D_MODEL, N_IN), D_MODEL ** -0.5)
    w_in = w_in.at[:, :, FOX_F_START:FOX_F_START + FOX_HEADS].multiply(0.1)
    return {
        'x': nrm(ks[0], (BATCH, SEQ, D_MODEL), 1.0),
        'meta_tokens': nrm(ks[1], (N_META, D_MODEL), 1.0),
        'norm1_g': 1.0 + nrm(ks[2], (DEPTH, D_MODEL), 0.1),
        'w_in': w_in,
        'fox_b_f': 3.0 + nrm(ks[4], (DEPTH, FOX_HEADS), 0.1),
        'gate_b': nrm(ks[5], (DEPTH, 3 * D_MODEL), 0.01),
        'conv_w': nrm(ks[6], (DEPTH, CONV_K, CONV_CH), CONV_K ** -0.5),
        'gla_w_g2': nrm(ks[7], (DEPTH, GLA_RANK, GLA_HEADS * GLA_DK), GLA_RANK ** -0.5),
        'gla_b_g': nrm(ks[8], (DEPTH, GLA_HEADS * GLA_DK), 0.1),
        'gla_norm_g': 1.0 + nrm(ks[9], (DEPTH, GLA_HEADS * GLA_DV), 0.1),
        'w_a_o': nrm(ks[10], (DEPTH, FOX_WIDTH, D_MODEL), FOX_WIDTH ** -0.5),
        'w_b_o': nrm(ks[11], (DEPTH, CONV_CH, D_MODEL), CONV_CH ** -0.5),
        'w_c_o': nrm(ks[12], (DEPTH, GLA_HEADS * GLA_DV, D_MODEL), (GLA_HEADS * GLA_DV) ** -0.5),
        'w_o': nrm(ks[13], (DEPTH, D_MODEL, D_MODEL), D_MODEL ** -0.5),
        'norm2_g': 1.0 + nrm(ks[14], (DEPTH, D_MODEL), 0.1),
        'w_up': nrm(ks[15], (DEPTH, D_MODEL, 2 * D_FF), D_MODEL ** -0.5),
        'mlp_conv_w': nrm(ks[16], (DEPTH, MLP_CONV_K, 2 * D_FF), MLP_CONV_K ** -0.5),
        'w_down': nrm(ks[17], (DEPTH, D_FF, D_MODEL), D_FF ** -0.5),
        'final_norm_g': 1.0 + nrm(ks[18], (D_MODEL,), 0.1),
    }


def _fwd_reference(x, meta_tokens, norm1_g, w_in, fox_b_f, gate_b, conv_w, gla_w_g2, gla_b_g,
              gla_norm_g, w_a_o, w_b_o, w_c_o, w_o, norm2_g, w_up, mlp_conv_w, w_down,
              final_norm_g):
    f32 = jnp.float32
    bn = x.shape[0]
    meta = jnp.broadcast_to(meta_tokens[None].astype(x.dtype), (bn, N_META, D_MODEL))
    h = jnp.concatenate([meta, x], axis=1)
    for l in range(DEPTH):
        xn = _rmsnorm(h, norm1_g[l])
        proj = xn @ w_in[l]
        (qa, ka, va, fa, sc_b, sc_c, sc_h, qc, kc, vc, rc, glr,
         g_a, g_b, g_c) = jnp.split(proj, SPLIT_POINTS, axis=-1)

        logf = jax.nn.log_sigmoid(fa.astype(f32) + fox_b_f[l].astype(f32))
        logf = jnp.pad(logf.transpose(0, 2, 1), ((0, 0), (0, 0), (PAD, 0)))
        oa = _fox_attention(_to_heads_padded(qa, FOX_HEADS), _to_heads_padded(ka, FOX_HEADS),
                            _to_heads_padded(va, FOX_HEADS), logf)
        ya = _from_heads_padded(oa) @ w_a_o[l]

        yb = (sc_b * _causal_dwconv(sc_c * sc_h, conv_w[l])) @ w_b_o[l]

        logg = jax.nn.log_sigmoid((glr @ gla_w_g2[l]).astype(f32) + gla_b_g[l].astype(f32)) / GLA_TAU
        oc = _gla_chunked(_to_heads_padded(qc * (GLA_DK ** -0.5), GLA_HEADS),
                          _to_heads_padded(kc, GLA_HEADS), _to_heads_padded(vc, GLA_HEADS),
                          _to_heads_padded(logg, GLA_HEADS))
        oc = _from_heads_padded(oc).astype(h.dtype)
        oc = _rmsnorm(oc.reshape(oc.shape[0], oc.shape[1], GLA_HEADS, GLA_DV),
                      gla_norm_g[l].reshape(GLA_HEADS, GLA_DV)).reshape(oc.shape)
        yc = (jax.nn.silu(rc) * oc) @ w_c_o[l]

        gbias = gate_b[l]
        mix = (jax.nn.sigmoid(g_a + gbias[:D_MODEL]) * ya
               + jax.nn.sigmoid(g_b + gbias[D_MODEL:2 * D_MODEL]) * yb
               + jax.nn.sigmoid(g_c + gbias[2 * D_MODEL:]) * yc)
        h = h + mix @ w_o[l]

        u = _causal_dwconv(_rmsnorm(h, norm2_g[l]) @ w_up[l], mlp_conv_w[l])
        u_gate, u_up = jnp.split(u, 2, axis=-1)
        h = h + (jax.nn.silu(u_gate) * u_up) @ w_down[l]
    return _rmsnorm(h, final_norm_g)[:, N_META:]


import jax as _jax
import jax.numpy as _jnp

TWIN_FORMAT = 'train_step'
FWD_PARAMS = ['x', 'meta_tokens', 'norm1_g', 'w_in', 'fox_b_f', 'gate_b', 'conv_w', 'gla_w_g2', 'gla_b_g', 'gla_norm_g', 'w_a_o', 'w_b_o', 'w_c_o', 'w_o', 'norm2_g', 'w_up', 'mlp_conv_w', 'w_down', 'final_norm_g']
TWIN_WEIGHTS = ['meta_tokens', 'norm1_g', 'w_in', 'fox_b_f', 'gate_b', 'conv_w', 'gla_w_g2', 'gla_b_g', 'gla_norm_g', 'w_a_o', 'w_b_o', 'w_c_o', 'w_o', 'norm2_g', 'w_up', 'mlp_conv_w', 'w_down', 'final_norm_g']
TWIN_DIFF_INPUT = 'x'
TWIN_INPUTS = ['x', 'meta_tokens', 'norm1_g', 'w_in', 'fox_b_f', 'gate_b', 'conv_w', 'gla_w_g2', 'gla_b_g', 'gla_norm_g', 'w_a_o', 'w_b_o', 'w_c_o', 'w_o', 'norm2_g', 'w_up', 'mlp_conv_w', 'w_down', 'final_norm_g', 'loss_target', 'm_meta_tokens', 'm_norm1_g', 'm_w_in', 'm_fox_b_f', 'm_gate_b', 'm_conv_w', 'm_gla_w_g2', 'm_gla_b_g', 'm_gla_norm_g', 'm_w_a_o', 'm_w_b_o', 'm_w_c_o', 'm_w_o', 'm_norm2_g', 'm_w_up', 'm_mlp_conv_w', 'm_w_down', 'm_final_norm_g', 'v_meta_tokens', 'v_norm1_g', 'v_w_in', 'v_fox_b_f', 'v_gate_b', 'v_conv_w', 'v_gla_w_g2', 'v_gla_b_g', 'v_gla_norm_g', 'v_w_a_o', 'v_w_b_o', 'v_w_c_o', 'v_w_o', 'v_norm2_g', 'v_w_up', 'v_mlp_conv_w', 'v_w_down', 'v_final_norm_g']
TWIN_OUTPUTS = ['loss', 'grad_x', 'grad_meta_tokens', 'grad_norm1_g', 'grad_w_in', 'grad_fox_b_f', 'grad_gate_b', 'grad_conv_w', 'grad_gla_w_g2', 'grad_gla_b_g', 'grad_gla_norm_g', 'grad_w_a_o', 'grad_w_b_o', 'grad_w_c_o', 'grad_w_o', 'grad_norm2_g', 'grad_w_up', 'grad_mlp_conv_w', 'grad_w_down', 'grad_final_norm_g', 'delta_meta_tokens', 'delta_norm1_g', 'delta_w_in', 'delta_fox_b_f', 'delta_gate_b', 'delta_conv_w', 'delta_gla_w_g2', 'delta_gla_b_g', 'delta_gla_norm_g', 'delta_w_a_o', 'delta_w_b_o', 'delta_w_c_o', 'delta_w_o', 'delta_norm2_g', 'delta_w_up', 'delta_mlp_conv_w', 'delta_w_down', 'delta_final_norm_g', 'new_m_meta_tokens', 'new_m_norm1_g', 'new_m_w_in', 'new_m_fox_b_f', 'new_m_gate_b', 'new_m_conv_w', 'new_m_gla_w_g2', 'new_m_gla_b_g', 'new_m_gla_norm_g', 'new_m_w_a_o', 'new_m_w_b_o', 'new_m_w_c_o', 'new_m_w_o', 'new_m_norm2_g', 'new_m_w_up', 'new_m_mlp_conv_w', 'new_m_w_down', 'new_m_final_norm_g', 'new_v_meta_tokens', 'new_v_norm1_g', 'new_v_w_in', 'new_v_fox_b_f', 'new_v_gate_b', 'new_v_conv_w', 'new_v_gla_w_g2', 'new_v_gla_b_g', 'new_v_gla_norm_g', 'new_v_w_a_o', 'new_v_w_b_o', 'new_v_w_c_o', 'new_v_w_o', 'new_v_norm2_g', 'new_v_w_up', 'new_v_mlp_conv_w', 'new_v_w_down', 'new_v_final_norm_g']
TWIN_LEAF_KINDS = {'loss': 'loss', 'grad_x': 'grad_x', 'grad_meta_tokens': 'grad_w', 'grad_norm1_g': 'grad_w', 'grad_w_in': 'grad_w', 'grad_fox_b_f': 'grad_w', 'grad_gate_b': 'grad_w', 'grad_conv_w': 'grad_w', 'grad_gla_w_g2': 'grad_w', 'grad_gla_b_g': 'grad_w', 'grad_gla_norm_g': 'grad_w', 'grad_w_a_o': 'grad_w', 'grad_w_b_o': 'grad_w', 'grad_w_c_o': 'grad_w', 'grad_w_o': 'grad_w', 'grad_norm2_g': 'grad_w', 'grad_w_up': 'grad_w', 'grad_mlp_conv_w': 'grad_w', 'grad_w_down': 'grad_w', 'grad_final_norm_g': 'grad_w', 'delta_meta_tokens': 'delta_w', 'delta_norm1_g': 'delta_w', 'delta_w_in': 'delta_w', 'delta_fox_b_f': 'delta_w', 'delta_gate_b': 'delta_w', 'delta_conv_w': 'delta_w', 'delta_gla_w_g2': 'delta_w', 'delta_gla_b_g': 'delta_w', 'delta_gla_norm_g': 'delta_w', 'delta_w_a_o': 'delta_w', 'delta_w_b_o': 'delta_w', 'delta_w_c_o': 'delta_w', 'delta_w_o': 'delta_w', 'delta_norm2_g': 'delta_w', 'delta_w_up': 'delta_w', 'delta_mlp_conv_w': 'delta_w', 'delta_w_down': 'delta_w', 'delta_final_norm_g': 'delta_w', 'new_m_meta_tokens': 'new_m', 'new_m_norm1_g': 'new_m', 'new_m_w_in': 'new_m', 'new_m_fox_b_f': 'new_m', 'new_m_gate_b': 'new_m', 'new_m_conv_w': 'new_m', 'new_m_gla_w_g2': 'new_m', 'new_m_gla_b_g': 'new_m', 'new_m_gla_norm_g': 'new_m', 'new_m_w_a_o': 'new_m', 'new_m_w_b_o': 'new_m', 'new_m_w_c_o': 'new_m', 'new_m_w_o': 'new_m', 'new_m_norm2_g': 'new_m', 'new_m_w_up': 'new_m', 'new_m_mlp_conv_w': 'new_m', 'new_m_w_down': 'new_m', 'new_m_final_norm_g': 'new_m', 'new_v_meta_tokens': 'new_v', 'new_v_norm1_g': 'new_v', 'new_v_w_in': 'new_v', 'new_v_fox_b_f': 'new_v', 'new_v_gate_b': 'new_v', 'new_v_conv_w': 'new_v', 'new_v_gla_w_g2': 'new_v', 'new_v_gla_b_g': 'new_v', 'new_v_gla_norm_g': 'new_v', 'new_v_w_a_o': 'new_v', 'new_v_w_b_o': 'new_v', 'new_v_w_c_o': 'new_v', 'new_v_w_o': 'new_v', 'new_v_norm2_g': 'new_v', 'new_v_w_up': 'new_v', 'new_v_mlp_conv_w': 'new_v', 'new_v_w_down': 'new_v', 'new_v_final_norm_g': 'new_v'}


def _forward(args):
    return _fwd_reference(*[args[k] for k in FWD_PARAMS])


def _output_shape():
    def fwd():
        inp = _fwd_setup_inputs(0)
        return _fwd_reference(*[inp[k] for k in FWD_PARAMS])
    out = _jax.eval_shape(fwd)
    return out.shape, out.dtype

N_MICROBATCH = 1
ADAM_LR = 0.001
ADAM_B1 = 0.9
ADAM_B2 = 0.999
ADAM_EPS = 1e-08
ADAM_WD = 0.01
ADAM_STEP = 10
PER_EXAMPLE_BATCH_AXIS = {'x': 0, 'loss_target': 0}
SHARED_INPUTS = []
_WEIGHT_DTYPES = {'meta_tokens': _jnp.float32, 'norm1_g': _jnp.float32, 'w_in': _jnp.float32, 'fox_b_f': _jnp.float32, 'gate_b': _jnp.float32, 'conv_w': _jnp.float32, 'gla_w_g2': _jnp.float32, 'gla_b_g': _jnp.float32, 'gla_norm_g': _jnp.float32, 'w_a_o': _jnp.float32, 'w_b_o': _jnp.float32, 'w_c_o': _jnp.float32, 'w_o': _jnp.float32, 'norm2_g': _jnp.float32, 'w_up': _jnp.float32, 'mlp_conv_w': _jnp.float32, 'w_down': _jnp.float32, 'final_norm_g': _jnp.float32}
MOMENT_SCALE = {'meta_tokens': 6.325985e-03, 'norm1_g': 1.607173e-01, 'w_in': 5.674608e-02, 'fox_b_f': 3.227056e-01, 'gate_b': 1.878786e-02, 'conv_w': 1.007050e-01, 'gla_w_g2': 9.766707e-03, 'gla_b_g': 3.990663e-02, 'gla_norm_g': 5.990782e-02, 'w_a_o': 1.773684e-02, 'w_b_o': 7.194565e-02, 'w_c_o': 4.202109e-02, 'w_o': 8.563714e-02, 'norm2_g': 9.862212e-02, 'w_up': 3.968002e-02, 'mlp_conv_w': 4.012490e-02, 'w_down': 6.589738e-02, 'final_norm_g': 3.220794e+01}


def _to_microbatches(a, axis):
    t = _jnp.moveaxis(a, axis, 0)
    t = t.reshape((N_MICROBATCH, t.shape[0] // N_MICROBATCH) + t.shape[1:])
    return _jnp.moveaxis(t, 1, axis + 1)


def setup_inputs(seed: int = 0) -> dict:
    inp = _fwd_setup_inputs(seed)
    key = _jax.random.fold_in(_jax.random.key(seed), 7919)
    shape, _ = _output_shape()
    out = dict(inp)
    out["loss_target"] = _jax.random.normal(_jax.random.fold_in(key, 0), shape, _jnp.float32)
    for i, name in enumerate(TWIN_WEIGHTS):
        w = inp[name].astype(_jnp.float32)
        if MOMENT_SCALE is None:
            s = _jnp.sqrt(_jnp.mean(_jnp.square(w)) + 1e-30)
        else:
            s = MOMENT_SCALE[name]
        km, kv = _jax.random.split(_jax.random.fold_in(key, i + 1))
        out[name] = w
        out["m_" + name] = s * _jax.random.normal(km, w.shape, _jnp.float32)
        out["v_" + name] = (s * s) * _jax.random.uniform(kv, w.shape, _jnp.float32, 0.5, 1.5)
    if N_MICROBATCH > 1:
        for name, axis in PER_EXAMPLE_BATCH_AXIS.items():
            out[name] = _to_microbatches(out[name], axis)
    return {'x': out['x'], 'meta_tokens': out['meta_tokens'], 'norm1_g': out['norm1_g'], 'w_in': out['w_in'], 'fox_b_f': out['fox_b_f'], 'gate_b': out['gate_b'], 'conv_w': out['conv_w'], 'gla_w_g2': out['gla_w_g2'], 'gla_b_g': out['gla_b_g'], 'gla_norm_g': out['gla_norm_g'], 'w_a_o': out['w_a_o'], 'w_b_o': out['w_b_o'], 'w_c_o': out['w_c_o'], 'w_o': out['w_o'], 'norm2_g': out['norm2_g'], 'w_up': out['w_up'], 'mlp_conv_w': out['mlp_conv_w'], 'w_down': out['w_down'], 'final_norm_g': out['final_norm_g'], 'loss_target': out['loss_target'], 'm_meta_tokens': out['m_meta_tokens'], 'm_norm1_g': out['m_norm1_g'], 'm_w_in': out['m_w_in'], 'm_fox_b_f': out['m_fox_b_f'], 'm_gate_b': out['m_gate_b'], 'm_conv_w': out['m_conv_w'], 'm_gla_w_g2': out['m_gla_w_g2'], 'm_gla_b_g': out['m_gla_b_g'], 'm_gla_norm_g': out['m_gla_norm_g'], 'm_w_a_o': out['m_w_a_o'], 'm_w_b_o': out['m_w_b_o'], 'm_w_c_o': out['m_w_c_o'], 'm_w_o': out['m_w_o'], 'm_norm2_g': out['m_norm2_g'], 'm_w_up': out['m_w_up'], 'm_mlp_conv_w': out['m_mlp_conv_w'], 'm_w_down': out['m_w_down'], 'm_final_norm_g': out['m_final_norm_g'], 'v_meta_tokens': out['v_meta_tokens'], 'v_norm1_g': out['v_norm1_g'], 'v_w_in': out['v_w_in'], 'v_fox_b_f': out['v_fox_b_f'], 'v_gate_b': out['v_gate_b'], 'v_conv_w': out['v_conv_w'], 'v_gla_w_g2': out['v_gla_w_g2'], 'v_gla_b_g': out['v_gla_b_g'], 'v_gla_norm_g': out['v_gla_norm_g'], 'v_w_a_o': out['v_w_a_o'], 'v_w_b_o': out['v_w_b_o'], 'v_w_c_o': out['v_w_c_o'], 'v_w_o': out['v_w_o'], 'v_norm2_g': out['v_norm2_g'], 'v_w_up': out['v_w_up'], 'v_mlp_conv_w': out['v_mlp_conv_w'], 'v_w_down': out['v_w_down'], 'v_final_norm_g': out['v_final_norm_g']}


def _loss(weights, diff, rest, loss_target):
    with _jax.named_scope("forward"):
        args = {**rest, TWIN_DIFF_INPUT: diff, **{k: w.astype(_WEIGHT_DTYPES[k]) for k, w in weights.items()}}
        y = _forward(args)
    with _jax.named_scope("loss_head"):
        err = _jnp.square(y.astype(_jnp.float32) - loss_target)
        return 0.5 * _jnp.sum(_jnp.mean(err, axis=-1)) if err.ndim else 0.5 * err


def _adamw(w, g, m, v):
    m = ADAM_B1 * m + (1.0 - ADAM_B1) * g
    v = ADAM_B2 * v + (1.0 - ADAM_B2) * _jnp.square(g)
    m_hat = m / (1.0 - ADAM_B1 ** ADAM_STEP)
    v_hat = v / (1.0 - ADAM_B2 ** ADAM_STEP)
    delta = -ADAM_LR * (m_hat / (_jnp.sqrt(v_hat) + ADAM_EPS) + ADAM_WD * w)
    return delta, m, v


def reference(x, meta_tokens, norm1_g, w_in, fox_b_f, gate_b, conv_w, gla_w_g2, gla_b_g, gla_norm_g, w_a_o, w_b_o, w_c_o, w_o, norm2_g, w_up, mlp_conv_w, w_down, final_norm_g, loss_target, m_meta_tokens, m_norm1_g, m_w_in, m_fox_b_f, m_gate_b, m_conv_w, m_gla_w_g2, m_gla_b_g, m_gla_norm_g, m_w_a_o, m_w_b_o, m_w_c_o, m_w_o, m_norm2_g, m_w_up, m_mlp_conv_w, m_w_down, m_final_norm_g, v_meta_tokens, v_norm1_g, v_w_in, v_fox_b_f, v_gate_b, v_conv_w, v_gla_w_g2, v_gla_b_g, v_gla_norm_g, v_w_a_o, v_w_b_o, v_w_c_o, v_w_o, v_norm2_g, v_w_up, v_mlp_conv_w, v_w_down, v_final_norm_g):
    given = dict(x=x, meta_tokens=meta_tokens, norm1_g=norm1_g, w_in=w_in, fox_b_f=fox_b_f, gate_b=gate_b, conv_w=conv_w, gla_w_g2=gla_w_g2, gla_b_g=gla_b_g, gla_norm_g=gla_norm_g, w_a_o=w_a_o, w_b_o=w_b_o, w_c_o=w_c_o, w_o=w_o, norm2_g=norm2_g, w_up=w_up, mlp_conv_w=mlp_conv_w, w_down=w_down, final_norm_g=final_norm_g, loss_target=loss_target, m_meta_tokens=m_meta_tokens, m_norm1_g=m_norm1_g, m_w_in=m_w_in, m_fox_b_f=m_fox_b_f, m_gate_b=m_gate_b, m_conv_w=m_conv_w, m_gla_w_g2=m_gla_w_g2, m_gla_b_g=m_gla_b_g, m_gla_norm_g=m_gla_norm_g, m_w_a_o=m_w_a_o, m_w_b_o=m_w_b_o, m_w_c_o=m_w_c_o, m_w_o=m_w_o, m_norm2_g=m_norm2_g, m_w_up=m_w_up, m_mlp_conv_w=m_mlp_conv_w, m_w_down=m_w_down, m_final_norm_g=m_final_norm_g, v_meta_tokens=v_meta_tokens, v_norm1_g=v_norm1_g, v_w_in=v_w_in, v_fox_b_f=v_fox_b_f, v_gate_b=v_gate_b, v_conv_w=v_conv_w, v_gla_w_g2=v_gla_w_g2, v_gla_b_g=v_gla_b_g, v_gla_norm_g=v_gla_norm_g, v_w_a_o=v_w_a_o, v_w_b_o=v_w_b_o, v_w_c_o=v_w_c_o, v_w_o=v_w_o, v_norm2_g=v_norm2_g, v_w_up=v_w_up, v_mlp_conv_w=v_mlp_conv_w, v_w_down=v_w_down, v_final_norm_g=v_final_norm_g)
    weights = {n: given[n] for n in TWIN_WEIGHTS}
    shared = {n: given[n] for n in SHARED_INPUTS}
    per_example = {n: given[n] for n in ['x']}
    grad_fn = _jax.value_and_grad(_loss, argnums=(0, 1))

    def one_microbatch(ex, loss_target):
        ex = dict(ex)
        diff = ex.pop(TWIN_DIFF_INPUT)
        return grad_fn(weights, diff, {**shared, **ex}, loss_target)

    if N_MICROBATCH == 1:
        loss, (grad_w, grad_x) = one_microbatch(per_example, given["loss_target"])
    else:
        def body(carry, xs):
            loss_sum, grad_sum = carry
            l_k, (gw_k, gx_k) = one_microbatch(xs[0], xs[1])
            with _jax.named_scope("update"):
                return (loss_sum + l_k, _jax.tree.map(_jnp.add, grad_sum, gw_k)), gx_k

        init = (_jnp.zeros((), _jnp.float32), _jax.tree.map(_jnp.zeros_like, weights))
        (loss, grad_w), grad_x = _jax.lax.scan(body, init, (per_example, given["loss_target"]))
    with _jax.named_scope("update"):
        delta_w, new_m, new_v = {}, {}, {}
        for n in TWIN_WEIGHTS:
            delta_w[n], new_m[n], new_v[n] = _adamw(weights[n], grad_w[n], given["m_" + n], given["v_" + n])
    return (loss, grad_x, *[grad_w[n] for n in TWIN_WEIGHTS], *[delta_w[n] for n in TWIN_WEIGHTS],
            *[new_m[n] for n in TWIN_WEIGHTS], *[new_v[n] for n in TWIN_WEIGHTS])
```

```python
import functools

import jax
import jax.numpy as jnp
from jax import lax
from jax.experimental import pallas as pl
from jax.experimental.pallas import tpu as pltpu

F32 = jnp.float32
BF16 = jnp.bfloat16

D = 2048
DEPTH = 2
N_META = 16
BLK = 128
PAD = BLK - N_META
EPS = 1e-6
NEG = -1e30

FOX_H, FOX_DH = 8, 128
FOX_W = FOX_H * FOX_DH
CONV_CH = 1024
GLA_H, GLA_DK, GLA_DV, GLA_R, GLA_TAU = 4, 128, 256, 16, 16.0
D_FF = 5632
N_IN = 15384
N_DEV = 8

ADAM_LR, ADAM_B1, ADAM_B2, ADAM_EPS, ADAM_WD, ADAM_STEP = 0.001, 0.9, 0.999, 1e-08, 0.01, 10

CONV_TC = 512
GATE_TN = 512
KV0 = 1024
REST0 = 3072
GLA_GRP = 768
SMALL_W = 512
NP = 15872
NREST = NP - REST0
R_CONV_BLK0 = 0
R_GLA_BLK0 = (6144 - REST0) // GLA_GRP
R_GATE_BLK0 = (9216 - REST0) // (3 * GATE_TN)
R_SMALL_BLK128 = (15360 - REST0) // 128
F_CONV_BLK0 = 3072 // (3 * CONV_TC)
F_GLA_BLK0 = 6144 // GLA_GRP
F_GATE_BLK0 = 9216 // (3 * GATE_TN)
F_SMALL_BLK0 = 15360 // SMALL_W

VMEM_LIMIT = 56 * 1024 * 1024
ADAMW_BLOCK_ELEMS = 128 * 1024


def _segments():
    seg = [(0, 1024)]
    for h in range(FOX_H):
        seg += [(1024 + 128 * h, 128), (2048 + 128 * h, 128)]
    for j in range(CONV_CH // CONV_TC):
        seg += [(3080 + CONV_TC * j, CONV_TC), (4104 + CONV_TC * j, CONV_TC), (5128 + CONV_TC * j, CONV_TC)]
    for h in range(GLA_H):
        seg += [(6152 + 128 * h, 128), (6664 + 128 * h, 128), (7176 + 256 * h, 256), (8200 + 256 * h, 256)]
    for j in range(D // GATE_TN):
        seg += [(9240 + GATE_TN * j, GATE_TN), (11288 + GATE_TN * j, GATE_TN), (13336 + GATE_TN * j, GATE_TN)]
    seg += [(3072, 8), (9224, 16)]
    return seg


def _cparams(*sem):
    return pltpu.CompilerParams(dimension_semantics=sem, vmem_limit_bytes=VMEM_LIMIT)


def _row_tile(n, target):
    best = BLK
    t = BLK
    while t <= min(n, target):
        if n % t == 0:
            best = t
        t += BLK
    return best


def _sigmoid(x):
    return 1.0 / (1.0 + jnp.exp(-x))


def _log_sigmoid(x):
    return jnp.minimum(x, 0.0) - jnp.log(1.0 + jnp.exp(-jnp.abs(x)))


def _valid_rows(row0, n):
    return (row0 + lax.broadcasted_iota(jnp.int32, (n, 1), 0)) >= PAD


def _dot(a, b):
    return jnp.dot(a, b, preferred_element_type=F32)


def _dot_nt(a, b):
    return lax.dot_general(a, b, (((1,), (1,)), ((), ())), preferred_element_type=F32)


def _dot_tn(a, b):
    return lax.dot_general(a, b, (((0,), (0,)), ((), ())), preferred_element_type=F32)


def _mm_nn(a, b, *, n0=0, n=None, out_dtype=F32, res=None, tm=640, tn=512, tk=None, name="mm_nn"):
    m, k = a.shape
    n = b.shape[1] - n0 if n is None else n
    tm = _row_tile(m, tm)
    tk = k if tk is None else tk
    nk = k // tk
    assert k % tk == 0 and n % tn == 0 and n0 % tn == 0
    nb0 = n0 // tn

    def kern(*refs):
        if res is None:
            a_ref, b_ref, o_ref, acc = refs
        else:
            a_ref, b_ref, r_ref, o_ref, acc = refs
        kk = pl.program_id(2)
        row0 = pl.program_id(0) * tm

        @pl.when(kk == 0)
        def _():
            acc[...] = jnp.zeros_like(acc)

        acc[...] += _dot(a_ref[...].astype(BF16), b_ref[...].astype(BF16))

        @pl.when(kk == nk - 1)
        def _():
            if res is None:
                o_ref[...] = acc[...].astype(out_dtype)
            else:
                valid = _valid_rows(row0, tm)
                o_ref[...] = (r_ref[...] + jnp.where(valid, acc[...], 0.0)).astype(out_dtype)

    in_specs = [pl.BlockSpec((tm, tk), lambda i, j, kk: (i, kk)),
                pl.BlockSpec((tk, tn), lambda i, j, kk: (kk, nb0 + j))]
    args = [a, b]
    if res is not None:
        in_specs.append(pl.BlockSpec((tm, tn), lambda i, j, kk: (i, j)))
        args.append(res)
    return pl.pallas_call(
        kern, grid=(m // tm, n // tn, nk), in_specs=in_specs,
        out_specs=pl.BlockSpec((tm, tn), lambda i, j, kk: (i, j)),
        out_shape=jax.ShapeDtypeStruct((m, n), out_dtype),
        scratch_shapes=[pltpu.VMEM((tm, tn), F32)],
        compiler_params=_cparams("parallel", "parallel", "arbitrary"), name=name)(*args)


def _mm_nt(a, b, *, k0=0, kw=None, out_dtype=F32, add=None, tm=640, tn=None, tk=512, name="mm_nt"):
    m = a.shape[0]
    kw = a.shape[1] if kw is None else kw
    nn = b.shape[0]
    tm = _row_tile(m, tm)
    tn = nn if tn is None else tn
    assert kw % tk == 0 and k0 % tk == 0 and nn % tn == 0 and a.shape[1] == kw
    nk = kw // tk
    kb0 = k0 // tk

    def kern(*refs):
        if add is None:
            a_ref, b_ref, o_ref, acc = refs
        else:
            a_ref, b_ref, d_ref, o_ref, acc = refs
        kk = pl.program_id(2)

        @pl.when(kk == 0)
        def _():
            acc[...] = jnp.zeros_like(acc)

        acc[...] += _dot_nt(a_ref[...].astype(BF16), b_ref[...].astype(BF16))

        @pl.when(kk == nk - 1)
        def _():
            if add is None:
                o_ref[...] = acc[...].astype(out_dtype)
            else:
                o_ref[...] = (acc[...] + d_ref[...]).astype(out_dtype)

    in_specs = [pl.BlockSpec((tm, tk), lambda i, j, kk: (i, kk)),
                pl.BlockSpec((tn, tk), lambda i, j, kk: (j, kb0 + kk))]
    args = [a, b]
    if add is not None:
        in_specs.append(pl.BlockSpec((tm, tn), lambda i, j, kk: (i, j)))
        args.append(add)
    return pl.pallas_call(
        kern, grid=(m // tm, nn // tn, nk), in_specs=in_specs,
        out_specs=pl.BlockSpec((tm, tn), lambda i, j, kk: (i, j)),
        out_shape=jax.ShapeDtypeStruct((m, nn), out_dtype),
        scratch_shapes=[pltpu.VMEM((tm, tn), F32)],
        compiler_params=_cparams("parallel", "parallel", "arbitrary"), name=name)(*args)


def _mm_tn(a, b, *, out_dtype=BF16, tm=640, tk=None, tn=512, name="mm_tn"):
    m, k = a.shape
    n = b.shape[1]
    tm = _row_tile(m, tm)
    tk = k if tk is None else tk
    assert k % tk == 0 and n % tn == 0
    nm = m // tm

    def kern(a_ref, b_ref, o_ref, acc):
        mm = pl.program_id(2)

        @pl.when(mm == 0)
        def _():
            acc[...] = jnp.zeros_like(acc)

        acc[...] += _dot_tn(a_ref[...].astype(BF16), b_ref[...].astype(BF16))

        @pl.when(mm == nm - 1)
        def _():
            o_ref[...] = acc[...].astype(out_dtype)

    return pl.pallas_call(
        kern, grid=(k // tk, n // tn, nm),
        in_specs=[pl.BlockSpec((tm, tk), lambda i, j, mm: (mm, i)),
                  pl.BlockSpec((tm, tn), lambda i, j, mm: (mm, j))],
        out_specs=pl.BlockSpec((tk, tn), lambda i, j, mm: (i, j)),
        out_shape=jax.ShapeDtypeStruct((k, n), out_dtype),
        scratch_shapes=[pltpu.VMEM((tk, tn), F32)],
        compiler_params=_cparams("parallel", "parallel", "arbitrary"), name=name)(a, b)


def _rmsnorm_fwd(h, g, tr=640):
    lp = h.shape[0]
    tr = _row_tile(lp, tr)

    def kern(h_ref, g_ref, o_ref):
        x = h_ref[...]
        r = lax.rsqrt(jnp.mean(x * x, axis=-1, keepdims=True) + EPS)
        o_ref[...] = (x * r * g_ref[...]).astype(BF16)

    return pl.pallas_call(
        kern, grid=(lp // tr,),
        in_specs=[pl.BlockSpec((tr, D), lambda i: (i, 0)), pl.BlockSpec((1, D), lambda i: (0, 0))],
        out_specs=pl.BlockSpec((tr, D), lambda i: (i, 0)),
        out_shape=jax.ShapeDtypeStruct((lp, D), BF16),
        compiler_params=_cparams("parallel"), name="rmsnorm_fwd")(h, g)


def _rmsnorm_bwd(h, g, dxn, dres, tr=640):
    lp = h.shape[0]
    tr = _row_tile(lp, tr)

    def kern(h_ref, g_ref, dxn_ref, dres_ref, dh_ref, dg_ref):
        i = pl.program_id(0)
        x = h_ref[...]
        r = lax.rsqrt(jnp.mean(x * x, axis=-1, keepdims=True) + EPS)
        xhat = x * r
        dy = jnp.where(_valid_rows(i * tr, tr), dxn_ref[...], 0.0)

        @pl.when(i == 0)
        def _():
            dg_ref[...] = jnp.zeros_like(dg_ref)

        dg_ref[...] += jnp.sum(dy * xhat, axis=0, keepdims=True)
        dxh = dy * g_ref[...]
        dh_ref[...] = dres_ref[...] + r * (dxh - xhat * jnp.mean(dxh * xhat, axis=-1, keepdims=True))

    row = pl.BlockSpec((tr, D), lambda i: (i, 0))
    vec = pl.BlockSpec((1, D), lambda i: (0, 0))
    return pl.pallas_call(
        kern, grid=(lp // tr,), in_specs=[row, vec, row, row], out_specs=[row, vec],
        out_shape=[jax.ShapeDtypeStruct((lp, D), F32), jax.ShapeDtypeStruct((1, D), F32)],
        compiler_params=_cparams("arbitrary"), name="rmsnorm_bwd")(h, g, dxn, dres)


def _shift_down(xe, k):
    return xe if k == 0 else pltpu.roll(xe, k, 0)


def _shift_up(xe, k):
    return xe if k == 0 else pltpu.roll(xe, xe.shape[0] - k, 0)


def _conv_ext(xe, w_ref):
    return w_ref[2:3, :] * xe + w_ref[1:2, :] * _shift_down(xe, 1) + w_ref[0:1, :] * _shift_down(xe, 2)


def _halo_specs(tr, width, col_of, nrows, rows_first):
    r8 = tr // 8
    last8 = nrows // 8 - 1
    if rows_first:
        prev = pl.BlockSpec((8, width), lambda i, j: (jnp.maximum(i * r8 - 1, 0), col_of(j)))
        nxt = pl.BlockSpec((8, width), lambda i, j: (jnp.minimum((i + 1) * r8, last8), col_of(j)))
    else:
        prev = pl.BlockSpec((8, width), lambda j, i: (jnp.maximum(i * r8 - 1, 0), col_of(j)))
        nxt = pl.BlockSpec((8, width), lambda j, i: (jnp.minimum((i + 1) * r8, last8), col_of(j)))
    return prev, nxt


def _convb_fwd(proj_r, conv_w, tr=640):
    lp = proj_r.shape[0]
    tr = _row_tile(lp, tr)
    tc = CONV_TC
    gw = 3 * tc

    def kern(g_ref, gp_ref, w_ref, o_ref):
        i = pl.program_id(0)
        g = g_ref[...]
        p = g[:, tc:2 * tc] * g[:, 2 * tc:]
        gp = gp_ref[...]
        pp = jnp.where(i > 0, gp[:, tc:2 * tc] * gp[:, 2 * tc:], 0.0)
        y = _conv_ext(jnp.concatenate([pp, p], axis=0), w_ref)[8:]
        o_ref[...] = (g[:, :tc] * y).astype(BF16)

    prev, _ = _halo_specs(tr, gw, lambda j: R_CONV_BLK0 + j, lp, True)
    return pl.pallas_call(
        kern, grid=(lp // tr, CONV_CH // tc),
        in_specs=[pl.BlockSpec((tr, gw), lambda i, j: (i, R_CONV_BLK0 + j)), prev,
                  pl.BlockSpec((3, tc), lambda i, j: (0, j))],
        out_specs=pl.BlockSpec((tr, tc), lambda i, j: (i, j)),
        out_shape=jax.ShapeDtypeStruct((lp, CONV_CH), BF16),
        compiler_params=_cparams("parallel", "parallel"), name="convb_fwd")(proj_r, proj_r, conv_w)


def _convb_bwd(proj_r, dzb, conv_w, dproj, tr=640):
    lp = proj_r.shape[0]
    tr = _row_tile(lp, tr)
    nr = lp // tr
    tc = CONV_TC
    gw = 3 * tc

    def kern(g_ref, gp_ref, gn_ref, dz_ref, dzn_ref, w_ref, dp_any, dg_ref, dw_ref):
        del dp_any
        i = pl.program_id(1)
        g = g_ref[...]
        b, c, hh = g[:, :tc], g[:, tc:2 * tc], g[:, 2 * tc:]
        p = c * hh
        gp = gp_ref[...]
        pp = jnp.where(i > 0, gp[:, tc:2 * tc] * gp[:, 2 * tc:], 0.0)
        pe = jnp.concatenate([pp, p], axis=0)
        s1 = _shift_down(pe, 1)[8:]
        s2 = _shift_down(pe, 2)[8:]
        y = w_ref[2:3, :] * p + w_ref[1:2, :] * s1 + w_ref[0:1, :] * s2
        dz = dz_ref[...]
        dy = dz * b
        dyn = jnp.where(i < nr - 1, dzn_ref[...] * gn_ref[...][:, :tc], 0.0)
        dye = jnp.concatenate([dy, dyn], axis=0)
        dp = (w_ref[2:3, :] * dy + w_ref[1:2, :] * _shift_up(dye, 1)[:tr]
              + w_ref[0:1, :] * _shift_up(dye, 2)[:tr])
        valid = _valid_rows(i * tr, tr)
        dg_ref[...] = jnp.where(valid, jnp.concatenate([dz * y, dp * hh, dp * c], axis=1), 0.0).astype(BF16)

        @pl.when(i == 0)
        def _():
            dw_ref[...] = jnp.zeros_like(dw_ref)

        dw_ref[0:1, :] += jnp.sum(dy * s2, axis=0, keepdims=True)
        dw_ref[1:2, :] += jnp.sum(dy * s1, axis=0, keepdims=True)
        dw_ref[2:3, :] += jnp.sum(dy * p, axis=0, keepdims=True)

    gprev, gnext = _halo_specs(tr, gw, lambda j: R_CONV_BLK0 + j, lp, False)
    _, dznext = _halo_specs(tr, tc, lambda j: j, lp, False)
    return pl.pallas_call(
        kern, grid=(CONV_CH // tc, nr),
        in_specs=[pl.BlockSpec((tr, gw), lambda j, i: (i, R_CONV_BLK0 + j)), gprev, gnext,
                  pl.BlockSpec((tr, tc), lambda j, i: (i, j)), dznext,
                  pl.BlockSpec((3, tc), lambda j, i: (0, j)),
                  pl.BlockSpec(memory_space=pl.ANY)],
        out_specs=[pl.BlockSpec((tr, gw), lambda j, i: (i, F_CONV_BLK0 + j)),
                   pl.BlockSpec((3, tc), lambda j, i: (0, j))],
        out_shape=[jax.ShapeDtypeStruct(dproj.shape, BF16), jax.ShapeDtypeStruct((3, CONV_CH), F32)],
        input_output_aliases={6: 0},
        compiler_params=_cparams("parallel", "arbitrary"), name="convb_bwd",
    )(proj_r, proj_r, proj_r, dzb, dzb, conv_w, dproj)


MLP_TC = 256


def _mlp_gate_fwd(z, w, tr=640):
    lp = z.shape[0]
    tr = _row_tile(lp, tr)
    tc = 512
    nc = D_FF // tc

    def kern(zg_ref, zgp_ref, zu_ref, zup_ref, wg_ref, wu_ref, o_ref):
        i = pl.program_id(0)
        zge = jnp.concatenate([jnp.where(i > 0, zgp_ref[...], 0.0), zg_ref[...]], axis=0)
        zue = jnp.concatenate([jnp.where(i > 0, zup_ref[...], 0.0), zu_ref[...]], axis=0)
        ug = _conv_ext(zge, wg_ref)[8:]
        uu = _conv_ext(zue, wu_ref)[8:]
        o_ref[...] = (ug * _sigmoid(ug) * uu).astype(BF16)

    gprev, _ = _halo_specs(tr, tc, lambda j: j, lp, True)
    uprev, _ = _halo_specs(tr, tc, lambda j: nc + j, lp, True)
    return pl.pallas_call(
        kern, grid=(lp // tr, nc),
        in_specs=[pl.BlockSpec((tr, tc), lambda i, j: (i, j)), gprev,
                  pl.BlockSpec((tr, tc), lambda i, j: (i, nc + j)), uprev,
                  pl.BlockSpec((3, tc), lambda i, j: (0, j)),
                  pl.BlockSpec((3, tc), lambda i, j: (0, nc + j))],
        out_specs=pl.BlockSpec((tr, tc), lambda i, j: (i, j)),
        out_shape=jax.ShapeDtypeStruct((lp, D_FF), BF16),
        compiler_params=_cparams("parallel", "parallel"), name="mlp_gate_fwd")(z, z, z, z, w, w)


def _mlp_gate_bwd(z, da, w, tr=640):
    lp = z.shape[0]
    tr = _row_tile(lp, tr)
    nr = lp // tr
    tc = MLP_TC
    nc = D_FF // tc

    def kern(zg_ref, zgp_ref, zgn_ref, zu_ref, zup_ref, zun_ref, da_ref, dan_ref, wg_ref, wu_ref,
             dzg_ref, dzu_ref, dwg_ref, dwu_ref):
        i = pl.program_id(1)
        first, last = i == 0, i == nr - 1

        def ext(m_ref, p_ref, n_ref):
            return jnp.concatenate([jnp.where(first, 0.0, p_ref[...]), m_ref[...],
                                    jnp.where(last, 0.0, n_ref[...])], axis=0)

        zge, zue = ext(zg_ref, zgp_ref, zgn_ref), ext(zu_ref, zup_ref, zun_ref)
        ug = _conv_ext(zge, wg_ref)[8:]
        uu = _conv_ext(zue, wu_ref)[8:]
        dae = jnp.concatenate([da_ref[...], jnp.where(last, 0.0, dan_ref[...])], axis=0)
        sg = _sigmoid(ug)
        dug = dae * uu * (sg * (1.0 + ug * (1.0 - sg)))
        duu = dae * (ug * sg)
        valid = _valid_rows(i * tr, tr)

        @pl.when(first)
        def _():
            dwg_ref[...] = jnp.zeros_like(dwg_ref)
            dwu_ref[...] = jnp.zeros_like(dwu_ref)

        for du, ze, w_ref, dz_ref, dw_ref in ((dug, zge, wg_ref, dzg_ref, dwg_ref),
                                              (duu, zue, wu_ref, dzu_ref, dwu_ref)):
            dz = (w_ref[2:3, :] * du + w_ref[1:2, :] * _shift_up(du, 1) + w_ref[0:1, :] * _shift_up(du, 2))[:tr]
            dz_ref[...] = jnp.where(valid, dz, 0.0).astype(BF16)
            dum = du[:tr]
            for kk in range(3):
                dw_ref[kk:kk + 1, :] += jnp.sum(dum * _shift_down(ze, 2 - kk)[8:8 + tr], axis=0, keepdims=True)

    gprev, gnext = _halo_specs(tr, tc, lambda j: j, lp, False)
    uprev, unext = _halo_specs(tr, tc, lambda j: nc + j, lp, False)
    main = pl.BlockSpec((tr, tc), lambda j, i: (i, j))
    wspec = pl.BlockSpec((3, tc), lambda j, i: (0, j))
    return pl.pallas_call(
        kern, grid=(nc, nr),
        in_specs=[main, gprev, gnext, pl.BlockSpec((tr, tc), lambda j, i: (i, nc + j)), uprev, unext,
                  main, gnext, wspec, pl.BlockSpec((3, tc), lambda j, i: (0, nc + j))],
        out_specs=[main, main, wspec, wspec],
        out_shape=[jax.ShapeDtypeStruct((lp, D_FF), BF16), jax.ShapeDtypeStruct((lp, D_FF), BF16),
                   jax.ShapeDtypeStruct((3, D_FF), F32), jax.ShapeDtypeStruct((3, D_FF), F32)],
        compiler_params=_cparams("parallel", "arbitrary"), name="mlp_gate_bwd",
    )(z, z, z, z, z, z, da, da, w, w)


def _tri(n, lower):
    r = lax.broadcasted_iota(jnp.int32, (n, n), 0)
    c = lax.broadcasted_iota(jnp.int32, (n, n), 1)
    return jnp.where((c <= r) if lower else (c >= r), 1.0, 0.0).astype(F32)


def _dot_exact(a, b):
    return jnp.dot(a, b, preferred_element_type=F32, precision=lax.Precision.HIGHEST)


def _fox_gate_fwd(proj_r, bf128):
    lp = proj_r.shape[0]
    nb = lp // BLK

    def kern(s_ref, b_ref, c_ref):
        tri = _tri(BLK, True)

        def body(i, carry):
            rows = pl.ds(pl.multiple_of(i * BLK, BLK), BLK)
            lf = jnp.where(_valid_rows(i * BLK, BLK), _log_sigmoid(s_ref[rows, :] + b_ref[...]), 0.0)
            cs = _dot_exact(tri, lf) + carry
            c_ref[rows, :] = cs
            return cs[BLK - 1:BLK, :]

        lax.fori_loop(0, nb, body, jnp.zeros((1, BLK), F32))

    return pl.pallas_call(
        kern, grid=(1,),
        in_specs=[pl.BlockSpec((lp, BLK), lambda i: (0, R_SMALL_BLK128)), pl.BlockSpec((1, BLK), lambda i: (0, 0))],
        out_specs=pl.BlockSpec((lp, BLK), lambda i: (0, 0)),
        out_shape=jax.ShapeDtypeStruct((lp, BLK), F32),
        compiler_params=_cparams("arbitrary"), name="fox_gate_fwd")(proj_r, bf128)


def _fox_gate_bwd(proj_r, dc, bf128):
    lp = proj_r.shape[0]
    nb = lp // BLK

    def kern(s_ref, dc_ref, b_ref, dfa_ref, dbf_ref):
        tri = _tri(BLK, False)

        dbf_ref[...] = jnp.zeros_like(dbf_ref)

        def body(ii, run):
            i = nb - 1 - ii
            rows = pl.ds(pl.multiple_of(i * BLK, BLK), BLK)
            dcb = dc_ref[rows, :]
            suf = _dot_exact(tri, dcb) + run
            dfa = jnp.where(_valid_rows(i * BLK, BLK), suf * _sigmoid(-(s_ref[rows, :] + b_ref[...])), 0.0)
            dfa_ref[rows, :] = dfa
            dbf_ref[...] += jnp.sum(dfa, axis=0, keepdims=True)
            return run + jnp.sum(dcb, axis=0, keepdims=True)

        lax.fori_loop(0, nb, body, jnp.zeros((1, BLK), F32))

    return pl.pallas_call(
        kern, grid=(1,),
        in_specs=[pl.BlockSpec((lp, BLK), lambda i: (0, R_SMALL_BLK128)), pl.BlockSpec((lp, BLK), lambda i: (0, 0)),
                  pl.BlockSpec((1, BLK), lambda i: (0, 0))],
        out_specs=[pl.BlockSpec((lp, BLK), lambda i: (0, 0)), pl.BlockSpec((1, BLK), lambda i: (0, 0))],
        out_shape=[jax.ShapeDtypeStruct((lp, BLK), F32), jax.ShapeDtypeStruct((1, BLK), F32)],
        compiler_params=_cparams("arbitrary"), name="fox_gate_bwd")(proj_r, dc, bf128)


def _attn_scores(q, k, cq, ck, row0, col0, tq, tk):
    s = _dot_nt(q, k) * (FOX_DH ** -0.5) + (cq - ck)
    qpos = row0 + lax.broadcasted_iota(jnp.int32, (tq, tk), 0)
    kpos = col0 + lax.broadcasted_iota(jnp.int32, (tq, tk), 1)
    keep = jnp.where(kpos >= PAD, kpos, jnp.int32(2 ** 30)) <= qpos
    return jnp.where(keep, s, NEG), keep


def _attn_fwd(proj_a, cq, ck, tq=640):
    lp = proj_a.shape[0]
    tq = _row_tile(lp, tq)
    tk = tq
    nq = lp // tq

    def kern(q_ref, kv_ref, cq_ref, ck_ref, o_ref, lse_ref, m_sc, l_sc, acc):
        i, j = pl.program_id(1), pl.program_id(2)

        @pl.when(j == 0)
        def _():
            m_sc[...] = jnp.full_like(m_sc, -jnp.inf)
            l_sc[...] = jnp.zeros_like(l_sc)
            acc[...] = jnp.zeros_like(acc)

        @pl.when(j <= i)
        def _():
            kv = kv_ref[...]
            s, _ = _attn_scores(q_ref[...], kv[:, :FOX_DH], cq_ref[...], ck_ref[...], i * tq, j * tk, tq, tk)
            m_new = jnp.maximum(m_sc[...], jnp.max(s, axis=-1, keepdims=True))
            alpha = jnp.exp(m_sc[...] - m_new)
            p = jnp.exp(s - m_new)
            l_sc[...] = alpha * l_sc[...] + jnp.sum(p, axis=-1, keepdims=True)
            acc[...] = alpha * acc[...] + _dot(p.astype(BF16), kv[:, FOX_DH:])
            m_sc[...] = m_new

        @pl.when(j == nq - 1)
        def _():
            valid = _valid_rows(i * tq, tq)
            o_ref[...] = jnp.where(valid, acc[...] / l_sc[...], 0.0).astype(BF16)
            lse_ref[...] = m_sc[...] + jnp.log(l_sc[...])

    return pl.pallas_call(
        kern, grid=(FOX_H, nq, nq),
        in_specs=[pl.BlockSpec((tq, FOX_DH), lambda h, i, j: (i, h)),
                  pl.BlockSpec((tk, 2 * FOX_DH), lambda h, i, j: (jnp.minimum(j, i), KV0 // 256 + h)),
                  pl.BlockSpec((None, tq, 1), lambda h, i, j: (h, i, 0)),
                  pl.BlockSpec((None, 1, tk), lambda h, i, j: (h, 0, jnp.minimum(j, i)))],
        out_specs=[pl.BlockSpec((tq, FOX_DH), lambda h, i, j: (i, h)),
                   pl.BlockSpec((None, tq, 1), lambda h, i, j: (h, i, 0))],
        out_shape=[jax.ShapeDtypeStruct((lp, FOX_W), BF16), jax.ShapeDtypeStruct((FOX_H, lp, 1), F32)],
        scratch_shapes=[pltpu.VMEM((tq, 1), F32), pltpu.VMEM((tq, 1), F32), pltpu.VMEM((tq, FOX_DH), F32)],
        compiler_params=_cparams("parallel", "parallel", "arbitrary"), name="attn_fwd",
    )(proj_a, proj_a, cq, ck)


def _attn_bwd_dq(proj_a, do, o, lse, cq, ck, dproj, tq=640):
    lp = proj_a.shape[0]
    tq = _row_tile(lp, tq)
    tk = tq
    nq = lp // tq

    def kern(q_ref, kv_ref, do_ref, o_ref, lse_ref, cq_ref, ck_ref, dp_any, dq_ref, dcq_ref, dl_ref, dq_acc):
        del dp_any
        i, j = pl.program_id(1), pl.program_id(2)

        @pl.when(j == 0)
        def _():
            dl_ref[...] = jnp.sum(do_ref[...].astype(F32) * o_ref[...].astype(F32), axis=-1, keepdims=True)
            dcq_ref[...] = jnp.zeros_like(dcq_ref)
            dq_acc[...] = jnp.zeros_like(dq_acc)

        @pl.when(j <= i)
        def _():
            kv = kv_ref[...]
            s, keep = _attn_scores(q_ref[...], kv[:, :FOX_DH], cq_ref[...], ck_ref[...], i * tq, j * tk, tq, tk)
            p = jnp.where(keep, jnp.exp(s - lse_ref[...]), 0.0)
            ds = p * (_dot_nt(do_ref[...], kv[:, FOX_DH:]) - dl_ref[...])
            dcq_ref[...] += jnp.sum(ds, axis=-1, keepdims=True)
            dq_acc[...] += _dot(ds.astype(BF16), kv[:, :FOX_DH])

        @pl.when(j == nq - 1)
        def _():
            dq_ref[...] = (dq_acc[...] * (FOX_DH ** -0.5)).astype(BF16)

    qspec = pl.BlockSpec((tq, FOX_DH), lambda h, i, j: (i, h))
    col = pl.BlockSpec((None, tq, 1), lambda h, i, j: (h, i, 0))
    return pl.pallas_call(
        kern, grid=(FOX_H, nq, nq),
        in_specs=[qspec, pl.BlockSpec((tk, 2 * FOX_DH), lambda h, i, j: (jnp.minimum(j, i), KV0 // 256 + h)),
                  qspec, qspec, col, col,
                  pl.BlockSpec((None, 1, tk), lambda h, i, j: (h, 0, jnp.minimum(j, i))),
                  pl.BlockSpec(memory_space=pl.ANY)],
        out_specs=[qspec, col, col],
        out_shape=[jax.ShapeDtypeStruct(dproj.shape, BF16), jax.ShapeDtypeStruct((FOX_H, lp, 1), F32),
                   jax.ShapeDtypeStruct((FOX_H, lp, 1), F32)],
        scratch_shapes=[pltpu.VMEM((tq, FOX_DH), F32)],
        input_output_aliases={7: 0},
        compiler_params=_cparams("parallel", "parallel", "arbitrary"), name="attn_bwd_dq",
    )(proj_a, proj_a, do, o, lse, cq, ck, dproj)


def _attn_bwd_dkv(proj_a, do, lse, delta, cq, ck, dproj, tq=640):
    lp = proj_a.shape[0]
    tq = _row_tile(lp, tq)
    tk = tq
    nq = lp // tq

    def kern(q_ref, kv_ref, do_ref, lse_ref, dl_ref, cq_ref, ck_ref, dp_any, dkv_ref, dck_ref, dk_acc, dv_acc):
        del dp_any
        j, i = pl.program_id(1), pl.program_id(2)

        @pl.when(i == 0)
        def _():
            dck_ref[...] = jnp.zeros_like(dck_ref)
            dk_acc[...] = jnp.zeros_like(dk_acc)
            dv_acc[...] = jnp.zeros_like(dv_acc)

        @pl.when(i >= j)
        def _():
            kv = kv_ref[...]
            s, keep = _attn_scores(q_ref[...], kv[:, :FOX_DH], cq_ref[...], ck_ref[...], i * tq, j * tk, tq, tk)
            p = jnp.where(keep, jnp.exp(s - lse_ref[...]), 0.0)
            dv_acc[...] += _dot_tn(p.astype(BF16), do_ref[...])
            ds = p * (_dot_nt(do_ref[...], kv[:, FOX_DH:]) - dl_ref[...])
            dck_ref[...] -= jnp.sum(ds, axis=0, keepdims=True)
            dk_acc[...] += _dot_tn(ds.astype(BF16), q_ref[...])

        @pl.when(i == nq - 1)
        def _():
            dkv_ref[...] = jnp.concatenate([dk_acc[...] * (FOX_DH ** -0.5), dv_acc[...]], axis=1).astype(BF16)

    qspec = pl.BlockSpec((tq, FOX_DH), lambda h, j, i: (jnp.maximum(i, j), h))
    col = pl.BlockSpec((None, tq, 1), lambda h, j, i: (h, jnp.maximum(i, j), 0))
    kvspec = pl.BlockSpec((tk, 2 * FOX_DH), lambda h, j, i: (j, KV0 // 256 + h))
    rowspec = pl.BlockSpec((None, 1, tk), lambda h, j, i: (h, 0, j))
    return pl.pallas_call(
        kern, grid=(FOX_H, nq, nq),
        in_specs=[qspec, kvspec, qspec, col, col, col, rowspec, pl.BlockSpec(memory_space=pl.ANY)],
        out_specs=[kvspec, rowspec],
        out_shape=[jax.ShapeDtypeStruct(dproj.shape, BF16), jax.ShapeDtypeStruct((FOX_H, 1, lp), F32)],
        scratch_shapes=[pltpu.VMEM((tk, FOX_DH), F32), pltpu.VMEM((tk, FOX_DH), F32)],
        input_output_aliases={7: 0},
        compiler_params=_cparams("parallel", "parallel", "arbitrary"), name="attn_bwd_dkv",
    )(proj_a, proj_a, do, lse, delta, cq, ck, dproj)


def _gla_gate_fwd(proj_r, wg2p, bg, tr=640):
    lp = proj_r.shape[0]
    tr = _row_tile(lp, tr)
    w = GLA_H * GLA_DK

    def kern(s_ref, w_ref, b_ref, o_ref):
        zg = _dot(s_ref[...].astype(BF16), w_ref[...]) + b_ref[...]
        o_ref[...] = jnp.where(_valid_rows(pl.program_id(0) * tr, tr), _log_sigmoid(zg) * (1.0 / GLA_TAU), 0.0)

    return pl.pallas_call(
        kern, grid=(lp // tr,),
        in_specs=[pl.BlockSpec((tr, BLK), lambda i: (i, R_SMALL_BLK128)), pl.BlockSpec((BLK, w), lambda i: (0, 0)),
                  pl.BlockSpec((1, w), lambda i: (0, 0))],
        out_specs=pl.BlockSpec((tr, w), lambda i: (i, 0)),
        out_shape=jax.ShapeDtypeStruct((lp, w), F32),
        compiler_params=_cparams("parallel"), name="gla_gate_fwd")(proj_r, wg2p, bg)


def _gla_chunk(grp, g):
    q = grp[:, :GLA_DK] * (GLA_DK ** -0.5)
    k = grp[:, GLA_DK:2 * GLA_DK]
    v = grp[:, 2 * GLA_DK:2 * GLA_DK + GLA_DV]
    r = grp[:, 2 * GLA_DK + GLA_DV:]
    b = _dot_exact(_tri(BLK, True), g)
    bl = b[BLK - 1:BLK, :]
    eb = jnp.exp(b)
    enb = jnp.exp(-b)
    ebl = jnp.exp(bl - b)
    qe, ke, kd = q * eb, k * enb, k * ebl
    causal = lax.broadcasted_iota(jnp.int32, (BLK, BLK), 1) <= lax.broadcasted_iota(jnp.int32, (BLK, BLK), 0)
    att = jnp.where(causal, _dot_nt(qe.astype(BF16), ke.astype(BF16)), 0.0)
    return q, k, v, r, bl, eb, enb, ebl, qe, ke, kd, causal, att


def _gla_fwd(proj_r, logg, gn):
    lp = proj_r.shape[0]
    nc = lp // BLK
    wv = GLA_H * GLA_DV

    def kern(grp_ref, g_ref, gn_ref, o_ref, zc_ref, st_ref, st):
        c = pl.program_id(1)

        @pl.when(c == 0)
        def _():
            st[...] = jnp.zeros_like(st)

        q, k, v, r, bl, eb, enb, ebl, qe, ke, kd, causal, att = _gla_chunk(grp_ref[...], g_ref[...])
        s_t = st[...]
        st_ref[...] = s_t
        vb = v.astype(BF16)
        o = _dot(att.astype(BF16), vb) + _dot_nt(qe.astype(BF16), s_t.astype(BF16))
        st[...] = s_t * jnp.exp(bl) + _dot_tn(vb, kd.astype(BF16))
        o_ref[...] = o
        rstd = lax.rsqrt(jnp.mean(o * o, axis=-1, keepdims=True) + EPS)
        zc_ref[...] = (r * _sigmoid(r) * (o * rstd * gn_ref[...])).astype(BF16)

    vspec = pl.BlockSpec((BLK, GLA_DV), lambda h, c: (c, h))
    return pl.pallas_call(
        kern, grid=(GLA_H, nc),
        in_specs=[pl.BlockSpec((BLK, GLA_GRP), lambda h, c: (c, R_GLA_BLK0 + h)),
                  pl.BlockSpec((BLK, GLA_DK), lambda h, c: (c, h)),
                  pl.BlockSpec((1, GLA_DV), lambda h, c: (0, h))],
        out_specs=[vspec, vspec, pl.BlockSpec((None, None, GLA_DV, GLA_DK), lambda h, c: (h, c, 0, 0))],
        out_shape=[jax.ShapeDtypeStruct((lp, wv), F32), jax.ShapeDtypeStruct((lp, wv), BF16),
                   jax.ShapeDtypeStruct((GLA_H, nc, GLA_DV, GLA_DK), F32)],
        scratch_shapes=[pltpu.VMEM((GLA_DV, GLA_DK), F32)],
        compiler_params=_cparams("parallel", "arbitrary"), name="gla_fwd")(proj_r, logg, gn)


def _gla_bwd(proj_r, logg, st_all, o_all, dzc, gn, dproj):
    lp = proj_r.shape[0]
    nc = lp // BLK

    def kern(grp_ref, g_ref, st_ref, o_ref, dzc_ref, gn_ref, dp_any, dgrp_ref, dlg_ref, dgn_ref, dst):
        del dp_any
        cc = pl.program_id(1)

        @pl.when(cc == 0)
        def _():
            dst[...] = jnp.zeros_like(dst)
            dgn_ref[...] = jnp.zeros_like(dgn_ref)

        q, k, v, r, bl, eb, enb, ebl, qe, ke, kd, causal, att = _gla_chunk(grp_ref[...], g_ref[...])
        s_t = st_ref[...]
        d_st = dst[...]
        o = o_ref[...]
        dzc_v = dzc_ref[...]
        rstd = lax.rsqrt(jnp.mean(o * o, axis=-1, keepdims=True) + EPS)
        xhat = o * rstd
        sr = _sigmoid(r)
        dr = dzc_v * (xhat * gn_ref[...]) * (sr * (1.0 + r * (1.0 - sr)))
        docn = dzc_v * (r * sr)
        dgn_ref[...] += jnp.sum(docn * xhat, axis=0, keepdims=True)
        dxh = docn * gn_ref[...]
        do = rstd * (dxh - xhat * jnp.mean(dxh * xhat, axis=-1, keepdims=True))
        dob, vb = do.astype(BF16), v.astype(BF16)
        qeb, keb, kdb = qe.astype(BF16), ke.astype(BF16), kd.astype(BF16)
        datt = jnp.where(causal, _dot_nt(dob, vb), 0.0).astype(BF16)
        dv = _dot_tn(att.astype(BF16), dob) + _dot_nt(kdb, d_st.astype(BF16))
        dqe = _dot(datt, keb) + _dot(dob, s_t.astype(BF16))
        dke = _dot_tn(datt, qeb)
        dkd = _dot(vb, d_st.astype(BF16))
        dq = dqe * eb * (GLA_DK ** -0.5)
        dk = dke * enb + dkd * ebl
        kd_dkd = dkd * kd
        db = dqe * qe - dke * ke - kd_dkd
        db_last = (jnp.sum(kd_dkd, axis=0, keepdims=True)
                   + jnp.exp(bl) * jnp.sum(s_t * d_st, axis=0, keepdims=True))
        dlg_ref[...] = _dot_exact(_tri(BLK, False), db) + db_last
        dst[...] = d_st * jnp.exp(bl) + _dot_tn(dob, qeb)
        dgrp_ref[...] = jnp.concatenate([dq, dk, dv, dr], axis=1).astype(BF16)

    rev = lambda c: nc - 1 - c
    vspec = pl.BlockSpec((BLK, GLA_DV), lambda h, c: (rev(c), h))
    return pl.pallas_call(
        kern, grid=(GLA_H, nc),
        in_specs=[pl.BlockSpec((BLK, GLA_GRP), lambda h, c: (rev(c), R_GLA_BLK0 + h)),
                  pl.BlockSpec((BLK, GLA_DK), lambda h, c: (rev(c), h)),
                  pl.BlockSpec((None, None, GLA_DV, GLA_DK), lambda h, c: (h, rev(c), 0, 0)),
                  vspec, vspec, pl.BlockSpec((1, GLA_DV), lambda h, c: (0, h)),
                  pl.BlockSpec(memory_space=pl.ANY)],
        out_specs=[pl.BlockSpec((BLK, GLA_GRP), lambda h, c: (rev(c), F_GLA_BLK0 + h)),
                   pl.BlockSpec((BLK, GLA_DK), lambda h, c: (rev(c), h)),
                   pl.BlockSpec((1, GLA_DV), lambda h, c: (0, h))],
        out_shape=[jax.ShapeDtypeStruct(dproj.shape, BF16), jax.ShapeDtypeStruct((lp, GLA_H * GLA_DK), F32),
                   jax.ShapeDtypeStruct((1, GLA_H * GLA_DV), F32)],
        scratch_shapes=[pltpu.VMEM((GLA_DV, GLA_DK), F32)],
        input_output_aliases={6: 0},
        compiler_params=_cparams("parallel", "arbitrary"), name="gla_bwd",
    )(proj_r, logg, st_all, o_all, dzc, gn, dproj)


def _small_bwd(proj_r, dlogg, wg2p, wg2pt, bg, dfa, dproj, tr=640):
    lp = proj_r.shape[0]
    tr = _row_tile(lp, tr)
    w = GLA_H * GLA_DK

    def kern(s_ref, dlg_ref, w_ref, wt_ref, b_ref, dfa_ref, dp_any, ds_ref, dbg_ref, dw_ref):
        del dp_any
        i = pl.program_id(0)
        sb = s_ref[...].astype(BF16)
        zg = _dot(sb, w_ref[...]) + b_ref[...]
        dzg = jnp.where(_valid_rows(i * tr, tr), dlg_ref[...] * (1.0 / GLA_TAU) * _sigmoid(-zg), 0.0)

        @pl.when(i == 0)
        def _():
            dbg_ref[...] = jnp.zeros_like(dbg_ref)
            dw_ref[...] = jnp.zeros_like(dw_ref)

        dbg_ref[...] += jnp.sum(dzg, axis=0, keepdims=True)
        dzb = dzg.astype(BF16)
        dw_ref[...] += _dot_tn(sb, dzb)
        dsm = _dot(dzb, wt_ref[...]) + dfa_ref[...]
        ds_ref[...] = jnp.concatenate([dsm, jnp.zeros((tr, SMALL_W - BLK), F32)], axis=1).astype(BF16)

    return pl.pallas_call(
        kern, grid=(lp // tr,),
        in_specs=[pl.BlockSpec((tr, BLK), lambda i: (i, R_SMALL_BLK128)), pl.BlockSpec((tr, w), lambda i: (i, 0)),
                  pl.BlockSpec((BLK, w), lambda i: (0, 0)), pl.BlockSpec((w, BLK), lambda i: (0, 0)),
                  pl.BlockSpec((1, w), lambda i: (0, 0)), pl.BlockSpec((tr, BLK), lambda i: (i, 0)),
                  pl.BlockSpec(memory_space=pl.ANY)],
        out_specs=[pl.BlockSpec((tr, SMALL_W), lambda i: (i, F_SMALL_BLK0)), pl.BlockSpec((1, w), lambda i: (0, 0)),
                   pl.BlockSpec((BLK, w), lambda i: (0, 0))],
        out_shape=[jax.ShapeDtypeStruct(dproj.shape, BF16), jax.ShapeDtypeStruct((1, w), F32),
                   jax.ShapeDtypeStruct((BLK, w), F32)],
        input_output_aliases={6: 0},
        compiler_params=_cparams("arbitrary"), name="small_bwd",
    )(proj_r, dlogg, wg2p, wg2pt, bg, dfa, dproj)


def _merge_fwd(proj_r, gate_b3, ya, yb, yc, tr=640):
    lp = proj_r.shape[0]
    tr = _row_tile(lp, tr)
    tn = GATE_TN

    def kern(g_ref, b_ref, ya_ref, yb_ref, yc_ref, o_ref):
        g = g_ref[...]
        mix = (_sigmoid(g[:, :tn] + b_ref[0:1, :]) * ya_ref[...]
               + _sigmoid(g[:, tn:2 * tn] + b_ref[1:2, :]) * yb_ref[...]
               + _sigmoid(g[:, 2 * tn:] + b_ref[2:3, :]) * yc_ref[...])
        o_ref[...] = mix.astype(BF16)

    y = pl.BlockSpec((tr, tn), lambda i, j: (i, j))
    return pl.pallas_call(
        kern, grid=(lp // tr, D // tn),
        in_specs=[pl.BlockSpec((tr, 3 * tn), lambda i, j: (i, R_GATE_BLK0 + j)),
                  pl.BlockSpec((3, tn), lambda i, j: (0, j)), y, y, y],
        out_specs=y, out_shape=jax.ShapeDtypeStruct((lp, D), BF16),
        compiler_params=_cparams("parallel", "parallel"), name="merge_fwd")(proj_r, gate_b3, ya, yb, yc)


def _merge_bwd(proj_r, gate_b3, ya, yb, yc, dmix, tr=640):
    lp = proj_r.shape[0]
    tr = _row_tile(lp, tr)
    tn = GATE_TN

    def kern(g_ref, b_ref, ya_ref, yb_ref, yc_ref, dm_ref, dya_ref, dyb_ref, dyc_ref, dg_ref, db_ref):
        i = pl.program_id(1)
        g = g_ref[...]
        dm = dm_ref[...]

        @pl.when(i == 0)
        def _():
            db_ref[...] = jnp.zeros_like(db_ref)

        dgs = []
        for n, (y_ref, dy_ref) in enumerate(((ya_ref, dya_ref), (yb_ref, dyb_ref), (yc_ref, dyc_ref))):
            s = _sigmoid(g[:, n * tn:(n + 1) * tn] + b_ref[n:n + 1, :])
            dy_ref[...] = (dm * s).astype(BF16)
            dgn = dm * y_ref[...] * (s * (1.0 - s))
            db_ref[n:n + 1, :] += jnp.sum(dgn, axis=0, keepdims=True)
            dgs.append(dgn)
        dg_ref[...] = jnp.concatenate(dgs, axis=1).astype(BF16)

    y = pl.BlockSpec((tr, tn), lambda j, i: (i, j))
    bspec = pl.BlockSpec((3, tn), lambda j, i: (0, j))
    return pl.pallas_call(
        kern, grid=(D // tn, lp // tr),
        in_specs=[pl.BlockSpec((tr, 3 * tn), lambda j, i: (i, R_GATE_BLK0 + j)), bspec, y, y, y, y],
        out_specs=[y, y, y, pl.BlockSpec((tr, 3 * tn), lambda j, i: (i, F_GATE_BLK0 + j)), bspec],
        out_shape=[jax.ShapeDtypeStruct((lp, D), BF16)] * 3
        + [jax.ShapeDtypeStruct((lp, NP), BF16), jax.ShapeDtypeStruct((3, D), F32)],
        compiler_params=_cparams("parallel", "arbitrary"), name="merge_bwd")(proj_r, gate_b3, ya, yb, yc, dmix)


def _final_loss(h, gf, tgt):
    lp = h.shape[0]
    nb = lp // BLK

    def kern(h_ref, g_ref, t_ref, dh_ref, dg_ref, ls_ref):
        i = pl.program_id(0)

        @pl.when(i == 0)
        def _():
            dh_ref[...] = jnp.zeros_like(dh_ref)
            dg_ref[...] = jnp.zeros_like(dg_ref)
            ls_ref[...] = jnp.zeros_like(ls_ref)

        @pl.when(i > 0)
        def _():
            x = h_ref[...]
            r = lax.rsqrt(jnp.mean(x * x, axis=-1, keepdims=True) + EPS)
            xhat = x * r
            err = xhat * g_ref[...] - t_ref[...]
            ls_ref[...] += jnp.sum(jnp.sum(err * err, axis=0, keepdims=True), axis=1, keepdims=True)
            dy = err * (1.0 / D)
            dg_ref[...] += jnp.sum(dy * xhat, axis=0, keepdims=True)
            dxh = dy * g_ref[...]
            dh_ref[...] = r * (dxh - xhat * jnp.mean(dxh * xhat, axis=-1, keepdims=True))

    row = pl.BlockSpec((BLK, D), lambda i: (i, 0))
    vec = pl.BlockSpec((1, D), lambda i: (0, 0))
    return pl.pallas_call(
        kern, grid=(nb,),
        in_specs=[row, vec, pl.BlockSpec((BLK, D), lambda i: (jnp.maximum(i - 1, 0), 0))],
        out_specs=[row, vec, pl.BlockSpec((1, 1), lambda i: (0, 0))],
        out_shape=[jax.ShapeDtypeStruct((lp, D), F32), jax.ShapeDtypeStruct((1, D), F32),
                   jax.ShapeDtypeStruct((1, 1), F32)],
        compiler_params=_cparams("arbitrary"), name="final_loss")(h, gf, tgt)


def _gate_cols(c):
    ct = c[:, :FOX_H].T
    return ct[:, :, None], ct[:, None, :]


def _layer_fwd(h, w):
    xn = _rmsnorm_fwd(h, w["norm1_g"])
    proj_a = _mm_nn(xn, w["w_in"], n0=0, n=REST0, out_dtype=BF16, name="proj_a")
    proj_r = _mm_nn(xn, w["w_in"], n0=REST0, n=NREST, out_dtype=F32, name="proj_r")
    cq, ck = _gate_cols(_fox_gate_fwd(proj_r, w["bf128"]))
    oa, lse = _attn_fwd(proj_a, cq, ck)
    zb = _convb_fwd(proj_r, w["conv_w"])
    logg = _gla_gate_fwd(proj_r, w["wg2p"], w["gla_b_g"])
    o_gla, zc, st_all = _gla_fwd(proj_r, logg, w["gla_norm_g"])
    ya = _mm_nn(oa, w["w_a_o"], name="branch_a")
    yb = _mm_nn(zb, w["w_b_o"], name="branch_b")
    yc = _mm_nn(zc, w["w_c_o"], name="branch_c")
    mix = _merge_fwd(proj_r, w["gate_b3"], ya, yb, yc)
    h1 = _mm_nn(mix, w["w_o"], res=h, name="out_proj")
    xn2 = _rmsnorm_fwd(h1, w["norm2_g"])
    z = _mm_nn(xn2, w["w_up"], name="up_proj")
    a = _mlp_gate_fwd(z, w["mlp_conv_w"])
    h2 = _mm_nn(a, w["w_down"], res=h1, tk=D_FF // 4, name="down_proj")
    saved = dict(h=h, xn=xn, proj_a=proj_a, proj_r=proj_r, cq=cq, ck=ck, oa=oa, lse=lse, zb=zb, logg=logg,
                 o_gla=o_gla, zc=zc, st_all=st_all, ya=ya, yb=yb, yc=yc, mix=mix, h1=h1, xn2=xn2, z=z, a=a)
    return h2, saved


def _layer_bwd(dh2, w, s):
    g = {}
    da = _mm_nt(dh2, w["w_down"], tn=D_FF // 4, name="d_down_in")
    g["w_down"] = _mm_tn(s["a"], dh2, tk=D_FF // 4, name="d_w_down")
    dzg, dzu, dmw_g, dmw_u = _mlp_gate_bwd(s["z"], da, w["mlp_conv_w"])
    g["mlp_conv_w"] = jnp.concatenate([dmw_g, dmw_u], axis=1)
    dxn2 = _mm_nt(dzg, w["w_up"], k0=0, kw=D_FF, name="d_up_in_g")
    dxn2 = _mm_nt(dzu, w["w_up"], k0=D_FF, kw=D_FF, add=dxn2, name="d_up_in_u")
    g["w_up"] = jnp.concatenate([_mm_tn(s["xn2"], dzg, name="d_w_up_g"), _mm_tn(s["xn2"], dzu, name="d_w_up_u")], axis=1)
    dh1, g["norm2_g"] = _rmsnorm_bwd(s["h1"], w["norm2_g"], dxn2, dh2)
    dmix = _mm_nt(dh1, w["w_o"], name="d_out_proj_in")
    g["w_o"] = _mm_tn(s["mix"], dh1, name="d_w_o")
    dya, dyb, dyc, dproj, g["gate_b3"] = _merge_bwd(s["proj_r"], w["gate_b3"], s["ya"], s["yb"], s["yc"], dmix)
    doa = _mm_nt(dya, w["w_a_o"], out_dtype=BF16, name="d_branch_a_in")
    g["w_a_o"] = _mm_tn(s["oa"], dya, name="d_w_a_o")
    dzb = _mm_nt(dyb, w["w_b_o"], name="d_branch_b_in")
    g["w_b_o"] = _mm_tn(s["zb"], dyb, name="d_w_b_o")
    dzc = _mm_nt(dyc, w["w_c_o"], name="d_branch_c_in")
    g["w_c_o"] = _mm_tn(s["zc"], dyc, name="d_w_c_o")
    dproj, dlogg, g["gla_norm_g"] = _gla_bwd(s["proj_r"], s["logg"], s["st_all"], s["o_gla"], dzc, w["gla_norm_g"], dproj)
    dproj, g["conv_w"] = _convb_bwd(s["proj_r"], dzb, w["conv_w"], dproj)
    dproj, dcq, delta = _attn_bwd_dq(s["proj_a"], doa, s["oa"], s["lse"], s["cq"], s["ck"], dproj)
    dproj, dck = _attn_bwd_dkv(s["proj_a"], doa, s["lse"], delta, s["cq"], s["ck"], dproj)
    dc = jnp.pad((dcq[:, :, 0] + dck[:, 0, :]).T, ((0, 0), (0, BLK - FOX_H)))
    dfa, g["bf128"] = _fox_gate_bwd(s["proj_r"], dc, w["bf128"])
    dproj, g["gla_b_g"], g["wg2p"] = _small_bwd(s["proj_r"], dlogg, w["wg2p"], w["wg2p"].T, w["gla_b_g"], dfa, dproj)
    dxn = _mm_nt(dproj, w["w_in"], name="d_in_proj_in")
    g["w_in"] = _mm_tn(s["xn"], dproj, name="d_w_in")
    dh0, g["norm1_g"] = _rmsnorm_bwd(s["h"], w["norm1_g"], dxn, dh1)
    return dh0, g


def _local_step(x, tgt, meta, final_g, layers):
    h = jnp.concatenate([jnp.zeros((PAD, D), F32), meta, x], axis=0)
    saved = []
    for w in layers:
        h, s = _layer_fwd(h, w)
        saved.append(s)
    dh, dgf, sq = _final_loss(h, final_g, tgt)
    grads = [None] * len(layers)
    for l in reversed(range(len(layers))):
        dh, grads[l] = _layer_bwd(dh, layers[l], saved[l])
    return sq[0, 0], dh[BLK:], dh[PAD:BLK], dgf, grads


def _w_in_to_kernel(w_nat):
    parts = [w_nat[:, s:s + n] for s, n in _segments()]
    parts.append(jnp.zeros((w_nat.shape[0], SMALL_W - 8 - GLA_R), w_nat.dtype))
    return jnp.concatenate(parts, axis=1)


def _w_in_from_kernel(w_k):
    pieces, off = [], 0
    for s, n in _segments():
        pieces.append((s, w_k[:, off:off + n]))
        off += n
    return jnp.concatenate([p for _, p in sorted(pieces, key=lambda t: t[0])], axis=1)


def _pad_rows_at(a, row0, nrows):
    return jnp.pad(a, ((row0, nrows - row0 - a.shape[0]), (0, 0)))


def _layer_weights(w_in, w_a_o, w_b_o, w_c_o, w_o, w_up, w_down, conv_w, gla_w_g2, mlp_conv_w,
                   norm1_g, fox_b_f, gate_b, gla_b_g, gla_norm_g, norm2_g):
    return dict(
        w_in=_w_in_to_kernel(w_in), w_a_o=w_a_o, w_b_o=w_b_o, w_c_o=w_c_o, w_o=w_o, w_up=w_up, w_down=w_down,
        conv_w=conv_w, mlp_conv_w=mlp_conv_w,
        wg2p=_pad_rows_at(gla_w_g2, 8, BLK).astype(BF16),
        norm1_g=norm1_g[None], norm2_g=norm2_g[None], gla_b_g=gla_b_g[None], gla_norm_g=gla_norm_g[None],
        bf128=jnp.pad(fox_b_f, (0, BLK - FOX_H))[None], gate_b3=gate_b.reshape(3, D))


def _layer_grads_natural(g):
    return dict(
        w_in=_w_in_from_kernel(g["w_in"]), w_a_o=g["w_a_o"], w_b_o=g["w_b_o"], w_c_o=g["w_c_o"], w_o=g["w_o"],
        w_up=g["w_up"], w_down=g["w_down"], conv_w=g["conv_w"], mlp_conv_w=g["mlp_conv_w"],
        gla_w_g2=g["wg2p"][8:8 + GLA_R], norm1_g=g["norm1_g"][0], norm2_g=g["norm2_g"][0],
        gla_b_g=g["gla_b_g"][0], gla_norm_g=g["gla_norm_g"][0], fox_b_f=g["bf128"][0, :FOX_H],
        gate_b=g["gate_b3"].reshape(3 * D))


def _exchange(arrays, bcast, name):
    n_arr = len(arrays)
    out_shape = [jax.ShapeDtypeStruct(((N_DEV,) + a.shape) if b else a.shape, a.dtype) for a, b in zip(arrays, bcast)]

    def body(*refs):
        ins, outs = refs[:n_arr], refs[n_arr:2 * n_arr]
        send_sems, recv_sems, local_sems = refs[2 * n_arr:]
        x, y, c = lax.axis_index("x"), lax.axis_index("y"), lax.axis_index("c")
        me = 4 * x + 2 * y + c

        def flip(v, bit):
            return 1 - v if bit else v

        local, sends, recvs = [], [], []
        for n in range(n_arr):
            src_me = ins[n] if bcast[n] else ins[n].at[me]
            local.append(pltpu.make_async_copy(src_me, outs[n].at[me], local_sems.at[n]))
            local[-1].start()
        for k in range(1, N_DEV):
            bx, by, bc = (k >> 2) & 1, (k >> 1) & 1, k & 1
            px, py, pc = flip(x, bx), flip(y, by), flip(c, bc)
            peer = 4 * px + 2 * py + pc
            for n in range(n_arr):
                src = ins[n] if bcast[n] else ins[n].at[peer]

                def copy(dst_slot, src=src, n=n, k=k, to=(px, py, pc)):
                    return pltpu.make_async_remote_copy(
                        src_ref=src, dst_ref=outs[n].at[dst_slot], send_sem=send_sems.at[n, k - 1],
                        recv_sem=recv_sems.at[n, k - 1], device_id=to, device_id_type=pl.DeviceIdType.MESH)

                sends.append(copy(me))
                sends[-1].start()
                recvs.append(copy(peer))
        for cp in recvs:
            cp.wait_recv()
        for cp in sends:
            cp.wait_send()
        for cp in local:
            cp.wait()

    hbm = pl.BlockSpec(memory_space=pltpu.HBM)
    return pl.pallas_call(
        body, out_shape=out_shape, in_specs=[hbm] * n_arr, out_specs=[hbm] * n_arr,
        scratch_shapes=[pltpu.SemaphoreType.DMA((n_arr, N_DEV - 1)), pltpu.SemaphoreType.DMA((n_arr, N_DEV - 1)),
                        pltpu.SemaphoreType.DMA((n_arr,))],
        compiler_params=pltpu.CompilerParams(has_side_effects=True), name=name)(*arrays)


def _adamw(recv, w, m, v, layer, prev=None, name="adamw"):
    n_slot, r, c = recv.shape
    lyr = w.shape[0]
    tr = r
    for t in range(16, r, 16):
        if r % t == 0 and t * c <= ADAMW_BLOCK_ELEMS:
            tr = t
    if r * c <= ADAMW_BLOCK_ELEMS:
        tr = r
    bc1, bc2 = 1.0 - ADAM_B1 ** ADAM_STEP, 1.0 - ADAM_B2 ** ADAM_STEP

    def kern(*refs):
        r_ref, w_ref, m_ref, v_ref = refs[:4]
        g_out, d_out, m_out, v_out = refs[-4:]
        g = r_ref[0].astype(F32)
        for sidx in range(1, n_slot):
            g = g + r_ref[sidx].astype(F32)
        m_new = ADAM_B1 * m_ref[...] + (1.0 - ADAM_B1) * g
        v_new = ADAM_B2 * v_ref[...] + (1.0 - ADAM_B2) * (g * g)
        g_out[...] = g
        m_out[...] = m_new
        v_out[...] = v_new
        d_out[...] = -ADAM_LR * ((m_new / bc1) / (jnp.sqrt(v_new / bc2) + ADAM_EPS) + ADAM_WD * w_ref[...])

    lspec = pl.BlockSpec((None, tr, c), lambda i: (layer, i, 0))
    in_specs = [pl.BlockSpec((n_slot, tr, c), lambda i: (0, i, 0)), lspec, lspec, lspec]
    args = [recv, w, m, v]
    aliases = {}
    if prev is not None:
        in_specs += [pl.BlockSpec(memory_space=pl.ANY)] * 4
        args += list(prev)
        aliases = {4: 0, 5: 1, 6: 2, 7: 3}
    return pl.pallas_call(
        kern, grid=(r // tr,), in_specs=in_specs, out_specs=[lspec] * 4,
        out_shape=[jax.ShapeDtypeStruct((lyr, r, c), F32)] * 4, input_output_aliases=aliases,
        compiler_params=_cparams("parallel"), name=name)(*args)


_BIG = ("w_in", "w_a_o", "w_b_o", "w_c_o", "w_o", "w_up", "w_down")
_COL_SHARDED = ("w_in", "w_a_o", "w_b_o", "w_c_o", "w_up", "conv_w", "gla_w_g2", "mlp_conv_w")
_REPL = ("norm1_g", "fox_b_f", "gate_b", "gla_b_g", "gla_norm_g", "norm2_g")


def _cols_from_slots(a):
    return jnp.transpose(a, (1, 0, 2)).reshape(a.shape[1], N_DEV * a.shape[2])


def _cols_to_slots(a):
    r, c8 = a.shape
    return jnp.transpose(a.reshape(r, N_DEV, c8 // N_DEV), (1, 0, 2))


def _rows_to_slots(a):
    return a.reshape(N_DEV, a.shape[0] // N_DEV, a.shape[1])


def kernel(x, meta_tokens, norm1_g, w_in, fox_b_f, gate_b, conv_w, gla_w_g2, gla_b_g, gla_norm_g, w_a_o, w_b_o, w_c_o, w_o, norm2_g, w_up, mlp_conv_w, w_down, final_norm_g, loss_target, m_meta_tokens, m_norm1_g, m_w_in, m_fox_b_f, m_gate_b, m_conv_w, m_gla_w_g2, m_gla_b_g, m_gla_norm_g, m_w_a_o, m_w_b_o, m_w_c_o, m_w_o, m_norm2_g, m_w_up, m_mlp_conv_w, m_w_down, m_final_norm_g, v_meta_tokens, v_norm1_g, v_w_in, v_fox_b_f, v_gate_b, v_conv_w, v_gla_w_g2, v_gla_b_g, v_gla_norm_g, v_w_a_o, v_w_b_o, v_w_c_o, v_w_o, v_norm2_g, v_w_up, v_mlp_conv_w, v_w_down, v_final_norm_g):
    names = ("meta_tokens", "norm1_g", "w_in", "fox_b_f", "gate_b", "conv_w", "gla_w_g2", "gla_b_g", "gla_norm_g",
             "w_a_o", "w_b_o", "w_c_o", "w_o", "norm2_g", "w_up", "mlp_conv_w", "w_down", "final_norm_g")
    wts = dict(zip(names, (meta_tokens, norm1_g, w_in, fox_b_f, gate_b, conv_w, gla_w_g2, gla_b_g, gla_norm_g,
                           w_a_o, w_b_o, w_c_o, w_o, norm2_g, w_up, mlp_conv_w, w_down, final_norm_g)))
    mom = dict(zip(names, (m_meta_tokens, m_norm1_g, m_w_in, m_fox_b_f, m_gate_b, m_conv_w, m_gla_w_g2, m_gla_b_g,
                           m_gla_norm_g, m_w_a_o, m_w_b_o, m_w_c_o, m_w_o, m_norm2_g, m_w_up, m_mlp_conv_w, m_w_down,
                           m_final_norm_g)))
    var = dict(zip(names, (v_meta_tokens, v_norm1_g, v_w_in, v_fox_b_f, v_gate_b, v_conv_w, v_gla_w_g2, v_gla_b_g,
                           v_gla_norm_g, v_w_a_o, v_w_b_o, v_w_c_o, v_w_o, v_norm2_g, v_w_up, v_mlp_conv_w, v_w_down,
                           v_final_norm_g)))

    small = _exchange([conv_w, gla_w_g2, mlp_conv_w, meta_tokens], [True] * 4, "gather_small")
    conv_full = jnp.transpose(small[0], (1, 2, 0, 3)).reshape(DEPTH, 3, CONV_CH)
    g2_full = jnp.transpose(small[1], (1, 2, 0, 3)).reshape(DEPTH, GLA_R, GLA_H * GLA_DK)
    mconv_full = jnp.transpose(small[2], (1, 2, 0, 3)).reshape(DEPTH, 3, 2 * D_FF)
    meta_full = _cols_from_slots(small[3])
    layers = []
    for l in range(DEPTH):
        got = dict(zip(_BIG, _exchange([wts[n][l].astype(BF16) for n in _BIG], [True] * len(_BIG), "gather_weights")))
        full = {n: (_cols_from_slots(got[n]) if n in _COL_SHARDED else got[n].reshape(-1, got[n].shape[-1]))
                for n in _BIG}
        layers.append(_layer_weights(
            full["w_in"], full["w_a_o"], full["w_b_o"], full["w_c_o"], full["w_o"], full["w_up"], full["w_down"],
            conv_full[l], g2_full[l], mconv_full[l], norm1_g[l], fox_b_f[l], gate_b[l], gla_b_g[l], gla_norm_g[l],
            norm2_g[l]))

    sq, grad_x, dmeta, dgf, grads_k = _local_step(x[0], loss_target[0], meta_full, final_norm_g[None], layers)
    loss = lax.psum(sq * (0.5 / D), ("x", "y", "c"))
    grads = [_layer_grads_natural(g) for g in grads_k]

    out_g, out_d, out_m, out_v = {}, {}, {}, {}

    def update(name, recv, layer, lyr_shape, prev):
        w3, m3, v3 = (t[name].reshape(lyr_shape) for t in (wts, mom, var))
        return _adamw(recv.reshape((recv.shape[0],) + lyr_shape[1:]), w3, m3, v3, layer, prev, name="adamw_" + name)

    def store(name, res):
        shape = wts[name].shape
        out_g[name], out_d[name], out_m[name], out_v[name] = (t.reshape(shape) for t in res)

    res_big = {n: None for n in _BIG}
    for l in range(DEPTH):
        send = [(_cols_to_slots(grads[l][n]) if n in _COL_SHARDED else _rows_to_slots(grads[l][n])) for n in _BIG]
        recv = _exchange(send, [False] * len(_BIG), "scatter_grads")
        for n, rv in zip(_BIG, recv):
            res_big[n] = update(n, rv, l, wts[n].shape, res_big[n])
    for n in _BIG:
        store(n, res_big[n])

    def stack_layers(name):
        return jnp.stack([grads[l][name] for l in range(DEPTH)])

    s_conv = jnp.transpose(stack_layers("conv_w").reshape(DEPTH, 3, N_DEV, -1), (2, 0, 1, 3))
    s_g2 = jnp.transpose(stack_layers("gla_w_g2").reshape(DEPTH, GLA_R, N_DEV, -1), (2, 0, 1, 3))
    s_mconv = jnp.transpose(stack_layers("mlp_conv_w").reshape(DEPTH, 3, N_DEV, -1), (2, 0, 1, 3))
    s_meta = _cols_to_slots(dmeta)
    repl = [stack_layers(n) for n in _REPL] + [dgf]
    pack = jnp.concatenate([jnp.pad(a.reshape(-1), (0, (-a.size) % 1024)) for a in repl]).reshape(-1, BLK)
    r_conv, r_g2, r_mconv, r_meta, r_pack = _exchange(
        [s_conv, s_g2, s_mconv, s_meta, pack], [False, False, False, False, True], "scatter_small")
    store("conv_w", update("conv_w", r_conv, 0, (1, DEPTH * 3, CONV_CH // N_DEV), None))
    store("gla_w_g2", update("gla_w_g2", r_g2, 0, (1, DEPTH * GLA_R, GLA_H * GLA_DK // N_DEV), None))
    store("mlp_conv_w", update("mlp_conv_w", r_mconv, 0, (1, DEPTH * 3, 2 * D_FF // N_DEV), None))
    store("meta_tokens", update("meta_tokens", r_meta, 0, (1, N_META, D // N_DEV), None))
    off = 0
    for n, a in zip(_REPL + ("final_norm_g",), repl):
        rows = (a.size + 1023) // 1024 * 8
        part = r_pack[:, off:off + rows].reshape(N_DEV, -1)[:, :a.size]
        off += rows
        shape2 = (1, 1, a.size) if a.size % BLK else (1, a.size // BLK, BLK)
        store(n, update(n, part, 0, shape2, None))

    order = lambda d: [d[n] for n in names]
    return (loss, grad_x[None], *order(out_g), *order(out_d), *order(out_m), *order(out_v))
```

```python
import functools

import jax
import jax.numpy as jnp
from jax import lax
from jax.experimental import pallas as pl
from jax.experimental.pallas import tpu as pltpu

F32 = jnp.float32
BF16 = jnp.bfloat16

D = 2048
DEPTH = 2
N_META = 16
BLK = 128
PAD = BLK - N_META
EPS = 1e-6
NEG = -1e30

FOX_H, FOX_DH = 8, 128
FOX_W = FOX_H * FOX_DH
CONV_CH = 1024
GLA_H, GLA_DK, GLA_DV, GLA_R, GLA_TAU = 4, 128, 256, 16, 16.0
D_FF = 5632
N_IN = 15384
N_DEV = 8

ADAM_LR, ADAM_B1, ADAM_B2, ADAM_EPS, ADAM_WD, ADAM_STEP = 0.001, 0.9, 0.999, 1e-08, 0.01, 10

CONV_TC = 512
GATE_TN = 512
KV0 = 1024
REST0 = 3072
GLA_GRP = 768
SMALL_W = 512
NP = 15872
NREST = NP - REST0
R_CONV_BLK0 = 0
R_GLA_BLK0 = (6144 - REST0) // GLA_GRP
R_GATE_BLK0 = (9216 - REST0) // (3 * GATE_TN)
R_SMALL_BLK128 = (15360 - REST0) // 128
F_CONV_BLK0 = 3072 // (3 * CONV_TC)
F_GLA_BLK0 = 6144 // GLA_GRP
F_GATE_BLK0 = 9216 // (3 * GATE_TN)
F_SMALL_BLK0 = 15360 // SMALL_W

VMEM_LIMIT = 56 * 1024 * 1024
ADAMW_BLOCK_ELEMS = 128 * 1024


def _segments():
    seg = [(0, 1024)]
    for h in range(FOX_H):
        seg += [(1024 + 128 * h, 128), (2048 + 128 * h, 128)]
    for j in range(CONV_CH // CONV_TC):
        seg += [(3080 + CONV_TC * j, CONV_TC), (4104 + CONV_TC * j, CONV_TC), (5128 + CONV_TC * j, CONV_TC)]
    for h in range(GLA_H):
        seg += [(6152 + 128 * h, 128), (6664 + 128 * h, 128), (7176 + 256 * h, 256), (8200 + 256 * h, 256)]
    for j in range(D // GATE_TN):
        seg += [(9240 + GATE_TN * j, GATE_TN), (11288 + GATE_TN * j, GATE_TN), (13336 + GATE_TN * j, GATE_TN)]
    seg += [(3072, 8), (9224, 16)]
    return seg


def _cparams(*sem):
    return pltpu.CompilerParams(dimension_semantics=sem, vmem_limit_bytes=VMEM_LIMIT)


def _row_tile(n, target):
    best = BLK
    t = BLK
    while t <= min(n, target):
        if n % t == 0:
            best = t
        t += BLK
    return best


def _sigmoid(x):
    return 1.0 / (1.0 + jnp.exp(-x))


def _log_sigmoid(x):
    return jnp.minimum(x, 0.0) - jnp.log(1.0 + jnp.exp(-jnp.abs(x)))


def _valid_rows(row0, n):
    return (row0 + lax.broadcasted_iota(jnp.int32, (n, 1), 0)) >= PAD


def _dot(a, b):
    return jnp.dot(a, b, preferred_element_type=F32)


def _dot_nt(a, b):
    return lax.dot_general(a, b, (((1,), (1,)), ((), ())), preferred_element_type=F32)


def _dot_tn(a, b):
    return lax.dot_general(a, b, (((0,), (0,)), ((), ())), preferred_element_type=F32)


def _exchange_copies(ins, outs, bcast, send_sems, recv_sems, local_sems):
    x, y, c = lax.axis_index("x"), lax.axis_index("y"), lax.axis_index("c")
    me = 4 * x + 2 * y + c
    local, sends, recvs = [], [], []
    for n in range(len(ins)):
        src_me = ins[n] if bcast[n] else ins[n].at[me]
        local.append(pltpu.make_async_copy(src_me, outs[n].at[me], local_sems.at[n]))
    for k in range(1, N_DEV):
        px = 1 - x if (k >> 2) & 1 else x
        py = 1 - y if (k >> 1) & 1 else y
        pc = 1 - c if k & 1 else c
        peer = 4 * px + 2 * py + pc
        for n in range(len(ins)):
            src = ins[n] if bcast[n] else ins[n].at[peer]

            def copy(dst_slot, src=src, n=n, k=k, to=(px, py, pc)):
                return pltpu.make_async_remote_copy(
                    src_ref=src, dst_ref=outs[n].at[dst_slot], send_sem=send_sems.at[n, k - 1],
                    recv_sem=recv_sems.at[n, k - 1], device_id=to, device_id_type=pl.DeviceIdType.MESH)

            sends.append(copy(me))
            recvs.append(copy(peer))
    return local, sends, recvs


def _exchange_start(copies):
    local, sends, _ = copies
    for cp in local + sends:
        cp.start()


def _exchange_wait(copies):
    local, sends, recvs = copies
    for cp in recvs:
        cp.wait_recv()
    for cp in sends:
        cp.wait_send()
    for cp in local:
        cp.wait()


def _exchange_shapes(arrays, bcast):
    return [jax.ShapeDtypeStruct(((N_DEV,) + a.shape) if b else a.shape, a.dtype) for a, b in zip(arrays, bcast)]


def _exchange_sems(n_arr):
    return [pltpu.SemaphoreType.DMA((n_arr, N_DEV - 1)), pltpu.SemaphoreType.DMA((n_arr, N_DEV - 1)),
            pltpu.SemaphoreType.DMA((n_arr,))]


def _exchange(arrays, bcast, name):
    n_arr = len(arrays)

    def body(*refs):
        copies = _exchange_copies(refs[:n_arr], refs[n_arr:2 * n_arr], bcast, *refs[2 * n_arr:])
        _exchange_start(copies)
        _exchange_wait(copies)

    hbm = pl.BlockSpec(memory_space=pltpu.HBM)
    return pl.pallas_call(
        body, out_shape=_exchange_shapes(arrays, bcast), in_specs=[hbm] * n_arr, out_specs=[hbm] * n_arr,
        scratch_shapes=_exchange_sems(n_arr),
        compiler_params=pltpu.CompilerParams(has_side_effects=True), name=name)(*arrays)


def _pcall(kern, *, grid, in_specs, out_specs, out_shape, scratch, sem, name, args, aliases=None, exch=None):
    params = pltpu.CompilerParams(dimension_semantics=sem, vmem_limit_bytes=VMEM_LIMIT,
                                  has_side_effects=exch is not None)
    kw = dict(grid=grid, compiler_params=params, name=name, input_output_aliases=aliases or {})
    if exch is None:
        out = pl.pallas_call(kern, in_specs=in_specs, out_specs=out_specs, out_shape=out_shape,
                             scratch_shapes=scratch, **kw)(*args)
        return out, None
    arrays, bcast = exch
    n_x, n_in, n_out, n_sc = len(arrays), len(in_specs), len(out_specs), len(scratch)

    def hosted(*refs):
        ins, x_in = refs[:n_in], refs[n_in:n_in + n_x]
        outs, x_out = refs[n_in + n_x:n_in + n_x + n_out], refs[n_in + n_x + n_out:n_in + 2 * n_x + n_out]
        sc, sems = refs[n_in + 2 * n_x + n_out:n_in + 2 * n_x + n_out + n_sc], refs[n_in + 2 * n_x + n_out + n_sc:]
        ids = [pl.program_id(d) for d in range(len(grid))]
        first = functools.reduce(jnp.logical_and, [i == 0 for i in ids])
        last = functools.reduce(jnp.logical_and, [i == g - 1 for i, g in zip(ids, grid)])

        @pl.when(first)
        def _():
            _exchange_start(_exchange_copies(x_in, x_out, bcast, *sems))

        kern(*ins, *outs, *sc)

        @pl.when(last)
        def _():
            _exchange_wait(_exchange_copies(x_in, x_out, bcast, *sems))

    hbm = pl.BlockSpec(memory_space=pltpu.HBM)
    out = pl.pallas_call(
        hosted, in_specs=list(in_specs) + [hbm] * n_x, out_specs=list(out_specs) + [hbm] * n_x,
        out_shape=list(out_shape) + _exchange_shapes(arrays, bcast),
        scratch_shapes=list(scratch) + _exchange_sems(n_x), **kw)(*args, *arrays)
    return out[:n_out], out[n_out:]


def _mm_nn(a, b, *, n0=0, n=None, out_dtype=F32, res=None, tm=640, tn=512, tk=None, name="mm_nn", exch=None):
    m, k = a.shape
    n = b.shape[1] - n0 if n is None else n
    tm = _row_tile(m, tm)
    tk = k if tk is None else tk
    nk = k // tk
    assert k % tk == 0 and n % tn == 0 and n0 % tn == 0
    nb0 = n0 // tn

    def kern(*refs):
        if res is None:
            a_ref, b_ref, o_ref, acc = refs
        else:
            a_ref, b_ref, r_ref, o_ref, acc = refs
        kk = pl.program_id(2)
        row0 = pl.program_id(0) * tm

        def finish(prod):
            if res is None:
                o_ref[...] = prod.astype(out_dtype)
            else:
                o_ref[...] = (r_ref[...] + jnp.where(_valid_rows(row0, tm), prod, 0.0)).astype(out_dtype)

        prod = _dot(a_ref[...].astype(BF16), b_ref[...].astype(BF16))
        if nk == 1:
            finish(prod)
            return

        @pl.when(kk == 0)
        def _():
            acc[...] = prod

        @pl.when(kk > 0)
        def _():
            acc[...] += prod

        @pl.when(kk == nk - 1)
        def _():
            finish(acc[...])

    in_specs = [pl.BlockSpec((tm, tk), lambda i, j, kk: (i, kk)),
                pl.BlockSpec((tk, tn), lambda i, j, kk: (kk, nb0 + j))]
    args = [a, b]
    if res is not None:
        in_specs.append(pl.BlockSpec((tm, tn), lambda i, j, kk: (i, j)))
        args.append(res)
    out, got = _pcall(
        kern, grid=(m // tm, n // tn, nk), in_specs=in_specs,
        out_specs=[pl.BlockSpec((tm, tn), lambda i, j, kk: (i, j))],
        out_shape=[jax.ShapeDtypeStruct((m, n), out_dtype)],
        scratch=[pltpu.VMEM((tm, tn) if nk > 1 else (8, 128), F32)],
        sem=("parallel", "parallel", "arbitrary"), name=name, args=args, exch=exch)
    return out[0] if exch is None else (out[0], got)


def _mm_nt(a, b, *, k0=0, kw=None, out_dtype=F32, add=None, tm=640, tn=None, tk=512, name="mm_nt"):
    m = a.shape[0]
    kw = a.shape[1] if kw is None else kw
    nn = b.shape[0]
    tm = _row_tile(m, tm)
    tn = nn if tn is None else tn
    assert kw % tk == 0 and k0 % tk == 0 and nn % tn == 0 and a.shape[1] == kw
    nk = kw // tk
    kb0 = k0 // tk

    def kern(*refs):
        if add is None:
            a_ref, b_ref, o_ref, acc = refs
        else:
            a_ref, b_ref, d_ref, o_ref, acc = refs
        kk = pl.program_id(2)

        prod = _dot_nt(a_ref[...].astype(BF16), b_ref[...].astype(BF16))

        @pl.when(kk == 0)
        def _():
            acc[...] = prod

        @pl.when(kk > 0)
        def _():
            acc[...] += prod

        @pl.when(kk == nk - 1)
        def _():
            if add is None:
                o_ref[...] = acc[...].astype(out_dtype)
            else:
                o_ref[...] = (acc[...] + d_ref[...]).astype(out_dtype)

    in_specs = [pl.BlockSpec((tm, tk), lambda i, j, kk: (i, kk)),
                pl.BlockSpec((tn, tk), lambda i, j, kk: (j, kb0 + kk))]
    args = [a, b]
    if add is not None:
        in_specs.append(pl.BlockSpec((tm, tn), lambda i, j, kk: (i, j)))
        args.append(add)
    return pl.pallas_call(
        kern, grid=(m // tm, nn // tn, nk), in_specs=in_specs,
        out_specs=pl.BlockSpec((tm, tn), lambda i, j, kk: (i, j)),
        out_shape=jax.ShapeDtypeStruct((m, nn), out_dtype),
        scratch_shapes=[pltpu.VMEM((tm, tn), F32)],
        compiler_params=_cparams("parallel", "parallel", "arbitrary"), name=name)(*args)


def _mm_tn(a, b, *, out_dtype=BF16, tm=640, tk=None, tn=512, name="mm_tn"):
    m, k = a.shape
    n = b.shape[1]
    tm = _row_tile(m, tm)
    tk = k if tk is None else tk
    assert k % tk == 0 and n % tn == 0
    nm = m // tm

    def kern(a_ref, b_ref, o_ref, acc):
        mm = pl.program_id(2)

        prod = _dot_tn(a_ref[...].astype(BF16), b_ref[...].astype(BF16))

        @pl.when(mm == 0)
        def _():
            acc[...] = prod

        @pl.when(mm > 0)
        def _():
            acc[...] += prod

        @pl.when(mm == nm - 1)
        def _():
            o_ref[...] = acc[...].astype(out_dtype)

    return pl.pallas_call(
        kern, grid=(k // tk, n // tn, nm),
        in_specs=[pl.BlockSpec((tm, tk), lambda i, j, mm: (mm, i)),
                  pl.BlockSpec((tm, tn), lambda i, j, mm: (mm, j))],
        out_specs=pl.BlockSpec((tk, tn), lambda i, j, mm: (i, j)),
        out_shape=jax.ShapeDtypeStruct((k, n), out_dtype),
        scratch_shapes=[pltpu.VMEM((tk, tn), F32)],
        compiler_params=_cparams("parallel", "parallel", "arbitrary"), name=name)(a, b)


def _rmsnorm_fwd(h, g, tr=640):
    lp = h.shape[0]
    tr = _row_tile(lp, tr)

    def kern(h_ref, g_ref, o_ref):
        x = h_ref[...]
        r = lax.rsqrt(jnp.mean(x * x, axis=-1, keepdims=True) + EPS)
        o_ref[...] = (x * r * g_ref[...]).astype(BF16)

    return pl.pallas_call(
        kern, grid=(lp // tr,),
        in_specs=[pl.BlockSpec((tr, D), lambda i: (i, 0)), pl.BlockSpec((1, D), lambda i: (0, 0))],
        out_specs=pl.BlockSpec((tr, D), lambda i: (i, 0)),
        out_shape=jax.ShapeDtypeStruct((lp, D), BF16),
        compiler_params=_cparams("parallel"), name="rmsnorm_fwd")(h, g)


def _rmsnorm_bwd(h, g, dxn, dres, tr=640):
    lp = h.shape[0]
    tr = _row_tile(lp, tr)

    def kern(h_ref, g_ref, dxn_ref, dres_ref, dh_ref, dg_ref):
        i = pl.program_id(0)
        x = h_ref[...]
        r = lax.rsqrt(jnp.mean(x * x, axis=-1, keepdims=True) + EPS)
        xhat = x * r
        dy = jnp.where(_valid_rows(i * tr, tr), dxn_ref[...], 0.0)

        @pl.when(i == 0)
        def _():
            dg_ref[...] = jnp.zeros_like(dg_ref)

        dg_ref[...] += jnp.sum(dy * xhat, axis=0, keepdims=True)
        dxh = dy * g_ref[...]
        dh_ref[...] = dres_ref[...] + r * (dxh - xhat * jnp.mean(dxh * xhat, axis=-1, keepdims=True))

    row = pl.BlockSpec((tr, D), lambda i: (i, 0))
    vec = pl.BlockSpec((1, D), lambda i: (0, 0))
    return pl.pallas_call(
        kern, grid=(lp // tr,), in_specs=[row, vec, row, row], out_specs=[row, vec],
        out_shape=[jax.ShapeDtypeStruct((lp, D), F32), jax.ShapeDtypeStruct((1, D), F32)],
        compiler_params=_cparams("arbitrary"), name="rmsnorm_bwd")(h, g, dxn, dres)


def _shift_down(xe, k):
    return xe if k == 0 else pltpu.roll(xe, k, 0)


def _shift_up(xe, k):
    return xe if k == 0 else pltpu.roll(xe, xe.shape[0] - k, 0)


def _conv_ext(xe, w_ref):
    return w_ref[2:3, :] * xe + w_ref[1:2, :] * _shift_down(xe, 1) + w_ref[0:1, :] * _shift_down(xe, 2)


def _halo_specs(tr, width, col_of, nrows, rows_first):
    r8 = tr // 8
    last8 = nrows // 8 - 1
    if rows_first:
        prev = pl.BlockSpec((8, width), lambda i, j: (jnp.maximum(i * r8 - 1, 0), col_of(j)))
        nxt = pl.BlockSpec((8, width), lambda i, j: (jnp.minimum((i + 1) * r8, last8), col_of(j)))
    else:
        prev = pl.BlockSpec((8, width), lambda j, i: (jnp.maximum(i * r8 - 1, 0), col_of(j)))
        nxt = pl.BlockSpec((8, width), lambda j, i: (jnp.minimum((i + 1) * r8, last8), col_of(j)))
    return prev, nxt


def _convb_fwd(proj_r, conv_w, tr=640):
    lp = proj_r.shape[0]
    tr = _row_tile(lp, tr)
    tc = CONV_TC
    gw = 3 * tc

    def kern(g_ref, gp_ref, w_ref, o_ref):
        i = pl.program_id(0)
        g = g_ref[...]
        p = g[:, tc:2 * tc] * g[:, 2 * tc:]
        gp = gp_ref[...]
        pp = jnp.where(i > 0, gp[:, tc:2 * tc] * gp[:, 2 * tc:], 0.0)
        y = _conv_ext(jnp.concatenate([pp, p], axis=0), w_ref)[8:]
        o_ref[...] = (g[:, :tc] * y).astype(BF16)

    prev, _ = _halo_specs(tr, gw, lambda j: R_CONV_BLK0 + j, lp, True)
    return pl.pallas_call(
        kern, grid=(lp // tr, CONV_CH // tc),
        in_specs=[pl.BlockSpec((tr, gw), lambda i, j: (i, R_CONV_BLK0 + j)), prev,
                  pl.BlockSpec((3, tc), lambda i, j: (0, j))],
        out_specs=pl.BlockSpec((tr, tc), lambda i, j: (i, j)),
        out_shape=jax.ShapeDtypeStruct((lp, CONV_CH), BF16),
        compiler_params=_cparams("parallel", "parallel"), name="convb_fwd")(proj_r, proj_r, conv_w)


def _convb_bwd(proj_r, dzb, conv_w, dproj, tr=640):
    lp = proj_r.shape[0]
    tr = _row_tile(lp, tr)
    nr = lp // tr
    tc = CONV_TC
    gw = 3 * tc

    def kern(g_ref, gp_ref, gn_ref, dz_ref, dzn_ref, w_ref, dp_any, dg_ref, dw_ref):
        del dp_any
        i = pl.program_id(1)
        g = g_ref[...]
        b, c, hh = g[:, :tc], g[:, tc:2 * tc], g[:, 2 * tc:]
        p = c * hh
        gp = gp_ref[...]
        pp = jnp.where(i > 0, gp[:, tc:2 * tc] * gp[:, 2 * tc:], 0.0)
        pe = jnp.concatenate([pp, p], axis=0)
        s1 = _shift_down(pe, 1)[8:]
        s2 = _shift_down(pe, 2)[8:]
        y = w_ref[2:3, :] * p + w_ref[1:2, :] * s1 + w_ref[0:1, :] * s2
        dz = dz_ref[...]
        dy = dz * b
        dyn = jnp.where(i < nr - 1, dzn_ref[...] * gn_ref[...][:, :tc], 0.0)
        dye = jnp.concatenate([dy, dyn], axis=0)
        dp = (w_ref[2:3, :] * dy + w_ref[1:2, :] * _shift_up(dye, 1)[:tr]
              + w_ref[0:1, :] * _shift_up(dye, 2)[:tr])
        valid = _valid_rows(i * tr, tr)
        dg_ref[...] = jnp.where(valid, jnp.concatenate([dz * y, dp * hh, dp * c], axis=1), 0.0).astype(BF16)

        @pl.when(i == 0)
        def _():
            dw_ref[...] = jnp.zeros_like(dw_ref)

        dw_ref[0:1, :] += jnp.sum(dy * s2, axis=0, keepdims=True)
        dw_ref[1:2, :] += jnp.sum(dy * s1, axis=0, keepdims=True)
        dw_ref[2:3, :] += jnp.sum(dy * p, axis=0, keepdims=True)

    gprev, gnext = _halo_specs(tr, gw, lambda j: R_CONV_BLK0 + j, lp, False)
    _, dznext = _halo_specs(tr, tc, lambda j: j, lp, False)
    return pl.pallas_call(
        kern, grid=(CONV_CH // tc, nr),
        in_specs=[pl.BlockSpec((tr, gw), lambda j, i: (i, R_CONV_BLK0 + j)), gprev, gnext,
                  pl.BlockSpec((tr, tc), lambda j, i: (i, j)), dznext,
                  pl.BlockSpec((3, tc), lambda j, i: (0, j)),
                  pl.BlockSpec(memory_space=pl.ANY)],
        out_specs=[pl.BlockSpec((tr, gw), lambda j, i: (i, F_CONV_BLK0 + j)),
                   pl.BlockSpec((3, tc), lambda j, i: (0, j))],
        out_shape=[jax.ShapeDtypeStruct(dproj.shape, BF16), jax.ShapeDtypeStruct((3, CONV_CH), F32)],
        input_output_aliases={6: 0},
        compiler_params=_cparams("parallel", "arbitrary"), name="convb_bwd",
    )(proj_r, proj_r, proj_r, dzb, dzb, conv_w, dproj)


MLP_TC = 256


def _mlp_gate_fwd(z, w, tr=640):
    lp = z.shape[0]
    tr = _row_tile(lp, tr)
    tc = 512
    nc = D_FF // tc

    def kern(zg_ref, zgp_ref, zu_ref, zup_ref, wg_ref, wu_ref, o_ref):
        i = pl.program_id(0)
        zge = jnp.concatenate([jnp.where(i > 0, zgp_ref[...], 0.0), zg_ref[...]], axis=0)
        zue = jnp.concatenate([jnp.where(i > 0, zup_ref[...], 0.0), zu_ref[...]], axis=0)
        ug = _conv_ext(zge, wg_ref)[8:]
        uu = _conv_ext(zue, wu_ref)[8:]
        o_ref[...] = (ug * _sigmoid(ug) * uu).astype(BF16)

    gprev, _ = _halo_specs(tr, tc, lambda j: j, lp, True)
    uprev, _ = _halo_specs(tr, tc, lambda j: nc + j, lp, True)
    return pl.pallas_call(
        kern, grid=(lp // tr, nc),
        in_specs=[pl.BlockSpec((tr, tc), lambda i, j: (i, j)), gprev,
                  pl.BlockSpec((tr, tc), lambda i, j: (i, nc + j)), uprev,
                  pl.BlockSpec((3, tc), lambda i, j: (0, j)),
                  pl.BlockSpec((3, tc), lambda i, j: (0, nc + j))],
        out_specs=pl.BlockSpec((tr, tc), lambda i, j: (i, j)),
        out_shape=jax.ShapeDtypeStruct((lp, D_FF), BF16),
        compiler_params=_cparams("parallel", "parallel"), name="mlp_gate_fwd")(z, z, z, z, w, w)


def _mlp_gate_bwd(z, da, w, tr=640):
    lp = z.shape[0]
    tr = _row_tile(lp, tr)
    nr = lp // tr
    tc = MLP_TC
    nc = D_FF // tc

    def kern(zg_ref, zgp_ref, zgn_ref, zu_ref, zup_ref, zun_ref, da_ref, dan_ref, wg_ref, wu_ref,
             dzg_ref, dzu_ref, dwg_ref, dwu_ref):
        i = pl.program_id(1)
        first, last = i == 0, i == nr - 1

        def ext(m_ref, p_ref, n_ref):
            return jnp.concatenate([jnp.where(first, 0.0, p_ref[...]), m_ref[...],
                                    jnp.where(last, 0.0, n_ref[...])], axis=0)

        zge, zue = ext(zg_ref, zgp_ref, zgn_ref), ext(zu_ref, zup_ref, zun_ref)
        ug = _conv_ext(zge, wg_ref)[8:]
        uu = _conv_ext(zue, wu_ref)[8:]
        dae = jnp.concatenate([da_ref[...], jnp.where(last, 0.0, dan_ref[...])], axis=0)
        sg = _sigmoid(ug)
        dug = dae * uu * (sg * (1.0 + ug * (1.0 - sg)))
        duu = dae * (ug * sg)
        valid = _valid_rows(i * tr, tr)

        @pl.when(first)
        def _():
            dwg_ref[...] = jnp.zeros_like(dwg_ref)
            dwu_ref[...] = jnp.zeros_like(dwu_ref)

        for du, ze, w_ref, dz_ref, dw_ref in ((dug, zge, wg_ref, dzg_ref, dwg_ref),
                                              (duu, zue, wu_ref, dzu_ref, dwu_ref)):
            dz = (w_ref[2:3, :] * du + w_ref[1:2, :] * _shift_up(du, 1) + w_ref[0:1, :] * _shift_up(du, 2))[:tr]
            dz_ref[...] = jnp.where(valid, dz, 0.0).astype(BF16)
            dum = du[:tr]
            for kk in range(3):
                dw_ref[kk:kk + 1, :] += jnp.sum(dum * _shift_down(ze, 2 - kk)[8:8 + tr], axis=0, keepdims=True)

    gprev, gnext = _halo_specs(tr, tc, lambda j: j, lp, False)
    uprev, unext = _halo_specs(tr, tc, lambda j: nc + j, lp, False)
    main = pl.BlockSpec((tr, tc), lambda j, i: (i, j))
    wspec = pl.BlockSpec((3, tc), lambda j, i: (0, j))
    return pl.pallas_call(
        kern, grid=(nc, nr),
        in_specs=[main, gprev, gnext, pl.BlockSpec((tr, tc), lambda j, i: (i, nc + j)), uprev, unext,
                  main, gnext, wspec, pl.BlockSpec((3, tc), lambda j, i: (0, nc + j))],
        out_specs=[main, main, wspec, wspec],
        out_shape=[jax.ShapeDtypeStruct((lp, D_FF), BF16), jax.ShapeDtypeStruct((lp, D_FF), BF16),
                   jax.ShapeDtypeStruct((3, D_FF), F32), jax.ShapeDtypeStruct((3, D_FF), F32)],
        compiler_params=_cparams("parallel", "arbitrary"), name="mlp_gate_bwd",
    )(z, z, z, z, z, z, da, da, w, w)


def _tri(n, lower):
    r = lax.broadcasted_iota(jnp.int32, (n, n), 0)
    c = lax.broadcasted_iota(jnp.int32, (n, n), 1)
    return jnp.where((c <= r) if lower else (c >= r), 1.0, 0.0).astype(F32)


def _dot_exact(a, b):
    return jnp.dot(a, b, preferred_element_type=F32, precision=lax.Precision.HIGHEST)


def _fox_gate_fwd(proj_r, bf128):
    lp = proj_r.shape[0]
    nb = lp // BLK

    def kern(s_ref, b_ref, c_ref):
        tri = _tri(BLK, True)

        def body(i, carry):
            rows = pl.ds(pl.multiple_of(i * BLK, BLK), BLK)
            lf = jnp.where(_valid_rows(i * BLK, BLK), _log_sigmoid(s_ref[rows, :] + b_ref[...]), 0.0)
            cs = _dot_exact(tri, lf) + carry
            c_ref[rows, :] = cs
            return cs[BLK - 1:BLK, :]

        lax.fori_loop(0, nb, body, jnp.zeros((1, BLK), F32))

    return pl.pallas_call(
        kern, grid=(1,),
        in_specs=[pl.BlockSpec((lp, BLK), lambda i: (0, R_SMALL_BLK128)), pl.BlockSpec((1, BLK), lambda i: (0, 0))],
        out_specs=pl.BlockSpec((lp, BLK), lambda i: (0, 0)),
        out_shape=jax.ShapeDtypeStruct((lp, BLK), F32),
        compiler_params=_cparams("arbitrary"), name="fox_gate_fwd")(proj_r, bf128)


def _fox_gate_bwd(proj_r, dc, bf128):
    lp = proj_r.shape[0]
    nb = lp // BLK

    def kern(s_ref, dc_ref, b_ref, dfa_ref, dbf_ref):
        tri = _tri(BLK, False)

        dbf_ref[...] = jnp.zeros_like(dbf_ref)

        def body(ii, run):
            i = nb - 1 - ii
            rows = pl.ds(pl.multiple_of(i * BLK, BLK), BLK)
            dcb = dc_ref[rows, :]
            suf = _dot_exact(tri, dcb) + run
            dfa = jnp.where(_valid_rows(i * BLK, BLK), suf * _sigmoid(-(s_ref[rows, :] + b_ref[...])), 0.0)
            dfa_ref[rows, :] = dfa
            dbf_ref[...] += jnp.sum(dfa, axis=0, keepdims=True)
            return run + jnp.sum(dcb, axis=0, keepdims=True)

        lax.fori_loop(0, nb, body, jnp.zeros((1, BLK), F32))

    return pl.pallas_call(
        kern, grid=(1,),
        in_specs=[pl.BlockSpec((lp, BLK), lambda i: (0, R_SMALL_BLK128)), pl.BlockSpec((lp, BLK), lambda i: (0, 0)),
                  pl.BlockSpec((1, BLK), lambda i: (0, 0))],
        out_specs=[pl.BlockSpec((lp, BLK), lambda i: (0, 0)), pl.BlockSpec((1, BLK), lambda i: (0, 0))],
        out_shape=[jax.ShapeDtypeStruct((lp, BLK), F32), jax.ShapeDtypeStruct((1, BLK), F32)],
        compiler_params=_cparams("arbitrary"), name="fox_gate_bwd")(proj_r, dc, bf128)


LOG2E = 1.4426950408889634
KEY_PAD_BIAS = 1e30


def _attn_logits2(q, k, ck, diag):
    t = _dot_nt(q, k) * (LOG2E * FOX_DH ** -0.5) - ck * LOG2E
    if diag:
        r = lax.broadcasted_iota(jnp.int32, t.shape, 0)
        c = lax.broadcasted_iota(jnp.int32, t.shape, 1)
        t = jnp.where(c <= r, t, NEG)
    return t


def _on_blocks(i, j, step):
    pl.when(j < i)(functools.partial(step, False))
    pl.when(j == i)(functools.partial(step, True))


def _attn_fwd(proj_a, cq, ck, tq=640, exch=None):
    lp = proj_a.shape[0]
    tq = _row_tile(lp, tq)
    tk = tq
    nq = lp // tq

    def kern(q_ref, kv_ref, cq_ref, ck_ref, o_ref, lse_ref, m_sc, l_sc, acc):
        i, j = pl.program_id(1), pl.program_id(2)

        @pl.when(j == 0)
        def _():
            m_sc[...] = jnp.full_like(m_sc, -jnp.inf)
            l_sc[...] = jnp.zeros_like(l_sc)
            acc[...] = jnp.zeros_like(acc)

        def step(diag):
            kv = kv_ref[...]
            cq2 = cq_ref[...] * LOG2E
            t = _attn_logits2(q_ref[...], kv[:, :FOX_DH], ck_ref[...], diag)
            m_old = m_sc[...]
            m_new = jnp.maximum(m_old, jnp.max(t, axis=-1, keepdims=True) + cq2)
            p = jnp.exp2(t + (cq2 - m_new))
            alpha = jnp.exp2(m_old - m_new)
            l_sc[...] = alpha * l_sc[...] + jnp.sum(p, axis=-1, keepdims=True)
            acc[...] = alpha * acc[...] + _dot(p.astype(BF16), kv[:, FOX_DH:])
            m_sc[...] = m_new

        _on_blocks(i, j, step)

        @pl.when(j == nq - 1)
        def _():
            valid = _valid_rows(i * tq, tq)
            o_ref[...] = jnp.where(valid, acc[...] / l_sc[...], 0.0).astype(BF16)
            lse_ref[...] = m_sc[...] + jnp.log(l_sc[...]) * LOG2E

    out, got = _pcall(
        kern, grid=(FOX_H, nq, nq),
        in_specs=[pl.BlockSpec((tq, FOX_DH), lambda h, i, j: (i, h)),
                  pl.BlockSpec((tk, 2 * FOX_DH), lambda h, i, j: (jnp.minimum(j, i), KV0 // 256 + h)),
                  pl.BlockSpec((None, tq, 1), lambda h, i, j: (h, i, 0)),
                  pl.BlockSpec((None, 1, tk), lambda h, i, j: (h, 0, jnp.minimum(j, i)))],
        out_specs=[pl.BlockSpec((tq, FOX_DH), lambda h, i, j: (i, h)),
                   pl.BlockSpec((None, tq, 1), lambda h, i, j: (h, i, 0))],
        out_shape=[jax.ShapeDtypeStruct((lp, FOX_W), BF16), jax.ShapeDtypeStruct((FOX_H, lp, 1), F32)],
        scratch=[pltpu.VMEM((tq, 1), F32), pltpu.VMEM((tq, 1), F32), pltpu.VMEM((tq, FOX_DH), F32)],
        sem=("parallel", "parallel", "arbitrary"), name="attn_fwd", args=(proj_a, proj_a, cq, ck), exch=exch)
    return out[0], out[1], got


def _attn_bwd_dq(proj_a, do, o, lse, cq, ck, dproj, tq=640, exch=None):
    lp = proj_a.shape[0]
    tq = _row_tile(lp, tq)
    tk = tq
    nq = lp // tq

    def kern(q_ref, kv_ref, do_ref, o_ref, lse_ref, cq_ref, ck_ref, dp_any, dq_ref, dcq_ref, dl_ref, dq_acc):
        del dp_any
        i, j = pl.program_id(1), pl.program_id(2)

        @pl.when(j == 0)
        def _():
            dl_ref[...] = jnp.sum(do_ref[...].astype(F32) * o_ref[...].astype(F32), axis=-1, keepdims=True)
            dcq_ref[...] = jnp.zeros_like(dcq_ref)
            dq_acc[...] = jnp.zeros_like(dq_acc)

        def step(diag):
            kv = kv_ref[...]
            t = _attn_logits2(q_ref[...], kv[:, :FOX_DH], ck_ref[...], diag)
            p = jnp.exp2(t + (cq_ref[...] * LOG2E - lse_ref[...]))
            ds = p * (_dot_nt(do_ref[...], kv[:, FOX_DH:]) - dl_ref[...])
            dcq_ref[...] += jnp.sum(ds, axis=-1, keepdims=True)
            dq_acc[...] += _dot(ds.astype(BF16), kv[:, :FOX_DH])

        _on_blocks(i, j, step)

        @pl.when(j == nq - 1)
        def _():
            dq_ref[...] = (dq_acc[...] * (FOX_DH ** -0.5)).astype(BF16)

    qspec = pl.BlockSpec((tq, FOX_DH), lambda h, i, j: (i, h))
    col = pl.BlockSpec((None, tq, 1), lambda h, i, j: (h, i, 0))
    out, got = _pcall(
        kern, grid=(FOX_H, nq, nq),
        in_specs=[qspec, pl.BlockSpec((tk, 2 * FOX_DH), lambda h, i, j: (jnp.minimum(j, i), KV0 // 256 + h)),
                  qspec, qspec, col, col,
                  pl.BlockSpec((None, 1, tk), lambda h, i, j: (h, 0, jnp.minimum(j, i))),
                  pl.BlockSpec(memory_space=pl.ANY)],
        out_specs=[qspec, col, col],
        out_shape=[jax.ShapeDtypeStruct(dproj.shape, BF16), jax.ShapeDtypeStruct((FOX_H, lp, 1), F32),
                   jax.ShapeDtypeStruct((FOX_H, lp, 1), F32)],
        scratch=[pltpu.VMEM((tq, FOX_DH), F32)], aliases={7: 0},
        sem=("parallel", "parallel", "arbitrary"), name="attn_bwd_dq",
        args=(proj_a, proj_a, do, o, lse, cq, ck, dproj), exch=exch)
    return out[0], out[1], out[2], got


def _attn_bwd_dkv(proj_a, do, lse, delta, cq, ck, dproj, tq=640, exch=None):
    lp = proj_a.shape[0]
    tq = _row_tile(lp, tq)
    tk = tq
    nq = lp // tq

    def kern(q_ref, kv_ref, do_ref, lse_ref, dl_ref, cq_ref, ck_ref, dp_any, dkv_ref, dck_ref, dk_acc, dv_acc):
        del dp_any
        j, i = pl.program_id(1), pl.program_id(2)

        @pl.when(i == 0)
        def _():
            dck_ref[...] = jnp.zeros_like(dck_ref)
            dk_acc[...] = jnp.zeros_like(dk_acc)
            dv_acc[...] = jnp.zeros_like(dv_acc)

        def step(diag):
            kv = kv_ref[...]
            t = _attn_logits2(q_ref[...], kv[:, :FOX_DH], ck_ref[...], diag)
            p = jnp.exp2(t + (cq_ref[...] * LOG2E - lse_ref[...]))
            dv_acc[...] += _dot_tn(p.astype(BF16), do_ref[...])
            ds = p * (_dot_nt(do_ref[...], kv[:, FOX_DH:]) - dl_ref[...])
            dck_ref[...] -= jnp.sum(ds, axis=0, keepdims=True)
            dk_acc[...] += _dot_tn(ds.astype(BF16), q_ref[...])

        _on_blocks(i, j, step)

        @pl.when(i == nq - 1)
        def _():
            dkv_ref[...] = jnp.concatenate([dk_acc[...] * (FOX_DH ** -0.5), dv_acc[...]], axis=1).astype(BF16)

    qspec = pl.BlockSpec((tq, FOX_DH), lambda h, j, i: (jnp.maximum(i, j), h))
    col = pl.BlockSpec((None, tq, 1), lambda h, j, i: (h, jnp.maximum(i, j), 0))
    kvspec = pl.BlockSpec((tk, 2 * FOX_DH), lambda h, j, i: (j, KV0 // 256 + h))
    rowspec = pl.BlockSpec((None, 1, tk), lambda h, j, i: (h, 0, j))
    out, got = _pcall(
        kern, grid=(FOX_H, nq, nq),
        in_specs=[qspec, kvspec, qspec, col, col, col, rowspec, pl.BlockSpec(memory_space=pl.ANY)],
        out_specs=[kvspec, rowspec],
        out_shape=[jax.ShapeDtypeStruct(dproj.shape, BF16), jax.ShapeDtypeStruct((FOX_H, 1, lp), F32)],
        scratch=[pltpu.VMEM((tk, FOX_DH), F32), pltpu.VMEM((tk, FOX_DH), F32)], aliases={7: 0},
        sem=("parallel", "parallel", "arbitrary"), name="attn_bwd_dkv",
        args=(proj_a, proj_a, do, lse, delta, cq, ck, dproj), exch=exch)
    return out[0], out[1], got


def _gla_gate_fwd(proj_r, wg2p, bg, tr=640):
    lp = proj_r.shape[0]
    tr = _row_tile(lp, tr)
    w = GLA_H * GLA_DK

    def kern(s_ref, w_ref, b_ref, o_ref):
        zg = _dot(s_ref[...].astype(BF16), w_ref[...]) + b_ref[...]
        o_ref[...] = jnp.where(_valid_rows(pl.program_id(0) * tr, tr), _log_sigmoid(zg) * (1.0 / GLA_TAU), 0.0)

    return pl.pallas_call(
        kern, grid=(lp // tr,),
        in_specs=[pl.BlockSpec((tr, BLK), lambda i: (i, R_SMALL_BLK128)), pl.BlockSpec((BLK, w), lambda i: (0, 0)),
                  pl.BlockSpec((1, w), lambda i: (0, 0))],
        out_specs=pl.BlockSpec((tr, w), lambda i: (i, 0)),
        out_shape=jax.ShapeDtypeStruct((lp, w), F32),
        compiler_params=_cparams("parallel"), name="gla_gate_fwd")(proj_r, wg2p, bg)


def _gla_chunk(grp, g):
    q = grp[:, :GLA_DK] * (GLA_DK ** -0.5)
    k = grp[:, GLA_DK:2 * GLA_DK]
    v = grp[:, 2 * GLA_DK:2 * GLA_DK + GLA_DV]
    r = grp[:, 2 * GLA_DK + GLA_DV:]
    b = _dot_exact(_tri(BLK, True), g)
    bl = b[BLK - 1:BLK, :]
    eb = jnp.exp(b)
    enb = jnp.exp(-b)
    ebl = jnp.exp(bl - b)
    qe, ke, kd = q * eb, k * enb, k * ebl
    causal = lax.broadcasted_iota(jnp.int32, (BLK, BLK), 1) <= lax.broadcasted_iota(jnp.int32, (BLK, BLK), 0)
    att = jnp.where(causal, _dot_nt(qe.astype(BF16), ke.astype(BF16)), 0.0)
    return q, k, v, r, bl, eb, enb, ebl, qe, ke, kd, causal, att


def _gla_fwd(proj_r, logg, gn):
    lp = proj_r.shape[0]
    nc = lp // BLK
    wv = GLA_H * GLA_DV

    def kern(grp_ref, g_ref, gn_ref, o_ref, zc_ref, st_ref, st):
        c = pl.program_id(1)

        @pl.when(c == 0)
        def _():
            st[...] = jnp.zeros_like(st)

        q, k, v, r, bl, eb, enb, ebl, qe, ke, kd, causal, att = _gla_chunk(grp_ref[...], g_ref[...])
        s_t = st[...]
        st_ref[...] = s_t
        vb = v.astype(BF16)
        o = _dot(att.astype(BF16), vb) + _dot_nt(qe.astype(BF16), s_t.astype(BF16))
        st[...] = s_t * jnp.exp(bl) + _dot_tn(vb, kd.astype(BF16))
        o_ref[...] = o
        rstd = lax.rsqrt(jnp.mean(o * o, axis=-1, keepdims=True) + EPS)
        zc_ref[...] = (r * _sigmoid(r) * (o * rstd * gn_ref[...])).astype(BF16)

    vspec = pl.BlockSpec((BLK, GLA_DV), lambda h, c: (c, h))
    return pl.pallas_call(
        kern, grid=(GLA_H, nc),
        in_specs=[pl.BlockSpec((BLK, GLA_GRP), lambda h, c: (c, R_GLA_BLK0 + h)),
                  pl.BlockSpec((BLK, GLA_DK), lambda h, c: (c, h)),
                  pl.BlockSpec((1, GLA_DV), lambda h, c: (0, h))],
        out_specs=[vspec, vspec, pl.BlockSpec((None, None, GLA_DV, GLA_DK), lambda h, c: (h, c, 0, 0))],
        out_shape=[jax.ShapeDtypeStruct((lp, wv), F32), jax.ShapeDtypeStruct((lp, wv), BF16),
                   jax.ShapeDtypeStruct((GLA_H, nc, GLA_DV, GLA_DK), F32)],
        scratch_shapes=[pltpu.VMEM((GLA_DV, GLA_DK), F32)],
        compiler_params=_cparams("parallel", "arbitrary"), name="gla_fwd")(proj_r, logg, gn)


def _gla_bwd(proj_r, logg, st_all, o_all, dzc, gn, dproj):
    lp = proj_r.shape[0]
    nc = lp // BLK

    def kern(grp_ref, g_ref, st_ref, o_ref, dzc_ref, gn_ref, dp_any, dgrp_ref, dlg_ref, dgn_ref, dst):
        del dp_any
        cc = pl.program_id(1)

        @pl.when(cc == 0)
        def _():
            dst[...] = jnp.zeros_like(dst)
            dgn_ref[...] = jnp.zeros_like(dgn_ref)

        q, k, v, r, bl, eb, enb, ebl, qe, ke, kd, causal, att = _gla_chunk(grp_ref[...], g_ref[...])
        s_t = st_ref[...]
        d_st = dst[...]
        o = o_ref[...]
        dzc_v = dzc_ref[...]
        rstd = lax.rsqrt(jnp.mean(o * o, axis=-1, keepdims=True) + EPS)
        xhat = o * rstd
        sr = _sigmoid(r)
        dr = dzc_v * (xhat * gn_ref[...]) * (sr * (1.0 + r * (1.0 - sr)))
        docn = dzc_v * (r * sr)
        dgn_ref[...] += jnp.sum(docn * xhat, axis=0, keepdims=True)
        dxh = docn * gn_ref[...]
        do = rstd * (dxh - xhat * jnp.mean(dxh * xhat, axis=-1, keepdims=True))
        dob, vb = do.astype(BF16), v.astype(BF16)
        qeb, keb, kdb = qe.astype(BF16), ke.astype(BF16), kd.astype(BF16)
        datt = jnp.where(causal, _dot_nt(dob, vb), 0.0).astype(BF16)
        dv = _dot_tn(att.astype(BF16), dob) + _dot_nt(kdb, d_st.astype(BF16))
        dqe = _dot(datt, keb) + _dot(dob, s_t.astype(BF16))
        dke = _dot_tn(datt, qeb)
        dkd = _dot(vb, d_st.astype(BF16))
        dq = dqe * eb * (GLA_DK ** -0.5)
        dk = dke * enb + dkd * ebl
        kd_dkd = dkd * kd
        db = dqe * qe - dke * ke - kd_dkd
        db_last = (jnp.sum(kd_dkd, axis=0, keepdims=True)
                   + jnp.exp(bl) * jnp.sum(s_t * d_st, axis=0, keepdims=True))
        dlg_ref[...] = _dot_exact(_tri(BLK, False), db) + db_last
        dst[...] = d_st * jnp.exp(bl) + _dot_tn(dob, qeb)
        dgrp_ref[...] = jnp.concatenate([dq, dk, dv, dr], axis=1).astype(BF16)

    rev = lambda c: nc - 1 - c
    vspec = pl.BlockSpec((BLK, GLA_DV), lambda h, c: (rev(c), h))
    return pl.pallas_call(
        kern, grid=(GLA_H, nc),
        in_specs=[pl.BlockSpec((BLK, GLA_GRP), lambda h, c: (rev(c), R_GLA_BLK0 + h)),
                  pl.BlockSpec((BLK, GLA_DK), lambda h, c: (rev(c), h)),
                  pl.BlockSpec((None, None, GLA_DV, GLA_DK), lambda h, c: (h, rev(c), 0, 0)),
                  vspec, vspec, pl.BlockSpec((1, GLA_DV), lambda h, c: (0, h)),
                  pl.BlockSpec(memory_space=pl.ANY)],
        out_specs=[pl.BlockSpec((BLK, GLA_GRP), lambda h, c: (rev(c), F_GLA_BLK0 + h)),
                   pl.BlockSpec((BLK, GLA_DK), lambda h, c: (rev(c), h)),
                   pl.BlockSpec((1, GLA_DV), lambda h, c: (0, h))],
        out_shape=[jax.ShapeDtypeStruct(dproj.shape, BF16), jax.ShapeDtypeStruct((lp, GLA_H * GLA_DK), F32),
                   jax.ShapeDtypeStruct((1, GLA_H * GLA_DV), F32)],
        scratch_shapes=[pltpu.VMEM((GLA_DV, GLA_DK), F32)],
        input_output_aliases={6: 0},
        compiler_params=_cparams("parallel", "arbitrary"), name="gla_bwd",
    )(proj_r, logg, st_all, o_all, dzc, gn, dproj)


def _small_bwd(proj_r, dlogg, wg2p, wg2pt, bg, dfa, dproj, tr=640):
    lp = proj_r.shape[0]
    tr = _row_tile(lp, tr)
    w = GLA_H * GLA_DK

    def kern(s_ref, dlg_ref, w_ref, wt_ref, b_ref, dfa_ref, dp_any, ds_ref, dbg_ref, dw_ref):
        del dp_any
        i = pl.program_id(0)
        sb = s_ref[...].astype(BF16)
        zg = _dot(sb, w_ref[...]) + b_ref[...]
        dzg = jnp.where(_valid_rows(i * tr, tr), dlg_ref[...] * (1.0 / GLA_TAU) * _sigmoid(-zg), 0.0)

        @pl.when(i == 0)
        def _():
            dbg_ref[...] = jnp.zeros_like(dbg_ref)
            dw_ref[...] = jnp.zeros_like(dw_ref)

        dbg_ref[...] += jnp.sum(dzg, axis=0, keepdims=True)
        dzb = dzg.astype(BF16)
        dw_ref[...] += _dot_tn(sb, dzb)
        dsm = _dot(dzb, wt_ref[...]) + dfa_ref[...]
        ds_ref[...] = jnp.concatenate([dsm, jnp.zeros((tr, SMALL_W - BLK), F32)], axis=1).astype(BF16)

    return pl.pallas_call(
        kern, grid=(lp // tr,),
        in_specs=[pl.BlockSpec((tr, BLK), lambda i: (i, R_SMALL_BLK128)), pl.BlockSpec((tr, w), lambda i: (i, 0)),
                  pl.BlockSpec((BLK, w), lambda i: (0, 0)), pl.BlockSpec((w, BLK), lambda i: (0, 0)),
                  pl.BlockSpec((1, w), lambda i: (0, 0)), pl.BlockSpec((tr, BLK), lambda i: (i, 0)),
                  pl.BlockSpec(memory_space=pl.ANY)],
        out_specs=[pl.BlockSpec((tr, SMALL_W), lambda i: (i, F_SMALL_BLK0)), pl.BlockSpec((1, w), lambda i: (0, 0)),
                   pl.BlockSpec((BLK, w), lambda i: (0, 0))],
        out_shape=[jax.ShapeDtypeStruct(dproj.shape, BF16), jax.ShapeDtypeStruct((1, w), F32),
                   jax.ShapeDtypeStruct((BLK, w), F32)],
        input_output_aliases={6: 0},
        compiler_params=_cparams("arbitrary"), name="small_bwd",
    )(proj_r, dlogg, wg2p, wg2pt, bg, dfa, dproj)


def _merge_fwd(proj_r, gate_b3, ya, yb, yc, tr=640):
    lp = proj_r.shape[0]
    tr = _row_tile(lp, tr)
    tn = GATE_TN

    def kern(g_ref, b_ref, ya_ref, yb_ref, yc_ref, o_ref):
        g = g_ref[...]
        mix = (_sigmoid(g[:, :tn] + b_ref[0:1, :]) * ya_ref[...]
               + _sigmoid(g[:, tn:2 * tn] + b_ref[1:2, :]) * yb_ref[...]
               + _sigmoid(g[:, 2 * tn:] + b_ref[2:3, :]) * yc_ref[...])
        o_ref[...] = mix.astype(BF16)

    y = pl.BlockSpec((tr, tn), lambda i, j: (i, j))
    return pl.pallas_call(
        kern, grid=(lp // tr, D // tn),
        in_specs=[pl.BlockSpec((tr, 3 * tn), lambda i, j: (i, R_GATE_BLK0 + j)),
                  pl.BlockSpec((3, tn), lambda i, j: (0, j)), y, y, y],
        out_specs=y, out_shape=jax.ShapeDtypeStruct((lp, D), BF16),
        compiler_params=_cparams("parallel", "parallel"), name="merge_fwd")(proj_r, gate_b3, ya, yb, yc)


def _merge_bwd(proj_r, gate_b3, ya, yb, yc, dmix, tr=640):
    lp = proj_r.shape[0]
    tr = _row_tile(lp, tr)
    tn = GATE_TN

    def kern(g_ref, b_ref, ya_ref, yb_ref, yc_ref, dm_ref, dya_ref, dyb_ref, dyc_ref, dg_ref, db_ref):
        i = pl.program_id(1)
        g = g_ref[...]
        dm = dm_ref[...]

        @pl.when(i == 0)
        def _():
            db_ref[...] = jnp.zeros_like(db_ref)

        dgs = []
        for n, (y_ref, dy_ref) in enumerate(((ya_ref, dya_ref), (yb_ref, dyb_ref), (yc_ref, dyc_ref))):
            s = _sigmoid(g[:, n * tn:(n + 1) * tn] + b_ref[n:n + 1, :])
            dy_ref[...] = (dm * s).astype(BF16)
            dgn = dm * y_ref[...] * (s * (1.0 - s))
            db_ref[n:n + 1, :] += jnp.sum(dgn, axis=0, keepdims=True)
            dgs.append(dgn)
        dg_ref[...] = jnp.concatenate(dgs, axis=1).astype(BF16)

    y = pl.BlockSpec((tr, tn), lambda j, i: (i, j))
    bspec = pl.BlockSpec((3, tn), lambda j, i: (0, j))
    return pl.pallas_call(
        kern, grid=(D // tn, lp // tr),
        in_specs=[pl.BlockSpec((tr, 3 * tn), lambda j, i: (i, R_GATE_BLK0 + j)), bspec, y, y, y, y],
        out_specs=[y, y, y, pl.BlockSpec((tr, 3 * tn), lambda j, i: (i, F_GATE_BLK0 + j)), bspec],
        out_shape=[jax.ShapeDtypeStruct((lp, D), BF16)] * 3
        + [jax.ShapeDtypeStruct((lp, NP), BF16), jax.ShapeDtypeStruct((3, D), F32)],
        compiler_params=_cparams("parallel", "arbitrary"), name="merge_bwd")(proj_r, gate_b3, ya, yb, yc, dmix)


def _final_loss(h, gf, tgt):
    lp = h.shape[0]
    nb = lp // BLK

    def kern(h_ref, g_ref, t_ref, dh_ref, dg_ref, ls_ref):
        i = pl.program_id(0)

        @pl.when(i == 0)
        def _():
            dh_ref[...] = jnp.zeros_like(dh_ref)
            dg_ref[...] = jnp.zeros_like(dg_ref)
            ls_ref[...] = jnp.zeros_like(ls_ref)

        @pl.when(i > 0)
        def _():
            x = h_ref[...]
            r = lax.rsqrt(jnp.mean(x * x, axis=-1, keepdims=True) + EPS)
            xhat = x * r
            err = xhat * g_ref[...] - t_ref[...]
            ls_ref[...] += jnp.sum(jnp.sum(err * err, axis=0, keepdims=True), axis=1, keepdims=True)
            dy = err * (1.0 / D)
            dg_ref[...] += jnp.sum(dy * xhat, axis=0, keepdims=True)
            dxh = dy * g_ref[...]
            dh_ref[...] = r * (dxh - xhat * jnp.mean(dxh * xhat, axis=-1, keepdims=True))

    row = pl.BlockSpec((BLK, D), lambda i: (i, 0))
    vec = pl.BlockSpec((1, D), lambda i: (0, 0))
    return pl.pallas_call(
        kern, grid=(nb,),
        in_specs=[row, vec, pl.BlockSpec((BLK, D), lambda i: (jnp.maximum(i - 1, 0), 0))],
        out_specs=[row, vec, pl.BlockSpec((1, 1), lambda i: (0, 0))],
        out_shape=[jax.ShapeDtypeStruct((lp, D), F32), jax.ShapeDtypeStruct((1, D), F32),
                   jax.ShapeDtypeStruct((1, 1), F32)],
        compiler_params=_cparams("arbitrary"), name="final_loss")(h, gf, tgt)


def _gate_cols(c):
    ct = c[:, :FOX_H].T
    ck = jnp.where(jnp.arange(ct.shape[1]) < PAD, KEY_PAD_BIAS, ct)
    return ct[:, :, None], ck[:, None, :]


def _run(hosts, name, ctx, fn):
    if hosts and name in hosts:
        make, done = hosts[name]
        res = fn(make(ctx))
        done(res[-1])
    else:
        res = fn(None)
    return res[:-1]


def _mm_nn_x(a, b, exch, **kw):
    out = _mm_nn(a, b, exch=exch, **kw)
    return out if exch is not None else (out, None)


def _layer_fwd(h, w, hosts=None):
    xn = _rmsnorm_fwd(h, w["norm1_g"])
    proj_a = _mm_nn(xn, w["w_in"], n0=0, n=REST0, out_dtype=BF16, name="proj_a")
    proj_r, = _run(hosts, "proj_r", w, lambda e: _mm_nn_x(xn, w["w_in"], e, n0=REST0, n=NREST, name="proj_r"))
    cq, ck = _gate_cols(_fox_gate_fwd(proj_r, w["bf128"]))
    oa, lse = _run(hosts, "attn_fwd", w, lambda e: _attn_fwd(proj_a, cq, ck, exch=e))
    zb = _convb_fwd(proj_r, w["conv_w"])
    logg = _gla_gate_fwd(proj_r, w["wg2p"], w["gla_b_g"])
    o_gla, zc, st_all = _gla_fwd(proj_r, logg, w["gla_norm_g"])
    ya = _mm_nn(oa, w["w_a_o"], name="branch_a")
    yb = _mm_nn(zb, w["w_b_o"], name="branch_b")
    yc = _mm_nn(zc, w["w_c_o"], name="branch_c")
    mix = _merge_fwd(proj_r, w["gate_b3"], ya, yb, yc)
    h1 = _mm_nn(mix, w["w_o"], res=h, name="out_proj")
    xn2 = _rmsnorm_fwd(h1, w["norm2_g"])
    z, = _run(hosts, "up_proj", w, lambda e: _mm_nn_x(xn2, w["w_up"], e, name="up_proj"))
    a = _mlp_gate_fwd(z, w["mlp_conv_w"])
    h2, = _run(hosts, "down_proj", w,
               lambda e: _mm_nn_x(a, w["w_down"], e, res=h1, tk=D_FF // 4, name="down_proj"))
    saved = dict(h=h, xn=xn, proj_a=proj_a, proj_r=proj_r, cq=cq, ck=ck, oa=oa, lse=lse, zb=zb, logg=logg,
                 o_gla=o_gla, zc=zc, st_all=st_all, ya=ya, yb=yb, yc=yc, mix=mix, h1=h1, xn2=xn2, z=z, a=a)
    return h2, saved


def _layer_bwd(dh2, w, s, hosts=None):
    g = {}
    da = _mm_nt(dh2, w["w_down"], tn=D_FF // 4, name="d_down_in")
    g["w_down"] = _mm_tn(s["a"], dh2, tk=D_FF // 4, name="d_w_down")
    dzg, dzu, dmw_g, dmw_u = _mlp_gate_bwd(s["z"], da, w["mlp_conv_w"])
    g["mlp_conv_w"] = jnp.concatenate([dmw_g, dmw_u], axis=1)
    dxn2 = _mm_nt(dzg, w["w_up"], k0=0, kw=D_FF, name="d_up_in_g")
    dxn2 = _mm_nt(dzu, w["w_up"], k0=D_FF, kw=D_FF, add=dxn2, name="d_up_in_u")
    g["w_up"] = jnp.concatenate([_mm_tn(s["xn2"], dzg, name="d_w_up_g"), _mm_tn(s["xn2"], dzu, name="d_w_up_u")], axis=1)
    dh1, g["norm2_g"] = _rmsnorm_bwd(s["h1"], w["norm2_g"], dxn2, dh2)
    dmix = _mm_nt(dh1, w["w_o"], name="d_out_proj_in")
    g["w_o"] = _mm_tn(s["mix"], dh1, name="d_w_o")
    dya, dyb, dyc, dproj, g["gate_b3"] = _merge_bwd(s["proj_r"], w["gate_b3"], s["ya"], s["yb"], s["yc"], dmix)
    doa = _mm_nt(dya, w["w_a_o"], out_dtype=BF16, name="d_branch_a_in")
    g["w_a_o"] = _mm_tn(s["oa"], dya, name="d_w_a_o")
    dzb = _mm_nt(dyb, w["w_b_o"], name="d_branch_b_in")
    g["w_b_o"] = _mm_tn(s["zb"], dyb, name="d_w_b_o")
    dzc = _mm_nt(dyc, w["w_c_o"], name="d_branch_c_in")
    g["w_c_o"] = _mm_tn(s["zc"], dyc, name="d_w_c_o")
    dproj, dlogg, g["gla_norm_g"] = _gla_bwd(s["proj_r"], s["logg"], s["st_all"], s["o_gla"], dzc, w["gla_norm_g"], dproj)
    dproj, g["conv_w"] = _convb_bwd(s["proj_r"], dzb, w["conv_w"], dproj)
    dproj, dcq, delta = _run(hosts, "attn_bwd_dq", g, lambda e: _attn_bwd_dq(
        s["proj_a"], doa, s["oa"], s["lse"], s["cq"], s["ck"], dproj, exch=e))
    dproj, dck = _run(hosts, "attn_bwd_dkv", g, lambda e: _attn_bwd_dkv(
        s["proj_a"], doa, s["lse"], delta, s["cq"], s["ck"], dproj, exch=e))
    dc = jnp.pad((dcq[:, :, 0] + dck[:, 0, :]).T, ((0, 0), (0, BLK - FOX_H)))
    dfa, g["bf128"] = _fox_gate_bwd(s["proj_r"], dc, w["bf128"])
    dproj, g["gla_b_g"], g["wg2p"] = _small_bwd(s["proj_r"], dlogg, w["wg2p"], w["wg2p"].T, w["gla_b_g"], dfa, dproj)
    dxn = _mm_nt(dproj, w["w_in"], name="d_in_proj_in")
    g["w_in"] = _mm_tn(s["xn"], dproj, name="d_w_in")
    dh0, g["norm1_g"] = _rmsnorm_bwd(s["h"], w["norm1_g"], dxn, dh1)
    return dh0, g


def _local_step(x, tgt, meta, final_g, layers, hosts_fwd=None, hosts_bwd=None):
    h = jnp.concatenate([jnp.zeros((PAD, D), F32), meta, x], axis=0)
    saved = []
    for l, w in enumerate(layers):
        h, s = _layer_fwd(h, w, hosts_fwd[l] if hosts_fwd else None)
        saved.append(s)
    dh, dgf, sq = _final_loss(h, final_g, tgt)
    grads = [None] * len(layers)
    for l in reversed(range(len(layers))):
        dh, grads[l] = _layer_bwd(dh, layers[l], saved[l], hosts_bwd[l](grads) if hosts_bwd else None)
    return sq[0, 0], dh[BLK:], dh[PAD:BLK], dgf, grads


def _w_in_to_kernel(w_nat):
    parts = [w_nat[:, s:s + n] for s, n in _segments()]
    parts.append(jnp.zeros((w_nat.shape[0], SMALL_W - 8 - GLA_R), w_nat.dtype))
    return jnp.concatenate(parts, axis=1)


def _w_in_from_kernel(w_k):
    pieces, off = [], 0
    for s, n in _segments():
        pieces.append((s, w_k[:, off:off + n]))
        off += n
    return jnp.concatenate([p for _, p in sorted(pieces, key=lambda t: t[0])], axis=1)


def _pad_rows_at(a, row0, nrows):
    return jnp.pad(a, ((row0, nrows - row0 - a.shape[0]), (0, 0)))


def _big_to_kernel(name, full):
    return _w_in_to_kernel(full) if name == "w_in" else full


def _layer_weights(big, conv_w, gla_w_g2, mlp_conv_w, norm1_g, fox_b_f, gate_b, gla_b_g, gla_norm_g, norm2_g):
    w = {n: _big_to_kernel(n, a) for n, a in big.items()}
    w.update(
        conv_w=conv_w, mlp_conv_w=mlp_conv_w,
        wg2p=_pad_rows_at(gla_w_g2, 8, BLK).astype(BF16),
        norm1_g=norm1_g[None], norm2_g=norm2_g[None], gla_b_g=gla_b_g[None], gla_norm_g=gla_norm_g[None],
        bf128=jnp.pad(fox_b_f, (0, BLK - FOX_H))[None], gate_b3=gate_b.reshape(3, D))
    return w


def _layer_grads_natural(g):
    return dict(
        w_in=_w_in_from_kernel(g["w_in"]), w_a_o=g["w_a_o"], w_b_o=g["w_b_o"], w_c_o=g["w_c_o"], w_o=g["w_o"],
        w_up=g["w_up"], w_down=g["w_down"], conv_w=g["conv_w"], mlp_conv_w=g["mlp_conv_w"],
        gla_w_g2=g["wg2p"][8:8 + GLA_R], norm1_g=g["norm1_g"][0], norm2_g=g["norm2_g"][0],
        gla_b_g=g["gla_b_g"][0], gla_norm_g=g["gla_norm_g"][0], fox_b_f=g["bf128"][0, :FOX_H],
        gate_b=g["gate_b3"].reshape(3 * D))


def _adamw(recv, w, m, v, layer, prev=None, name="adamw"):
    n_slot, r, c = recv.shape
    lyr = w.shape[0]
    tr = r
    for t in range(16, r, 16):
        if r % t == 0 and t * c <= ADAMW_BLOCK_ELEMS:
            tr = t
    if r * c <= ADAMW_BLOCK_ELEMS:
        tr = r
    bc1, bc2 = 1.0 - ADAM_B1 ** ADAM_STEP, 1.0 - ADAM_B2 ** ADAM_STEP

    def kern(*refs):
        r_ref, w_ref, m_ref, v_ref = refs[:4]
        g_out, d_out, m_out, v_out = refs[-4:]
        g = r_ref[0].astype(F32)
        for sidx in range(1, n_slot):
            g = g + r_ref[sidx].astype(F32)
        m_new = ADAM_B1 * m_ref[...] + (1.0 - ADAM_B1) * g
        v_new = ADAM_B2 * v_ref[...] + (1.0 - ADAM_B2) * (g * g)
        g_out[...] = g
        m_out[...] = m_new
        v_out[...] = v_new
        d_out[...] = -ADAM_LR * ((m_new / bc1) / (jnp.sqrt(v_new / bc2) + ADAM_EPS) + ADAM_WD * w_ref[...])

    lspec = pl.BlockSpec((None, tr, c), lambda i: (layer, i, 0))
    in_specs = [pl.BlockSpec((n_slot, tr, c), lambda i: (0, i, 0)), lspec, lspec, lspec]
    args = [recv, w, m, v]
    aliases = {}
    if prev is not None:
        in_specs += [pl.BlockSpec(memory_space=pl.ANY)] * 4
        args += list(prev)
        aliases = {4: 0, 5: 1, 6: 2, 7: 3}
    return pl.pallas_call(
        kern, grid=(r // tr,), in_specs=in_specs, out_specs=[lspec] * 4,
        out_shape=[jax.ShapeDtypeStruct((lyr, r, c), F32)] * 4, input_output_aliases=aliases,
        compiler_params=_cparams("parallel"), name=name)(*args)


_BIG = ("w_in", "w_a_o", "w_b_o", "w_c_o", "w_o", "w_up", "w_down")
_COL_SHARDED = ("w_in", "w_a_o", "w_b_o", "w_c_o", "w_up", "conv_w", "gla_w_g2", "mlp_conv_w")
_REPL = ("norm1_g", "fox_b_f", "gate_b", "gla_b_g", "gla_norm_g", "norm2_g")


def _cols_from_slots(a):
    return jnp.transpose(a, (1, 0, 2)).reshape(a.shape[1], N_DEV * a.shape[2])


def _cols_to_slots(a):
    r, c8 = a.shape
    return jnp.transpose(a.reshape(r, N_DEV, c8 // N_DEV), (1, 0, 2))


def _rows_to_slots(a):
    return a.reshape(N_DEV, a.shape[0] // N_DEV, a.shape[1])


def kernel(x, meta_tokens, norm1_g, w_in, fox_b_f, gate_b, conv_w, gla_w_g2, gla_b_g, gla_norm_g, w_a_o, w_b_o, w_c_o, w_o, norm2_g, w_up, mlp_conv_w, w_down, final_norm_g, loss_target, m_meta_tokens, m_norm1_g, m_w_in, m_fox_b_f, m_gate_b, m_conv_w, m_gla_w_g2, m_gla_b_g, m_gla_norm_g, m_w_a_o, m_w_b_o, m_w_c_o, m_w_o, m_norm2_g, m_w_up, m_mlp_conv_w, m_w_down, m_final_norm_g, v_meta_tokens, v_norm1_g, v_w_in, v_fox_b_f, v_gate_b, v_conv_w, v_gla_w_g2, v_gla_b_g, v_gla_norm_g, v_w_a_o, v_w_b_o, v_w_c_o, v_w_o, v_norm2_g, v_w_up, v_mlp_conv_w, v_w_down, v_final_norm_g):
    names = ("meta_tokens", "norm1_g", "w_in", "fox_b_f", "gate_b", "conv_w", "gla_w_g2", "gla_b_g", "gla_norm_g",
             "w_a_o", "w_b_o", "w_c_o", "w_o", "norm2_g", "w_up", "mlp_conv_w", "w_down", "final_norm_g")
    wts = dict(zip(names, (meta_tokens, norm1_g, w_in, fox_b_f, gate_b, conv_w, gla_w_g2, gla_b_g, gla_norm_g,
                           w_a_o, w_b_o, w_c_o, w_o, norm2_g, w_up, mlp_conv_w, w_down, final_norm_g)))
    mom = dict(zip(names, (m_meta_tokens, m_norm1_g, m_w_in, m_fox_b_f, m_gate_b, m_conv_w, m_gla_w_g2, m_gla_b_g,
                           m_gla_norm_g, m_w_a_o, m_w_b_o, m_w_c_o, m_w_o, m_norm2_g, m_w_up, m_mlp_conv_w, m_w_down,
                           m_final_norm_g)))
    var = dict(zip(names, (v_meta_tokens, v_norm1_g, v_w_in, v_fox_b_f, v_gate_b, v_conv_w, v_gla_w_g2, v_gla_b_g,
                           v_gla_norm_g, v_w_a_o, v_w_b_o, v_w_c_o, v_w_o, v_norm2_g, v_w_up, v_mlp_conv_w, v_w_down,
                           v_final_norm_g)))

    small = _exchange([conv_w, gla_w_g2, mlp_conv_w, meta_tokens], [True] * 4, "gather_small")
    conv_full = jnp.transpose(small[0], (1, 2, 0, 3)).reshape(DEPTH, 3, CONV_CH)
    g2_full = jnp.transpose(small[1], (1, 2, 0, 3)).reshape(DEPTH, GLA_R, GLA_H * GLA_DK)
    mconv_full = jnp.transpose(small[2], (1, 2, 0, 3)).reshape(DEPTH, 3, 2 * D_FF)
    meta_full = _cols_from_slots(small[3])
    layers = [_layer_weights({}, conv_full[l], g2_full[l], mconv_full[l], norm1_g[l], fox_b_f[l], gate_b[l],
                             gla_b_g[l], gla_norm_g[l], norm2_g[l]) for l in range(DEPTH)]

    def gather(l, which):
        def make(_):
            return [wts[n][l].astype(BF16) for n in which], [True] * len(which)

        def done(got):
            for n, a in zip(which, got):
                full = _cols_from_slots(a) if n in _COL_SHARDED else a.reshape(-1, a.shape[-1])
                layers[l][n] = _big_to_kernel(n, full)

        return make, done

    recv_big = [dict() for _ in range(DEPTH)]

    def scatter(l, which, grads_of):
        def make(ctx):
            g = grads_of(ctx)
            nat = [_w_in_from_kernel(g[n]) if n == "w_in" else g[n] for n in which]
            return ([_cols_to_slots(a) if n in _COL_SHARDED else _rows_to_slots(a) for n, a in zip(which, nat)],
                    [False] * len(which))

        def done(got):
            recv_big[l].update(zip(which, got))

        return make, done

    early = ("w_down", "w_up", "w_o", "w_a_o", "w_b_o", "w_c_o")
    make, done = gather(0, ("w_in",))
    done(_exchange(*make(None), "gather_w_in"))
    hosts_fwd = [
        {"proj_r": gather(0, early), "attn_fwd": gather(1, ("w_in",)), "up_proj": gather(1, ("w_up",)),
         "down_proj": gather(1, ("w_down", "w_o", "w_a_o", "w_b_o", "w_c_o"))},
        None]
    hosts_bwd = [
        lambda grads: {"attn_bwd_dq": scatter(1, _BIG, lambda _: grads[1]),
                       "attn_bwd_dkv": scatter(0, early, lambda g: g)},
        lambda grads: None]

    sq, grad_x, dmeta, dgf, grads_k = _local_step(x[0], loss_target[0], meta_full, final_norm_g[None], layers,
                                                  hosts_fwd, hosts_bwd)
    loss = lax.psum(sq * (0.5 / D), ("x", "y", "c"))
    grads = [_layer_grads_natural(g) for g in grads_k]

    out_g, out_d, out_m, out_v = {}, {}, {}, {}

    def update(name, recv, layer, lyr_shape, prev):
        w3, m3, v3 = (t[name].reshape(lyr_shape) for t in (wts, mom, var))
        return _adamw(recv.reshape((recv.shape[0],) + lyr_shape[1:]), w3, m3, v3, layer, prev, name="adamw_" + name)

    def store(name, res):
        shape = wts[name].shape
        out_g[name], out_d[name], out_m[name], out_v[name] = (t.reshape(shape) for t in res)

    make, done = scatter(0, ("w_in",), lambda g: g)
    done(_exchange(*make(grads_k[0]), "scatter_w_in"))
    for n in _BIG:
        res = None
        for l in range(DEPTH):
            res = update(n, recv_big[l][n], l, wts[n].shape, res)
        store(n, res)

    def stack_layers(name):
        return jnp.stack([grads[l][name] for l in range(DEPTH)])

    s_conv = jnp.transpose(stack_layers("conv_w").reshape(DEPTH, 3, N_DEV, -1), (2, 0, 1, 3))
    s_g2 = jnp.transpose(stack_layers("gla_w_g2").reshape(DEPTH, GLA_R, N_DEV, -1), (2, 0, 1, 3))
    s_mconv = jnp.transpose(stack_layers("mlp_conv_w").reshape(DEPTH, 3, N_DEV, -1), (2, 0, 1, 3))
    s_meta = _cols_to_slots(dmeta)
    repl = [stack_layers(n) for n in _REPL] + [dgf]
    pack = jnp.concatenate([jnp.pad(a.reshape(-1), (0, (-a.size) % 1024)) for a in repl]).reshape(-1, BLK)
    r_conv, r_g2, r_mconv, r_meta, r_pack = _exchange(
        [s_conv, s_g2, s_mconv, s_meta, pack], [False, False, False, False, True], "scatter_small")
    store("conv_w", update("conv_w", r_conv, 0, (1, DEPTH * 3, CONV_CH // N_DEV), None))
    store("gla_w_g2", update("gla_w_g2", r_g2, 0, (1, DEPTH * GLA_R, GLA_H * GLA_DK // N_DEV), None))
    store("mlp_conv_w", update("mlp_conv_w", r_mconv, 0, (1, DEPTH * 3, 2 * D_FF // N_DEV), None))
    store("meta_tokens", update("meta_tokens", r_meta, 0, (1, N_META, D // N_DEV), None))
    off = 0
    for n, a in zip(_REPL + ("final_norm_g",), repl):
        rows = (a.size + 1023) // 1024 * 8
        part = r_pack[:, off:off + rows].reshape(N_DEV, -1)[:, :a.size]
        off += rows
        shape2 = (1, 1, a.size) if a.size % BLK else (1, a.size // BLK, BLK)
        store(n, update(n, part, 0, shape2, None))

    order = lambda d: [d[n] for n in names]
    return (loss, grad_x[None], *order(out_g), *order(out_d), *order(out_m), *order(out_v))
```

```python
import functools

import jax
import jax.numpy as jnp
from jax import lax
from jax.experimental import pallas as pl
from jax.experimental.pallas import tpu as pltpu

F32 = jnp.float32
BF16 = jnp.bfloat16

D = 2048
DEPTH = 2
N_META = 16
BLK = 128
PAD = BLK - N_META
EPS = 1e-6
NEG = -1e30

FOX_H, FOX_DH = 8, 128
FOX_W = FOX_H * FOX_DH
CONV_CH = 1024
GLA_H, GLA_DK, GLA_DV, GLA_R, GLA_TAU = 4, 128, 256, 16, 16.0
D_FF = 5632
N_IN = 15384
N_DEV = 8

ADAM_LR, ADAM_B1, ADAM_B2, ADAM_EPS, ADAM_WD, ADAM_STEP = 0.001, 0.9, 0.999, 1e-08, 0.01, 10

CONV_TC = 512
GATE_TN = 512
KV0 = 1024
REST0 = 3072
GLA_GRP = 768
SMALL_W = 512
NP = 15872
NREST = NP - REST0
R_CONV_BLK0 = 0
R_GLA_BLK0 = (6144 - REST0) // GLA_GRP
R_GATE_BLK0 = (9216 - REST0) // (3 * GATE_TN)
R_SMALL_BLK128 = (15360 - REST0) // 128
F_CONV_BLK0 = 3072 // (3 * CONV_TC)
F_GLA_BLK0 = 6144 // GLA_GRP
F_GATE_BLK0 = 9216 // (3 * GATE_TN)
F_SMALL_BLK0 = 15360 // SMALL_W

VMEM_LIMIT = 56 * 1024 * 1024
ADAMW_BLOCK_ELEMS = 128 * 1024


def _segments():
    seg = [(0, 1024)]
    for h in range(FOX_H):
        seg += [(1024 + 128 * h, 128), (2048 + 128 * h, 128)]
    for j in range(CONV_CH // CONV_TC):
        seg += [(3080 + CONV_TC * j, CONV_TC), (4104 + CONV_TC * j, CONV_TC), (5128 + CONV_TC * j, CONV_TC)]
    for h in range(GLA_H):
        seg += [(6152 + 128 * h, 128), (6664 + 128 * h, 128), (7176 + 256 * h, 256), (8200 + 256 * h, 256)]
    for j in range(D // GATE_TN):
        seg += [(9240 + GATE_TN * j, GATE_TN), (11288 + GATE_TN * j, GATE_TN), (13336 + GATE_TN * j, GATE_TN)]
    seg += [(3072, 8), (9224, 16)]
    return seg


def _cparams(*sem):
    return pltpu.CompilerParams(dimension_semantics=sem, vmem_limit_bytes=VMEM_LIMIT)


def _row_tile(n, target):
    best = BLK
    t = BLK
    while t <= min(n, target):
        if n % t == 0:
            best = t
        t += BLK
    return best


def _sigmoid(x):
    return 1.0 / (1.0 + jnp.exp(-x))


def _log_sigmoid(x):
    return jnp.minimum(x, 0.0) - jnp.log(1.0 + jnp.exp(-jnp.abs(x)))


def _valid_rows(row0, n):
    return (row0 + lax.broadcasted_iota(jnp.int32, (n, 1), 0)) >= PAD


def _dot(a, b):
    return jnp.dot(a, b, preferred_element_type=F32)


def _dot_nt(a, b):
    return lax.dot_general(a, b, (((1,), (1,)), ((), ())), preferred_element_type=F32)


def _dot_tn(a, b):
    return lax.dot_general(a, b, (((0,), (0,)), ((), ())), preferred_element_type=F32)


def _exchange_copies(ins, outs, bcast, send_sems, recv_sems, local_sems):
    x, y, c = lax.axis_index("x"), lax.axis_index("y"), lax.axis_index("c")
    me = 4 * x + 2 * y + c
    local, sends, recvs = [], [], []
    for n in range(len(ins)):
        src_me = ins[n] if bcast[n] else ins[n].at[me]
        local.append(pltpu.make_async_copy(src_me, outs[n].at[me], local_sems.at[n]))
    for k in range(1, N_DEV):
        px = 1 - x if (k >> 2) & 1 else x
        py = 1 - y if (k >> 1) & 1 else y
        pc = 1 - c if k & 1 else c
        peer = 4 * px + 2 * py + pc
        for n in range(len(ins)):
            src = ins[n] if bcast[n] else ins[n].at[peer]

            def copy(dst_slot, src=src, n=n, k=k, to=(px, py, pc)):
                return pltpu.make_async_remote_copy(
                    src_ref=src, dst_ref=outs[n].at[dst_slot], send_sem=send_sems.at[n, k - 1],
                    recv_sem=recv_sems.at[n, k - 1], device_id=to, device_id_type=pl.DeviceIdType.MESH)

            sends.append(copy(me))
            recvs.append(copy(peer))
    return local, sends, recvs


def _exchange_start(copies):
    local, sends, _ = copies
    for cp in local + sends:
        cp.start()


def _exchange_wait(copies):
    local, sends, recvs = copies
    for cp in recvs:
        cp.wait_recv()
    for cp in sends:
        cp.wait_send()
    for cp in local:
        cp.wait()


def _exchange_shapes(arrays, bcast):
    return [jax.ShapeDtypeStruct(((N_DEV,) + a.shape) if b else a.shape, a.dtype) for a, b in zip(arrays, bcast)]


def _exchange_sems(n_arr):
    return [pltpu.SemaphoreType.DMA((n_arr, N_DEV - 1)), pltpu.SemaphoreType.DMA((n_arr, N_DEV - 1)),
            pltpu.SemaphoreType.DMA((n_arr,))]


def _exchange(arrays, bcast, name):
    n_arr = len(arrays)

    def body(*refs):
        copies = _exchange_copies(refs[:n_arr], refs[n_arr:2 * n_arr], bcast, *refs[2 * n_arr:])
        _exchange_start(copies)
        _exchange_wait(copies)

    hbm = pl.BlockSpec(memory_space=pltpu.HBM)
    return pl.pallas_call(
        body, out_shape=_exchange_shapes(arrays, bcast), in_specs=[hbm] * n_arr, out_specs=[hbm] * n_arr,
        scratch_shapes=_exchange_sems(n_arr),
        compiler_params=pltpu.CompilerParams(has_side_effects=True), name=name)(*arrays)


def _pcall(kern, *, grid, in_specs, out_specs, out_shape, scratch, sem, name, args, aliases=None, exch=None):
    params = pltpu.CompilerParams(dimension_semantics=sem, vmem_limit_bytes=VMEM_LIMIT,
                                  has_side_effects=exch is not None)
    kw = dict(grid=grid, compiler_params=params, name=name, input_output_aliases=aliases or {})
    if exch is None:
        out = pl.pallas_call(kern, in_specs=in_specs, out_specs=out_specs, out_shape=out_shape,
                             scratch_shapes=scratch, **kw)(*args)
        return out, None
    arrays, bcast = exch
    n_x, n_in, n_out, n_sc = len(arrays), len(in_specs), len(out_specs), len(scratch)

    def hosted(*refs):
        ins, x_in = refs[:n_in], refs[n_in:n_in + n_x]
        outs, x_out = refs[n_in + n_x:n_in + n_x + n_out], refs[n_in + n_x + n_out:n_in + 2 * n_x + n_out]
        sc, sems = refs[n_in + 2 * n_x + n_out:n_in + 2 * n_x + n_out + n_sc], refs[n_in + 2 * n_x + n_out + n_sc:]
        ids = [pl.program_id(d) for d in range(len(grid))]
        first = functools.reduce(jnp.logical_and, [i == 0 for i in ids])
        last = functools.reduce(jnp.logical_and, [i == g - 1 for i, g in zip(ids, grid)])

        @pl.when(first)
        def _():
            _exchange_start(_exchange_copies(x_in, x_out, bcast, *sems))

        kern(*ins, *outs, *sc)

        @pl.when(last)
        def _():
            _exchange_wait(_exchange_copies(x_in, x_out, bcast, *sems))

    hbm = pl.BlockSpec(memory_space=pltpu.HBM)
    out = pl.pallas_call(
        hosted, in_specs=list(in_specs) + [hbm] * n_x, out_specs=list(out_specs) + [hbm] * n_x,
        out_shape=list(out_shape) + _exchange_shapes(arrays, bcast),
        scratch_shapes=list(scratch) + _exchange_sems(n_x), **kw)(*args, *arrays)
    return out[:n_out], out[n_out:]


def _mm_nn(a, b, *, n0=0, n=None, out_dtype=F32, res=None, tm=640, tn=512, tk=None, name="mm_nn", exch=None):
    m, k = a.shape
    n = b.shape[1] - n0 if n is None else n
    tm = _row_tile(m, tm)
    tk = k if tk is None else tk
    nk = k // tk
    assert k % tk == 0 and n % tn == 0 and n0 % tn == 0
    nb0 = n0 // tn

    def kern(*refs):
        if res is None:
            a_ref, b_ref, o_ref, acc = refs
        else:
            a_ref, b_ref, r_ref, o_ref, acc = refs
        kk = pl.program_id(2)
        row0 = pl.program_id(0) * tm

        def finish(prod):
            if res is None:
                o_ref[...] = prod.astype(out_dtype)
            else:
                o_ref[...] = (r_ref[...] + jnp.where(_valid_rows(row0, tm), prod, 0.0)).astype(out_dtype)

        if nk == 1:
            finish(_dot(a_ref[...].astype(BF16), b_ref[...].astype(BF16)))
            return

        @pl.when(kk == 0)
        def _():
            acc[...] = jnp.zeros_like(acc)

        acc[...] += _dot(a_ref[...].astype(BF16), b_ref[...].astype(BF16))

        @pl.when(kk == nk - 1)
        def _():
            finish(acc[...])

    in_specs = [pl.BlockSpec((tm, tk), lambda i, j, kk: (i, kk)),
                pl.BlockSpec((tk, tn), lambda i, j, kk: (kk, nb0 + j))]
    args = [a, b]
    if res is not None:
        in_specs.append(pl.BlockSpec((tm, tn), lambda i, j, kk: (i, j)))
        args.append(res)
    out, got = _pcall(
        kern, grid=(m // tm, n // tn, nk), in_specs=in_specs,
        out_specs=[pl.BlockSpec((tm, tn), lambda i, j, kk: (i, j))],
        out_shape=[jax.ShapeDtypeStruct((m, n), out_dtype)],
        scratch=[pltpu.VMEM((tm, tn) if nk > 1 else (8, 128), F32)],
        sem=("parallel", "parallel", "arbitrary"), name=name, args=args, exch=exch)
    return out[0] if exch is None else (out[0], got)


def _mm_nt(a, b, *, k0=0, kw=None, out_dtype=F32, add=None, tm=640, tn=None, tk=512, name="mm_nt", exch=None):
    m = a.shape[0]
    kw = a.shape[1] if kw is None else kw
    nn = b.shape[0]
    tm = _row_tile(m, tm)
    tn = nn if tn is None else tn
    assert kw % tk == 0 and k0 % tk == 0 and nn % tn == 0 and a.shape[1] == kw
    nk = kw // tk
    kb0 = k0 // tk

    def kern(*refs):
        if add is None:
            a_ref, b_ref, o_ref, acc = refs
        else:
            a_ref, b_ref, d_ref, o_ref, acc = refs
        kk = pl.program_id(2)

        @pl.when(kk == 0)
        def _():
            acc[...] = jnp.zeros_like(acc)

        acc[...] += _dot_nt(a_ref[...].astype(BF16), b_ref[...].astype(BF16))

        @pl.when(kk == nk - 1)
        def _():
            if add is None:
                o_ref[...] = acc[...].astype(out_dtype)
            else:
                o_ref[...] = (acc[...] + d_ref[...]).astype(out_dtype)

    in_specs = [pl.BlockSpec((tm, tk), lambda i, j, kk: (i, kk)),
                pl.BlockSpec((tn, tk), lambda i, j, kk: (j, kb0 + kk))]
    args = [a, b]
    if add is not None:
        in_specs.append(pl.BlockSpec((tm, tn), lambda i, j, kk: (i, j)))
        args.append(add)
    out, got = _pcall(
        kern, grid=(m // tm, nn // tn, nk), in_specs=in_specs,
        out_specs=[pl.BlockSpec((tm, tn), lambda i, j, kk: (i, j))],
        out_shape=[jax.ShapeDtypeStruct((m, nn), out_dtype)],
        scratch=[pltpu.VMEM((tm, tn), F32)],
        sem=("parallel", "parallel", "arbitrary"), name=name, args=args, exch=exch)
    return out[0] if exch is None else (out[0], got)


def _mm_tn(a, b, *, out_dtype=BF16, tm=640, tk=None, tn=512, name="mm_tn", exch=None):
    m, k = a.shape
    n = b.shape[1]
    tm = _row_tile(m, tm)
    tk = k if tk is None else tk
    assert k % tk == 0 and n % tn == 0
    nm = m // tm

    def kern(a_ref, b_ref, o_ref, acc):
        mm = pl.program_id(2)

        @pl.when(mm == 0)
        def _():
            acc[...] = jnp.zeros_like(acc)

        acc[...] += _dot_tn(a_ref[...].astype(BF16), b_ref[...].astype(BF16))

        @pl.when(mm == nm - 1)
        def _():
            o_ref[...] = acc[...].astype(out_dtype)

    out, got = _pcall(
        kern, grid=(k // tk, n // tn, nm),
        in_specs=[pl.BlockSpec((tm, tk), lambda i, j, mm: (mm, i)),
                  pl.BlockSpec((tm, tn), lambda i, j, mm: (mm, j))],
        out_specs=[pl.BlockSpec((tk, tn), lambda i, j, mm: (i, j))],
        out_shape=[jax.ShapeDtypeStruct((k, n), out_dtype)],
        scratch=[pltpu.VMEM((tk, tn), F32)],
        sem=("parallel", "parallel", "arbitrary"), name=name, args=(a, b), exch=exch)
    return out[0] if exch is None else (out[0], got)


def _rmsnorm_fwd(h, g, tr=640):
    lp = h.shape[0]
    tr = _row_tile(lp, tr)

    def kern(h_ref, g_ref, o_ref):
        x = h_ref[...]
        r = lax.rsqrt(jnp.mean(x * x, axis=-1, keepdims=True) + EPS)
        o_ref[...] = (x * r * g_ref[...]).astype(BF16)

    return pl.pallas_call(
        kern, grid=(lp // tr,),
        in_specs=[pl.BlockSpec((tr, D), lambda i: (i, 0)), pl.BlockSpec((1, D), lambda i: (0, 0))],
        out_specs=pl.BlockSpec((tr, D), lambda i: (i, 0)),
        out_shape=jax.ShapeDtypeStruct((lp, D), BF16),
        compiler_params=_cparams("parallel"), name="rmsnorm_fwd")(h, g)


def _rmsnorm_bwd(h, g, dxn, dres, tr=640):
    lp = h.shape[0]
    tr = _row_tile(lp, tr)

    def kern(h_ref, g_ref, dxn_ref, dres_ref, dh_ref, dg_ref):
        i = pl.program_id(0)
        x = h_ref[...]
        r = lax.rsqrt(jnp.mean(x * x, axis=-1, keepdims=True) + EPS)
        xhat = x * r
        dy = jnp.where(_valid_rows(i * tr, tr), dxn_ref[...], 0.0)

        @pl.when(i == 0)
        def _():
            dg_ref[...] = jnp.zeros_like(dg_ref)

        dg_ref[...] += jnp.sum(dy * xhat, axis=0, keepdims=True)
        dxh = dy * g_ref[...]
        dh_ref[...] = dres_ref[...] + r * (dxh - xhat * jnp.mean(dxh * xhat, axis=-1, keepdims=True))

    row = pl.BlockSpec((tr, D), lambda i: (i, 0))
    vec = pl.BlockSpec((1, D), lambda i: (0, 0))
    return pl.pallas_call(
        kern, grid=(lp // tr,), in_specs=[row, vec, row, row], out_specs=[row, vec],
        out_shape=[jax.ShapeDtypeStruct((lp, D), F32), jax.ShapeDtypeStruct((1, D), F32)],
        compiler_params=_cparams("arbitrary"), name="rmsnorm_bwd")(h, g, dxn, dres)


def _shift_down(xe, k):
    return xe if k == 0 else pltpu.roll(xe, k, 0)


def _shift_up(xe, k):
    return xe if k == 0 else pltpu.roll(xe, xe.shape[0] - k, 0)


def _conv_ext(xe, w_ref):
    return w_ref[2:3, :] * xe + w_ref[1:2, :] * _shift_down(xe, 1) + w_ref[0:1, :] * _shift_down(xe, 2)


def _halo_specs(tr, width, col_of, nrows, rows_first):
    r8 = tr // 8
    last8 = nrows // 8 - 1
    if rows_first:
        prev = pl.BlockSpec((8, width), lambda i, j: (jnp.maximum(i * r8 - 1, 0), col_of(j)))
        nxt = pl.BlockSpec((8, width), lambda i, j: (jnp.minimum((i + 1) * r8, last8), col_of(j)))
    else:
        prev = pl.BlockSpec((8, width), lambda j, i: (jnp.maximum(i * r8 - 1, 0), col_of(j)))
        nxt = pl.BlockSpec((8, width), lambda j, i: (jnp.minimum((i + 1) * r8, last8), col_of(j)))
    return prev, nxt


def _convb_fwd(proj_r, conv_w, tr=640):
    lp = proj_r.shape[0]
    tr = _row_tile(lp, tr)
    tc = CONV_TC
    gw = 3 * tc

    def kern(g_ref, gp_ref, w_ref, o_ref):
        i = pl.program_id(0)
        g = g_ref[...]
        p = g[:, tc:2 * tc] * g[:, 2 * tc:]
        gp = gp_ref[...]
        pp = jnp.where(i > 0, gp[:, tc:2 * tc] * gp[:, 2 * tc:], 0.0)
        y = _conv_ext(jnp.concatenate([pp, p], axis=0), w_ref)[8:]
        o_ref[...] = (g[:, :tc] * y).astype(BF16)

    prev, _ = _halo_specs(tr, gw, lambda j: R_CONV_BLK0 + j, lp, True)
    return pl.pallas_call(
        kern, grid=(lp // tr, CONV_CH // tc),
        in_specs=[pl.BlockSpec((tr, gw), lambda i, j: (i, R_CONV_BLK0 + j)), prev,
                  pl.BlockSpec((3, tc), lambda i, j: (0, j))],
        out_specs=pl.BlockSpec((tr, tc), lambda i, j: (i, j)),
        out_shape=jax.ShapeDtypeStruct((lp, CONV_CH), BF16),
        compiler_params=_cparams("parallel", "parallel"), name="convb_fwd")(proj_r, proj_r, conv_w)


def _convb_bwd(proj_r, dzb, conv_w, dproj, tr=640):
    lp = proj_r.shape[0]
    tr = _row_tile(lp, tr)
    nr = lp // tr
    tc = CONV_TC
    gw = 3 * tc

    def kern(g_ref, gp_ref, gn_ref, dz_ref, dzn_ref, w_ref, dp_any, dg_ref, dw_ref):
        del dp_any
        i = pl.program_id(1)
        g = g_ref[...]
        b, c, hh = g[:, :tc], g[:, tc:2 * tc], g[:, 2 * tc:]
        p = c * hh
        gp = gp_ref[...]
        pp = jnp.where(i > 0, gp[:, tc:2 * tc] * gp[:, 2 * tc:], 0.0)
        pe = jnp.concatenate([pp, p], axis=0)
        s1 = _shift_down(pe, 1)[8:]
        s2 = _shift_down(pe, 2)[8:]
        y = w_ref[2:3, :] * p + w_ref[1:2, :] * s1 + w_ref[0:1, :] * s2
        dz = dz_ref[...]
        dy = dz * b
        dyn = jnp.where(i < nr - 1, dzn_ref[...] * gn_ref[...][:, :tc], 0.0)
        dye = jnp.concatenate([dy, dyn], axis=0)
        dp = (w_ref[2:3, :] * dy + w_ref[1:2, :] * _shift_up(dye, 1)[:tr]
              + w_ref[0:1, :] * _shift_up(dye, 2)[:tr])
        valid = _valid_rows(i * tr, tr)
        dg_ref[...] = jnp.where(valid, jnp.concatenate([dz * y, dp * hh, dp * c], axis=1), 0.0).astype(BF16)

        @pl.when(i == 0)
        def _():
            dw_ref[...] = jnp.zeros_like(dw_ref)

        dw_ref[0:1, :] += jnp.sum(dy * s2, axis=0, keepdims=True)
        dw_ref[1:2, :] += jnp.sum(dy * s1, axis=0, keepdims=True)
        dw_ref[2:3, :] += jnp.sum(dy * p, axis=0, keepdims=True)

    gprev, gnext = _halo_specs(tr, gw, lambda j: R_CONV_BLK0 + j, lp, False)
    _, dznext = _halo_specs(tr, tc, lambda j: j, lp, False)
    return pl.pallas_call(
        kern, grid=(CONV_CH // tc, nr),
        in_specs=[pl.BlockSpec((tr, gw), lambda j, i: (i, R_CONV_BLK0 + j)), gprev, gnext,
                  pl.BlockSpec((tr, tc), lambda j, i: (i, j)), dznext,
                  pl.BlockSpec((3, tc), lambda j, i: (0, j)),
                  pl.BlockSpec(memory_space=pl.ANY)],
        out_specs=[pl.BlockSpec((tr, gw), lambda j, i: (i, F_CONV_BLK0 + j)),
                   pl.BlockSpec((3, tc), lambda j, i: (0, j))],
        out_shape=[jax.ShapeDtypeStruct(dproj.shape, BF16), jax.ShapeDtypeStruct((3, CONV_CH), F32)],
        input_output_aliases={6: 0},
        compiler_params=_cparams("parallel", "arbitrary"), name="convb_bwd",
    )(proj_r, proj_r, proj_r, dzb, dzb, conv_w, dproj)


MLP_TC = 256


def _mlp_gate_fwd(z, w, tr=640):
    lp = z.shape[0]
    tr = _row_tile(lp, tr)
    tc = 512
    nc = D_FF // tc

    def kern(zg_ref, zgp_ref, zu_ref, zup_ref, wg_ref, wu_ref, o_ref):
        i = pl.program_id(0)
        zge = jnp.concatenate([jnp.where(i > 0, zgp_ref[...], 0.0), zg_ref[...]], axis=0)
        zue = jnp.concatenate([jnp.where(i > 0, zup_ref[...], 0.0), zu_ref[...]], axis=0)
        ug = _conv_ext(zge, wg_ref)[8:]
        uu = _conv_ext(zue, wu_ref)[8:]
        o_ref[...] = (ug * _sigmoid(ug) * uu).astype(BF16)

    gprev, _ = _halo_specs(tr, tc, lambda j: j, lp, True)
    uprev, _ = _halo_specs(tr, tc, lambda j: nc + j, lp, True)
    return pl.pallas_call(
        kern, grid=(lp // tr, nc),
        in_specs=[pl.BlockSpec((tr, tc), lambda i, j: (i, j)), gprev,
                  pl.BlockSpec((tr, tc), lambda i, j: (i, nc + j)), uprev,
                  pl.BlockSpec((3, tc), lambda i, j: (0, j)),
                  pl.BlockSpec((3, tc), lambda i, j: (0, nc + j))],
        out_specs=pl.BlockSpec((tr, tc), lambda i, j: (i, j)),
        out_shape=jax.ShapeDtypeStruct((lp, D_FF), BF16),
        compiler_params=_cparams("parallel", "parallel"), name="mlp_gate_fwd")(z, z, z, z, w, w)


def _mlp_gate_bwd(z, da, w, tr=640, exch=None):
    lp = z.shape[0]
    tr = _row_tile(lp, tr)
    nr = lp // tr
    tc = MLP_TC
    nc = D_FF // tc

    def kern(zg_ref, zgp_ref, zgn_ref, zu_ref, zup_ref, zun_ref, da_ref, dan_ref, wg_ref, wu_ref,
             dzg_ref, dzu_ref, dwg_ref, dwu_ref):
        i = pl.program_id(1)
        first, last = i == 0, i == nr - 1

        def ext(m_ref, p_ref, n_ref):
            return jnp.concatenate([jnp.where(first, 0.0, p_ref[...]), m_ref[...],
                                    jnp.where(last, 0.0, n_ref[...])], axis=0)

        zge, zue = ext(zg_ref, zgp_ref, zgn_ref), ext(zu_ref, zup_ref, zun_ref)
        ug = _conv_ext(zge, wg_ref)[8:]
        uu = _conv_ext(zue, wu_ref)[8:]
        dae = jnp.concatenate([da_ref[...], jnp.where(last, 0.0, dan_ref[...])], axis=0)
        sg = _sigmoid(ug)
        dug = dae * uu * (sg * (1.0 + ug * (1.0 - sg)))
        duu = dae * (ug * sg)
        valid = _valid_rows(i * tr, tr)

        @pl.when(first)
        def _():
            dwg_ref[...] = jnp.zeros_like(dwg_ref)
            dwu_ref[...] = jnp.zeros_like(dwu_ref)

        for du, ze, w_ref, dz_ref, dw_ref in ((dug, zge, wg_ref, dzg_ref, dwg_ref),
                                              (duu, zue, wu_ref, dzu_ref, dwu_ref)):
            dz = (w_ref[2:3, :] * du + w_ref[1:2, :] * _shift_up(du, 1) + w_ref[0:1, :] * _shift_up(du, 2))[:tr]
            dz_ref[...] = jnp.where(valid, dz, 0.0).astype(BF16)
            dum = du[:tr]
            for kk in range(3):
                dw_ref[kk:kk + 1, :] += jnp.sum(dum * _shift_down(ze, 2 - kk)[8:8 + tr], axis=0, keepdims=True)

    gprev, gnext = _halo_specs(tr, tc, lambda j: j, lp, False)
    uprev, unext = _halo_specs(tr, tc, lambda j: nc + j, lp, False)
    main = pl.BlockSpec((tr, tc), lambda j, i: (i, j))
    wspec = pl.BlockSpec((3, tc), lambda j, i: (0, j))
    out, got = _pcall(
        kern, grid=(nc, nr),
        in_specs=[main, gprev, gnext, pl.BlockSpec((tr, tc), lambda j, i: (i, nc + j)), uprev, unext,
                  main, gnext, wspec, pl.BlockSpec((3, tc), lambda j, i: (0, nc + j))],
        out_specs=[main, main, wspec, wspec],
        out_shape=[jax.ShapeDtypeStruct((lp, D_FF), BF16), jax.ShapeDtypeStruct((lp, D_FF), BF16),
                   jax.ShapeDtypeStruct((3, D_FF), F32), jax.ShapeDtypeStruct((3, D_FF), F32)],
        scratch=[], sem=("parallel", "arbitrary"), name="mlp_gate_bwd",
        args=(z, z, z, z, z, z, da, da, w, w), exch=exch)
    return (*out, got)


def _tri(n, lower):
    r = lax.broadcasted_iota(jnp.int32, (n, n), 0)
    c = lax.broadcasted_iota(jnp.int32, (n, n), 1)
    return jnp.where((c <= r) if lower else (c >= r), 1.0, 0.0).astype(F32)


def _dot_exact(a, b):
    return jnp.dot(a, b, preferred_element_type=F32, precision=lax.Precision.HIGHEST)


def _fox_gate_fwd(proj_r, bf128):
    lp = proj_r.shape[0]
    nb = lp // BLK

    def kern(s_ref, b_ref, c_ref):
        tri = _tri(BLK, True)

        def body(i, carry):
            rows = pl.ds(pl.multiple_of(i * BLK, BLK), BLK)
            lf = jnp.where(_valid_rows(i * BLK, BLK), _log_sigmoid(s_ref[rows, :] + b_ref[...]), 0.0)
            cs = _dot_exact(tri, lf) + carry
            c_ref[rows, :] = cs
            return cs[BLK - 1:BLK, :]

        lax.fori_loop(0, nb, body, jnp.zeros((1, BLK), F32))

    return pl.pallas_call(
        kern, grid=(1,),
        in_specs=[pl.BlockSpec((lp, BLK), lambda i: (0, R_SMALL_BLK128)), pl.BlockSpec((1, BLK), lambda i: (0, 0))],
        out_specs=pl.BlockSpec((lp, BLK), lambda i: (0, 0)),
        out_shape=jax.ShapeDtypeStruct((lp, BLK), F32),
        compiler_params=_cparams("arbitrary"), name="fox_gate_fwd")(proj_r, bf128)


def _fox_gate_bwd(proj_r, dc, bf128):
    lp = proj_r.shape[0]
    nb = lp // BLK

    def kern(s_ref, dc_ref, b_ref, dfa_ref, dbf_ref):
        tri = _tri(BLK, False)

        dbf_ref[...] = jnp.zeros_like(dbf_ref)

        def body(ii, run):
            i = nb - 1 - ii
            rows = pl.ds(pl.multiple_of(i * BLK, BLK), BLK)
            dcb = dc_ref[rows, :]
            suf = _dot_exact(tri, dcb) + run
            dfa = jnp.where(_valid_rows(i * BLK, BLK), suf * _sigmoid(-(s_ref[rows, :] + b_ref[...])), 0.0)
            dfa_ref[rows, :] = dfa
            dbf_ref[...] += jnp.sum(dfa, axis=0, keepdims=True)
            return run + jnp.sum(dcb, axis=0, keepdims=True)

        lax.fori_loop(0, nb, body, jnp.zeros((1, BLK), F32))

    return pl.pallas_call(
        kern, grid=(1,),
        in_specs=[pl.BlockSpec((lp, BLK), lambda i: (0, R_SMALL_BLK128)), pl.BlockSpec((lp, BLK), lambda i: (0, 0)),
                  pl.BlockSpec((1, BLK), lambda i: (0, 0))],
        out_specs=[pl.BlockSpec((lp, BLK), lambda i: (0, 0)), pl.BlockSpec((1, BLK), lambda i: (0, 0))],
        out_shape=[jax.ShapeDtypeStruct((lp, BLK), F32), jax.ShapeDtypeStruct((1, BLK), F32)],
        compiler_params=_cparams("arbitrary"), name="fox_gate_bwd")(proj_r, dc, bf128)


LOG2E = 1.4426950408889634
KEY_PAD_BIAS = 1e30


def _attn_logits2(q, k, ck, diag):
    t = _dot_nt(q, k) * (LOG2E * FOX_DH ** -0.5) - ck * LOG2E
    if diag:
        r = lax.broadcasted_iota(jnp.int32, t.shape, 0)
        c = lax.broadcasted_iota(jnp.int32, t.shape, 1)
        t = jnp.where(c <= r, t, NEG)
    return t


ATTN_HEADS = 2


def _head_cols(a):
    return (slice(a * FOX_DH, (a + 1) * FOX_DH), slice(2 * a * FOX_DH, (2 * a + 1) * FOX_DH),
            slice((2 * a + 1) * FOX_DH, (2 * a + 2) * FOX_DH))


def _on_blocks(i, j, step):
    pl.when(j < i)(functools.partial(step, False))
    pl.when(j == i)(functools.partial(step, True))


def _attn_fwd(proj_a, cq, ck, tq=640, exch=None):
    lp = proj_a.shape[0]
    tq = _row_tile(lp, tq)
    tk = tq
    nq = lp // tq

    def kern(q_ref, kv_ref, cq_ref, ck_ref, o_ref, lse_ref, m_sc, l_sc, acc):
        i, j = pl.program_id(1), pl.program_id(2)

        @pl.when(j == 0)
        def _():
            m_sc[...] = jnp.full_like(m_sc, -jnp.inf)
            l_sc[...] = jnp.zeros_like(l_sc)
            acc[...] = jnp.zeros_like(acc)

        def step(diag):
            for a in range(ATTN_HEADS):
                hq, hk, hv = _head_cols(a)
                cq2 = cq_ref[a] * LOG2E
                t = _attn_logits2(q_ref[:, hq], kv_ref[:, hk], ck_ref[a], diag)
                m_old = m_sc[a]
                m_new = jnp.maximum(m_old, jnp.max(t, axis=-1, keepdims=True) + cq2)
                p = jnp.exp2(t + (cq2 - m_new))
                alpha = jnp.exp2(m_old - m_new)
                l_sc[a] = alpha * l_sc[a] + jnp.sum(p, axis=-1, keepdims=True)
                acc[:, hq] = alpha * acc[:, hq] + _dot(p.astype(BF16), kv_ref[:, hv])
                m_sc[a] = m_new

        _on_blocks(i, j, step)

        @pl.when(j == nq - 1)
        def _():
            valid = _valid_rows(i * tq, tq)
            for a in range(ATTN_HEADS):
                hq, _, _ = _head_cols(a)
                o_ref[:, hq] = jnp.where(valid, acc[:, hq] / l_sc[a], 0.0).astype(BF16)
                lse_ref[a] = m_sc[a] + jnp.log(l_sc[a]) * LOG2E

    hp = ATTN_HEADS
    out, got = _pcall(
        kern, grid=(FOX_H // hp, nq, nq),
        in_specs=[pl.BlockSpec((tq, hp * FOX_DH), lambda h, i, j: (i, h)),
                  pl.BlockSpec((tk, 2 * hp * FOX_DH), lambda h, i, j: (jnp.minimum(j, i), KV0 // (2 * hp * FOX_DH) + h)),
                  pl.BlockSpec((hp, tq, 1), lambda h, i, j: (h, i, 0)),
                  pl.BlockSpec((hp, 1, tk), lambda h, i, j: (h, 0, jnp.minimum(j, i)))],
        out_specs=[pl.BlockSpec((tq, hp * FOX_DH), lambda h, i, j: (i, h)),
                   pl.BlockSpec((hp, tq, 1), lambda h, i, j: (h, i, 0))],
        out_shape=[jax.ShapeDtypeStruct((lp, FOX_W), BF16), jax.ShapeDtypeStruct((FOX_H, lp, 1), F32)],
        scratch=[pltpu.VMEM((hp, tq, 1), F32), pltpu.VMEM((hp, tq, 1), F32), pltpu.VMEM((tq, hp * FOX_DH), F32)],
        sem=("parallel", "parallel", "arbitrary"), name="attn_fwd", args=(proj_a, proj_a, cq, ck), exch=exch)
    return out[0], out[1], got


def _attn_bwd_dq(proj_a, do, o, lse, cq, ck, dproj, tq=640, exch=None):
    lp = proj_a.shape[0]
    tq = _row_tile(lp, tq)
    tk = tq
    nq = lp // tq

    def kern(q_ref, kv_ref, do_ref, o_ref, lse_ref, cq_ref, ck_ref, dp_any, dq_ref, dcq_ref, dl_ref, dq_acc):
        del dp_any
        i, j = pl.program_id(1), pl.program_id(2)

        @pl.when(j == 0)
        def _():
            for a in range(ATTN_HEADS):
                hq, _, _ = _head_cols(a)
                dl_ref[a] = jnp.sum(do_ref[:, hq].astype(F32) * o_ref[:, hq].astype(F32), axis=-1, keepdims=True)
            dcq_ref[...] = jnp.zeros_like(dcq_ref)
            dq_acc[...] = jnp.zeros_like(dq_acc)

        def step(diag):
            for a in range(ATTN_HEADS):
                hq, hk, hv = _head_cols(a)
                t = _attn_logits2(q_ref[:, hq], kv_ref[:, hk], ck_ref[a], diag)
                p = jnp.exp2(t + (cq_ref[a] * LOG2E - lse_ref[a]))
                ds = p * (_dot_nt(do_ref[:, hq], kv_ref[:, hv]) - dl_ref[a])
                dcq_ref[a] += jnp.sum(ds, axis=-1, keepdims=True)
                dq_acc[:, hq] += _dot(ds.astype(BF16), kv_ref[:, hk])

        _on_blocks(i, j, step)

        @pl.when(j == nq - 1)
        def _():
            dq_ref[...] = (dq_acc[...] * (FOX_DH ** -0.5)).astype(BF16)

    hp = ATTN_HEADS
    qspec = pl.BlockSpec((tq, hp * FOX_DH), lambda h, i, j: (i, h))
    col = pl.BlockSpec((hp, tq, 1), lambda h, i, j: (h, i, 0))
    out, got = _pcall(
        kern, grid=(FOX_H // hp, nq, nq),
        in_specs=[qspec, pl.BlockSpec((tk, 2 * hp * FOX_DH),
                                      lambda h, i, j: (jnp.minimum(j, i), KV0 // (2 * hp * FOX_DH) + h)),
                  qspec, qspec, col, col,
                  pl.BlockSpec((hp, 1, tk), lambda h, i, j: (h, 0, jnp.minimum(j, i))),
                  pl.BlockSpec(memory_space=pl.ANY)],
        out_specs=[qspec, col, col],
        out_shape=[jax.ShapeDtypeStruct(dproj.shape, BF16), jax.ShapeDtypeStruct((FOX_H, lp, 1), F32),
                   jax.ShapeDtypeStruct((FOX_H, lp, 1), F32)],
        scratch=[pltpu.VMEM((tq, hp * FOX_DH), F32)], aliases={7: 0},
        sem=("parallel", "parallel", "arbitrary"), name="attn_bwd_dq",
        args=(proj_a, proj_a, do, o, lse, cq, ck, dproj), exch=exch)
    return out[0], out[1], out[2], got


def _attn_bwd_dkv(proj_a, do, lse, delta, cq, ck, dproj, tq=640, exch=None):
    lp = proj_a.shape[0]
    tq = _row_tile(lp, tq)
    tk = tq
    nq = lp // tq

    def kern(q_ref, kv_ref, do_ref, lse_ref, dl_ref, cq_ref, ck_ref, dp_any, dkv_ref, dck_ref, dk_acc, dv_acc):
        del dp_any
        j, i = pl.program_id(1), pl.program_id(2)

        @pl.when(i == 0)
        def _():
            dck_ref[...] = jnp.zeros_like(dck_ref)
            dk_acc[...] = jnp.zeros_like(dk_acc)
            dv_acc[...] = jnp.zeros_like(dv_acc)

        def step(diag):
            for a in range(ATTN_HEADS):
                hq, hk, hv = _head_cols(a)
                t = _attn_logits2(q_ref[:, hq], kv_ref[:, hk], ck_ref[a], diag)
                p = jnp.exp2(t + (cq_ref[a] * LOG2E - lse_ref[a]))
                dv_acc[:, hq] += _dot_tn(p.astype(BF16), do_ref[:, hq])
                ds = p * (_dot_nt(do_ref[:, hq], kv_ref[:, hv]) - dl_ref[a])
                dck_ref[a] -= jnp.sum(ds, axis=0, keepdims=True)
                dk_acc[:, hq] += _dot_tn(ds.astype(BF16), q_ref[:, hq])

        _on_blocks(i, j, step)

        @pl.when(i == nq - 1)
        def _():
            parts = []
            for a in range(ATTN_HEADS):
                hq, _, _ = _head_cols(a)
                parts += [dk_acc[:, hq] * (FOX_DH ** -0.5), dv_acc[:, hq]]
            dkv_ref[...] = jnp.concatenate(parts, axis=1).astype(BF16)

    hp = ATTN_HEADS
    qspec = pl.BlockSpec((tq, hp * FOX_DH), lambda h, j, i: (jnp.maximum(i, j), h))
    col = pl.BlockSpec((hp, tq, 1), lambda h, j, i: (h, jnp.maximum(i, j), 0))
    kvspec = pl.BlockSpec((tk, 2 * hp * FOX_DH), lambda h, j, i: (j, KV0 // (2 * hp * FOX_DH) + h))
    rowspec = pl.BlockSpec((hp, 1, tk), lambda h, j, i: (h, 0, j))
    out, got = _pcall(
        kern, grid=(FOX_H // hp, nq, nq),
        in_specs=[qspec, kvspec, qspec, col, col, col, rowspec, pl.BlockSpec(memory_space=pl.ANY)],
        out_specs=[kvspec, rowspec],
        out_shape=[jax.ShapeDtypeStruct(dproj.shape, BF16), jax.ShapeDtypeStruct((FOX_H, 1, lp), F32)],
        scratch=[pltpu.VMEM((tk, hp * FOX_DH), F32), pltpu.VMEM((tk, hp * FOX_DH), F32)], aliases={7: 0},
        sem=("parallel", "parallel", "arbitrary"), name="attn_bwd_dkv",
        args=(proj_a, proj_a, do, lse, delta, cq, ck, dproj), exch=exch)
    return out[0], out[1], got


def _gla_gate_fwd(proj_r, wg2p, bg, tr=640):
    lp = proj_r.shape[0]
    tr = _row_tile(lp, tr)
    w = GLA_H * GLA_DK

    def kern(s_ref, w_ref, b_ref, o_ref):
        zg = _dot(s_ref[...].astype(BF16), w_ref[...]) + b_ref[...]
        o_ref[...] = jnp.where(_valid_rows(pl.program_id(0) * tr, tr), _log_sigmoid(zg) * (1.0 / GLA_TAU), 0.0)

    return pl.pallas_call(
        kern, grid=(lp // tr,),
        in_specs=[pl.BlockSpec((tr, BLK), lambda i: (i, R_SMALL_BLK128)), pl.BlockSpec((BLK, w), lambda i: (0, 0)),
                  pl.BlockSpec((1, w), lambda i: (0, 0))],
        out_specs=pl.BlockSpec((tr, w), lambda i: (i, 0)),
        out_shape=jax.ShapeDtypeStruct((lp, w), F32),
        compiler_params=_cparams("parallel"), name="gla_gate_fwd")(proj_r, wg2p, bg)


def _gla_chunk(grp, g):
    q = grp[:, :GLA_DK] * (GLA_DK ** -0.5)
    k = grp[:, GLA_DK:2 * GLA_DK]
    v = grp[:, 2 * GLA_DK:2 * GLA_DK + GLA_DV]
    r = grp[:, 2 * GLA_DK + GLA_DV:]
    b = _dot_exact(_tri(BLK, True), g)
    bl = b[BLK - 1:BLK, :]
    eb = jnp.exp(b)
    enb = jnp.exp(-b)
    ebl = jnp.exp(bl - b)
    qe, ke, kd = q * eb, k * enb, k * ebl
    causal = lax.broadcasted_iota(jnp.int32, (BLK, BLK), 1) <= lax.broadcasted_iota(jnp.int32, (BLK, BLK), 0)
    att = jnp.where(causal, _dot_nt(qe.astype(BF16), ke.astype(BF16)), 0.0)
    return q, k, v, r, bl, eb, enb, ebl, qe, ke, kd, causal, att


def _gla_fwd(proj_r, logg, gn):
    lp = proj_r.shape[0]
    nc = lp // BLK
    wv = GLA_H * GLA_DV

    def kern(grp_ref, g_ref, gn_ref, o_ref, zc_ref, st_ref, st):
        c = pl.program_id(1)

        @pl.when(c == 0)
        def _():
            st[...] = jnp.zeros_like(st)

        q, k, v, r, bl, eb, enb, ebl, qe, ke, kd, causal, att = _gla_chunk(grp_ref[...], g_ref[...])
        s_t = st[...]
        st_ref[...] = s_t
        vb = v.astype(BF16)
        o = _dot(att.astype(BF16), vb) + _dot_nt(qe.astype(BF16), s_t.astype(BF16))
        st[...] = s_t * jnp.exp(bl) + _dot_tn(vb, kd.astype(BF16))
        o_ref[...] = o
        rstd = lax.rsqrt(jnp.mean(o * o, axis=-1, keepdims=True) + EPS)
        zc_ref[...] = (r * _sigmoid(r) * (o * rstd * gn_ref[...])).astype(BF16)

    vspec = pl.BlockSpec((BLK, GLA_DV), lambda h, c: (c, h))
    return pl.pallas_call(
        kern, grid=(GLA_H, nc),
        in_specs=[pl.BlockSpec((BLK, GLA_GRP), lambda h, c: (c, R_GLA_BLK0 + h)),
                  pl.BlockSpec((BLK, GLA_DK), lambda h, c: (c, h)),
                  pl.BlockSpec((1, GLA_DV), lambda h, c: (0, h))],
        out_specs=[vspec, vspec, pl.BlockSpec((None, None, GLA_DV, GLA_DK), lambda h, c: (h, c, 0, 0))],
        out_shape=[jax.ShapeDtypeStruct((lp, wv), F32), jax.ShapeDtypeStruct((lp, wv), BF16),
                   jax.ShapeDtypeStruct((GLA_H, nc, GLA_DV, GLA_DK), F32)],
        scratch_shapes=[pltpu.VMEM((GLA_DV, GLA_DK), F32)],
        compiler_params=_cparams("parallel", "arbitrary"), name="gla_fwd")(proj_r, logg, gn)


def _gla_bwd(proj_r, logg, st_all, o_all, dzc, gn, dproj):
    lp = proj_r.shape[0]
    nc = lp // BLK

    def kern(grp_ref, g_ref, st_ref, o_ref, dzc_ref, gn_ref, dp_any, dgrp_ref, dlg_ref, dgn_ref, dst):
        del dp_any
        cc = pl.program_id(1)

        @pl.when(cc == 0)
        def _():
            dst[...] = jnp.zeros_like(dst)
            dgn_ref[...] = jnp.zeros_like(dgn_ref)

        q, k, v, r, bl, eb, enb, ebl, qe, ke, kd, causal, att = _gla_chunk(grp_ref[...], g_ref[...])
        s_t = st_ref[...]
        d_st = dst[...]
        o = o_ref[...]
        dzc_v = dzc_ref[...]
        rstd = lax.rsqrt(jnp.mean(o * o, axis=-1, keepdims=True) + EPS)
        xhat = o * rstd
        sr = _sigmoid(r)
        dr = dzc_v * (xhat * gn_ref[...]) * (sr * (1.0 + r * (1.0 - sr)))
        docn = dzc_v * (r * sr)
        dgn_ref[...] += jnp.sum(docn * xhat, axis=0, keepdims=True)
        dxh = docn * gn_ref[...]
        do = rstd * (dxh - xhat * jnp.mean(dxh * xhat, axis=-1, keepdims=True))
        dob, vb = do.astype(BF16), v.astype(BF16)
        qeb, keb, kdb = qe.astype(BF16), ke.astype(BF16), kd.astype(BF16)
        datt = jnp.where(causal, _dot_nt(dob, vb), 0.0).astype(BF16)
        dv = _dot_tn(att.astype(BF16), dob) + _dot_nt(kdb, d_st.astype(BF16))
        dqe = _dot(datt, keb) + _dot(dob, s_t.astype(BF16))
        dke = _dot_tn(datt, qeb)
        dkd = _dot(vb, d_st.astype(BF16))
        dq = dqe * eb * (GLA_DK ** -0.5)
        dk = dke * enb + dkd * ebl
        kd_dkd = dkd * kd
        db = dqe * qe - dke * ke - kd_dkd
        db_last = (jnp.sum(kd_dkd, axis=0, keepdims=True)
                   + jnp.exp(bl) * jnp.sum(s_t * d_st, axis=0, keepdims=True))
        dlg_ref[...] = _dot_exact(_tri(BLK, False), db) + db_last
        dst[...] = d_st * jnp.exp(bl) + _dot_tn(dob, qeb)
        dgrp_ref[...] = jnp.concatenate([dq, dk, dv, dr], axis=1).astype(BF16)

    rev = lambda c: nc - 1 - c
    vspec = pl.BlockSpec((BLK, GLA_DV), lambda h, c: (rev(c), h))
    return pl.pallas_call(
        kern, grid=(GLA_H, nc),
        in_specs=[pl.BlockSpec((BLK, GLA_GRP), lambda h, c: (rev(c), R_GLA_BLK0 + h)),
                  pl.BlockSpec((BLK, GLA_DK), lambda h, c: (rev(c), h)),
                  pl.BlockSpec((None, None, GLA_DV, GLA_DK), lambda h, c: (h, rev(c), 0, 0)),
                  vspec, vspec, pl.BlockSpec((1, GLA_DV), lambda h, c: (0, h)),
                  pl.BlockSpec(memory_space=pl.ANY)],
        out_specs=[pl.BlockSpec((BLK, GLA_GRP), lambda h, c: (rev(c), F_GLA_BLK0 + h)),
                   pl.BlockSpec((BLK, GLA_DK), lambda h, c: (rev(c), h)),
                   pl.BlockSpec((1, GLA_DV), lambda h, c: (0, h))],
        out_shape=[jax.ShapeDtypeStruct(dproj.shape, BF16), jax.ShapeDtypeStruct((lp, GLA_H * GLA_DK), F32),
                   jax.ShapeDtypeStruct((1, GLA_H * GLA_DV), F32)],
        scratch_shapes=[pltpu.VMEM((GLA_DV, GLA_DK), F32)],
        input_output_aliases={6: 0},
        compiler_params=_cparams("parallel", "arbitrary"), name="gla_bwd",
    )(proj_r, logg, st_all, o_all, dzc, gn, dproj)


def _small_bwd(proj_r, dlogg, wg2p, wg2pt, bg, dfa, dproj, tr=640):
    lp = proj_r.shape[0]
    tr = _row_tile(lp, tr)
    w = GLA_H * GLA_DK

    def kern(s_ref, dlg_ref, w_ref, wt_ref, b_ref, dfa_ref, dp_any, ds_ref, dbg_ref, dw_ref):
        del dp_any
        i = pl.program_id(0)
        sb = s_ref[...].astype(BF16)
        zg = _dot(sb, w_ref[...]) + b_ref[...]
        dzg = jnp.where(_valid_rows(i * tr, tr), dlg_ref[...] * (1.0 / GLA_TAU) * _sigmoid(-zg), 0.0)

        @pl.when(i == 0)
        def _():
            dbg_ref[...] = jnp.zeros_like(dbg_ref)
            dw_ref[...] = jnp.zeros_like(dw_ref)

        dbg_ref[...] += jnp.sum(dzg, axis=0, keepdims=True)
        dzb = dzg.astype(BF16)
        dw_ref[...] += _dot_tn(sb, dzb)
        dsm = _dot(dzb, wt_ref[...]) + dfa_ref[...]
        ds_ref[...] = jnp.concatenate([dsm, jnp.zeros((tr, SMALL_W - BLK), F32)], axis=1).astype(BF16)

    return pl.pallas_call(
        kern, grid=(lp // tr,),
        in_specs=[pl.BlockSpec((tr, BLK), lambda i: (i, R_SMALL_BLK128)), pl.BlockSpec((tr, w), lambda i: (i, 0)),
                  pl.BlockSpec((BLK, w), lambda i: (0, 0)), pl.BlockSpec((w, BLK), lambda i: (0, 0)),
                  pl.BlockSpec((1, w), lambda i: (0, 0)), pl.BlockSpec((tr, BLK), lambda i: (i, 0)),
                  pl.BlockSpec(memory_space=pl.ANY)],
        out_specs=[pl.BlockSpec((tr, SMALL_W), lambda i: (i, F_SMALL_BLK0)), pl.BlockSpec((1, w), lambda i: (0, 0)),
                   pl.BlockSpec((BLK, w), lambda i: (0, 0))],
        out_shape=[jax.ShapeDtypeStruct(dproj.shape, BF16), jax.ShapeDtypeStruct((1, w), F32),
                   jax.ShapeDtypeStruct((BLK, w), F32)],
        input_output_aliases={6: 0},
        compiler_params=_cparams("arbitrary"), name="small_bwd",
    )(proj_r, dlogg, wg2p, wg2pt, bg, dfa, dproj)


def _merge_fwd(proj_r, gate_b3, ya, yb, yc, tr=640):
    lp = proj_r.shape[0]
    tr = _row_tile(lp, tr)
    tn = GATE_TN

    def kern(g_ref, b_ref, ya_ref, yb_ref, yc_ref, o_ref):
        g = g_ref[...]
        mix = (_sigmoid(g[:, :tn] + b_ref[0:1, :]) * ya_ref[...]
               + _sigmoid(g[:, tn:2 * tn] + b_ref[1:2, :]) * yb_ref[...]
               + _sigmoid(g[:, 2 * tn:] + b_ref[2:3, :]) * yc_ref[...])
        o_ref[...] = mix.astype(BF16)

    y = pl.BlockSpec((tr, tn), lambda i, j: (i, j))
    return pl.pallas_call(
        kern, grid=(lp // tr, D // tn),
        in_specs=[pl.BlockSpec((tr, 3 * tn), lambda i, j: (i, R_GATE_BLK0 + j)),
                  pl.BlockSpec((3, tn), lambda i, j: (0, j)), y, y, y],
        out_specs=y, out_shape=jax.ShapeDtypeStruct((lp, D), BF16),
        compiler_params=_cparams("parallel", "parallel"), name="merge_fwd")(proj_r, gate_b3, ya, yb, yc)


def _merge_bwd(proj_r, gate_b3, ya, yb, yc, dmix, tr=640):
    lp = proj_r.shape[0]
    tr = _row_tile(lp, tr)
    tn = GATE_TN

    def kern(g_ref, b_ref, ya_ref, yb_ref, yc_ref, dm_ref, dya_ref, dyb_ref, dyc_ref, dg_ref, db_ref):
        i = pl.program_id(1)
        g = g_ref[...]
        dm = dm_ref[...]

        @pl.when(i == 0)
        def _():
            db_ref[...] = jnp.zeros_like(db_ref)

        dgs = []
        for n, (y_ref, dy_ref) in enumerate(((ya_ref, dya_ref), (yb_ref, dyb_ref), (yc_ref, dyc_ref))):
            s = _sigmoid(g[:, n * tn:(n + 1) * tn] + b_ref[n:n + 1, :])
            dy_ref[...] = (dm * s).astype(BF16)
            dgn = dm * y_ref[...] * (s * (1.0 - s))
            db_ref[n:n + 1, :] += jnp.sum(dgn, axis=0, keepdims=True)
            dgs.append(dgn)
        dg_ref[...] = jnp.concatenate(dgs, axis=1).astype(BF16)

    y = pl.BlockSpec((tr, tn), lambda j, i: (i, j))
    bspec = pl.BlockSpec((3, tn), lambda j, i: (0, j))
    return pl.pallas_call(
        kern, grid=(D // tn, lp // tr),
        in_specs=[pl.BlockSpec((tr, 3 * tn), lambda j, i: (i, R_GATE_BLK0 + j)), bspec, y, y, y, y],
        out_specs=[y, y, y, pl.BlockSpec((tr, 3 * tn), lambda j, i: (i, F_GATE_BLK0 + j)), bspec],
        out_shape=[jax.ShapeDtypeStruct((lp, D), BF16)] * 3
        + [jax.ShapeDtypeStruct((lp, NP), BF16), jax.ShapeDtypeStruct((3, D), F32)],
        compiler_params=_cparams("parallel", "arbitrary"), name="merge_bwd")(proj_r, gate_b3, ya, yb, yc, dmix)


def _final_loss(h, gf, tgt):
    lp = h.shape[0]
    nb = lp // BLK

    def kern(h_ref, g_ref, t_ref, dh_ref, dg_ref, ls_ref):
        i = pl.program_id(0)

        @pl.when(i == 0)
        def _():
            dh_ref[...] = jnp.zeros_like(dh_ref)
            dg_ref[...] = jnp.zeros_like(dg_ref)
            ls_ref[...] = jnp.zeros_like(ls_ref)

        @pl.when(i > 0)
        def _():
            x = h_ref[...]
            r = lax.rsqrt(jnp.mean(x * x, axis=-1, keepdims=True) + EPS)
            xhat = x * r
            err = xhat * g_ref[...] - t_ref[...]
            ls_ref[...] += jnp.sum(jnp.sum(err * err, axis=0, keepdims=True), axis=1, keepdims=True)
            dy = err * (1.0 / D)
            dg_ref[...] += jnp.sum(dy * xhat, axis=0, keepdims=True)
            dxh = dy * g_ref[...]
            dh_ref[...] = r * (dxh - xhat * jnp.mean(dxh * xhat, axis=-1, keepdims=True))

    row = pl.BlockSpec((BLK, D), lambda i: (i, 0))
    vec = pl.BlockSpec((1, D), lambda i: (0, 0))
    return pl.pallas_call(
        kern, grid=(nb,),
        in_specs=[row, vec, pl.BlockSpec((BLK, D), lambda i: (jnp.maximum(i - 1, 0), 0))],
        out_specs=[row, vec, pl.BlockSpec((1, 1), lambda i: (0, 0))],
        out_shape=[jax.ShapeDtypeStruct((lp, D), F32), jax.ShapeDtypeStruct((1, D), F32),
                   jax.ShapeDtypeStruct((1, 1), F32)],
        compiler_params=_cparams("arbitrary"), name="final_loss")(h, gf, tgt)


def _gate_cols(c):
    ct = c[:, :FOX_H].T
    ck = jnp.where(jnp.arange(ct.shape[1]) < PAD, KEY_PAD_BIAS, ct)
    return ct[:, :, None], ck[:, None, :]


def _run(hosts, name, ctx, fn):
    if hosts and name in hosts:
        make, done = hosts[name]
        res = fn(make(ctx))
        done(res[-1])
    else:
        res = fn(None)
    return res[:-1]


def _mm_nn_x(a, b, exch, **kw):
    out = _mm_nn(a, b, exch=exch, **kw)
    return out if exch is not None else (out, None)


def _layer_fwd(h, w, hosts=None):
    xn = _rmsnorm_fwd(h, w["norm1_g"])
    proj_a = _mm_nn(xn, w["w_in"], n0=0, n=REST0, out_dtype=BF16, name="proj_a")
    proj_r, = _run(hosts, "proj_r", w, lambda e: _mm_nn_x(xn, w["w_in"], e, n0=REST0, n=NREST, name="proj_r"))
    cq, ck = _gate_cols(_fox_gate_fwd(proj_r, w["bf128"]))
    oa, lse = _run(hosts, "attn_fwd", w, lambda e: _attn_fwd(proj_a, cq, ck, exch=e))
    zb = _convb_fwd(proj_r, w["conv_w"])
    logg = _gla_gate_fwd(proj_r, w["wg2p"], w["gla_b_g"])
    o_gla, zc, st_all = _gla_fwd(proj_r, logg, w["gla_norm_g"])
    ya = _mm_nn(oa, w["w_a_o"], name="branch_a")
    yb = _mm_nn(zb, w["w_b_o"], name="branch_b")
    yc = _mm_nn(zc, w["w_c_o"], name="branch_c")
    mix = _merge_fwd(proj_r, w["gate_b3"], ya, yb, yc)
    h1 = _mm_nn(mix, w["w_o"], res=h, name="out_proj")
    xn2 = _rmsnorm_fwd(h1, w["norm2_g"])
    z, = _run(hosts, "up_proj", w, lambda e: _mm_nn_x(xn2, w["w_up"], e, name="up_proj"))
    a = _mlp_gate_fwd(z, w["mlp_conv_w"])
    h2, = _run(hosts, "down_proj", w,
               lambda e: _mm_nn_x(a, w["w_down"], e, res=h1, tk=D_FF // 4, name="down_proj"))
    saved = dict(h=h, xn=xn, proj_a=proj_a, proj_r=proj_r, cq=cq, ck=ck, oa=oa, lse=lse, zb=zb, logg=logg,
                 o_gla=o_gla, zc=zc, st_all=st_all, ya=ya, yb=yb, yc=yc, mix=mix, h1=h1, xn2=xn2, z=z, a=a)
    return h2, saved


def _layer_bwd(dh2, w, s, hosts=None):
    g = {}
    da = _mm_nt(dh2, w["w_down"], tn=D_FF // 4, name="d_down_in")
    g["w_down"] = _mm_tn(s["a"], dh2, tk=D_FF // 4, name="d_w_down")
    dzg, dzu, dmw_g, dmw_u = _run(hosts, "mlp_gate_bwd", g, lambda e: _mlp_gate_bwd(
        s["z"], da, w["mlp_conv_w"], exch=e))
    g["mlp_conv_w"] = jnp.concatenate([dmw_g, dmw_u], axis=1)
    dxn2 = _mm_nt(dzg, w["w_up"], k0=0, kw=D_FF, name="d_up_in_g")
    dxn2 = _mm_nt(dzu, w["w_up"], k0=D_FF, kw=D_FF, add=dxn2, name="d_up_in_u")
    g["w_up"] = jnp.concatenate([_mm_tn(s["xn2"], dzg, name="d_w_up_g"), _mm_tn(s["xn2"], dzu, name="d_w_up_u")], axis=1)
    dh1, g["norm2_g"] = _rmsnorm_bwd(s["h1"], w["norm2_g"], dxn2, dh2)
    dmix = _mm_nt(dh1, w["w_o"], name="d_out_proj_in")
    g["w_o"] = _mm_tn(s["mix"], dh1, name="d_w_o")
    dya, dyb, dyc, dproj, g["gate_b3"] = _merge_bwd(s["proj_r"], w["gate_b3"], s["ya"], s["yb"], s["yc"], dmix)
    doa = _mm_nt(dya, w["w_a_o"], out_dtype=BF16, name="d_branch_a_in")
    g["w_a_o"] = _mm_tn(s["oa"], dya, name="d_w_a_o")
    dzb = _mm_nt(dyb, w["w_b_o"], name="d_branch_b_in")
    g["w_b_o"] = _mm_tn(s["zb"], dyb, name="d_w_b_o")
    dzc = _mm_nt(dyc, w["w_c_o"], name="d_branch_c_in")
    g["w_c_o"] = _mm_tn(s["zc"], dyc, name="d_w_c_o")
    dproj, dlogg, g["gla_norm_g"] = _gla_bwd(s["proj_r"], s["logg"], s["st_all"], s["o_gla"], dzc, w["gla_norm_g"], dproj)
    dproj, g["conv_w"] = _convb_bwd(s["proj_r"], dzb, w["conv_w"], dproj)
    dproj, dcq, delta = _run(hosts, "attn_bwd_dq", g, lambda e: _attn_bwd_dq(
        s["proj_a"], doa, s["oa"], s["lse"], s["cq"], s["ck"], dproj, exch=e))
    dproj, dck = _run(hosts, "attn_bwd_dkv", g, lambda e: _attn_bwd_dkv(
        s["proj_a"], doa, s["lse"], delta, s["cq"], s["ck"], dproj, exch=e))
    dc = jnp.pad((dcq[:, :, 0] + dck[:, 0, :]).T, ((0, 0), (0, BLK - FOX_H)))
    dfa, g["bf128"] = _fox_gate_bwd(s["proj_r"], dc, w["bf128"])
    dproj, g["gla_b_g"], g["wg2p"] = _small_bwd(s["proj_r"], dlogg, w["wg2p"], w["wg2p"].T, w["gla_b_g"], dfa, dproj)
    def pair(out, e):
        return out if e is not None else (out, None)

    g["w_in"], = _run(hosts, "d_w_in", g, lambda e: pair(_mm_tn(s["xn"], dproj, name="d_w_in", exch=e), e))
    dxn, = _run(hosts, "d_in_proj_in", g, lambda e: pair(_mm_nt(dproj, w["w_in"], name="d_in_proj_in", exch=e), e))
    dh0, g["norm1_g"] = _rmsnorm_bwd(s["h"], w["norm1_g"], dxn, dh1)
    return dh0, g


def _local_step(x, tgt, meta, final_g, layers, hosts_fwd=None, hosts_bwd=None):
    h = jnp.concatenate([jnp.zeros((PAD, D), F32), meta, x], axis=0)
    saved = []
    for l, w in enumerate(layers):
        h, s = _layer_fwd(h, w, hosts_fwd[l] if hosts_fwd else None)
        saved.append(s)
    dh, dgf, sq = _final_loss(h, final_g, tgt)
    grads = [None] * len(layers)
    for l in reversed(range(len(layers))):
        dh, grads[l] = _layer_bwd(dh, layers[l], saved[l], hosts_bwd[l](grads) if hosts_bwd else None)
    return sq[0, 0], dh[BLK:], dh[PAD:BLK], dgf, grads


def _w_in_to_kernel(w_nat):
    parts = [w_nat[:, s:s + n] for s, n in _segments()]
    parts.append(jnp.zeros((w_nat.shape[0], SMALL_W - 8 - GLA_R), w_nat.dtype))
    return jnp.concatenate(parts, axis=1)


def _w_in_from_kernel(w_k):
    pieces, off = [], 0
    for s, n in _segments():
        pieces.append((s, w_k[:, off:off + n]))
        off += n
    return jnp.concatenate([p for _, p in sorted(pieces, key=lambda t: t[0])], axis=1)


def _pad_rows_at(a, row0, nrows):
    return jnp.pad(a, ((row0, nrows - row0 - a.shape[0]), (0, 0)))


def _big_to_kernel(name, full):
    return _w_in_to_kernel(full) if name == "w_in" else full


def _layer_weights(big, conv_w, gla_w_g2, mlp_conv_w, norm1_g, fox_b_f, gate_b, gla_b_g, gla_norm_g, norm2_g):
    w = {n: _big_to_kernel(n, a) for n, a in big.items()}
    w.update(
        conv_w=conv_w, mlp_conv_w=mlp_conv_w,
        wg2p=_pad_rows_at(gla_w_g2, 8, BLK).astype(BF16),
        norm1_g=norm1_g[None], norm2_g=norm2_g[None], gla_b_g=gla_b_g[None], gla_norm_g=gla_norm_g[None],
        bf128=jnp.pad(fox_b_f, (0, BLK - FOX_H))[None], gate_b3=gate_b.reshape(3, D))
    return w


def _layer_grads_natural(g):
    return dict(
        w_in=_w_in_from_kernel(g["w_in"]), w_a_o=g["w_a_o"], w_b_o=g["w_b_o"], w_c_o=g["w_c_o"], w_o=g["w_o"],
        w_up=g["w_up"], w_down=g["w_down"], conv_w=g["conv_w"], mlp_conv_w=g["mlp_conv_w"],
        gla_w_g2=g["wg2p"][8:8 + GLA_R], norm1_g=g["norm1_g"][0], norm2_g=g["norm2_g"][0],
        gla_b_g=g["gla_b_g"][0], gla_norm_g=g["gla_norm_g"][0], fox_b_f=g["bf128"][0, :FOX_H],
        gate_b=g["gate_b3"].reshape(3 * D))


def _adamw(recv, w, m, v, layer, prev=None, name="adamw"):
    n_slot, r, c = recv.shape
    lyr = w.shape[0]
    tr = r
    for t in range(16, r, 16):
        if r % t == 0 and t * c <= ADAMW_BLOCK_ELEMS:
            tr = t
    if r * c <= ADAMW_BLOCK_ELEMS:
        tr = r
    bc1, bc2 = 1.0 - ADAM_B1 ** ADAM_STEP, 1.0 - ADAM_B2 ** ADAM_STEP

    def kern(*refs):
        r_ref, w_ref, m_ref, v_ref = refs[:4]
        g_out, d_out, m_out, v_out = refs[-4:]
        g = r_ref[0].astype(F32)
        for sidx in range(1, n_slot):
            g = g + r_ref[sidx].astype(F32)
        m_new = ADAM_B1 * m_ref[...] + (1.0 - ADAM_B1) * g
        v_new = ADAM_B2 * v_ref[...] + (1.0 - ADAM_B2) * (g * g)
        g_out[...] = g
        m_out[...] = m_new
        v_out[...] = v_new
        d_out[...] = -ADAM_LR * ((m_new / bc1) / (jnp.sqrt(v_new / bc2) + ADAM_EPS) + ADAM_WD * w_ref[...])

    lspec = pl.BlockSpec((None, tr, c), lambda i: (layer, i, 0))
    in_specs = [pl.BlockSpec((n_slot, tr, c), lambda i: (0, i, 0)), lspec, lspec, lspec]
    args = [recv, w, m, v]
    aliases = {}
    if prev is not None:
        in_specs += [pl.BlockSpec(memory_space=pl.ANY)] * 4
        args += list(prev)
        aliases = {4: 0, 5: 1, 6: 2, 7: 3}
    return pl.pallas_call(
        kern, grid=(r // tr,), in_specs=in_specs, out_specs=[lspec] * 4,
        out_shape=[jax.ShapeDtypeStruct((lyr, r, c), F32)] * 4, input_output_aliases=aliases,
        compiler_params=_cparams("parallel"), name=name)(*args)


_BIG = ("w_in", "w_a_o", "w_b_o", "w_c_o", "w_o", "w_up", "w_down")
_COL_SHARDED = ("w_in", "w_a_o", "w_b_o", "w_c_o", "w_up", "conv_w", "gla_w_g2", "mlp_conv_w")
_REPL = ("norm1_g", "fox_b_f", "gate_b", "gla_b_g", "gla_norm_g", "norm2_g")


def _cols_from_slots(a):
    return jnp.transpose(a, (1, 0, 2)).reshape(a.shape[1], N_DEV * a.shape[2])


def _cols_to_slots(a):
    r, c8 = a.shape
    return jnp.transpose(a.reshape(r, N_DEV, c8 // N_DEV), (1, 0, 2))


def _rows_to_slots(a):
    return a.reshape(N_DEV, a.shape[0] // N_DEV, a.shape[1])


def kernel(x, meta_tokens, norm1_g, w_in, fox_b_f, gate_b, conv_w, gla_w_g2, gla_b_g, gla_norm_g, w_a_o, w_b_o, w_c_o, w_o, norm2_g, w_up, mlp_conv_w, w_down, final_norm_g, loss_target, m_meta_tokens, m_norm1_g, m_w_in, m_fox_b_f, m_gate_b, m_conv_w, m_gla_w_g2, m_gla_b_g, m_gla_norm_g, m_w_a_o, m_w_b_o, m_w_c_o, m_w_o, m_norm2_g, m_w_up, m_mlp_conv_w, m_w_down, m_final_norm_g, v_meta_tokens, v_norm1_g, v_w_in, v_fox_b_f, v_gate_b, v_conv_w, v_gla_w_g2, v_gla_b_g, v_gla_norm_g, v_w_a_o, v_w_b_o, v_w_c_o, v_w_o, v_norm2_g, v_w_up, v_mlp_conv_w, v_w_down, v_final_norm_g):
    names = ("meta_tokens", "norm1_g", "w_in", "fox_b_f", "gate_b", "conv_w", "gla_w_g2", "gla_b_g", "gla_norm_g",
             "w_a_o", "w_b_o", "w_c_o", "w_o", "norm2_g", "w_up", "mlp_conv_w", "w_down", "final_norm_g")
    wts = dict(zip(names, (meta_tokens, norm1_g, w_in, fox_b_f, gate_b, conv_w, gla_w_g2, gla_b_g, gla_norm_g,
                           w_a_o, w_b_o, w_c_o, w_o, norm2_g, w_up, mlp_conv_w, w_down, final_norm_g)))
    mom = dict(zip(names, (m_meta_tokens, m_norm1_g, m_w_in, m_fox_b_f, m_gate_b, m_conv_w, m_gla_w_g2, m_gla_b_g,
                           m_gla_norm_g, m_w_a_o, m_w_b_o, m_w_c_o, m_w_o, m_norm2_g, m_w_up, m_mlp_conv_w, m_w_down,
                           m_final_norm_g)))
    var = dict(zip(names, (v_meta_tokens, v_norm1_g, v_w_in, v_fox_b_f, v_gate_b, v_conv_w, v_gla_w_g2, v_gla_b_g,
                           v_gla_norm_g, v_w_a_o, v_w_b_o, v_w_c_o, v_w_o, v_norm2_g, v_w_up, v_mlp_conv_w, v_w_down,
                           v_final_norm_g)))

    small = _exchange([conv_w, gla_w_g2, mlp_conv_w, meta_tokens], [True] * 4, "gather_small")
    conv_full = jnp.transpose(small[0], (1, 2, 0, 3)).reshape(DEPTH, 3, CONV_CH)
    g2_full = jnp.transpose(small[1], (1, 2, 0, 3)).reshape(DEPTH, GLA_R, GLA_H * GLA_DK)
    mconv_full = jnp.transpose(small[2], (1, 2, 0, 3)).reshape(DEPTH, 3, 2 * D_FF)
    meta_full = _cols_from_slots(small[3])
    layers = [_layer_weights({}, conv_full[l], g2_full[l], mconv_full[l], norm1_g[l], fox_b_f[l], gate_b[l],
                             gla_b_g[l], gla_norm_g[l], norm2_g[l]) for l in range(DEPTH)]

    def gather(l, which):
        def make(_):
            return [wts[n][l].astype(BF16) for n in which], [True] * len(which)

        def done(got):
            for n, a in zip(which, got):
                full = _cols_from_slots(a) if n in _COL_SHARDED else a.reshape(-1, a.shape[-1])
                layers[l][n] = _big_to_kernel(n, full)

        return make, done

    recv_big = [dict() for _ in range(DEPTH)]

    def scatter(l, which, grads_of):
        def make(ctx):
            g = grads_of(ctx)
            nat = [_w_in_from_kernel(g[n]) if n == "w_in" else g[n] for n in which]
            return ([_cols_to_slots(a) if n in _COL_SHARDED else _rows_to_slots(a) for n, a in zip(which, nat)],
                    [False] * len(which))

        def done(got):
            recv_big[l].update(zip(which, got))

        return make, done

    mixers = ("w_o", "w_a_o", "w_b_o", "w_c_o")
    early = ("w_down", "w_up") + mixers
    make, done = gather(0, ("w_in",))
    done(_exchange(*make(None), "gather_w_in"))
    hosts_fwd = [
        {"proj_r": gather(0, mixers + ("w_down",)), "attn_fwd": gather(0, ("w_up",)),
         "up_proj": gather(1, ("w_in",)), "down_proj": gather(1, mixers)},
        {"attn_fwd": gather(1, ("w_up", "w_down"))}]
    hosts_bwd = [
        lambda grads: {"mlp_gate_bwd": scatter(1, ("w_in",), lambda _: grads[1]),
                       "attn_bwd_dq": scatter(1, ("w_up", "w_down"), lambda _: grads[1]),
                       "attn_bwd_dkv": scatter(1, mixers, lambda _: grads[1]),
                       "d_w_in": scatter(0, early, lambda g: g),
                       "d_in_proj_in": scatter(0, ("w_in",), lambda g: g)},
        lambda grads: None]

    sq, grad_x, dmeta, dgf, grads_k = _local_step(x[0], loss_target[0], meta_full, final_norm_g[None], layers,
                                                  hosts_fwd, hosts_bwd)
    loss = lax.psum(sq * (0.5 / D), ("x", "y", "c"))
    grads = [_layer_grads_natural(g) for g in grads_k]

    out_g, out_d, out_m, out_v = {}, {}, {}, {}

    def update(name, recv, layer, lyr_shape, prev):
        w3, m3, v3 = (t[name].reshape(lyr_shape) for t in (wts, mom, var))
        return _adamw(recv.reshape((recv.shape[0],) + lyr_shape[1:]), w3, m3, v3, layer, prev, name="adamw_" + name)

    def store(name, res):
        shape = wts[name].shape
        out_g[name], out_d[name], out_m[name], out_v[name] = (t.reshape(shape) for t in res)

    for n in _BIG:
        res = None
        for l in range(DEPTH):
            res = update(n, recv_big[l][n], l, wts[n].shape, res)
        store(n, res)

    def stack_layers(name):
        return jnp.stack([grads[l][name] for l in range(DEPTH)])

    s_conv = jnp.transpose(stack_layers("conv_w").reshape(DEPTH, 3, N_DEV, -1), (2, 0, 1, 3))
    s_g2 = jnp.transpose(stack_layers("gla_w_g2").reshape(DEPTH, GLA_R, N_DEV, -1), (2, 0, 1, 3))
    s_mconv = jnp.transpose(stack_layers("mlp_conv_w").reshape(DEPTH, 3, N_DEV, -1), (2, 0, 1, 3))
    s_meta = _cols_to_slots(dmeta)
    repl = [stack_layers(n) for n in _REPL] + [dgf]
    pack = jnp.concatenate([jnp.pad(a.reshape(-1), (0, (-a.size) % 1024)) for a in repl]).reshape(-1, BLK)
    r_conv, r_g2, r_mconv, r_meta, r_pack = _exchange(
        [s_conv, s_g2, s_mconv, s_meta, pack], [False, False, False, False, True], "scatter_small")
    store("conv_w", update("conv_w", r_conv, 0, (1, DEPTH * 3, CONV_CH // N_DEV), None))
    store("gla_w_g2", update("gla_w_g2", r_g2, 0, (1, DEPTH * GLA_R, GLA_H * GLA_DK // N_DEV), None))
    store("mlp_conv_w", update("mlp_conv_w", r_mconv, 0, (1, DEPTH * 3, 2 * D_FF // N_DEV), None))
    store("meta_tokens", update("meta_tokens", r_meta, 0, (1, N_META, D // N_DEV), None))
    off = 0
    for n, a in zip(_REPL + ("final_norm_g",), repl):
        rows = (a.size + 1023) // 1024 * 8
        part = r_pack[:, off:off + rows].reshape(N_DEV, -1)[:, :a.size]
        off += rows
        shape2 = (1, 1, a.size) if a.size % BLK else (1, a.size // BLK, BLK)
        store(n, update(n, part, 0, shape2, None))

    order = lambda d: [d[n] for n in names]
    return (loss, grad_x[None], *order(out_g), *order(out_d), *order(out_m), *order(out_v))
```

```python
import functools

import jax
import jax.numpy as jnp
from jax import lax
from jax.experimental import pallas as pl
from jax.experimental.pallas import tpu as pltpu

F32 = jnp.float32
BF16 = jnp.bfloat16

D = 2048
DEPTH = 2
N_META = 16
BLK = 128
PAD = BLK - N_META
EPS = 1e-6
NEG = -1e30

FOX_H, FOX_DH = 8, 128
FOX_W = FOX_H * FOX_DH
CONV_CH = 1024
GLA_H, GLA_DK, GLA_DV, GLA_R, GLA_TAU = 4, 128, 256, 16, 16.0
D_FF = 5632
N_IN = 15384
N_DEV = 8

ADAM_LR, ADAM_B1, ADAM_B2, ADAM_EPS, ADAM_WD, ADAM_STEP = 0.001, 0.9, 0.999, 1e-08, 0.01, 10

CONV_TC = 512
GATE_TN = 512
KV0 = 1024
REST0 = 3072
GLA_GRP = 768
SMALL_W = 1024
NP = 16384
NREST = NP - REST0
R_CONV_BLK0 = 0
R_GLA_BLK0 = (6144 - REST0) // GLA_GRP
R_GATE_BLK0 = (9216 - REST0) // (3 * GATE_TN)
R_SMALL_BLK128 = (15360 - REST0) // 128
F_CONV_BLK0 = 3072 // (3 * CONV_TC)
F_GLA_BLK0 = 6144 // GLA_GRP
F_GATE_BLK0 = 9216 // (3 * GATE_TN)
F_SMALL_BLK0 = 15360 // SMALL_W

VMEM_LIMIT = 56 * 1024 * 1024
ADAMW_BLOCK_ELEMS = 128 * 1024


def _segments():
    seg = [(0, 1024)]
    for h in range(FOX_H):
        seg += [(1024 + 128 * h, 128), (2048 + 128 * h, 128)]
    for j in range(CONV_CH // CONV_TC):
        seg += [(3080 + CONV_TC * j, CONV_TC), (4104 + CONV_TC * j, CONV_TC), (5128 + CONV_TC * j, CONV_TC)]
    for h in range(GLA_H):
        seg += [(6152 + 128 * h, 128), (6664 + 128 * h, 128), (7176 + 256 * h, 256), (8200 + 256 * h, 256)]
    for j in range(D // GATE_TN):
        seg += [(9240 + GATE_TN * j, GATE_TN), (11288 + GATE_TN * j, GATE_TN), (13336 + GATE_TN * j, GATE_TN)]
    seg += [(3072, 8), (9224, 16)]
    return seg


def _cparams(*sem):
    return pltpu.CompilerParams(dimension_semantics=sem, vmem_limit_bytes=VMEM_LIMIT)


def _row_tile(n, target):
    best = BLK
    t = BLK
    while t <= min(n, target):
        if n % t == 0:
            best = t
        t += BLK
    return best


def _sigmoid(x):
    return 1.0 / (1.0 + jnp.exp(-x))


def _log_sigmoid(x):
    return jnp.minimum(x, 0.0) - jnp.log(1.0 + jnp.exp(-jnp.abs(x)))


def _valid_rows(row0, n):
    return (row0 + lax.broadcasted_iota(jnp.int32, (n, 1), 0)) >= PAD


def _dot(a, b):
    return jnp.dot(a, b, preferred_element_type=F32)


def _dot_nt(a, b):
    return lax.dot_general(a, b, (((1,), (1,)), ((), ())), preferred_element_type=F32)


def _dot_tn(a, b):
    return lax.dot_general(a, b, (((0,), (0,)), ((), ())), preferred_element_type=F32)


def _exchange_copies(ins, outs, bcast, send_sems, recv_sems, local_sems):
    x, y, c = lax.axis_index("x"), lax.axis_index("y"), lax.axis_index("c")
    me = 4 * x + 2 * y + c
    local, sends, recvs = [], [], []
    for n in range(len(ins)):
        src_me = ins[n] if bcast[n] else ins[n].at[me]
        local.append(pltpu.make_async_copy(src_me, outs[n].at[me], local_sems.at[n]))
    for k in range(1, N_DEV):
        px = 1 - x if (k >> 2) & 1 else x
        py = 1 - y if (k >> 1) & 1 else y
        pc = 1 - c if k & 1 else c
        peer = 4 * px + 2 * py + pc
        for n in range(len(ins)):
            src = ins[n] if bcast[n] else ins[n].at[peer]

            def copy(dst_slot, src=src, n=n, k=k, to=(px, py, pc)):
                return pltpu.make_async_remote_copy(
                    src_ref=src, dst_ref=outs[n].at[dst_slot], send_sem=send_sems.at[n, k - 1],
                    recv_sem=recv_sems.at[n, k - 1], device_id=to, device_id_type=pl.DeviceIdType.MESH)

            sends.append(copy(me))
            recvs.append(copy(peer))
    return local, sends, recvs


def _exchange_start(copies):
    local, sends, _ = copies
    for cp in local + sends:
        cp.start()


def _exchange_wait(copies):
    local, sends, recvs = copies
    for cp in recvs:
        cp.wait_recv()
    for cp in sends:
        cp.wait_send()
    for cp in local:
        cp.wait()


def _exchange_shapes(arrays, bcast):
    return [jax.ShapeDtypeStruct(((N_DEV,) + a.shape) if b else a.shape, a.dtype) for a, b in zip(arrays, bcast)]


def _exchange_sems(n_arr):
    return [pltpu.SemaphoreType.DMA((n_arr, N_DEV - 1)), pltpu.SemaphoreType.DMA((n_arr, N_DEV - 1)),
            pltpu.SemaphoreType.DMA((n_arr,))]


def _exchange(arrays, bcast, name):
    n_arr = len(arrays)

    def body(*refs):
        copies = _exchange_copies(refs[:n_arr], refs[n_arr:2 * n_arr], bcast, *refs[2 * n_arr:])
        _exchange_start(copies)
        _exchange_wait(copies)

    hbm = pl.BlockSpec(memory_space=pltpu.HBM)
    return pl.pallas_call(
        body, out_shape=_exchange_shapes(arrays, bcast), in_specs=[hbm] * n_arr, out_specs=[hbm] * n_arr,
        scratch_shapes=_exchange_sems(n_arr),
        compiler_params=pltpu.CompilerParams(has_side_effects=True), name=name)(*arrays)


def _pcall(kern, *, grid, in_specs, out_specs, out_shape, scratch, sem, name, args, aliases=None, exch=None):
    params = pltpu.CompilerParams(dimension_semantics=sem, vmem_limit_bytes=VMEM_LIMIT,
                                  has_side_effects=exch is not None)
    kw = dict(grid=grid, compiler_params=params, name=name, input_output_aliases=aliases or {})
    if exch is None:
        out = pl.pallas_call(kern, in_specs=in_specs, out_specs=out_specs, out_shape=out_shape,
                             scratch_shapes=scratch, **kw)(*args)
        return out, None
    arrays, bcast = exch
    n_x, n_in, n_out, n_sc = len(arrays), len(in_specs), len(out_specs), len(scratch)

    def hosted(*refs):
        ins, x_in = refs[:n_in], refs[n_in:n_in + n_x]
        outs, x_out = refs[n_in + n_x:n_in + n_x + n_out], refs[n_in + n_x + n_out:n_in + 2 * n_x + n_out]
        sc, sems = refs[n_in + 2 * n_x + n_out:n_in + 2 * n_x + n_out + n_sc], refs[n_in + 2 * n_x + n_out + n_sc:]
        ids = [pl.program_id(d) for d in range(len(grid))]
        first = functools.reduce(jnp.logical_and, [i == 0 for i in ids])
        last = functools.reduce(jnp.logical_and, [i == g - 1 for i, g in zip(ids, grid)])

        @pl.when(first)
        def _():
            _exchange_start(_exchange_copies(x_in, x_out, bcast, *sems))

        kern(*ins, *outs, *sc)

        @pl.when(last)
        def _():
            _exchange_wait(_exchange_copies(x_in, x_out, bcast, *sems))

    hbm = pl.BlockSpec(memory_space=pltpu.HBM)
    out = pl.pallas_call(
        hosted, in_specs=list(in_specs) + [hbm] * n_x, out_specs=list(out_specs) + [hbm] * n_x,
        out_shape=list(out_shape) + _exchange_shapes(arrays, bcast),
        scratch_shapes=list(scratch) + _exchange_sems(n_x), **kw)(*args, *arrays)
    return out[:n_out], out[n_out:]


def _mm_nn(a, b, *, n0=0, n=None, out_dtype=F32, res=None, tm=1664, tn=512, tk=None, name="mm_nn", exch=None):
    m, k = a.shape
    n = b.shape[1] - n0 if n is None else n
    tm = _row_tile(m, tm)
    tk = k if tk is None else tk
    nk = k // tk
    assert k % tk == 0 and n % tn == 0 and n0 % tn == 0
    nb0 = n0 // tn

    def kern(*refs):
        if res is None:
            a_ref, b_ref, o_ref, acc = refs
        else:
            a_ref, b_ref, r_ref, o_ref, acc = refs
        kk = pl.program_id(2)
        row0 = pl.program_id(0) * tm

        def finish(prod):
            if res is None:
                o_ref[...] = prod.astype(out_dtype)
            else:
                o_ref[...] = (r_ref[...] + jnp.where(_valid_rows(row0, tm), prod, 0.0)).astype(out_dtype)

        if nk == 1:
            finish(_dot(a_ref[...].astype(BF16), b_ref[...].astype(BF16)))
            return

        @pl.when(kk == 0)
        def _():
            acc[...] = jnp.zeros_like(acc)

        acc[...] += _dot(a_ref[...].astype(BF16), b_ref[...].astype(BF16))

        @pl.when(kk == nk - 1)
        def _():
            finish(acc[...])

    in_specs = [pl.BlockSpec((tm, tk), lambda i, j, kk: (i, kk)),
                pl.BlockSpec((tk, tn), lambda i, j, kk: (kk, nb0 + j))]
    args = [a, b]
    if res is not None:
        in_specs.append(pl.BlockSpec((tm, tn), lambda i, j, kk: (i, j)))
        args.append(res)
    out, got = _pcall(
        kern, grid=(m // tm, n // tn, nk), in_specs=in_specs,
        out_specs=[pl.BlockSpec((tm, tn), lambda i, j, kk: (i, j))],
        out_shape=[jax.ShapeDtypeStruct((m, n), out_dtype)],
        scratch=[pltpu.VMEM((tm, tn) if nk > 1 else (8, 128), F32)],
        sem=("parallel", "parallel", "arbitrary"), name=name, args=args, exch=exch)
    return out[0] if exch is None else (out[0], got)


def _mm_nt(a, b, *, k0=0, kw=None, out_dtype=F32, add=None, tm=640, tn=None, tk=2048, name="mm_nt", exch=None):
    m = a.shape[0]
    kw = a.shape[1] if kw is None else kw
    nn = b.shape[0]
    tm = _row_tile(m, tm)
    tn = min(nn, 2048) if tn is None else tn
    tk = min(tk, kw)
    assert kw % tk == 0 and k0 % tk == 0 and nn % tn == 0 and a.shape[1] == kw
    nk = kw // tk
    kb0 = k0 // tk

    def kern(*refs):
        if add is None:
            a_ref, b_ref, o_ref, acc = refs
        else:
            a_ref, b_ref, d_ref, o_ref, acc = refs
        kk = pl.program_id(2)

        def finish(prod):
            o_ref[...] = (prod if add is None else prod + d_ref[...]).astype(out_dtype)

        if nk == 1:
            finish(_dot_nt(a_ref[...].astype(BF16), b_ref[...].astype(BF16)))
            return

        @pl.when(kk == 0)
        def _():
            acc[...] = jnp.zeros_like(acc)

        acc[...] += _dot_nt(a_ref[...].astype(BF16), b_ref[...].astype(BF16))

        @pl.when(kk == nk - 1)
        def _():
            finish(acc[...])

    in_specs = [pl.BlockSpec((tm, tk), lambda i, j, kk: (i, kk)),
                pl.BlockSpec((tn, tk), lambda i, j, kk: (j, kb0 + kk))]
    args = [a, b]
    if add is not None:
        in_specs.append(pl.BlockSpec((tm, tn), lambda i, j, kk: (i, j)))
        args.append(add)
    out, got = _pcall(
        kern, grid=(m // tm, nn // tn, nk), in_specs=in_specs,
        out_specs=[pl.BlockSpec((tm, tn), lambda i, j, kk: (i, j))],
        out_shape=[jax.ShapeDtypeStruct((m, nn), out_dtype)],
        scratch=[pltpu.VMEM((tm, tn) if nk > 1 else (8, 128), F32)],
        sem=("parallel", "parallel", "arbitrary"), name=name, args=args, exch=exch)
    return out[0] if exch is None else (out[0], got)


def _mm_tn(a, b, *, out_dtype=BF16, tm=1664, tk=None, tn=None, name="mm_tn", exch=None):
    m, k = a.shape
    n = b.shape[1]
    tm = _row_tile(m, tm)
    tk = k if tk is None else tk
    if tn is None:
        tn = 1024 if n % 1024 == 0 else 512
    assert k % tk == 0 and n % tn == 0
    nm = m // tm

    def kern(a_ref, b_ref, o_ref, acc):
        mm = pl.program_id(2)

        @pl.when(mm == 0)
        def _():
            acc[...] = jnp.zeros_like(acc)

        acc[...] += _dot_tn(a_ref[...].astype(BF16), b_ref[...].astype(BF16))

        @pl.when(mm == nm - 1)
        def _():
            o_ref[...] = acc[...].astype(out_dtype)

    out, got = _pcall(
        kern, grid=(k // tk, n // tn, nm),
        in_specs=[pl.BlockSpec((tm, tk), lambda i, j, mm: (mm, i)),
                  pl.BlockSpec((tm, tn), lambda i, j, mm: (mm, j))],
        out_specs=[pl.BlockSpec((tk, tn), lambda i, j, mm: (i, j))],
        out_shape=[jax.ShapeDtypeStruct((k, n), out_dtype)],
        scratch=[pltpu.VMEM((tk, tn), F32)],
        sem=("parallel", "parallel", "arbitrary"), name=name, args=(a, b), exch=exch)
    return out[0] if exch is None else (out[0], got)


def _rmsnorm_fwd(h, g, tr=640):
    lp = h.shape[0]
    tr = _row_tile(lp, tr)

    def kern(h_ref, g_ref, o_ref):
        x = h_ref[...]
        r = lax.rsqrt(jnp.mean(x * x, axis=-1, keepdims=True) + EPS)
        o_ref[...] = (x * r * g_ref[...]).astype(BF16)

    return pl.pallas_call(
        kern, grid=(lp // tr,),
        in_specs=[pl.BlockSpec((tr, D), lambda i: (i, 0)), pl.BlockSpec((1, D), lambda i: (0, 0))],
        out_specs=pl.BlockSpec((tr, D), lambda i: (i, 0)),
        out_shape=jax.ShapeDtypeStruct((lp, D), BF16),
        compiler_params=_cparams("parallel"), name="rmsnorm_fwd")(h, g)


def _rmsnorm_bwd(h, g, dxn, dres, tr=640):
    lp = h.shape[0]
    tr = _row_tile(lp, tr)

    def kern(h_ref, g_ref, dxn_ref, dres_ref, dh_ref, dg_ref):
        i = pl.program_id(0)
        x = h_ref[...]
        r = lax.rsqrt(jnp.mean(x * x, axis=-1, keepdims=True) + EPS)
        xhat = x * r
        dy = jnp.where(_valid_rows(i * tr, tr), dxn_ref[...], 0.0)

        @pl.when(i == 0)
        def _():
            dg_ref[...] = jnp.zeros_like(dg_ref)

        dg_ref[...] += jnp.sum(dy * xhat, axis=0, keepdims=True)
        dxh = dy * g_ref[...]
        dh_ref[...] = dres_ref[...] + r * (dxh - xhat * jnp.mean(dxh * xhat, axis=-1, keepdims=True))

    row = pl.BlockSpec((tr, D), lambda i: (i, 0))
    vec = pl.BlockSpec((1, D), lambda i: (0, 0))
    return pl.pallas_call(
        kern, grid=(lp // tr,), in_specs=[row, vec, row, row], out_specs=[row, vec],
        out_shape=[jax.ShapeDtypeStruct((lp, D), F32), jax.ShapeDtypeStruct((1, D), F32)],
        compiler_params=_cparams("arbitrary"), name="rmsnorm_bwd")(h, g, dxn, dres)


def _shift_down(xe, k):
    return xe if k == 0 else pltpu.roll(xe, k, 0)


def _shift_up(xe, k):
    return xe if k == 0 else pltpu.roll(xe, xe.shape[0] - k, 0)


def _conv_ext(xe, w_ref):
    return w_ref[2:3, :] * xe + w_ref[1:2, :] * _shift_down(xe, 1) + w_ref[0:1, :] * _shift_down(xe, 2)


def _halo_specs(tr, width, col_of, nrows, rows_first):
    r8 = tr // 8
    last8 = nrows // 8 - 1
    if rows_first:
        prev = pl.BlockSpec((8, width), lambda i, j: (jnp.maximum(i * r8 - 1, 0), col_of(j)))
        nxt = pl.BlockSpec((8, width), lambda i, j: (jnp.minimum((i + 1) * r8, last8), col_of(j)))
    else:
        prev = pl.BlockSpec((8, width), lambda j, i: (jnp.maximum(i * r8 - 1, 0), col_of(j)))
        nxt = pl.BlockSpec((8, width), lambda j, i: (jnp.minimum((i + 1) * r8, last8), col_of(j)))
    return prev, nxt


def _convb_fwd(proj_r, conv_w, tr=640):
    lp = proj_r.shape[0]
    tr = _row_tile(lp, tr)
    tc = CONV_TC
    gw = 3 * tc

    def kern(g_ref, gp_ref, w_ref, o_ref):
        i = pl.program_id(0)
        g = g_ref[...]
        p = g[:, tc:2 * tc] * g[:, 2 * tc:]
        gp = gp_ref[...]
        pp = jnp.where(i > 0, gp[:, tc:2 * tc] * gp[:, 2 * tc:], 0.0)
        y = _conv_ext(jnp.concatenate([pp, p], axis=0), w_ref)[8:]
        o_ref[...] = (g[:, :tc] * y).astype(BF16)

    prev, _ = _halo_specs(tr, gw, lambda j: R_CONV_BLK0 + j, lp, True)
    return pl.pallas_call(
        kern, grid=(lp // tr, CONV_CH // tc),
        in_specs=[pl.BlockSpec((tr, gw), lambda i, j: (i, R_CONV_BLK0 + j)), prev,
                  pl.BlockSpec((3, tc), lambda i, j: (0, j))],
        out_specs=pl.BlockSpec((tr, tc), lambda i, j: (i, j)),
        out_shape=jax.ShapeDtypeStruct((lp, CONV_CH), BF16),
        compiler_params=_cparams("parallel", "parallel"), name="convb_fwd")(proj_r, proj_r, conv_w)


def _convb_bwd(proj_r, dzb, conv_w, dproj, tr=640):
    lp = proj_r.shape[0]
    tr = _row_tile(lp, tr)
    nr = lp // tr
    tc = CONV_TC
    gw = 3 * tc

    def kern(g_ref, gp_ref, gn_ref, dz_ref, dzn_ref, w_ref, dp_any, dg_ref, dw_ref):
        del dp_any
        i = pl.program_id(1)
        g = g_ref[...]
        b, c, hh = g[:, :tc], g[:, tc:2 * tc], g[:, 2 * tc:]
        p = c * hh
        gp = gp_ref[...]
        pp = jnp.where(i > 0, gp[:, tc:2 * tc] * gp[:, 2 * tc:], 0.0)
        pe = jnp.concatenate([pp, p], axis=0)
        s1 = _shift_down(pe, 1)[8:]
        s2 = _shift_down(pe, 2)[8:]
        y = w_ref[2:3, :] * p + w_ref[1:2, :] * s1 + w_ref[0:1, :] * s2
        dz = dz_ref[...]
        dy = dz * b
        dyn = jnp.where(i < nr - 1, dzn_ref[...] * gn_ref[...][:, :tc], 0.0)
        dye = jnp.concatenate([dy, dyn], axis=0)
        dp = (w_ref[2:3, :] * dy + w_ref[1:2, :] * _shift_up(dye, 1)[:tr]
              + w_ref[0:1, :] * _shift_up(dye, 2)[:tr])
        valid = _valid_rows(i * tr, tr)
        dg_ref[...] = jnp.where(valid, jnp.concatenate([dz * y, dp * hh, dp * c], axis=1), 0.0).astype(BF16)

        @pl.when(i == 0)
        def _():
            dw_ref[...] = jnp.zeros_like(dw_ref)

        dw_ref[0:1, :] += jnp.sum(dy * s2, axis=0, keepdims=True)
        dw_ref[1:2, :] += jnp.sum(dy * s1, axis=0, keepdims=True)
        dw_ref[2:3, :] += jnp.sum(dy * p, axis=0, keepdims=True)

    gprev, gnext = _halo_specs(tr, gw, lambda j: R_CONV_BLK0 + j, lp, False)
    _, dznext = _halo_specs(tr, tc, lambda j: j, lp, False)
    return pl.pallas_call(
        kern, grid=(CONV_CH // tc, nr),
        in_specs=[pl.BlockSpec((tr, gw), lambda j, i: (i, R_CONV_BLK0 + j)), gprev, gnext,
                  pl.BlockSpec((tr, tc), lambda j, i: (i, j)), dznext,
                  pl.BlockSpec((3, tc), lambda j, i: (0, j)),
                  pl.BlockSpec(memory_space=pl.ANY)],
        out_specs=[pl.BlockSpec((tr, gw), lambda j, i: (i, F_CONV_BLK0 + j)),
                   pl.BlockSpec((3, tc), lambda j, i: (0, j))],
        out_shape=[jax.ShapeDtypeStruct(dproj.shape, BF16), jax.ShapeDtypeStruct((3, CONV_CH), F32)],
        input_output_aliases={6: 0},
        compiler_params=_cparams("parallel", "arbitrary"), name="convb_bwd",
    )(proj_r, proj_r, proj_r, dzb, dzb, conv_w, dproj)


MLP_TC = 256


def _mlp_gate_fwd(z, w, tr=640):
    lp = z.shape[0]
    tr = _row_tile(lp, tr)
    tc = 512
    nc = D_FF // tc

    def kern(zg_ref, zgp_ref, zu_ref, zup_ref, wg_ref, wu_ref, o_ref):
        i = pl.program_id(0)
        zge = jnp.concatenate([jnp.where(i > 0, zgp_ref[...], 0.0), zg_ref[...]], axis=0)
        zue = jnp.concatenate([jnp.where(i > 0, zup_ref[...], 0.0), zu_ref[...]], axis=0)
        ug = _conv_ext(zge, wg_ref)[8:]
        uu = _conv_ext(zue, wu_ref)[8:]
        o_ref[...] = (ug * _sigmoid(ug) * uu).astype(BF16)

    gprev, _ = _halo_specs(tr, tc, lambda j: j, lp, True)
    uprev, _ = _halo_specs(tr, tc, lambda j: nc + j, lp, True)
    return pl.pallas_call(
        kern, grid=(lp // tr, nc),
        in_specs=[pl.BlockSpec((tr, tc), lambda i, j: (i, j)), gprev,
                  pl.BlockSpec((tr, tc), lambda i, j: (i, nc + j)), uprev,
                  pl.BlockSpec((3, tc), lambda i, j: (0, j)),
                  pl.BlockSpec((3, tc), lambda i, j: (0, nc + j))],
        out_specs=pl.BlockSpec((tr, tc), lambda i, j: (i, j)),
        out_shape=jax.ShapeDtypeStruct((lp, D_FF), BF16),
        compiler_params=_cparams("parallel", "parallel"), name="mlp_gate_fwd")(z, z, z, z, w, w)


def _mlp_gate_bwd(z, da, w, tr=640, exch=None):
    lp = z.shape[0]
    tr = _row_tile(lp, tr)
    nr = lp // tr
    tc = MLP_TC
    nc = D_FF // tc

    def kern(zg_ref, zgp_ref, zgn_ref, zu_ref, zup_ref, zun_ref, da_ref, dan_ref, wg_ref, wu_ref,
             dzg_ref, dzu_ref, dwg_ref, dwu_ref):
        i = pl.program_id(1)
        first, last = i == 0, i == nr - 1

        def ext(m_ref, p_ref, n_ref):
            return jnp.concatenate([jnp.where(first, 0.0, p_ref[...]), m_ref[...],
                                    jnp.where(last, 0.0, n_ref[...])], axis=0)

        zge, zue = ext(zg_ref, zgp_ref, zgn_ref), ext(zu_ref, zup_ref, zun_ref)
        ug = _conv_ext(zge, wg_ref)[8:]
        uu = _conv_ext(zue, wu_ref)[8:]
        dae = jnp.concatenate([da_ref[...], jnp.where(last, 0.0, dan_ref[...])], axis=0)
        sg = _sigmoid(ug)
        dug = dae * uu * (sg * (1.0 + ug * (1.0 - sg)))
        duu = dae * (ug * sg)
        valid = _valid_rows(i * tr, tr)

        @pl.when(first)
        def _():
            dwg_ref[...] = jnp.zeros_like(dwg_ref)
            dwu_ref[...] = jnp.zeros_like(dwu_ref)

        for du, ze, w_ref, dz_ref, dw_ref in ((dug, zge, wg_ref, dzg_ref, dwg_ref),
                                              (duu, zue, wu_ref, dzu_ref, dwu_ref)):
            dz = (w_ref[2:3, :] * du + w_ref[1:2, :] * _shift_up(du, 1) + w_ref[0:1, :] * _shift_up(du, 2))[:tr]
            dz_ref[...] = jnp.where(valid, dz, 0.0).astype(BF16)
            dum = du[:tr]
            for kk in range(3):
                dw_ref[kk:kk + 1, :] += jnp.sum(dum * _shift_down(ze, 2 - kk)[8:8 + tr], axis=0, keepdims=True)

    gprev, gnext = _halo_specs(tr, tc, lambda j: j, lp, False)
    uprev, unext = _halo_specs(tr, tc, lambda j: nc + j, lp, False)
    main = pl.BlockSpec((tr, tc), lambda j, i: (i, j))
    wspec = pl.BlockSpec((3, tc), lambda j, i: (0, j))
    out, got = _pcall(
        kern, grid=(nc, nr),
        in_specs=[main, gprev, gnext, pl.BlockSpec((tr, tc), lambda j, i: (i, nc + j)), uprev, unext,
                  main, gnext, wspec, pl.BlockSpec((3, tc), lambda j, i: (0, nc + j))],
        out_specs=[main, main, wspec, wspec],
        out_shape=[jax.ShapeDtypeStruct((lp, D_FF), BF16), jax.ShapeDtypeStruct((lp, D_FF), BF16),
                   jax.ShapeDtypeStruct((3, D_FF), F32), jax.ShapeDtypeStruct((3, D_FF), F32)],
        scratch=[], sem=("parallel", "arbitrary"), name="mlp_gate_bwd",
        args=(z, z, z, z, z, z, da, da, w, w), exch=exch)
    return (*out, got)


def _tri(n, lower):
    r = lax.broadcasted_iota(jnp.int32, (n, n), 0)
    c = lax.broadcasted_iota(jnp.int32, (n, n), 1)
    return jnp.where((c <= r) if lower else (c >= r), 1.0, 0.0).astype(F32)


def _dot_exact(a, b):
    return jnp.dot(a, b, preferred_element_type=F32, precision=lax.Precision.HIGHEST)


def _fox_gate_fwd(proj_r, bf128):
    lp = proj_r.shape[0]
    nb = lp // BLK

    def kern(s_ref, b_ref, c_ref):
        tri = _tri(BLK, True)

        def body(i, carry):
            rows = pl.ds(pl.multiple_of(i * BLK, BLK), BLK)
            lf = jnp.where(_valid_rows(i * BLK, BLK), _log_sigmoid(s_ref[rows, :] + b_ref[...]), 0.0)
            cs = _dot_exact(tri, lf) + carry
            c_ref[rows, :] = cs
            return cs[BLK - 1:BLK, :]

        lax.fori_loop(0, nb, body, jnp.zeros((1, BLK), F32))

    return pl.pallas_call(
        kern, grid=(1,),
        in_specs=[pl.BlockSpec((lp, BLK), lambda i: (0, R_SMALL_BLK128)), pl.BlockSpec((1, BLK), lambda i: (0, 0))],
        out_specs=pl.BlockSpec((lp, BLK), lambda i: (0, 0)),
        out_shape=jax.ShapeDtypeStruct((lp, BLK), F32),
        compiler_params=_cparams("arbitrary"), name="fox_gate_fwd")(proj_r, bf128)


def _fox_gate_bwd(proj_r, dc, bf128):
    lp = proj_r.shape[0]
    nb = lp // BLK

    def kern(s_ref, dc_ref, b_ref, dfa_ref, dbf_ref):
        tri = _tri(BLK, False)

        dbf_ref[...] = jnp.zeros_like(dbf_ref)

        def body(ii, run):
            i = nb - 1 - ii
            rows = pl.ds(pl.multiple_of(i * BLK, BLK), BLK)
            dcb = dc_ref[rows, :]
            suf = _dot_exact(tri, dcb) + run
            dfa = jnp.where(_valid_rows(i * BLK, BLK), suf * _sigmoid(-(s_ref[rows, :] + b_ref[...])), 0.0)
            dfa_ref[rows, :] = dfa
            dbf_ref[...] += jnp.sum(dfa, axis=0, keepdims=True)
            return run + jnp.sum(dcb, axis=0, keepdims=True)

        lax.fori_loop(0, nb, body, jnp.zeros((1, BLK), F32))

    return pl.pallas_call(
        kern, grid=(1,),
        in_specs=[pl.BlockSpec((lp, BLK), lambda i: (0, R_SMALL_BLK128)), pl.BlockSpec((lp, BLK), lambda i: (0, 0)),
                  pl.BlockSpec((1, BLK), lambda i: (0, 0))],
        out_specs=[pl.BlockSpec((lp, BLK), lambda i: (0, 0)), pl.BlockSpec((1, BLK), lambda i: (0, 0))],
        out_shape=[jax.ShapeDtypeStruct((lp, BLK), F32), jax.ShapeDtypeStruct((1, BLK), F32)],
        compiler_params=_cparams("arbitrary"), name="fox_gate_bwd")(proj_r, dc, bf128)


LOG2E = 1.4426950408889634
KEY_PAD_BIAS = 1e30


def _attn_logits2(q, k, ck, diag):
    t = _dot_nt(q, k) * (LOG2E * FOX_DH ** -0.5) - ck * LOG2E
    if diag:
        r = lax.broadcasted_iota(jnp.int32, t.shape, 0)
        c = lax.broadcasted_iota(jnp.int32, t.shape, 1)
        t = jnp.where(c <= r, t, NEG)
    return t


ATTN_HEADS = 2


def _head_cols(a):
    return (slice(a * FOX_DH, (a + 1) * FOX_DH), slice(2 * a * FOX_DH, (2 * a + 1) * FOX_DH),
            slice((2 * a + 1) * FOX_DH, (2 * a + 2) * FOX_DH))


def _on_blocks(i, j, step):
    pl.when(j < i)(functools.partial(step, False))
    pl.when(j == i)(functools.partial(step, True))


def _attn_fwd(proj_a, cq, ck, tq=640, exch=None):
    lp = proj_a.shape[0]
    tq = _row_tile(lp, tq)
    tk = tq
    nq = lp // tq

    def kern(q_ref, kv_ref, cq_ref, ck_ref, o_ref, lse_ref, m_sc, l_sc, acc):
        i, j = pl.program_id(1), pl.program_id(2)

        @pl.when(j == 0)
        def _():
            m_sc[...] = jnp.full_like(m_sc, -jnp.inf)
            l_sc[...] = jnp.zeros_like(l_sc)
            acc[...] = jnp.zeros_like(acc)

        def step(diag):
            for a in range(ATTN_HEADS):
                hq, hk, hv = _head_cols(a)
                cq2 = cq_ref[a] * LOG2E
                t = _attn_logits2(q_ref[:, hq], kv_ref[:, hk], ck_ref[a], diag)
                m_old = m_sc[a]
                m_new = jnp.maximum(m_old, jnp.max(t, axis=-1, keepdims=True) + cq2)
                p = jnp.exp2(t + (cq2 - m_new))
                alpha = jnp.exp2(m_old - m_new)
                l_sc[a] = alpha * l_sc[a] + jnp.sum(p, axis=-1, keepdims=True)
                acc[:, hq] = alpha * acc[:, hq] + _dot(p.astype(BF16), kv_ref[:, hv])
                m_sc[a] = m_new

        _on_blocks(i, j, step)

        @pl.when(j == nq - 1)
        def _():
            valid = _valid_rows(i * tq, tq)
            for a in range(ATTN_HEADS):
                hq, _, _ = _head_cols(a)
                o_ref[:, hq] = jnp.where(valid, acc[:, hq] / l_sc[a], 0.0).astype(BF16)
                lse_ref[a] = m_sc[a] + jnp.log(l_sc[a]) * LOG2E

    hp = ATTN_HEADS
    out, got = _pcall(
        kern, grid=(FOX_H // hp, nq, nq),
        in_specs=[pl.BlockSpec((tq, hp * FOX_DH), lambda h, i, j: (i, h)),
                  pl.BlockSpec((tk, 2 * hp * FOX_DH), lambda h, i, j: (jnp.minimum(j, i), KV0 // (2 * hp * FOX_DH) + h)),
                  pl.BlockSpec((hp, tq, 1), lambda h, i, j: (h, i, 0)),
                  pl.BlockSpec((hp, 1, tk), lambda h, i, j: (h, 0, jnp.minimum(j, i)))],
        out_specs=[pl.BlockSpec((tq, hp * FOX_DH), lambda h, i, j: (i, h)),
                   pl.BlockSpec((hp, tq, 1), lambda h, i, j: (h, i, 0))],
        out_shape=[jax.ShapeDtypeStruct((lp, FOX_W), BF16), jax.ShapeDtypeStruct((FOX_H, lp, 1), F32)],
        scratch=[pltpu.VMEM((hp, tq, 1), F32), pltpu.VMEM((hp, tq, 1), F32), pltpu.VMEM((tq, hp * FOX_DH), F32)],
        sem=("parallel", "parallel", "arbitrary"), name="attn_fwd", args=(proj_a, proj_a, cq, ck), exch=exch)
    return out[0], out[1], got


def _attn_bwd_dq(proj_a, do, o, lse, cq, ck, dproj, tq=640, exch=None):
    lp = proj_a.shape[0]
    tq = _row_tile(lp, tq)
    tk = tq
    nq = lp // tq

    def kern(q_ref, kv_ref, do_ref, o_ref, lse_ref, cq_ref, ck_ref, dp_any, dq_ref, dcq_ref, dl_ref, dq_acc):
        del dp_any
        i, j = pl.program_id(1), pl.program_id(2)

        @pl.when(j == 0)
        def _():
            for a in range(ATTN_HEADS):
                hq, _, _ = _head_cols(a)
                dl_ref[a] = jnp.sum(do_ref[:, hq].astype(F32) * o_ref[:, hq].astype(F32), axis=-1, keepdims=True)
            dcq_ref[...] = jnp.zeros_like(dcq_ref)
            dq_acc[...] = jnp.zeros_like(dq_acc)

        def step(diag):
            for a in range(ATTN_HEADS):
                hq, hk, hv = _head_cols(a)
                t = _attn_logits2(q_ref[:, hq], kv_ref[:, hk], ck_ref[a], diag)
                p = jnp.exp2(t + (cq_ref[a] * LOG2E - lse_ref[a]))
                ds = p * (_dot_nt(do_ref[:, hq], kv_ref[:, hv]) - dl_ref[a])
                dcq_ref[a] += jnp.sum(ds, axis=-1, keepdims=True)
                dq_acc[:, hq] += _dot(ds.astype(BF16), kv_ref[:, hk])

        _on_blocks(i, j, step)

        @pl.when(j == nq - 1)
        def _():
            dq_ref[...] = (dq_acc[...] * (FOX_DH ** -0.5)).astype(BF16)

    hp = ATTN_HEADS
    qspec = pl.BlockSpec((tq, hp * FOX_DH), lambda h, i, j: (i, h))
    col = pl.BlockSpec((hp, tq, 1), lambda h, i, j: (h, i, 0))
    out, got = _pcall(
        kern, grid=(FOX_H // hp, nq, nq),
        in_specs=[qspec, pl.BlockSpec((tk, 2 * hp * FOX_DH),
                                      lambda h, i, j: (jnp.minimum(j, i), KV0 // (2 * hp * FOX_DH) + h)),
                  qspec, qspec, col, col,
                  pl.BlockSpec((hp, 1, tk), lambda h, i, j: (h, 0, jnp.minimum(j, i))),
                  pl.BlockSpec(memory_space=pl.ANY)],
        out_specs=[qspec, col, col],
        out_shape=[jax.ShapeDtypeStruct(dproj.shape, BF16), jax.ShapeDtypeStruct((FOX_H, lp, 1), F32),
                   jax.ShapeDtypeStruct((FOX_H, lp, 1), F32)],
        scratch=[pltpu.VMEM((tq, hp * FOX_DH), F32)], aliases={7: 0},
        sem=("parallel", "parallel", "arbitrary"), name="attn_bwd_dq",
        args=(proj_a, proj_a, do, o, lse, cq, ck, dproj), exch=exch)
    return out[0], out[1], out[2], got


def _attn_bwd_dkv(proj_a, do, lse, delta, cq, ck, dproj, tq=640, exch=None):
    lp = proj_a.shape[0]
    tq = _row_tile(lp, tq)
    tk = tq
    nq = lp // tq

    def kern(q_ref, kv_ref, do_ref, lse_ref, dl_ref, cq_ref, ck_ref, dp_any, dkv_ref, dck_ref, dk_acc, dv_acc):
        del dp_any
        j, i = pl.program_id(1), pl.program_id(2)

        @pl.when(i == 0)
        def _():
            dck_ref[...] = jnp.zeros_like(dck_ref)
            dk_acc[...] = jnp.zeros_like(dk_acc)
            dv_acc[...] = jnp.zeros_like(dv_acc)

        def step(diag):
            for a in range(ATTN_HEADS):
                hq, hk, hv = _head_cols(a)
                t = _attn_logits2(q_ref[:, hq], kv_ref[:, hk], ck_ref[a], diag)
                p = jnp.exp2(t + (cq_ref[a] * LOG2E - lse_ref[a]))
                dv_acc[:, hq] += _dot_tn(p.astype(BF16), do_ref[:, hq])
                ds = p * (_dot_nt(do_ref[:, hq], kv_ref[:, hv]) - dl_ref[a])
                dck_ref[a] -= jnp.sum(ds, axis=0, keepdims=True)
                dk_acc[:, hq] += _dot_tn(ds.astype(BF16), q_ref[:, hq])

        _on_blocks(i, j, step)

        @pl.when(i == nq - 1)
        def _():
            parts = []
            for a in range(ATTN_HEADS):
                hq, _, _ = _head_cols(a)
                parts += [dk_acc[:, hq] * (FOX_DH ** -0.5), dv_acc[:, hq]]
            dkv_ref[...] = jnp.concatenate(parts, axis=1).astype(BF16)

    hp = ATTN_HEADS
    qspec = pl.BlockSpec((tq, hp * FOX_DH), lambda h, j, i: (jnp.maximum(i, j), h))
    col = pl.BlockSpec((hp, tq, 1), lambda h, j, i: (h, jnp.maximum(i, j), 0))
    kvspec = pl.BlockSpec((tk, 2 * hp * FOX_DH), lambda h, j, i: (j, KV0 // (2 * hp * FOX_DH) + h))
    rowspec = pl.BlockSpec((hp, 1, tk), lambda h, j, i: (h, 0, j))
    out, got = _pcall(
        kern, grid=(FOX_H // hp, nq, nq),
        in_specs=[qspec, kvspec, qspec, col, col, col, rowspec, pl.BlockSpec(memory_space=pl.ANY)],
        out_specs=[kvspec, rowspec],
        out_shape=[jax.ShapeDtypeStruct(dproj.shape, BF16), jax.ShapeDtypeStruct((FOX_H, 1, lp), F32)],
        scratch=[pltpu.VMEM((tk, hp * FOX_DH), F32), pltpu.VMEM((tk, hp * FOX_DH), F32)], aliases={7: 0},
        sem=("parallel", "parallel", "arbitrary"), name="attn_bwd_dkv",
        args=(proj_a, proj_a, do, lse, delta, cq, ck, dproj), exch=exch)
    return out[0], out[1], got


def _gla_gate_fwd(proj_r, wg2p, bg, tr=640):
    lp = proj_r.shape[0]
    tr = _row_tile(lp, tr)
    w = GLA_H * GLA_DK

    def kern(s_ref, w_ref, b_ref, o_ref):
        zg = _dot(s_ref[...].astype(BF16), w_ref[...]) + b_ref[...]
        o_ref[...] = jnp.where(_valid_rows(pl.program_id(0) * tr, tr), _log_sigmoid(zg) * (1.0 / GLA_TAU), 0.0)

    return pl.pallas_call(
        kern, grid=(lp // tr,),
        in_specs=[pl.BlockSpec((tr, BLK), lambda i: (i, R_SMALL_BLK128)), pl.BlockSpec((BLK, w), lambda i: (0, 0)),
                  pl.BlockSpec((1, w), lambda i: (0, 0))],
        out_specs=pl.BlockSpec((tr, w), lambda i: (i, 0)),
        out_shape=jax.ShapeDtypeStruct((lp, w), F32),
        compiler_params=_cparams("parallel"), name="gla_gate_fwd")(proj_r, wg2p, bg)


def _gla_chunk(grp, g):
    q = grp[:, :GLA_DK] * (GLA_DK ** -0.5)
    k = grp[:, GLA_DK:2 * GLA_DK]
    v = grp[:, 2 * GLA_DK:2 * GLA_DK + GLA_DV]
    r = grp[:, 2 * GLA_DK + GLA_DV:]
    b = _dot_exact(_tri(BLK, True), g)
    bl = b[BLK - 1:BLK, :]
    eb = jnp.exp(b)
    enb = jnp.exp(-b)
    ebl = jnp.exp(bl - b)
    qe, ke, kd = q * eb, k * enb, k * ebl
    causal = lax.broadcasted_iota(jnp.int32, (BLK, BLK), 1) <= lax.broadcasted_iota(jnp.int32, (BLK, BLK), 0)
    att = jnp.where(causal, _dot_nt(qe.astype(BF16), ke.astype(BF16)), 0.0)
    return q, k, v, r, bl, eb, enb, ebl, qe, ke, kd, causal, att


def _gla_fwd(proj_r, logg, gn):
    lp = proj_r.shape[0]
    nc = lp // BLK
    wv = GLA_H * GLA_DV

    def kern(grp_ref, g_ref, gn_ref, o_ref, zc_ref, st_ref, st):
        c = pl.program_id(1)

        @pl.when(c == 0)
        def _():
            st[...] = jnp.zeros_like(st)

        q, k, v, r, bl, eb, enb, ebl, qe, ke, kd, causal, att = _gla_chunk(grp_ref[...], g_ref[...])
        s_t = st[...]
        st_ref[...] = s_t
        vb = v.astype(BF16)
        o = _dot(att.astype(BF16), vb) + _dot_nt(qe.astype(BF16), s_t.astype(BF16))
        st[...] = s_t * jnp.exp(bl) + _dot_tn(vb, kd.astype(BF16))
        o_ref[...] = o
        rstd = lax.rsqrt(jnp.mean(o * o, axis=-1, keepdims=True) + EPS)
        zc_ref[...] = (r * _sigmoid(r) * (o * rstd * gn_ref[...])).astype(BF16)

    vspec = pl.BlockSpec((BLK, GLA_DV), lambda h, c: (c, h))
    return pl.pallas_call(
        kern, grid=(GLA_H, nc),
        in_specs=[pl.BlockSpec((BLK, GLA_GRP), lambda h, c: (c, R_GLA_BLK0 + h)),
                  pl.BlockSpec((BLK, GLA_DK), lambda h, c: (c, h)),
                  pl.BlockSpec((1, GLA_DV), lambda h, c: (0, h))],
        out_specs=[vspec, vspec, pl.BlockSpec((None, None, GLA_DV, GLA_DK), lambda h, c: (h, c, 0, 0))],
        out_shape=[jax.ShapeDtypeStruct((lp, wv), F32), jax.ShapeDtypeStruct((lp, wv), BF16),
                   jax.ShapeDtypeStruct((GLA_H, nc, GLA_DV, GLA_DK), F32)],
        scratch_shapes=[pltpu.VMEM((GLA_DV, GLA_DK), F32)],
        compiler_params=_cparams("parallel", "arbitrary"), name="gla_fwd")(proj_r, logg, gn)


def _gla_bwd(proj_r, logg, st_all, o_all, dzc, gn, dproj):
    lp = proj_r.shape[0]
    nc = lp // BLK

    def kern(grp_ref, g_ref, st_ref, o_ref, dzc_ref, gn_ref, dp_any, dgrp_ref, dlg_ref, dgn_ref, dst):
        del dp_any
        cc = pl.program_id(1)

        @pl.when(cc == 0)
        def _():
            dst[...] = jnp.zeros_like(dst)
            dgn_ref[...] = jnp.zeros_like(dgn_ref)

        q, k, v, r, bl, eb, enb, ebl, qe, ke, kd, causal, att = _gla_chunk(grp_ref[...], g_ref[...])
        s_t = st_ref[...]
        d_st = dst[...]
        o = o_ref[...]
        dzc_v = dzc_ref[...]
        rstd = lax.rsqrt(jnp.mean(o * o, axis=-1, keepdims=True) + EPS)
        xhat = o * rstd
        sr = _sigmoid(r)
        dr = dzc_v * (xhat * gn_ref[...]) * (sr * (1.0 + r * (1.0 - sr)))
        docn = dzc_v * (r * sr)
        dgn_ref[...] += jnp.sum(docn * xhat, axis=0, keepdims=True)
        dxh = docn * gn_ref[...]
        do = rstd * (dxh - xhat * jnp.mean(dxh * xhat, axis=-1, keepdims=True))
        dob, vb = do.astype(BF16), v.astype(BF16)
        qeb, keb, kdb = qe.astype(BF16), ke.astype(BF16), kd.astype(BF16)
        datt = jnp.where(causal, _dot_nt(dob, vb), 0.0).astype(BF16)
        dv = _dot_tn(att.astype(BF16), dob) + _dot_nt(kdb, d_st.astype(BF16))
        dqe = _dot(datt, keb) + _dot(dob, s_t.astype(BF16))
        dke = _dot_tn(datt, qeb)
        dkd = _dot(vb, d_st.astype(BF16))
        dq = dqe * eb * (GLA_DK ** -0.5)
        dk = dke * enb + dkd * ebl
        kd_dkd = dkd * kd
        db = dqe * qe - dke * ke - kd_dkd
        db_last = (jnp.sum(kd_dkd, axis=0, keepdims=True)
                   + jnp.exp(bl) * jnp.sum(s_t * d_st, axis=0, keepdims=True))
        dlg_ref[...] = _dot_exact(_tri(BLK, False), db) + db_last
        dst[...] = d_st * jnp.exp(bl) + _dot_tn(dob, qeb)
        dgrp_ref[...] = jnp.concatenate([dq, dk, dv, dr], axis=1).astype(BF16)

    rev = lambda c: nc - 1 - c
    vspec = pl.BlockSpec((BLK, GLA_DV), lambda h, c: (rev(c), h))
    return pl.pallas_call(
        kern, grid=(GLA_H, nc),
        in_specs=[pl.BlockSpec((BLK, GLA_GRP), lambda h, c: (rev(c), R_GLA_BLK0 + h)),
                  pl.BlockSpec((BLK, GLA_DK), lambda h, c: (rev(c), h)),
                  pl.BlockSpec((None, None, GLA_DV, GLA_DK), lambda h, c: (h, rev(c), 0, 0)),
                  vspec, vspec, pl.BlockSpec((1, GLA_DV), lambda h, c: (0, h)),
                  pl.BlockSpec(memory_space=pl.ANY)],
        out_specs=[pl.BlockSpec((BLK, GLA_GRP), lambda h, c: (rev(c), F_GLA_BLK0 + h)),
                   pl.BlockSpec((BLK, GLA_DK), lambda h, c: (rev(c), h)),
                   pl.BlockSpec((1, GLA_DV), lambda h, c: (0, h))],
        out_shape=[jax.ShapeDtypeStruct(dproj.shape, BF16), jax.ShapeDtypeStruct((lp, GLA_H * GLA_DK), F32),
                   jax.ShapeDtypeStruct((1, GLA_H * GLA_DV), F32)],
        scratch_shapes=[pltpu.VMEM((GLA_DV, GLA_DK), F32)],
        input_output_aliases={6: 0},
        compiler_params=_cparams("parallel", "arbitrary"), name="gla_bwd",
    )(proj_r, logg, st_all, o_all, dzc, gn, dproj)


def _small_bwd(proj_r, dlogg, wg2p, wg2pt, bg, dfa, dproj, tr=640):
    lp = proj_r.shape[0]
    tr = _row_tile(lp, tr)
    w = GLA_H * GLA_DK

    def kern(s_ref, dlg_ref, w_ref, wt_ref, b_ref, dfa_ref, dp_any, ds_ref, dbg_ref, dw_ref):
        del dp_any
        i = pl.program_id(0)
        sb = s_ref[...].astype(BF16)
        zg = _dot(sb, w_ref[...]) + b_ref[...]
        dzg = jnp.where(_valid_rows(i * tr, tr), dlg_ref[...] * (1.0 / GLA_TAU) * _sigmoid(-zg), 0.0)

        @pl.when(i == 0)
        def _():
            dbg_ref[...] = jnp.zeros_like(dbg_ref)
            dw_ref[...] = jnp.zeros_like(dw_ref)

        dbg_ref[...] += jnp.sum(dzg, axis=0, keepdims=True)
        dzb = dzg.astype(BF16)
        dw_ref[...] += _dot_tn(sb, dzb)
        dsm = _dot(dzb, wt_ref[...]) + dfa_ref[...]
        ds_ref[...] = jnp.concatenate([dsm, jnp.zeros((tr, SMALL_W - BLK), F32)], axis=1).astype(BF16)

    return pl.pallas_call(
        kern, grid=(lp // tr,),
        in_specs=[pl.BlockSpec((tr, BLK), lambda i: (i, R_SMALL_BLK128)), pl.BlockSpec((tr, w), lambda i: (i, 0)),
                  pl.BlockSpec((BLK, w), lambda i: (0, 0)), pl.BlockSpec((w, BLK), lambda i: (0, 0)),
                  pl.BlockSpec((1, w), lambda i: (0, 0)), pl.BlockSpec((tr, BLK), lambda i: (i, 0)),
                  pl.BlockSpec(memory_space=pl.ANY)],
        out_specs=[pl.BlockSpec((tr, SMALL_W), lambda i: (i, F_SMALL_BLK0)), pl.BlockSpec((1, w), lambda i: (0, 0)),
                   pl.BlockSpec((BLK, w), lambda i: (0, 0))],
        out_shape=[jax.ShapeDtypeStruct(dproj.shape, BF16), jax.ShapeDtypeStruct((1, w), F32),
                   jax.ShapeDtypeStruct((BLK, w), F32)],
        input_output_aliases={6: 0},
        compiler_params=_cparams("arbitrary"), name="small_bwd",
    )(proj_r, dlogg, wg2p, wg2pt, bg, dfa, dproj)


def _merge_fwd(proj_r, gate_b3, ya, yb, yc, tr=640):
    lp = proj_r.shape[0]
    tr = _row_tile(lp, tr)
    tn = GATE_TN

    def kern(g_ref, b_ref, ya_ref, yb_ref, yc_ref, o_ref):
        g = g_ref[...]
        mix = (_sigmoid(g[:, :tn] + b_ref[0:1, :]) * ya_ref[...]
               + _sigmoid(g[:, tn:2 * tn] + b_ref[1:2, :]) * yb_ref[...]
               + _sigmoid(g[:, 2 * tn:] + b_ref[2:3, :]) * yc_ref[...])
        o_ref[...] = mix.astype(BF16)

    y = pl.BlockSpec((tr, tn), lambda i, j: (i, j))
    return pl.pallas_call(
        kern, grid=(lp // tr, D // tn),
        in_specs=[pl.BlockSpec((tr, 3 * tn), lambda i, j: (i, R_GATE_BLK0 + j)),
                  pl.BlockSpec((3, tn), lambda i, j: (0, j)), y, y, y],
        out_specs=y, out_shape=jax.ShapeDtypeStruct((lp, D), BF16),
        compiler_params=_cparams("parallel", "parallel"), name="merge_fwd")(proj_r, gate_b3, ya, yb, yc)


def _merge_bwd(proj_r, gate_b3, ya, yb, yc, dmix, tr=640):
    lp = proj_r.shape[0]
    tr = _row_tile(lp, tr)
    tn = GATE_TN

    def kern(g_ref, b_ref, ya_ref, yb_ref, yc_ref, dm_ref, dya_ref, dyb_ref, dyc_ref, dg_ref, db_ref):
        i = pl.program_id(1)
        g = g_ref[...]
        dm = dm_ref[...]

        @pl.when(i == 0)
        def _():
            db_ref[...] = jnp.zeros_like(db_ref)

        dgs = []
        for n, (y_ref, dy_ref) in enumerate(((ya_ref, dya_ref), (yb_ref, dyb_ref), (yc_ref, dyc_ref))):
            s = _sigmoid(g[:, n * tn:(n + 1) * tn] + b_ref[n:n + 1, :])
            dy_ref[...] = (dm * s).astype(BF16)
            dgn = dm * y_ref[...] * (s * (1.0 - s))
            db_ref[n:n + 1, :] += jnp.sum(dgn, axis=0, keepdims=True)
            dgs.append(dgn)
        dg_ref[...] = jnp.concatenate(dgs, axis=1).astype(BF16)

    y = pl.BlockSpec((tr, tn), lambda j, i: (i, j))
    bspec = pl.BlockSpec((3, tn), lambda j, i: (0, j))
    return pl.pallas_call(
        kern, grid=(D // tn, lp // tr),
        in_specs=[pl.BlockSpec((tr, 3 * tn), lambda j, i: (i, R_GATE_BLK0 + j)), bspec, y, y, y, y],
        out_specs=[y, y, y, pl.BlockSpec((tr, 3 * tn), lambda j, i: (i, F_GATE_BLK0 + j)), bspec],
        out_shape=[jax.ShapeDtypeStruct((lp, D), BF16)] * 3
        + [jax.ShapeDtypeStruct((lp, NP), BF16), jax.ShapeDtypeStruct((3, D), F32)],
        compiler_params=_cparams("parallel", "arbitrary"), name="merge_bwd")(proj_r, gate_b3, ya, yb, yc, dmix)


def _final_loss(h, gf, tgt):
    lp = h.shape[0]
    nb = lp // BLK

    def kern(h_ref, g_ref, t_ref, dh_ref, dg_ref, ls_ref):
        i = pl.program_id(0)

        @pl.when(i == 0)
        def _():
            dh_ref[...] = jnp.zeros_like(dh_ref)
            dg_ref[...] = jnp.zeros_like(dg_ref)
            ls_ref[...] = jnp.zeros_like(ls_ref)

        @pl.when(i > 0)
        def _():
            x = h_ref[...]
            r = lax.rsqrt(jnp.mean(x * x, axis=-1, keepdims=True) + EPS)
            xhat = x * r
            err = xhat * g_ref[...] - t_ref[...]
            ls_ref[...] += jnp.sum(jnp.sum(err * err, axis=0, keepdims=True), axis=1, keepdims=True)
            dy = err * (1.0 / D)
            dg_ref[...] += jnp.sum(dy * xhat, axis=0, keepdims=True)
            dxh = dy * g_ref[...]
            dh_ref[...] = r * (dxh - xhat * jnp.mean(dxh * xhat, axis=-1, keepdims=True))

    row = pl.BlockSpec((BLK, D), lambda i: (i, 0))
    vec = pl.BlockSpec((1, D), lambda i: (0, 0))
    return pl.pallas_call(
        kern, grid=(nb,),
        in_specs=[row, vec, pl.BlockSpec((BLK, D), lambda i: (jnp.maximum(i - 1, 0), 0))],
        out_specs=[row, vec, pl.BlockSpec((1, 1), lambda i: (0, 0))],
        out_shape=[jax.ShapeDtypeStruct((lp, D), F32), jax.ShapeDtypeStruct((1, D), F32),
                   jax.ShapeDtypeStruct((1, 1), F32)],
        compiler_params=_cparams("arbitrary"), name="final_loss")(h, gf, tgt)


def _gate_cols(c):
    ct = c[:, :FOX_H].T
    ck = jnp.where(jnp.arange(ct.shape[1]) < PAD, KEY_PAD_BIAS, ct)
    return ct[:, :, None], ck[:, None, :]


def _run(hosts, name, ctx, fn):
    if hosts and name in hosts:
        make, done = hosts[name]
        res = fn(make(ctx))
        done(res[-1])
    else:
        res = fn(None)
    return res[:-1]


def _mm_nn_x(a, b, exch, **kw):
    out = _mm_nn(a, b, exch=exch, **kw)
    return out if exch is not None else (out, None)


def _layer_fwd(h, w, hosts=None):
    xn = _rmsnorm_fwd(h, w["norm1_g"])
    proj_a = _mm_nn(xn, w["w_in"], n0=0, n=REST0, out_dtype=BF16, name="proj_a")
    proj_r, = _run(hosts, "proj_r", w, lambda e: _mm_nn_x(xn, w["w_in"], e, n0=REST0, n=NREST, name="proj_r"))
    cq, ck = _gate_cols(_fox_gate_fwd(proj_r, w["bf128"]))
    oa, lse = _run(hosts, "attn_fwd", w, lambda e: _attn_fwd(proj_a, cq, ck, exch=e))
    zb = _convb_fwd(proj_r, w["conv_w"])
    logg = _gla_gate_fwd(proj_r, w["wg2p"], w["gla_b_g"])
    o_gla, zc, st_all = _gla_fwd(proj_r, logg, w["gla_norm_g"])
    ya = _mm_nn(oa, w["w_a_o"], name="branch_a")
    yb = _mm_nn(zb, w["w_b_o"], name="branch_b")
    yc = _mm_nn(zc, w["w_c_o"], name="branch_c")
    mix = _merge_fwd(proj_r, w["gate_b3"], ya, yb, yc)
    h1 = _mm_nn(mix, w["w_o"], res=h, name="out_proj")
    xn2 = _rmsnorm_fwd(h1, w["norm2_g"])
    z, = _run(hosts, "up_proj", w, lambda e: _mm_nn_x(xn2, w["w_up"], e, name="up_proj"))
    a = _mlp_gate_fwd(z, w["mlp_conv_w"])
    h2, = _run(hosts, "down_proj", w,
               lambda e: _mm_nn_x(a, w["w_down"], e, res=h1, tk=D_FF // 4, name="down_proj"))
    saved = dict(h=h, xn=xn, proj_a=proj_a, proj_r=proj_r, cq=cq, ck=ck, oa=oa, lse=lse, zb=zb, logg=logg,
                 o_gla=o_gla, zc=zc, st_all=st_all, ya=ya, yb=yb, yc=yc, mix=mix, h1=h1, xn2=xn2, z=z, a=a)
    return h2, saved


def _layer_bwd(dh2, w, s, hosts=None):
    g = {}
    da = _mm_nt(dh2, w["w_down"], tn=D_FF // 4, name="d_down_in")
    g["w_down"] = _mm_tn(s["a"], dh2, tk=D_FF // 4, name="d_w_down")
    dzg, dzu, dmw_g, dmw_u = _run(hosts, "mlp_gate_bwd", g, lambda e: _mlp_gate_bwd(
        s["z"], da, w["mlp_conv_w"], exch=e))
    g["mlp_conv_w"] = jnp.concatenate([dmw_g, dmw_u], axis=1)
    dxn2 = _mm_nt(dzg, w["w_up"], k0=0, kw=D_FF, tk=D_FF // 4, name="d_up_in_g")
    dxn2 = _mm_nt(dzu, w["w_up"], k0=D_FF, kw=D_FF, tk=D_FF // 4, add=dxn2, name="d_up_in_u")
    g["w_up"] = jnp.concatenate([_mm_tn(s["xn2"], dzg, name="d_w_up_g"), _mm_tn(s["xn2"], dzu, name="d_w_up_u")], axis=1)
    dh1, g["norm2_g"] = _rmsnorm_bwd(s["h1"], w["norm2_g"], dxn2, dh2)
    dmix = _mm_nt(dh1, w["w_o"], name="d_out_proj_in")
    g["w_o"] = _mm_tn(s["mix"], dh1, name="d_w_o")
    dya, dyb, dyc, dproj, g["gate_b3"] = _merge_bwd(s["proj_r"], w["gate_b3"], s["ya"], s["yb"], s["yc"], dmix)
    doa = _mm_nt(dya, w["w_a_o"], out_dtype=BF16, name="d_branch_a_in")
    g["w_a_o"] = _mm_tn(s["oa"], dya, name="d_w_a_o")
    dzb = _mm_nt(dyb, w["w_b_o"], name="d_branch_b_in")
    g["w_b_o"] = _mm_tn(s["zb"], dyb, name="d_w_b_o")
    dzc = _mm_nt(dyc, w["w_c_o"], name="d_branch_c_in")
    g["w_c_o"] = _mm_tn(s["zc"], dyc, name="d_w_c_o")
    dproj, dlogg, g["gla_norm_g"] = _gla_bwd(s["proj_r"], s["logg"], s["st_all"], s["o_gla"], dzc, w["gla_norm_g"], dproj)
    dproj, g["conv_w"] = _convb_bwd(s["proj_r"], dzb, w["conv_w"], dproj)
    dproj, dcq, delta = _run(hosts, "attn_bwd_dq", g, lambda e: _attn_bwd_dq(
        s["proj_a"], doa, s["oa"], s["lse"], s["cq"], s["ck"], dproj, exch=e))
    dproj, dck = _run(hosts, "attn_bwd_dkv", g, lambda e: _attn_bwd_dkv(
        s["proj_a"], doa, s["lse"], delta, s["cq"], s["ck"], dproj, exch=e))
    dc = jnp.pad((dcq[:, :, 0] + dck[:, 0, :]).T, ((0, 0), (0, BLK - FOX_H)))
    dfa, g["bf128"] = _fox_gate_bwd(s["proj_r"], dc, w["bf128"])
    dproj, g["gla_b_g"], g["wg2p"] = _small_bwd(s["proj_r"], dlogg, w["wg2p"], w["wg2p"].T, w["gla_b_g"], dfa, dproj)
    def pair(out, e):
        return out if e is not None else (out, None)

    g["w_in"], = _run(hosts, "d_w_in", g, lambda e: pair(_mm_tn(s["xn"], dproj, name="d_w_in", exch=e), e))
    dxn, = _run(hosts, "d_in_proj_in", g, lambda e: pair(_mm_nt(dproj, w["w_in"], name="d_in_proj_in", exch=e), e))
    dh0, g["norm1_g"] = _rmsnorm_bwd(s["h"], w["norm1_g"], dxn, dh1)
    return dh0, g


def _local_step(x, tgt, meta, final_g, layers, hosts_fwd=None, hosts_bwd=None):
    h = jnp.concatenate([jnp.zeros((PAD, D), F32), meta, x], axis=0)
    saved = []
    for l, w in enumerate(layers):
        h, s = _layer_fwd(h, w, hosts_fwd[l] if hosts_fwd else None)
        saved.append(s)
    dh, dgf, sq = _final_loss(h, final_g, tgt)
    grads = [None] * len(layers)
    for l in reversed(range(len(layers))):
        dh, grads[l] = _layer_bwd(dh, layers[l], saved[l], hosts_bwd[l](grads) if hosts_bwd else None)
    return sq[0, 0], dh[BLK:], dh[PAD:BLK], dgf, grads


def _w_in_to_kernel(w_nat):
    parts = [w_nat[:, s:s + n] for s, n in _segments()]
    parts.append(jnp.zeros((w_nat.shape[0], SMALL_W - 8 - GLA_R), w_nat.dtype))
    return jnp.concatenate(parts, axis=1)


def _w_in_from_kernel(w_k):
    pieces, off = [], 0
    for s, n in _segments():
        pieces.append((s, w_k[:, off:off + n]))
        off += n
    return jnp.concatenate([p for _, p in sorted(pieces, key=lambda t: t[0])], axis=1)


def _pad_rows_at(a, row0, nrows):
    return jnp.pad(a, ((row0, nrows - row0 - a.shape[0]), (0, 0)))


def _big_to_kernel(name, full):
    return _w_in_to_kernel(full) if name == "w_in" else full


def _layer_weights(big, conv_w, gla_w_g2, mlp_conv_w, norm1_g, fox_b_f, gate_b, gla_b_g, gla_norm_g, norm2_g):
    w = {n: _big_to_kernel(n, a) for n, a in big.items()}
    w.update(
        conv_w=conv_w, mlp_conv_w=mlp_conv_w,
        wg2p=_pad_rows_at(gla_w_g2, 8, BLK).astype(BF16),
        norm1_g=norm1_g[None], norm2_g=norm2_g[None], gla_b_g=gla_b_g[None], gla_norm_g=gla_norm_g[None],
        bf128=jnp.pad(fox_b_f, (0, BLK - FOX_H))[None], gate_b3=gate_b.reshape(3, D))
    return w


def _layer_grads_natural(g):
    return dict(
        w_in=_w_in_from_kernel(g["w_in"]), w_a_o=g["w_a_o"], w_b_o=g["w_b_o"], w_c_o=g["w_c_o"], w_o=g["w_o"],
        w_up=g["w_up"], w_down=g["w_down"], conv_w=g["conv_w"], mlp_conv_w=g["mlp_conv_w"],
        gla_w_g2=g["wg2p"][8:8 + GLA_R], norm1_g=g["norm1_g"][0], norm2_g=g["norm2_g"][0],
        gla_b_g=g["gla_b_g"][0], gla_norm_g=g["gla_norm_g"][0], fox_b_f=g["bf128"][0, :FOX_H],
        gate_b=g["gate_b3"].reshape(3 * D))


def _adamw(recv, w, m, v, layer, prev=None, name="adamw"):
    n_slot, r, c = recv.shape
    lyr = w.shape[0]
    tr = r
    for t in range(16, r, 16):
        if r % t == 0 and t * c <= ADAMW_BLOCK_ELEMS:
            tr = t
    if r * c <= ADAMW_BLOCK_ELEMS:
        tr = r
    bc1, bc2 = 1.0 - ADAM_B1 ** ADAM_STEP, 1.0 - ADAM_B2 ** ADAM_STEP

    def kern(*refs):
        r_ref, w_ref, m_ref, v_ref = refs[:4]
        g_out, d_out, m_out, v_out = refs[-4:]
        g = r_ref[0].astype(F32)
        for sidx in range(1, n_slot):
            g = g + r_ref[sidx].astype(F32)
        m_new = ADAM_B1 * m_ref[...] + (1.0 - ADAM_B1) * g
        v_new = ADAM_B2 * v_ref[...] + (1.0 - ADAM_B2) * (g * g)
        g_out[...] = g
        m_out[...] = m_new
        v_out[...] = v_new
        d_out[...] = -ADAM_LR * ((m_new / bc1) / (jnp.sqrt(v_new / bc2) + ADAM_EPS) + ADAM_WD * w_ref[...])

    lspec = pl.BlockSpec((None, tr, c), lambda i: (layer, i, 0))
    in_specs = [pl.BlockSpec((n_slot, tr, c), lambda i: (0, i, 0)), lspec, lspec, lspec]
    args = [recv, w, m, v]
    aliases = {}
    if prev is not None:
        in_specs += [pl.BlockSpec(memory_space=pl.ANY)] * 4
        args += list(prev)
        aliases = {4: 0, 5: 1, 6: 2, 7: 3}
    return pl.pallas_call(
        kern, grid=(r // tr,), in_specs=in_specs, out_specs=[lspec] * 4,
        out_shape=[jax.ShapeDtypeStruct((lyr, r, c), F32)] * 4, input_output_aliases=aliases,
        compiler_params=_cparams("parallel"), name=name)(*args)


_BIG = ("w_in", "w_a_o", "w_b_o", "w_c_o", "w_o", "w_up", "w_down")
_COL_SHARDED = ("w_in", "w_a_o", "w_b_o", "w_c_o", "w_up", "conv_w", "gla_w_g2", "mlp_conv_w")
_REPL = ("norm1_g", "fox_b_f", "gate_b", "gla_b_g", "gla_norm_g", "norm2_g")


def _cols_from_slots(a):
    return jnp.transpose(a, (1, 0, 2)).reshape(a.shape[1], N_DEV * a.shape[2])


def _cols_to_slots(a):
    r, c8 = a.shape
    return jnp.transpose(a.reshape(r, N_DEV, c8 // N_DEV), (1, 0, 2))


def _rows_to_slots(a):
    return a.reshape(N_DEV, a.shape[0] // N_DEV, a.shape[1])


def kernel(x, meta_tokens, norm1_g, w_in, fox_b_f, gate_b, conv_w, gla_w_g2, gla_b_g, gla_norm_g, w_a_o, w_b_o, w_c_o, w_o, norm2_g, w_up, mlp_conv_w, w_down, final_norm_g, loss_target, m_meta_tokens, m_norm1_g, m_w_in, m_fox_b_f, m_gate_b, m_conv_w, m_gla_w_g2, m_gla_b_g, m_gla_norm_g, m_w_a_o, m_w_b_o, m_w_c_o, m_w_o, m_norm2_g, m_w_up, m_mlp_conv_w, m_w_down, m_final_norm_g, v_meta_tokens, v_norm1_g, v_w_in, v_fox_b_f, v_gate_b, v_conv_w, v_gla_w_g2, v_gla_b_g, v_gla_norm_g, v_w_a_o, v_w_b_o, v_w_c_o, v_w_o, v_norm2_g, v_w_up, v_mlp_conv_w, v_w_down, v_final_norm_g):
    names = ("meta_tokens", "norm1_g", "w_in", "fox_b_f", "gate_b", "conv_w", "gla_w_g2", "gla_b_g", "gla_norm_g",
             "w_a_o", "w_b_o", "w_c_o", "w_o", "norm2_g", "w_up", "mlp_conv_w", "w_down", "final_norm_g")
    wts = dict(zip(names, (meta_tokens, norm1_g, w_in, fox_b_f, gate_b, conv_w, gla_w_g2, gla_b_g, gla_norm_g,
                           w_a_o, w_b_o, w_c_o, w_o, norm2_g, w_up, mlp_conv_w, w_down, final_norm_g)))
    mom = dict(zip(names, (m_meta_tokens, m_norm1_g, m_w_in, m_fox_b_f, m_gate_b, m_conv_w, m_gla_w_g2, m_gla_b_g,
                           m_gla_norm_g, m_w_a_o, m_w_b_o, m_w_c_o, m_w_o, m_norm2_g, m_w_up, m_mlp_conv_w, m_w_down,
                           m_final_norm_g)))
    var = dict(zip(names, (v_meta_tokens, v_norm1_g, v_w_in, v_fox_b_f, v_gate_b, v_conv_w, v_gla_w_g2, v_gla_b_g,
                           v_gla_norm_g, v_w_a_o, v_w_b_o, v_w_c_o, v_w_o, v_norm2_g, v_w_up, v_mlp_conv_w, v_w_down,
                           v_final_norm_g)))

    small = _exchange([conv_w, gla_w_g2, mlp_conv_w, meta_tokens], [True] * 4, "gather_small")
    conv_full = jnp.transpose(small[0], (1, 2, 0, 3)).reshape(DEPTH, 3, CONV_CH)
    g2_full = jnp.transpose(small[1], (1, 2, 0, 3)).reshape(DEPTH, GLA_R, GLA_H * GLA_DK)
    mconv_full = jnp.transpose(small[2], (1, 2, 0, 3)).reshape(DEPTH, 3, 2 * D_FF)
    meta_full = _cols_from_slots(small[3])
    layers = [_layer_weights({}, conv_full[l], g2_full[l], mconv_full[l], norm1_g[l], fox_b_f[l], gate_b[l],
                             gla_b_g[l], gla_norm_g[l], norm2_g[l]) for l in range(DEPTH)]

    def gather(l, which):
        def make(_):
            return [wts[n][l].astype(BF16) for n in which], [True] * len(which)

        def done(got):
            for n, a in zip(which, got):
                full = _cols_from_slots(a) if n in _COL_SHARDED else a.reshape(-1, a.shape[-1])
                layers[l][n] = _big_to_kernel(n, full)

        return make, done

    recv_big = [dict() for _ in range(DEPTH)]

    def scatter(l, which, grads_of):
        def make(ctx):
            g = grads_of(ctx)
            nat = [_w_in_from_kernel(g[n]) if n == "w_in" else g[n] for n in which]
            return ([_cols_to_slots(a) if n in _COL_SHARDED else _rows_to_slots(a) for n, a in zip(which, nat)],
                    [False] * len(which))

        def done(got):
            recv_big[l].update(zip(which, got))

        return make, done

    mixers = ("w_o", "w_a_o", "w_b_o", "w_c_o")
    early = ("w_down", "w_up") + mixers
    make, done = gather(0, ("w_in",))
    done(_exchange(*make(None), "gather_w_in"))
    hosts_fwd = [
        {"proj_r": gather(0, mixers + ("w_down",)), "attn_fwd": gather(0, ("w_up",)),
         "up_proj": gather(1, ("w_in",)), "down_proj": gather(1, mixers)},
        {"attn_fwd": gather(1, ("w_up", "w_down"))}]
    hosts_bwd = [
        lambda grads: {"mlp_gate_bwd": scatter(1, ("w_in",), lambda _: grads[1]),
                       "attn_bwd_dq": scatter(1, ("w_up", "w_down"), lambda _: grads[1]),
                       "attn_bwd_dkv": scatter(1, mixers, lambda _: grads[1]),
                       "d_w_in": scatter(0, early, lambda g: g),
                       "d_in_proj_in": scatter(0, ("w_in",), lambda g: g)},
        lambda grads: None]

    sq, grad_x, dmeta, dgf, grads_k = _local_step(x[0], loss_target[0], meta_full, final_norm_g[None], layers,
                                                  hosts_fwd, hosts_bwd)
    loss = lax.psum(sq * (0.5 / D), ("x", "y", "c"))
    grads = [_layer_grads_natural(g) for g in grads_k]

    out_g, out_d, out_m, out_v = {}, {}, {}, {}

    def update(name, recv, layer, lyr_shape, prev):
        w3, m3, v3 = (t[name].reshape(lyr_shape) for t in (wts, mom, var))
        return _adamw(recv.reshape((recv.shape[0],) + lyr_shape[1:]), w3, m3, v3, layer, prev, name="adamw_" + name)

    def store(name, res):
        shape = wts[name].shape
        out_g[name], out_d[name], out_m[name], out_v[name] = (t.reshape(shape) for t in res)

    for n in _BIG:
        res = None
        for l in range(DEPTH):
            res = update(n, recv_big[l][n], l, wts[n].shape, res)
        store(n, res)

    def stack_layers(name):
        return jnp.stack([grads[l][name] for l in range(DEPTH)])

    s_conv = jnp.transpose(stack_layers("conv_w").reshape(DEPTH, 3, N_DEV, -1), (2, 0, 1, 3))
    s_g2 = jnp.transpose(stack_layers("gla_w_g2").reshape(DEPTH, GLA_R, N_DEV, -1), (2, 0, 1, 3))
    s_mconv = jnp.transpose(stack_layers("mlp_conv_w").reshape(DEPTH, 3, N_DEV, -1), (2, 0, 1, 3))
    s_meta = _cols_to_slots(dmeta)
    repl = [stack_layers(n) for n in _REPL] + [dgf]
    pack = jnp.concatenate([jnp.pad(a.reshape(-1), (0, (-a.size) % 1024)) for a in repl]).reshape(-1, BLK)
    r_conv, r_g2, r_mconv, r_meta, r_pack = _exchange(
        [s_conv, s_g2, s_mconv, s_meta, pack], [False, False, False, False, True], "scatter_small")
    store("conv_w", update("conv_w", r_conv, 0, (1, DEPTH * 3, CONV_CH // N_DEV), None))
    store("gla_w_g2", update("gla_w_g2", r_g2, 0, (1, DEPTH * GLA_R, GLA_H * GLA_DK // N_DEV), None))
    store("mlp_conv_w", update("mlp_conv_w", r_mconv, 0, (1, DEPTH * 3, 2 * D_FF // N_DEV), None))
    store("meta_tokens", update("meta_tokens", r_meta, 0, (1, N_META, D // N_DEV), None))
    off = 0
    for n, a in zip(_REPL + ("final_norm_g",), repl):
        rows = (a.size + 1023) // 1024 * 8
        part = r_pack[:, off:off + rows].reshape(N_DEV, -1)[:, :a.size]
        off += rows
        shape2 = (1, 1, a.size) if a.size % BLK else (1, a.size // BLK, BLK)
        store(n, update(n, part, 0, shape2, None))

    order = lambda d: [d[n] for n in names]
    return (loss, grad_x[None], *order(out_g), *order(out_d), *order(out_m), *order(out_v))
```

```python
import functools

import jax
import jax.numpy as jnp
from jax import lax
from jax.experimental import pallas as pl
from jax.experimental.pallas import tpu as pltpu

F32 = jnp.float32
BF16 = jnp.bfloat16

D = 2048
DEPTH = 2
N_META = 16
BLK = 128
PAD = BLK - N_META
EPS = 1e-6
NEG = -1e30

FOX_H, FOX_DH = 8, 128
FOX_W = FOX_H * FOX_DH
CONV_CH = 1024
GLA_H, GLA_DK, GLA_DV, GLA_R, GLA_TAU = 4, 128, 256, 16, 16.0
D_FF = 5632
N_IN = 15384
N_DEV = 8

ADAM_LR, ADAM_B1, ADAM_B2, ADAM_EPS, ADAM_WD, ADAM_STEP = 0.001, 0.9, 0.999, 1e-08, 0.01, 10

CONV_TC = 512
GATE_TN = 512
KV0 = 1024
REST0 = 3072
GLA_GRP = 768
SMALL_W = 1024
NP = 16384
NREST = NP - REST0
R_CONV_BLK0 = 0
R_GLA_BLK0 = (6144 - REST0) // GLA_GRP
R_GATE_BLK0 = (9216 - REST0) // (3 * GATE_TN)
R_SMALL_BLK128 = (15360 - REST0) // 128
F_CONV_BLK0 = 3072 // (3 * CONV_TC)
F_GLA_BLK0 = 6144 // GLA_GRP
F_GATE_BLK0 = 9216 // (3 * GATE_TN)
F_SMALL_BLK0 = 15360 // SMALL_W

VMEM_LIMIT = 56 * 1024 * 1024
ADAMW_BLOCK_ELEMS = 128 * 1024


def _segments():
    seg = [(0, 1024)]
    for h in range(FOX_H):
        seg += [(1024 + 128 * h, 128), (2048 + 128 * h, 128)]
    for j in range(CONV_CH // CONV_TC):
        seg += [(3080 + CONV_TC * j, CONV_TC), (4104 + CONV_TC * j, CONV_TC), (5128 + CONV_TC * j, CONV_TC)]
    for h in range(GLA_H):
        seg += [(6152 + 128 * h, 128), (6664 + 128 * h, 128), (7176 + 256 * h, 256), (8200 + 256 * h, 256)]
    for j in range(D // GATE_TN):
        seg += [(9240 + GATE_TN * j, GATE_TN), (11288 + GATE_TN * j, GATE_TN), (13336 + GATE_TN * j, GATE_TN)]
    seg += [(3072, 8), (9224, 16)]
    return seg


def _cparams(*sem):
    return pltpu.CompilerParams(dimension_semantics=sem, vmem_limit_bytes=VMEM_LIMIT)


def _row_tile(n, target):
    best = BLK
    t = BLK
    while t <= min(n, target):
        if n % t == 0:
            best = t
        t += BLK
    return best


def _sigmoid(x):
    return 1.0 / (1.0 + jnp.exp(-x))


def _log_sigmoid(x):
    return jnp.minimum(x, 0.0) - jnp.log(1.0 + jnp.exp(-jnp.abs(x)))


def _valid_rows(row0, n):
    return (row0 + lax.broadcasted_iota(jnp.int32, (n, 1), 0)) >= PAD


def _dot(a, b):
    return jnp.dot(a, b, preferred_element_type=F32)


def _dot_nt(a, b):
    return lax.dot_general(a, b, (((1,), (1,)), ((), ())), preferred_element_type=F32)


def _dot_tn(a, b):
    return lax.dot_general(a, b, (((0,), (0,)), ((), ())), preferred_element_type=F32)


def _exchange_copies(ins, outs, bcast, send_sems, recv_sems, local_sems):
    x, y, c = lax.axis_index("x"), lax.axis_index("y"), lax.axis_index("c")
    me = 4 * x + 2 * y + c

    def src_for(n, dev):
        if bcast[n] is True or bcast[n] == "wide":
            return ins[n]
        if bcast[n] == "cols":
            w = ins[n].shape[1] // N_DEV
            return ins[n].at[:, pl.ds(pl.multiple_of(dev * w, BLK), w)]
        return ins[n].at[dev]

    def dst_of(n, dev):
        if bcast[n] == "wide":
            w = ins[n].shape[1]
            return outs[n].at[:, pl.ds(pl.multiple_of(dev * w, BLK), w)]
        return outs[n].at[dev]

    local, sends, recvs = [], [], []
    for n in range(len(ins)):
        local.append(pltpu.make_async_copy(src_for(n, me), dst_of(n, me), local_sems.at[n]))
    for k in range(1, N_DEV):
        px = 1 - x if (k >> 2) & 1 else x
        py = 1 - y if (k >> 1) & 1 else y
        pc = 1 - c if k & 1 else c
        peer = 4 * px + 2 * py + pc
        for n in range(len(ins)):
            def copy(dst_dev, n=n, k=k, to=(px, py, pc), peer=peer):
                return pltpu.make_async_remote_copy(
                    src_ref=src_for(n, peer), dst_ref=dst_of(n, dst_dev), send_sem=send_sems.at[n, k - 1],
                    recv_sem=recv_sems.at[n, k - 1], device_id=to, device_id_type=pl.DeviceIdType.MESH)

            sends.append(copy(me))
            recvs.append(copy(peer))
    return local, sends, recvs


def _exchange_start(copies):
    local, sends, _ = copies
    for cp in local + sends:
        cp.start()


def _exchange_wait(copies):
    local, sends, recvs = copies
    for cp in recvs:
        cp.wait_recv()
    for cp in sends:
        cp.wait_send()
    for cp in local:
        cp.wait()


def _exchange_shapes(arrays, bcast):
    def shape(a, b):
        if b is True:
            return (N_DEV,) + a.shape
        if b == "wide":
            return (a.shape[0], N_DEV * a.shape[1])
        if b == "cols":
            return (N_DEV, a.shape[0], a.shape[1] // N_DEV)
        return a.shape

    return [jax.ShapeDtypeStruct(shape(a, b), a.dtype) for a, b in zip(arrays, bcast)]


def _exchange_sems(n_arr):
    return [pltpu.SemaphoreType.DMA((n_arr, N_DEV - 1)), pltpu.SemaphoreType.DMA((n_arr, N_DEV - 1)),
            pltpu.SemaphoreType.DMA((n_arr,))]


def _exchange(arrays, bcast, name):
    n_arr = len(arrays)

    def body(*refs):
        copies = _exchange_copies(refs[:n_arr], refs[n_arr:2 * n_arr], bcast, *refs[2 * n_arr:])
        _exchange_start(copies)
        _exchange_wait(copies)

    hbm = pl.BlockSpec(memory_space=pltpu.HBM)
    return pl.pallas_call(
        body, out_shape=_exchange_shapes(arrays, bcast), in_specs=[hbm] * n_arr, out_specs=[hbm] * n_arr,
        scratch_shapes=_exchange_sems(n_arr),
        compiler_params=pltpu.CompilerParams(has_side_effects=True), name=name)(*arrays)


def _pcall(kern, *, grid, in_specs, out_specs, out_shape, scratch, sem, name, args, aliases=None, exch=None):
    params = pltpu.CompilerParams(dimension_semantics=sem, vmem_limit_bytes=VMEM_LIMIT,
                                  has_side_effects=exch is not None)
    kw = dict(grid=grid, compiler_params=params, name=name, input_output_aliases=aliases or {})
    if exch is None:
        out = pl.pallas_call(kern, in_specs=in_specs, out_specs=out_specs, out_shape=out_shape,
                             scratch_shapes=scratch, **kw)(*args)
        return out, None
    arrays, bcast = exch
    n_x, n_in, n_out, n_sc = len(arrays), len(in_specs), len(out_specs), len(scratch)

    def hosted(*refs):
        ins, x_in = refs[:n_in], refs[n_in:n_in + n_x]
        outs, x_out = refs[n_in + n_x:n_in + n_x + n_out], refs[n_in + n_x + n_out:n_in + 2 * n_x + n_out]
        sc, sems = refs[n_in + 2 * n_x + n_out:n_in + 2 * n_x + n_out + n_sc], refs[n_in + 2 * n_x + n_out + n_sc:]
        ids = [pl.program_id(d) for d in range(len(grid))]
        first = functools.reduce(jnp.logical_and, [i == 0 for i in ids])
        last = functools.reduce(jnp.logical_and, [i == g - 1 for i, g in zip(ids, grid)])

        @pl.when(first)
        def _():
            _exchange_start(_exchange_copies(x_in, x_out, bcast, *sems))

        kern(*ins, *outs, *sc)

        @pl.when(last)
        def _():
            _exchange_wait(_exchange_copies(x_in, x_out, bcast, *sems))

    hbm = pl.BlockSpec(memory_space=pltpu.HBM)
    out = pl.pallas_call(
        hosted, in_specs=list(in_specs) + [hbm] * n_x, out_specs=list(out_specs) + [hbm] * n_x,
        out_shape=list(out_shape) + _exchange_shapes(arrays, bcast),
        scratch_shapes=list(scratch) + _exchange_sems(n_x), **kw)(*args, *arrays)
    return out[:n_out], out[n_out:]


def _mm_nn(a, b, *, n0=0, n=None, out_dtype=F32, res=None, tm=1664, tn=512, tk=None, name="mm_nn", exch=None):
    m, k = a.shape
    n = b.shape[1] - n0 if n is None else n
    tm = _row_tile(m, tm)
    tk = k if tk is None else tk
    nk = k // tk
    assert k % tk == 0 and n % tn == 0 and n0 % tn == 0
    nb0 = n0 // tn

    def kern(*refs):
        if res is None:
            a_ref, b_ref, o_ref, acc = refs
        else:
            a_ref, b_ref, r_ref, o_ref, acc = refs
        kk = pl.program_id(2)
        row0 = pl.program_id(0) * tm

        def finish(prod):
            if res is None:
                o_ref[...] = prod.astype(out_dtype)
            else:
                o_ref[...] = (r_ref[...] + jnp.where(_valid_rows(row0, tm), prod, 0.0)).astype(out_dtype)

        if nk == 1:
            finish(_dot(a_ref[...].astype(BF16), b_ref[...].astype(BF16)))
            return

        @pl.when(kk == 0)
        def _():
            acc[...] = jnp.zeros_like(acc)

        acc[...] += _dot(a_ref[...].astype(BF16), b_ref[...].astype(BF16))

        @pl.when(kk == nk - 1)
        def _():
            finish(acc[...])

    in_specs = [pl.BlockSpec((tm, tk), lambda i, j, kk: (i, kk)),
                pl.BlockSpec((tk, tn), lambda i, j, kk: (kk, nb0 + j))]
    args = [a, b]
    if res is not None:
        in_specs.append(pl.BlockSpec((tm, tn), lambda i, j, kk: (i, j)))
        args.append(res)
    out, got = _pcall(
        kern, grid=(m // tm, n // tn, nk), in_specs=in_specs,
        out_specs=[pl.BlockSpec((tm, tn), lambda i, j, kk: (i, j))],
        out_shape=[jax.ShapeDtypeStruct((m, n), out_dtype)],
        scratch=[pltpu.VMEM((tm, tn) if nk > 1 else (8, 128), F32)],
        sem=("parallel", "parallel", "arbitrary"), name=name, args=args, exch=exch)
    return out[0] if exch is None else (out[0], got)


def _mm_nt(a, b, *, k0=0, kw=None, out_dtype=F32, add=None, tm=640, tn=None, tk=2048, name="mm_nt", exch=None):
    m = a.shape[0]
    kw = a.shape[1] if kw is None else kw
    nn = b.shape[0]
    tm = _row_tile(m, tm)
    tn = min(nn, 2048) if tn is None else tn
    tk = min(tk, kw)
    assert kw % tk == 0 and k0 % tk == 0 and nn % tn == 0 and a.shape[1] == kw
    nk = kw // tk
    kb0 = k0 // tk

    def kern(*refs):
        if add is None:
            a_ref, b_ref, o_ref, acc = refs
        else:
            a_ref, b_ref, d_ref, o_ref, acc = refs
        kk = pl.program_id(2)

        def finish(prod):
            o_ref[...] = (prod if add is None else prod + d_ref[...]).astype(out_dtype)

        if nk == 1:
            finish(_dot_nt(a_ref[...].astype(BF16), b_ref[...].astype(BF16)))
            return

        @pl.when(kk == 0)
        def _():
            acc[...] = jnp.zeros_like(acc)

        acc[...] += _dot_nt(a_ref[...].astype(BF16), b_ref[...].astype(BF16))

        @pl.when(kk == nk - 1)
        def _():
            finish(acc[...])

    in_specs = [pl.BlockSpec((tm, tk), lambda i, j, kk: (i, kk)),
                pl.BlockSpec((tn, tk), lambda i, j, kk: (j, kb0 + kk))]
    args = [a, b]
    if add is not None:
        in_specs.append(pl.BlockSpec((tm, tn), lambda i, j, kk: (i, j)))
        args.append(add)
    out, got = _pcall(
        kern, grid=(m // tm, nn // tn, nk), in_specs=in_specs,
        out_specs=[pl.BlockSpec((tm, tn), lambda i, j, kk: (i, j))],
        out_shape=[jax.ShapeDtypeStruct((m, nn), out_dtype)],
        scratch=[pltpu.VMEM((tm, tn) if nk > 1 else (8, 128), F32)],
        sem=("parallel", "parallel", "arbitrary"), name=name, args=args, exch=exch)
    return out[0] if exch is None else (out[0], got)


def _mm_tn(a, b, *, out_dtype=BF16, tm=1664, tk=None, tn=None, name="mm_tn", exch=None):
    m, k = a.shape
    n = b.shape[1]
    tm = _row_tile(m, tm)
    tk = k if tk is None else tk
    if tn is None:
        tn = 1024 if n % 1024 == 0 else 512
    assert k % tk == 0 and n % tn == 0
    nm = m // tm

    def kern(a_ref, b_ref, o_ref, acc):
        mm = pl.program_id(2)

        @pl.when(mm == 0)
        def _():
            acc[...] = jnp.zeros_like(acc)

        acc[...] += _dot_tn(a_ref[...].astype(BF16), b_ref[...].astype(BF16))

        @pl.when(mm == nm - 1)
        def _():
            o_ref[...] = acc[...].astype(out_dtype)

    out, got = _pcall(
        kern, grid=(k // tk, n // tn, nm),
        in_specs=[pl.BlockSpec((tm, tk), lambda i, j, mm: (mm, i)),
                  pl.BlockSpec((tm, tn), lambda i, j, mm: (mm, j))],
        out_specs=[pl.BlockSpec((tk, tn), lambda i, j, mm: (i, j))],
        out_shape=[jax.ShapeDtypeStruct((k, n), out_dtype)],
        scratch=[pltpu.VMEM((tk, tn), F32)],
        sem=("parallel", "parallel", "arbitrary"), name=name, args=(a, b), exch=exch)
    return out[0] if exch is None else (out[0], got)


def _rmsnorm_fwd(h, g, tr=640):
    lp = h.shape[0]
    tr = _row_tile(lp, tr)

    def kern(h_ref, g_ref, o_ref):
        x = h_ref[...]
        r = lax.rsqrt(jnp.mean(x * x, axis=-1, keepdims=True) + EPS)
        o_ref[...] = (x * r * g_ref[...]).astype(BF16)

    return pl.pallas_call(
        kern, grid=(lp // tr,),
        in_specs=[pl.BlockSpec((tr, D), lambda i: (i, 0)), pl.BlockSpec((1, D), lambda i: (0, 0))],
        out_specs=pl.BlockSpec((tr, D), lambda i: (i, 0)),
        out_shape=jax.ShapeDtypeStruct((lp, D), BF16),
        compiler_params=_cparams("parallel"), name="rmsnorm_fwd")(h, g)


def _rmsnorm_bwd(h, g, dxn, dres, tr=640):
    lp = h.shape[0]
    tr = _row_tile(lp, tr)

    def kern(h_ref, g_ref, dxn_ref, dres_ref, dh_ref, dg_ref):
        i = pl.program_id(0)
        x = h_ref[...]
        r = lax.rsqrt(jnp.mean(x * x, axis=-1, keepdims=True) + EPS)
        xhat = x * r
        dy = jnp.where(_valid_rows(i * tr, tr), dxn_ref[...], 0.0)

        @pl.when(i == 0)
        def _():
            dg_ref[...] = jnp.zeros_like(dg_ref)

        dg_ref[...] += jnp.sum(dy * xhat, axis=0, keepdims=True)
        dxh = dy * g_ref[...]
        dh_ref[...] = dres_ref[...] + r * (dxh - xhat * jnp.mean(dxh * xhat, axis=-1, keepdims=True))

    row = pl.BlockSpec((tr, D), lambda i: (i, 0))
    vec = pl.BlockSpec((1, D), lambda i: (0, 0))
    return pl.pallas_call(
        kern, grid=(lp // tr,), in_specs=[row, vec, row, row], out_specs=[row, vec],
        out_shape=[jax.ShapeDtypeStruct((lp, D), F32), jax.ShapeDtypeStruct((1, D), F32)],
        compiler_params=_cparams("arbitrary"), name="rmsnorm_bwd")(h, g, dxn, dres)


def _shift_down(xe, k):
    return xe if k == 0 else pltpu.roll(xe, k, 0)


def _shift_up(xe, k):
    return xe if k == 0 else pltpu.roll(xe, xe.shape[0] - k, 0)


def _conv_ext(xe, w_ref):
    return w_ref[2:3, :] * xe + w_ref[1:2, :] * _shift_down(xe, 1) + w_ref[0:1, :] * _shift_down(xe, 2)


def _halo_specs(tr, width, col_of, nrows, rows_first):
    r8 = tr // 8
    last8 = nrows // 8 - 1
    if rows_first:
        prev = pl.BlockSpec((8, width), lambda i, j: (jnp.maximum(i * r8 - 1, 0), col_of(j)))
        nxt = pl.BlockSpec((8, width), lambda i, j: (jnp.minimum((i + 1) * r8, last8), col_of(j)))
    else:
        prev = pl.BlockSpec((8, width), lambda j, i: (jnp.maximum(i * r8 - 1, 0), col_of(j)))
        nxt = pl.BlockSpec((8, width), lambda j, i: (jnp.minimum((i + 1) * r8, last8), col_of(j)))
    return prev, nxt


def _convb_fwd(proj_r, conv_w, tr=640):
    lp = proj_r.shape[0]
    tr = _row_tile(lp, tr)
    tc = CONV_TC
    gw = 3 * tc

    def kern(g_ref, gp_ref, w_ref, o_ref):
        i = pl.program_id(0)
        g = g_ref[...]
        p = g[:, tc:2 * tc] * g[:, 2 * tc:]
        gp = gp_ref[...]
        pp = jnp.where(i > 0, gp[:, tc:2 * tc] * gp[:, 2 * tc:], 0.0)
        y = _conv_ext(jnp.concatenate([pp, p], axis=0), w_ref)[8:]
        o_ref[...] = (g[:, :tc] * y).astype(BF16)

    prev, _ = _halo_specs(tr, gw, lambda j: R_CONV_BLK0 + j, lp, True)
    return pl.pallas_call(
        kern, grid=(lp // tr, CONV_CH // tc),
        in_specs=[pl.BlockSpec((tr, gw), lambda i, j: (i, R_CONV_BLK0 + j)), prev,
                  pl.BlockSpec((3, tc), lambda i, j: (0, j))],
        out_specs=pl.BlockSpec((tr, tc), lambda i, j: (i, j)),
        out_shape=jax.ShapeDtypeStruct((lp, CONV_CH), BF16),
        compiler_params=_cparams("parallel", "parallel"), name="convb_fwd")(proj_r, proj_r, conv_w)


def _convb_bwd(proj_r, dzb, conv_w, dproj, tr=640):
    lp = proj_r.shape[0]
    tr = _row_tile(lp, tr)
    nr = lp // tr
    tc = CONV_TC
    gw = 3 * tc

    def kern(g_ref, gp_ref, gn_ref, dz_ref, dzn_ref, w_ref, dp_any, dg_ref, dw_ref):
        del dp_any
        i = pl.program_id(1)
        g = g_ref[...]
        b, c, hh = g[:, :tc], g[:, tc:2 * tc], g[:, 2 * tc:]
        p = c * hh
        gp = gp_ref[...]
        pp = jnp.where(i > 0, gp[:, tc:2 * tc] * gp[:, 2 * tc:], 0.0)
        pe = jnp.concatenate([pp, p], axis=0)
        s1 = _shift_down(pe, 1)[8:]
        s2 = _shift_down(pe, 2)[8:]
        y = w_ref[2:3, :] * p + w_ref[1:2, :] * s1 + w_ref[0:1, :] * s2
        dz = dz_ref[...]
        dy = dz * b
        dyn = jnp.where(i < nr - 1, dzn_ref[...] * gn_ref[...][:, :tc], 0.0)
        dye = jnp.concatenate([dy, dyn], axis=0)
        dp = (w_ref[2:3, :] * dy + w_ref[1:2, :] * _shift_up(dye, 1)[:tr]
              + w_ref[0:1, :] * _shift_up(dye, 2)[:tr])
        valid = _valid_rows(i * tr, tr)
        dg_ref[...] = jnp.where(valid, jnp.concatenate([dz * y, dp * hh, dp * c], axis=1), 0.0).astype(BF16)

        @pl.when(i == 0)
        def _():
            dw_ref[...] = jnp.zeros_like(dw_ref)

        dw_ref[0:1, :] += jnp.sum(dy * s2, axis=0, keepdims=True)
        dw_ref[1:2, :] += jnp.sum(dy * s1, axis=0, keepdims=True)
        dw_ref[2:3, :] += jnp.sum(dy * p, axis=0, keepdims=True)

    gprev, gnext = _halo_specs(tr, gw, lambda j: R_CONV_BLK0 + j, lp, False)
    _, dznext = _halo_specs(tr, tc, lambda j: j, lp, False)
    return pl.pallas_call(
        kern, grid=(CONV_CH // tc, nr),
        in_specs=[pl.BlockSpec((tr, gw), lambda j, i: (i, R_CONV_BLK0 + j)), gprev, gnext,
                  pl.BlockSpec((tr, tc), lambda j, i: (i, j)), dznext,
                  pl.BlockSpec((3, tc), lambda j, i: (0, j)),
                  pl.BlockSpec(memory_space=pl.ANY)],
        out_specs=[pl.BlockSpec((tr, gw), lambda j, i: (i, F_CONV_BLK0 + j)),
                   pl.BlockSpec((3, tc), lambda j, i: (0, j))],
        out_shape=[jax.ShapeDtypeStruct(dproj.shape, BF16), jax.ShapeDtypeStruct((3, CONV_CH), F32)],
        input_output_aliases={6: 0},
        compiler_params=_cparams("parallel", "arbitrary"), name="convb_bwd",
    )(proj_r, proj_r, proj_r, dzb, dzb, conv_w, dproj)


MLP_TC = 256


def _mlp_gate_fwd(z, w, tr=640):
    lp = z.shape[0]
    tr = _row_tile(lp, tr)
    tc = 512
    nc = D_FF // tc

    def kern(zg_ref, zgp_ref, zu_ref, zup_ref, wg_ref, wu_ref, o_ref):
        i = pl.program_id(0)
        zge = jnp.concatenate([jnp.where(i > 0, zgp_ref[...], 0.0), zg_ref[...]], axis=0)
        zue = jnp.concatenate([jnp.where(i > 0, zup_ref[...], 0.0), zu_ref[...]], axis=0)
        ug = _conv_ext(zge, wg_ref)[8:]
        uu = _conv_ext(zue, wu_ref)[8:]
        o_ref[...] = (ug * _sigmoid(ug) * uu).astype(BF16)

    gprev, _ = _halo_specs(tr, tc, lambda j: j, lp, True)
    uprev, _ = _halo_specs(tr, tc, lambda j: nc + j, lp, True)
    return pl.pallas_call(
        kern, grid=(lp // tr, nc),
        in_specs=[pl.BlockSpec((tr, tc), lambda i, j: (i, j)), gprev,
                  pl.BlockSpec((tr, tc), lambda i, j: (i, nc + j)), uprev,
                  pl.BlockSpec((3, tc), lambda i, j: (0, j)),
                  pl.BlockSpec((3, tc), lambda i, j: (0, nc + j))],
        out_specs=pl.BlockSpec((tr, tc), lambda i, j: (i, j)),
        out_shape=jax.ShapeDtypeStruct((lp, D_FF), BF16),
        compiler_params=_cparams("parallel", "parallel"), name="mlp_gate_fwd")(z, z, z, z, w, w)


def _mlp_gate_bwd(z, da, w, tr=640, exch=None):
    lp = z.shape[0]
    tr = _row_tile(lp, tr)
    nr = lp // tr
    tc = MLP_TC
    nc = D_FF // tc

    def kern(zg_ref, zgp_ref, zgn_ref, zu_ref, zup_ref, zun_ref, da_ref, dan_ref, wg_ref, wu_ref,
             dzg_ref, dzu_ref, dwg_ref, dwu_ref):
        i = pl.program_id(1)
        first, last = i == 0, i == nr - 1

        def ext(m_ref, p_ref, n_ref):
            return jnp.concatenate([jnp.where(first, 0.0, p_ref[...]), m_ref[...],
                                    jnp.where(last, 0.0, n_ref[...])], axis=0)

        zge, zue = ext(zg_ref, zgp_ref, zgn_ref), ext(zu_ref, zup_ref, zun_ref)
        ug = _conv_ext(zge, wg_ref)[8:]
        uu = _conv_ext(zue, wu_ref)[8:]
        dae = jnp.concatenate([da_ref[...], jnp.where(last, 0.0, dan_ref[...])], axis=0)
        sg = _sigmoid(ug)
        dug = dae * uu * (sg * (1.0 + ug * (1.0 - sg)))
        duu = dae * (ug * sg)
        valid = _valid_rows(i * tr, tr)

        @pl.when(first)
        def _():
            dwg_ref[...] = jnp.zeros_like(dwg_ref)
            dwu_ref[...] = jnp.zeros_like(dwu_ref)

        for du, ze, w_ref, dz_ref, dw_ref in ((dug, zge, wg_ref, dzg_ref, dwg_ref),
                                              (duu, zue, wu_ref, dzu_ref, dwu_ref)):
            dz = (w_ref[2:3, :] * du + w_ref[1:2, :] * _shift_up(du, 1) + w_ref[0:1, :] * _shift_up(du, 2))[:tr]
            dz_ref[...] = jnp.where(valid, dz, 0.0).astype(BF16)
            dum = du[:tr]
            for kk in range(3):
                dw_ref[kk:kk + 1, :] += jnp.sum(dum * _shift_down(ze, 2 - kk)[8:8 + tr], axis=0, keepdims=True)

    gprev, gnext = _halo_specs(tr, tc, lambda j: j, lp, False)
    uprev, unext = _halo_specs(tr, tc, lambda j: nc + j, lp, False)
    main = pl.BlockSpec((tr, tc), lambda j, i: (i, j))
    wspec = pl.BlockSpec((3, tc), lambda j, i: (0, j))
    out, got = _pcall(
        kern, grid=(nc, nr),
        in_specs=[main, gprev, gnext, pl.BlockSpec((tr, tc), lambda j, i: (i, nc + j)), uprev, unext,
                  main, gnext, wspec, pl.BlockSpec((3, tc), lambda j, i: (0, nc + j))],
        out_specs=[main, main, wspec, wspec],
        out_shape=[jax.ShapeDtypeStruct((lp, D_FF), BF16), jax.ShapeDtypeStruct((lp, D_FF), BF16),
                   jax.ShapeDtypeStruct((3, D_FF), F32), jax.ShapeDtypeStruct((3, D_FF), F32)],
        scratch=[], sem=("parallel", "arbitrary"), name="mlp_gate_bwd",
        args=(z, z, z, z, z, z, da, da, w, w), exch=exch)
    return (*out, got)


def _tri(n, lower):
    r = lax.broadcasted_iota(jnp.int32, (n, n), 0)
    c = lax.broadcasted_iota(jnp.int32, (n, n), 1)
    return jnp.where((c <= r) if lower else (c >= r), 1.0, 0.0).astype(F32)


def _dot_exact(a, b):
    return jnp.dot(a, b, preferred_element_type=F32, precision=lax.Precision.HIGHEST)


def _fox_gate_fwd(proj_r, bf128):
    lp = proj_r.shape[0]
    nb = lp // BLK

    def kern(s_ref, b_ref, c_ref):
        tri = _tri(BLK, True)

        def body(i, carry):
            rows = pl.ds(pl.multiple_of(i * BLK, BLK), BLK)
            lf = jnp.where(_valid_rows(i * BLK, BLK), _log_sigmoid(s_ref[rows, :] + b_ref[...]), 0.0)
            cs = _dot_exact(tri, lf) + carry
            c_ref[rows, :] = cs
            return cs[BLK - 1:BLK, :]

        lax.fori_loop(0, nb, body, jnp.zeros((1, BLK), F32))

    return pl.pallas_call(
        kern, grid=(1,),
        in_specs=[pl.BlockSpec((lp, BLK), lambda i: (0, R_SMALL_BLK128)), pl.BlockSpec((1, BLK), lambda i: (0, 0))],
        out_specs=pl.BlockSpec((lp, BLK), lambda i: (0, 0)),
        out_shape=jax.ShapeDtypeStruct((lp, BLK), F32),
        compiler_params=_cparams("arbitrary"), name="fox_gate_fwd")(proj_r, bf128)


def _fox_gate_bwd(proj_r, dc, bf128):
    lp = proj_r.shape[0]
    nb = lp // BLK

    def kern(s_ref, dc_ref, b_ref, dfa_ref, dbf_ref):
        tri = _tri(BLK, False)

        dbf_ref[...] = jnp.zeros_like(dbf_ref)

        def body(ii, run):
            i = nb - 1 - ii
            rows = pl.ds(pl.multiple_of(i * BLK, BLK), BLK)
            dcb = dc_ref[rows, :]
            suf = _dot_exact(tri, dcb) + run
            dfa = jnp.where(_valid_rows(i * BLK, BLK), suf * _sigmoid(-(s_ref[rows, :] + b_ref[...])), 0.0)
            dfa_ref[rows, :] = dfa
            dbf_ref[...] += jnp.sum(dfa, axis=0, keepdims=True)
            return run + jnp.sum(dcb, axis=0, keepdims=True)

        lax.fori_loop(0, nb, body, jnp.zeros((1, BLK), F32))

    return pl.pallas_call(
        kern, grid=(1,),
        in_specs=[pl.BlockSpec((lp, BLK), lambda i: (0, R_SMALL_BLK128)), pl.BlockSpec((lp, BLK), lambda i: (0, 0)),
                  pl.BlockSpec((1, BLK), lambda i: (0, 0))],
        out_specs=[pl.BlockSpec((lp, BLK), lambda i: (0, 0)), pl.BlockSpec((1, BLK), lambda i: (0, 0))],
        out_shape=[jax.ShapeDtypeStruct((lp, BLK), F32), jax.ShapeDtypeStruct((1, BLK), F32)],
        compiler_params=_cparams("arbitrary"), name="fox_gate_bwd")(proj_r, dc, bf128)


LOG2E = 1.4426950408889634
KEY_PAD_BIAS = 1e30


def _attn_logits2(q, k, ck, diag):
    t = _dot_nt(q, k) * (LOG2E * FOX_DH ** -0.5) - ck * LOG2E
    if diag:
        r = lax.broadcasted_iota(jnp.int32, t.shape, 0)
        c = lax.broadcasted_iota(jnp.int32, t.shape, 1)
        t = jnp.where(c <= r, t, NEG)
    return t


ATTN_HEADS = 2


def _head_cols(a):
    return (slice(a * FOX_DH, (a + 1) * FOX_DH), slice(2 * a * FOX_DH, (2 * a + 1) * FOX_DH),
            slice((2 * a + 1) * FOX_DH, (2 * a + 2) * FOX_DH))


def _on_blocks(i, j, step):
    pl.when(j < i)(functools.partial(step, False))
    pl.when(j == i)(functools.partial(step, True))


def _attn_fwd(proj_a, cq, ck, tq=640, exch=None):
    lp = proj_a.shape[0]
    tq = _row_tile(lp, tq)
    tk = tq
    nq = lp // tq

    def kern(q_ref, kv_ref, cq_ref, ck_ref, o_ref, lse_ref, m_sc, l_sc, acc):
        i, j = pl.program_id(1), pl.program_id(2)

        @pl.when(j == 0)
        def _():
            m_sc[...] = jnp.full_like(m_sc, -jnp.inf)
            l_sc[...] = jnp.zeros_like(l_sc)
            acc[...] = jnp.zeros_like(acc)

        def step(diag):
            for a in range(ATTN_HEADS):
                hq, hk, hv = _head_cols(a)
                cq2 = cq_ref[a] * LOG2E
                t = _attn_logits2(q_ref[:, hq], kv_ref[:, hk], ck_ref[a], diag)
                m_old = m_sc[a]
                m_new = jnp.maximum(m_old, jnp.max(t, axis=-1, keepdims=True) + cq2)
                p = jnp.exp2(t + (cq2 - m_new))
                alpha = jnp.exp2(m_old - m_new)
                l_sc[a] = alpha * l_sc[a] + jnp.sum(p, axis=-1, keepdims=True)
                acc[:, hq] = alpha * acc[:, hq] + _dot(p.astype(BF16), kv_ref[:, hv])
                m_sc[a] = m_new

        _on_blocks(i, j, step)

        @pl.when(j == nq - 1)
        def _():
            valid = _valid_rows(i * tq, tq)
            for a in range(ATTN_HEADS):
                hq, _, _ = _head_cols(a)
                o_ref[:, hq] = jnp.where(valid, acc[:, hq] / l_sc[a], 0.0).astype(BF16)
                lse_ref[a] = m_sc[a] + jnp.log(l_sc[a]) * LOG2E

    hp = ATTN_HEADS
    out, got = _pcall(
        kern, grid=(FOX_H // hp, nq, nq),
        in_specs=[pl.BlockSpec((tq, hp * FOX_DH), lambda h, i, j: (i, h)),
                  pl.BlockSpec((tk, 2 * hp * FOX_DH), lambda h, i, j: (jnp.minimum(j, i), KV0 // (2 * hp * FOX_DH) + h)),
                  pl.BlockSpec((hp, tq, 1), lambda h, i, j: (h, i, 0)),
                  pl.BlockSpec((hp, 1, tk), lambda h, i, j: (h, 0, jnp.minimum(j, i)))],
        out_specs=[pl.BlockSpec((tq, hp * FOX_DH), lambda h, i, j: (i, h)),
                   pl.BlockSpec((hp, tq, 1), lambda h, i, j: (h, i, 0))],
        out_shape=[jax.ShapeDtypeStruct((lp, FOX_W), BF16), jax.ShapeDtypeStruct((FOX_H, lp, 1), F32)],
        scratch=[pltpu.VMEM((hp, tq, 1), F32), pltpu.VMEM((hp, tq, 1), F32), pltpu.VMEM((tq, hp * FOX_DH), F32)],
        sem=("parallel", "parallel", "arbitrary"), name="attn_fwd", args=(proj_a, proj_a, cq, ck), exch=exch)
    return out[0], out[1], got


def _attn_bwd_dq(proj_a, do, o, lse, cq, ck, dproj, tq=640, exch=None):
    lp = proj_a.shape[0]
    tq = _row_tile(lp, tq)
    tk = tq
    nq = lp // tq

    def kern(q_ref, kv_ref, do_ref, o_ref, lse_ref, cq_ref, ck_ref, dp_any, dq_ref, dcq_ref, dl_ref, dq_acc):
        del dp_any
        i, j = pl.program_id(1), pl.program_id(2)

        @pl.when(j == 0)
        def _():
            for a in range(ATTN_HEADS):
                hq, _, _ = _head_cols(a)
                dl_ref[a] = jnp.sum(do_ref[:, hq].astype(F32) * o_ref[:, hq].astype(F32), axis=-1, keepdims=True)
            dcq_ref[...] = jnp.zeros_like(dcq_ref)
            dq_acc[...] = jnp.zeros_like(dq_acc)

        def step(diag):
            for a in range(ATTN_HEADS):
                hq, hk, hv = _head_cols(a)
                t = _attn_logits2(q_ref[:, hq], kv_ref[:, hk], ck_ref[a], diag)
                p = jnp.exp2(t + (cq_ref[a] * LOG2E - lse_ref[a]))
                ds = p * (_dot_nt(do_ref[:, hq], kv_ref[:, hv]) - dl_ref[a])
                dcq_ref[a] += jnp.sum(ds, axis=-1, keepdims=True)
                dq_acc[:, hq] += _dot(ds.astype(BF16), kv_ref[:, hk])

        _on_blocks(i, j, step)

        @pl.when(j == nq - 1)
        def _():
            dq_ref[...] = (dq_acc[...] * (FOX_DH ** -0.5)).astype(BF16)

    hp = ATTN_HEADS
    qspec = pl.BlockSpec((tq, hp * FOX_DH), lambda h, i, j: (i, h))
    col = pl.BlockSpec((hp, tq, 1), lambda h, i, j: (h, i, 0))
    out, got = _pcall(
        kern, grid=(FOX_H // hp, nq, nq),
        in_specs=[qspec, pl.BlockSpec((tk, 2 * hp * FOX_DH),
                                      lambda h, i, j: (jnp.minimum(j, i), KV0 // (2 * hp * FOX_DH) + h)),
                  qspec, qspec, col, col,
                  pl.BlockSpec((hp, 1, tk), lambda h, i, j: (h, 0, jnp.minimum(j, i))),
                  pl.BlockSpec(memory_space=pl.ANY)],
        out_specs=[qspec, col, col],
        out_shape=[jax.ShapeDtypeStruct(dproj.shape, BF16), jax.ShapeDtypeStruct((FOX_H, lp, 1), F32),
                   jax.ShapeDtypeStruct((FOX_H, lp, 1), F32)],
        scratch=[pltpu.VMEM((tq, hp * FOX_DH), F32)], aliases={7: 0},
        sem=("parallel", "parallel", "arbitrary"), name="attn_bwd_dq",
        args=(proj_a, proj_a, do, o, lse, cq, ck, dproj), exch=exch)
    return out[0], out[1], out[2], got


def _attn_bwd_dkv(proj_a, do, lse, delta, cq, ck, dproj, tq=640, exch=None):
    lp = proj_a.shape[0]
    tq = _row_tile(lp, tq)
    tk = tq
    nq = lp // tq

    def kern(q_ref, kv_ref, do_ref, lse_ref, dl_ref, cq_ref, ck_ref, dp_any, dkv_ref, dck_ref, dk_acc, dv_acc):
        del dp_any
        j, i = pl.program_id(1), pl.program_id(2)

        @pl.when(i == 0)
        def _():
            dck_ref[...] = jnp.zeros_like(dck_ref)
            dk_acc[...] = jnp.zeros_like(dk_acc)
            dv_acc[...] = jnp.zeros_like(dv_acc)

        def step(diag):
            for a in range(ATTN_HEADS):
                hq, hk, hv = _head_cols(a)
                t = _attn_logits2(q_ref[:, hq], kv_ref[:, hk], ck_ref[a], diag)
                p = jnp.exp2(t + (cq_ref[a] * LOG2E - lse_ref[a]))
                dv_acc[:, hq] += _dot_tn(p.astype(BF16), do_ref[:, hq])
                ds = p * (_dot_nt(do_ref[:, hq], kv_ref[:, hv]) - dl_ref[a])
                dck_ref[a] -= jnp.sum(ds, axis=0, keepdims=True)
                dk_acc[:, hq] += _dot_tn(ds.astype(BF16), q_ref[:, hq])

        _on_blocks(i, j, step)

        @pl.when(i == nq - 1)
        def _():
            parts = []
            for a in range(ATTN_HEADS):
                hq, _, _ = _head_cols(a)
                parts += [dk_acc[:, hq] * (FOX_DH ** -0.5), dv_acc[:, hq]]
            dkv_ref[...] = jnp.concatenate(parts, axis=1).astype(BF16)

    hp = ATTN_HEADS
    qspec = pl.BlockSpec((tq, hp * FOX_DH), lambda h, j, i: (jnp.maximum(i, j), h))
    col = pl.BlockSpec((hp, tq, 1), lambda h, j, i: (h, jnp.maximum(i, j), 0))
    kvspec = pl.BlockSpec((tk, 2 * hp * FOX_DH), lambda h, j, i: (j, KV0 // (2 * hp * FOX_DH) + h))
    rowspec = pl.BlockSpec((hp, 1, tk), lambda h, j, i: (h, 0, j))
    out, got = _pcall(
        kern, grid=(FOX_H // hp, nq, nq),
        in_specs=[qspec, kvspec, qspec, col, col, col, rowspec, pl.BlockSpec(memory_space=pl.ANY)],
        out_specs=[kvspec, rowspec],
        out_shape=[jax.ShapeDtypeStruct(dproj.shape, BF16), jax.ShapeDtypeStruct((FOX_H, 1, lp), F32)],
        scratch=[pltpu.VMEM((tk, hp * FOX_DH), F32), pltpu.VMEM((tk, hp * FOX_DH), F32)], aliases={7: 0},
        sem=("parallel", "parallel", "arbitrary"), name="attn_bwd_dkv",
        args=(proj_a, proj_a, do, lse, delta, cq, ck, dproj), exch=exch)
    return out[0], out[1], got


def _gla_gate_fwd(proj_r, wg2p, bg, tr=640):
    lp = proj_r.shape[0]
    tr = _row_tile(lp, tr)
    w = GLA_H * GLA_DK

    def kern(s_ref, w_ref, b_ref, o_ref):
        zg = _dot(s_ref[...].astype(BF16), w_ref[...]) + b_ref[...]
        o_ref[...] = jnp.where(_valid_rows(pl.program_id(0) * tr, tr), _log_sigmoid(zg) * (1.0 / GLA_TAU), 0.0)

    return pl.pallas_call(
        kern, grid=(lp // tr,),
        in_specs=[pl.BlockSpec((tr, BLK), lambda i: (i, R_SMALL_BLK128)), pl.BlockSpec((BLK, w), lambda i: (0, 0)),
                  pl.BlockSpec((1, w), lambda i: (0, 0))],
        out_specs=pl.BlockSpec((tr, w), lambda i: (i, 0)),
        out_shape=jax.ShapeDtypeStruct((lp, w), F32),
        compiler_params=_cparams("parallel"), name="gla_gate_fwd")(proj_r, wg2p, bg)


def _gla_chunk(grp, g):
    q = grp[:, :GLA_DK] * (GLA_DK ** -0.5)
    k = grp[:, GLA_DK:2 * GLA_DK]
    v = grp[:, 2 * GLA_DK:2 * GLA_DK + GLA_DV]
    r = grp[:, 2 * GLA_DK + GLA_DV:]
    b = _dot_exact(_tri(BLK, True), g)
    bl = b[BLK - 1:BLK, :]
    eb = jnp.exp(b)
    enb = jnp.exp(-b)
    ebl = jnp.exp(bl - b)
    qe, ke, kd = q * eb, k * enb, k * ebl
    causal = lax.broadcasted_iota(jnp.int32, (BLK, BLK), 1) <= lax.broadcasted_iota(jnp.int32, (BLK, BLK), 0)
    att = jnp.where(causal, _dot_nt(qe.astype(BF16), ke.astype(BF16)), 0.0)
    return q, k, v, r, bl, eb, enb, ebl, qe, ke, kd, causal, att


def _gla_fwd(proj_r, logg, gn):
    lp = proj_r.shape[0]
    nc = lp // BLK
    wv = GLA_H * GLA_DV

    def kern(grp_ref, g_ref, gn_ref, o_ref, zc_ref, st_ref, st):
        c = pl.program_id(0)

        @pl.when(c == 0)
        def _():
            st[...] = jnp.zeros_like(st)

        for h in range(GLA_H):
            kcol = slice(h * GLA_DK, (h + 1) * GLA_DK)
            vcol = slice(h * GLA_DV, (h + 1) * GLA_DV)
            q, k, v, r, bl, eb, enb, ebl, qe, ke, kd, causal, att = _gla_chunk(
                grp_ref[:, h * GLA_GRP:(h + 1) * GLA_GRP], g_ref[:, kcol])
            s_t = st[h]
            st_ref[h] = s_t
            vb = v.astype(BF16)
            o = _dot(att.astype(BF16), vb) + _dot_nt(qe.astype(BF16), s_t.astype(BF16))
            st[h] = s_t * jnp.exp(bl) + _dot_tn(vb, kd.astype(BF16))
            o_ref[:, vcol] = o
            rstd = lax.rsqrt(jnp.mean(o * o, axis=-1, keepdims=True) + EPS)
            zc_ref[:, vcol] = (r * _sigmoid(r) * (o * rstd * gn_ref[:, vcol])).astype(BF16)

    vspec = pl.BlockSpec((BLK, wv), lambda c: (c, 0))
    return pl.pallas_call(
        kern, grid=(nc,),
        in_specs=[pl.BlockSpec((BLK, GLA_H * GLA_GRP), lambda c: (c, R_GLA_BLK0 // GLA_H)),
                  pl.BlockSpec((BLK, GLA_H * GLA_DK), lambda c: (c, 0)),
                  pl.BlockSpec((1, wv), lambda c: (0, 0))],
        out_specs=[vspec, vspec, pl.BlockSpec((GLA_H, None, GLA_DV, GLA_DK), lambda c: (0, c, 0, 0))],
        out_shape=[jax.ShapeDtypeStruct((lp, wv), F32), jax.ShapeDtypeStruct((lp, wv), BF16),
                   jax.ShapeDtypeStruct((GLA_H, nc, GLA_DV, GLA_DK), F32)],
        scratch_shapes=[pltpu.VMEM((GLA_H, GLA_DV, GLA_DK), F32)],
        compiler_params=_cparams("arbitrary"), name="gla_fwd")(proj_r, logg, gn)


def _gla_bwd(proj_r, logg, st_all, o_all, dzc, gn, dproj):
    lp = proj_r.shape[0]
    nc = lp // BLK

    def kern(grp_ref, g_ref, st_ref, o_ref, dzc_ref, gn_ref, dp_any, dgrp_ref, dlg_ref, dgn_ref, dst):
        del dp_any
        cc = pl.program_id(0)

        @pl.when(cc == 0)
        def _():
            dst[...] = jnp.zeros_like(dst)
            dgn_ref[...] = jnp.zeros_like(dgn_ref)

        for h in range(GLA_H):
            kcol = slice(h * GLA_DK, (h + 1) * GLA_DK)
            vcol = slice(h * GLA_DV, (h + 1) * GLA_DV)
            q, k, v, r, bl, eb, enb, ebl, qe, ke, kd, causal, att = _gla_chunk(
                grp_ref[:, h * GLA_GRP:(h + 1) * GLA_GRP], g_ref[:, kcol])
            s_t = st_ref[h]
            d_st = dst[h]
            o = o_ref[:, vcol]
            dzc_v = dzc_ref[:, vcol]
            gnv = gn_ref[:, vcol]
            rstd = lax.rsqrt(jnp.mean(o * o, axis=-1, keepdims=True) + EPS)
            xhat = o * rstd
            sr = _sigmoid(r)
            dr = dzc_v * (xhat * gnv) * (sr * (1.0 + r * (1.0 - sr)))
            docn = dzc_v * (r * sr)
            dgn_ref[:, vcol] += jnp.sum(docn * xhat, axis=0, keepdims=True)
            dxh = docn * gnv
            do = rstd * (dxh - xhat * jnp.mean(dxh * xhat, axis=-1, keepdims=True))
            dob, vb = do.astype(BF16), v.astype(BF16)
            qeb, keb, kdb = qe.astype(BF16), ke.astype(BF16), kd.astype(BF16)
            datt = jnp.where(causal, _dot_nt(dob, vb), 0.0).astype(BF16)
            dv = _dot_tn(att.astype(BF16), dob) + _dot_nt(kdb, d_st.astype(BF16))
            dqe = _dot(datt, keb) + _dot(dob, s_t.astype(BF16))
            dke = _dot_tn(datt, qeb)
            dkd = _dot(vb, d_st.astype(BF16))
            dq = dqe * eb * (GLA_DK ** -0.5)
            dk = dke * enb + dkd * ebl
            kd_dkd = dkd * kd
            db = dqe * qe - dke * ke - kd_dkd
            db_last = (jnp.sum(kd_dkd, axis=0, keepdims=True)
                       + jnp.exp(bl) * jnp.sum(s_t * d_st, axis=0, keepdims=True))
            dlg_ref[:, kcol] = _dot_exact(_tri(BLK, False), db) + db_last
            dst[h] = d_st * jnp.exp(bl) + _dot_tn(dob, qeb)
            dgrp_ref[:, h * GLA_GRP:(h + 1) * GLA_GRP] = jnp.concatenate([dq, dk, dv, dr], axis=1).astype(BF16)

    rev = lambda c: nc - 1 - c
    wv = GLA_H * GLA_DV
    vspec = pl.BlockSpec((BLK, wv), lambda c: (rev(c), 0))
    kspec = pl.BlockSpec((BLK, GLA_H * GLA_DK), lambda c: (rev(c), 0))
    return pl.pallas_call(
        kern, grid=(nc,),
        in_specs=[pl.BlockSpec((BLK, GLA_H * GLA_GRP), lambda c: (rev(c), R_GLA_BLK0 // GLA_H)), kspec,
                  pl.BlockSpec((GLA_H, None, GLA_DV, GLA_DK), lambda c: (0, rev(c), 0, 0)),
                  vspec, vspec, pl.BlockSpec((1, wv), lambda c: (0, 0)),
                  pl.BlockSpec(memory_space=pl.ANY)],
        out_specs=[pl.BlockSpec((BLK, GLA_H * GLA_GRP), lambda c: (rev(c), F_GLA_BLK0 // GLA_H)), kspec,
                   pl.BlockSpec((1, wv), lambda c: (0, 0))],
        out_shape=[jax.ShapeDtypeStruct(dproj.shape, BF16), jax.ShapeDtypeStruct((lp, GLA_H * GLA_DK), F32),
                   jax.ShapeDtypeStruct((1, wv), F32)],
        scratch_shapes=[pltpu.VMEM((GLA_H, GLA_DV, GLA_DK), F32)],
        input_output_aliases={6: 0},
        compiler_params=_cparams("arbitrary"), name="gla_bwd",
    )(proj_r, logg, st_all, o_all, dzc, gn, dproj)


def _small_bwd(proj_r, dlogg, wg2p, wg2pt, bg, dfa, dproj, tr=640):
    lp = proj_r.shape[0]
    tr = _row_tile(lp, tr)
    w = GLA_H * GLA_DK

    def kern(s_ref, dlg_ref, w_ref, wt_ref, b_ref, dfa_ref, dp_any, ds_ref, dbg_ref, dw_ref):
        del dp_any
        i = pl.program_id(0)
        sb = s_ref[...].astype(BF16)
        zg = _dot(sb, w_ref[...]) + b_ref[...]
        dzg = jnp.where(_valid_rows(i * tr, tr), dlg_ref[...] * (1.0 / GLA_TAU) * _sigmoid(-zg), 0.0)

        @pl.when(i == 0)
        def _():
            dbg_ref[...] = jnp.zeros_like(dbg_ref)
            dw_ref[...] = jnp.zeros_like(dw_ref)

        dbg_ref[...] += jnp.sum(dzg, axis=0, keepdims=True)
        dzb = dzg.astype(BF16)
        dw_ref[...] += _dot_tn(sb, dzb)
        dsm = _dot(dzb, wt_ref[...]) + dfa_ref[...]
        ds_ref[...] = jnp.concatenate([dsm, jnp.zeros((tr, SMALL_W - BLK), F32)], axis=1).astype(BF16)

    return pl.pallas_call(
        kern, grid=(lp // tr,),
        in_specs=[pl.BlockSpec((tr, BLK), lambda i: (i, R_SMALL_BLK128)), pl.BlockSpec((tr, w), lambda i: (i, 0)),
                  pl.BlockSpec((BLK, w), lambda i: (0, 0)), pl.BlockSpec((w, BLK), lambda i: (0, 0)),
                  pl.BlockSpec((1, w), lambda i: (0, 0)), pl.BlockSpec((tr, BLK), lambda i: (i, 0)),
                  pl.BlockSpec(memory_space=pl.ANY)],
        out_specs=[pl.BlockSpec((tr, SMALL_W), lambda i: (i, F_SMALL_BLK0)), pl.BlockSpec((1, w), lambda i: (0, 0)),
                   pl.BlockSpec((BLK, w), lambda i: (0, 0))],
        out_shape=[jax.ShapeDtypeStruct(dproj.shape, BF16), jax.ShapeDtypeStruct((1, w), F32),
                   jax.ShapeDtypeStruct((BLK, w), F32)],
        input_output_aliases={6: 0},
        compiler_params=_cparams("arbitrary"), name="small_bwd",
    )(proj_r, dlogg, wg2p, wg2pt, bg, dfa, dproj)


def _merge_fwd(proj_r, gate_b3, ya, yb, yc, tr=640):
    lp = proj_r.shape[0]
    tr = _row_tile(lp, tr)
    tn = GATE_TN

    def kern(g_ref, b_ref, ya_ref, yb_ref, yc_ref, o_ref):
        g = g_ref[...]
        mix = (_sigmoid(g[:, :tn] + b_ref[0:1, :]) * ya_ref[...]
               + _sigmoid(g[:, tn:2 * tn] + b_ref[1:2, :]) * yb_ref[...]
               + _sigmoid(g[:, 2 * tn:] + b_ref[2:3, :]) * yc_ref[...])
        o_ref[...] = mix.astype(BF16)

    y = pl.BlockSpec((tr, tn), lambda i, j: (i, j))
    return pl.pallas_call(
        kern, grid=(lp // tr, D // tn),
        in_specs=[pl.BlockSpec((tr, 3 * tn), lambda i, j: (i, R_GATE_BLK0 + j)),
                  pl.BlockSpec((3, tn), lambda i, j: (0, j)), y, y, y],
        out_specs=y, out_shape=jax.ShapeDtypeStruct((lp, D), BF16),
        compiler_params=_cparams("parallel", "parallel"), name="merge_fwd")(proj_r, gate_b3, ya, yb, yc)


def _merge_bwd(proj_r, gate_b3, ya, yb, yc, dmix, tr=640):
    lp = proj_r.shape[0]
    tr = _row_tile(lp, tr)
    tn = GATE_TN

    def kern(g_ref, b_ref, ya_ref, yb_ref, yc_ref, dm_ref, dya_ref, dyb_ref, dyc_ref, dg_ref, db_ref):
        i = pl.program_id(1)
        g = g_ref[...]
        dm = dm_ref[...]

        @pl.when(i == 0)
        def _():
            db_ref[...] = jnp.zeros_like(db_ref)

        dgs = []
        for n, (y_ref, dy_ref) in enumerate(((ya_ref, dya_ref), (yb_ref, dyb_ref), (yc_ref, dyc_ref))):
            s = _sigmoid(g[:, n * tn:(n + 1) * tn] + b_ref[n:n + 1, :])
            dy_ref[...] = (dm * s).astype(BF16)
            dgn = dm * y_ref[...] * (s * (1.0 - s))
            db_ref[n:n + 1, :] += jnp.sum(dgn, axis=0, keepdims=True)
            dgs.append(dgn)
        dg_ref[...] = jnp.concatenate(dgs, axis=1).astype(BF16)

    y = pl.BlockSpec((tr, tn), lambda j, i: (i, j))
    bspec = pl.BlockSpec((3, tn), lambda j, i: (0, j))
    return pl.pallas_call(
        kern, grid=(D // tn, lp // tr),
        in_specs=[pl.BlockSpec((tr, 3 * tn), lambda j, i: (i, R_GATE_BLK0 + j)), bspec, y, y, y, y],
        out_specs=[y, y, y, pl.BlockSpec((tr, 3 * tn), lambda j, i: (i, F_GATE_BLK0 + j)), bspec],
        out_shape=[jax.ShapeDtypeStruct((lp, D), BF16)] * 3
        + [jax.ShapeDtypeStruct((lp, NP), BF16), jax.ShapeDtypeStruct((3, D), F32)],
        compiler_params=_cparams("parallel", "arbitrary"), name="merge_bwd")(proj_r, gate_b3, ya, yb, yc, dmix)


def _final_loss(h, gf, tgt):
    lp = h.shape[0]
    nb = lp // BLK

    def kern(h_ref, g_ref, t_ref, dh_ref, dg_ref, ls_ref):
        i = pl.program_id(0)

        @pl.when(i == 0)
        def _():
            dh_ref[...] = jnp.zeros_like(dh_ref)
            dg_ref[...] = jnp.zeros_like(dg_ref)
            ls_ref[...] = jnp.zeros_like(ls_ref)

        @pl.when(i > 0)
        def _():
            x = h_ref[...]
            r = lax.rsqrt(jnp.mean(x * x, axis=-1, keepdims=True) + EPS)
            xhat = x * r
            err = xhat * g_ref[...] - t_ref[...]
            ls_ref[...] += jnp.sum(jnp.sum(err * err, axis=0, keepdims=True), axis=1, keepdims=True)
            dy = err * (1.0 / D)
            dg_ref[...] += jnp.sum(dy * xhat, axis=0, keepdims=True)
            dxh = dy * g_ref[...]
            dh_ref[...] = r * (dxh - xhat * jnp.mean(dxh * xhat, axis=-1, keepdims=True))

    row = pl.BlockSpec((BLK, D), lambda i: (i, 0))
    vec = pl.BlockSpec((1, D), lambda i: (0, 0))
    return pl.pallas_call(
        kern, grid=(nb,),
        in_specs=[row, vec, pl.BlockSpec((BLK, D), lambda i: (jnp.maximum(i - 1, 0), 0))],
        out_specs=[row, vec, pl.BlockSpec((1, 1), lambda i: (0, 0))],
        out_shape=[jax.ShapeDtypeStruct((lp, D), F32), jax.ShapeDtypeStruct((1, D), F32),
                   jax.ShapeDtypeStruct((1, 1), F32)],
        compiler_params=_cparams("arbitrary"), name="final_loss")(h, gf, tgt)


def _gate_cols(c):
    ct = c[:, :FOX_H].T
    ck = jnp.where(jnp.arange(ct.shape[1]) < PAD, KEY_PAD_BIAS, ct)
    return ct[:, :, None], ck[:, None, :]


def _run(hosts, name, ctx, fn):
    if hosts and name in hosts:
        make, done = hosts[name]
        res = fn(make(ctx))
        done(res[-1])
    else:
        res = fn(None)
    return res[:-1]


def _mm_nn_x(a, b, exch, **kw):
    out = _mm_nn(a, b, exch=exch, **kw)
    return out if exch is not None else (out, None)


def _layer_fwd(h, w, hosts=None):
    xn = _rmsnorm_fwd(h, w["norm1_g"])
    proj_a = _mm_nn(xn, w["w_in"], n0=0, n=REST0, out_dtype=BF16, name="proj_a")
    proj_r, = _run(hosts, "proj_r", w, lambda e: _mm_nn_x(xn, w["w_in"], e, n0=REST0, n=NREST, name="proj_r"))
    cq, ck = _gate_cols(_fox_gate_fwd(proj_r, w["bf128"]))
    oa, lse = _run(hosts, "attn_fwd", w, lambda e: _attn_fwd(proj_a, cq, ck, exch=e))
    zb = _convb_fwd(proj_r, w["conv_w"])
    logg = _gla_gate_fwd(proj_r, w["wg2p"], w["gla_b_g"])
    o_gla, zc, st_all = _gla_fwd(proj_r, logg, w["gla_norm_g"])
    ya = _mm_nn(oa, w["w_a_o"], name="branch_a")
    yb = _mm_nn(zb, w["w_b_o"], name="branch_b")
    yc = _mm_nn(zc, w["w_c_o"], name="branch_c")
    mix = _merge_fwd(proj_r, w["gate_b3"], ya, yb, yc)
    h1 = _mm_nn(mix, w["w_o"], res=h, name="out_proj")
    xn2 = _rmsnorm_fwd(h1, w["norm2_g"])
    z, = _run(hosts, "up_proj", w, lambda e: _mm_nn_x(xn2, w["w_up"], e, name="up_proj"))
    a = _mlp_gate_fwd(z, w["mlp_conv_w"])
    h2, = _run(hosts, "down_proj", w,
               lambda e: _mm_nn_x(a, w["w_down"], e, res=h1, tk=D_FF // 4, name="down_proj"))
    saved = dict(h=h, xn=xn, proj_a=proj_a, proj_r=proj_r, cq=cq, ck=ck, oa=oa, lse=lse, zb=zb, logg=logg,
                 o_gla=o_gla, zc=zc, st_all=st_all, ya=ya, yb=yb, yc=yc, mix=mix, h1=h1, xn2=xn2, z=z, a=a)
    return h2, saved


def _layer_bwd(dh2, w, s, hosts=None):
    g = {}
    da = _mm_nt(dh2, w["w_down"], tn=D_FF // 4, name="d_down_in")
    g["w_down"] = _mm_tn(s["a"], dh2, tk=D_FF // 4, name="d_w_down")
    dzg, dzu, dmw_g, dmw_u = _run(hosts, "mlp_gate_bwd", g, lambda e: _mlp_gate_bwd(
        s["z"], da, w["mlp_conv_w"], exch=e))
    g["mlp_conv_w"] = jnp.concatenate([dmw_g, dmw_u], axis=1)
    dxn2 = _mm_nt(dzg, w["w_up"], k0=0, kw=D_FF, tk=D_FF // 4, name="d_up_in_g")
    dxn2 = _mm_nt(dzu, w["w_up"], k0=D_FF, kw=D_FF, tk=D_FF // 4, add=dxn2, name="d_up_in_u")
    g["w_up"] = jnp.concatenate([_mm_tn(s["xn2"], dzg, name="d_w_up_g"), _mm_tn(s["xn2"], dzu, name="d_w_up_u")], axis=1)
    dh1, g["norm2_g"] = _rmsnorm_bwd(s["h1"], w["norm2_g"], dxn2, dh2)
    dmix = _mm_nt(dh1, w["w_o"], name="d_out_proj_in")
    g["w_o"] = _mm_tn(s["mix"], dh1, name="d_w_o")
    dya, dyb, dyc, dproj, g["gate_b3"] = _merge_bwd(s["proj_r"], w["gate_b3"], s["ya"], s["yb"], s["yc"], dmix)
    doa = _mm_nt(dya, w["w_a_o"], out_dtype=BF16, name="d_branch_a_in")
    g["w_a_o"] = _mm_tn(s["oa"], dya, name="d_w_a_o")
    dzb = _mm_nt(dyb, w["w_b_o"], name="d_branch_b_in")
    g["w_b_o"] = _mm_tn(s["zb"], dyb, name="d_w_b_o")
    dzc = _mm_nt(dyc, w["w_c_o"], name="d_branch_c_in")
    g["w_c_o"] = _mm_tn(s["zc"], dyc, name="d_w_c_o")
    dproj, dlogg, g["gla_norm_g"] = _gla_bwd(s["proj_r"], s["logg"], s["st_all"], s["o_gla"], dzc, w["gla_norm_g"], dproj)
    dproj, g["conv_w"] = _convb_bwd(s["proj_r"], dzb, w["conv_w"], dproj)
    dproj, dcq, delta = _run(hosts, "attn_bwd_dq", g, lambda e: _attn_bwd_dq(
        s["proj_a"], doa, s["oa"], s["lse"], s["cq"], s["ck"], dproj, exch=e))
    dproj, dck = _run(hosts, "attn_bwd_dkv", g, lambda e: _attn_bwd_dkv(
        s["proj_a"], doa, s["lse"], delta, s["cq"], s["ck"], dproj, exch=e))
    dc = jnp.pad((dcq[:, :, 0] + dck[:, 0, :]).T, ((0, 0), (0, BLK - FOX_H)))
    dfa, g["bf128"] = _fox_gate_bwd(s["proj_r"], dc, w["bf128"])
    dproj, g["gla_b_g"], g["wg2p"] = _small_bwd(s["proj_r"], dlogg, w["wg2p"], w["wg2p"].T, w["gla_b_g"], dfa, dproj)
    def pair(out, e):
        return out if e is not None else (out, None)

    g["w_in"], = _run(hosts, "d_w_in", g, lambda e: pair(_mm_tn(s["xn"], dproj, name="d_w_in", exch=e), e))
    dxn, = _run(hosts, "d_in_proj_in", g, lambda e: pair(_mm_nt(dproj, w["w_in"], name="d_in_proj_in", exch=e), e))
    dh0, g["norm1_g"] = _rmsnorm_bwd(s["h"], w["norm1_g"], dxn, dh1)
    return dh0, g


def _local_step(x, tgt, meta, final_g, layers, hosts_fwd=None, hosts_bwd=None):
    h = jnp.concatenate([jnp.zeros((PAD, D), F32), meta, x], axis=0)
    saved = []
    for l, w in enumerate(layers):
        h, s = _layer_fwd(h, w, hosts_fwd[l] if hosts_fwd else None)
        saved.append(s)
    dh, dgf, sq = _final_loss(h, final_g, tgt)
    grads = [None] * len(layers)
    for l in reversed(range(len(layers))):
        dh, grads[l] = _layer_bwd(dh, layers[l], saved[l], hosts_bwd[l](grads) if hosts_bwd else None)
    return sq[0, 0], dh[BLK:], dh[PAD:BLK], dgf, grads


def _w_in_to_kernel(w_nat):
    parts = [w_nat[:, s:s + n] for s, n in _segments()]
    parts.append(jnp.zeros((w_nat.shape[0], SMALL_W - 8 - GLA_R), w_nat.dtype))
    return jnp.concatenate(parts, axis=1)


def _w_in_from_kernel(w_k):
    pieces, off = [], 0
    for s, n in _segments():
        pieces.append((s, w_k[:, off:off + n]))
        off += n
    return jnp.concatenate([p for _, p in sorted(pieces, key=lambda t: t[0])], axis=1)


def _w_in_slots_to_kernel(got):
    per = got.shape[2]
    parts = []
    for s, n in _segments():
        while n > 0:
            d, lo = divmod(s, per)
            take = min(n, per - lo)
            parts.append(got[d, :, lo:lo + take])
            s, n = s + take, n - take
    parts.append(jnp.zeros((got.shape[1], SMALL_W - 8 - GLA_R), got.dtype))
    return jnp.concatenate(parts, axis=1)


def _w_in_kernel_to_slots(w_k):
    per = N_IN // N_DEV
    pieces, off = [], 0
    for s, n in _segments():
        pieces.append((s, n, off))
        off += n
    slots = []
    for d in range(N_DEV):
        lo, hi = d * per, (d + 1) * per
        parts = [w_k[:, off + max(s, lo) - s:off + min(s + n, hi) - s]
                 for s, n, off in sorted(pieces) if max(s, lo) < min(s + n, hi)]
        slots.append(jnp.concatenate(parts, axis=1))
    return jnp.stack(slots)


def _pad_rows_at(a, row0, nrows):
    return jnp.pad(a, ((row0, nrows - row0 - a.shape[0]), (0, 0)))


def _big_to_kernel(name, full):
    return _w_in_to_kernel(full) if name == "w_in" else full


def _layer_weights(big, conv_w, gla_w_g2, mlp_conv_w, norm1_g, fox_b_f, gate_b, gla_b_g, gla_norm_g, norm2_g):
    w = {n: _big_to_kernel(n, a) for n, a in big.items()}
    w.update(
        conv_w=conv_w, mlp_conv_w=mlp_conv_w,
        wg2p=_pad_rows_at(gla_w_g2, 8, BLK).astype(BF16),
        norm1_g=norm1_g[None], norm2_g=norm2_g[None], gla_b_g=gla_b_g[None], gla_norm_g=gla_norm_g[None],
        bf128=jnp.pad(fox_b_f, (0, BLK - FOX_H))[None], gate_b3=gate_b.reshape(3, D))
    return w


def _layer_grads_natural(g):
    return dict(
        w_in=_w_in_from_kernel(g["w_in"]), w_a_o=g["w_a_o"], w_b_o=g["w_b_o"], w_c_o=g["w_c_o"], w_o=g["w_o"],
        w_up=g["w_up"], w_down=g["w_down"], conv_w=g["conv_w"], mlp_conv_w=g["mlp_conv_w"],
        gla_w_g2=g["wg2p"][8:8 + GLA_R], norm1_g=g["norm1_g"][0], norm2_g=g["norm2_g"][0],
        gla_b_g=g["gla_b_g"][0], gla_norm_g=g["gla_norm_g"][0], fox_b_f=g["bf128"][0, :FOX_H],
        gate_b=g["gate_b3"].reshape(3 * D))


def _adamw(recv, w, m, v, layer, prev=None, name="adamw"):
    n_slot, r, c = recv.shape
    lyr = w.shape[0]
    tr = r
    for t in range(16, r, 16):
        if r % t == 0 and t * c <= ADAMW_BLOCK_ELEMS:
            tr = t
    if r * c <= ADAMW_BLOCK_ELEMS:
        tr = r
    bc1, bc2 = 1.0 - ADAM_B1 ** ADAM_STEP, 1.0 - ADAM_B2 ** ADAM_STEP

    def kern(*refs):
        r_ref, w_ref, m_ref, v_ref = refs[:4]
        g_out, d_out, m_out, v_out = refs[-4:]
        g = r_ref[0].astype(F32)
        for sidx in range(1, n_slot):
            g = g + r_ref[sidx].astype(F32)
        m_new = ADAM_B1 * m_ref[...] + (1.0 - ADAM_B1) * g
        v_new = ADAM_B2 * v_ref[...] + (1.0 - ADAM_B2) * (g * g)
        g_out[...] = g
        m_out[...] = m_new
        v_out[...] = v_new
        d_out[...] = -ADAM_LR * ((m_new / bc1) / (jnp.sqrt(v_new / bc2) + ADAM_EPS) + ADAM_WD * w_ref[...])

    lspec = pl.BlockSpec((None, tr, c), lambda i: (layer, i, 0))
    in_specs = [pl.BlockSpec((n_slot, tr, c), lambda i: (0, i, 0)), lspec, lspec, lspec]
    args = [recv, w, m, v]
    aliases = {}
    if prev is not None:
        in_specs += [pl.BlockSpec(memory_space=pl.ANY)] * 4
        args += list(prev)
        aliases = {4: 0, 5: 1, 6: 2, 7: 3}
    return pl.pallas_call(
        kern, grid=(r // tr,), in_specs=in_specs, out_specs=[lspec] * 4,
        out_shape=[jax.ShapeDtypeStruct((lyr, r, c), F32)] * 4, input_output_aliases=aliases,
        compiler_params=_cparams("parallel"), name=name)(*args)


_BIG = ("w_in", "w_a_o", "w_b_o", "w_c_o", "w_o", "w_up", "w_down")
_COL_SHARDED = ("w_in", "w_a_o", "w_b_o", "w_c_o", "w_up", "conv_w", "gla_w_g2", "mlp_conv_w")
_REPL = ("norm1_g", "fox_b_f", "gate_b", "gla_b_g", "gla_norm_g", "norm2_g")


def _cols_from_slots(a):
    return jnp.transpose(a, (1, 0, 2)).reshape(a.shape[1], N_DEV * a.shape[2])


def _cols_to_slots(a):
    r, c8 = a.shape
    return jnp.transpose(a.reshape(r, N_DEV, c8 // N_DEV), (1, 0, 2))


def _rows_to_slots(a):
    return a.reshape(N_DEV, a.shape[0] // N_DEV, a.shape[1])


def kernel(x, meta_tokens, norm1_g, w_in, fox_b_f, gate_b, conv_w, gla_w_g2, gla_b_g, gla_norm_g, w_a_o, w_b_o, w_c_o, w_o, norm2_g, w_up, mlp_conv_w, w_down, final_norm_g, loss_target, m_meta_tokens, m_norm1_g, m_w_in, m_fox_b_f, m_gate_b, m_conv_w, m_gla_w_g2, m_gla_b_g, m_gla_norm_g, m_w_a_o, m_w_b_o, m_w_c_o, m_w_o, m_norm2_g, m_w_up, m_mlp_conv_w, m_w_down, m_final_norm_g, v_meta_tokens, v_norm1_g, v_w_in, v_fox_b_f, v_gate_b, v_conv_w, v_gla_w_g2, v_gla_b_g, v_gla_norm_g, v_w_a_o, v_w_b_o, v_w_c_o, v_w_o, v_norm2_g, v_w_up, v_mlp_conv_w, v_w_down, v_final_norm_g):
    names = ("meta_tokens", "norm1_g", "w_in", "fox_b_f", "gate_b", "conv_w", "gla_w_g2", "gla_b_g", "gla_norm_g",
             "w_a_o", "w_b_o", "w_c_o", "w_o", "norm2_g", "w_up", "mlp_conv_w", "w_down", "final_norm_g")
    wts = dict(zip(names, (meta_tokens, norm1_g, w_in, fox_b_f, gate_b, conv_w, gla_w_g2, gla_b_g, gla_norm_g,
                           w_a_o, w_b_o, w_c_o, w_o, norm2_g, w_up, mlp_conv_w, w_down, final_norm_g)))
    mom = dict(zip(names, (m_meta_tokens, m_norm1_g, m_w_in, m_fox_b_f, m_gate_b, m_conv_w, m_gla_w_g2, m_gla_b_g,
                           m_gla_norm_g, m_w_a_o, m_w_b_o, m_w_c_o, m_w_o, m_norm2_g, m_w_up, m_mlp_conv_w, m_w_down,
                           m_final_norm_g)))
    var = dict(zip(names, (v_meta_tokens, v_norm1_g, v_w_in, v_fox_b_f, v_gate_b, v_conv_w, v_gla_w_g2, v_gla_b_g,
                           v_gla_norm_g, v_w_a_o, v_w_b_o, v_w_c_o, v_w_o, v_norm2_g, v_w_up, v_mlp_conv_w, v_w_down,
                           v_final_norm_g)))

    small = _exchange([conv_w, gla_w_g2, mlp_conv_w, meta_tokens], [True] * 4, "gather_small")
    conv_full = jnp.transpose(small[0], (1, 2, 0, 3)).reshape(DEPTH, 3, CONV_CH)
    g2_full = jnp.transpose(small[1], (1, 2, 0, 3)).reshape(DEPTH, GLA_R, GLA_H * GLA_DK)
    mconv_full = jnp.transpose(small[2], (1, 2, 0, 3)).reshape(DEPTH, 3, 2 * D_FF)
    meta_full = _cols_from_slots(small[3])
    layers = [_layer_weights({}, conv_full[l], g2_full[l], mconv_full[l], norm1_g[l], fox_b_f[l], gate_b[l],
                             gla_b_g[l], gla_norm_g[l], norm2_g[l]) for l in range(DEPTH)]

    wide = ("w_a_o", "w_b_o", "w_c_o", "w_up")

    def gather(l, which):
        def make(_):
            return [wts[n][l].astype(BF16) for n in which], [("wide" if n in wide else True) for n in which]

        def done(got):
            for n, a in zip(which, got):
                if n == "w_in":
                    layers[l][n] = _w_in_slots_to_kernel(a)
                else:
                    layers[l][n] = a if n in wide else a.reshape(-1, a.shape[-1])

        return make, done

    recv_big = [dict() for _ in range(DEPTH)]

    def scatter(l, which, grads_of):
        def make(ctx):
            g = grads_of(ctx)
            send = [_w_in_kernel_to_slots(g[n]) if n == "w_in" else g[n] if n in wide else _rows_to_slots(g[n])
                    for n in which]
            return send, [("cols" if n in wide else False) for n in which]

        def done(got):
            recv_big[l].update(zip(which, got))

        return make, done

    mixers = ("w_o", "w_a_o", "w_b_o", "w_c_o")
    early = ("w_down", "w_up") + mixers
    make, done = gather(0, ("w_in",))
    done(_exchange(*make(None), "gather_w_in"))
    hosts_fwd = [
        {"proj_r": gather(0, mixers + ("w_down",)), "attn_fwd": gather(0, ("w_up",)),
         "up_proj": gather(1, ("w_in",)), "down_proj": gather(1, mixers)},
        {"attn_fwd": gather(1, ("w_up", "w_down"))}]
    hosts_bwd = [
        lambda grads: {"mlp_gate_bwd": scatter(1, ("w_in",), lambda _: grads[1]),
                       "attn_bwd_dq": scatter(1, ("w_up", "w_down"), lambda _: grads[1]),
                       "attn_bwd_dkv": scatter(1, mixers, lambda _: grads[1]),
                       "d_w_in": scatter(0, early, lambda g: g),
                       "d_in_proj_in": scatter(0, ("w_in",), lambda g: g)},
        lambda grads: None]

    sq, grad_x, dmeta, dgf, grads_k = _local_step(x[0], loss_target[0], meta_full, final_norm_g[None], layers,
                                                  hosts_fwd, hosts_bwd)
    loss = lax.psum(sq * (0.5 / D), ("x", "y", "c"))
    grads = [_layer_grads_natural(g) for g in grads_k]

    out_g, out_d, out_m, out_v = {}, {}, {}, {}

    def update(name, recv, layer, lyr_shape, prev):
        w3, m3, v3 = (t[name].reshape(lyr_shape) for t in (wts, mom, var))
        return _adamw(recv.reshape((recv.shape[0],) + lyr_shape[1:]), w3, m3, v3, layer, prev, name="adamw_" + name)

    def store(name, res):
        shape = wts[name].shape
        out_g[name], out_d[name], out_m[name], out_v[name] = (t.reshape(shape) for t in res)

    for n in _BIG:
        res = None
        for l in range(DEPTH):
            res = update(n, recv_big[l][n], l, wts[n].shape, res)
        store(n, res)

    def stack_layers(name):
        return jnp.stack([grads[l][name] for l in range(DEPTH)])

    s_conv = jnp.transpose(stack_layers("conv_w").reshape(DEPTH, 3, N_DEV, -1), (2, 0, 1, 3))
    s_g2 = jnp.transpose(stack_layers("gla_w_g2").reshape(DEPTH, GLA_R, N_DEV, -1), (2, 0, 1, 3))
    s_mconv = jnp.transpose(stack_layers("mlp_conv_w").reshape(DEPTH, 3, N_DEV, -1), (2, 0, 1, 3))
    s_meta = _cols_to_slots(dmeta)
    repl = [stack_layers(n) for n in _REPL] + [dgf]
    pack = jnp.concatenate([jnp.pad(a.reshape(-1), (0, (-a.size) % 1024)) for a in repl]).reshape(-1, BLK)
    r_conv, r_g2, r_mconv, r_meta, r_pack = _exchange(
        [s_conv, s_g2, s_mconv, s_meta, pack], [False, False, False, False, True], "scatter_small")
    store("conv_w", update("conv_w", r_conv, 0, (1, DEPTH * 3, CONV_CH // N_DEV), None))
    store("gla_w_g2", update("gla_w_g2", r_g2, 0, (1, DEPTH * GLA_R, GLA_H * GLA_DK // N_DEV), None))
    store("mlp_conv_w", update("mlp_conv_w", r_mconv, 0, (1, DEPTH * 3, 2 * D_FF // N_DEV), None))
    store("meta_tokens", update("meta_tokens", r_meta, 0, (1, N_META, D // N_DEV), None))
    off = 0
    for n, a in zip(_REPL + ("final_norm_g",), repl):
        rows = (a.size + 1023) // 1024 * 8
        part = r_pack[:, off:off + rows].reshape(N_DEV, -1)[:, :a.size]
        off += rows
        shape2 = (1, 1, a.size) if a.size % BLK else (1, a.size // BLK, BLK)
        store(n, update(n, part, 0, shape2, None))

    order = lambda d: [d[n] for n in names]
    return (loss, grad_x[None], *order(out_g), *order(out_d), *order(out_m), *order(out_v))
```

```python
import functools

import jax
import jax.numpy as jnp
from jax import lax
from jax.experimental import pallas as pl
from jax.experimental.pallas import tpu as pltpu

F32 = jnp.float32
BF16 = jnp.bfloat16

D = 2048
DEPTH = 2
N_META = 16
BLK = 128
PAD = BLK - N_META
EPS = 1e-6
NEG = -1e30

FOX_H, FOX_DH = 8, 128
FOX_W = FOX_H * FOX_DH
CONV_CH = 1024
GLA_H, GLA_DK, GLA_DV, GLA_R, GLA_TAU = 4, 128, 256, 16, 16.0
D_FF = 5632
N_IN = 15384
N_DEV = 8

ADAM_LR, ADAM_B1, ADAM_B2, ADAM_EPS, ADAM_WD, ADAM_STEP = 0.001, 0.9, 0.999, 1e-08, 0.01, 10

CONV_TC = 512
GATE_TN = 512
KV0 = 1024
REST0 = 3072
GLA_GRP = 768
SMALL_W = 1024
NP = 16384
NREST = NP - REST0
R_CONV_BLK0 = 0
R_GLA_BLK0 = (6144 - REST0) // GLA_GRP
R_GATE_BLK0 = (9216 - REST0) // (3 * GATE_TN)
R_SMALL_BLK128 = (15360 - REST0) // 128
F_CONV_BLK0 = 3072 // (3 * CONV_TC)
F_GLA_BLK0 = 6144 // GLA_GRP
F_GATE_BLK0 = 9216 // (3 * GATE_TN)
F_SMALL_BLK0 = 15360 // SMALL_W

VMEM_LIMIT = 56 * 1024 * 1024
ADAMW_BLOCK_ELEMS = 128 * 1024


def _segments():
    seg = [(0, 1024)]
    for h in range(FOX_H):
        seg += [(1024 + 128 * h, 128), (2048 + 128 * h, 128)]
    for j in range(CONV_CH // CONV_TC):
        seg += [(3080 + CONV_TC * j, CONV_TC), (4104 + CONV_TC * j, CONV_TC), (5128 + CONV_TC * j, CONV_TC)]
    for h in range(GLA_H):
        seg += [(6152 + 128 * h, 128), (6664 + 128 * h, 128), (7176 + 256 * h, 256), (8200 + 256 * h, 256)]
    for j in range(D // GATE_TN):
        seg += [(9240 + GATE_TN * j, GATE_TN), (11288 + GATE_TN * j, GATE_TN), (13336 + GATE_TN * j, GATE_TN)]
    seg += [(3072, 8), (9224, 16)]
    return seg


def _cparams(*sem):
    return pltpu.CompilerParams(dimension_semantics=sem, vmem_limit_bytes=VMEM_LIMIT)


def _row_tile(n, target):
    best = BLK
    t = BLK
    while t <= min(n, target):
        if n % t == 0:
            best = t
        t += BLK
    return best


def _sigmoid(x):
    return 1.0 / (1.0 + jnp.exp(-x))


def _log_sigmoid(x):
    return jnp.minimum(x, 0.0) - jnp.log(1.0 + jnp.exp(-jnp.abs(x)))


def _valid_rows(row0, n):
    return (row0 + lax.broadcasted_iota(jnp.int32, (n, 1), 0)) >= PAD


def _dot(a, b):
    return jnp.dot(a, b, preferred_element_type=F32)


def _dot_nt(a, b):
    return lax.dot_general(a, b, (((1,), (1,)), ((), ())), preferred_element_type=F32)


def _dot_tn(a, b):
    return lax.dot_general(a, b, (((0,), (0,)), ((), ())), preferred_element_type=F32)


def _exchange_copies(ins, outs, bcast, send_sems, recv_sems, local_sems):
    x, y, c = lax.axis_index("x"), lax.axis_index("y"), lax.axis_index("c")
    me = 4 * x + 2 * y + c

    def src_for(n, dev):
        if bcast[n] is True or bcast[n] == "wide":
            return ins[n]
        if bcast[n] == "cols":
            w = ins[n].shape[1] // N_DEV
            return ins[n].at[:, pl.ds(pl.multiple_of(dev * w, BLK), w)]
        return ins[n].at[dev]

    def dst_of(n, dev):
        if bcast[n] == "wide":
            w = ins[n].shape[1]
            return outs[n].at[:, pl.ds(pl.multiple_of(dev * w, BLK), w)]
        return outs[n].at[dev]

    local, sends, recvs = [], [], []
    for n in range(len(ins)):
        local.append(pltpu.make_async_copy(src_for(n, me), dst_of(n, me), local_sems.at[n]))
    for k in range(1, N_DEV):
        px = 1 - x if (k >> 2) & 1 else x
        py = 1 - y if (k >> 1) & 1 else y
        pc = 1 - c if k & 1 else c
        peer = 4 * px + 2 * py + pc
        for n in range(len(ins)):
            def copy(dst_dev, n=n, k=k, to=(px, py, pc), peer=peer):
                return pltpu.make_async_remote_copy(
                    src_ref=src_for(n, peer), dst_ref=dst_of(n, dst_dev), send_sem=send_sems.at[n, k - 1],
                    recv_sem=recv_sems.at[n, k - 1], device_id=to, device_id_type=pl.DeviceIdType.MESH)

            sends.append(copy(me))
            recvs.append(copy(peer))
    return local, sends, recvs


def _exchange_start(copies):
    local, sends, _ = copies
    for cp in local + sends:
        cp.start()


def _exchange_wait(copies):
    local, sends, recvs = copies
    for cp in recvs:
        cp.wait_recv()
    for cp in sends:
        cp.wait_send()
    for cp in local:
        cp.wait()


def _exchange_shapes(arrays, bcast):
    def shape(a, b):
        if b is True:
            return (N_DEV,) + a.shape
        if b == "wide":
            return (a.shape[0], N_DEV * a.shape[1])
        if b == "cols":
            return (N_DEV, a.shape[0], a.shape[1] // N_DEV)
        return a.shape

    return [jax.ShapeDtypeStruct(shape(a, b), a.dtype) for a, b in zip(arrays, bcast)]


def _exchange_sems(n_arr):
    return [pltpu.SemaphoreType.DMA((n_arr, N_DEV - 1)), pltpu.SemaphoreType.DMA((n_arr, N_DEV - 1)),
            pltpu.SemaphoreType.DMA((n_arr,))]


def _exchange(arrays, bcast, name):
    n_arr = len(arrays)

    def body(*refs):
        copies = _exchange_copies(refs[:n_arr], refs[n_arr:2 * n_arr], bcast, *refs[2 * n_arr:])
        _exchange_start(copies)
        _exchange_wait(copies)

    hbm = pl.BlockSpec(memory_space=pltpu.HBM)
    return pl.pallas_call(
        body, out_shape=_exchange_shapes(arrays, bcast), in_specs=[hbm] * n_arr, out_specs=[hbm] * n_arr,
        scratch_shapes=_exchange_sems(n_arr),
        compiler_params=pltpu.CompilerParams(has_side_effects=True), name=name)(*arrays)


def _pcall(kern, *, grid, in_specs, out_specs, out_shape, scratch, sem, name, args, aliases=None, exch=None):
    params = pltpu.CompilerParams(dimension_semantics=sem, vmem_limit_bytes=VMEM_LIMIT,
                                  has_side_effects=exch is not None)
    kw = dict(grid=grid, compiler_params=params, name=name, input_output_aliases=aliases or {})
    if exch is None:
        out = pl.pallas_call(kern, in_specs=in_specs, out_specs=out_specs, out_shape=out_shape,
                             scratch_shapes=scratch, **kw)(*args)
        return out, None
    arrays, bcast = exch
    n_x, n_in, n_out, n_sc = len(arrays), len(in_specs), len(out_specs), len(scratch)

    def hosted(*refs):
        ins, x_in = refs[:n_in], refs[n_in:n_in + n_x]
        outs, x_out = refs[n_in + n_x:n_in + n_x + n_out], refs[n_in + n_x + n_out:n_in + 2 * n_x + n_out]
        sc, sems = refs[n_in + 2 * n_x + n_out:n_in + 2 * n_x + n_out + n_sc], refs[n_in + 2 * n_x + n_out + n_sc:]
        ids = [pl.program_id(d) for d in range(len(grid))]
        first = functools.reduce(jnp.logical_and, [i == 0 for i in ids])
        last = functools.reduce(jnp.logical_and, [i == g - 1 for i, g in zip(ids, grid)])

        @pl.when(first)
        def _():
            _exchange_start(_exchange_copies(x_in, x_out, bcast, *sems))

        kern(*ins, *outs, *sc)

        @pl.when(last)
        def _():
            _exchange_wait(_exchange_copies(x_in, x_out, bcast, *sems))

    hbm = pl.BlockSpec(memory_space=pltpu.HBM)
    out = pl.pallas_call(
        hosted, in_specs=list(in_specs) + [hbm] * n_x, out_specs=list(out_specs) + [hbm] * n_x,
        out_shape=list(out_shape) + _exchange_shapes(arrays, bcast),
        scratch_shapes=list(scratch) + _exchange_sems(n_x), **kw)(*args, *arrays)
    return out[:n_out], out[n_out:]


def _mm_nn(a, b, *, n0=0, n=None, out_dtype=F32, res=None, tm=1664, tn=512, tk=None, name="mm_nn", exch=None):
    m, k = a.shape
    n = b.shape[1] - n0 if n is None else n
    tm = _row_tile(m, tm)
    tk = k if tk is None else tk
    nk = k // tk
    assert k % tk == 0 and n % tn == 0 and n0 % tn == 0
    nb0 = n0 // tn

    def kern(*refs):
        if res is None:
            a_ref, b_ref, o_ref, acc = refs
        else:
            a_ref, b_ref, r_ref, o_ref, acc = refs
        kk = pl.program_id(2)
        row0 = pl.program_id(0) * tm

        def finish(prod):
            if res is None:
                o_ref[...] = prod.astype(out_dtype)
            else:
                o_ref[...] = (r_ref[...] + jnp.where(_valid_rows(row0, tm), prod, 0.0)).astype(out_dtype)

        if nk == 1:
            finish(_dot(a_ref[...].astype(BF16), b_ref[...].astype(BF16)))
            return

        @pl.when(kk == 0)
        def _():
            acc[...] = jnp.zeros_like(acc)

        acc[...] += _dot(a_ref[...].astype(BF16), b_ref[...].astype(BF16))

        @pl.when(kk == nk - 1)
        def _():
            finish(acc[...])

    in_specs = [pl.BlockSpec((tm, tk), lambda i, j, kk: (i, kk)),
                pl.BlockSpec((tk, tn), lambda i, j, kk: (kk, nb0 + j))]
    args = [a, b]
    if res is not None:
        in_specs.append(pl.BlockSpec((tm, tn), lambda i, j, kk: (i, j)))
        args.append(res)
    out, got = _pcall(
        kern, grid=(m // tm, n // tn, nk), in_specs=in_specs,
        out_specs=[pl.BlockSpec((tm, tn), lambda i, j, kk: (i, j))],
        out_shape=[jax.ShapeDtypeStruct((m, n), out_dtype)],
        scratch=[pltpu.VMEM((tm, tn) if nk > 1 else (8, 128), F32)],
        sem=("parallel", "parallel", "arbitrary"), name=name, args=args, exch=exch)
    return out[0] if exch is None else (out[0], got)


def _mm_nt(a, b, *, k0=0, kw=None, out_dtype=F32, add=None, tm=640, tn=None, tk=2048, name="mm_nt", exch=None):
    m = a.shape[0]
    kw = a.shape[1] if kw is None else kw
    nn = b.shape[0]
    tm = _row_tile(m, tm)
    tn = min(nn, 2048) if tn is None else tn
    tk = min(tk, kw)
    assert kw % tk == 0 and k0 % tk == 0 and nn % tn == 0 and a.shape[1] == kw
    nk = kw // tk
    kb0 = k0 // tk

    def kern(*refs):
        if add is None:
            a_ref, b_ref, o_ref, acc = refs
        else:
            a_ref, b_ref, d_ref, o_ref, acc = refs
        kk = pl.program_id(2)

        def finish(prod):
            o_ref[...] = (prod if add is None else prod + d_ref[...]).astype(out_dtype)

        if nk == 1:
            finish(_dot_nt(a_ref[...].astype(BF16), b_ref[...].astype(BF16)))
            return

        @pl.when(kk == 0)
        def _():
            acc[...] = jnp.zeros_like(acc)

        acc[...] += _dot_nt(a_ref[...].astype(BF16), b_ref[...].astype(BF16))

        @pl.when(kk == nk - 1)
        def _():
            finish(acc[...])

    in_specs = [pl.BlockSpec((tm, tk), lambda i, j, kk: (i, kk)),
                pl.BlockSpec((tn, tk), lambda i, j, kk: (j, kb0 + kk))]
    args = [a, b]
    if add is not None:
        in_specs.append(pl.BlockSpec((tm, tn), lambda i, j, kk: (i, j)))
        args.append(add)
    out, got = _pcall(
        kern, grid=(m // tm, nn // tn, nk), in_specs=in_specs,
        out_specs=[pl.BlockSpec((tm, tn), lambda i, j, kk: (i, j))],
        out_shape=[jax.ShapeDtypeStruct((m, nn), out_dtype)],
        scratch=[pltpu.VMEM((tm, tn) if nk > 1 else (8, 128), F32)],
        sem=("parallel", "parallel", "arbitrary"), name=name, args=args, exch=exch)
    return out[0] if exch is None else (out[0], got)


def _mm_tn(a, b, *, out_dtype=BF16, tm=1664, tk=None, tn=None, name="mm_tn", exch=None):
    m, k = a.shape
    n = b.shape[1]
    tm = _row_tile(m, tm)
    tk = k if tk is None else tk
    if tn is None:
        tn = 1024 if n % 1024 == 0 else 512
    assert k % tk == 0 and n % tn == 0
    nm = m // tm

    def kern(a_ref, b_ref, o_ref, acc):
        mm = pl.program_id(2)

        @pl.when(mm == 0)
        def _():
            acc[...] = jnp.zeros_like(acc)

        acc[...] += _dot_tn(a_ref[...].astype(BF16), b_ref[...].astype(BF16))

        @pl.when(mm == nm - 1)
        def _():
            o_ref[...] = acc[...].astype(out_dtype)

    out, got = _pcall(
        kern, grid=(k // tk, n // tn, nm),
        in_specs=[pl.BlockSpec((tm, tk), lambda i, j, mm: (mm, i)),
                  pl.BlockSpec((tm, tn), lambda i, j, mm: (mm, j))],
        out_specs=[pl.BlockSpec((tk, tn), lambda i, j, mm: (i, j))],
        out_shape=[jax.ShapeDtypeStruct((k, n), out_dtype)],
        scratch=[pltpu.VMEM((tk, tn), F32)],
        sem=("parallel", "parallel", "arbitrary"), name=name, args=(a, b), exch=exch)
    return out[0] if exch is None else (out[0], got)


def _rmsnorm_fwd(h, g, tr=640):
    lp = h.shape[0]
    tr = _row_tile(lp, tr)

    def kern(h_ref, g_ref, o_ref):
        x = h_ref[...]
        r = lax.rsqrt(jnp.mean(x * x, axis=-1, keepdims=True) + EPS)
        o_ref[...] = (x * r * g_ref[...]).astype(BF16)

    return pl.pallas_call(
        kern, grid=(lp // tr,),
        in_specs=[pl.BlockSpec((tr, D), lambda i: (i, 0)), pl.BlockSpec((1, D), lambda i: (0, 0))],
        out_specs=pl.BlockSpec((tr, D), lambda i: (i, 0)),
        out_shape=jax.ShapeDtypeStruct((lp, D), BF16),
        compiler_params=_cparams("parallel"), name="rmsnorm_fwd")(h, g)


def _rmsnorm_bwd(h, g, dxn, dres, tr=640):
    lp = h.shape[0]
    tr = _row_tile(lp, tr)

    def kern(h_ref, g_ref, dxn_ref, dres_ref, dh_ref, dg_ref):
        i = pl.program_id(0)
        x = h_ref[...]
        r = lax.rsqrt(jnp.mean(x * x, axis=-1, keepdims=True) + EPS)
        xhat = x * r
        dy = jnp.where(_valid_rows(i * tr, tr), dxn_ref[...], 0.0)

        @pl.when(i == 0)
        def _():
            dg_ref[...] = jnp.zeros_like(dg_ref)

        dg_ref[...] += jnp.sum(dy * xhat, axis=0, keepdims=True)
        dxh = dy * g_ref[...]
        dh_ref[...] = dres_ref[...] + r * (dxh - xhat * jnp.mean(dxh * xhat, axis=-1, keepdims=True))

    row = pl.BlockSpec((tr, D), lambda i: (i, 0))
    vec = pl.BlockSpec((1, D), lambda i: (0, 0))
    return pl.pallas_call(
        kern, grid=(lp // tr,), in_specs=[row, vec, row, row], out_specs=[row, vec],
        out_shape=[jax.ShapeDtypeStruct((lp, D), F32), jax.ShapeDtypeStruct((1, D), F32)],
        compiler_params=_cparams("arbitrary"), name="rmsnorm_bwd")(h, g, dxn, dres)


def _shift_down(xe, k):
    return xe if k == 0 else pltpu.roll(xe, k, 0)


def _shift_up(xe, k):
    return xe if k == 0 else pltpu.roll(xe, xe.shape[0] - k, 0)


def _conv_ext(xe, w_ref):
    return w_ref[2:3, :] * xe + w_ref[1:2, :] * _shift_down(xe, 1) + w_ref[0:1, :] * _shift_down(xe, 2)


def _halo_specs(tr, width, col_of, nrows, rows_first):
    r8 = tr // 8
    last8 = nrows // 8 - 1
    if rows_first:
        prev = pl.BlockSpec((8, width), lambda i, j: (jnp.maximum(i * r8 - 1, 0), col_of(j)))
        nxt = pl.BlockSpec((8, width), lambda i, j: (jnp.minimum((i + 1) * r8, last8), col_of(j)))
    else:
        prev = pl.BlockSpec((8, width), lambda j, i: (jnp.maximum(i * r8 - 1, 0), col_of(j)))
        nxt = pl.BlockSpec((8, width), lambda j, i: (jnp.minimum((i + 1) * r8, last8), col_of(j)))
    return prev, nxt


def _convb_fwd(proj_r, conv_w, tr=640):
    lp = proj_r.shape[0]
    tr = _row_tile(lp, tr)
    tc = CONV_TC
    gw = 3 * tc

    def kern(g_ref, gp_ref, w_ref, o_ref):
        i = pl.program_id(0)
        g = g_ref[...]
        p = g[:, tc:2 * tc] * g[:, 2 * tc:]
        gp = gp_ref[...]
        pp = jnp.where(i > 0, gp[:, tc:2 * tc] * gp[:, 2 * tc:], 0.0)
        y = _conv_ext(jnp.concatenate([pp, p], axis=0), w_ref)[8:]
        o_ref[...] = (g[:, :tc] * y).astype(BF16)

    prev, _ = _halo_specs(tr, gw, lambda j: R_CONV_BLK0 + j, lp, True)
    return pl.pallas_call(
        kern, grid=(lp // tr, CONV_CH // tc),
        in_specs=[pl.BlockSpec((tr, gw), lambda i, j: (i, R_CONV_BLK0 + j)), prev,
                  pl.BlockSpec((3, tc), lambda i, j: (0, j))],
        out_specs=pl.BlockSpec((tr, tc), lambda i, j: (i, j)),
        out_shape=jax.ShapeDtypeStruct((lp, CONV_CH), BF16),
        compiler_params=_cparams("parallel", "parallel"), name="convb_fwd")(proj_r, proj_r, conv_w)


def _convb_bwd(proj_r, dzb, conv_w, dproj, tr=640):
    lp = proj_r.shape[0]
    tr = _row_tile(lp, tr)
    nr = lp // tr
    tc = CONV_TC
    gw = 3 * tc

    def kern(g_ref, gp_ref, gn_ref, dz_ref, dzn_ref, w_ref, dp_any, dg_ref, dw_ref):
        del dp_any
        i = pl.program_id(1)
        g = g_ref[...]
        b, c, hh = g[:, :tc], g[:, tc:2 * tc], g[:, 2 * tc:]
        p = c * hh
        gp = gp_ref[...]
        pp = jnp.where(i > 0, gp[:, tc:2 * tc] * gp[:, 2 * tc:], 0.0)
        pe = jnp.concatenate([pp, p], axis=0)
        s1 = _shift_down(pe, 1)[8:]
        s2 = _shift_down(pe, 2)[8:]
        y = w_ref[2:3, :] * p + w_ref[1:2, :] * s1 + w_ref[0:1, :] * s2
        dz = dz_ref[...]
        dy = dz * b
        dyn = jnp.where(i < nr - 1, dzn_ref[...] * gn_ref[...][:, :tc], 0.0)
        dye = jnp.concatenate([dy, dyn], axis=0)
        dp = (w_ref[2:3, :] * dy + w_ref[1:2, :] * _shift_up(dye, 1)[:tr]
              + w_ref[0:1, :] * _shift_up(dye, 2)[:tr])
        valid = _valid_rows(i * tr, tr)
        dg_ref[...] = jnp.where(valid, jnp.concatenate([dz * y, dp * hh, dp * c], axis=1), 0.0).astype(BF16)

        @pl.when(i == 0)
        def _():
            dw_ref[...] = jnp.zeros_like(dw_ref)

        dw_ref[0:1, :] += jnp.sum(dy * s2, axis=0, keepdims=True)
        dw_ref[1:2, :] += jnp.sum(dy * s1, axis=0, keepdims=True)
        dw_ref[2:3, :] += jnp.sum(dy * p, axis=0, keepdims=True)

    gprev, gnext = _halo_specs(tr, gw, lambda j: R_CONV_BLK0 + j, lp, False)
    _, dznext = _halo_specs(tr, tc, lambda j: j, lp, False)
    return pl.pallas_call(
        kern, grid=(CONV_CH // tc, nr),
        in_specs=[pl.BlockSpec((tr, gw), lambda j, i: (i, R_CONV_BLK0 + j)), gprev, gnext,
                  pl.BlockSpec((tr, tc), lambda j, i: (i, j)), dznext,
                  pl.BlockSpec((3, tc), lambda j, i: (0, j)),
                  pl.BlockSpec(memory_space=pl.ANY)],
        out_specs=[pl.BlockSpec((tr, gw), lambda j, i: (i, F_CONV_BLK0 + j)),
                   pl.BlockSpec((3, tc), lambda j, i: (0, j))],
        out_shape=[jax.ShapeDtypeStruct(dproj.shape, BF16), jax.ShapeDtypeStruct((3, CONV_CH), F32)],
        input_output_aliases={6: 0},
        compiler_params=_cparams("parallel", "arbitrary"), name="convb_bwd",
    )(proj_r, proj_r, proj_r, dzb, dzb, conv_w, dproj)


MLP_TC = 256


def _mlp_gate_fwd(z, w, tr=640):
    lp = z.shape[0]
    tr = _row_tile(lp, tr)
    tc = 512
    nc = D_FF // tc

    def kern(zg_ref, zgp_ref, zu_ref, zup_ref, wg_ref, wu_ref, o_ref):
        i = pl.program_id(0)
        zge = jnp.concatenate([jnp.where(i > 0, zgp_ref[...], 0.0), zg_ref[...]], axis=0)
        zue = jnp.concatenate([jnp.where(i > 0, zup_ref[...], 0.0), zu_ref[...]], axis=0)
        ug = _conv_ext(zge, wg_ref)[8:]
        uu = _conv_ext(zue, wu_ref)[8:]
        o_ref[...] = (ug * _sigmoid(ug) * uu).astype(BF16)

    gprev, _ = _halo_specs(tr, tc, lambda j: j, lp, True)
    uprev, _ = _halo_specs(tr, tc, lambda j: nc + j, lp, True)
    return pl.pallas_call(
        kern, grid=(lp // tr, nc),
        in_specs=[pl.BlockSpec((tr, tc), lambda i, j: (i, j)), gprev,
                  pl.BlockSpec((tr, tc), lambda i, j: (i, nc + j)), uprev,
                  pl.BlockSpec((3, tc), lambda i, j: (0, j)),
                  pl.BlockSpec((3, tc), lambda i, j: (0, nc + j))],
        out_specs=pl.BlockSpec((tr, tc), lambda i, j: (i, j)),
        out_shape=jax.ShapeDtypeStruct((lp, D_FF), BF16),
        compiler_params=_cparams("parallel", "parallel"), name="mlp_gate_fwd")(z, z, z, z, w, w)


def _mlp_gate_bwd(z, da, w, tr=640, exch=None):
    lp = z.shape[0]
    tr = _row_tile(lp, tr)
    nr = lp // tr
    tc = MLP_TC
    nc = D_FF // tc

    def kern(zg_ref, zgp_ref, zgn_ref, zu_ref, zup_ref, zun_ref, da_ref, dan_ref, wg_ref, wu_ref,
             dzg_ref, dzu_ref, dwg_ref, dwu_ref):
        i = pl.program_id(1)
        first, last = i == 0, i == nr - 1

        def ext(m_ref, p_ref, n_ref):
            return jnp.concatenate([jnp.where(first, 0.0, p_ref[...]), m_ref[...],
                                    jnp.where(last, 0.0, n_ref[...])], axis=0)

        zge, zue = ext(zg_ref, zgp_ref, zgn_ref), ext(zu_ref, zup_ref, zun_ref)
        ug = _conv_ext(zge, wg_ref)[8:]
        uu = _conv_ext(zue, wu_ref)[8:]
        dae = jnp.concatenate([da_ref[...], jnp.where(last, 0.0, dan_ref[...])], axis=0)
        sg = _sigmoid(ug)
        dug = dae * uu * (sg * (1.0 + ug * (1.0 - sg)))
        duu = dae * (ug * sg)
        valid = _valid_rows(i * tr, tr)

        @pl.when(first)
        def _():
            dwg_ref[...] = jnp.zeros_like(dwg_ref)
            dwu_ref[...] = jnp.zeros_like(dwu_ref)

        for du, ze, w_ref, dz_ref, dw_ref in ((dug, zge, wg_ref, dzg_ref, dwg_ref),
                                              (duu, zue, wu_ref, dzu_ref, dwu_ref)):
            dz = (w_ref[2:3, :] * du + w_ref[1:2, :] * _shift_up(du, 1) + w_ref[0:1, :] * _shift_up(du, 2))[:tr]
            dz_ref[...] = jnp.where(valid, dz, 0.0).astype(BF16)
            dum = du[:tr]
            for kk in range(3):
                dw_ref[kk:kk + 1, :] += jnp.sum(dum * _shift_down(ze, 2 - kk)[8:8 + tr], axis=0, keepdims=True)

    gprev, gnext = _halo_specs(tr, tc, lambda j: j, lp, False)
    uprev, unext = _halo_specs(tr, tc, lambda j: nc + j, lp, False)
    main = pl.BlockSpec((tr, tc), lambda j, i: (i, j))
    wspec = pl.BlockSpec((3, tc), lambda j, i: (0, j))
    out, got = _pcall(
        kern, grid=(nc, nr),
        in_specs=[main, gprev, gnext, pl.BlockSpec((tr, tc), lambda j, i: (i, nc + j)), uprev, unext,
                  main, gnext, wspec, pl.BlockSpec((3, tc), lambda j, i: (0, nc + j))],
        out_specs=[main, main, wspec, wspec],
        out_shape=[jax.ShapeDtypeStruct((lp, D_FF), BF16), jax.ShapeDtypeStruct((lp, D_FF), BF16),
                   jax.ShapeDtypeStruct((3, D_FF), F32), jax.ShapeDtypeStruct((3, D_FF), F32)],
        scratch=[], sem=("parallel", "arbitrary"), name="mlp_gate_bwd",
        args=(z, z, z, z, z, z, da, da, w, w), exch=exch)
    return (*out, got)


def _tri(n, lower):
    r = lax.broadcasted_iota(jnp.int32, (n, n), 0)
    c = lax.broadcasted_iota(jnp.int32, (n, n), 1)
    return jnp.where((c <= r) if lower else (c >= r), 1.0, 0.0).astype(F32)


def _dot_exact(a, b):
    return jnp.dot(a, b, preferred_element_type=F32, precision=lax.Precision.HIGHEST)


def _fox_gate_fwd(proj_r, bf128):
    lp = proj_r.shape[0]
    nb = lp // BLK

    def kern(s_ref, b_ref, c_ref):
        tri = _tri(BLK, True)

        def body(i, carry):
            rows = pl.ds(pl.multiple_of(i * BLK, BLK), BLK)
            lf = jnp.where(_valid_rows(i * BLK, BLK), _log_sigmoid(s_ref[rows, :] + b_ref[...]), 0.0)
            cs = _dot_exact(tri, lf) + carry
            c_ref[rows, :] = cs
            return cs[BLK - 1:BLK, :]

        lax.fori_loop(0, nb, body, jnp.zeros((1, BLK), F32))

    return pl.pallas_call(
        kern, grid=(1,),
        in_specs=[pl.BlockSpec((lp, BLK), lambda i: (0, R_SMALL_BLK128)), pl.BlockSpec((1, BLK), lambda i: (0, 0))],
        out_specs=pl.BlockSpec((lp, BLK), lambda i: (0, 0)),
        out_shape=jax.ShapeDtypeStruct((lp, BLK), F32),
        compiler_params=_cparams("arbitrary"), name="fox_gate_fwd")(proj_r, bf128)


def _fox_gate_bwd(proj_r, dc, bf128):
    lp = proj_r.shape[0]
    nb = lp // BLK

    def kern(s_ref, dc_ref, b_ref, dfa_ref, dbf_ref):
        tri = _tri(BLK, False)

        dbf_ref[...] = jnp.zeros_like(dbf_ref)

        def body(ii, run):
            i = nb - 1 - ii
            rows = pl.ds(pl.multiple_of(i * BLK, BLK), BLK)
            dcb = dc_ref[rows, :]
            suf = _dot_exact(tri, dcb) + run
            dfa = jnp.where(_valid_rows(i * BLK, BLK), suf * _sigmoid(-(s_ref[rows, :] + b_ref[...])), 0.0)
            dfa_ref[rows, :] = dfa
            dbf_ref[...] += jnp.sum(dfa, axis=0, keepdims=True)
            return run + jnp.sum(dcb, axis=0, keepdims=True)

        lax.fori_loop(0, nb, body, jnp.zeros((1, BLK), F32))

    return pl.pallas_call(
        kern, grid=(1,),
        in_specs=[pl.BlockSpec((lp, BLK), lambda i: (0, R_SMALL_BLK128)), pl.BlockSpec((lp, BLK), lambda i: (0, 0)),
                  pl.BlockSpec((1, BLK), lambda i: (0, 0))],
        out_specs=[pl.BlockSpec((lp, BLK), lambda i: (0, 0)), pl.BlockSpec((1, BLK), lambda i: (0, 0))],
        out_shape=[jax.ShapeDtypeStruct((lp, BLK), F32), jax.ShapeDtypeStruct((1, BLK), F32)],
        compiler_params=_cparams("arbitrary"), name="fox_gate_bwd")(proj_r, dc, bf128)


LOG2E = 1.4426950408889634
KEY_PAD_BIAS = 1e30


def _attn_logits2(q, k, ck, diag):
    t = _dot_nt(q, k) * (LOG2E * FOX_DH ** -0.5) - ck * LOG2E
    if diag:
        r = lax.broadcasted_iota(jnp.int32, t.shape, 0)
        c = lax.broadcasted_iota(jnp.int32, t.shape, 1)
        t = jnp.where(c <= r, t, NEG)
    return t


ATTN_HEADS = 2


def _head_cols(a):
    return (slice(a * FOX_DH, (a + 1) * FOX_DH), slice(2 * a * FOX_DH, (2 * a + 1) * FOX_DH),
            slice((2 * a + 1) * FOX_DH, (2 * a + 2) * FOX_DH))


def _on_blocks(i, j, step):
    pl.when(j < i)(functools.partial(step, False))
    pl.when(j == i)(functools.partial(step, True))


def _attn_fwd(proj_a, cq, ck, tq=640, exch=None):
    lp = proj_a.shape[0]
    tq = _row_tile(lp, tq)
    tk = tq
    nq = lp // tq

    def kern(q_ref, kv_ref, cq_ref, ck_ref, o_ref, lse_ref, m_sc, l_sc, acc):
        i, j = pl.program_id(1), pl.program_id(2)

        @pl.when(j == 0)
        def _():
            m_sc[...] = jnp.full_like(m_sc, -jnp.inf)
            l_sc[...] = jnp.zeros_like(l_sc)
            acc[...] = jnp.zeros_like(acc)

        def step(diag):
            heads = range(ATTN_HEADS)
            cols = [_head_cols(a) for a in heads]
            m_old = [m_sc[a] for a in heads]
            l_old = [l_sc[a] for a in heads]
            acc_old = [acc[:, cols[a][0]] for a in heads]
            cq2 = [cq_ref[a] * LOG2E for a in heads]
            t = [_attn_logits2(q_ref[:, cols[a][0]], kv_ref[:, cols[a][1]], ck_ref[a], diag) for a in heads]
            m_new = [jnp.maximum(m_old[a], jnp.max(t[a], axis=-1, keepdims=True) + cq2[a]) for a in heads]
            p = [jnp.exp2(t[a] + (cq2[a] - m_new[a])) for a in heads]
            alpha = [jnp.exp2(m_old[a] - m_new[a]) for a in heads]
            l_new = [alpha[a] * l_old[a] + jnp.sum(p[a], axis=-1, keepdims=True) for a in heads]
            acc_new = [alpha[a] * acc_old[a] + _dot(p[a].astype(BF16), kv_ref[:, cols[a][2]]) for a in heads]
            for a in heads:
                m_sc[a] = m_new[a]
                l_sc[a] = l_new[a]
                acc[:, cols[a][0]] = acc_new[a]

        _on_blocks(i, j, step)

        @pl.when(j == nq - 1)
        def _():
            valid = _valid_rows(i * tq, tq)
            for a in range(ATTN_HEADS):
                hq, _, _ = _head_cols(a)
                o_ref[:, hq] = jnp.where(valid, acc[:, hq] / l_sc[a], 0.0).astype(BF16)
                lse_ref[a] = m_sc[a] + jnp.log(l_sc[a]) * LOG2E

    hp = ATTN_HEADS
    out, got = _pcall(
        kern, grid=(FOX_H // hp, nq, nq),
        in_specs=[pl.BlockSpec((tq, hp * FOX_DH), lambda h, i, j: (i, h)),
                  pl.BlockSpec((tk, 2 * hp * FOX_DH), lambda h, i, j: (jnp.minimum(j, i), KV0 // (2 * hp * FOX_DH) + h)),
                  pl.BlockSpec((hp, tq, 1), lambda h, i, j: (h, i, 0)),
                  pl.BlockSpec((hp, 1, tk), lambda h, i, j: (h, 0, jnp.minimum(j, i)))],
        out_specs=[pl.BlockSpec((tq, hp * FOX_DH), lambda h, i, j: (i, h)),
                   pl.BlockSpec((hp, tq, 1), lambda h, i, j: (h, i, 0))],
        out_shape=[jax.ShapeDtypeStruct((lp, FOX_W), BF16), jax.ShapeDtypeStruct((FOX_H, lp, 1), F32)],
        scratch=[pltpu.VMEM((hp, tq, 1), F32), pltpu.VMEM((hp, tq, 1), F32), pltpu.VMEM((tq, hp * FOX_DH), F32)],
        sem=("parallel", "parallel", "arbitrary"), name="attn_fwd", args=(proj_a, proj_a, cq, ck), exch=exch)
    return out[0], out[1], got


def _attn_bwd(proj_a, do, o, lse, cq, ck, dproj, tq=640, exch=None):
    lp = proj_a.shape[0]
    tq = _row_tile(lp, tq)
    tk = tq
    nq = lp // tq
    hp = ATTN_HEADS
    wq = hp * FOX_DH

    def kern(q_ref, kv_ref, do_ref, o_ref, lse_ref, cq_ref, ck_ref, dp_any, dproj_ref, dcq_ref, dck_ref,
             dk_acc, dv_acc, dq_acc, dcq_acc, dq_stage, dkv_stage, sems):
        del dp_any
        h, j, i = pl.program_id(0), pl.program_id(1), pl.program_id(2)
        rows = pl.ds(pl.multiple_of(i * tq, tq), tq)

        @pl.when(i == 0)
        def _():
            dck_ref[...] = jnp.zeros_like(dck_ref)
            dk_acc[...] = jnp.zeros_like(dk_acc)
            dv_acc[...] = jnp.zeros_like(dv_acc)

        @pl.when(jnp.logical_and(i == 0, j == 0))
        def _():
            dq_acc[...] = jnp.zeros_like(dq_acc)
            dcq_acc[...] = jnp.zeros_like(dcq_acc)

        def step(diag):
            heads = range(hp)
            cols = [_head_cols(a) for a in heads]
            dck_old = [dck_ref[a] for a in heads]
            dcq_old = [dcq_acc[a, rows, :] for a in heads]
            dk_old = [dk_acc[:, cols[a][0]] for a in heads]
            dv_old = [dv_acc[:, cols[a][0]] for a in heads]
            dq_old = [dq_acc[rows, cols[a][0]] for a in heads]
            shift = [cq_ref[a] * LOG2E - lse_ref[a] for a in heads]
            do_h = [do_ref[:, cols[a][0]] for a in heads]
            delta = [jnp.sum(do_h[a].astype(F32) * o_ref[:, cols[a][0]].astype(F32), axis=-1, keepdims=True)
                     for a in heads]
            t = [_attn_logits2(q_ref[:, cols[a][0]], kv_ref[:, cols[a][1]], ck_ref[a], diag) for a in heads]
            dp = [_dot_nt(do_h[a], kv_ref[:, cols[a][2]]) for a in heads]
            p = [jnp.exp2(t[a] + shift[a]) for a in heads]
            ds = [p[a] * (dp[a] - delta[a]) for a in heads]
            dsb = [ds[a].astype(BF16) for a in heads]
            dv_new = [dv_old[a] + _dot_tn(p[a].astype(BF16), do_h[a]) for a in heads]
            dck_new = [dck_old[a] - jnp.sum(ds[a], axis=0, keepdims=True) for a in heads]
            dcq_new = [dcq_old[a] + jnp.sum(ds[a], axis=-1, keepdims=True) for a in heads]
            dk_new = [dk_old[a] + _dot_tn(dsb[a], q_ref[:, cols[a][0]]) for a in heads]
            dq_new = [dq_old[a] + _dot(dsb[a], kv_ref[:, cols[a][1]]) for a in heads]
            for a in heads:
                dck_ref[a] = dck_new[a]
                dcq_acc[a, rows, :] = dcq_new[a]
                dk_acc[:, cols[a][0]] = dk_new[a]
                dv_acc[:, cols[a][0]] = dv_new[a]
                dq_acc[rows, cols[a][0]] = dq_new[a]

        _on_blocks(i, j, step)

        @pl.when(i == nq - 1)
        def _():
            parts = []
            for a in range(hp):
                hq, _, _ = _head_cols(a)
                parts += [dk_acc[:, hq] * (FOX_DH ** -0.5), dv_acc[:, hq]]
            dkv_stage[...] = jnp.concatenate(parts, axis=1).astype(BF16)
            out = pltpu.make_async_copy(
                dkv_stage, dproj_ref.at[pl.ds(pl.multiple_of(j * tk, tk), tk),
                                        pl.ds(pl.multiple_of(KV0 + h * 2 * wq, 2 * wq), 2 * wq)], sems.at[0])
            out.start()
            out.wait()

        @pl.when(jnp.logical_and(i == nq - 1, j == nq - 1))
        def _():
            dq_stage[...] = (dq_acc[...] * (FOX_DH ** -0.5)).astype(BF16)
            out = pltpu.make_async_copy(dq_stage, dproj_ref.at[:, pl.ds(pl.multiple_of(h * wq, wq), wq)], sems.at[1])
            rowsums = pltpu.make_async_copy(dcq_acc, dcq_ref.at[pl.ds(h * hp, hp)], sems.at[2])
            out.start()
            rowsums.start()
            out.wait()
            rowsums.wait()

    qspec = pl.BlockSpec((tq, wq), lambda h, j, i: (jnp.maximum(i, j), h))
    col = pl.BlockSpec((hp, tq, 1), lambda h, j, i: (h, jnp.maximum(i, j), 0))
    kvspec = pl.BlockSpec((tk, 2 * wq), lambda h, j, i: (j, KV0 // (2 * wq) + h))
    rowspec = pl.BlockSpec((hp, 1, tk), lambda h, j, i: (h, 0, j))
    out, got = _pcall(
        kern, grid=(FOX_H // hp, nq, nq),
        in_specs=[qspec, kvspec, qspec, qspec, col, col, rowspec, pl.BlockSpec(memory_space=pl.ANY)],
        out_specs=[pl.BlockSpec(memory_space=pl.ANY), pl.BlockSpec(memory_space=pl.ANY), rowspec],
        out_shape=[jax.ShapeDtypeStruct(dproj.shape, BF16), jax.ShapeDtypeStruct((FOX_H, lp, 1), F32),
                   jax.ShapeDtypeStruct((FOX_H, 1, lp), F32)],
        scratch=[pltpu.VMEM((tk, wq), F32), pltpu.VMEM((tk, wq), F32), pltpu.VMEM((lp, wq), F32),
                 pltpu.VMEM((hp, lp, 1), F32), pltpu.VMEM((lp, wq), BF16), pltpu.VMEM((tk, 2 * wq), BF16),
                 pltpu.SemaphoreType.DMA((3,))],
        aliases={7: 0}, sem=("arbitrary", "arbitrary", "arbitrary"), name="attn_bwd",
        args=(proj_a, proj_a, do, o, lse, cq, ck, dproj), exch=exch)
    return out[0], out[1], out[2], got


def _gla_gate_fwd(proj_r, wg2p, bg, tr=640):
    lp = proj_r.shape[0]
    tr = _row_tile(lp, tr)
    w = GLA_H * GLA_DK

    def kern(s_ref, w_ref, b_ref, o_ref):
        zg = _dot(s_ref[...].astype(BF16), w_ref[...]) + b_ref[...]
        o_ref[...] = jnp.where(_valid_rows(pl.program_id(0) * tr, tr), _log_sigmoid(zg) * (1.0 / GLA_TAU), 0.0)

    return pl.pallas_call(
        kern, grid=(lp // tr,),
        in_specs=[pl.BlockSpec((tr, BLK), lambda i: (i, R_SMALL_BLK128)), pl.BlockSpec((BLK, w), lambda i: (0, 0)),
                  pl.BlockSpec((1, w), lambda i: (0, 0))],
        out_specs=pl.BlockSpec((tr, w), lambda i: (i, 0)),
        out_shape=jax.ShapeDtypeStruct((lp, w), F32),
        compiler_params=_cparams("parallel"), name="gla_gate_fwd")(proj_r, wg2p, bg)


def _gla_chunk(grp, g):
    q = grp[:, :GLA_DK] * (GLA_DK ** -0.5)
    k = grp[:, GLA_DK:2 * GLA_DK]
    v = grp[:, 2 * GLA_DK:2 * GLA_DK + GLA_DV]
    r = grp[:, 2 * GLA_DK + GLA_DV:]
    b = _dot_exact(_tri(BLK, True), g)
    bl = b[BLK - 1:BLK, :]
    eb = jnp.exp(b)
    enb = jnp.exp(-b)
    ebl = jnp.exp(bl - b)
    qe, ke, kd = q * eb, k * enb, k * ebl
    causal = lax.broadcasted_iota(jnp.int32, (BLK, BLK), 1) <= lax.broadcasted_iota(jnp.int32, (BLK, BLK), 0)
    att = jnp.where(causal, _dot_nt(qe.astype(BF16), ke.astype(BF16)), 0.0)
    return q, k, v, r, bl, eb, enb, ebl, qe, ke, kd, causal, att


def _gla_fwd(proj_r, logg, gn):
    lp = proj_r.shape[0]
    nc = lp // BLK
    wv = GLA_H * GLA_DV

    def kern(grp_ref, g_ref, gn_ref, o_ref, zc_ref, st_ref, st):
        c = pl.program_id(0)

        @pl.when(c == 0)
        def _():
            st[...] = jnp.zeros_like(st)

        for h in range(GLA_H):
            kcol = slice(h * GLA_DK, (h + 1) * GLA_DK)
            vcol = slice(h * GLA_DV, (h + 1) * GLA_DV)
            q, k, v, r, bl, eb, enb, ebl, qe, ke, kd, causal, att = _gla_chunk(
                grp_ref[:, h * GLA_GRP:(h + 1) * GLA_GRP], g_ref[:, kcol])
            s_t = st[h]
            st_ref[h] = s_t
            vb = v.astype(BF16)
            o = _dot(att.astype(BF16), vb) + _dot_nt(qe.astype(BF16), s_t.astype(BF16))
            st[h] = s_t * jnp.exp(bl) + _dot_tn(vb, kd.astype(BF16))
            o_ref[:, vcol] = o
            rstd = lax.rsqrt(jnp.mean(o * o, axis=-1, keepdims=True) + EPS)
            zc_ref[:, vcol] = (r * _sigmoid(r) * (o * rstd * gn_ref[:, vcol])).astype(BF16)

    vspec = pl.BlockSpec((BLK, wv), lambda c: (c, 0))
    return pl.pallas_call(
        kern, grid=(nc,),
        in_specs=[pl.BlockSpec((BLK, GLA_H * GLA_GRP), lambda c: (c, R_GLA_BLK0 // GLA_H)),
                  pl.BlockSpec((BLK, GLA_H * GLA_DK), lambda c: (c, 0)),
                  pl.BlockSpec((1, wv), lambda c: (0, 0))],
        out_specs=[vspec, vspec, pl.BlockSpec((GLA_H, None, GLA_DV, GLA_DK), lambda c: (0, c, 0, 0))],
        out_shape=[jax.ShapeDtypeStruct((lp, wv), F32), jax.ShapeDtypeStruct((lp, wv), BF16),
                   jax.ShapeDtypeStruct((GLA_H, nc, GLA_DV, GLA_DK), F32)],
        scratch_shapes=[pltpu.VMEM((GLA_H, GLA_DV, GLA_DK), F32)],
        compiler_params=_cparams("arbitrary"), name="gla_fwd")(proj_r, logg, gn)


def _gla_bwd(proj_r, logg, st_all, o_all, dzc, gn, dproj):
    lp = proj_r.shape[0]
    nc = lp // BLK

    def kern(grp_ref, g_ref, st_ref, o_ref, dzc_ref, gn_ref, dp_any, dgrp_ref, dlg_ref, dgn_ref, dst):
        del dp_any
        cc = pl.program_id(0)

        @pl.when(cc == 0)
        def _():
            dst[...] = jnp.zeros_like(dst)
            dgn_ref[...] = jnp.zeros_like(dgn_ref)

        for h in range(GLA_H):
            kcol = slice(h * GLA_DK, (h + 1) * GLA_DK)
            vcol = slice(h * GLA_DV, (h + 1) * GLA_DV)
            q, k, v, r, bl, eb, enb, ebl, qe, ke, kd, causal, att = _gla_chunk(
                grp_ref[:, h * GLA_GRP:(h + 1) * GLA_GRP], g_ref[:, kcol])
            s_t = st_ref[h]
            d_st = dst[h]
            o = o_ref[:, vcol]
            dzc_v = dzc_ref[:, vcol]
            gnv = gn_ref[:, vcol]
            rstd = lax.rsqrt(jnp.mean(o * o, axis=-1, keepdims=True) + EPS)
            xhat = o * rstd
            sr = _sigmoid(r)
            dr = dzc_v * (xhat * gnv) * (sr * (1.0 + r * (1.0 - sr)))
            docn = dzc_v * (r * sr)
            dgn_ref[:, vcol] += jnp.sum(docn * xhat, axis=0, keepdims=True)
            dxh = docn * gnv
            do = rstd * (dxh - xhat * jnp.mean(dxh * xhat, axis=-1, keepdims=True))
            dob, vb = do.astype(BF16), v.astype(BF16)
            qeb, keb, kdb = qe.astype(BF16), ke.astype(BF16), kd.astype(BF16)
            datt = jnp.where(causal, _dot_nt(dob, vb), 0.0).astype(BF16)
            dv = _dot_tn(att.astype(BF16), dob) + _dot_nt(kdb, d_st.astype(BF16))
            dqe = _dot(datt, keb) + _dot(dob, s_t.astype(BF16))
            dke = _dot_tn(datt, qeb)
            dkd = _dot(vb, d_st.astype(BF16))
            dq = dqe * eb * (GLA_DK ** -0.5)
            dk = dke * enb + dkd * ebl
            kd_dkd = dkd * kd
            db = dqe * qe - dke * ke - kd_dkd
            db_last = (jnp.sum(kd_dkd, axis=0, keepdims=True)
                       + jnp.exp(bl) * jnp.sum(s_t * d_st, axis=0, keepdims=True))
            dlg_ref[:, kcol] = _dot_exact(_tri(BLK, False), db) + db_last
            dst[h] = d_st * jnp.exp(bl) + _dot_tn(dob, qeb)
            dgrp_ref[:, h * GLA_GRP:(h + 1) * GLA_GRP] = jnp.concatenate([dq, dk, dv, dr], axis=1).astype(BF16)

    rev = lambda c: nc - 1 - c
    wv = GLA_H * GLA_DV
    vspec = pl.BlockSpec((BLK, wv), lambda c: (rev(c), 0))
    kspec = pl.BlockSpec((BLK, GLA_H * GLA_DK), lambda c: (rev(c), 0))
    return pl.pallas_call(
        kern, grid=(nc,),
        in_specs=[pl.BlockSpec((BLK, GLA_H * GLA_GRP), lambda c: (rev(c), R_GLA_BLK0 // GLA_H)), kspec,
                  pl.BlockSpec((GLA_H, None, GLA_DV, GLA_DK), lambda c: (0, rev(c), 0, 0)),
                  vspec, vspec, pl.BlockSpec((1, wv), lambda c: (0, 0)),
                  pl.BlockSpec(memory_space=pl.ANY)],
        out_specs=[pl.BlockSpec((BLK, GLA_H * GLA_GRP), lambda c: (rev(c), F_GLA_BLK0 // GLA_H)), kspec,
                   pl.BlockSpec((1, wv), lambda c: (0, 0))],
        out_shape=[jax.ShapeDtypeStruct(dproj.shape, BF16), jax.ShapeDtypeStruct((lp, GLA_H * GLA_DK), F32),
                   jax.ShapeDtypeStruct((1, wv), F32)],
        scratch_shapes=[pltpu.VMEM((GLA_H, GLA_DV, GLA_DK), F32)],
        input_output_aliases={6: 0},
        compiler_params=_cparams("arbitrary"), name="gla_bwd",
    )(proj_r, logg, st_all, o_all, dzc, gn, dproj)


def _small_bwd(proj_r, dlogg, wg2p, wg2pt, bg, dfa, dproj, tr=640):
    lp = proj_r.shape[0]
    tr = _row_tile(lp, tr)
    w = GLA_H * GLA_DK

    def kern(s_ref, dlg_ref, w_ref, wt_ref, b_ref, dfa_ref, dp_any, ds_ref, dbg_ref, dw_ref):
        del dp_any
        i = pl.program_id(0)
        sb = s_ref[...].astype(BF16)
        zg = _dot(sb, w_ref[...]) + b_ref[...]
        dzg = jnp.where(_valid_rows(i * tr, tr), dlg_ref[...] * (1.0 / GLA_TAU) * _sigmoid(-zg), 0.0)

        @pl.when(i == 0)
        def _():
            dbg_ref[...] = jnp.zeros_like(dbg_ref)
            dw_ref[...] = jnp.zeros_like(dw_ref)

        dbg_ref[...] += jnp.sum(dzg, axis=0, keepdims=True)
        dzb = dzg.astype(BF16)
        dw_ref[...] += _dot_tn(sb, dzb)
        dsm = _dot(dzb, wt_ref[...]) + dfa_ref[...]
        ds_ref[...] = jnp.concatenate([dsm, jnp.zeros((tr, SMALL_W - BLK), F32)], axis=1).astype(BF16)

    return pl.pallas_call(
        kern, grid=(lp // tr,),
        in_specs=[pl.BlockSpec((tr, BLK), lambda i: (i, R_SMALL_BLK128)), pl.BlockSpec((tr, w), lambda i: (i, 0)),
                  pl.BlockSpec((BLK, w), lambda i: (0, 0)), pl.BlockSpec((w, BLK), lambda i: (0, 0)),
                  pl.BlockSpec((1, w), lambda i: (0, 0)), pl.BlockSpec((tr, BLK), lambda i: (i, 0)),
                  pl.BlockSpec(memory_space=pl.ANY)],
        out_specs=[pl.BlockSpec((tr, SMALL_W), lambda i: (i, F_SMALL_BLK0)), pl.BlockSpec((1, w), lambda i: (0, 0)),
                   pl.BlockSpec((BLK, w), lambda i: (0, 0))],
        out_shape=[jax.ShapeDtypeStruct(dproj.shape, BF16), jax.ShapeDtypeStruct((1, w), F32),
                   jax.ShapeDtypeStruct((BLK, w), F32)],
        input_output_aliases={6: 0},
        compiler_params=_cparams("arbitrary"), name="small_bwd",
    )(proj_r, dlogg, wg2p, wg2pt, bg, dfa, dproj)


def _merge_fwd(proj_r, gate_b3, ya, yb, yc, tr=640):
    lp = proj_r.shape[0]
    tr = _row_tile(lp, tr)
    tn = GATE_TN

    def kern(g_ref, b_ref, ya_ref, yb_ref, yc_ref, o_ref):
        g = g_ref[...]
        mix = (_sigmoid(g[:, :tn] + b_ref[0:1, :]) * ya_ref[...]
               + _sigmoid(g[:, tn:2 * tn] + b_ref[1:2, :]) * yb_ref[...]
               + _sigmoid(g[:, 2 * tn:] + b_ref[2:3, :]) * yc_ref[...])
        o_ref[...] = mix.astype(BF16)

    y = pl.BlockSpec((tr, tn), lambda i, j: (i, j))
    return pl.pallas_call(
        kern, grid=(lp // tr, D // tn),
        in_specs=[pl.BlockSpec((tr, 3 * tn), lambda i, j: (i, R_GATE_BLK0 + j)),
                  pl.BlockSpec((3, tn), lambda i, j: (0, j)), y, y, y],
        out_specs=y, out_shape=jax.ShapeDtypeStruct((lp, D), BF16),
        compiler_params=_cparams("parallel", "parallel"), name="merge_fwd")(proj_r, gate_b3, ya, yb, yc)


def _merge_bwd(proj_r, gate_b3, ya, yb, yc, dmix, tr=640):
    lp = proj_r.shape[0]
    tr = _row_tile(lp, tr)
    tn = GATE_TN

    def kern(g_ref, b_ref, ya_ref, yb_ref, yc_ref, dm_ref, dya_ref, dyb_ref, dyc_ref, dg_ref, db_ref):
        i = pl.program_id(1)
        g = g_ref[...]
        dm = dm_ref[...]

        @pl.when(i == 0)
        def _():
            db_ref[...] = jnp.zeros_like(db_ref)

        dgs = []
        for n, (y_ref, dy_ref) in enumerate(((ya_ref, dya_ref), (yb_ref, dyb_ref), (yc_ref, dyc_ref))):
            s = _sigmoid(g[:, n * tn:(n + 1) * tn] + b_ref[n:n + 1, :])
            dy_ref[...] = (dm * s).astype(BF16)
            dgn = dm * y_ref[...] * (s * (1.0 - s))
            db_ref[n:n + 1, :] += jnp.sum(dgn, axis=0, keepdims=True)
            dgs.append(dgn)
        dg_ref[...] = jnp.concatenate(dgs, axis=1).astype(BF16)

    y = pl.BlockSpec((tr, tn), lambda j, i: (i, j))
    bspec = pl.BlockSpec((3, tn), lambda j, i: (0, j))
    return pl.pallas_call(
        kern, grid=(D // tn, lp // tr),
        in_specs=[pl.BlockSpec((tr, 3 * tn), lambda j, i: (i, R_GATE_BLK0 + j)), bspec, y, y, y, y],
        out_specs=[y, y, y, pl.BlockSpec((tr, 3 * tn), lambda j, i: (i, F_GATE_BLK0 + j)), bspec],
        out_shape=[jax.ShapeDtypeStruct((lp, D), BF16)] * 3
        + [jax.ShapeDtypeStruct((lp, NP), BF16), jax.ShapeDtypeStruct((3, D), F32)],
        compiler_params=_cparams("parallel", "arbitrary"), name="merge_bwd")(proj_r, gate_b3, ya, yb, yc, dmix)


def _final_loss(h, gf, tgt):
    lp = h.shape[0]
    nb = lp // BLK

    def kern(h_ref, g_ref, t_ref, dh_ref, dg_ref, ls_ref):
        i = pl.program_id(0)

        @pl.when(i == 0)
        def _():
            dh_ref[...] = jnp.zeros_like(dh_ref)
            dg_ref[...] = jnp.zeros_like(dg_ref)
            ls_ref[...] = jnp.zeros_like(ls_ref)

        @pl.when(i > 0)
        def _():
            x = h_ref[...]
            r = lax.rsqrt(jnp.mean(x * x, axis=-1, keepdims=True) + EPS)
            xhat = x * r
            err = xhat * g_ref[...] - t_ref[...]
            ls_ref[...] += jnp.sum(jnp.sum(err * err, axis=0, keepdims=True), axis=1, keepdims=True)
            dy = err * (1.0 / D)
            dg_ref[...] += jnp.sum(dy * xhat, axis=0, keepdims=True)
            dxh = dy * g_ref[...]
            dh_ref[...] = r * (dxh - xhat * jnp.mean(dxh * xhat, axis=-1, keepdims=True))

    row = pl.BlockSpec((BLK, D), lambda i: (i, 0))
    vec = pl.BlockSpec((1, D), lambda i: (0, 0))
    return pl.pallas_call(
        kern, grid=(nb,),
        in_specs=[row, vec, pl.BlockSpec((BLK, D), lambda i: (jnp.maximum(i - 1, 0), 0))],
        out_specs=[row, vec, pl.BlockSpec((1, 1), lambda i: (0, 0))],
        out_shape=[jax.ShapeDtypeStruct((lp, D), F32), jax.ShapeDtypeStruct((1, D), F32),
                   jax.ShapeDtypeStruct((1, 1), F32)],
        compiler_params=_cparams("arbitrary"), name="final_loss")(h, gf, tgt)


def _gate_cols(c):
    ct = c[:, :FOX_H].T
    ck = jnp.where(jnp.arange(ct.shape[1]) < PAD, KEY_PAD_BIAS, ct)
    return ct[:, :, None], ck[:, None, :]


def _run(hosts, name, ctx, fn):
    if hosts and name in hosts:
        make, done = hosts[name]
        res = fn(make(ctx))
        done(res[-1])
    else:
        res = fn(None)
    return res[:-1]


def _mm_nn_x(a, b, exch, **kw):
    out = _mm_nn(a, b, exch=exch, **kw)
    return out if exch is not None else (out, None)


def _layer_fwd(h, w, hosts=None):
    xn = _rmsnorm_fwd(h, w["norm1_g"])
    proj_a = _mm_nn(xn, w["w_in"], n0=0, n=REST0, out_dtype=BF16, name="proj_a")
    proj_r, = _run(hosts, "proj_r", w, lambda e: _mm_nn_x(xn, w["w_in"], e, n0=REST0, n=NREST, name="proj_r"))
    cq, ck = _gate_cols(_fox_gate_fwd(proj_r, w["bf128"]))
    oa, lse = _run(hosts, "attn_fwd", w, lambda e: _attn_fwd(proj_a, cq, ck, exch=e))
    zb = _convb_fwd(proj_r, w["conv_w"])
    logg = _gla_gate_fwd(proj_r, w["wg2p"], w["gla_b_g"])
    o_gla, zc, st_all = _gla_fwd(proj_r, logg, w["gla_norm_g"])
    ya = _mm_nn(oa, w["w_a_o"], name="branch_a")
    yb = _mm_nn(zb, w["w_b_o"], name="branch_b")
    yc = _mm_nn(zc, w["w_c_o"], name="branch_c")
    mix = _merge_fwd(proj_r, w["gate_b3"], ya, yb, yc)
    h1 = _mm_nn(mix, w["w_o"], res=h, name="out_proj")
    xn2 = _rmsnorm_fwd(h1, w["norm2_g"])
    z, = _run(hosts, "up_proj", w, lambda e: _mm_nn_x(xn2, w["w_up"], e, name="up_proj"))
    a = _mlp_gate_fwd(z, w["mlp_conv_w"])
    h2, = _run(hosts, "down_proj", w,
               lambda e: _mm_nn_x(a, w["w_down"], e, res=h1, tk=D_FF // 4, name="down_proj"))
    saved = dict(h=h, xn=xn, proj_a=proj_a, proj_r=proj_r, cq=cq, ck=ck, oa=oa, lse=lse, zb=zb, logg=logg,
                 o_gla=o_gla, zc=zc, st_all=st_all, ya=ya, yb=yb, yc=yc, mix=mix, h1=h1, xn2=xn2, z=z, a=a)
    return h2, saved


def _layer_bwd(dh2, w, s, hosts=None):
    g = {}
    da = _mm_nt(dh2, w["w_down"], tn=D_FF // 4, name="d_down_in")
    g["w_down"] = _mm_tn(s["a"], dh2, tk=D_FF // 4, name="d_w_down")
    dzg, dzu, dmw_g, dmw_u = _run(hosts, "mlp_gate_bwd", g, lambda e: _mlp_gate_bwd(
        s["z"], da, w["mlp_conv_w"], exch=e))
    g["mlp_conv_w"] = jnp.concatenate([dmw_g, dmw_u], axis=1)
    dxn2 = _mm_nt(dzg, w["w_up"], k0=0, kw=D_FF, tk=D_FF // 4, name="d_up_in_g")
    dxn2 = _mm_nt(dzu, w["w_up"], k0=D_FF, kw=D_FF, tk=D_FF // 4, add=dxn2, name="d_up_in_u")
    g["w_up"] = jnp.concatenate([_mm_tn(s["xn2"], dzg, name="d_w_up_g"), _mm_tn(s["xn2"], dzu, name="d_w_up_u")], axis=1)
    dh1, g["norm2_g"] = _rmsnorm_bwd(s["h1"], w["norm2_g"], dxn2, dh2)
    dmix = _mm_nt(dh1, w["w_o"], name="d_out_proj_in")
    g["w_o"] = _mm_tn(s["mix"], dh1, name="d_w_o")
    dya, dyb, dyc, dproj, g["gate_b3"] = _merge_bwd(s["proj_r"], w["gate_b3"], s["ya"], s["yb"], s["yc"], dmix)
    doa = _mm_nt(dya, w["w_a_o"], out_dtype=BF16, name="d_branch_a_in")
    g["w_a_o"] = _mm_tn(s["oa"], dya, name="d_w_a_o")
    dzb = _mm_nt(dyb, w["w_b_o"], name="d_branch_b_in")
    g["w_b_o"] = _mm_tn(s["zb"], dyb, name="d_w_b_o")
    dzc = _mm_nt(dyc, w["w_c_o"], name="d_branch_c_in")
    g["w_c_o"] = _mm_tn(s["zc"], dyc, name="d_w_c_o")
    dproj, dlogg, g["gla_norm_g"] = _gla_bwd(s["proj_r"], s["logg"], s["st_all"], s["o_gla"], dzc, w["gla_norm_g"], dproj)
    dproj, g["conv_w"] = _convb_bwd(s["proj_r"], dzb, w["conv_w"], dproj)
    dproj, dcq, dck = _run(hosts, "attn_bwd", g, lambda e: _attn_bwd(
        s["proj_a"], doa, s["oa"], s["lse"], s["cq"], s["ck"], dproj, exch=e))
    dc = jnp.pad((dcq[:, :, 0] + dck[:, 0, :]).T, ((0, 0), (0, BLK - FOX_H)))
    dfa, g["bf128"] = _fox_gate_bwd(s["proj_r"], dc, w["bf128"])
    dproj, g["gla_b_g"], g["wg2p"] = _small_bwd(s["proj_r"], dlogg, w["wg2p"], w["wg2p"].T, w["gla_b_g"], dfa, dproj)
    def pair(out, e):
        return out if e is not None else (out, None)

    g["w_in"], = _run(hosts, "d_w_in", g, lambda e: pair(_mm_tn(s["xn"], dproj, name="d_w_in", exch=e), e))
    dxn, = _run(hosts, "d_in_proj_in", g, lambda e: pair(_mm_nt(dproj, w["w_in"], name="d_in_proj_in", exch=e), e))
    dh0, g["norm1_g"] = _rmsnorm_bwd(s["h"], w["norm1_g"], dxn, dh1)
    return dh0, g


def _local_step(x, tgt, meta, final_g, layers, hosts_fwd=None, hosts_bwd=None):
    h = jnp.concatenate([jnp.zeros((PAD, D), F32), meta, x], axis=0)
    saved = []
    for l, w in enumerate(layers):
        h, s = _layer_fwd(h, w, hosts_fwd[l] if hosts_fwd else None)
        saved.append(s)
    dh, dgf, sq = _final_loss(h, final_g, tgt)
    grads = [None] * len(layers)
    for l in reversed(range(len(layers))):
        dh, grads[l] = _layer_bwd(dh, layers[l], saved[l], hosts_bwd[l](grads) if hosts_bwd else None)
    return sq[0, 0], dh[BLK:], dh[PAD:BLK], dgf, grads


def _w_in_to_kernel(w_nat):
    parts = [w_nat[:, s:s + n] for s, n in _segments()]
    parts.append(jnp.zeros((w_nat.shape[0], SMALL_W - 8 - GLA_R), w_nat.dtype))
    return jnp.concatenate(parts, axis=1)


def _w_in_from_kernel(w_k):
    pieces, off = [], 0
    for s, n in _segments():
        pieces.append((s, w_k[:, off:off + n]))
        off += n
    return jnp.concatenate([p for _, p in sorted(pieces, key=lambda t: t[0])], axis=1)


def _w_in_slots_to_kernel(got):
    per = got.shape[2]
    parts = []
    for s, n in _segments():
        while n > 0:
            d, lo = divmod(s, per)
            take = min(n, per - lo)
            parts.append(got[d, :, lo:lo + take])
            s, n = s + take, n - take
    parts.append(jnp.zeros((got.shape[1], SMALL_W - 8 - GLA_R), got.dtype))
    return jnp.concatenate(parts, axis=1)


def _w_in_kernel_to_slots(w_k):
    per = N_IN // N_DEV
    pieces, off = [], 0
    for s, n in _segments():
        pieces.append((s, n, off))
        off += n
    slots = []
    for d in range(N_DEV):
        lo, hi = d * per, (d + 1) * per
        parts = [w_k[:, off + max(s, lo) - s:off + min(s + n, hi) - s]
                 for s, n, off in sorted(pieces) if max(s, lo) < min(s + n, hi)]
        slots.append(jnp.concatenate(parts, axis=1))
    return jnp.stack(slots)


def _pad_rows_at(a, row0, nrows):
    return jnp.pad(a, ((row0, nrows - row0 - a.shape[0]), (0, 0)))


def _big_to_kernel(name, full):
    return _w_in_to_kernel(full) if name == "w_in" else full


def _layer_weights(big, conv_w, gla_w_g2, mlp_conv_w, norm1_g, fox_b_f, gate_b, gla_b_g, gla_norm_g, norm2_g):
    w = {n: _big_to_kernel(n, a) for n, a in big.items()}
    w.update(
        conv_w=conv_w, mlp_conv_w=mlp_conv_w,
        wg2p=_pad_rows_at(gla_w_g2, 8, BLK).astype(BF16),
        norm1_g=norm1_g[None], norm2_g=norm2_g[None], gla_b_g=gla_b_g[None], gla_norm_g=gla_norm_g[None],
        bf128=jnp.pad(fox_b_f, (0, BLK - FOX_H))[None], gate_b3=gate_b.reshape(3, D))
    return w


def _layer_grads_natural(g):
    return dict(
        w_in=_w_in_from_kernel(g["w_in"]), w_a_o=g["w_a_o"], w_b_o=g["w_b_o"], w_c_o=g["w_c_o"], w_o=g["w_o"],
        w_up=g["w_up"], w_down=g["w_down"], conv_w=g["conv_w"], mlp_conv_w=g["mlp_conv_w"],
        gla_w_g2=g["wg2p"][8:8 + GLA_R], norm1_g=g["norm1_g"][0], norm2_g=g["norm2_g"][0],
        gla_b_g=g["gla_b_g"][0], gla_norm_g=g["gla_norm_g"][0], fox_b_f=g["bf128"][0, :FOX_H],
        gate_b=g["gate_b3"].reshape(3 * D))


def _adamw(recv, w, m, v, layer, prev=None, name="adamw"):
    n_slot, r, c = recv.shape
    lyr = w.shape[0]
    tr = r
    for t in range(16, r, 16):
        if r % t == 0 and t * c <= ADAMW_BLOCK_ELEMS:
            tr = t
    if r * c <= ADAMW_BLOCK_ELEMS:
        tr = r
    bc1, bc2 = 1.0 - ADAM_B1 ** ADAM_STEP, 1.0 - ADAM_B2 ** ADAM_STEP

    def kern(*refs):
        r_ref, w_ref, m_ref, v_ref = refs[:4]
        g_out, d_out, m_out, v_out = refs[-4:]
        g = r_ref[0].astype(F32)
        for sidx in range(1, n_slot):
            g = g + r_ref[sidx].astype(F32)
        m_new = ADAM_B1 * m_ref[...] + (1.0 - ADAM_B1) * g
        v_new = ADAM_B2 * v_ref[...] + (1.0 - ADAM_B2) * (g * g)
        g_out[...] = g
        m_out[...] = m_new
        v_out[...] = v_new
        d_out[...] = -ADAM_LR * ((m_new / bc1) / (jnp.sqrt(v_new / bc2) + ADAM_EPS) + ADAM_WD * w_ref[...])

    lspec = pl.BlockSpec((None, tr, c), lambda i: (layer, i, 0))
    in_specs = [pl.BlockSpec((n_slot, tr, c), lambda i: (0, i, 0)), lspec, lspec, lspec]
    args = [recv, w, m, v]
    aliases = {}
    if prev is not None:
        in_specs += [pl.BlockSpec(memory_space=pl.ANY)] * 4
        args += list(prev)
        aliases = {4: 0, 5: 1, 6: 2, 7: 3}
    return pl.pallas_call(
        kern, grid=(r // tr,), in_specs=in_specs, out_specs=[lspec] * 4,
        out_shape=[jax.ShapeDtypeStruct((lyr, r, c), F32)] * 4, input_output_aliases=aliases,
        compiler_params=_cparams("parallel"), name=name)(*args)


_BIG = ("w_in", "w_a_o", "w_b_o", "w_c_o", "w_o", "w_up", "w_down")
_COL_SHARDED = ("w_in", "w_a_o", "w_b_o", "w_c_o", "w_up", "conv_w", "gla_w_g2", "mlp_conv_w")
_REPL = ("norm1_g", "fox_b_f", "gate_b", "gla_b_g", "gla_norm_g", "norm2_g")


def _cols_from_slots(a):
    return jnp.transpose(a, (1, 0, 2)).reshape(a.shape[1], N_DEV * a.shape[2])


def _cols_to_slots(a):
    r, c8 = a.shape
    return jnp.transpose(a.reshape(r, N_DEV, c8 // N_DEV), (1, 0, 2))


def _rows_to_slots(a):
    return a.reshape(N_DEV, a.shape[0] // N_DEV, a.shape[1])


def kernel(x, meta_tokens, norm1_g, w_in, fox_b_f, gate_b, conv_w, gla_w_g2, gla_b_g, gla_norm_g, w_a_o, w_b_o, w_c_o, w_o, norm2_g, w_up, mlp_conv_w, w_down, final_norm_g, loss_target, m_meta_tokens, m_norm1_g, m_w_in, m_fox_b_f, m_gate_b, m_conv_w, m_gla_w_g2, m_gla_b_g, m_gla_norm_g, m_w_a_o, m_w_b_o, m_w_c_o, m_w_o, m_norm2_g, m_w_up, m_mlp_conv_w, m_w_down, m_final_norm_g, v_meta_tokens, v_norm1_g, v_w_in, v_fox_b_f, v_gate_b, v_conv_w, v_gla_w_g2, v_gla_b_g, v_gla_norm_g, v_w_a_o, v_w_b_o, v_w_c_o, v_w_o, v_norm2_g, v_w_up, v_mlp_conv_w, v_w_down, v_final_norm_g):
    names = ("meta_tokens", "norm1_g", "w_in", "fox_b_f", "gate_b", "conv_w", "gla_w_g2", "gla_b_g", "gla_norm_g",
             "w_a_o", "w_b_o", "w_c_o", "w_o", "norm2_g", "w_up", "mlp_conv_w", "w_down", "final_norm_g")
    wts = dict(zip(names, (meta_tokens, norm1_g, w_in, fox_b_f, gate_b, conv_w, gla_w_g2, gla_b_g, gla_norm_g,
                           w_a_o, w_b_o, w_c_o, w_o, norm2_g, w_up, mlp_conv_w, w_down, final_norm_g)))
    mom = dict(zip(names, (m_meta_tokens, m_norm1_g, m_w_in, m_fox_b_f, m_gate_b, m_conv_w, m_gla_w_g2, m_gla_b_g,
                           m_gla_norm_g, m_w_a_o, m_w_b_o, m_w_c_o, m_w_o, m_norm2_g, m_w_up, m_mlp_conv_w, m_w_down,
                           m_final_norm_g)))
    var = dict(zip(names, (v_meta_tokens, v_norm1_g, v_w_in, v_fox_b_f, v_gate_b, v_conv_w, v_gla_w_g2, v_gla_b_g,
                           v_gla_norm_g, v_w_a_o, v_w_b_o, v_w_c_o, v_w_o, v_norm2_g, v_w_up, v_mlp_conv_w, v_w_down,
                           v_final_norm_g)))

    small = _exchange([conv_w, gla_w_g2, mlp_conv_w, meta_tokens], [True] * 4, "gather_small")
    conv_full = jnp.transpose(small[0], (1, 2, 0, 3)).reshape(DEPTH, 3, CONV_CH)
    g2_full = jnp.transpose(small[1], (1, 2, 0, 3)).reshape(DEPTH, GLA_R, GLA_H * GLA_DK)
    mconv_full = jnp.transpose(small[2], (1, 2, 0, 3)).reshape(DEPTH, 3, 2 * D_FF)
    meta_full = _cols_from_slots(small[3])
    layers = [_layer_weights({}, conv_full[l], g2_full[l], mconv_full[l], norm1_g[l], fox_b_f[l], gate_b[l],
                             gla_b_g[l], gla_norm_g[l], norm2_g[l]) for l in range(DEPTH)]

    wide = ("w_a_o", "w_b_o", "w_c_o", "w_up")

    def gather(l, which):
        def make(_):
            return [wts[n][l].astype(BF16) for n in which], [("wide" if n in wide else True) for n in which]

        def done(got):
            for n, a in zip(which, got):
                if n == "w_in":
                    layers[l][n] = _w_in_slots_to_kernel(a)
                else:
                    layers[l][n] = a if n in wide else a.reshape(-1, a.shape[-1])

        return make, done

    recv_big = [dict() for _ in range(DEPTH)]

    def scatter(l, which, grads_of):
        def make(ctx):
            g = grads_of(ctx)
            send = [_w_in_kernel_to_slots(g[n]) if n == "w_in" else g[n] if n in wide else _rows_to_slots(g[n])
                    for n in which]
            return send, [("cols" if n in wide else False) for n in which]

        def done(got):
            recv_big[l].update(zip(which, got))

        return make, done

    mixers = ("w_o", "w_a_o", "w_b_o", "w_c_o")
    early = ("w_down", "w_up") + mixers
    make, done = gather(0, ("w_in",))
    done(_exchange(*make(None), "gather_w_in"))
    hosts_fwd = [
        {"proj_r": gather(0, mixers + ("w_down",)), "attn_fwd": gather(0, ("w_up",)),
         "up_proj": gather(1, ("w_in",)), "down_proj": gather(1, mixers)},
        {"attn_fwd": gather(1, ("w_up", "w_down"))}]
    hosts_bwd = [
        lambda grads: {"mlp_gate_bwd": scatter(1, ("w_in",), lambda _: grads[1]),
                       "attn_bwd": scatter(1, ("w_up", "w_down") + mixers, lambda _: grads[1]),
                       "d_w_in": scatter(0, early, lambda g: g),
                       "d_in_proj_in": scatter(0, ("w_in",), lambda g: g)},
        lambda grads: None]

    sq, grad_x, dmeta, dgf, grads_k = _local_step(x[0], loss_target[0], meta_full, final_norm_g[None], layers,
                                                  hosts_fwd, hosts_bwd)
    loss = lax.psum(sq * (0.5 / D), ("x", "y", "c"))
    grads = [_layer_grads_natural(g) for g in grads_k]

    out_g, out_d, out_m, out_v = {}, {}, {}, {}

    def update(name, recv, layer, lyr_shape, prev):
        w3, m3, v3 = (t[name].reshape(lyr_shape) for t in (wts, mom, var))
        return _adamw(recv.reshape((recv.shape[0],) + lyr_shape[1:]), w3, m3, v3, layer, prev, name="adamw_" + name)

    def store(name, res):
        shape = wts[name].shape
        out_g[name], out_d[name], out_m[name], out_v[name] = (t.reshape(shape) for t in res)

    for n in _BIG:
        res = None
        for l in range(DEPTH):
            res = update(n, recv_big[l][n], l, wts[n].shape, res)
        store(n, res)

    def stack_layers(name):
        return jnp.stack([grads[l][name] for l in range(DEPTH)])

    s_conv = jnp.transpose(stack_layers("conv_w").reshape(DEPTH, 3, N_DEV, -1), (2, 0, 1, 3))
    s_g2 = jnp.transpose(stack_layers("gla_w_g2").reshape(DEPTH, GLA_R, N_DEV, -1), (2, 0, 1, 3))
    s_mconv = jnp.transpose(stack_layers("mlp_conv_w").reshape(DEPTH, 3, N_DEV, -1), (2, 0, 1, 3))
    s_meta = _cols_to_slots(dmeta)
    repl = [stack_layers(n) for n in _REPL] + [dgf]
    pack = jnp.concatenate([jnp.pad(a.reshape(-1), (0, (-a.size) % 1024)) for a in repl]).reshape(-1, BLK)
    r_conv, r_g2, r_mconv, r_meta, r_pack = _exchange(
        [s_conv, s_g2, s_mconv, s_meta, pack], [False, False, False, False, True], "scatter_small")
    store("conv_w", update("conv_w", r_conv, 0, (1, DEPTH * 3, CONV_CH // N_DEV), None))
    store("gla_w_g2", update("gla_w_g2", r_g2, 0, (1, DEPTH * GLA_R, GLA_H * GLA_DK // N_DEV), None))
    store("mlp_conv_w", update("mlp_conv_w", r_mconv, 0, (1, DEPTH * 3, 2 * D_FF // N_DEV), None))
    store("meta_tokens", update("meta_tokens", r_meta, 0, (1, N_META, D // N_DEV), None))
    off = 0
    for n, a in zip(_REPL + ("final_norm_g",), repl):
        rows = (a.size + 1023) // 1024 * 8
        part = r_pack[:, off:off + rows].reshape(N_DEV, -1)[:, :a.size]
        off += rows
        shape2 = (1, 1, a.size) if a.size % BLK else (1, a.size // BLK, BLK)
        store(n, update(n, part, 0, shape2, None))

    order = lambda d: [d[n] for n in names]
    return (loss, grad_x[None], *order(out_g), *order(out_d), *order(out_m), *order(out_v))
```

```python
import functools

import jax
import jax.numpy as jnp
from jax import lax
from jax.experimental import pallas as pl
from jax.experimental.pallas import tpu as pltpu

F32 = jnp.float32
BF16 = jnp.bfloat16

D = 2048
DEPTH = 2
N_META = 16
BLK = 128
PAD = BLK - N_META
EPS = 1e-6
NEG = -1e30

FOX_H, FOX_DH = 8, 128
FOX_W = FOX_H * FOX_DH
CONV_CH = 1024
GLA_H, GLA_DK, GLA_DV, GLA_R, GLA_TAU = 4, 128, 256, 16, 16.0
D_FF = 5632
N_IN = 15384
N_DEV = 8

ADAM_LR, ADAM_B1, ADAM_B2, ADAM_EPS, ADAM_WD, ADAM_STEP = 0.001, 0.9, 0.999, 1e-08, 0.01, 10

CONV_TC = 512
GATE_TN = 512
KV0 = 1024
REST0 = 3072
GLA_GRP = 768
SMALL_W = 1024
NP = 16384
NREST = NP - REST0
R_CONV_BLK0 = 0
R_GLA_BLK0 = (6144 - REST0) // GLA_GRP
R_GATE_BLK0 = (9216 - REST0) // (3 * GATE_TN)
R_SMALL_BLK128 = (15360 - REST0) // 128
F_CONV_BLK0 = 3072 // (3 * CONV_TC)
F_GLA_BLK0 = 6144 // GLA_GRP
F_GATE_BLK0 = 9216 // (3 * GATE_TN)
F_SMALL_BLK0 = 15360 // SMALL_W

VMEM_LIMIT = 56 * 1024 * 1024
ADAMW_BLOCK_ELEMS = 128 * 1024


def _segments():
    seg = [(0, 1024)]
    for h in range(FOX_H):
        seg += [(1024 + 128 * h, 128), (2048 + 128 * h, 128)]
    for j in range(CONV_CH // CONV_TC):
        seg += [(3080 + CONV_TC * j, CONV_TC), (4104 + CONV_TC * j, CONV_TC), (5128 + CONV_TC * j, CONV_TC)]
    for h in range(GLA_H):
        seg += [(6152 + 128 * h, 128), (6664 + 128 * h, 128), (7176 + 256 * h, 256), (8200 + 256 * h, 256)]
    for j in range(D // GATE_TN):
        seg += [(9240 + GATE_TN * j, GATE_TN), (11288 + GATE_TN * j, GATE_TN), (13336 + GATE_TN * j, GATE_TN)]
    seg += [(3072, 8), (9224, 16)]
    return seg


def _cparams(*sem):
    return pltpu.CompilerParams(dimension_semantics=sem, vmem_limit_bytes=VMEM_LIMIT)


def _row_tile(n, target):
    best = BLK
    t = BLK
    while t <= min(n, target):
        if n % t == 0:
            best = t
        t += BLK
    return best


def _sigmoid(x):
    return 1.0 / (1.0 + jnp.exp(-x))


def _log_sigmoid(x):
    return jnp.minimum(x, 0.0) - jnp.log(1.0 + jnp.exp(-jnp.abs(x)))


def _valid_rows(row0, n):
    return (row0 + lax.broadcasted_iota(jnp.int32, (n, 1), 0)) >= PAD


def _dot(a, b):
    return jnp.dot(a, b, preferred_element_type=F32)


def _dot_nt(a, b):
    return lax.dot_general(a, b, (((1,), (1,)), ((), ())), preferred_element_type=F32)


def _dot_tn(a, b):
    return lax.dot_general(a, b, (((0,), (0,)), ((), ())), preferred_element_type=F32)


def _exchange_copies(ins, outs, bcast, send_sems, recv_sems, local_sems):
    x, y, c = lax.axis_index("x"), lax.axis_index("y"), lax.axis_index("c")
    me = 4 * x + 2 * y + c

    def src_for(n, dev):
        if bcast[n] is True or bcast[n] == "wide":
            return ins[n]
        if bcast[n] == "cols":
            w = ins[n].shape[1] // N_DEV
            return ins[n].at[:, pl.ds(pl.multiple_of(dev * w, BLK), w)]
        return ins[n].at[dev]

    def dst_of(n, dev):
        if bcast[n] == "wide":
            w = ins[n].shape[1]
            return outs[n].at[:, pl.ds(pl.multiple_of(dev * w, BLK), w)]
        return outs[n].at[dev]

    local, sends, recvs = [], [], []
    for n in range(len(ins)):
        local.append(pltpu.make_async_copy(src_for(n, me), dst_of(n, me), local_sems.at[n]))
    for k in range(1, N_DEV):
        px = 1 - x if (k >> 2) & 1 else x
        py = 1 - y if (k >> 1) & 1 else y
        pc = 1 - c if k & 1 else c
        peer = 4 * px + 2 * py + pc
        for n in range(len(ins)):
            def copy(dst_dev, n=n, k=k, to=(px, py, pc), peer=peer):
                return pltpu.make_async_remote_copy(
                    src_ref=src_for(n, peer), dst_ref=dst_of(n, dst_dev), send_sem=send_sems.at[n, k - 1],
                    recv_sem=recv_sems.at[n, k - 1], device_id=to, device_id_type=pl.DeviceIdType.MESH)

            sends.append(copy(me))
            recvs.append(copy(peer))
    return local, sends, recvs


def _exchange_start(copies):
    local, sends, _ = copies
    for cp in local + sends:
        cp.start()


def _exchange_wait(copies):
    local, sends, recvs = copies
    for cp in recvs:
        cp.wait_recv()
    for cp in sends:
        cp.wait_send()
    for cp in local:
        cp.wait()


def _exchange_shapes(arrays, bcast):
    def shape(a, b):
        if b is True:
            return (N_DEV,) + a.shape
        if b == "wide":
            return (a.shape[0], N_DEV * a.shape[1])
        if b == "cols":
            return (N_DEV, a.shape[0], a.shape[1] // N_DEV)
        return a.shape

    return [jax.ShapeDtypeStruct(shape(a, b), a.dtype) for a, b in zip(arrays, bcast)]


def _exchange_sems(n_arr):
    return [pltpu.SemaphoreType.DMA((n_arr, N_DEV - 1)), pltpu.SemaphoreType.DMA((n_arr, N_DEV - 1)),
            pltpu.SemaphoreType.DMA((n_arr,))]


def _exchange(arrays, bcast, name):
    n_arr = len(arrays)

    def body(*refs):
        copies = _exchange_copies(refs[:n_arr], refs[n_arr:2 * n_arr], bcast, *refs[2 * n_arr:])
        _exchange_start(copies)
        _exchange_wait(copies)

    hbm = pl.BlockSpec(memory_space=pltpu.HBM)
    return pl.pallas_call(
        body, out_shape=_exchange_shapes(arrays, bcast), in_specs=[hbm] * n_arr, out_specs=[hbm] * n_arr,
        scratch_shapes=_exchange_sems(n_arr),
        compiler_params=pltpu.CompilerParams(has_side_effects=True), name=name)(*arrays)


def _pcall(kern, *, grid, in_specs, out_specs, out_shape, scratch, sem, name, args, aliases=None, exch=None):
    params = pltpu.CompilerParams(dimension_semantics=sem, vmem_limit_bytes=VMEM_LIMIT,
                                  has_side_effects=exch is not None)
    kw = dict(grid=grid, compiler_params=params, name=name, input_output_aliases=aliases or {})
    if exch is None:
        out = pl.pallas_call(kern, in_specs=in_specs, out_specs=out_specs, out_shape=out_shape,
                             scratch_shapes=scratch, **kw)(*args)
        return out, None
    arrays, bcast = exch
    n_x, n_in, n_out, n_sc = len(arrays), len(in_specs), len(out_specs), len(scratch)

    def hosted(*refs):
        ins, x_in = refs[:n_in], refs[n_in:n_in + n_x]
        outs, x_out = refs[n_in + n_x:n_in + n_x + n_out], refs[n_in + n_x + n_out:n_in + 2 * n_x + n_out]
        sc, sems = refs[n_in + 2 * n_x + n_out:n_in + 2 * n_x + n_out + n_sc], refs[n_in + 2 * n_x + n_out + n_sc:]
        ids = [pl.program_id(d) for d in range(len(grid))]
        first = functools.reduce(jnp.logical_and, [i == 0 for i in ids])
        last = functools.reduce(jnp.logical_and, [i == g - 1 for i, g in zip(ids, grid)])

        @pl.when(first)
        def _():
            _exchange_start(_exchange_copies(x_in, x_out, bcast, *sems))

        kern(*ins, *outs, *sc)

        @pl.when(last)
        def _():
            _exchange_wait(_exchange_copies(x_in, x_out, bcast, *sems))

    hbm = pl.BlockSpec(memory_space=pltpu.HBM)
    out = pl.pallas_call(
        hosted, in_specs=list(in_specs) + [hbm] * n_x, out_specs=list(out_specs) + [hbm] * n_x,
        out_shape=list(out_shape) + _exchange_shapes(arrays, bcast),
        scratch_shapes=list(scratch) + _exchange_sems(n_x), **kw)(*args, *arrays)
    return out[:n_out], out[n_out:]


def _mm_nn(a, b, *, n0=0, n=None, out_dtype=F32, res=None, tm=1664, tn=512, tk=None, name="mm_nn", exch=None):
    m, k = a.shape
    n = b.shape[1] - n0 if n is None else n
    tm = _row_tile(m, tm)
    tk = k if tk is None else tk
    nk = k // tk
    assert k % tk == 0 and n % tn == 0 and n0 % tn == 0
    nb0 = n0 // tn

    def kern(*refs):
        if res is None:
            a_ref, b_ref, o_ref, acc = refs
        else:
            a_ref, b_ref, r_ref, o_ref, acc = refs
        kk = pl.program_id(2)
        row0 = pl.program_id(0) * tm

        def finish(prod):
            if res is None:
                o_ref[...] = prod.astype(out_dtype)
            else:
                o_ref[...] = (r_ref[...] + jnp.where(_valid_rows(row0, tm), prod, 0.0)).astype(out_dtype)

        if nk == 1:
            finish(_dot(a_ref[...].astype(BF16), b_ref[...].astype(BF16)))
            return

        @pl.when(kk == 0)
        def _():
            acc[...] = jnp.zeros_like(acc)

        acc[...] += _dot(a_ref[...].astype(BF16), b_ref[...].astype(BF16))

        @pl.when(kk == nk - 1)
        def _():
            finish(acc[...])

    in_specs = [pl.BlockSpec((tm, tk), lambda i, j, kk: (i, kk)),
                pl.BlockSpec((tk, tn), lambda i, j, kk: (kk, nb0 + j))]
    args = [a, b]
    if res is not None:
        in_specs.append(pl.BlockSpec((tm, tn), lambda i, j, kk: (i, j)))
        args.append(res)
    out, got = _pcall(
        kern, grid=(m // tm, n // tn, nk), in_specs=in_specs,
        out_specs=[pl.BlockSpec((tm, tn), lambda i, j, kk: (i, j))],
        out_shape=[jax.ShapeDtypeStruct((m, n), out_dtype)],
        scratch=[pltpu.VMEM((tm, tn) if nk > 1 else (8, 128), F32)],
        sem=("parallel", "parallel", "arbitrary"), name=name, args=args, exch=exch)
    return out[0] if exch is None else (out[0], got)


def _mm_nt(a, b, *, k0=0, kw=None, out_dtype=F32, add=None, tm=640, tn=None, tk=2048, name="mm_nt", exch=None):
    m = a.shape[0]
    kw = a.shape[1] if kw is None else kw
    nn = b.shape[0]
    tm = _row_tile(m, tm)
    tn = min(nn, 2048) if tn is None else tn
    tk = min(tk, kw)
    assert kw % tk == 0 and k0 % tk == 0 and nn % tn == 0 and a.shape[1] == kw
    nk = kw // tk
    kb0 = k0 // tk

    def kern(*refs):
        if add is None:
            a_ref, b_ref, o_ref, acc = refs
        else:
            a_ref, b_ref, d_ref, o_ref, acc = refs
        kk = pl.program_id(2)

        def finish(prod):
            o_ref[...] = (prod if add is None else prod + d_ref[...]).astype(out_dtype)

        if nk == 1:
            finish(_dot_nt(a_ref[...].astype(BF16), b_ref[...].astype(BF16)))
            return

        @pl.when(kk == 0)
        def _():
            acc[...] = jnp.zeros_like(acc)

        acc[...] += _dot_nt(a_ref[...].astype(BF16), b_ref[...].astype(BF16))

        @pl.when(kk == nk - 1)
        def _():
            finish(acc[...])

    in_specs = [pl.BlockSpec((tm, tk), lambda i, j, kk: (i, kk)),
                pl.BlockSpec((tn, tk), lambda i, j, kk: (j, kb0 + kk))]
    args = [a, b]
    if add is not None:
        in_specs.append(pl.BlockSpec((tm, tn), lambda i, j, kk: (i, j)))
        args.append(add)
    out, got = _pcall(
        kern, grid=(m // tm, nn // tn, nk), in_specs=in_specs,
        out_specs=[pl.BlockSpec((tm, tn), lambda i, j, kk: (i, j))],
        out_shape=[jax.ShapeDtypeStruct((m, nn), out_dtype)],
        scratch=[pltpu.VMEM((tm, tn) if nk > 1 else (8, 128), F32)],
        sem=("parallel", "parallel", "arbitrary"), name=name, args=args, exch=exch)
    return out[0] if exch is None else (out[0], got)


def _mm_tn(a, b, *, out_dtype=BF16, tm=1664, tk=None, tn=None, name="mm_tn", exch=None):
    m, k = a.shape
    n = b.shape[1]
    tm = _row_tile(m, tm)
    tk = k if tk is None else tk
    if tn is None:
        tn = 1024 if n % 1024 == 0 else 512
    assert k % tk == 0 and n % tn == 0
    nm = m // tm

    def kern(a_ref, b_ref, o_ref, acc):
        mm = pl.program_id(2)

        @pl.when(mm == 0)
        def _():
            acc[...] = jnp.zeros_like(acc)

        acc[...] += _dot_tn(a_ref[...].astype(BF16), b_ref[...].astype(BF16))

        @pl.when(mm == nm - 1)
        def _():
            o_ref[...] = acc[...].astype(out_dtype)

    out, got = _pcall(
        kern, grid=(k // tk, n // tn, nm),
        in_specs=[pl.BlockSpec((tm, tk), lambda i, j, mm: (mm, i)),
                  pl.BlockSpec((tm, tn), lambda i, j, mm: (mm, j))],
        out_specs=[pl.BlockSpec((tk, tn), lambda i, j, mm: (i, j))],
        out_shape=[jax.ShapeDtypeStruct((k, n), out_dtype)],
        scratch=[pltpu.VMEM((tk, tn), F32)],
        sem=("parallel", "parallel", "arbitrary"), name=name, args=(a, b), exch=exch)
    return out[0] if exch is None else (out[0], got)


def _rmsnorm_fwd(h, g, tr=640):
    lp = h.shape[0]
    tr = _row_tile(lp, tr)

    def kern(h_ref, g_ref, o_ref):
        x = h_ref[...]
        r = lax.rsqrt(jnp.mean(x * x, axis=-1, keepdims=True) + EPS)
        o_ref[...] = (x * r * g_ref[...]).astype(BF16)

    return pl.pallas_call(
        kern, grid=(lp // tr,),
        in_specs=[pl.BlockSpec((tr, D), lambda i: (i, 0)), pl.BlockSpec((1, D), lambda i: (0, 0))],
        out_specs=pl.BlockSpec((tr, D), lambda i: (i, 0)),
        out_shape=jax.ShapeDtypeStruct((lp, D), BF16),
        compiler_params=_cparams("parallel"), name="rmsnorm_fwd")(h, g)


def _rmsnorm_bwd(h, g, dxn, dres, tr=640):
    lp = h.shape[0]
    tr = _row_tile(lp, tr)

    def kern(h_ref, g_ref, dxn_ref, dres_ref, dh_ref, dg_ref):
        i = pl.program_id(0)
        x = h_ref[...]
        r = lax.rsqrt(jnp.mean(x * x, axis=-1, keepdims=True) + EPS)
        xhat = x * r
        dy = jnp.where(_valid_rows(i * tr, tr), dxn_ref[...], 0.0)

        @pl.when(i == 0)
        def _():
            dg_ref[...] = jnp.zeros_like(dg_ref)

        dg_ref[...] += jnp.sum(dy * xhat, axis=0, keepdims=True)
        dxh = dy * g_ref[...]
        dh_ref[...] = dres_ref[...] + r * (dxh - xhat * jnp.mean(dxh * xhat, axis=-1, keepdims=True))

    row = pl.BlockSpec((tr, D), lambda i: (i, 0))
    vec = pl.BlockSpec((1, D), lambda i: (0, 0))
    return pl.pallas_call(
        kern, grid=(lp // tr,), in_specs=[row, vec, row, row], out_specs=[row, vec],
        out_shape=[jax.ShapeDtypeStruct((lp, D), F32), jax.ShapeDtypeStruct((1, D), F32)],
        compiler_params=_cparams("arbitrary"), name="rmsnorm_bwd")(h, g, dxn, dres)


def _shift_down(xe, k):
    return xe if k == 0 else pltpu.roll(xe, k, 0)


def _shift_up(xe, k):
    return xe if k == 0 else pltpu.roll(xe, xe.shape[0] - k, 0)


def _conv_ext(xe, w_ref):
    return w_ref[2:3, :] * xe + w_ref[1:2, :] * _shift_down(xe, 1) + w_ref[0:1, :] * _shift_down(xe, 2)


def _halo_specs(tr, width, col_of, nrows, rows_first, halo=8):
    r8 = tr // halo
    last8 = nrows // halo - 1
    if rows_first:
        prev = pl.BlockSpec((halo, width), lambda i, j: (jnp.maximum(i * r8 - 1, 0), col_of(j)))
        nxt = pl.BlockSpec((halo, width), lambda i, j: (jnp.minimum((i + 1) * r8, last8), col_of(j)))
    else:
        prev = pl.BlockSpec((halo, width), lambda j, i: (jnp.maximum(i * r8 - 1, 0), col_of(j)))
        nxt = pl.BlockSpec((halo, width), lambda j, i: (jnp.minimum((i + 1) * r8, last8), col_of(j)))
    return prev, nxt


def _convb_fwd(proj_r, conv_w, tr=640):
    lp = proj_r.shape[0]
    tr = _row_tile(lp, tr)
    tc = CONV_TC
    gw = 3 * tc

    def kern(g_ref, gp_ref, w_ref, o_ref):
        i = pl.program_id(0)
        g = g_ref[...]
        p = g[:, tc:2 * tc] * g[:, 2 * tc:]
        gp = gp_ref[...]
        pp = jnp.where(i > 0, gp[:, tc:2 * tc] * gp[:, 2 * tc:], 0.0)
        y = _conv_ext(jnp.concatenate([pp, p], axis=0), w_ref)[8:]
        o_ref[...] = (g[:, :tc] * y).astype(BF16)

    prev, _ = _halo_specs(tr, gw, lambda j: R_CONV_BLK0 + j, lp, True)
    return pl.pallas_call(
        kern, grid=(lp // tr, CONV_CH // tc),
        in_specs=[pl.BlockSpec((tr, gw), lambda i, j: (i, R_CONV_BLK0 + j)), prev,
                  pl.BlockSpec((3, tc), lambda i, j: (0, j))],
        out_specs=pl.BlockSpec((tr, tc), lambda i, j: (i, j)),
        out_shape=jax.ShapeDtypeStruct((lp, CONV_CH), BF16),
        compiler_params=_cparams("parallel", "parallel"), name="convb_fwd")(proj_r, proj_r, conv_w)


def _convb_bwd(proj_r, dzb, conv_w, dproj, tr=640):
    lp = proj_r.shape[0]
    tr = _row_tile(lp, tr)
    nr = lp // tr
    tc = CONV_TC
    gw = 3 * tc

    def kern(g_ref, gp_ref, gn_ref, dz_ref, dzn_ref, w_ref, dp_any, dg_ref, dw_ref):
        del dp_any
        i = pl.program_id(1)
        g = g_ref[...]
        b, c, hh = g[:, :tc], g[:, tc:2 * tc], g[:, 2 * tc:]
        p = c * hh
        gp = gp_ref[...]
        pp = jnp.where(i > 0, gp[:, tc:2 * tc] * gp[:, 2 * tc:], 0.0)
        pe = jnp.concatenate([pp, p], axis=0)
        s1 = _shift_down(pe, 1)[8:]
        s2 = _shift_down(pe, 2)[8:]
        y = w_ref[2:3, :] * p + w_ref[1:2, :] * s1 + w_ref[0:1, :] * s2
        dz = dz_ref[...]
        dy = dz * b
        dyn = jnp.where(i < nr - 1, dzn_ref[...] * gn_ref[...][:, :tc], 0.0)
        dye = jnp.concatenate([dy, dyn], axis=0)
        dp = (w_ref[2:3, :] * dy + w_ref[1:2, :] * _shift_up(dye, 1)[:tr]
              + w_ref[0:1, :] * _shift_up(dye, 2)[:tr])
        valid = _valid_rows(i * tr, tr)
        dg_ref[...] = jnp.where(valid, jnp.concatenate([dz * y, dp * hh, dp * c], axis=1), 0.0).astype(BF16)

        @pl.when(i == 0)
        def _():
            dw_ref[...] = jnp.zeros_like(dw_ref)

        dw_ref[0:1, :] += jnp.sum(dy * s2, axis=0, keepdims=True)
        dw_ref[1:2, :] += jnp.sum(dy * s1, axis=0, keepdims=True)
        dw_ref[2:3, :] += jnp.sum(dy * p, axis=0, keepdims=True)

    gprev, gnext = _halo_specs(tr, gw, lambda j: R_CONV_BLK0 + j, lp, False)
    _, dznext = _halo_specs(tr, tc, lambda j: j, lp, False)
    return pl.pallas_call(
        kern, grid=(CONV_CH // tc, nr),
        in_specs=[pl.BlockSpec((tr, gw), lambda j, i: (i, R_CONV_BLK0 + j)), gprev, gnext,
                  pl.BlockSpec((tr, tc), lambda j, i: (i, j)), dznext,
                  pl.BlockSpec((3, tc), lambda j, i: (0, j)),
                  pl.BlockSpec(memory_space=pl.ANY)],
        out_specs=[pl.BlockSpec((tr, gw), lambda j, i: (i, F_CONV_BLK0 + j)),
                   pl.BlockSpec((3, tc), lambda j, i: (0, j))],
        out_shape=[jax.ShapeDtypeStruct(dproj.shape, BF16), jax.ShapeDtypeStruct((3, CONV_CH), F32)],
        input_output_aliases={6: 0},
        compiler_params=_cparams("parallel", "arbitrary"), name="convb_bwd",
    )(proj_r, proj_r, proj_r, dzb, dzb, conv_w, dproj)


MLP_TC = 256
MLP_HALO = 16


def _mlp_gate_fwd(z, w, tr=640):
    lp = z.shape[0]
    tr = _row_tile(lp, tr)
    tc = 512
    nc = D_FF // tc

    def kern(zg_ref, zgp_ref, zu_ref, zup_ref, wg_ref, wu_ref, o_ref):
        i = pl.program_id(0)

        def ext(m_ref, p_ref):
            return jnp.concatenate([jnp.where(i > 0, p_ref[...].astype(F32), 0.0), m_ref[...].astype(F32)], axis=0)

        ug = _conv_ext(ext(zg_ref, zgp_ref), wg_ref)[MLP_HALO:]
        uu = _conv_ext(ext(zu_ref, zup_ref), wu_ref)[MLP_HALO:]
        o_ref[...] = (ug * _sigmoid(ug) * uu).astype(BF16)

    gprev, _ = _halo_specs(tr, tc, lambda j: j, lp, True, MLP_HALO)
    uprev, _ = _halo_specs(tr, tc, lambda j: nc + j, lp, True, MLP_HALO)
    return pl.pallas_call(
        kern, grid=(lp // tr, nc),
        in_specs=[pl.BlockSpec((tr, tc), lambda i, j: (i, j)), gprev,
                  pl.BlockSpec((tr, tc), lambda i, j: (i, nc + j)), uprev,
                  pl.BlockSpec((3, tc), lambda i, j: (0, j)),
                  pl.BlockSpec((3, tc), lambda i, j: (0, nc + j))],
        out_specs=pl.BlockSpec((tr, tc), lambda i, j: (i, j)),
        out_shape=jax.ShapeDtypeStruct((lp, D_FF), BF16),
        compiler_params=_cparams("parallel", "parallel"), name="mlp_gate_fwd")(z, z, z, z, w, w)


def _mlp_gate_bwd(z, da, w, tr=640, exch=None):
    lp = z.shape[0]
    tr = _row_tile(lp, tr)
    nr = lp // tr
    tc = MLP_TC
    nc = D_FF // tc

    def kern(zg_ref, zgp_ref, zgn_ref, zu_ref, zup_ref, zun_ref, da_ref, dan_ref, wg_ref, wu_ref,
             dzg_ref, dzu_ref, dwg_ref, dwu_ref):
        i = pl.program_id(1)
        first, last = i == 0, i == nr - 1

        hl = MLP_HALO

        def ext(m_ref, p_ref, n_ref):
            return jnp.concatenate([jnp.where(first, 0.0, p_ref[...].astype(F32)), m_ref[...].astype(F32),
                                    jnp.where(last, 0.0, n_ref[...].astype(F32))], axis=0)

        zge, zue = ext(zg_ref, zgp_ref, zgn_ref), ext(zu_ref, zup_ref, zun_ref)
        ug = _conv_ext(zge, wg_ref)[hl:]
        uu = _conv_ext(zue, wu_ref)[hl:]
        dae = jnp.concatenate([da_ref[...].astype(F32), jnp.where(last, 0.0, dan_ref[...].astype(F32))], axis=0)
        sg = _sigmoid(ug)
        dug = dae * uu * (sg * (1.0 + ug * (1.0 - sg)))
        duu = dae * (ug * sg)
        valid = _valid_rows(i * tr, tr)

        @pl.when(first)
        def _():
            dwg_ref[...] = jnp.zeros_like(dwg_ref)
            dwu_ref[...] = jnp.zeros_like(dwu_ref)

        for du, ze, w_ref, dz_ref, dw_ref in ((dug, zge, wg_ref, dzg_ref, dwg_ref),
                                              (duu, zue, wu_ref, dzu_ref, dwu_ref)):
            dz = (w_ref[2:3, :] * du + w_ref[1:2, :] * _shift_up(du, 1) + w_ref[0:1, :] * _shift_up(du, 2))[:tr]
            dz_ref[...] = jnp.where(valid, dz, 0.0).astype(BF16)
            dum = du[:tr]
            for kk in range(3):
                dw_ref[kk:kk + 1, :] += jnp.sum(dum * _shift_down(ze, 2 - kk)[hl:hl + tr], axis=0, keepdims=True)

    gprev, gnext = _halo_specs(tr, tc, lambda j: j, lp, False, MLP_HALO)
    uprev, unext = _halo_specs(tr, tc, lambda j: nc + j, lp, False, MLP_HALO)
    main = pl.BlockSpec((tr, tc), lambda j, i: (i, j))
    wspec = pl.BlockSpec((3, tc), lambda j, i: (0, j))
    out, got = _pcall(
        kern, grid=(nc, nr),
        in_specs=[main, gprev, gnext, pl.BlockSpec((tr, tc), lambda j, i: (i, nc + j)), uprev, unext,
                  main, gnext, wspec, pl.BlockSpec((3, tc), lambda j, i: (0, nc + j))],
        out_specs=[main, main, wspec, wspec],
        out_shape=[jax.ShapeDtypeStruct((lp, D_FF), BF16), jax.ShapeDtypeStruct((lp, D_FF), BF16),
                   jax.ShapeDtypeStruct((3, D_FF), F32), jax.ShapeDtypeStruct((3, D_FF), F32)],
        scratch=[], sem=("parallel", "arbitrary"), name="mlp_gate_bwd",
        args=(z, z, z, z, z, z, da, da, w, w), exch=exch)
    return (*out, got)


def _tri(n, lower):
    r = lax.broadcasted_iota(jnp.int32, (n, n), 0)
    c = lax.broadcasted_iota(jnp.int32, (n, n), 1)
    return jnp.where((c <= r) if lower else (c >= r), 1.0, 0.0).astype(F32)


def _dot_exact(a, b):
    return jnp.dot(a, b, preferred_element_type=F32, precision=lax.Precision.HIGHEST)


def _fox_gate_fwd(proj_r, bf128):
    lp = proj_r.shape[0]
    nb = lp // BLK

    def kern(s_ref, b_ref, c_ref):
        tri = _tri(BLK, True)

        def body(i, carry):
            rows = pl.ds(pl.multiple_of(i * BLK, BLK), BLK)
            lf = jnp.where(_valid_rows(i * BLK, BLK), _log_sigmoid(s_ref[rows, :] + b_ref[...]), 0.0)
            cs = _dot_exact(tri, lf) + carry
            c_ref[rows, :] = cs
            return cs[BLK - 1:BLK, :]

        lax.fori_loop(0, nb, body, jnp.zeros((1, BLK), F32))

    return pl.pallas_call(
        kern, grid=(1,),
        in_specs=[pl.BlockSpec((lp, BLK), lambda i: (0, R_SMALL_BLK128)), pl.BlockSpec((1, BLK), lambda i: (0, 0))],
        out_specs=pl.BlockSpec((lp, BLK), lambda i: (0, 0)),
        out_shape=jax.ShapeDtypeStruct((lp, BLK), F32),
        compiler_params=_cparams("arbitrary"), name="fox_gate_fwd")(proj_r, bf128)


def _fox_gate_bwd(proj_r, dc, bf128):
    lp = proj_r.shape[0]
    nb = lp // BLK

    def kern(s_ref, dc_ref, b_ref, dfa_ref, dbf_ref):
        tri = _tri(BLK, False)

        dbf_ref[...] = jnp.zeros_like(dbf_ref)

        def body(ii, run):
            i = nb - 1 - ii
            rows = pl.ds(pl.multiple_of(i * BLK, BLK), BLK)
            dcb = dc_ref[rows, :]
            suf = _dot_exact(tri, dcb) + run
            dfa = jnp.where(_valid_rows(i * BLK, BLK), suf * _sigmoid(-(s_ref[rows, :] + b_ref[...])), 0.0)
            dfa_ref[rows, :] = dfa
            dbf_ref[...] += jnp.sum(dfa, axis=0, keepdims=True)
            return run + jnp.sum(dcb, axis=0, keepdims=True)

        lax.fori_loop(0, nb, body, jnp.zeros((1, BLK), F32))

    return pl.pallas_call(
        kern, grid=(1,),
        in_specs=[pl.BlockSpec((lp, BLK), lambda i: (0, R_SMALL_BLK128)), pl.BlockSpec((lp, BLK), lambda i: (0, 0)),
                  pl.BlockSpec((1, BLK), lambda i: (0, 0))],
        out_specs=[pl.BlockSpec((lp, BLK), lambda i: (0, 0)), pl.BlockSpec((1, BLK), lambda i: (0, 0))],
        out_shape=[jax.ShapeDtypeStruct((lp, BLK), F32), jax.ShapeDtypeStruct((1, BLK), F32)],
        compiler_params=_cparams("arbitrary"), name="fox_gate_bwd")(proj_r, dc, bf128)


LOG2E = 1.4426950408889634
KEY_PAD_BIAS = 1e30


def _attn_logits2(q, k, ck, diag):
    t = _dot_nt(q, k) * (LOG2E * FOX_DH ** -0.5) - ck * LOG2E
    if diag:
        r = lax.broadcasted_iota(jnp.int32, t.shape, 0)
        c = lax.broadcasted_iota(jnp.int32, t.shape, 1)
        t = jnp.where(c <= r, t, NEG)
    return t


ATTN_HEADS = 2


def _head_cols(a):
    return (slice(a * FOX_DH, (a + 1) * FOX_DH), slice(2 * a * FOX_DH, (2 * a + 1) * FOX_DH),
            slice((2 * a + 1) * FOX_DH, (2 * a + 2) * FOX_DH))


def _on_blocks(i, j, step):
    pl.when(j < i)(functools.partial(step, False))
    pl.when(j == i)(functools.partial(step, True))


def _attn_fwd(proj_a, cq, ck, tq=640, exch=None):
    lp = proj_a.shape[0]
    tq = _row_tile(lp, tq)
    tk = tq
    nq = lp // tq

    def kern(q_ref, kv_ref, cq_ref, ck_ref, o_ref, lse_ref, m_sc, l_sc, acc):
        i, j = pl.program_id(1), pl.program_id(2)

        @pl.when(j == 0)
        def _():
            m_sc[...] = jnp.full_like(m_sc, -jnp.inf)
            l_sc[...] = jnp.zeros_like(l_sc)
            acc[...] = jnp.zeros_like(acc)

        def step(diag):
            heads = range(ATTN_HEADS)
            cols = [_head_cols(a) for a in heads]
            m_old = [m_sc[a] for a in heads]
            l_old = [l_sc[a] for a in heads]
            acc_old = [acc[:, cols[a][0]] for a in heads]
            cq2 = [cq_ref[a] * LOG2E for a in heads]
            t = [_attn_logits2(q_ref[:, cols[a][0]], kv_ref[:, cols[a][1]], ck_ref[a], diag) for a in heads]
            m_new = [jnp.maximum(m_old[a], jnp.max(t[a], axis=-1, keepdims=True) + cq2[a]) for a in heads]
            p = [jnp.exp2(t[a] + (cq2[a] - m_new[a])) for a in heads]
            alpha = [jnp.exp2(m_old[a] - m_new[a]) for a in heads]
            l_new = [alpha[a] * l_old[a] + jnp.sum(p[a], axis=-1, keepdims=True) for a in heads]
            acc_new = [alpha[a] * acc_old[a] + _dot(p[a].astype(BF16), kv_ref[:, cols[a][2]]) for a in heads]
            for a in heads:
                m_sc[a] = m_new[a]
                l_sc[a] = l_new[a]
                acc[:, cols[a][0]] = acc_new[a]

        _on_blocks(i, j, step)

        @pl.when(j == nq - 1)
        def _():
            valid = _valid_rows(i * tq, tq)
            for a in range(ATTN_HEADS):
                hq, _, _ = _head_cols(a)
                o_ref[:, hq] = jnp.where(valid, acc[:, hq] / l_sc[a], 0.0).astype(BF16)
                lse_ref[a] = m_sc[a] + jnp.log(l_sc[a]) * LOG2E

    hp = ATTN_HEADS
    out, got = _pcall(
        kern, grid=(FOX_H // hp, nq, nq),
        in_specs=[pl.BlockSpec((tq, hp * FOX_DH), lambda h, i, j: (i, h)),
                  pl.BlockSpec((tk, 2 * hp * FOX_DH), lambda h, i, j: (jnp.minimum(j, i), KV0 // (2 * hp * FOX_DH) + h)),
                  pl.BlockSpec((hp, tq, 1), lambda h, i, j: (h, i, 0)),
                  pl.BlockSpec((hp, 1, tk), lambda h, i, j: (h, 0, jnp.minimum(j, i)))],
        out_specs=[pl.BlockSpec((tq, hp * FOX_DH), lambda h, i, j: (i, h)),
                   pl.BlockSpec((hp, tq, 1), lambda h, i, j: (h, i, 0))],
        out_shape=[jax.ShapeDtypeStruct((lp, FOX_W), BF16), jax.ShapeDtypeStruct((FOX_H, lp, 1), F32)],
        scratch=[pltpu.VMEM((hp, tq, 1), F32), pltpu.VMEM((hp, tq, 1), F32), pltpu.VMEM((tq, hp * FOX_DH), F32)],
        sem=("parallel", "parallel", "arbitrary"), name="attn_fwd", args=(proj_a, proj_a, cq, ck), exch=exch)
    return out[0], out[1], got


def _attn_bwd(proj_a, do, o, lse, cq, ck, dproj, tq=640, exch=None):
    lp = proj_a.shape[0]
    tq = _row_tile(lp, tq)
    tk = tq
    nq = lp // tq
    hp = ATTN_HEADS
    wq = hp * FOX_DH

    def kern(q_ref, kv_ref, do_ref, o_ref, lse_ref, cq_ref, ck_ref, dp_any, dproj_ref, dcq_ref, dck_ref,
             dk_acc, dv_acc, dq_acc, dcq_acc, dq_stage, dkv_stage, sems):
        del dp_any
        h, j, i = pl.program_id(0), pl.program_id(1), pl.program_id(2)
        rows = pl.ds(pl.multiple_of(i * tq, tq), tq)

        @pl.when(i == 0)
        def _():
            dck_ref[...] = jnp.zeros_like(dck_ref)
            dk_acc[...] = jnp.zeros_like(dk_acc)
            dv_acc[...] = jnp.zeros_like(dv_acc)

        @pl.when(jnp.logical_and(i == 0, j == 0))
        def _():
            dq_acc[...] = jnp.zeros_like(dq_acc)
            dcq_acc[...] = jnp.zeros_like(dcq_acc)

        def step(diag):
            heads = range(hp)
            cols = [_head_cols(a) for a in heads]
            dck_old = [dck_ref[a] for a in heads]
            dcq_old = [dcq_acc[a, rows, :] for a in heads]
            dk_old = [dk_acc[:, cols[a][0]] for a in heads]
            dv_old = [dv_acc[:, cols[a][0]] for a in heads]
            dq_old = [dq_acc[rows, cols[a][0]] for a in heads]
            shift = [cq_ref[a] * LOG2E - lse_ref[a] for a in heads]
            do_h = [do_ref[:, cols[a][0]] for a in heads]
            delta = [jnp.sum(do_h[a].astype(F32) * o_ref[:, cols[a][0]].astype(F32), axis=-1, keepdims=True)
                     for a in heads]
            t = [_attn_logits2(q_ref[:, cols[a][0]], kv_ref[:, cols[a][1]], ck_ref[a], diag) for a in heads]
            dp = [_dot_nt(do_h[a], kv_ref[:, cols[a][2]]) for a in heads]
            p = [jnp.exp2(t[a] + shift[a]) for a in heads]
            ds = [p[a] * (dp[a] - delta[a]) for a in heads]
            dsb = [ds[a].astype(BF16) for a in heads]
            dv_new = [dv_old[a] + _dot_tn(p[a].astype(BF16), do_h[a]) for a in heads]
            dck_new = [dck_old[a] - jnp.sum(ds[a], axis=0, keepdims=True) for a in heads]
            dcq_new = [dcq_old[a] + jnp.sum(ds[a], axis=-1, keepdims=True) for a in heads]
            dk_new = [dk_old[a] + _dot_tn(dsb[a], q_ref[:, cols[a][0]]) for a in heads]
            dq_new = [dq_old[a] + _dot(dsb[a], kv_ref[:, cols[a][1]]) for a in heads]
            for a in heads:
                dck_ref[a] = dck_new[a]
                dcq_acc[a, rows, :] = dcq_new[a]
                dk_acc[:, cols[a][0]] = dk_new[a]
                dv_acc[:, cols[a][0]] = dv_new[a]
                dq_acc[rows, cols[a][0]] = dq_new[a]

        _on_blocks(i, j, step)

        @pl.when(i == nq - 1)
        def _():
            parts = []
            for a in range(hp):
                hq, _, _ = _head_cols(a)
                parts += [dk_acc[:, hq] * (FOX_DH ** -0.5), dv_acc[:, hq]]
            dkv_stage[...] = jnp.concatenate(parts, axis=1).astype(BF16)
            out = pltpu.make_async_copy(
                dkv_stage, dproj_ref.at[pl.ds(pl.multiple_of(j * tk, tk), tk),
                                        pl.ds(pl.multiple_of(KV0 + h * 2 * wq, 2 * wq), 2 * wq)], sems.at[0])
            out.start()
            out.wait()

        @pl.when(jnp.logical_and(i == nq - 1, j == nq - 1))
        def _():
            dq_stage[...] = (dq_acc[...] * (FOX_DH ** -0.5)).astype(BF16)
            out = pltpu.make_async_copy(dq_stage, dproj_ref.at[:, pl.ds(pl.multiple_of(h * wq, wq), wq)], sems.at[1])
            rowsums = pltpu.make_async_copy(dcq_acc, dcq_ref.at[pl.ds(h * hp, hp)], sems.at[2])
            out.start()
            rowsums.start()
            out.wait()
            rowsums.wait()

    qspec = pl.BlockSpec((tq, wq), lambda h, j, i: (jnp.maximum(i, j), h))
    col = pl.BlockSpec((hp, tq, 1), lambda h, j, i: (h, jnp.maximum(i, j), 0))
    kvspec = pl.BlockSpec((tk, 2 * wq), lambda h, j, i: (j, KV0 // (2 * wq) + h))
    rowspec = pl.BlockSpec((hp, 1, tk), lambda h, j, i: (h, 0, j))
    out, got = _pcall(
        kern, grid=(FOX_H // hp, nq, nq),
        in_specs=[qspec, kvspec, qspec, qspec, col, col, rowspec, pl.BlockSpec(memory_space=pl.ANY)],
        out_specs=[pl.BlockSpec(memory_space=pl.ANY), pl.BlockSpec(memory_space=pl.ANY), rowspec],
        out_shape=[jax.ShapeDtypeStruct(dproj.shape, BF16), jax.ShapeDtypeStruct((FOX_H, lp, 1), F32),
                   jax.ShapeDtypeStruct((FOX_H, 1, lp), F32)],
        scratch=[pltpu.VMEM((tk, wq), F32), pltpu.VMEM((tk, wq), F32), pltpu.VMEM((lp, wq), F32),
                 pltpu.VMEM((hp, lp, 1), F32), pltpu.VMEM((lp, wq), BF16), pltpu.VMEM((tk, 2 * wq), BF16),
                 pltpu.SemaphoreType.DMA((3,))],
        aliases={7: 0}, sem=("arbitrary", "arbitrary", "arbitrary"), name="attn_bwd",
        args=(proj_a, proj_a, do, o, lse, cq, ck, dproj), exch=exch)
    return out[0], out[1], out[2], got


def _gla_gate_fwd(proj_r, wg2p, bg, tr=640):
    lp = proj_r.shape[0]
    tr = _row_tile(lp, tr)
    w = GLA_H * GLA_DK

    def kern(s_ref, w_ref, b_ref, o_ref):
        zg = _dot(s_ref[...].astype(BF16), w_ref[...]) + b_ref[...]
        o_ref[...] = jnp.where(_valid_rows(pl.program_id(0) * tr, tr), _log_sigmoid(zg) * (1.0 / GLA_TAU), 0.0)

    return pl.pallas_call(
        kern, grid=(lp // tr,),
        in_specs=[pl.BlockSpec((tr, BLK), lambda i: (i, R_SMALL_BLK128)), pl.BlockSpec((BLK, w), lambda i: (0, 0)),
                  pl.BlockSpec((1, w), lambda i: (0, 0))],
        out_specs=pl.BlockSpec((tr, w), lambda i: (i, 0)),
        out_shape=jax.ShapeDtypeStruct((lp, w), F32),
        compiler_params=_cparams("parallel"), name="gla_gate_fwd")(proj_r, wg2p, bg)


def _gla_chunk(grp, g):
    q = grp[:, :GLA_DK] * (GLA_DK ** -0.5)
    k = grp[:, GLA_DK:2 * GLA_DK]
    v = grp[:, 2 * GLA_DK:2 * GLA_DK + GLA_DV]
    r = grp[:, 2 * GLA_DK + GLA_DV:]
    b = _dot_exact(_tri(BLK, True), g)
    bl = b[BLK - 1:BLK, :]
    eb = jnp.exp(b)
    enb = jnp.exp(-b)
    ebl = jnp.exp(bl - b)
    qe, ke, kd = q * eb, k * enb, k * ebl
    causal = lax.broadcasted_iota(jnp.int32, (BLK, BLK), 1) <= lax.broadcasted_iota(jnp.int32, (BLK, BLK), 0)
    att = jnp.where(causal, _dot_nt(qe.astype(BF16), ke.astype(BF16)), 0.0)
    return q, k, v, r, bl, eb, enb, ebl, qe, ke, kd, causal, att


def _gla_fwd(proj_r, logg, gn):
    lp = proj_r.shape[0]
    nc = lp // BLK
    wv = GLA_H * GLA_DV

    def kern(grp_ref, g_ref, gn_ref, o_ref, zc_ref, st_ref, st):
        c = pl.program_id(0)

        @pl.when(c == 0)
        def _():
            st[...] = jnp.zeros_like(st)

        for h in range(GLA_H):
            kcol = slice(h * GLA_DK, (h + 1) * GLA_DK)
            vcol = slice(h * GLA_DV, (h + 1) * GLA_DV)
            q, k, v, r, bl, eb, enb, ebl, qe, ke, kd, causal, att = _gla_chunk(
                grp_ref[:, h * GLA_GRP:(h + 1) * GLA_GRP], g_ref[:, kcol])
            s_t = st[h]
            st_ref[h] = s_t
            vb = v.astype(BF16)
            o = _dot(att.astype(BF16), vb) + _dot_nt(qe.astype(BF16), s_t.astype(BF16))
            st[h] = s_t * jnp.exp(bl) + _dot_tn(vb, kd.astype(BF16))
            o_ref[:, vcol] = o
            rstd = lax.rsqrt(jnp.mean(o * o, axis=-1, keepdims=True) + EPS)
            zc_ref[:, vcol] = (r * _sigmoid(r) * (o * rstd * gn_ref[:, vcol])).astype(BF16)

    vspec = pl.BlockSpec((BLK, wv), lambda c: (c, 0))
    return pl.pallas_call(
        kern, grid=(nc,),
        in_specs=[pl.BlockSpec((BLK, GLA_H * GLA_GRP), lambda c: (c, R_GLA_BLK0 // GLA_H)),
                  pl.BlockSpec((BLK, GLA_H * GLA_DK), lambda c: (c, 0)),
                  pl.BlockSpec((1, wv), lambda c: (0, 0))],
        out_specs=[vspec, vspec, pl.BlockSpec((GLA_H, None, GLA_DV, GLA_DK), lambda c: (0, c, 0, 0))],
        out_shape=[jax.ShapeDtypeStruct((lp, wv), F32), jax.ShapeDtypeStruct((lp, wv), BF16),
                   jax.ShapeDtypeStruct((GLA_H, nc, GLA_DV, GLA_DK), F32)],
        scratch_shapes=[pltpu.VMEM((GLA_H, GLA_DV, GLA_DK), F32)],
        compiler_params=_cparams("arbitrary"), name="gla_fwd")(proj_r, logg, gn)


def _gla_bwd(proj_r, logg, st_all, o_all, dzc, gn, dproj):
    lp = proj_r.shape[0]
    nc = lp // BLK

    def kern(grp_ref, g_ref, st_ref, o_ref, dzc_ref, gn_ref, dp_any, dgrp_ref, dlg_ref, dgn_ref, dst):
        del dp_any
        cc = pl.program_id(0)

        @pl.when(cc == 0)
        def _():
            dst[...] = jnp.zeros_like(dst)
            dgn_ref[...] = jnp.zeros_like(dgn_ref)

        for h in range(GLA_H):
            kcol = slice(h * GLA_DK, (h + 1) * GLA_DK)
            vcol = slice(h * GLA_DV, (h + 1) * GLA_DV)
            q, k, v, r, bl, eb, enb, ebl, qe, ke, kd, causal, att = _gla_chunk(
                grp_ref[:, h * GLA_GRP:(h + 1) * GLA_GRP], g_ref[:, kcol])
            s_t = st_ref[h]
            d_st = dst[h]
            o = o_ref[:, vcol]
            dzc_v = dzc_ref[:, vcol]
            gnv = gn_ref[:, vcol]
            rstd = lax.rsqrt(jnp.mean(o * o, axis=-1, keepdims=True) + EPS)
            xhat = o * rstd
            sr = _sigmoid(r)
            dr = dzc_v * (xhat * gnv) * (sr * (1.0 + r * (1.0 - sr)))
            docn = dzc_v * (r * sr)
            dgn_ref[:, vcol] += jnp.sum(docn * xhat, axis=0, keepdims=True)
            dxh = docn * gnv
            do = rstd * (dxh - xhat * jnp.mean(dxh * xhat, axis=-1, keepdims=True))
            dob, vb = do.astype(BF16), v.astype(BF16)
            qeb, keb, kdb = qe.astype(BF16), ke.astype(BF16), kd.astype(BF16)
            datt = jnp.where(causal, _dot_nt(dob, vb), 0.0).astype(BF16)
            dv = _dot_tn(att.astype(BF16), dob) + _dot_nt(kdb, d_st.astype(BF16))
            dqe = _dot(datt, keb) + _dot(dob, s_t.astype(BF16))
            dke = _dot_tn(datt, qeb)
            dkd = _dot(vb, d_st.astype(BF16))
            dq = dqe * eb * (GLA_DK ** -0.5)
            dk = dke * enb + dkd * ebl
            kd_dkd = dkd * kd
            db = dqe * qe - dke * ke - kd_dkd
            db_last = (jnp.sum(kd_dkd, axis=0, keepdims=True)
                       + jnp.exp(bl) * jnp.sum(s_t * d_st, axis=0, keepdims=True))
            dlg_ref[:, kcol] = _dot_exact(_tri(BLK, False), db) + db_last
            dst[h] = d_st * jnp.exp(bl) + _dot_tn(dob, qeb)
            dgrp_ref[:, h * GLA_GRP:(h + 1) * GLA_GRP] = jnp.concatenate([dq, dk, dv, dr], axis=1).astype(BF16)

    rev = lambda c: nc - 1 - c
    wv = GLA_H * GLA_DV
    vspec = pl.BlockSpec((BLK, wv), lambda c: (rev(c), 0))
    kspec = pl.BlockSpec((BLK, GLA_H * GLA_DK), lambda c: (rev(c), 0))
    return pl.pallas_call(
        kern, grid=(nc,),
        in_specs=[pl.BlockSpec((BLK, GLA_H * GLA_GRP), lambda c: (rev(c), R_GLA_BLK0 // GLA_H)), kspec,
                  pl.BlockSpec((GLA_H, None, GLA_DV, GLA_DK), lambda c: (0, rev(c), 0, 0)),
                  vspec, vspec, pl.BlockSpec((1, wv), lambda c: (0, 0)),
                  pl.BlockSpec(memory_space=pl.ANY)],
        out_specs=[pl.BlockSpec((BLK, GLA_H * GLA_GRP), lambda c: (rev(c), F_GLA_BLK0 // GLA_H)), kspec,
                   pl.BlockSpec((1, wv), lambda c: (0, 0))],
        out_shape=[jax.ShapeDtypeStruct(dproj.shape, BF16), jax.ShapeDtypeStruct((lp, GLA_H * GLA_DK), F32),
                   jax.ShapeDtypeStruct((1, wv), F32)],
        scratch_shapes=[pltpu.VMEM((GLA_H, GLA_DV, GLA_DK), F32)],
        input_output_aliases={6: 0},
        compiler_params=_cparams("arbitrary"), name="gla_bwd",
    )(proj_r, logg, st_all, o_all, dzc, gn, dproj)


def _small_bwd(proj_r, dlogg, wg2p, wg2pt, bg, dfa, dproj, tr=640):
    lp = proj_r.shape[0]
    tr = _row_tile(lp, tr)
    w = GLA_H * GLA_DK

    def kern(s_ref, dlg_ref, w_ref, wt_ref, b_ref, dfa_ref, dp_any, ds_ref, dbg_ref, dw_ref):
        del dp_any
        i = pl.program_id(0)
        sb = s_ref[...].astype(BF16)
        zg = _dot(sb, w_ref[...]) + b_ref[...]
        dzg = jnp.where(_valid_rows(i * tr, tr), dlg_ref[...] * (1.0 / GLA_TAU) * _sigmoid(-zg), 0.0)

        @pl.when(i == 0)
        def _():
            dbg_ref[...] = jnp.zeros_like(dbg_ref)
            dw_ref[...] = jnp.zeros_like(dw_ref)

        dbg_ref[...] += jnp.sum(dzg, axis=0, keepdims=True)
        dzb = dzg.astype(BF16)
        dw_ref[...] += _dot_tn(sb, dzb)
        dsm = _dot(dzb, wt_ref[...]) + dfa_ref[...]
        ds_ref[...] = jnp.concatenate([dsm, jnp.zeros((tr, SMALL_W - BLK), F32)], axis=1).astype(BF16)

    return pl.pallas_call(
        kern, grid=(lp // tr,),
        in_specs=[pl.BlockSpec((tr, BLK), lambda i: (i, R_SMALL_BLK128)), pl.BlockSpec((tr, w), lambda i: (i, 0)),
                  pl.BlockSpec((BLK, w), lambda i: (0, 0)), pl.BlockSpec((w, BLK), lambda i: (0, 0)),
                  pl.BlockSpec((1, w), lambda i: (0, 0)), pl.BlockSpec((tr, BLK), lambda i: (i, 0)),
                  pl.BlockSpec(memory_space=pl.ANY)],
        out_specs=[pl.BlockSpec((tr, SMALL_W), lambda i: (i, F_SMALL_BLK0)), pl.BlockSpec((1, w), lambda i: (0, 0)),
                   pl.BlockSpec((BLK, w), lambda i: (0, 0))],
        out_shape=[jax.ShapeDtypeStruct(dproj.shape, BF16), jax.ShapeDtypeStruct((1, w), F32),
                   jax.ShapeDtypeStruct((BLK, w), F32)],
        input_output_aliases={6: 0},
        compiler_params=_cparams("arbitrary"), name="small_bwd",
    )(proj_r, dlogg, wg2p, wg2pt, bg, dfa, dproj)


def _merge_fwd(proj_r, gate_b3, ya, yb, yc, tr=640):
    lp = proj_r.shape[0]
    tr = _row_tile(lp, tr)
    tn = GATE_TN

    def kern(g_ref, b_ref, ya_ref, yb_ref, yc_ref, o_ref):
        g = g_ref[...]
        mix = (_sigmoid(g[:, :tn] + b_ref[0:1, :]) * ya_ref[...]
               + _sigmoid(g[:, tn:2 * tn] + b_ref[1:2, :]) * yb_ref[...]
               + _sigmoid(g[:, 2 * tn:] + b_ref[2:3, :]) * yc_ref[...])
        o_ref[...] = mix.astype(BF16)

    y = pl.BlockSpec((tr, tn), lambda i, j: (i, j))
    return pl.pallas_call(
        kern, grid=(lp // tr, D // tn),
        in_specs=[pl.BlockSpec((tr, 3 * tn), lambda i, j: (i, R_GATE_BLK0 + j)),
                  pl.BlockSpec((3, tn), lambda i, j: (0, j)), y, y, y],
        out_specs=y, out_shape=jax.ShapeDtypeStruct((lp, D), BF16),
        compiler_params=_cparams("parallel", "parallel"), name="merge_fwd")(proj_r, gate_b3, ya, yb, yc)


def _merge_bwd(proj_r, gate_b3, ya, yb, yc, dmix, tr=640):
    lp = proj_r.shape[0]
    tr = _row_tile(lp, tr)
    tn = GATE_TN

    def kern(g_ref, b_ref, ya_ref, yb_ref, yc_ref, dm_ref, dya_ref, dyb_ref, dyc_ref, dg_ref, db_ref):
        i = pl.program_id(1)
        g = g_ref[...]
        dm = dm_ref[...]

        @pl.when(i == 0)
        def _():
            db_ref[...] = jnp.zeros_like(db_ref)

        dgs = []
        for n, (y_ref, dy_ref) in enumerate(((ya_ref, dya_ref), (yb_ref, dyb_ref), (yc_ref, dyc_ref))):
            s = _sigmoid(g[:, n * tn:(n + 1) * tn] + b_ref[n:n + 1, :])
            dy_ref[...] = (dm * s).astype(BF16)
            dgn = dm * y_ref[...] * (s * (1.0 - s))
            db_ref[n:n + 1, :] += jnp.sum(dgn, axis=0, keepdims=True)
            dgs.append(dgn)
        dg_ref[...] = jnp.concatenate(dgs, axis=1).astype(BF16)

    y = pl.BlockSpec((tr, tn), lambda j, i: (i, j))
    bspec = pl.BlockSpec((3, tn), lambda j, i: (0, j))
    return pl.pallas_call(
        kern, grid=(D // tn, lp // tr),
        in_specs=[pl.BlockSpec((tr, 3 * tn), lambda j, i: (i, R_GATE_BLK0 + j)), bspec, y, y, y, y],
        out_specs=[y, y, y, pl.BlockSpec((tr, 3 * tn), lambda j, i: (i, F_GATE_BLK0 + j)), bspec],
        out_shape=[jax.ShapeDtypeStruct((lp, D), BF16)] * 3
        + [jax.ShapeDtypeStruct((lp, NP), BF16), jax.ShapeDtypeStruct((3, D), F32)],
        compiler_params=_cparams("parallel", "arbitrary"), name="merge_bwd")(proj_r, gate_b3, ya, yb, yc, dmix)


def _final_loss(h, gf, tgt):
    lp = h.shape[0]
    nb = lp // BLK

    def kern(h_ref, g_ref, t_ref, dh_ref, dg_ref, ls_ref):
        i = pl.program_id(0)

        @pl.when(i == 0)
        def _():
            dh_ref[...] = jnp.zeros_like(dh_ref)
            dg_ref[...] = jnp.zeros_like(dg_ref)
            ls_ref[...] = jnp.zeros_like(ls_ref)

        @pl.when(i > 0)
        def _():
            x = h_ref[...]
            r = lax.rsqrt(jnp.mean(x * x, axis=-1, keepdims=True) + EPS)
            xhat = x * r
            err = xhat * g_ref[...] - t_ref[...]
            ls_ref[...] += jnp.sum(jnp.sum(err * err, axis=0, keepdims=True), axis=1, keepdims=True)
            dy = err * (1.0 / D)
            dg_ref[...] += jnp.sum(dy * xhat, axis=0, keepdims=True)
            dxh = dy * g_ref[...]
            dh_ref[...] = r * (dxh - xhat * jnp.mean(dxh * xhat, axis=-1, keepdims=True))

    row = pl.BlockSpec((BLK, D), lambda i: (i, 0))
    vec = pl.BlockSpec((1, D), lambda i: (0, 0))
    return pl.pallas_call(
        kern, grid=(nb,),
        in_specs=[row, vec, pl.BlockSpec((BLK, D), lambda i: (jnp.maximum(i - 1, 0), 0))],
        out_specs=[row, vec, pl.BlockSpec((1, 1), lambda i: (0, 0))],
        out_shape=[jax.ShapeDtypeStruct((lp, D), F32), jax.ShapeDtypeStruct((1, D), F32),
                   jax.ShapeDtypeStruct((1, 1), F32)],
        compiler_params=_cparams("arbitrary"), name="final_loss")(h, gf, tgt)


def _gate_cols(c):
    ct = c[:, :FOX_H].T
    ck = jnp.where(jnp.arange(ct.shape[1]) < PAD, KEY_PAD_BIAS, ct)
    return ct[:, :, None], ck[:, None, :]


def _run(hosts, name, ctx, fn):
    if hosts and name in hosts:
        make, done = hosts[name]
        res = fn(make(ctx))
        done(res[-1])
    else:
        res = fn(None)
    return res[:-1]


def _mm_nn_x(a, b, exch, **kw):
    out = _mm_nn(a, b, exch=exch, **kw)
    return out if exch is not None else (out, None)


def _layer_fwd(h, w, hosts=None):
    xn = _rmsnorm_fwd(h, w["norm1_g"])
    proj_a = _mm_nn(xn, w["w_in"], n0=0, n=REST0, out_dtype=BF16, name="proj_a")
    proj_r, = _run(hosts, "proj_r", w, lambda e: _mm_nn_x(xn, w["w_in"], e, n0=REST0, n=NREST, name="proj_r"))
    cq, ck = _gate_cols(_fox_gate_fwd(proj_r, w["bf128"]))
    oa, lse = _run(hosts, "attn_fwd", w, lambda e: _attn_fwd(proj_a, cq, ck, exch=e))
    zb = _convb_fwd(proj_r, w["conv_w"])
    logg = _gla_gate_fwd(proj_r, w["wg2p"], w["gla_b_g"])
    o_gla, zc, st_all = _gla_fwd(proj_r, logg, w["gla_norm_g"])
    ya = _mm_nn(oa, w["w_a_o"], name="branch_a")
    yb = _mm_nn(zb, w["w_b_o"], name="branch_b")
    yc = _mm_nn(zc, w["w_c_o"], name="branch_c")
    mix = _merge_fwd(proj_r, w["gate_b3"], ya, yb, yc)
    h1 = _mm_nn(mix, w["w_o"], res=h, name="out_proj")
    xn2 = _rmsnorm_fwd(h1, w["norm2_g"])
    z, = _run(hosts, "up_proj", w, lambda e: _mm_nn_x(xn2, w["w_up"], e, out_dtype=BF16, name="up_proj"))
    a = _mlp_gate_fwd(z, w["mlp_conv_w"])
    h2, = _run(hosts, "down_proj", w,
               lambda e: _mm_nn_x(a, w["w_down"], e, res=h1, tk=D_FF // 4, name="down_proj"))
    saved = dict(h=h, xn=xn, proj_a=proj_a, proj_r=proj_r, cq=cq, ck=ck, oa=oa, lse=lse, zb=zb, logg=logg,
                 o_gla=o_gla, zc=zc, st_all=st_all, ya=ya, yb=yb, yc=yc, mix=mix, h1=h1, xn2=xn2, z=z, a=a)
    return h2, saved


def _layer_bwd(dh2, w, s, hosts=None):
    g = {}
    da = _mm_nt(dh2, w["w_down"], tn=D_FF // 4, out_dtype=BF16, name="d_down_in")
    g["w_down"] = _mm_tn(s["a"], dh2, tk=D_FF // 4, name="d_w_down")
    dzg, dzu, dmw_g, dmw_u = _run(hosts, "mlp_gate_bwd", g, lambda e: _mlp_gate_bwd(
        s["z"], da, w["mlp_conv_w"], exch=e))
    g["mlp_conv_w"] = jnp.concatenate([dmw_g, dmw_u], axis=1)
    dxn2 = _mm_nt(dzg, w["w_up"], k0=0, kw=D_FF, tk=D_FF // 4, name="d_up_in_g")
    dxn2 = _mm_nt(dzu, w["w_up"], k0=D_FF, kw=D_FF, tk=D_FF // 4, add=dxn2, name="d_up_in_u")
    g["w_up"] = jnp.concatenate([_mm_tn(s["xn2"], dzg, name="d_w_up_g"), _mm_tn(s["xn2"], dzu, name="d_w_up_u")], axis=1)
    dh1, g["norm2_g"] = _rmsnorm_bwd(s["h1"], w["norm2_g"], dxn2, dh2)
    dmix = _mm_nt(dh1, w["w_o"], name="d_out_proj_in")
    g["w_o"] = _mm_tn(s["mix"], dh1, name="d_w_o")
    dya, dyb, dyc, dproj, g["gate_b3"] = _merge_bwd(s["proj_r"], w["gate_b3"], s["ya"], s["yb"], s["yc"], dmix)
    doa = _mm_nt(dya, w["w_a_o"], out_dtype=BF16, name="d_branch_a_in")
    g["w_a_o"] = _mm_tn(s["oa"], dya, name="d_w_a_o")
    dzb = _mm_nt(dyb, w["w_b_o"], name="d_branch_b_in")
    g["w_b_o"] = _mm_tn(s["zb"], dyb, name="d_w_b_o")
    dzc = _mm_nt(dyc, w["w_c_o"], name="d_branch_c_in")
    g["w_c_o"] = _mm_tn(s["zc"], dyc, name="d_w_c_o")
    dproj, dlogg, g["gla_norm_g"] = _gla_bwd(s["proj_r"], s["logg"], s["st_all"], s["o_gla"], dzc, w["gla_norm_g"], dproj)
    dproj, g["conv_w"] = _convb_bwd(s["proj_r"], dzb, w["conv_w"], dproj)
    dproj, dcq, dck = _run(hosts, "attn_bwd", g, lambda e: _attn_bwd(
        s["proj_a"], doa, s["oa"], s["lse"], s["cq"], s["ck"], dproj, exch=e))
    dc = jnp.pad((dcq[:, :, 0] + dck[:, 0, :]).T, ((0, 0), (0, BLK - FOX_H)))
    dfa, g["bf128"] = _fox_gate_bwd(s["proj_r"], dc, w["bf128"])
    dproj, g["gla_b_g"], g["wg2p"] = _small_bwd(s["proj_r"], dlogg, w["wg2p"], w["wg2p"].T, w["gla_b_g"], dfa, dproj)
    def pair(out, e):
        return out if e is not None else (out, None)

    g["w_in"], = _run(hosts, "d_w_in", g, lambda e: pair(_mm_tn(s["xn"], dproj, name="d_w_in", exch=e), e))
    dxn, = _run(hosts, "d_in_proj_in", g, lambda e: pair(_mm_nt(dproj, w["w_in"], name="d_in_proj_in", exch=e), e))
    dh0, g["norm1_g"] = _rmsnorm_bwd(s["h"], w["norm1_g"], dxn, dh1)
    return dh0, g


def _local_step(x, tgt, meta, final_g, layers, hosts_fwd=None, hosts_bwd=None):
    h = jnp.concatenate([jnp.zeros((PAD, D), F32), meta, x], axis=0)
    saved = []
    for l, w in enumerate(layers):
        h, s = _layer_fwd(h, w, hosts_fwd[l] if hosts_fwd else None)
        saved.append(s)
    dh, dgf, sq = _final_loss(h, final_g, tgt)
    grads = [None] * len(layers)
    for l in reversed(range(len(layers))):
        dh, grads[l] = _layer_bwd(dh, layers[l], saved[l], hosts_bwd[l](grads) if hosts_bwd else None)
    return sq[0, 0], dh[BLK:], dh[PAD:BLK], dgf, grads


def _w_in_to_kernel(w_nat):
    parts = [w_nat[:, s:s + n] for s, n in _segments()]
    parts.append(jnp.zeros((w_nat.shape[0], SMALL_W - 8 - GLA_R), w_nat.dtype))
    return jnp.concatenate(parts, axis=1)


def _w_in_from_kernel(w_k):
    pieces, off = [], 0
    for s, n in _segments():
        pieces.append((s, w_k[:, off:off + n]))
        off += n
    return jnp.concatenate([p for _, p in sorted(pieces, key=lambda t: t[0])], axis=1)


def _w_in_slots_to_kernel(got):
    per = got.shape[2]
    parts = []
    for s, n in _segments():
        while n > 0:
            d, lo = divmod(s, per)
            take = min(n, per - lo)
            parts.append(got[d, :, lo:lo + take])
            s, n = s + take, n - take
    parts.append(jnp.zeros((got.shape[1], SMALL_W - 8 - GLA_R), got.dtype))
    return jnp.concatenate(parts, axis=1)


def _w_in_kernel_to_slots(w_k):
    per = N_IN // N_DEV
    pieces, off = [], 0
    for s, n in _segments():
        pieces.append((s, n, off))
        off += n
    slots = []
    for d in range(N_DEV):
        lo, hi = d * per, (d + 1) * per
        parts = [w_k[:, off + max(s, lo) - s:off + min(s + n, hi) - s]
                 for s, n, off in sorted(pieces) if max(s, lo) < min(s + n, hi)]
        slots.append(jnp.concatenate(parts, axis=1))
    return jnp.stack(slots)


def _pad_rows_at(a, row0, nrows):
    return jnp.pad(a, ((row0, nrows - row0 - a.shape[0]), (0, 0)))


def _big_to_kernel(name, full):
    return _w_in_to_kernel(full) if name == "w_in" else full


def _layer_weights(big, conv_w, gla_w_g2, mlp_conv_w, norm1_g, fox_b_f, gate_b, gla_b_g, gla_norm_g, norm2_g):
    w = {n: _big_to_kernel(n, a) for n, a in big.items()}
    w.update(
        conv_w=conv_w, mlp_conv_w=mlp_conv_w,
        wg2p=_pad_rows_at(gla_w_g2, 8, BLK).astype(BF16),
        norm1_g=norm1_g[None], norm2_g=norm2_g[None], gla_b_g=gla_b_g[None], gla_norm_g=gla_norm_g[None],
        bf128=jnp.pad(fox_b_f, (0, BLK - FOX_H))[None], gate_b3=gate_b.reshape(3, D))
    return w


def _layer_grads_natural(g):
    return dict(
        w_in=_w_in_from_kernel(g["w_in"]), w_a_o=g["w_a_o"], w_b_o=g["w_b_o"], w_c_o=g["w_c_o"], w_o=g["w_o"],
        w_up=g["w_up"], w_down=g["w_down"], conv_w=g["conv_w"], mlp_conv_w=g["mlp_conv_w"],
        gla_w_g2=g["wg2p"][8:8 + GLA_R], norm1_g=g["norm1_g"][0], norm2_g=g["norm2_g"][0],
        gla_b_g=g["gla_b_g"][0], gla_norm_g=g["gla_norm_g"][0], fox_b_f=g["bf128"][0, :FOX_H],
        gate_b=g["gate_b3"].reshape(3 * D))


def _adamw(recv, w, m, v, layer, prev=None, name="adamw"):
    n_slot, r, c = recv.shape
    lyr = w.shape[0]
    tr = r
    for t in range(16, r, 16):
        if r % t == 0 and t * c <= ADAMW_BLOCK_ELEMS:
            tr = t
    if r * c <= ADAMW_BLOCK_ELEMS:
        tr = r
    bc1, bc2 = 1.0 - ADAM_B1 ** ADAM_STEP, 1.0 - ADAM_B2 ** ADAM_STEP

    def kern(*refs):
        r_ref, w_ref, m_ref, v_ref = refs[:4]
        g_out, d_out, m_out, v_out = refs[-4:]
        g = r_ref[0].astype(F32)
        for sidx in range(1, n_slot):
            g = g + r_ref[sidx].astype(F32)
        m_new = ADAM_B1 * m_ref[...] + (1.0 - ADAM_B1) * g
        v_new = ADAM_B2 * v_ref[...] + (1.0 - ADAM_B2) * (g * g)
        g_out[...] = g
        m_out[...] = m_new
        v_out[...] = v_new
        d_out[...] = -ADAM_LR * ((m_new / bc1) / (jnp.sqrt(v_new / bc2) + ADAM_EPS) + ADAM_WD * w_ref[...])

    lspec = pl.BlockSpec((None, tr, c), lambda i: (layer, i, 0))
    in_specs = [pl.BlockSpec((n_slot, tr, c), lambda i: (0, i, 0)), lspec, lspec, lspec]
    args = [recv, w, m, v]
    aliases = {}
    if prev is not None:
        in_specs += [pl.BlockSpec(memory_space=pl.ANY)] * 4
        args += list(prev)
        aliases = {4: 0, 5: 1, 6: 2, 7: 3}
    return pl.pallas_call(
        kern, grid=(r // tr,), in_specs=in_specs, out_specs=[lspec] * 4,
        out_shape=[jax.ShapeDtypeStruct((lyr, r, c), F32)] * 4, input_output_aliases=aliases,
        compiler_params=_cparams("parallel"), name=name)(*args)


_BIG = ("w_in", "w_a_o", "w_b_o", "w_c_o", "w_o", "w_up", "w_down")
_COL_SHARDED = ("w_in", "w_a_o", "w_b_o", "w_c_o", "w_up", "conv_w", "gla_w_g2", "mlp_conv_w")
_REPL = ("norm1_g", "fox_b_f", "gate_b", "gla_b_g", "gla_norm_g", "norm2_g")


def _cols_from_slots(a):
    return jnp.transpose(a, (1, 0, 2)).reshape(a.shape[1], N_DEV * a.shape[2])


def _cols_to_slots(a):
    r, c8 = a.shape
    return jnp.transpose(a.reshape(r, N_DEV, c8 // N_DEV), (1, 0, 2))


def _rows_to_slots(a):
    return a.reshape(N_DEV, a.shape[0] // N_DEV, a.shape[1])


def kernel(x, meta_tokens, norm1_g, w_in, fox_b_f, gate_b, conv_w, gla_w_g2, gla_b_g, gla_norm_g, w_a_o, w_b_o, w_c_o, w_o, norm2_g, w_up, mlp_conv_w, w_down, final_norm_g, loss_target, m_meta_tokens, m_norm1_g, m_w_in, m_fox_b_f, m_gate_b, m_conv_w, m_gla_w_g2, m_gla_b_g, m_gla_norm_g, m_w_a_o, m_w_b_o, m_w_c_o, m_w_o, m_norm2_g, m_w_up, m_mlp_conv_w, m_w_down, m_final_norm_g, v_meta_tokens, v_norm1_g, v_w_in, v_fox_b_f, v_gate_b, v_conv_w, v_gla_w_g2, v_gla_b_g, v_gla_norm_g, v_w_a_o, v_w_b_o, v_w_c_o, v_w_o, v_norm2_g, v_w_up, v_mlp_conv_w, v_w_down, v_final_norm_g):
    names = ("meta_tokens", "norm1_g", "w_in", "fox_b_f", "gate_b", "conv_w", "gla_w_g2", "gla_b_g", "gla_norm_g",
             "w_a_o", "w_b_o", "w_c_o", "w_o", "norm2_g", "w_up", "mlp_conv_w", "w_down", "final_norm_g")
    wts = dict(zip(names, (meta_tokens, norm1_g, w_in, fox_b_f, gate_b, conv_w, gla_w_g2, gla_b_g, gla_norm_g,
                           w_a_o, w_b_o, w_c_o, w_o, norm2_g, w_up, mlp_conv_w, w_down, final_norm_g)))
    mom = dict(zip(names, (m_meta_tokens, m_norm1_g, m_w_in, m_fox_b_f, m_gate_b, m_conv_w, m_gla_w_g2, m_gla_b_g,
                           m_gla_norm_g, m_w_a_o, m_w_b_o, m_w_c_o, m_w_o, m_norm2_g, m_w_up, m_mlp_conv_w, m_w_down,
                           m_final_norm_g)))
    var = dict(zip(names, (v_meta_tokens, v_norm1_g, v_w_in, v_fox_b_f, v_gate_b, v_conv_w, v_gla_w_g2, v_gla_b_g,
                           v_gla_norm_g, v_w_a_o, v_w_b_o, v_w_c_o, v_w_o, v_norm2_g, v_w_up, v_mlp_conv_w, v_w_down,
                           v_final_norm_g)))

    small = _exchange([conv_w, gla_w_g2, mlp_conv_w, meta_tokens], [True] * 4, "gather_small")
    conv_full = jnp.transpose(small[0], (1, 2, 0, 3)).reshape(DEPTH, 3, CONV_CH)
    g2_full = jnp.transpose(small[1], (1, 2, 0, 3)).reshape(DEPTH, GLA_R, GLA_H * GLA_DK)
    mconv_full = jnp.transpose(small[2], (1, 2, 0, 3)).reshape(DEPTH, 3, 2 * D_FF)
    meta_full = _cols_from_slots(small[3])
    layers = [_layer_weights({}, conv_full[l], g2_full[l], mconv_full[l], norm1_g[l], fox_b_f[l], gate_b[l],
                             gla_b_g[l], gla_norm_g[l], norm2_g[l]) for l in range(DEPTH)]

    wide = ("w_a_o", "w_b_o", "w_c_o", "w_up")

    def gather(l, which):
        def make(_):
            return [wts[n][l].astype(BF16) for n in which], [("wide" if n in wide else True) for n in which]

        def done(got):
            for n, a in zip(which, got):
                if n == "w_in":
                    layers[l][n] = _w_in_slots_to_kernel(a)
                else:
                    layers[l][n] = a if n in wide else a.reshape(-1, a.shape[-1])

        return make, done

    recv_big = [dict() for _ in range(DEPTH)]

    def scatter(l, which, grads_of):
        def make(ctx):
            g = grads_of(ctx)
            send = [_w_in_kernel_to_slots(g[n]) if n == "w_in" else g[n] if n in wide else _rows_to_slots(g[n])
                    for n in which]
            return send, [("cols" if n in wide else False) for n in which]

        def done(got):
            recv_big[l].update(zip(which, got))

        return make, done

    mixers = ("w_o", "w_a_o", "w_b_o", "w_c_o")
    early = ("w_down", "w_up") + mixers
    make, done = gather(0, ("w_in",))
    done(_exchange(*make(None), "gather_w_in"))
    hosts_fwd = [
        {"proj_r": gather(0, mixers + ("w_down",)), "attn_fwd": gather(0, ("w_up",)),
         "up_proj": gather(1, ("w_in",)), "down_proj": gather(1, mixers)},
        {"attn_fwd": gather(1, ("w_up", "w_down"))}]
    hosts_bwd = [
        lambda grads: {"mlp_gate_bwd": scatter(1, ("w_in",), lambda _: grads[1]),
                       "attn_bwd": scatter(1, ("w_up", "w_down") + mixers, lambda _: grads[1]),
                       "d_w_in": scatter(0, early, lambda g: g),
                       "d_in_proj_in": scatter(0, ("w_in",), lambda g: g)},
        lambda grads: None]

    sq, grad_x, dmeta, dgf, grads_k = _local_step(x[0], loss_target[0], meta_full, final_norm_g[None], layers,
                                                  hosts_fwd, hosts_bwd)
    loss = lax.psum(sq * (0.5 / D), ("x", "y", "c"))
    grads = [_layer_grads_natural(g) for g in grads_k]

    out_g, out_d, out_m, out_v = {}, {}, {}, {}

    def update(name, recv, layer, lyr_shape, prev):
        w3, m3, v3 = (t[name].reshape(lyr_shape) for t in (wts, mom, var))
        return _adamw(recv.reshape((recv.shape[0],) + lyr_shape[1:]), w3, m3, v3, layer, prev, name="adamw_" + name)

    def store(name, res):
        shape = wts[name].shape
        out_g[name], out_d[name], out_m[name], out_v[name] = (t.reshape(shape) for t in res)

    for n in _BIG:
        res = None
        for l in range(DEPTH):
            res = update(n, recv_big[l][n], l, wts[n].shape, res)
        store(n, res)

    def stack_layers(name):
        return jnp.stack([grads[l][name] for l in range(DEPTH)])

    s_conv = jnp.transpose(stack_layers("conv_w").reshape(DEPTH, 3, N_DEV, -1), (2, 0, 1, 3))
    s_g2 = jnp.transpose(stack_layers("gla_w_g2").reshape(DEPTH, GLA_R, N_DEV, -1), (2, 0, 1, 3))
    s_mconv = jnp.transpose(stack_layers("mlp_conv_w").reshape(DEPTH, 3, N_DEV, -1), (2, 0, 1, 3))
    s_meta = _cols_to_slots(dmeta)
    repl = [stack_layers(n) for n in _REPL] + [dgf]
    pack = jnp.concatenate([jnp.pad(a.reshape(-1), (0, (-a.size) % 1024)) for a in repl]).reshape(-1, BLK)
    r_conv, r_g2, r_mconv, r_meta, r_pack = _exchange(
        [s_conv, s_g2, s_mconv, s_meta, pack], [False, False, False, False, True], "scatter_small")
    store("conv_w", update("conv_w", r_conv, 0, (1, DEPTH * 3, CONV_CH // N_DEV), None))
    store("gla_w_g2", update("gla_w_g2", r_g2, 0, (1, DEPTH * GLA_R, GLA_H * GLA_DK // N_DEV), None))
    store("mlp_conv_w", update("mlp_conv_w", r_mconv, 0, (1, DEPTH * 3, 2 * D_FF // N_DEV), None))
    store("meta_tokens", update("meta_tokens", r_meta, 0, (1, N_META, D // N_DEV), None))
    off = 0
    for n, a in zip(_REPL + ("final_norm_g",), repl):
        rows = (a.size + 1023) // 1024 * 8
        part = r_pack[:, off:off + rows].reshape(N_DEV, -1)[:, :a.size]
        off += rows
        shape2 = (1, 1, a.size) if a.size % BLK else (1, a.size // BLK, BLK)
        store(n, update(n, part, 0, shape2, None))

    order = lambda d: [d[n] for n in names]
    return (loss, grad_x[None], *order(out_g), *order(out_d), *order(out_m), *order(out_v))
```

```python
import functools

import jax
import jax.numpy as jnp
from jax import lax
from jax.experimental import pallas as pl
from jax.experimental.pallas import tpu as pltpu

F32 = jnp.float32
BF16 = jnp.bfloat16

D = 2048
DEPTH = 2
N_META = 16
BLK = 128
PAD = BLK - N_META
EPS = 1e-6
NEG = -1e30

FOX_H, FOX_DH = 8, 128
FOX_W = FOX_H * FOX_DH
CONV_CH = 1024
GLA_H, GLA_DK, GLA_DV, GLA_R, GLA_TAU = 4, 128, 256, 16, 16.0
D_FF = 5632
N_IN = 15384
N_DEV = 8

ADAM_LR, ADAM_B1, ADAM_B2, ADAM_EPS, ADAM_WD, ADAM_STEP = 0.001, 0.9, 0.999, 1e-08, 0.01, 10

CONV_TC = 512
GATE_TN = 512
KV0 = 1024
REST0 = 3072
GLA_GRP = 768
SMALL_W = 1024
NP = 16384
NREST = NP - REST0
R_CONV_BLK0 = 0
R_GLA_BLK0 = (6144 - REST0) // GLA_GRP
R_GATE_BLK0 = (9216 - REST0) // (3 * GATE_TN)
R_SMALL_BLK128 = (15360 - REST0) // 128
F_CONV_BLK0 = 3072 // (3 * CONV_TC)
F_GLA_BLK0 = 6144 // GLA_GRP
F_GATE_BLK0 = 9216 // (3 * GATE_TN)
F_SMALL_BLK0 = 15360 // SMALL_W

VMEM_LIMIT = 56 * 1024 * 1024
ADAMW_BLOCK_ELEMS = 128 * 1024


def _segments():
    seg = [(0, 1024)]
    for h in range(FOX_H):
        seg += [(1024 + 128 * h, 128), (2048 + 128 * h, 128)]
    for j in range(CONV_CH // CONV_TC):
        seg += [(3080 + CONV_TC * j, CONV_TC), (4104 + CONV_TC * j, CONV_TC), (5128 + CONV_TC * j, CONV_TC)]
    for h in range(GLA_H):
        seg += [(6152 + 128 * h, 128), (6664 + 128 * h, 128), (7176 + 256 * h, 256), (8200 + 256 * h, 256)]
    for j in range(D // GATE_TN):
        seg += [(9240 + GATE_TN * j, GATE_TN), (11288 + GATE_TN * j, GATE_TN), (13336 + GATE_TN * j, GATE_TN)]
    seg += [(3072, 8), (9224, 16)]
    return seg


def _cparams(*sem):
    return pltpu.CompilerParams(dimension_semantics=sem, vmem_limit_bytes=VMEM_LIMIT)


def _row_tile(n, target):
    best = BLK
    t = BLK
    while t <= min(n, target):
        if n % t == 0:
            best = t
        t += BLK
    return best


def _sigmoid(x):
    return 1.0 / (1.0 + jnp.exp(-x))


def _log_sigmoid(x):
    return jnp.minimum(x, 0.0) - jnp.log(1.0 + jnp.exp(-jnp.abs(x)))


def _valid_rows(row0, n):
    return (row0 + lax.broadcasted_iota(jnp.int32, (n, 1), 0)) >= PAD


def _dot(a, b):
    return jnp.dot(a, b, preferred_element_type=F32)


def _dot_nt(a, b):
    return lax.dot_general(a, b, (((1,), (1,)), ((), ())), preferred_element_type=F32)


def _dot_tn(a, b):
    return lax.dot_general(a, b, (((0,), (0,)), ((), ())), preferred_element_type=F32)


def _exchange_copies(ins, outs, bcast, send_sems, recv_sems, local_sems):
    x, y, c = lax.axis_index("x"), lax.axis_index("y"), lax.axis_index("c")
    me = 4 * x + 2 * y + c

    def src_for(n, dev):
        if bcast[n] is True or bcast[n] == "wide":
            return ins[n]
        if bcast[n] == "cols":
            w = ins[n].shape[1] // N_DEV
            return ins[n].at[:, pl.ds(pl.multiple_of(dev * w, BLK), w)]
        return ins[n].at[dev]

    def dst_of(n, dev):
        if bcast[n] == "wide":
            w = ins[n].shape[1]
            return outs[n].at[:, pl.ds(pl.multiple_of(dev * w, BLK), w)]
        return outs[n].at[dev]

    local, sends, recvs = [], [], []
    for n in range(len(ins)):
        local.append(pltpu.make_async_copy(src_for(n, me), dst_of(n, me), local_sems.at[n]))
    for k in range(1, N_DEV):
        px = 1 - x if (k >> 2) & 1 else x
        py = 1 - y if (k >> 1) & 1 else y
        pc = 1 - c if k & 1 else c
        peer = 4 * px + 2 * py + pc
        for n in range(len(ins)):
            def copy(dst_dev, n=n, k=k, to=(px, py, pc), peer=peer):
                return pltpu.make_async_remote_copy(
                    src_ref=src_for(n, peer), dst_ref=dst_of(n, dst_dev), send_sem=send_sems.at[n, k - 1],
                    recv_sem=recv_sems.at[n, k - 1], device_id=to, device_id_type=pl.DeviceIdType.MESH)

            sends.append(copy(me))
            recvs.append(copy(peer))
    return local, sends, recvs


def _exchange_start(copies):
    local, sends, _ = copies
    for cp in local + sends:
        cp.start()


def _exchange_wait(copies):
    local, sends, recvs = copies
    for cp in recvs:
        cp.wait_recv()
    for cp in sends:
        cp.wait_send()
    for cp in local:
        cp.wait()


def _exchange_shapes(arrays, bcast):
    def shape(a, b):
        if b is True:
            return (N_DEV,) + a.shape
        if b == "wide":
            return (a.shape[0], N_DEV * a.shape[1])
        if b == "cols":
            return (N_DEV, a.shape[0], a.shape[1] // N_DEV)
        return a.shape

    return [jax.ShapeDtypeStruct(shape(a, b), a.dtype) for a, b in zip(arrays, bcast)]


def _exchange_sems(n_arr):
    return [pltpu.SemaphoreType.DMA((n_arr, N_DEV - 1)), pltpu.SemaphoreType.DMA((n_arr, N_DEV - 1)),
            pltpu.SemaphoreType.DMA((n_arr,))]


def _exchange(arrays, bcast, name):
    n_arr = len(arrays)

    def body(*refs):
        copies = _exchange_copies(refs[:n_arr], refs[n_arr:2 * n_arr], bcast, *refs[2 * n_arr:])
        _exchange_start(copies)
        _exchange_wait(copies)

    hbm = pl.BlockSpec(memory_space=pltpu.HBM)
    return pl.pallas_call(
        body, out_shape=_exchange_shapes(arrays, bcast), in_specs=[hbm] * n_arr, out_specs=[hbm] * n_arr,
        scratch_shapes=_exchange_sems(n_arr),
        compiler_params=pltpu.CompilerParams(has_side_effects=True), name=name)(*arrays)


def _gather_by_chip(block, name):
    def body(x_ref, out_ref, send_sems, recv_sems, local_sem):
        x, y, c = lax.axis_index("x"), lax.axis_index("y"), lax.axis_index("c")
        me, sibling = (x, y, c), (x, y, 1 - c)
        chips = [(1 - x, y), (x, 1 - y), (1 - x, 1 - y)]

        def slot(px, py, pc):
            return out_ref.at[4 * px + 2 * py + pc]

        def copy(k, block_of, to, src=None):
            return pltpu.make_async_remote_copy(
                src_ref=slot(*block_of) if src is None else src, dst_ref=slot(*block_of),
                send_sem=send_sems.at[k], recv_sem=recv_sems.at[k], device_id=to,
                device_id_type=pl.DeviceIdType.MESH)

        mine = pltpu.make_async_copy(x_ref, slot(*me), local_sem)
        mine.start()
        first = [copy(0, me, sibling, src=x_ref)]
        first += [copy(1 + j, me, (*chip, c), src=x_ref) for j, chip in enumerate(chips)]
        for cp in first:
            cp.start()
        passed = [copy(4 + j, (*chip, c), sibling) for j, chip in enumerate(chips)]
        for j, chip in enumerate(chips):
            copy(1 + j, (*chip, c), me).wait_recv()
            passed[j].start()
        copy(0, sibling, me).wait_recv()
        for j, chip in enumerate(chips):
            copy(4 + j, (*chip, 1 - c), me).wait_recv()
        for cp in first + passed:
            cp.wait_send()
        mine.wait()

    hbm = pl.BlockSpec(memory_space=pltpu.HBM)
    return pl.pallas_call(
        body, out_shape=jax.ShapeDtypeStruct((N_DEV,) + block.shape, block.dtype), in_specs=[hbm], out_specs=hbm,
        scratch_shapes=[pltpu.SemaphoreType.DMA((N_DEV - 1,)), pltpu.SemaphoreType.DMA((N_DEV - 1,)),
                        pltpu.SemaphoreType.DMA],
        compiler_params=pltpu.CompilerParams(has_side_effects=True), name=name)(block)


def _pcall(kern, *, grid, in_specs, out_specs, out_shape, scratch, sem, name, args, aliases=None, exch=None):
    params = pltpu.CompilerParams(dimension_semantics=sem, vmem_limit_bytes=VMEM_LIMIT,
                                  has_side_effects=exch is not None)
    kw = dict(grid=grid, compiler_params=params, name=name, input_output_aliases=aliases or {})
    if exch is None:
        out = pl.pallas_call(kern, in_specs=in_specs, out_specs=out_specs, out_shape=out_shape,
                             scratch_shapes=scratch, **kw)(*args)
        return out, None
    arrays, bcast = exch
    n_x, n_in, n_out, n_sc = len(arrays), len(in_specs), len(out_specs), len(scratch)

    def hosted(*refs):
        ins, x_in = refs[:n_in], refs[n_in:n_in + n_x]
        outs, x_out = refs[n_in + n_x:n_in + n_x + n_out], refs[n_in + n_x + n_out:n_in + 2 * n_x + n_out]
        sc, sems = refs[n_in + 2 * n_x + n_out:n_in + 2 * n_x + n_out + n_sc], refs[n_in + 2 * n_x + n_out + n_sc:]
        ids = [pl.program_id(d) for d in range(len(grid))]
        first = functools.reduce(jnp.logical_and, [i == 0 for i in ids])
        last = functools.reduce(jnp.logical_and, [i == g - 1 for i, g in zip(ids, grid)])

        @pl.when(first)
        def _():
            _exchange_start(_exchange_copies(x_in, x_out, bcast, *sems))

        kern(*ins, *outs, *sc)

        @pl.when(last)
        def _():
            _exchange_wait(_exchange_copies(x_in, x_out, bcast, *sems))

    hbm = pl.BlockSpec(memory_space=pltpu.HBM)
    out = pl.pallas_call(
        hosted, in_specs=list(in_specs) + [hbm] * n_x, out_specs=list(out_specs) + [hbm] * n_x,
        out_shape=list(out_shape) + _exchange_shapes(arrays, bcast),
        scratch_shapes=list(scratch) + _exchange_sems(n_x), **kw)(*args, *arrays)
    return out[:n_out], out[n_out:]


def _mm_nn(a, b, *, n0=0, n=None, out_dtype=F32, res=None, tm=1664, tn=512, tk=None, name="mm_nn", exch=None):
    m, k = a.shape
    n = b.shape[1] - n0 if n is None else n
    tm = _row_tile(m, tm)
    tk = k if tk is None else tk
    nk = k // tk
    assert k % tk == 0 and n % tn == 0 and n0 % tn == 0
    nb0 = n0 // tn

    def kern(*refs):
        if res is None:
            a_ref, b_ref, o_ref, acc = refs
        else:
            a_ref, b_ref, r_ref, o_ref, acc = refs
        kk = pl.program_id(2)
        row0 = pl.program_id(0) * tm

        def finish(prod):
            if res is None:
                o_ref[...] = prod.astype(out_dtype)
            else:
                o_ref[...] = (r_ref[...] + jnp.where(_valid_rows(row0, tm), prod, 0.0)).astype(out_dtype)

        if nk == 1:
            finish(_dot(a_ref[...].astype(BF16), b_ref[...].astype(BF16)))
            return

        @pl.when(kk == 0)
        def _():
            acc[...] = jnp.zeros_like(acc)

        acc[...] += _dot(a_ref[...].astype(BF16), b_ref[...].astype(BF16))

        @pl.when(kk == nk - 1)
        def _():
            finish(acc[...])

    in_specs = [pl.BlockSpec((tm, tk), lambda i, j, kk: (i, kk)),
                pl.BlockSpec((tk, tn), lambda i, j, kk: (kk, nb0 + j))]
    args = [a, b]
    if res is not None:
        in_specs.append(pl.BlockSpec((tm, tn), lambda i, j, kk: (i, j)))
        args.append(res)
    out, got = _pcall(
        kern, grid=(m // tm, n // tn, nk), in_specs=in_specs,
        out_specs=[pl.BlockSpec((tm, tn), lambda i, j, kk: (i, j))],
        out_shape=[jax.ShapeDtypeStruct((m, n), out_dtype)],
        scratch=[pltpu.VMEM((tm, tn) if nk > 1 else (8, 128), F32)],
        sem=("parallel", "parallel", "arbitrary"), name=name, args=args, exch=exch)
    return out[0] if exch is None else (out[0], got)


def _mm_nt(a, b, *, k0=0, kw=None, out_dtype=F32, add=None, tm=640, tn=None, tk=2048, name="mm_nt", exch=None):
    m = a.shape[0]
    kw = a.shape[1] if kw is None else kw
    nn = b.shape[0]
    tm = _row_tile(m, tm)
    tn = min(nn, 2048) if tn is None else tn
    tk = min(tk, kw)
    assert kw % tk == 0 and k0 % tk == 0 and nn % tn == 0 and a.shape[1] == kw
    nk = kw // tk
    kb0 = k0 // tk

    def kern(*refs):
        if add is None:
            a_ref, b_ref, o_ref, acc = refs
        else:
            a_ref, b_ref, d_ref, o_ref, acc = refs
        kk = pl.program_id(2)

        def finish(prod):
            o_ref[...] = (prod if add is None else prod + d_ref[...]).astype(out_dtype)

        if nk == 1:
            finish(_dot_nt(a_ref[...].astype(BF16), b_ref[...].astype(BF16)))
            return

        @pl.when(kk == 0)
        def _():
            acc[...] = jnp.zeros_like(acc)

        acc[...] += _dot_nt(a_ref[...].astype(BF16), b_ref[...].astype(BF16))

        @pl.when(kk == nk - 1)
        def _():
            finish(acc[...])

    in_specs = [pl.BlockSpec((tm, tk), lambda i, j, kk: (i, kk)),
                pl.BlockSpec((tn, tk), lambda i, j, kk: (j, kb0 + kk))]
    args = [a, b]
    if add is not None:
        in_specs.append(pl.BlockSpec((tm, tn), lambda i, j, kk: (i, j)))
        args.append(add)
    out, got = _pcall(
        kern, grid=(m // tm, nn // tn, nk), in_specs=in_specs,
        out_specs=[pl.BlockSpec((tm, tn), lambda i, j, kk: (i, j))],
        out_shape=[jax.ShapeDtypeStruct((m, nn), out_dtype)],
        scratch=[pltpu.VMEM((tm, tn) if nk > 1 else (8, 128), F32)],
        sem=("parallel", "parallel", "arbitrary"), name=name, args=args, exch=exch)
    return out[0] if exch is None else (out[0], got)


def _mm_tn(a, b, *, out_dtype=BF16, tm=1664, tk=None, tn=None, name="mm_tn", exch=None):
    m, k = a.shape
    n = b.shape[1]
    tm = _row_tile(m, tm)
    tk = k if tk is None else tk
    if tn is None:
        tn = 1024 if n % 1024 == 0 else 512
    assert k % tk == 0 and n % tn == 0
    nm = m // tm

    def kern(a_ref, b_ref, o_ref, acc):
        mm = pl.program_id(2)

        @pl.when(mm == 0)
        def _():
            acc[...] = jnp.zeros_like(acc)

        acc[...] += _dot_tn(a_ref[...].astype(BF16), b_ref[...].astype(BF16))

        @pl.when(mm == nm - 1)
        def _():
            o_ref[...] = acc[...].astype(out_dtype)

    out, got = _pcall(
        kern, grid=(k // tk, n // tn, nm),
        in_specs=[pl.BlockSpec((tm, tk), lambda i, j, mm: (mm, i)),
                  pl.BlockSpec((tm, tn), lambda i, j, mm: (mm, j))],
        out_specs=[pl.BlockSpec((tk, tn), lambda i, j, mm: (i, j))],
        out_shape=[jax.ShapeDtypeStruct((k, n), out_dtype)],
        scratch=[pltpu.VMEM((tk, tn), F32)],
        sem=("parallel", "parallel", "arbitrary"), name=name, args=(a, b), exch=exch)
    return out[0] if exch is None else (out[0], got)


def _rmsnorm_fwd(h, g, tr=640):
    lp = h.shape[0]
    tr = _row_tile(lp, tr)

    def kern(h_ref, g_ref, o_ref):
        x = h_ref[...]
        r = lax.rsqrt(jnp.mean(x * x, axis=-1, keepdims=True) + EPS)
        o_ref[...] = (x * r * g_ref[...]).astype(BF16)

    return pl.pallas_call(
        kern, grid=(lp // tr,),
        in_specs=[pl.BlockSpec((tr, D), lambda i: (i, 0)), pl.BlockSpec((1, D), lambda i: (0, 0))],
        out_specs=pl.BlockSpec((tr, D), lambda i: (i, 0)),
        out_shape=jax.ShapeDtypeStruct((lp, D), BF16),
        compiler_params=_cparams("parallel"), name="rmsnorm_fwd")(h, g)


def _rmsnorm_bwd(h, g, dxn, dres, tr=640):
    lp = h.shape[0]
    tr = _row_tile(lp, tr)

    def kern(h_ref, g_ref, dxn_ref, dres_ref, dh_ref, dg_ref):
        i = pl.program_id(0)
        x = h_ref[...]
        r = lax.rsqrt(jnp.mean(x * x, axis=-1, keepdims=True) + EPS)
        xhat = x * r
        dy = jnp.where(_valid_rows(i * tr, tr), dxn_ref[...], 0.0)

        @pl.when(i == 0)
        def _():
            dg_ref[...] = jnp.zeros_like(dg_ref)

        dg_ref[...] += jnp.sum(dy * xhat, axis=0, keepdims=True)
        dxh = dy * g_ref[...]
        dh_ref[...] = dres_ref[...] + r * (dxh - xhat * jnp.mean(dxh * xhat, axis=-1, keepdims=True))

    row = pl.BlockSpec((tr, D), lambda i: (i, 0))
    vec = pl.BlockSpec((1, D), lambda i: (0, 0))
    return pl.pallas_call(
        kern, grid=(lp // tr,), in_specs=[row, vec, row, row], out_specs=[row, vec],
        out_shape=[jax.ShapeDtypeStruct((lp, D), F32), jax.ShapeDtypeStruct((1, D), F32)],
        compiler_params=_cparams("arbitrary"), name="rmsnorm_bwd")(h, g, dxn, dres)


def _shift_down(xe, k):
    return xe if k == 0 else pltpu.roll(xe, k, 0)


def _shift_up(xe, k):
    return xe if k == 0 else pltpu.roll(xe, xe.shape[0] - k, 0)


def _conv_ext(xe, w_ref):
    return w_ref[2:3, :] * xe + w_ref[1:2, :] * _shift_down(xe, 1) + w_ref[0:1, :] * _shift_down(xe, 2)


def _halo_specs(tr, width, col_of, nrows, rows_first, halo=8):
    r8 = tr // halo
    last8 = nrows // halo - 1
    if rows_first:
        prev = pl.BlockSpec((halo, width), lambda i, j: (jnp.maximum(i * r8 - 1, 0), col_of(j)))
        nxt = pl.BlockSpec((halo, width), lambda i, j: (jnp.minimum((i + 1) * r8, last8), col_of(j)))
    else:
        prev = pl.BlockSpec((halo, width), lambda j, i: (jnp.maximum(i * r8 - 1, 0), col_of(j)))
        nxt = pl.BlockSpec((halo, width), lambda j, i: (jnp.minimum((i + 1) * r8, last8), col_of(j)))
    return prev, nxt


def _convb_fwd(proj_r, conv_w, tr=640):
    lp = proj_r.shape[0]
    tr = _row_tile(lp, tr)
    tc = CONV_TC
    gw = 3 * tc

    def kern(g_ref, gp_ref, w_ref, o_ref):
        i = pl.program_id(0)
        g = g_ref[...]
        p = g[:, tc:2 * tc] * g[:, 2 * tc:]
        gp = gp_ref[...]
        pp = jnp.where(i > 0, gp[:, tc:2 * tc] * gp[:, 2 * tc:], 0.0)
        y = _conv_ext(jnp.concatenate([pp, p], axis=0), w_ref)[8:]
        o_ref[...] = (g[:, :tc] * y).astype(BF16)

    prev, _ = _halo_specs(tr, gw, lambda j: R_CONV_BLK0 + j, lp, True)
    return pl.pallas_call(
        kern, grid=(lp // tr, CONV_CH // tc),
        in_specs=[pl.BlockSpec((tr, gw), lambda i, j: (i, R_CONV_BLK0 + j)), prev,
                  pl.BlockSpec((3, tc), lambda i, j: (0, j))],
        out_specs=pl.BlockSpec((tr, tc), lambda i, j: (i, j)),
        out_shape=jax.ShapeDtypeStruct((lp, CONV_CH), BF16),
        compiler_params=_cparams("parallel", "parallel"), name="convb_fwd")(proj_r, proj_r, conv_w)


def _convb_bwd(proj_r, dzb, conv_w, dproj, tr=640):
    lp = proj_r.shape[0]
    tr = _row_tile(lp, tr)
    nr = lp // tr
    tc = CONV_TC
    gw = 3 * tc

    def kern(g_ref, gp_ref, gn_ref, dz_ref, dzn_ref, w_ref, dp_any, dg_ref, dw_ref):
        del dp_any
        i = pl.program_id(1)
        g = g_ref[...]
        b, c, hh = g[:, :tc], g[:, tc:2 * tc], g[:, 2 * tc:]
        p = c * hh
        gp = gp_ref[...]
        pp = jnp.where(i > 0, gp[:, tc:2 * tc] * gp[:, 2 * tc:], 0.0)
        pe = jnp.concatenate([pp, p], axis=0)
        s1 = _shift_down(pe, 1)[8:]
        s2 = _shift_down(pe, 2)[8:]
        y = w_ref[2:3, :] * p + w_ref[1:2, :] * s1 + w_ref[0:1, :] * s2
        dz = dz_ref[...]
        dy = dz * b
        dyn = jnp.where(i < nr - 1, dzn_ref[...] * gn_ref[...][:, :tc], 0.0)
        dye = jnp.concatenate([dy, dyn], axis=0)
        dp = (w_ref[2:3, :] * dy + w_ref[1:2, :] * _shift_up(dye, 1)[:tr]
              + w_ref[0:1, :] * _shift_up(dye, 2)[:tr])
        valid = _valid_rows(i * tr, tr)
        dg_ref[...] = jnp.where(valid, jnp.concatenate([dz * y, dp * hh, dp * c], axis=1), 0.0).astype(BF16)

        @pl.when(i == 0)
        def _():
            dw_ref[...] = jnp.zeros_like(dw_ref)

        dw_ref[0:1, :] += jnp.sum(dy * s2, axis=0, keepdims=True)
        dw_ref[1:2, :] += jnp.sum(dy * s1, axis=0, keepdims=True)
        dw_ref[2:3, :] += jnp.sum(dy * p, axis=0, keepdims=True)

    gprev, gnext = _halo_specs(tr, gw, lambda j: R_CONV_BLK0 + j, lp, False)
    _, dznext = _halo_specs(tr, tc, lambda j: j, lp, False)
    return pl.pallas_call(
        kern, grid=(CONV_CH // tc, nr),
        in_specs=[pl.BlockSpec((tr, gw), lambda j, i: (i, R_CONV_BLK0 + j)), gprev, gnext,
                  pl.BlockSpec((tr, tc), lambda j, i: (i, j)), dznext,
                  pl.BlockSpec((3, tc), lambda j, i: (0, j)),
                  pl.BlockSpec(memory_space=pl.ANY)],
        out_specs=[pl.BlockSpec((tr, gw), lambda j, i: (i, F_CONV_BLK0 + j)),
                   pl.BlockSpec((3, tc), lambda j, i: (0, j))],
        out_shape=[jax.ShapeDtypeStruct(dproj.shape, BF16), jax.ShapeDtypeStruct((3, CONV_CH), F32)],
        input_output_aliases={6: 0},
        compiler_params=_cparams("parallel", "arbitrary"), name="convb_bwd",
    )(proj_r, proj_r, proj_r, dzb, dzb, conv_w, dproj)


MLP_TC = 256
MLP_HALO = 16


def _mlp_gate_fwd(z, w, tr=640):
    lp = z.shape[0]
    tr = _row_tile(lp, tr)
    tc = 512
    nc = D_FF // tc

    def kern(zg_ref, zgp_ref, zu_ref, zup_ref, wg_ref, wu_ref, o_ref):
        i = pl.program_id(0)

        def ext(m_ref, p_ref):
            return jnp.concatenate([jnp.where(i > 0, p_ref[...].astype(F32), 0.0), m_ref[...].astype(F32)], axis=0)

        ug = _conv_ext(ext(zg_ref, zgp_ref), wg_ref)[MLP_HALO:]
        uu = _conv_ext(ext(zu_ref, zup_ref), wu_ref)[MLP_HALO:]
        o_ref[...] = (ug * _sigmoid(ug) * uu).astype(BF16)

    gprev, _ = _halo_specs(tr, tc, lambda j: j, lp, True, MLP_HALO)
    uprev, _ = _halo_specs(tr, tc, lambda j: nc + j, lp, True, MLP_HALO)
    return pl.pallas_call(
        kern, grid=(lp // tr, nc),
        in_specs=[pl.BlockSpec((tr, tc), lambda i, j: (i, j)), gprev,
                  pl.BlockSpec((tr, tc), lambda i, j: (i, nc + j)), uprev,
                  pl.BlockSpec((3, tc), lambda i, j: (0, j)),
                  pl.BlockSpec((3, tc), lambda i, j: (0, nc + j))],
        out_specs=pl.BlockSpec((tr, tc), lambda i, j: (i, j)),
        out_shape=jax.ShapeDtypeStruct((lp, D_FF), BF16),
        compiler_params=_cparams("parallel", "parallel"), name="mlp_gate_fwd")(z, z, z, z, w, w)


def _mlp_gate_bwd(z, da, w, tr=640, exch=None):
    lp = z.shape[0]
    tr = _row_tile(lp, tr)
    nr = lp // tr
    tc = MLP_TC
    nc = D_FF // tc

    def kern(zg_ref, zgp_ref, zgn_ref, zu_ref, zup_ref, zun_ref, da_ref, dan_ref, wg_ref, wu_ref,
             dzg_ref, dzu_ref, dwg_ref, dwu_ref):
        i = pl.program_id(1)
        first, last = i == 0, i == nr - 1

        hl = MLP_HALO

        def ext(m_ref, p_ref, n_ref):
            return jnp.concatenate([jnp.where(first, 0.0, p_ref[...].astype(F32)), m_ref[...].astype(F32),
                                    jnp.where(last, 0.0, n_ref[...].astype(F32))], axis=0)

        zge, zue = ext(zg_ref, zgp_ref, zgn_ref), ext(zu_ref, zup_ref, zun_ref)
        ug = _conv_ext(zge, wg_ref)[hl:]
        uu = _conv_ext(zue, wu_ref)[hl:]
        dae = jnp.concatenate([da_ref[...].astype(F32), jnp.where(last, 0.0, dan_ref[...].astype(F32))], axis=0)
        sg = _sigmoid(ug)
        dug = dae * uu * (sg * (1.0 + ug * (1.0 - sg)))
        duu = dae * (ug * sg)
        valid = _valid_rows(i * tr, tr)

        @pl.when(first)
        def _():
            dwg_ref[...] = jnp.zeros_like(dwg_ref)
            dwu_ref[...] = jnp.zeros_like(dwu_ref)

        for du, ze, w_ref, dz_ref, dw_ref in ((dug, zge, wg_ref, dzg_ref, dwg_ref),
                                              (duu, zue, wu_ref, dzu_ref, dwu_ref)):
            dz = (w_ref[2:3, :] * du + w_ref[1:2, :] * _shift_up(du, 1) + w_ref[0:1, :] * _shift_up(du, 2))[:tr]
            dz_ref[...] = jnp.where(valid, dz, 0.0).astype(BF16)
            dum = du[:tr]
            for kk in range(3):
                dw_ref[kk:kk + 1, :] += jnp.sum(dum * _shift_down(ze, 2 - kk)[hl:hl + tr], axis=0, keepdims=True)

    gprev, gnext = _halo_specs(tr, tc, lambda j: j, lp, False, MLP_HALO)
    uprev, unext = _halo_specs(tr, tc, lambda j: nc + j, lp, False, MLP_HALO)
    main = pl.BlockSpec((tr, tc), lambda j, i: (i, j))
    wspec = pl.BlockSpec((3, tc), lambda j, i: (0, j))
    out, got = _pcall(
        kern, grid=(nc, nr),
        in_specs=[main, gprev, gnext, pl.BlockSpec((tr, tc), lambda j, i: (i, nc + j)), uprev, unext,
                  main, gnext, wspec, pl.BlockSpec((3, tc), lambda j, i: (0, nc + j))],
        out_specs=[main, main, wspec, wspec],
        out_shape=[jax.ShapeDtypeStruct((lp, D_FF), BF16), jax.ShapeDtypeStruct((lp, D_FF), BF16),
                   jax.ShapeDtypeStruct((3, D_FF), F32), jax.ShapeDtypeStruct((3, D_FF), F32)],
        scratch=[], sem=("parallel", "arbitrary"), name="mlp_gate_bwd",
        args=(z, z, z, z, z, z, da, da, w, w), exch=exch)
    return (*out, got)


def _tri(n, lower):
    r = lax.broadcasted_iota(jnp.int32, (n, n), 0)
    c = lax.broadcasted_iota(jnp.int32, (n, n), 1)
    return jnp.where((c <= r) if lower else (c >= r), 1.0, 0.0).astype(F32)


def _dot_exact(a, b):
    return jnp.dot(a, b, preferred_element_type=F32, precision=lax.Precision.HIGHEST)


def _fox_gate_fwd(proj_r, bf128):
    lp = proj_r.shape[0]
    nb = lp // BLK

    def kern(s_ref, b_ref, c_ref):
        tri = _tri(BLK, True)

        def body(i, carry):
            rows = pl.ds(pl.multiple_of(i * BLK, BLK), BLK)
            lf = jnp.where(_valid_rows(i * BLK, BLK), _log_sigmoid(s_ref[rows, :] + b_ref[...]), 0.0)
            cs = _dot_exact(tri, lf) + carry
            c_ref[rows, :] = cs
            return cs[BLK - 1:BLK, :]

        lax.fori_loop(0, nb, body, jnp.zeros((1, BLK), F32))

    return pl.pallas_call(
        kern, grid=(1,),
        in_specs=[pl.BlockSpec((lp, BLK), lambda i: (0, R_SMALL_BLK128)), pl.BlockSpec((1, BLK), lambda i: (0, 0))],
        out_specs=pl.BlockSpec((lp, BLK), lambda i: (0, 0)),
        out_shape=jax.ShapeDtypeStruct((lp, BLK), F32),
        compiler_params=_cparams("arbitrary"), name="fox_gate_fwd")(proj_r, bf128)


def _fox_gate_bwd(proj_r, dc, bf128):
    lp = proj_r.shape[0]
    nb = lp // BLK

    def kern(s_ref, dc_ref, b_ref, dfa_ref, dbf_ref):
        tri = _tri(BLK, False)

        dbf_ref[...] = jnp.zeros_like(dbf_ref)

        def body(ii, run):
            i = nb - 1 - ii
            rows = pl.ds(pl.multiple_of(i * BLK, BLK), BLK)
            dcb = dc_ref[rows, :]
            suf = _dot_exact(tri, dcb) + run
            dfa = jnp.where(_valid_rows(i * BLK, BLK), suf * _sigmoid(-(s_ref[rows, :] + b_ref[...])), 0.0)
            dfa_ref[rows, :] = dfa
            dbf_ref[...] += jnp.sum(dfa, axis=0, keepdims=True)
            return run + jnp.sum(dcb, axis=0, keepdims=True)

        lax.fori_loop(0, nb, body, jnp.zeros((1, BLK), F32))

    return pl.pallas_call(
        kern, grid=(1,),
        in_specs=[pl.BlockSpec((lp, BLK), lambda i: (0, R_SMALL_BLK128)), pl.BlockSpec((lp, BLK), lambda i: (0, 0)),
                  pl.BlockSpec((1, BLK), lambda i: (0, 0))],
        out_specs=[pl.BlockSpec((lp, BLK), lambda i: (0, 0)), pl.BlockSpec((1, BLK), lambda i: (0, 0))],
        out_shape=[jax.ShapeDtypeStruct((lp, BLK), F32), jax.ShapeDtypeStruct((1, BLK), F32)],
        compiler_params=_cparams("arbitrary"), name="fox_gate_bwd")(proj_r, dc, bf128)


LOG2E = 1.4426950408889634
KEY_PAD_BIAS = 1e30


def _attn_logits2(q, k, ck, diag):
    t = _dot_nt(q, k) * (LOG2E * FOX_DH ** -0.5) - ck * LOG2E
    if diag:
        r = lax.broadcasted_iota(jnp.int32, t.shape, 0)
        c = lax.broadcasted_iota(jnp.int32, t.shape, 1)
        t = jnp.where(c <= r, t, NEG)
    return t


ATTN_HEADS = 2


def _head_cols(a):
    return (slice(a * FOX_DH, (a + 1) * FOX_DH), slice(2 * a * FOX_DH, (2 * a + 1) * FOX_DH),
            slice((2 * a + 1) * FOX_DH, (2 * a + 2) * FOX_DH))


def _on_blocks(i, j, step):
    pl.when(j < i)(functools.partial(step, False))
    pl.when(j == i)(functools.partial(step, True))


def _attn_fwd(proj_a, cq, ck, tq=640, exch=None):
    lp = proj_a.shape[0]
    tq = _row_tile(lp, tq)
    tk = tq
    nq = lp // tq

    def kern(q_ref, kv_ref, cq_ref, ck_ref, o_ref, lse_ref, m_sc, l_sc, acc):
        i, j = pl.program_id(1), pl.program_id(2)

        @pl.when(j == 0)
        def _():
            m_sc[...] = jnp.full_like(m_sc, -jnp.inf)
            l_sc[...] = jnp.zeros_like(l_sc)
            acc[...] = jnp.zeros_like(acc)

        def step(diag):
            heads = range(ATTN_HEADS)
            cols = [_head_cols(a) for a in heads]
            m_old = [m_sc[a] for a in heads]
            l_old = [l_sc[a] for a in heads]
            acc_old = [acc[:, cols[a][0]] for a in heads]
            cq2 = [cq_ref[a] * LOG2E for a in heads]
            t = [_attn_logits2(q_ref[:, cols[a][0]], kv_ref[:, cols[a][1]], ck_ref[a], diag) for a in heads]
            m_new = [jnp.maximum(m_old[a], jnp.max(t[a], axis=-1, keepdims=True) + cq2[a]) for a in heads]
            p = [jnp.exp2(t[a] + (cq2[a] - m_new[a])) for a in heads]
            alpha = [jnp.exp2(m_old[a] - m_new[a]) for a in heads]
            l_new = [alpha[a] * l_old[a] + jnp.sum(p[a], axis=-1, keepdims=True) for a in heads]
            acc_new = [alpha[a] * acc_old[a] + _dot(p[a].astype(BF16), kv_ref[:, cols[a][2]]) for a in heads]
            for a in heads:
                m_sc[a] = m_new[a]
                l_sc[a] = l_new[a]
                acc[:, cols[a][0]] = acc_new[a]

        _on_blocks(i, j, step)

        @pl.when(j == nq - 1)
        def _():
            valid = _valid_rows(i * tq, tq)
            for a in range(ATTN_HEADS):
                hq, _, _ = _head_cols(a)
                o_ref[:, hq] = jnp.where(valid, acc[:, hq] / l_sc[a], 0.0).astype(BF16)
                lse_ref[a] = m_sc[a] + jnp.log(l_sc[a]) * LOG2E

    hp = ATTN_HEADS
    out, got = _pcall(
        kern, grid=(FOX_H // hp, nq, nq),
        in_specs=[pl.BlockSpec((tq, hp * FOX_DH), lambda h, i, j: (i, h)),
                  pl.BlockSpec((tk, 2 * hp * FOX_DH), lambda h, i, j: (jnp.minimum(j, i), KV0 // (2 * hp * FOX_DH) + h)),
                  pl.BlockSpec((hp, tq, 1), lambda h, i, j: (h, i, 0)),
                  pl.BlockSpec((hp, 1, tk), lambda h, i, j: (h, 0, jnp.minimum(j, i)))],
        out_specs=[pl.BlockSpec((tq, hp * FOX_DH), lambda h, i, j: (i, h)),
                   pl.BlockSpec((hp, tq, 1), lambda h, i, j: (h, i, 0))],
        out_shape=[jax.ShapeDtypeStruct((lp, FOX_W), BF16), jax.ShapeDtypeStruct((FOX_H, lp, 1), F32)],
        scratch=[pltpu.VMEM((hp, tq, 1), F32), pltpu.VMEM((hp, tq, 1), F32), pltpu.VMEM((tq, hp * FOX_DH), F32)],
        sem=("parallel", "parallel", "arbitrary"), name="attn_fwd", args=(proj_a, proj_a, cq, ck), exch=exch)
    return out[0], out[1], got


def _attn_bwd(proj_a, do, o, lse, cq, ck, dproj, tq=640, exch=None):
    lp = proj_a.shape[0]
    tq = _row_tile(lp, tq)
    tk = tq
    nq = lp // tq
    hp = ATTN_HEADS
    wq = hp * FOX_DH

    def kern(q_ref, kv_ref, do_ref, o_ref, lse_ref, cq_ref, ck_ref, dp_any, dproj_ref, dcq_ref, dck_ref,
             dk_acc, dv_acc, dq_acc, dcq_acc, dq_stage, dkv_stage, sems):
        del dp_any
        h, j, i = pl.program_id(0), pl.program_id(1), pl.program_id(2)
        rows = pl.ds(pl.multiple_of(i * tq, tq), tq)

        @pl.when(i == 0)
        def _():
            dck_ref[...] = jnp.zeros_like(dck_ref)
            dk_acc[...] = jnp.zeros_like(dk_acc)
            dv_acc[...] = jnp.zeros_like(dv_acc)

        @pl.when(jnp.logical_and(i == 0, j == 0))
        def _():
            dq_acc[...] = jnp.zeros_like(dq_acc)
            dcq_acc[...] = jnp.zeros_like(dcq_acc)

        def step(diag):
            heads = range(hp)
            cols = [_head_cols(a) for a in heads]
            dck_old = [dck_ref[a] for a in heads]
            dcq_old = [dcq_acc[a, rows, :] for a in heads]
            dk_old = [dk_acc[:, cols[a][0]] for a in heads]
            dv_old = [dv_acc[:, cols[a][0]] for a in heads]
            dq_old = [dq_acc[rows, cols[a][0]] for a in heads]
            shift = [cq_ref[a] * LOG2E - lse_ref[a] for a in heads]
            do_h = [do_ref[:, cols[a][0]] for a in heads]
            delta = [jnp.sum(do_h[a].astype(F32) * o_ref[:, cols[a][0]].astype(F32), axis=-1, keepdims=True)
                     for a in heads]
            t = [_attn_logits2(q_ref[:, cols[a][0]], kv_ref[:, cols[a][1]], ck_ref[a], diag) for a in heads]
            dp = [_dot_nt(do_h[a], kv_ref[:, cols[a][2]]) for a in heads]
            p = [jnp.exp2(t[a] + shift[a]) for a in heads]
            ds = [p[a] * (dp[a] - delta[a]) for a in heads]
            dsb = [ds[a].astype(BF16) for a in heads]
            dv_new = [dv_old[a] + _dot_tn(p[a].astype(BF16), do_h[a]) for a in heads]
            dck_new = [dck_old[a] - jnp.sum(ds[a], axis=0, keepdims=True) for a in heads]
            dcq_new = [dcq_old[a] + jnp.sum(ds[a], axis=-1, keepdims=True) for a in heads]
            dk_new = [dk_old[a] + _dot_tn(dsb[a], q_ref[:, cols[a][0]]) for a in heads]
            dq_new = [dq_old[a] + _dot(dsb[a], kv_ref[:, cols[a][1]]) for a in heads]
            for a in heads:
                dck_ref[a] = dck_new[a]
                dcq_acc[a, rows, :] = dcq_new[a]
                dk_acc[:, cols[a][0]] = dk_new[a]
                dv_acc[:, cols[a][0]] = dv_new[a]
                dq_acc[rows, cols[a][0]] = dq_new[a]

        _on_blocks(i, j, step)

        @pl.when(i == nq - 1)
        def _():
            parts = []
            for a in range(hp):
                hq, _, _ = _head_cols(a)
                parts += [dk_acc[:, hq] * (FOX_DH ** -0.5), dv_acc[:, hq]]
            dkv_stage[...] = jnp.concatenate(parts, axis=1).astype(BF16)
            out = pltpu.make_async_copy(
                dkv_stage, dproj_ref.at[pl.ds(pl.multiple_of(j * tk, tk), tk),
                                        pl.ds(pl.multiple_of(KV0 + h * 2 * wq, 2 * wq), 2 * wq)], sems.at[0])
            out.start()
            out.wait()

        @pl.when(jnp.logical_and(i == nq - 1, j == nq - 1))
        def _():
            dq_stage[...] = (dq_acc[...] * (FOX_DH ** -0.5)).astype(BF16)
            out = pltpu.make_async_copy(dq_stage, dproj_ref.at[:, pl.ds(pl.multiple_of(h * wq, wq), wq)], sems.at[1])
            rowsums = pltpu.make_async_copy(dcq_acc, dcq_ref.at[pl.ds(h * hp, hp)], sems.at[2])
            out.start()
            rowsums.start()
            out.wait()
            rowsums.wait()

    qspec = pl.BlockSpec((tq, wq), lambda h, j, i: (jnp.maximum(i, j), h))
    col = pl.BlockSpec((hp, tq, 1), lambda h, j, i: (h, jnp.maximum(i, j), 0))
    kvspec = pl.BlockSpec((tk, 2 * wq), lambda h, j, i: (j, KV0 // (2 * wq) + h))
    rowspec = pl.BlockSpec((hp, 1, tk), lambda h, j, i: (h, 0, j))
    out, got = _pcall(
        kern, grid=(FOX_H // hp, nq, nq),
        in_specs=[qspec, kvspec, qspec, qspec, col, col, rowspec, pl.BlockSpec(memory_space=pl.ANY)],
        out_specs=[pl.BlockSpec(memory_space=pl.ANY), pl.BlockSpec(memory_space=pl.ANY), rowspec],
        out_shape=[jax.ShapeDtypeStruct(dproj.shape, BF16), jax.ShapeDtypeStruct((FOX_H, lp, 1), F32),
                   jax.ShapeDtypeStruct((FOX_H, 1, lp), F32)],
        scratch=[pltpu.VMEM((tk, wq), F32), pltpu.VMEM((tk, wq), F32), pltpu.VMEM((lp, wq), F32),
                 pltpu.VMEM((hp, lp, 1), F32), pltpu.VMEM((lp, wq), BF16), pltpu.VMEM((tk, 2 * wq), BF16),
                 pltpu.SemaphoreType.DMA((3,))],
        aliases={7: 0}, sem=("arbitrary", "arbitrary", "arbitrary"), name="attn_bwd",
        args=(proj_a, proj_a, do, o, lse, cq, ck, dproj), exch=exch)
    return out[0], out[1], out[2], got


def _gla_gate_fwd(proj_r, wg2p, bg, tr=640):
    lp = proj_r.shape[0]
    tr = _row_tile(lp, tr)
    w = GLA_H * GLA_DK

    def kern(s_ref, w_ref, b_ref, o_ref):
        zg = _dot(s_ref[...].astype(BF16), w_ref[...]) + b_ref[...]
        o_ref[...] = jnp.where(_valid_rows(pl.program_id(0) * tr, tr), _log_sigmoid(zg) * (1.0 / GLA_TAU), 0.0)

    return pl.pallas_call(
        kern, grid=(lp // tr,),
        in_specs=[pl.BlockSpec((tr, BLK), lambda i: (i, R_SMALL_BLK128)), pl.BlockSpec((BLK, w), lambda i: (0, 0)),
                  pl.BlockSpec((1, w), lambda i: (0, 0))],
        out_specs=pl.BlockSpec((tr, w), lambda i: (i, 0)),
        out_shape=jax.ShapeDtypeStruct((lp, w), F32),
        compiler_params=_cparams("parallel"), name="gla_gate_fwd")(proj_r, wg2p, bg)


def _gla_chunk(grp, g):
    q = grp[:, :GLA_DK] * (GLA_DK ** -0.5)
    k = grp[:, GLA_DK:2 * GLA_DK]
    v = grp[:, 2 * GLA_DK:2 * GLA_DK + GLA_DV]
    r = grp[:, 2 * GLA_DK + GLA_DV:]
    b = _dot_exact(_tri(BLK, True), g)
    bl = b[BLK - 1:BLK, :]
    eb = jnp.exp(b)
    enb = jnp.exp(-b)
    ebl = jnp.exp(bl - b)
    qe, ke, kd = q * eb, k * enb, k * ebl
    causal = lax.broadcasted_iota(jnp.int32, (BLK, BLK), 1) <= lax.broadcasted_iota(jnp.int32, (BLK, BLK), 0)
    att = jnp.where(causal, _dot_nt(qe.astype(BF16), ke.astype(BF16)), 0.0)
    return q, k, v, r, bl, eb, enb, ebl, qe, ke, kd, causal, att


def _gla_fwd(proj_r, logg, gn):
    lp = proj_r.shape[0]
    nc = lp // BLK
    wv = GLA_H * GLA_DV

    def kern(grp_ref, g_ref, gn_ref, o_ref, zc_ref, st_ref, st):
        c = pl.program_id(0)

        @pl.when(c == 0)
        def _():
            st[...] = jnp.zeros_like(st)

        for h in range(GLA_H):
            kcol = slice(h * GLA_DK, (h + 1) * GLA_DK)
            vcol = slice(h * GLA_DV, (h + 1) * GLA_DV)
            q, k, v, r, bl, eb, enb, ebl, qe, ke, kd, causal, att = _gla_chunk(
                grp_ref[:, h * GLA_GRP:(h + 1) * GLA_GRP], g_ref[:, kcol])
            s_t = st[h]
            st_ref[h] = s_t
            vb = v.astype(BF16)
            o = _dot(att.astype(BF16), vb) + _dot_nt(qe.astype(BF16), s_t.astype(BF16))
            st[h] = s_t * jnp.exp(bl) + _dot_tn(vb, kd.astype(BF16))
            o_ref[:, vcol] = o
            rstd = lax.rsqrt(jnp.mean(o * o, axis=-1, keepdims=True) + EPS)
            zc_ref[:, vcol] = (r * _sigmoid(r) * (o * rstd * gn_ref[:, vcol])).astype(BF16)

    vspec = pl.BlockSpec((BLK, wv), lambda c: (c, 0))
    return pl.pallas_call(
        kern, grid=(nc,),
        in_specs=[pl.BlockSpec((BLK, GLA_H * GLA_GRP), lambda c: (c, R_GLA_BLK0 // GLA_H)),
                  pl.BlockSpec((BLK, GLA_H * GLA_DK), lambda c: (c, 0)),
                  pl.BlockSpec((1, wv), lambda c: (0, 0))],
        out_specs=[vspec, vspec, pl.BlockSpec((GLA_H, None, GLA_DV, GLA_DK), lambda c: (0, c, 0, 0))],
        out_shape=[jax.ShapeDtypeStruct((lp, wv), F32), jax.ShapeDtypeStruct((lp, wv), BF16),
                   jax.ShapeDtypeStruct((GLA_H, nc, GLA_DV, GLA_DK), F32)],
        scratch_shapes=[pltpu.VMEM((GLA_H, GLA_DV, GLA_DK), F32)],
        compiler_params=_cparams("arbitrary"), name="gla_fwd")(proj_r, logg, gn)


def _gla_bwd(proj_r, logg, st_all, o_all, dzc, gn, dproj):
    lp = proj_r.shape[0]
    nc = lp // BLK

    def kern(grp_ref, g_ref, st_ref, o_ref, dzc_ref, gn_ref, dp_any, dgrp_ref, dlg_ref, dgn_ref, dst):
        del dp_any
        cc = pl.program_id(0)

        @pl.when(cc == 0)
        def _():
            dst[...] = jnp.zeros_like(dst)
            dgn_ref[...] = jnp.zeros_like(dgn_ref)

        for h in range(GLA_H):
            kcol = slice(h * GLA_DK, (h + 1) * GLA_DK)
            vcol = slice(h * GLA_DV, (h + 1) * GLA_DV)
            q, k, v, r, bl, eb, enb, ebl, qe, ke, kd, causal, att = _gla_chunk(
                grp_ref[:, h * GLA_GRP:(h + 1) * GLA_GRP], g_ref[:, kcol])
            s_t = st_ref[h]
            d_st = dst[h]
            o = o_ref[:, vcol]
            dzc_v = dzc_ref[:, vcol]
            gnv = gn_ref[:, vcol]
            rstd = lax.rsqrt(jnp.mean(o * o, axis=-1, keepdims=True) + EPS)
            xhat = o * rstd
            sr = _sigmoid(r)
            dr = dzc_v * (xhat * gnv) * (sr * (1.0 + r * (1.0 - sr)))
            docn = dzc_v * (r * sr)
            dgn_ref[:, vcol] += jnp.sum(docn * xhat, axis=0, keepdims=True)
            dxh = docn * gnv
            do = rstd * (dxh - xhat * jnp.mean(dxh * xhat, axis=-1, keepdims=True))
            dob, vb = do.astype(BF16), v.astype(BF16)
            qeb, keb, kdb = qe.astype(BF16), ke.astype(BF16), kd.astype(BF16)
            datt = jnp.where(causal, _dot_nt(dob, vb), 0.0).astype(BF16)
            dv = _dot_tn(att.astype(BF16), dob) + _dot_nt(kdb, d_st.astype(BF16))
            dqe = _dot(datt, keb) + _dot(dob, s_t.astype(BF16))
            dke = _dot_tn(datt, qeb)
            dkd = _dot(vb, d_st.astype(BF16))
            dq = dqe * eb * (GLA_DK ** -0.5)
            dk = dke * enb + dkd * ebl
            kd_dkd = dkd * kd
            db = dqe * qe - dke * ke - kd_dkd
            db_last = (jnp.sum(kd_dkd, axis=0, keepdims=True)
                       + jnp.exp(bl) * jnp.sum(s_t * d_st, axis=0, keepdims=True))
            dlg_ref[:, kcol] = _dot_exact(_tri(BLK, False), db) + db_last
            dst[h] = d_st * jnp.exp(bl) + _dot_tn(dob, qeb)
            dgrp_ref[:, h * GLA_GRP:(h + 1) * GLA_GRP] = jnp.concatenate([dq, dk, dv, dr], axis=1).astype(BF16)

    rev = lambda c: nc - 1 - c
    wv = GLA_H * GLA_DV
    vspec = pl.BlockSpec((BLK, wv), lambda c: (rev(c), 0))
    kspec = pl.BlockSpec((BLK, GLA_H * GLA_DK), lambda c: (rev(c), 0))
    return pl.pallas_call(
        kern, grid=(nc,),
        in_specs=[pl.BlockSpec((BLK, GLA_H * GLA_GRP), lambda c: (rev(c), R_GLA_BLK0 // GLA_H)), kspec,
                  pl.BlockSpec((GLA_H, None, GLA_DV, GLA_DK), lambda c: (0, rev(c), 0, 0)),
                  vspec, vspec, pl.BlockSpec((1, wv), lambda c: (0, 0)),
                  pl.BlockSpec(memory_space=pl.ANY)],
        out_specs=[pl.BlockSpec((BLK, GLA_H * GLA_GRP), lambda c: (rev(c), F_GLA_BLK0 // GLA_H)), kspec,
                   pl.BlockSpec((1, wv), lambda c: (0, 0))],
        out_shape=[jax.ShapeDtypeStruct(dproj.shape, BF16), jax.ShapeDtypeStruct((lp, GLA_H * GLA_DK), F32),
                   jax.ShapeDtypeStruct((1, wv), F32)],
        scratch_shapes=[pltpu.VMEM((GLA_H, GLA_DV, GLA_DK), F32)],
        input_output_aliases={6: 0},
        compiler_params=_cparams("arbitrary"), name="gla_bwd",
    )(proj_r, logg, st_all, o_all, dzc, gn, dproj)


def _small_bwd(proj_r, dlogg, wg2p, wg2pt, bg, dfa, dproj, tr=640):
    lp = proj_r.shape[0]
    tr = _row_tile(lp, tr)
    w = GLA_H * GLA_DK

    def kern(s_ref, dlg_ref, w_ref, wt_ref, b_ref, dfa_ref, dp_any, ds_ref, dbg_ref, dw_ref):
        del dp_any
        i = pl.program_id(0)
        sb = s_ref[...].astype(BF16)
        zg = _dot(sb, w_ref[...]) + b_ref[...]
        dzg = jnp.where(_valid_rows(i * tr, tr), dlg_ref[...] * (1.0 / GLA_TAU) * _sigmoid(-zg), 0.0)

        @pl.when(i == 0)
        def _():
            dbg_ref[...] = jnp.zeros_like(dbg_ref)
            dw_ref[...] = jnp.zeros_like(dw_ref)

        dbg_ref[...] += jnp.sum(dzg, axis=0, keepdims=True)
        dzb = dzg.astype(BF16)
        dw_ref[...] += _dot_tn(sb, dzb)
        dsm = _dot(dzb, wt_ref[...]) + dfa_ref[...]
        ds_ref[...] = jnp.concatenate([dsm, jnp.zeros((tr, SMALL_W - BLK), F32)], axis=1).astype(BF16)

    return pl.pallas_call(
        kern, grid=(lp // tr,),
        in_specs=[pl.BlockSpec((tr, BLK), lambda i: (i, R_SMALL_BLK128)), pl.BlockSpec((tr, w), lambda i: (i, 0)),
                  pl.BlockSpec((BLK, w), lambda i: (0, 0)), pl.BlockSpec((w, BLK), lambda i: (0, 0)),
                  pl.BlockSpec((1, w), lambda i: (0, 0)), pl.BlockSpec((tr, BLK), lambda i: (i, 0)),
                  pl.BlockSpec(memory_space=pl.ANY)],
        out_specs=[pl.BlockSpec((tr, SMALL_W), lambda i: (i, F_SMALL_BLK0)), pl.BlockSpec((1, w), lambda i: (0, 0)),
                   pl.BlockSpec((BLK, w), lambda i: (0, 0))],
        out_shape=[jax.ShapeDtypeStruct(dproj.shape, BF16), jax.ShapeDtypeStruct((1, w), F32),
                   jax.ShapeDtypeStruct((BLK, w), F32)],
        input_output_aliases={6: 0},
        compiler_params=_cparams("arbitrary"), name="small_bwd",
    )(proj_r, dlogg, wg2p, wg2pt, bg, dfa, dproj)


def _merge_fwd(proj_r, gate_b3, ya, yb, yc, tr=640):
    lp = proj_r.shape[0]
    tr = _row_tile(lp, tr)
    tn = GATE_TN

    def kern(g_ref, b_ref, ya_ref, yb_ref, yc_ref, o_ref):
        g = g_ref[...]
        mix = (_sigmoid(g[:, :tn] + b_ref[0:1, :]) * ya_ref[...]
               + _sigmoid(g[:, tn:2 * tn] + b_ref[1:2, :]) * yb_ref[...]
               + _sigmoid(g[:, 2 * tn:] + b_ref[2:3, :]) * yc_ref[...])
        o_ref[...] = mix.astype(BF16)

    y = pl.BlockSpec((tr, tn), lambda i, j: (i, j))
    return pl.pallas_call(
        kern, grid=(lp // tr, D // tn),
        in_specs=[pl.BlockSpec((tr, 3 * tn), lambda i, j: (i, R_GATE_BLK0 + j)),
                  pl.BlockSpec((3, tn), lambda i, j: (0, j)), y, y, y],
        out_specs=y, out_shape=jax.ShapeDtypeStruct((lp, D), BF16),
        compiler_params=_cparams("parallel", "parallel"), name="merge_fwd")(proj_r, gate_b3, ya, yb, yc)


def _merge_bwd(proj_r, gate_b3, ya, yb, yc, dmix, tr=640):
    lp = proj_r.shape[0]
    tr = _row_tile(lp, tr)
    tn = GATE_TN

    def kern(g_ref, b_ref, ya_ref, yb_ref, yc_ref, dm_ref, dya_ref, dyb_ref, dyc_ref, dg_ref, db_ref):
        i = pl.program_id(1)
        g = g_ref[...]
        dm = dm_ref[...]

        @pl.when(i == 0)
        def _():
            db_ref[...] = jnp.zeros_like(db_ref)

        dgs = []
        for n, (y_ref, dy_ref) in enumerate(((ya_ref, dya_ref), (yb_ref, dyb_ref), (yc_ref, dyc_ref))):
            s = _sigmoid(g[:, n * tn:(n + 1) * tn] + b_ref[n:n + 1, :])
            dy_ref[...] = (dm * s).astype(BF16)
            dgn = dm * y_ref[...] * (s * (1.0 - s))
            db_ref[n:n + 1, :] += jnp.sum(dgn, axis=0, keepdims=True)
            dgs.append(dgn)
        dg_ref[...] = jnp.concatenate(dgs, axis=1).astype(BF16)

    y = pl.BlockSpec((tr, tn), lambda j, i: (i, j))
    bspec = pl.BlockSpec((3, tn), lambda j, i: (0, j))
    return pl.pallas_call(
        kern, grid=(D // tn, lp // tr),
        in_specs=[pl.BlockSpec((tr, 3 * tn), lambda j, i: (i, R_GATE_BLK0 + j)), bspec, y, y, y, y],
        out_specs=[y, y, y, pl.BlockSpec((tr, 3 * tn), lambda j, i: (i, F_GATE_BLK0 + j)), bspec],
        out_shape=[jax.ShapeDtypeStruct((lp, D), BF16)] * 3
        + [jax.ShapeDtypeStruct((lp, NP), BF16), jax.ShapeDtypeStruct((3, D), F32)],
        compiler_params=_cparams("parallel", "arbitrary"), name="merge_bwd")(proj_r, gate_b3, ya, yb, yc, dmix)


def _final_loss(h, gf, tgt):
    lp = h.shape[0]
    nb = lp // BLK

    def kern(h_ref, g_ref, t_ref, dh_ref, dg_ref, ls_ref):
        i = pl.program_id(0)

        @pl.when(i == 0)
        def _():
            dh_ref[...] = jnp.zeros_like(dh_ref)
            dg_ref[...] = jnp.zeros_like(dg_ref)
            ls_ref[...] = jnp.zeros_like(ls_ref)

        @pl.when(i > 0)
        def _():
            x = h_ref[...]
            r = lax.rsqrt(jnp.mean(x * x, axis=-1, keepdims=True) + EPS)
            xhat = x * r
            err = xhat * g_ref[...] - t_ref[...]
            ls_ref[...] += jnp.sum(jnp.sum(err * err, axis=0, keepdims=True), axis=1, keepdims=True)
            dy = err * (1.0 / D)
            dg_ref[...] += jnp.sum(dy * xhat, axis=0, keepdims=True)
            dxh = dy * g_ref[...]
            dh_ref[...] = r * (dxh - xhat * jnp.mean(dxh * xhat, axis=-1, keepdims=True))

    row = pl.BlockSpec((BLK, D), lambda i: (i, 0))
    vec = pl.BlockSpec((1, D), lambda i: (0, 0))
    return pl.pallas_call(
        kern, grid=(nb,),
        in_specs=[row, vec, pl.BlockSpec((BLK, D), lambda i: (jnp.maximum(i - 1, 0), 0))],
        out_specs=[row, vec, pl.BlockSpec((1, 1), lambda i: (0, 0))],
        out_shape=[jax.ShapeDtypeStruct((lp, D), F32), jax.ShapeDtypeStruct((1, D), F32),
                   jax.ShapeDtypeStruct((1, 1), F32)],
        compiler_params=_cparams("arbitrary"), name="final_loss")(h, gf, tgt)


def _gate_cols(c):
    ct = c[:, :FOX_H].T
    ck = jnp.where(jnp.arange(ct.shape[1]) < PAD, KEY_PAD_BIAS, ct)
    return ct[:, :, None], ck[:, None, :]


def _run(hosts, name, ctx, fn):
    if hosts and name in hosts:
        make, done = hosts[name]
        res = fn(make(ctx))
        done(res[-1])
    else:
        res = fn(None)
    return res[:-1]


def _mm_nn_x(a, b, exch, **kw):
    out = _mm_nn(a, b, exch=exch, **kw)
    return out if exch is not None else (out, None)


def _layer_fwd(h, w, hosts=None):
    xn = _rmsnorm_fwd(h, w["norm1_g"])
    proj_a = _mm_nn(xn, w["w_in"], n0=0, n=REST0, out_dtype=BF16, name="proj_a")
    proj_r, = _run(hosts, "proj_r", w, lambda e: _mm_nn_x(xn, w["w_in"], e, n0=REST0, n=NREST, name="proj_r"))
    cq, ck = _gate_cols(_fox_gate_fwd(proj_r, w["bf128"]))
    oa, lse = _run(hosts, "attn_fwd", w, lambda e: _attn_fwd(proj_a, cq, ck, exch=e))
    zb = _convb_fwd(proj_r, w["conv_w"])
    logg = _gla_gate_fwd(proj_r, w["wg2p"], w["gla_b_g"])
    o_gla, zc, st_all = _gla_fwd(proj_r, logg, w["gla_norm_g"])
    ya = _mm_nn(oa, w["w_a_o"], name="branch_a")
    yb = _mm_nn(zb, w["w_b_o"], name="branch_b")
    yc = _mm_nn(zc, w["w_c_o"], name="branch_c")
    mix = _merge_fwd(proj_r, w["gate_b3"], ya, yb, yc)
    h1 = _mm_nn(mix, w["w_o"], res=h, name="out_proj")
    xn2 = _rmsnorm_fwd(h1, w["norm2_g"])
    z, = _run(hosts, "up_proj", w, lambda e: _mm_nn_x(xn2, w["w_up"], e, out_dtype=BF16, name="up_proj"))
    a = _mlp_gate_fwd(z, w["mlp_conv_w"])
    h2, = _run(hosts, "down_proj", w,
               lambda e: _mm_nn_x(a, w["w_down"], e, res=h1, tk=D_FF // 4, name="down_proj"))
    saved = dict(h=h, xn=xn, proj_a=proj_a, proj_r=proj_r, cq=cq, ck=ck, oa=oa, lse=lse, zb=zb, logg=logg,
                 o_gla=o_gla, zc=zc, st_all=st_all, ya=ya, yb=yb, yc=yc, mix=mix, h1=h1, xn2=xn2, z=z, a=a)
    return h2, saved


def _layer_bwd(dh2, w, s, hosts=None):
    g = {}
    da = _mm_nt(dh2, w["w_down"], tn=D_FF // 4, out_dtype=BF16, name="d_down_in")
    g["w_down"] = _mm_tn(s["a"], dh2, tk=D_FF // 4, name="d_w_down")
    dzg, dzu, dmw_g, dmw_u = _run(hosts, "mlp_gate_bwd", g, lambda e: _mlp_gate_bwd(
        s["z"], da, w["mlp_conv_w"], exch=e))
    g["mlp_conv_w"] = jnp.concatenate([dmw_g, dmw_u], axis=1)
    dxn2 = _mm_nt(dzg, w["w_up"], k0=0, kw=D_FF, tk=D_FF // 4, name="d_up_in_g")
    dxn2 = _mm_nt(dzu, w["w_up"], k0=D_FF, kw=D_FF, tk=D_FF // 4, add=dxn2, name="d_up_in_u")
    g["w_up"] = jnp.concatenate([_mm_tn(s["xn2"], dzg, name="d_w_up_g"), _mm_tn(s["xn2"], dzu, name="d_w_up_u")], axis=1)
    dh1, g["norm2_g"] = _rmsnorm_bwd(s["h1"], w["norm2_g"], dxn2, dh2)
    dmix = _mm_nt(dh1, w["w_o"], name="d_out_proj_in")
    g["w_o"] = _mm_tn(s["mix"], dh1, name="d_w_o")
    dya, dyb, dyc, dproj, g["gate_b3"] = _merge_bwd(s["proj_r"], w["gate_b3"], s["ya"], s["yb"], s["yc"], dmix)
    doa = _mm_nt(dya, w["w_a_o"], out_dtype=BF16, name="d_branch_a_in")
    g["w_a_o"] = _mm_tn(s["oa"], dya, name="d_w_a_o")
    dzb = _mm_nt(dyb, w["w_b_o"], name="d_branch_b_in")
    g["w_b_o"] = _mm_tn(s["zb"], dyb, name="d_w_b_o")
    dzc = _mm_nt(dyc, w["w_c_o"], name="d_branch_c_in")
    g["w_c_o"] = _mm_tn(s["zc"], dyc, name="d_w_c_o")
    dproj, dlogg, g["gla_norm_g"] = _gla_bwd(s["proj_r"], s["logg"], s["st_all"], s["o_gla"], dzc, w["gla_norm_g"], dproj)
    dproj, g["conv_w"] = _convb_bwd(s["proj_r"], dzb, w["conv_w"], dproj)
    dproj, dcq, dck = _run(hosts, "attn_bwd", g, lambda e: _attn_bwd(
        s["proj_a"], doa, s["oa"], s["lse"], s["cq"], s["ck"], dproj, exch=e))
    dc = jnp.pad((dcq[:, :, 0] + dck[:, 0, :]).T, ((0, 0), (0, BLK - FOX_H)))
    dfa, g["bf128"] = _fox_gate_bwd(s["proj_r"], dc, w["bf128"])
    dproj, g["gla_b_g"], g["wg2p"] = _small_bwd(s["proj_r"], dlogg, w["wg2p"], w["wg2p"].T, w["gla_b_g"], dfa, dproj)
    def pair(out, e):
        return out if e is not None else (out, None)

    g["w_in"], = _run(hosts, "d_w_in", g, lambda e: pair(_mm_tn(s["xn"], dproj, name="d_w_in", exch=e), e))
    dxn, = _run(hosts, "d_in_proj_in", g, lambda e: pair(_mm_nt(dproj, w["w_in"], name="d_in_proj_in", exch=e), e))
    dh0, g["norm1_g"] = _rmsnorm_bwd(s["h"], w["norm1_g"], dxn, dh1)
    return dh0, g


def _local_step(x, tgt, meta, final_g, layers, hosts_fwd=None, hosts_bwd=None):
    h = jnp.concatenate([jnp.zeros((PAD, D), F32), meta, x], axis=0)
    saved = []
    for l, w in enumerate(layers):
        h, s = _layer_fwd(h, w, hosts_fwd[l] if hosts_fwd else None)
        saved.append(s)
    dh, dgf, sq = _final_loss(h, final_g, tgt)
    grads = [None] * len(layers)
    for l in reversed(range(len(layers))):
        dh, grads[l] = _layer_bwd(dh, layers[l], saved[l], hosts_bwd[l](grads) if hosts_bwd else None)
    return sq[0, 0], dh[BLK:], dh[PAD:BLK], dgf, grads


def _w_in_to_kernel(w_nat):
    parts = [w_nat[:, s:s + n] for s, n in _segments()]
    parts.append(jnp.zeros((w_nat.shape[0], SMALL_W - 8 - GLA_R), w_nat.dtype))
    return jnp.concatenate(parts, axis=1)


def _w_in_from_kernel(w_k):
    pieces, off = [], 0
    for s, n in _segments():
        pieces.append((s, w_k[:, off:off + n]))
        off += n
    return jnp.concatenate([p for _, p in sorted(pieces, key=lambda t: t[0])], axis=1)


def _w_in_slots_to_kernel(got):
    per = got.shape[2]
    parts = []
    for s, n in _segments():
        while n > 0:
            d, lo = divmod(s, per)
            take = min(n, per - lo)
            parts.append(got[d, :, lo:lo + take])
            s, n = s + take, n - take
    parts.append(jnp.zeros((got.shape[1], SMALL_W - 8 - GLA_R), got.dtype))
    return jnp.concatenate(parts, axis=1)


def _w_in_kernel_to_slots(w_k):
    per = N_IN // N_DEV
    pieces, off = [], 0
    for s, n in _segments():
        pieces.append((s, n, off))
        off += n
    slots = []
    for d in range(N_DEV):
        lo, hi = d * per, (d + 1) * per
        parts = [w_k[:, off + max(s, lo) - s:off + min(s + n, hi) - s]
                 for s, n, off in sorted(pieces) if max(s, lo) < min(s + n, hi)]
        slots.append(jnp.concatenate(parts, axis=1))
    return jnp.stack(slots)


def _pad_rows_at(a, row0, nrows):
    return jnp.pad(a, ((row0, nrows - row0 - a.shape[0]), (0, 0)))


def _big_to_kernel(name, full):
    return _w_in_to_kernel(full) if name == "w_in" else full


def _layer_weights(big, conv_w, gla_w_g2, mlp_conv_w, norm1_g, fox_b_f, gate_b, gla_b_g, gla_norm_g, norm2_g):
    w = {n: _big_to_kernel(n, a) for n, a in big.items()}
    w.update(
        conv_w=conv_w, mlp_conv_w=mlp_conv_w,
        wg2p=_pad_rows_at(gla_w_g2, 8, BLK).astype(BF16),
        norm1_g=norm1_g[None], norm2_g=norm2_g[None], gla_b_g=gla_b_g[None], gla_norm_g=gla_norm_g[None],
        bf128=jnp.pad(fox_b_f, (0, BLK - FOX_H))[None], gate_b3=gate_b.reshape(3, D))
    return w


def _layer_grads_natural(g):
    return dict(
        w_in=_w_in_from_kernel(g["w_in"]), w_a_o=g["w_a_o"], w_b_o=g["w_b_o"], w_c_o=g["w_c_o"], w_o=g["w_o"],
        w_up=g["w_up"], w_down=g["w_down"], conv_w=g["conv_w"], mlp_conv_w=g["mlp_conv_w"],
        gla_w_g2=g["wg2p"][8:8 + GLA_R], norm1_g=g["norm1_g"][0], norm2_g=g["norm2_g"][0],
        gla_b_g=g["gla_b_g"][0], gla_norm_g=g["gla_norm_g"][0], fox_b_f=g["bf128"][0, :FOX_H],
        gate_b=g["gate_b3"].reshape(3 * D))


def _adamw(recv, w, m, v, layer, prev=None, name="adamw"):
    n_slot, r, c = recv.shape
    lyr = w.shape[0]
    tr = r
    for t in range(16, r, 16):
        if r % t == 0 and t * c <= ADAMW_BLOCK_ELEMS:
            tr = t
    if r * c <= ADAMW_BLOCK_ELEMS:
        tr = r
    bc1, bc2 = 1.0 - ADAM_B1 ** ADAM_STEP, 1.0 - ADAM_B2 ** ADAM_STEP

    def kern(*refs):
        r_ref, w_ref, m_ref, v_ref = refs[:4]
        g_out, d_out, m_out, v_out = refs[-4:]
        g = r_ref[0].astype(F32)
        for sidx in range(1, n_slot):
            g = g + r_ref[sidx].astype(F32)
        m_new = ADAM_B1 * m_ref[...] + (1.0 - ADAM_B1) * g
        v_new = ADAM_B2 * v_ref[...] + (1.0 - ADAM_B2) * (g * g)
        g_out[...] = g
        m_out[...] = m_new
        v_out[...] = v_new
        d_out[...] = -ADAM_LR * ((m_new / bc1) / (jnp.sqrt(v_new / bc2) + ADAM_EPS) + ADAM_WD * w_ref[...])

    lspec = pl.BlockSpec((None, tr, c), lambda i: (layer, i, 0))
    in_specs = [pl.BlockSpec((n_slot, tr, c), lambda i: (0, i, 0)), lspec, lspec, lspec]
    args = [recv, w, m, v]
    aliases = {}
    if prev is not None:
        in_specs += [pl.BlockSpec(memory_space=pl.ANY)] * 4
        args += list(prev)
        aliases = {4: 0, 5: 1, 6: 2, 7: 3}
    return pl.pallas_call(
        kern, grid=(r // tr,), in_specs=in_specs, out_specs=[lspec] * 4,
        out_shape=[jax.ShapeDtypeStruct((lyr, r, c), F32)] * 4, input_output_aliases=aliases,
        compiler_params=_cparams("parallel"), name=name)(*args)


_BIG = ("w_in", "w_a_o", "w_b_o", "w_c_o", "w_o", "w_up", "w_down")
_COL_SHARDED = ("w_in", "w_a_o", "w_b_o", "w_c_o", "w_up", "conv_w", "gla_w_g2", "mlp_conv_w")
_REPL = ("norm1_g", "fox_b_f", "gate_b", "gla_b_g", "gla_norm_g", "norm2_g")


def _cols_from_slots(a):
    return jnp.transpose(a, (1, 0, 2)).reshape(a.shape[1], N_DEV * a.shape[2])


def _cols_to_slots(a):
    r, c8 = a.shape
    return jnp.transpose(a.reshape(r, N_DEV, c8 // N_DEV), (1, 0, 2))


def _rows_to_slots(a):
    return a.reshape(N_DEV, a.shape[0] // N_DEV, a.shape[1])


def kernel(x, meta_tokens, norm1_g, w_in, fox_b_f, gate_b, conv_w, gla_w_g2, gla_b_g, gla_norm_g, w_a_o, w_b_o, w_c_o, w_o, norm2_g, w_up, mlp_conv_w, w_down, final_norm_g, loss_target, m_meta_tokens, m_norm1_g, m_w_in, m_fox_b_f, m_gate_b, m_conv_w, m_gla_w_g2, m_gla_b_g, m_gla_norm_g, m_w_a_o, m_w_b_o, m_w_c_o, m_w_o, m_norm2_g, m_w_up, m_mlp_conv_w, m_w_down, m_final_norm_g, v_meta_tokens, v_norm1_g, v_w_in, v_fox_b_f, v_gate_b, v_conv_w, v_gla_w_g2, v_gla_b_g, v_gla_norm_g, v_w_a_o, v_w_b_o, v_w_c_o, v_w_o, v_norm2_g, v_w_up, v_mlp_conv_w, v_w_down, v_final_norm_g):
    names = ("meta_tokens", "norm1_g", "w_in", "fox_b_f", "gate_b", "conv_w", "gla_w_g2", "gla_b_g", "gla_norm_g",
             "w_a_o", "w_b_o", "w_c_o", "w_o", "norm2_g", "w_up", "mlp_conv_w", "w_down", "final_norm_g")
    wts = dict(zip(names, (meta_tokens, norm1_g, w_in, fox_b_f, gate_b, conv_w, gla_w_g2, gla_b_g, gla_norm_g,
                           w_a_o, w_b_o, w_c_o, w_o, norm2_g, w_up, mlp_conv_w, w_down, final_norm_g)))
    mom = dict(zip(names, (m_meta_tokens, m_norm1_g, m_w_in, m_fox_b_f, m_gate_b, m_conv_w, m_gla_w_g2, m_gla_b_g,
                           m_gla_norm_g, m_w_a_o, m_w_b_o, m_w_c_o, m_w_o, m_norm2_g, m_w_up, m_mlp_conv_w, m_w_down,
                           m_final_norm_g)))
    var = dict(zip(names, (v_meta_tokens, v_norm1_g, v_w_in, v_fox_b_f, v_gate_b, v_conv_w, v_gla_w_g2, v_gla_b_g,
                           v_gla_norm_g, v_w_a_o, v_w_b_o, v_w_c_o, v_w_o, v_norm2_g, v_w_up, v_mlp_conv_w, v_w_down,
                           v_final_norm_g)))

    small = _exchange([conv_w, gla_w_g2, mlp_conv_w, meta_tokens], [True] * 4, "gather_small")
    conv_full = jnp.transpose(small[0], (1, 2, 0, 3)).reshape(DEPTH, 3, CONV_CH)
    g2_full = jnp.transpose(small[1], (1, 2, 0, 3)).reshape(DEPTH, GLA_R, GLA_H * GLA_DK)
    mconv_full = jnp.transpose(small[2], (1, 2, 0, 3)).reshape(DEPTH, 3, 2 * D_FF)
    meta_full = _cols_from_slots(small[3])
    layers = [_layer_weights({}, conv_full[l], g2_full[l], mconv_full[l], norm1_g[l], fox_b_f[l], gate_b[l],
                             gla_b_g[l], gla_norm_g[l], norm2_g[l]) for l in range(DEPTH)]

    wide = ("w_a_o", "w_b_o", "w_c_o", "w_up")

    def gather(l, which):
        def make(_):
            return [wts[n][l].astype(BF16) for n in which], [("wide" if n in wide else True) for n in which]

        def done(got):
            for n, a in zip(which, got):
                if n == "w_in":
                    layers[l][n] = _w_in_slots_to_kernel(a)
                else:
                    layers[l][n] = a if n in wide else a.reshape(-1, a.shape[-1])

        return make, done

    recv_big = [dict() for _ in range(DEPTH)]

    def scatter(l, which, grads_of):
        def make(ctx):
            g = grads_of(ctx)
            send = [_w_in_kernel_to_slots(g[n]) if n == "w_in" else g[n] if n in wide else _rows_to_slots(g[n])
                    for n in which]
            return send, [("cols" if n in wide else False) for n in which]

        def done(got):
            recv_big[l].update(zip(which, got))

        return make, done

    mixers = ("w_o", "w_a_o", "w_b_o", "w_c_o")
    layers[0]["w_in"] = _w_in_slots_to_kernel(_gather_by_chip(w_in[0].astype(BF16), "gather_w_in"))
    hosts_fwd = [
        {"proj_r": gather(0, mixers + ("w_down",)), "attn_fwd": gather(0, ("w_up",)),
         "up_proj": gather(1, ("w_in",)), "down_proj": gather(1, mixers)},
        {"attn_fwd": gather(1, ("w_up", "w_down"))}]
    def both(first, n_first, second):
        def make(ctx):
            (a1, b1), (a2, b2) = first[0](ctx), second[0](ctx)
            return a1 + a2, b1 + b2

        def done(got):
            first[1](got[:n_first])
            second[1](got[n_first:])

        return make, done

    hosts_bwd = [
        lambda grads: {"mlp_gate_bwd": scatter(1, ("w_in",), lambda _: grads[1]),
                       "attn_bwd": both(scatter(1, ("w_down",) + mixers, lambda _: grads[1]), 1 + len(mixers),
                                        scatter(0, ("w_down", "w_up"), lambda g: g)),
                       "d_w_in": both(scatter(0, mixers, lambda g: g), len(mixers),
                                      scatter(1, ("w_up",), lambda _: grads[1])),
                       "d_in_proj_in": scatter(0, ("w_in",), lambda g: g)},
        lambda grads: None]

    sq, grad_x, dmeta, dgf, grads_k = _local_step(x[0], loss_target[0], meta_full, final_norm_g[None], layers,
                                                  hosts_fwd, hosts_bwd)
    loss = lax.psum(sq * (0.5 / D), ("x", "y", "c"))
    grads = [_layer_grads_natural(g) for g in grads_k]

    out_g, out_d, out_m, out_v = {}, {}, {}, {}

    def update(name, recv, layer, lyr_shape, prev):
        w3, m3, v3 = (t[name].reshape(lyr_shape) for t in (wts, mom, var))
        return _adamw(recv.reshape((recv.shape[0],) + lyr_shape[1:]), w3, m3, v3, layer, prev, name="adamw_" + name)

    def store(name, res):
        shape = wts[name].shape
        out_g[name], out_d[name], out_m[name], out_v[name] = (t.reshape(shape) for t in res)

    for n in _BIG:
        res = None
        for l in range(DEPTH):
            res = update(n, recv_big[l][n], l, wts[n].shape, res)
        store(n, res)

    def stack_layers(name):
        return jnp.stack([grads[l][name] for l in range(DEPTH)])

    s_conv = jnp.transpose(stack_layers("conv_w").reshape(DEPTH, 3, N_DEV, -1), (2, 0, 1, 3))
    s_g2 = jnp.transpose(stack_layers("gla_w_g2").reshape(DEPTH, GLA_R, N_DEV, -1), (2, 0, 1, 3))
    s_mconv = jnp.transpose(stack_layers("mlp_conv_w").reshape(DEPTH, 3, N_DEV, -1), (2, 0, 1, 3))
    s_meta = _cols_to_slots(dmeta)
    repl = [stack_layers(n) for n in _REPL] + [dgf]
    pack = jnp.concatenate([jnp.pad(a.reshape(-1), (0, (-a.size) % 1024)) for a in repl]).reshape(-1, BLK)
    r_conv, r_g2, r_mconv, r_meta, r_pack = _exchange(
        [s_conv, s_g2, s_mconv, s_meta, pack], [False, False, False, False, True], "scatter_small")
    store("conv_w", update("conv_w", r_conv, 0, (1, DEPTH * 3, CONV_CH // N_DEV), None))
    store("gla_w_g2", update("gla_w_g2", r_g2, 0, (1, DEPTH * GLA_R, GLA_H * GLA_DK // N_DEV), None))
    store("mlp_conv_w", update("mlp_conv_w", r_mconv, 0, (1, DEPTH * 3, 2 * D_FF // N_DEV), None))
    store("meta_tokens", update("meta_tokens", r_meta, 0, (1, N_META, D // N_DEV), None))
    off = 0
    for n, a in zip(_REPL + ("final_norm_g",), repl):
        rows = (a.size + 1023) // 1024 * 8
        part = r_pack[:, off:off + rows].reshape(N_DEV, -1)[:, :a.size]
        off += rows
        shape2 = (1, 1, a.size) if a.size % BLK else (1, a.size // BLK, BLK)
        store(n, update(n, part, 0, shape2, None))

    order = lambda d: [d[n] for n in names]
    return (loss, grad_x[None], *order(out_g), *order(out_d), *order(out_m), *order(out_v))
```

```python
import functools

import jax
import jax.numpy as jnp
from jax import lax
from jax.experimental import pallas as pl
from jax.experimental.pallas import tpu as pltpu

F32 = jnp.float32
BF16 = jnp.bfloat16

D = 2048
DEPTH = 2
N_META = 16
BLK = 128
PAD = BLK - N_META
EPS = 1e-6
NEG = -1e30

FOX_H, FOX_DH = 8, 128
FOX_W = FOX_H * FOX_DH
CONV_CH = 1024
GLA_H, GLA_DK, GLA_DV, GLA_R, GLA_TAU = 4, 128, 256, 16, 16.0
D_FF = 5632
N_IN = 15384
N_DEV = 8

ADAM_LR, ADAM_B1, ADAM_B2, ADAM_EPS, ADAM_WD, ADAM_STEP = 0.001, 0.9, 0.999, 1e-08, 0.01, 10

CONV_TC = 512
GATE_TN = 512
KV0 = 1024
REST0 = 3072
GLA_GRP = 768
SMALL_W = 1024
NP = 16384
NREST = NP - REST0
R_CONV_BLK0 = 0
R_GLA_BLK0 = (6144 - REST0) // GLA_GRP
R_GATE_BLK0 = (9216 - REST0) // (3 * GATE_TN)
R_SMALL_BLK128 = (15360 - REST0) // 128
F_CONV_BLK0 = 3072 // (3 * CONV_TC)
F_GLA_BLK0 = 6144 // GLA_GRP
F_GATE_BLK0 = 9216 // (3 * GATE_TN)
F_SMALL_BLK0 = 15360 // SMALL_W

VMEM_LIMIT = 56 * 1024 * 1024
ADAMW_BLOCK_ELEMS = 128 * 1024


def _segments():
    seg = [(0, 1024)]
    for h in range(FOX_H):
        seg += [(1024 + 128 * h, 128), (2048 + 128 * h, 128)]
    for j in range(CONV_CH // CONV_TC):
        seg += [(3080 + CONV_TC * j, CONV_TC), (4104 + CONV_TC * j, CONV_TC), (5128 + CONV_TC * j, CONV_TC)]
    for h in range(GLA_H):
        seg += [(6152 + 128 * h, 128), (6664 + 128 * h, 128), (7176 + 256 * h, 256), (8200 + 256 * h, 256)]
    for j in range(D // GATE_TN):
        seg += [(9240 + GATE_TN * j, GATE_TN), (11288 + GATE_TN * j, GATE_TN), (13336 + GATE_TN * j, GATE_TN)]
    seg += [(3072, 8), (9224, 16)]
    return seg


def _cparams(*sem):
    return pltpu.CompilerParams(dimension_semantics=sem, vmem_limit_bytes=VMEM_LIMIT)


def _row_tile(n, target):
    best = BLK
    t = BLK
    while t <= min(n, target):
        if n % t == 0:
            best = t
        t += BLK
    return best


def _sigmoid(x):
    return 1.0 / (1.0 + jnp.exp(-x))


def _log_sigmoid(x):
    return jnp.minimum(x, 0.0) - jnp.log(1.0 + jnp.exp(-jnp.abs(x)))


def _valid_rows(row0, n):
    return (row0 + lax.broadcasted_iota(jnp.int32, (n, 1), 0)) >= PAD


def _dot(a, b):
    return jnp.dot(a, b, preferred_element_type=F32)


def _dot_nt(a, b):
    return lax.dot_general(a, b, (((1,), (1,)), ((), ())), preferred_element_type=F32)


def _dot_tn(a, b):
    return lax.dot_general(a, b, (((0,), (0,)), ((), ())), preferred_element_type=F32)


def _exchange_copies(ins, outs, bcast, send_sems, recv_sems, local_sems):
    x, y, c = lax.axis_index("x"), lax.axis_index("y"), lax.axis_index("c")
    me = 4 * x + 2 * y + c

    def src_for(n, dev):
        if bcast[n] is True or bcast[n] == "wide":
            return ins[n]
        if bcast[n] == "cols":
            w = ins[n].shape[1] // N_DEV
            return ins[n].at[:, pl.ds(pl.multiple_of(dev * w, BLK), w)]
        return ins[n].at[dev]

    def dst_of(n, dev):
        if bcast[n] == "wide":
            w = ins[n].shape[1]
            return outs[n].at[:, pl.ds(pl.multiple_of(dev * w, BLK), w)]
        return outs[n].at[dev]

    local, sends, recvs = [], [], []
    for n in range(len(ins)):
        local.append(pltpu.make_async_copy(src_for(n, me), dst_of(n, me), local_sems.at[n]))
    for k in range(1, N_DEV):
        px = 1 - x if (k >> 2) & 1 else x
        py = 1 - y if (k >> 1) & 1 else y
        pc = 1 - c if k & 1 else c
        peer = 4 * px + 2 * py + pc
        for n in range(len(ins)):
            def copy(dst_dev, n=n, k=k, to=(px, py, pc), peer=peer):
                return pltpu.make_async_remote_copy(
                    src_ref=src_for(n, peer), dst_ref=dst_of(n, dst_dev), send_sem=send_sems.at[n, k - 1],
                    recv_sem=recv_sems.at[n, k - 1], device_id=to, device_id_type=pl.DeviceIdType.MESH)

            sends.append(copy(me))
            recvs.append(copy(peer))
    return local, sends, recvs


def _exchange_start(copies):
    local, sends, _ = copies
    for cp in local + sends:
        cp.start()


def _exchange_wait(copies):
    local, sends, recvs = copies
    for cp in recvs:
        cp.wait_recv()
    for cp in sends:
        cp.wait_send()
    for cp in local:
        cp.wait()


def _exchange_shapes(arrays, bcast):
    def shape(a, b):
        if b is True:
            return (N_DEV,) + a.shape
        if b == "wide":
            return (a.shape[0], N_DEV * a.shape[1])
        if b == "cols":
            return (N_DEV, a.shape[0], a.shape[1] // N_DEV)
        return a.shape

    return [jax.ShapeDtypeStruct(shape(a, b), a.dtype) for a, b in zip(arrays, bcast)]


def _exchange_sems(n_arr):
    return [pltpu.SemaphoreType.DMA((n_arr, N_DEV - 1)), pltpu.SemaphoreType.DMA((n_arr, N_DEV - 1)),
            pltpu.SemaphoreType.DMA((n_arr,))]


def _exchange(arrays, bcast, name):
    n_arr = len(arrays)

    def body(*refs):
        copies = _exchange_copies(refs[:n_arr], refs[n_arr:2 * n_arr], bcast, *refs[2 * n_arr:])
        _exchange_start(copies)
        _exchange_wait(copies)

    hbm = pl.BlockSpec(memory_space=pltpu.HBM)
    return pl.pallas_call(
        body, out_shape=_exchange_shapes(arrays, bcast), in_specs=[hbm] * n_arr, out_specs=[hbm] * n_arr,
        scratch_shapes=_exchange_sems(n_arr),
        compiler_params=pltpu.CompilerParams(has_side_effects=True), name=name)(*arrays)


def _gather_by_chip(block, name):
    def body(x_ref, out_ref, send_sems, recv_sems, local_sem):
        x, y, c = lax.axis_index("x"), lax.axis_index("y"), lax.axis_index("c")
        me, sibling = (x, y, c), (x, y, 1 - c)
        chips = [(1 - x, y), (x, 1 - y), (1 - x, 1 - y)]

        def slot(px, py, pc):
            return out_ref.at[4 * px + 2 * py + pc]

        def copy(k, block_of, to, src=None):
            return pltpu.make_async_remote_copy(
                src_ref=slot(*block_of) if src is None else src, dst_ref=slot(*block_of),
                send_sem=send_sems.at[k], recv_sem=recv_sems.at[k], device_id=to,
                device_id_type=pl.DeviceIdType.MESH)

        mine = pltpu.make_async_copy(x_ref, slot(*me), local_sem)
        mine.start()
        first = [copy(0, me, sibling, src=x_ref)]
        first += [copy(1 + j, me, (*chip, c), src=x_ref) for j, chip in enumerate(chips)]
        for cp in first:
            cp.start()
        passed = [copy(4 + j, (*chip, c), sibling) for j, chip in enumerate(chips)]
        for j, chip in enumerate(chips):
            copy(1 + j, (*chip, c), me).wait_recv()
            passed[j].start()
        copy(0, sibling, me).wait_recv()
        for j, chip in enumerate(chips):
            copy(4 + j, (*chip, 1 - c), me).wait_recv()
        for cp in first + passed:
            cp.wait_send()
        mine.wait()

    hbm = pl.BlockSpec(memory_space=pltpu.HBM)
    return pl.pallas_call(
        body, out_shape=jax.ShapeDtypeStruct((N_DEV,) + block.shape, block.dtype), in_specs=[hbm], out_specs=hbm,
        scratch_shapes=[pltpu.SemaphoreType.DMA((N_DEV - 1,)), pltpu.SemaphoreType.DMA((N_DEV - 1,)),
                        pltpu.SemaphoreType.DMA],
        compiler_params=pltpu.CompilerParams(has_side_effects=True), name=name)(block)


def _pcall(kern, *, grid, in_specs, out_specs, out_shape, scratch, sem, name, args, aliases=None, exch=None):
    params = pltpu.CompilerParams(dimension_semantics=sem, vmem_limit_bytes=VMEM_LIMIT,
                                  has_side_effects=exch is not None)
    kw = dict(grid=grid, compiler_params=params, name=name, input_output_aliases=aliases or {})
    if exch is None:
        out = pl.pallas_call(kern, in_specs=in_specs, out_specs=out_specs, out_shape=out_shape,
                             scratch_shapes=scratch, **kw)(*args)
        return out, None
    arrays, bcast = exch
    n_x, n_in, n_out, n_sc = len(arrays), len(in_specs), len(out_specs), len(scratch)

    def hosted(*refs):
        ins, x_in = refs[:n_in], refs[n_in:n_in + n_x]
        outs, x_out = refs[n_in + n_x:n_in + n_x + n_out], refs[n_in + n_x + n_out:n_in + 2 * n_x + n_out]
        sc, sems = refs[n_in + 2 * n_x + n_out:n_in + 2 * n_x + n_out + n_sc], refs[n_in + 2 * n_x + n_out + n_sc:]
        ids = [pl.program_id(d) for d in range(len(grid))]
        first = functools.reduce(jnp.logical_and, [i == 0 for i in ids])
        last = functools.reduce(jnp.logical_and, [i == g - 1 for i, g in zip(ids, grid)])

        @pl.when(first)
        def _():
            _exchange_start(_exchange_copies(x_in, x_out, bcast, *sems))

        kern(*ins, *outs, *sc)

        @pl.when(last)
        def _():
            _exchange_wait(_exchange_copies(x_in, x_out, bcast, *sems))

    hbm = pl.BlockSpec(memory_space=pltpu.HBM)
    out = pl.pallas_call(
        hosted, in_specs=list(in_specs) + [hbm] * n_x, out_specs=list(out_specs) + [hbm] * n_x,
        out_shape=list(out_shape) + _exchange_shapes(arrays, bcast),
        scratch_shapes=list(scratch) + _exchange_sems(n_x), **kw)(*args, *arrays)
    return out[:n_out], out[n_out:]


def _mm_nn(a, b, *, n0=0, n=None, out_dtype=F32, res=None, tm=1664, tn=512, tk=None, name="mm_nn", exch=None):
    m, k = a.shape
    n = b.shape[1] - n0 if n is None else n
    tm = _row_tile(m, tm)
    tk = k if tk is None else tk
    nk = k // tk
    assert k % tk == 0 and n % tn == 0 and n0 % tn == 0
    nb0 = n0 // tn

    def kern(*refs):
        if res is None:
            a_ref, b_ref, o_ref, acc = refs
        else:
            a_ref, b_ref, r_ref, o_ref, acc = refs
        kk = pl.program_id(2)
        row0 = pl.program_id(0) * tm

        def finish(prod):
            if res is None:
                o_ref[...] = prod.astype(out_dtype)
            else:
                o_ref[...] = (r_ref[...] + jnp.where(_valid_rows(row0, tm), prod, 0.0)).astype(out_dtype)

        if nk == 1:
            finish(_dot(a_ref[...].astype(BF16), b_ref[...].astype(BF16)))
            return

        @pl.when(kk == 0)
        def _():
            acc[...] = jnp.zeros_like(acc)

        acc[...] += _dot(a_ref[...].astype(BF16), b_ref[...].astype(BF16))

        @pl.when(kk == nk - 1)
        def _():
            finish(acc[...])

    in_specs = [pl.BlockSpec((tm, tk), lambda i, j, kk: (i, kk)),
                pl.BlockSpec((tk, tn), lambda i, j, kk: (kk, nb0 + j))]
    args = [a, b]
    if res is not None:
        in_specs.append(pl.BlockSpec((tm, tn), lambda i, j, kk: (i, j)))
        args.append(res)
    out, got = _pcall(
        kern, grid=(m // tm, n // tn, nk), in_specs=in_specs,
        out_specs=[pl.BlockSpec((tm, tn), lambda i, j, kk: (i, j))],
        out_shape=[jax.ShapeDtypeStruct((m, n), out_dtype)],
        scratch=[pltpu.VMEM((tm, tn) if nk > 1 else (8, 128), F32)],
        sem=("parallel", "parallel", "arbitrary"), name=name, args=args, exch=exch)
    return out[0] if exch is None else (out[0], got)


def _mm_nt(a, b, *, k0=0, kw=None, out_dtype=F32, add=None, tm=640, tn=None, tk=2048, name="mm_nt", exch=None):
    m = a.shape[0]
    kw = a.shape[1] if kw is None else kw
    nn = b.shape[0]
    tm = _row_tile(m, tm)
    tn = min(nn, 2048) if tn is None else tn
    tk = min(tk, kw)
    assert kw % tk == 0 and k0 % tk == 0 and nn % tn == 0 and a.shape[1] == kw
    nk = kw // tk
    kb0 = k0 // tk

    def kern(*refs):
        if add is None:
            a_ref, b_ref, o_ref, acc = refs
        else:
            a_ref, b_ref, d_ref, o_ref, acc = refs
        kk = pl.program_id(2)

        def finish(prod):
            o_ref[...] = (prod if add is None else prod + d_ref[...]).astype(out_dtype)

        if nk == 1:
            finish(_dot_nt(a_ref[...].astype(BF16), b_ref[...].astype(BF16)))
            return

        @pl.when(kk == 0)
        def _():
            acc[...] = jnp.zeros_like(acc)

        acc[...] += _dot_nt(a_ref[...].astype(BF16), b_ref[...].astype(BF16))

        @pl.when(kk == nk - 1)
        def _():
            finish(acc[...])

    in_specs = [pl.BlockSpec((tm, tk), lambda i, j, kk: (i, kk)),
                pl.BlockSpec((tn, tk), lambda i, j, kk: (j, kb0 + kk))]
    args = [a, b]
    if add is not None:
        in_specs.append(pl.BlockSpec((tm, tn), lambda i, j, kk: (i, j)))
        args.append(add)
    out, got = _pcall(
        kern, grid=(m // tm, nn // tn, nk), in_specs=in_specs,
        out_specs=[pl.BlockSpec((tm, tn), lambda i, j, kk: (i, j))],
        out_shape=[jax.ShapeDtypeStruct((m, nn), out_dtype)],
        scratch=[pltpu.VMEM((tm, tn) if nk > 1 else (8, 128), F32)],
        sem=("parallel", "parallel", "arbitrary"), name=name, args=args, exch=exch)
    return out[0] if exch is None else (out[0], got)


def _mm_tn(a, b, *, out_dtype=BF16, tm=1664, tk=None, tn=None, name="mm_tn", exch=None):
    m, k = a.shape
    n = b.shape[1]
    tm = _row_tile(m, tm)
    tk = k if tk is None else tk
    if tn is None:
        tn = 1024 if n % 1024 == 0 else 512
    assert k % tk == 0 and n % tn == 0
    nm = m // tm

    def kern(a_ref, b_ref, o_ref, acc):
        mm = pl.program_id(2)

        @pl.when(mm == 0)
        def _():
            acc[...] = jnp.zeros_like(acc)

        acc[...] += _dot_tn(a_ref[...].astype(BF16), b_ref[...].astype(BF16))

        @pl.when(mm == nm - 1)
        def _():
            o_ref[...] = acc[...].astype(out_dtype)

    out, got = _pcall(
        kern, grid=(k // tk, n // tn, nm),
        in_specs=[pl.BlockSpec((tm, tk), lambda i, j, mm: (mm, i)),
                  pl.BlockSpec((tm, tn), lambda i, j, mm: (mm, j))],
        out_specs=[pl.BlockSpec((tk, tn), lambda i, j, mm: (i, j))],
        out_shape=[jax.ShapeDtypeStruct((k, n), out_dtype)],
        scratch=[pltpu.VMEM((tk, tn), F32)],
        sem=("parallel", "parallel", "arbitrary"), name=name, args=(a, b), exch=exch)
    return out[0] if exch is None else (out[0], got)


def _rmsnorm_fwd(h, g, tr=640):
    lp = h.shape[0]
    tr = _row_tile(lp, tr)

    def kern(h_ref, g_ref, o_ref):
        x = h_ref[...]
        r = lax.rsqrt(jnp.mean(x * x, axis=-1, keepdims=True) + EPS)
        o_ref[...] = (x * r * g_ref[...]).astype(BF16)

    return pl.pallas_call(
        kern, grid=(lp // tr,),
        in_specs=[pl.BlockSpec((tr, D), lambda i: (i, 0)), pl.BlockSpec((1, D), lambda i: (0, 0))],
        out_specs=pl.BlockSpec((tr, D), lambda i: (i, 0)),
        out_shape=jax.ShapeDtypeStruct((lp, D), BF16),
        compiler_params=_cparams("parallel"), name="rmsnorm_fwd")(h, g)


def _rmsnorm_bwd(h, g, dxn, dres, tr=640):
    lp = h.shape[0]
    tr = _row_tile(lp, tr)

    def kern(h_ref, g_ref, dxn_ref, dres_ref, dh_ref, dg_ref):
        i = pl.program_id(0)
        x = h_ref[...]
        r = lax.rsqrt(jnp.mean(x * x, axis=-1, keepdims=True) + EPS)
        xhat = x * r
        dy = jnp.where(_valid_rows(i * tr, tr), dxn_ref[...], 0.0)

        @pl.when(i == 0)
        def _():
            dg_ref[...] = jnp.zeros_like(dg_ref)

        dg_ref[...] += jnp.sum(dy * xhat, axis=0, keepdims=True)
        dxh = dy * g_ref[...]
        dh_ref[...] = dres_ref[...] + r * (dxh - xhat * jnp.mean(dxh * xhat, axis=-1, keepdims=True))

    row = pl.BlockSpec((tr, D), lambda i: (i, 0))
    vec = pl.BlockSpec((1, D), lambda i: (0, 0))
    return pl.pallas_call(
        kern, grid=(lp // tr,), in_specs=[row, vec, row, row], out_specs=[row, vec],
        out_shape=[jax.ShapeDtypeStruct((lp, D), F32), jax.ShapeDtypeStruct((1, D), F32)],
        compiler_params=_cparams("arbitrary"), name="rmsnorm_bwd")(h, g, dxn, dres)


def _shift_down(xe, k):
    return xe if k == 0 else pltpu.roll(xe, k, 0)


def _shift_up(xe, k):
    return xe if k == 0 else pltpu.roll(xe, xe.shape[0] - k, 0)


def _conv_ext(xe, w_ref):
    return w_ref[2:3, :] * xe + w_ref[1:2, :] * _shift_down(xe, 1) + w_ref[0:1, :] * _shift_down(xe, 2)


def _halo_specs(tr, width, col_of, nrows, rows_first, halo=8):
    r8 = tr // halo
    last8 = nrows // halo - 1
    if rows_first:
        prev = pl.BlockSpec((halo, width), lambda i, j: (jnp.maximum(i * r8 - 1, 0), col_of(j)))
        nxt = pl.BlockSpec((halo, width), lambda i, j: (jnp.minimum((i + 1) * r8, last8), col_of(j)))
    else:
        prev = pl.BlockSpec((halo, width), lambda j, i: (jnp.maximum(i * r8 - 1, 0), col_of(j)))
        nxt = pl.BlockSpec((halo, width), lambda j, i: (jnp.minimum((i + 1) * r8, last8), col_of(j)))
    return prev, nxt


def _convb_fwd(proj_r, conv_w, tr=640):
    lp = proj_r.shape[0]
    tr = _row_tile(lp, tr)
    tc = CONV_TC
    gw = 3 * tc

    def kern(g_ref, gp_ref, w_ref, o_ref):
        i = pl.program_id(0)
        g = g_ref[...]
        p = g[:, tc:2 * tc] * g[:, 2 * tc:]
        gp = gp_ref[...]
        pp = jnp.where(i > 0, gp[:, tc:2 * tc] * gp[:, 2 * tc:], 0.0)
        y = _conv_ext(jnp.concatenate([pp, p], axis=0), w_ref)[8:]
        o_ref[...] = (g[:, :tc] * y).astype(BF16)

    prev, _ = _halo_specs(tr, gw, lambda j: R_CONV_BLK0 + j, lp, True)
    return pl.pallas_call(
        kern, grid=(lp // tr, CONV_CH // tc),
        in_specs=[pl.BlockSpec((tr, gw), lambda i, j: (i, R_CONV_BLK0 + j)), prev,
                  pl.BlockSpec((3, tc), lambda i, j: (0, j))],
        out_specs=pl.BlockSpec((tr, tc), lambda i, j: (i, j)),
        out_shape=jax.ShapeDtypeStruct((lp, CONV_CH), BF16),
        compiler_params=_cparams("parallel", "parallel"), name="convb_fwd")(proj_r, proj_r, conv_w)


def _convb_bwd(proj_r, dzb, conv_w, dproj, tr=640):
    lp = proj_r.shape[0]
    tr = _row_tile(lp, tr)
    nr = lp // tr
    tc = CONV_TC
    gw = 3 * tc

    def kern(g_ref, gp_ref, gn_ref, dz_ref, dzn_ref, w_ref, dp_any, dg_ref, dw_ref):
        del dp_any
        i = pl.program_id(1)
        g = g_ref[...]
        b, c, hh = g[:, :tc], g[:, tc:2 * tc], g[:, 2 * tc:]
        p = c * hh
        gp = gp_ref[...]
        pp = jnp.where(i > 0, gp[:, tc:2 * tc] * gp[:, 2 * tc:], 0.0)
        pe = jnp.concatenate([pp, p], axis=0)
        s1 = _shift_down(pe, 1)[8:]
        s2 = _shift_down(pe, 2)[8:]
        y = w_ref[2:3, :] * p + w_ref[1:2, :] * s1 + w_ref[0:1, :] * s2
        dz = dz_ref[...]
        dy = dz * b
        dyn = jnp.where(i < nr - 1, dzn_ref[...] * gn_ref[...][:, :tc], 0.0)
        dye = jnp.concatenate([dy, dyn], axis=0)
        dp = (w_ref[2:3, :] * dy + w_ref[1:2, :] * _shift_up(dye, 1)[:tr]
              + w_ref[0:1, :] * _shift_up(dye, 2)[:tr])
        valid = _valid_rows(i * tr, tr)
        dg_ref[...] = jnp.where(valid, jnp.concatenate([dz * y, dp * hh, dp * c], axis=1), 0.0).astype(BF16)

        @pl.when(i == 0)
        def _():
            dw_ref[...] = jnp.zeros_like(dw_ref)

        dw_ref[0:1, :] += jnp.sum(dy * s2, axis=0, keepdims=True)
        dw_ref[1:2, :] += jnp.sum(dy * s1, axis=0, keepdims=True)
        dw_ref[2:3, :] += jnp.sum(dy * p, axis=0, keepdims=True)

    gprev, gnext = _halo_specs(tr, gw, lambda j: R_CONV_BLK0 + j, lp, False)
    _, dznext = _halo_specs(tr, tc, lambda j: j, lp, False)
    return pl.pallas_call(
        kern, grid=(CONV_CH // tc, nr),
        in_specs=[pl.BlockSpec((tr, gw), lambda j, i: (i, R_CONV_BLK0 + j)), gprev, gnext,
                  pl.BlockSpec((tr, tc), lambda j, i: (i, j)), dznext,
                  pl.BlockSpec((3, tc), lambda j, i: (0, j)),
                  pl.BlockSpec(memory_space=pl.ANY)],
        out_specs=[pl.BlockSpec((tr, gw), lambda j, i: (i, F_CONV_BLK0 + j)),
                   pl.BlockSpec((3, tc), lambda j, i: (0, j))],
        out_shape=[jax.ShapeDtypeStruct(dproj.shape, BF16), jax.ShapeDtypeStruct((3, CONV_CH), F32)],
        input_output_aliases={6: 0},
        compiler_params=_cparams("parallel", "arbitrary"), name="convb_bwd",
    )(proj_r, proj_r, proj_r, dzb, dzb, conv_w, dproj)


MLP_TC = 256
MLP_HALO = 16


def _mlp_gate_fwd(z, w, tr=640):
    lp = z.shape[0]
    tr = _row_tile(lp, tr)
    tc = 512
    nc = D_FF // tc

    def kern(zg_ref, zgp_ref, zu_ref, zup_ref, wg_ref, wu_ref, o_ref):
        i = pl.program_id(0)

        def ext(m_ref, p_ref):
            return jnp.concatenate([jnp.where(i > 0, p_ref[...].astype(F32), 0.0), m_ref[...].astype(F32)], axis=0)

        ug = _conv_ext(ext(zg_ref, zgp_ref), wg_ref)[MLP_HALO:]
        uu = _conv_ext(ext(zu_ref, zup_ref), wu_ref)[MLP_HALO:]
        o_ref[...] = (ug * _sigmoid(ug) * uu).astype(BF16)

    gprev, _ = _halo_specs(tr, tc, lambda j: j, lp, True, MLP_HALO)
    uprev, _ = _halo_specs(tr, tc, lambda j: nc + j, lp, True, MLP_HALO)
    return pl.pallas_call(
        kern, grid=(lp // tr, nc),
        in_specs=[pl.BlockSpec((tr, tc), lambda i, j: (i, j)), gprev,
                  pl.BlockSpec((tr, tc), lambda i, j: (i, nc + j)), uprev,
                  pl.BlockSpec((3, tc), lambda i, j: (0, j)),
                  pl.BlockSpec((3, tc), lambda i, j: (0, nc + j))],
        out_specs=pl.BlockSpec((tr, tc), lambda i, j: (i, j)),
        out_shape=jax.ShapeDtypeStruct((lp, D_FF), BF16),
        compiler_params=_cparams("parallel", "parallel"), name="mlp_gate_fwd")(z, z, z, z, w, w)


def _mlp_gate_bwd(z, da, w, tr=640, exch=None):
    lp = z.shape[0]
    tr = _row_tile(lp, tr)
    nr = lp // tr
    tc = MLP_TC
    nc = D_FF // tc

    def kern(zg_ref, zgp_ref, zgn_ref, zu_ref, zup_ref, zun_ref, da_ref, dan_ref, wg_ref, wu_ref,
             dzg_ref, dzu_ref, dwg_ref, dwu_ref):
        i = pl.program_id(1)
        first, last = i == 0, i == nr - 1

        hl = MLP_HALO

        def ext(m_ref, p_ref, n_ref):
            return jnp.concatenate([jnp.where(first, 0.0, p_ref[...].astype(F32)), m_ref[...].astype(F32),
                                    jnp.where(last, 0.0, n_ref[...].astype(F32))], axis=0)

        zge, zue = ext(zg_ref, zgp_ref, zgn_ref), ext(zu_ref, zup_ref, zun_ref)
        ug = _conv_ext(zge, wg_ref)[hl:]
        uu = _conv_ext(zue, wu_ref)[hl:]
        dae = jnp.concatenate([da_ref[...].astype(F32), jnp.where(last, 0.0, dan_ref[...].astype(F32))], axis=0)
        sg = _sigmoid(ug)
        dug = dae * uu * (sg * (1.0 + ug * (1.0 - sg)))
        duu = dae * (ug * sg)
        valid = _valid_rows(i * tr, tr)

        @pl.when(first)
        def _():
            dwg_ref[...] = jnp.zeros_like(dwg_ref)
            dwu_ref[...] = jnp.zeros_like(dwu_ref)

        for du, ze, w_ref, dz_ref, dw_ref in ((dug, zge, wg_ref, dzg_ref, dwg_ref),
                                              (duu, zue, wu_ref, dzu_ref, dwu_ref)):
            dz = (w_ref[2:3, :] * du + w_ref[1:2, :] * _shift_up(du, 1) + w_ref[0:1, :] * _shift_up(du, 2))[:tr]
            dz_ref[...] = jnp.where(valid, dz, 0.0).astype(BF16)
            dum = du[:tr]
            for kk in range(3):
                dw_ref[kk:kk + 1, :] += jnp.sum(dum * _shift_down(ze, 2 - kk)[hl:hl + tr], axis=0, keepdims=True)

    gprev, gnext = _halo_specs(tr, tc, lambda j: j, lp, False, MLP_HALO)
    uprev, unext = _halo_specs(tr, tc, lambda j: nc + j, lp, False, MLP_HALO)
    main = pl.BlockSpec((tr, tc), lambda j, i: (i, j))
    wspec = pl.BlockSpec((3, tc), lambda j, i: (0, j))
    out, got = _pcall(
        kern, grid=(nc, nr),
        in_specs=[main, gprev, gnext, pl.BlockSpec((tr, tc), lambda j, i: (i, nc + j)), uprev, unext,
                  main, gnext, wspec, pl.BlockSpec((3, tc), lambda j, i: (0, nc + j))],
        out_specs=[main, main, wspec, wspec],
        out_shape=[jax.ShapeDtypeStruct((lp, D_FF), BF16), jax.ShapeDtypeStruct((lp, D_FF), BF16),
                   jax.ShapeDtypeStruct((3, D_FF), F32), jax.ShapeDtypeStruct((3, D_FF), F32)],
        scratch=[], sem=("parallel", "arbitrary"), name="mlp_gate_bwd",
        args=(z, z, z, z, z, z, da, da, w, w), exch=exch)
    return (*out, got)


def _tri(n, lower):
    r = lax.broadcasted_iota(jnp.int32, (n, n), 0)
    c = lax.broadcasted_iota(jnp.int32, (n, n), 1)
    return jnp.where((c <= r) if lower else (c >= r), 1.0, 0.0).astype(F32)


def _dot_exact(a, b):
    return jnp.dot(a, b, preferred_element_type=F32, precision=lax.Precision.HIGHEST)


def _fox_gate_fwd(proj_r, bf128):
    lp = proj_r.shape[0]
    nb = lp // BLK

    def kern(s_ref, b_ref, c_ref):
        tri = _tri(BLK, True)

        def body(i, carry):
            rows = pl.ds(pl.multiple_of(i * BLK, BLK), BLK)
            lf = jnp.where(_valid_rows(i * BLK, BLK), _log_sigmoid(s_ref[rows, :] + b_ref[...]), 0.0)
            cs = _dot_exact(tri, lf) + carry
            c_ref[rows, :] = cs
            return cs[BLK - 1:BLK, :]

        lax.fori_loop(0, nb, body, jnp.zeros((1, BLK), F32))

    return pl.pallas_call(
        kern, grid=(1,),
        in_specs=[pl.BlockSpec((lp, BLK), lambda i: (0, R_SMALL_BLK128)), pl.BlockSpec((1, BLK), lambda i: (0, 0))],
        out_specs=pl.BlockSpec((lp, BLK), lambda i: (0, 0)),
        out_shape=jax.ShapeDtypeStruct((lp, BLK), F32),
        compiler_params=_cparams("arbitrary"), name="fox_gate_fwd")(proj_r, bf128)


def _fox_gate_bwd(proj_r, dc, bf128):
    lp = proj_r.shape[0]
    nb = lp // BLK

    def kern(s_ref, dc_ref, b_ref, dfa_ref, dbf_ref):
        tri = _tri(BLK, False)

        dbf_ref[...] = jnp.zeros_like(dbf_ref)

        def body(ii, run):
            i = nb - 1 - ii
            rows = pl.ds(pl.multiple_of(i * BLK, BLK), BLK)
            dcb = dc_ref[rows, :]
            suf = _dot_exact(tri, dcb) + run
            dfa = jnp.where(_valid_rows(i * BLK, BLK), suf * _sigmoid(-(s_ref[rows, :] + b_ref[...])), 0.0)
            dfa_ref[rows, :] = dfa
            dbf_ref[...] += jnp.sum(dfa, axis=0, keepdims=True)
            return run + jnp.sum(dcb, axis=0, keepdims=True)

        lax.fori_loop(0, nb, body, jnp.zeros((1, BLK), F32))

    return pl.pallas_call(
        kern, grid=(1,),
        in_specs=[pl.BlockSpec((lp, BLK), lambda i: (0, R_SMALL_BLK128)), pl.BlockSpec((lp, BLK), lambda i: (0, 0)),
                  pl.BlockSpec((1, BLK), lambda i: (0, 0))],
        out_specs=[pl.BlockSpec((lp, BLK), lambda i: (0, 0)), pl.BlockSpec((1, BLK), lambda i: (0, 0))],
        out_shape=[jax.ShapeDtypeStruct((lp, BLK), F32), jax.ShapeDtypeStruct((1, BLK), F32)],
        compiler_params=_cparams("arbitrary"), name="fox_gate_bwd")(proj_r, dc, bf128)


LOG2E = 1.4426950408889634
KEY_PAD_BIAS = 1e30


def _attn_logits2(q, k, ck, diag):
    t = _dot_nt(q, k) * (LOG2E * FOX_DH ** -0.5) - ck * LOG2E
    if diag:
        r = lax.broadcasted_iota(jnp.int32, t.shape, 0)
        c = lax.broadcasted_iota(jnp.int32, t.shape, 1)
        t = jnp.where(c <= r, t, NEG)
    return t


ATTN_HEADS = 2
ATTN_HEADS_FWD = 4


def _head_cols(a):
    return (slice(a * FOX_DH, (a + 1) * FOX_DH), slice(2 * a * FOX_DH, (2 * a + 1) * FOX_DH),
            slice((2 * a + 1) * FOX_DH, (2 * a + 2) * FOX_DH))


def _on_blocks(i, j, step):
    pl.when(j < i)(functools.partial(step, False))
    pl.when(j == i)(functools.partial(step, True))


def _attn_fwd(proj_a, cq, ck, tq=640, exch=None):
    lp = proj_a.shape[0]
    tq = _row_tile(lp, tq)
    tk = tq
    nq = lp // tq

    def kern(q_ref, kv_ref, cq_ref, ck_ref, o_ref, lse_ref, m_sc, l_sc, acc):
        i, j = pl.program_id(1), pl.program_id(2)

        @pl.when(j == 0)
        def _():
            m_sc[...] = jnp.full_like(m_sc, -jnp.inf)
            l_sc[...] = jnp.zeros_like(l_sc)
            acc[...] = jnp.zeros_like(acc)

        def step(diag):
            heads = range(ATTN_HEADS_FWD)
            cols = [_head_cols(a) for a in heads]
            m_old = [m_sc[a] for a in heads]
            l_old = [l_sc[a] for a in heads]
            acc_old = [acc[:, cols[a][0]] for a in heads]
            cq2 = [cq_ref[a] * LOG2E for a in heads]
            t = [_attn_logits2(q_ref[:, cols[a][0]], kv_ref[:, cols[a][1]], ck_ref[a], diag) for a in heads]
            m_new = [jnp.maximum(m_old[a], jnp.max(t[a], axis=-1, keepdims=True) + cq2[a]) for a in heads]
            p = [jnp.exp2(t[a] + (cq2[a] - m_new[a])) for a in heads]
            alpha = [jnp.exp2(m_old[a] - m_new[a]) for a in heads]
            l_new = [alpha[a] * l_old[a] + jnp.sum(p[a], axis=-1, keepdims=True) for a in heads]
            acc_new = [alpha[a] * acc_old[a] + _dot(p[a].astype(BF16), kv_ref[:, cols[a][2]]) for a in heads]
            for a in heads:
                m_sc[a] = m_new[a]
                l_sc[a] = l_new[a]
                acc[:, cols[a][0]] = acc_new[a]

        _on_blocks(i, j, step)

        @pl.when(j == nq - 1)
        def _():
            valid = _valid_rows(i * tq, tq)
            for a in range(ATTN_HEADS_FWD):
                hq, _, _ = _head_cols(a)
                o_ref[:, hq] = jnp.where(valid, acc[:, hq] / l_sc[a], 0.0).astype(BF16)
                lse_ref[a] = m_sc[a] + jnp.log(l_sc[a]) * LOG2E

    hp = ATTN_HEADS_FWD
    out, got = _pcall(
        kern, grid=(FOX_H // hp, nq, nq),
        in_specs=[pl.BlockSpec((tq, hp * FOX_DH), lambda h, i, j: (i, h)),
                  pl.BlockSpec((tk, 2 * hp * FOX_DH), lambda h, i, j: (jnp.minimum(j, i), KV0 // (2 * hp * FOX_DH) + h)),
                  pl.BlockSpec((hp, tq, 1), lambda h, i, j: (h, i, 0)),
                  pl.BlockSpec((hp, 1, tk), lambda h, i, j: (h, 0, jnp.minimum(j, i)))],
        out_specs=[pl.BlockSpec((tq, hp * FOX_DH), lambda h, i, j: (i, h)),
                   pl.BlockSpec((hp, tq, 1), lambda h, i, j: (h, i, 0))],
        out_shape=[jax.ShapeDtypeStruct((lp, FOX_W), BF16), jax.ShapeDtypeStruct((FOX_H, lp, 1), F32)],
        scratch=[pltpu.VMEM((hp, tq, 1), F32), pltpu.VMEM((hp, tq, 1), F32), pltpu.VMEM((tq, hp * FOX_DH), F32)],
        sem=("parallel", "parallel", "arbitrary"), name="attn_fwd", args=(proj_a, proj_a, cq, ck), exch=exch)
    return out[0], out[1], got


def _attn_bwd(proj_a, do, o, lse, cq, ck, dproj, tq=640, exch=None):
    lp = proj_a.shape[0]
    tq = _row_tile(lp, tq)
    tk = tq
    nq = lp // tq
    hp = ATTN_HEADS
    wq = hp * FOX_DH

    def kern(q_ref, kv_ref, do_ref, o_ref, lse_ref, cq_ref, ck_ref, dp_any, dproj_ref, dcq_ref, dck_ref,
             dk_acc, dv_acc, dq_acc, dcq_acc, dq_stage, dkv_stage, sems):
        del dp_any
        h, j, i = pl.program_id(0), pl.program_id(1), pl.program_id(2)
        rows = pl.ds(pl.multiple_of(i * tq, tq), tq)

        @pl.when(i == 0)
        def _():
            dck_ref[...] = jnp.zeros_like(dck_ref)
            dk_acc[...] = jnp.zeros_like(dk_acc)
            dv_acc[...] = jnp.zeros_like(dv_acc)

        @pl.when(jnp.logical_and(i == 0, j == 0))
        def _():
            dq_acc[...] = jnp.zeros_like(dq_acc)
            dcq_acc[...] = jnp.zeros_like(dcq_acc)

        def step(diag):
            heads = range(hp)
            cols = [_head_cols(a) for a in heads]
            dck_old = [dck_ref[a] for a in heads]
            dcq_old = [dcq_acc[a, rows, :] for a in heads]
            dk_old = [dk_acc[:, cols[a][0]] for a in heads]
            dv_old = [dv_acc[:, cols[a][0]] for a in heads]
            dq_old = [dq_acc[rows, cols[a][0]] for a in heads]
            shift = [cq_ref[a] * LOG2E - lse_ref[a] for a in heads]
            do_h = [do_ref[:, cols[a][0]] for a in heads]
            delta = [jnp.sum(do_h[a].astype(F32) * o_ref[:, cols[a][0]].astype(F32), axis=-1, keepdims=True)
                     for a in heads]
            t = [_attn_logits2(q_ref[:, cols[a][0]], kv_ref[:, cols[a][1]], ck_ref[a], diag) for a in heads]
            dp = [_dot_nt(do_h[a], kv_ref[:, cols[a][2]]) for a in heads]
            p = [jnp.exp2(t[a] + shift[a]) for a in heads]
            ds = [p[a] * (dp[a] - delta[a]) for a in heads]
            dsb = [ds[a].astype(BF16) for a in heads]
            dv_new = [dv_old[a] + _dot_tn(p[a].astype(BF16), do_h[a]) for a in heads]
            dck_new = [dck_old[a] - jnp.sum(ds[a], axis=0, keepdims=True) for a in heads]
            dcq_new = [dcq_old[a] + jnp.sum(ds[a], axis=-1, keepdims=True) for a in heads]
            dk_new = [dk_old[a] + _dot_tn(dsb[a], q_ref[:, cols[a][0]]) for a in heads]
            dq_new = [dq_old[a] + _dot(dsb[a], kv_ref[:, cols[a][1]]) for a in heads]
            for a in heads:
                dck_ref[a] = dck_new[a]
                dcq_acc[a, rows, :] = dcq_new[a]
                dk_acc[:, cols[a][0]] = dk_new[a]
                dv_acc[:, cols[a][0]] = dv_new[a]
                dq_acc[rows, cols[a][0]] = dq_new[a]

        _on_blocks(i, j, step)

        @pl.when(i == nq - 1)
        def _():
            parts = []
            for a in range(hp):
                hq, _, _ = _head_cols(a)
                parts += [dk_acc[:, hq] * (FOX_DH ** -0.5), dv_acc[:, hq]]
            dkv_stage[...] = jnp.concatenate(parts, axis=1).astype(BF16)
            out = pltpu.make_async_copy(
                dkv_stage, dproj_ref.at[pl.ds(pl.multiple_of(j * tk, tk), tk),
                                        pl.ds(pl.multiple_of(KV0 + h * 2 * wq, 2 * wq), 2 * wq)], sems.at[0])
            out.start()
            out.wait()

        @pl.when(jnp.logical_and(i == nq - 1, j == nq - 1))
        def _():
            dq_stage[...] = (dq_acc[...] * (FOX_DH ** -0.5)).astype(BF16)
            out = pltpu.make_async_copy(dq_stage, dproj_ref.at[:, pl.ds(pl.multiple_of(h * wq, wq), wq)], sems.at[1])
            rowsums = pltpu.make_async_copy(dcq_acc, dcq_ref.at[pl.ds(h * hp, hp)], sems.at[2])
            out.start()
            rowsums.start()
            out.wait()
            rowsums.wait()

    qspec = pl.BlockSpec((tq, wq), lambda h, j, i: (jnp.maximum(i, j), h))
    col = pl.BlockSpec((hp, tq, 1), lambda h, j, i: (h, jnp.maximum(i, j), 0))
    kvspec = pl.BlockSpec((tk, 2 * wq), lambda h, j, i: (j, KV0 // (2 * wq) + h))
    rowspec = pl.BlockSpec((hp, 1, tk), lambda h, j, i: (h, 0, j))
    out, got = _pcall(
        kern, grid=(FOX_H // hp, nq, nq),
        in_specs=[qspec, kvspec, qspec, qspec, col, col, rowspec, pl.BlockSpec(memory_space=pl.ANY)],
        out_specs=[pl.BlockSpec(memory_space=pl.ANY), pl.BlockSpec(memory_space=pl.ANY), rowspec],
        out_shape=[jax.ShapeDtypeStruct(dproj.shape, BF16), jax.ShapeDtypeStruct((FOX_H, lp, 1), F32),
                   jax.ShapeDtypeStruct((FOX_H, 1, lp), F32)],
        scratch=[pltpu.VMEM((tk, wq), F32), pltpu.VMEM((tk, wq), F32), pltpu.VMEM((lp, wq), F32),
                 pltpu.VMEM((hp, lp, 1), F32), pltpu.VMEM((lp, wq), BF16), pltpu.VMEM((tk, 2 * wq), BF16),
                 pltpu.SemaphoreType.DMA((3,))],
        aliases={7: 0}, sem=("arbitrary", "arbitrary", "arbitrary"), name="attn_bwd",
        args=(proj_a, proj_a, do, o, lse, cq, ck, dproj), exch=exch)
    return out[0], out[1], out[2], got


def _gla_gate_fwd(proj_r, wg2p, bg, tr=640):
    lp = proj_r.shape[0]
    tr = _row_tile(lp, tr)
    w = GLA_H * GLA_DK

    def kern(s_ref, w_ref, b_ref, o_ref):
        zg = _dot(s_ref[...].astype(BF16), w_ref[...]) + b_ref[...]
        o_ref[...] = jnp.where(_valid_rows(pl.program_id(0) * tr, tr), _log_sigmoid(zg) * (1.0 / GLA_TAU), 0.0)

    return pl.pallas_call(
        kern, grid=(lp // tr,),
        in_specs=[pl.BlockSpec((tr, BLK), lambda i: (i, R_SMALL_BLK128)), pl.BlockSpec((BLK, w), lambda i: (0, 0)),
                  pl.BlockSpec((1, w), lambda i: (0, 0))],
        out_specs=pl.BlockSpec((tr, w), lambda i: (i, 0)),
        out_shape=jax.ShapeDtypeStruct((lp, w), F32),
        compiler_params=_cparams("parallel"), name="gla_gate_fwd")(proj_r, wg2p, bg)


def _gla_chunk(grp, g):
    q = grp[:, :GLA_DK] * (GLA_DK ** -0.5)
    k = grp[:, GLA_DK:2 * GLA_DK]
    v = grp[:, 2 * GLA_DK:2 * GLA_DK + GLA_DV]
    r = grp[:, 2 * GLA_DK + GLA_DV:]
    b = _dot_exact(_tri(BLK, True), g)
    bl = b[BLK - 1:BLK, :]
    eb = jnp.exp(b)
    enb = jnp.exp(-b)
    ebl = jnp.exp(bl - b)
    qe, ke, kd = q * eb, k * enb, k * ebl
    causal = lax.broadcasted_iota(jnp.int32, (BLK, BLK), 1) <= lax.broadcasted_iota(jnp.int32, (BLK, BLK), 0)
    att = jnp.where(causal, _dot_nt(qe.astype(BF16), ke.astype(BF16)), 0.0)
    return q, k, v, r, bl, eb, enb, ebl, qe, ke, kd, causal, att


def _gla_fwd(proj_r, logg, gn):
    lp = proj_r.shape[0]
    nc = lp // BLK
    wv = GLA_H * GLA_DV

    def kern(grp_ref, g_ref, gn_ref, o_ref, zc_ref, st_ref, st):
        c = pl.program_id(0)

        @pl.when(c == 0)
        def _():
            st[...] = jnp.zeros_like(st)

        for h in range(GLA_H):
            kcol = slice(h * GLA_DK, (h + 1) * GLA_DK)
            vcol = slice(h * GLA_DV, (h + 1) * GLA_DV)
            q, k, v, r, bl, eb, enb, ebl, qe, ke, kd, causal, att = _gla_chunk(
                grp_ref[:, h * GLA_GRP:(h + 1) * GLA_GRP], g_ref[:, kcol])
            s_t = st[h]
            st_ref[h] = s_t
            vb = v.astype(BF16)
            o = _dot(att.astype(BF16), vb) + _dot_nt(qe.astype(BF16), s_t.astype(BF16))
            st[h] = s_t * jnp.exp(bl) + _dot_tn(vb, kd.astype(BF16))
            o_ref[:, vcol] = o
            rstd = lax.rsqrt(jnp.mean(o * o, axis=-1, keepdims=True) + EPS)
            zc_ref[:, vcol] = (r * _sigmoid(r) * (o * rstd * gn_ref[:, vcol])).astype(BF16)

    vspec = pl.BlockSpec((BLK, wv), lambda c: (c, 0))
    return pl.pallas_call(
        kern, grid=(nc,),
        in_specs=[pl.BlockSpec((BLK, GLA_H * GLA_GRP), lambda c: (c, R_GLA_BLK0 // GLA_H)),
                  pl.BlockSpec((BLK, GLA_H * GLA_DK), lambda c: (c, 0)),
                  pl.BlockSpec((1, wv), lambda c: (0, 0))],
        out_specs=[vspec, vspec, pl.BlockSpec((GLA_H, None, GLA_DV, GLA_DK), lambda c: (0, c, 0, 0))],
        out_shape=[jax.ShapeDtypeStruct((lp, wv), F32), jax.ShapeDtypeStruct((lp, wv), BF16),
                   jax.ShapeDtypeStruct((GLA_H, nc, GLA_DV, GLA_DK), F32)],
        scratch_shapes=[pltpu.VMEM((GLA_H, GLA_DV, GLA_DK), F32)],
        compiler_params=_cparams("arbitrary"), name="gla_fwd")(proj_r, logg, gn)


def _gla_bwd(proj_r, logg, st_all, o_all, dzc, gn, dproj):
    lp = proj_r.shape[0]
    nc = lp // BLK

    def kern(grp_ref, g_ref, st_ref, o_ref, dzc_ref, gn_ref, dp_any, dgrp_ref, dlg_ref, dgn_ref, dst):
        del dp_any
        cc = pl.program_id(0)

        @pl.when(cc == 0)
        def _():
            dst[...] = jnp.zeros_like(dst)
            dgn_ref[...] = jnp.zeros_like(dgn_ref)

        for h in range(GLA_H):
            kcol = slice(h * GLA_DK, (h + 1) * GLA_DK)
            vcol = slice(h * GLA_DV, (h + 1) * GLA_DV)
            q, k, v, r, bl, eb, enb, ebl, qe, ke, kd, causal, att = _gla_chunk(
                grp_ref[:, h * GLA_GRP:(h + 1) * GLA_GRP], g_ref[:, kcol])
            s_t = st_ref[h]
            d_st = dst[h]
            o = o_ref[:, vcol]
            dzc_v = dzc_ref[:, vcol]
            gnv = gn_ref[:, vcol]
            rstd = lax.rsqrt(jnp.mean(o * o, axis=-1, keepdims=True) + EPS)
            xhat = o * rstd
            sr = _sigmoid(r)
            dr = dzc_v * (xhat * gnv) * (sr * (1.0 + r * (1.0 - sr)))
            docn = dzc_v * (r * sr)
            dgn_ref[:, vcol] += jnp.sum(docn * xhat, axis=0, keepdims=True)
            dxh = docn * gnv
            do = rstd * (dxh - xhat * jnp.mean(dxh * xhat, axis=-1, keepdims=True))
            dob, vb = do.astype(BF16), v.astype(BF16)
            qeb, keb, kdb = qe.astype(BF16), ke.astype(BF16), kd.astype(BF16)
            datt = jnp.where(causal, _dot_nt(dob, vb), 0.0).astype(BF16)
            dv = _dot_tn(att.astype(BF16), dob) + _dot_nt(kdb, d_st.astype(BF16))
            dqe = _dot(datt, keb) + _dot(dob, s_t.astype(BF16))
            dke = _dot_tn(datt, qeb)
            dkd = _dot(vb, d_st.astype(BF16))
            dq = dqe * eb * (GLA_DK ** -0.5)
            dk = dke * enb + dkd * ebl
            kd_dkd = dkd * kd
            db = dqe * qe - dke * ke - kd_dkd
            db_last = (jnp.sum(kd_dkd, axis=0, keepdims=True)
                       + jnp.exp(bl) * jnp.sum(s_t * d_st, axis=0, keepdims=True))
            dlg_ref[:, kcol] = _dot_exact(_tri(BLK, False), db) + db_last
            dst[h] = d_st * jnp.exp(bl) + _dot_tn(dob, qeb)
            dgrp_ref[:, h * GLA_GRP:(h + 1) * GLA_GRP] = jnp.concatenate([dq, dk, dv, dr], axis=1).astype(BF16)

    rev = lambda c: nc - 1 - c
    wv = GLA_H * GLA_DV
    vspec = pl.BlockSpec((BLK, wv), lambda c: (rev(c), 0))
    kspec = pl.BlockSpec((BLK, GLA_H * GLA_DK), lambda c: (rev(c), 0))
    return pl.pallas_call(
        kern, grid=(nc,),
        in_specs=[pl.BlockSpec((BLK, GLA_H * GLA_GRP), lambda c: (rev(c), R_GLA_BLK0 // GLA_H)), kspec,
                  pl.BlockSpec((GLA_H, None, GLA_DV, GLA_DK), lambda c: (0, rev(c), 0, 0)),
                  vspec, vspec, pl.BlockSpec((1, wv), lambda c: (0, 0)),
                  pl.BlockSpec(memory_space=pl.ANY)],
        out_specs=[pl.BlockSpec((BLK, GLA_H * GLA_GRP), lambda c: (rev(c), F_GLA_BLK0 // GLA_H)), kspec,
                   pl.BlockSpec((1, wv), lambda c: (0, 0))],
        out_shape=[jax.ShapeDtypeStruct(dproj.shape, BF16), jax.ShapeDtypeStruct((lp, GLA_H * GLA_DK), F32),
                   jax.ShapeDtypeStruct((1, wv), F32)],
        scratch_shapes=[pltpu.VMEM((GLA_H, GLA_DV, GLA_DK), F32)],
        input_output_aliases={6: 0},
        compiler_params=_cparams("arbitrary"), name="gla_bwd",
    )(proj_r, logg, st_all, o_all, dzc, gn, dproj)


def _small_bwd(proj_r, dlogg, wg2p, wg2pt, bg, dfa, dproj, tr=640):
    lp = proj_r.shape[0]
    tr = _row_tile(lp, tr)
    w = GLA_H * GLA_DK

    def kern(s_ref, dlg_ref, w_ref, wt_ref, b_ref, dfa_ref, dp_any, ds_ref, dbg_ref, dw_ref):
        del dp_any
        i = pl.program_id(0)
        sb = s_ref[...].astype(BF16)
        zg = _dot(sb, w_ref[...]) + b_ref[...]
        dzg = jnp.where(_valid_rows(i * tr, tr), dlg_ref[...] * (1.0 / GLA_TAU) * _sigmoid(-zg), 0.0)

        @pl.when(i == 0)
        def _():
            dbg_ref[...] = jnp.zeros_like(dbg_ref)
            dw_ref[...] = jnp.zeros_like(dw_ref)

        dbg_ref[...] += jnp.sum(dzg, axis=0, keepdims=True)
        dzb = dzg.astype(BF16)
        dw_ref[...] += _dot_tn(sb, dzb)
        dsm = _dot(dzb, wt_ref[...]) + dfa_ref[...]
        ds_ref[...] = jnp.concatenate([dsm, jnp.zeros((tr, SMALL_W - BLK), F32)], axis=1).astype(BF16)

    return pl.pallas_call(
        kern, grid=(lp // tr,),
        in_specs=[pl.BlockSpec((tr, BLK), lambda i: (i, R_SMALL_BLK128)), pl.BlockSpec((tr, w), lambda i: (i, 0)),
                  pl.BlockSpec((BLK, w), lambda i: (0, 0)), pl.BlockSpec((w, BLK), lambda i: (0, 0)),
                  pl.BlockSpec((1, w), lambda i: (0, 0)), pl.BlockSpec((tr, BLK), lambda i: (i, 0)),
                  pl.BlockSpec(memory_space=pl.ANY)],
        out_specs=[pl.BlockSpec((tr, SMALL_W), lambda i: (i, F_SMALL_BLK0)), pl.BlockSpec((1, w), lambda i: (0, 0)),
                   pl.BlockSpec((BLK, w), lambda i: (0, 0))],
        out_shape=[jax.ShapeDtypeStruct(dproj.shape, BF16), jax.ShapeDtypeStruct((1, w), F32),
                   jax.ShapeDtypeStruct((BLK, w), F32)],
        input_output_aliases={6: 0},
        compiler_params=_cparams("arbitrary"), name="small_bwd",
    )(proj_r, dlogg, wg2p, wg2pt, bg, dfa, dproj)


def _merge_fwd(proj_r, gate_b3, ya, yb, yc, tr=640):
    lp = proj_r.shape[0]
    tr = _row_tile(lp, tr)
    tn = GATE_TN

    def kern(g_ref, b_ref, ya_ref, yb_ref, yc_ref, o_ref):
        g = g_ref[...]
        mix = (_sigmoid(g[:, :tn] + b_ref[0:1, :]) * ya_ref[...]
               + _sigmoid(g[:, tn:2 * tn] + b_ref[1:2, :]) * yb_ref[...]
               + _sigmoid(g[:, 2 * tn:] + b_ref[2:3, :]) * yc_ref[...])
        o_ref[...] = mix.astype(BF16)

    y = pl.BlockSpec((tr, tn), lambda i, j: (i, j))
    return pl.pallas_call(
        kern, grid=(lp // tr, D // tn),
        in_specs=[pl.BlockSpec((tr, 3 * tn), lambda i, j: (i, R_GATE_BLK0 + j)),
                  pl.BlockSpec((3, tn), lambda i, j: (0, j)), y, y, y],
        out_specs=y, out_shape=jax.ShapeDtypeStruct((lp, D), BF16),
        compiler_params=_cparams("parallel", "parallel"), name="merge_fwd")(proj_r, gate_b3, ya, yb, yc)


def _merge_bwd(proj_r, gate_b3, ya, yb, yc, dmix, tr=640):
    lp = proj_r.shape[0]
    tr = _row_tile(lp, tr)
    tn = GATE_TN

    def kern(g_ref, b_ref, ya_ref, yb_ref, yc_ref, dm_ref, dya_ref, dyb_ref, dyc_ref, dg_ref, db_ref):
        i = pl.program_id(1)
        g = g_ref[...]
        dm = dm_ref[...]

        @pl.when(i == 0)
        def _():
            db_ref[...] = jnp.zeros_like(db_ref)

        dgs = []
        for n, (y_ref, dy_ref) in enumerate(((ya_ref, dya_ref), (yb_ref, dyb_ref), (yc_ref, dyc_ref))):
            s = _sigmoid(g[:, n * tn:(n + 1) * tn] + b_ref[n:n + 1, :])
            dy_ref[...] = (dm * s).astype(BF16)
            dgn = dm * y_ref[...] * (s * (1.0 - s))
            db_ref[n:n + 1, :] += jnp.sum(dgn, axis=0, keepdims=True)
            dgs.append(dgn)
        dg_ref[...] = jnp.concatenate(dgs, axis=1).astype(BF16)

    y = pl.BlockSpec((tr, tn), lambda j, i: (i, j))
    bspec = pl.BlockSpec((3, tn), lambda j, i: (0, j))
    return pl.pallas_call(
        kern, grid=(D // tn, lp // tr),
        in_specs=[pl.BlockSpec((tr, 3 * tn), lambda j, i: (i, R_GATE_BLK0 + j)), bspec, y, y, y, y],
        out_specs=[y, y, y, pl.BlockSpec((tr, 3 * tn), lambda j, i: (i, F_GATE_BLK0 + j)), bspec],
        out_shape=[jax.ShapeDtypeStruct((lp, D), BF16)] * 3
        + [jax.ShapeDtypeStruct((lp, NP), BF16), jax.ShapeDtypeStruct((3, D), F32)],
        compiler_params=_cparams("parallel", "arbitrary"), name="merge_bwd")(proj_r, gate_b3, ya, yb, yc, dmix)


def _final_loss(h, gf, tgt):
    lp = h.shape[0]
    nb = lp // BLK

    def kern(h_ref, g_ref, t_ref, dh_ref, dg_ref, ls_ref):
        i = pl.program_id(0)

        @pl.when(i == 0)
        def _():
            dh_ref[...] = jnp.zeros_like(dh_ref)
            dg_ref[...] = jnp.zeros_like(dg_ref)
            ls_ref[...] = jnp.zeros_like(ls_ref)

        @pl.when(i > 0)
        def _():
            x = h_ref[...]
            r = lax.rsqrt(jnp.mean(x * x, axis=-1, keepdims=True) + EPS)
            xhat = x * r
            err = xhat * g_ref[...] - t_ref[...]
            ls_ref[...] += jnp.sum(jnp.sum(err * err, axis=0, keepdims=True), axis=1, keepdims=True)
            dy = err * (1.0 / D)
            dg_ref[...] += jnp.sum(dy * xhat, axis=0, keepdims=True)
            dxh = dy * g_ref[...]
            dh_ref[...] = r * (dxh - xhat * jnp.mean(dxh * xhat, axis=-1, keepdims=True))

    row = pl.BlockSpec((BLK, D), lambda i: (i, 0))
    vec = pl.BlockSpec((1, D), lambda i: (0, 0))
    return pl.pallas_call(
        kern, grid=(nb,),
        in_specs=[row, vec, pl.BlockSpec((BLK, D), lambda i: (jnp.maximum(i - 1, 0), 0))],
        out_specs=[row, vec, pl.BlockSpec((1, 1), lambda i: (0, 0))],
        out_shape=[jax.ShapeDtypeStruct((lp, D), F32), jax.ShapeDtypeStruct((1, D), F32),
                   jax.ShapeDtypeStruct((1, 1), F32)],
        compiler_params=_cparams("arbitrary"), name="final_loss")(h, gf, tgt)


def _gate_cols(c):
    ct = c[:, :FOX_H].T
    ck = jnp.where(jnp.arange(ct.shape[1]) < PAD, KEY_PAD_BIAS, ct)
    return ct[:, :, None], ck[:, None, :]


def _run(hosts, name, ctx, fn):
    if hosts and name in hosts:
        make, done = hosts[name]
        res = fn(make(ctx))
        done(res[-1])
    else:
        res = fn(None)
    return res[:-1]


def _mm_nn_x(a, b, exch, **kw):
    out = _mm_nn(a, b, exch=exch, **kw)
    return out if exch is not None else (out, None)


def _layer_fwd(h, w, hosts=None):
    xn = _rmsnorm_fwd(h, w["norm1_g"])
    proj_a, = _run(hosts, "proj_a", w,
                   lambda e: _mm_nn_x(xn, w["w_in"], e, n0=0, n=REST0, out_dtype=BF16, name="proj_a"))
    proj_r, = _run(hosts, "proj_r", w, lambda e: _mm_nn_x(xn, w["w_in"], e, n0=REST0, n=NREST, name="proj_r"))
    cq, ck = _gate_cols(_fox_gate_fwd(proj_r, w["bf128"]))
    oa, lse = _run(hosts, "attn_fwd", w, lambda e: _attn_fwd(proj_a, cq, ck, exch=e))
    zb = _convb_fwd(proj_r, w["conv_w"])
    logg = _gla_gate_fwd(proj_r, w["wg2p"], w["gla_b_g"])
    o_gla, zc, st_all = _gla_fwd(proj_r, logg, w["gla_norm_g"])
    ya = _mm_nn(oa, w["w_a_o"], name="branch_a")
    yb = _mm_nn(zb, w["w_b_o"], name="branch_b")
    yc = _mm_nn(zc, w["w_c_o"], name="branch_c")
    mix = _merge_fwd(proj_r, w["gate_b3"], ya, yb, yc)
    h1 = _mm_nn(mix, w["w_o"], res=h, name="out_proj")
    xn2 = _rmsnorm_fwd(h1, w["norm2_g"])
    z, = _run(hosts, "up_proj", w, lambda e: _mm_nn_x(xn2, w["w_up"], e, out_dtype=BF16, name="up_proj"))
    a = _mlp_gate_fwd(z, w["mlp_conv_w"])
    h2, = _run(hosts, "down_proj", w,
               lambda e: _mm_nn_x(a, w["w_down"], e, res=h1, tk=D_FF // 4, name="down_proj"))
    saved = dict(h=h, xn=xn, proj_a=proj_a, proj_r=proj_r, cq=cq, ck=ck, oa=oa, lse=lse, zb=zb, logg=logg,
                 o_gla=o_gla, zc=zc, st_all=st_all, ya=ya, yb=yb, yc=yc, mix=mix, h1=h1, xn2=xn2, z=z, a=a)
    return h2, saved


def _layer_bwd(dh2, w, s, hosts=None):
    g = {}
    da = _mm_nt(dh2, w["w_down"], tn=D_FF // 4, out_dtype=BF16, name="d_down_in")
    g["w_down"] = _mm_tn(s["a"], dh2, tk=D_FF // 4, name="d_w_down")
    dzg, dzu, dmw_g, dmw_u = _run(hosts, "mlp_gate_bwd", g, lambda e: _mlp_gate_bwd(
        s["z"], da, w["mlp_conv_w"], exch=e))
    g["mlp_conv_w"] = jnp.concatenate([dmw_g, dmw_u], axis=1)
    dxn2 = _mm_nt(dzg, w["w_up"], k0=0, kw=D_FF, tk=D_FF // 4, name="d_up_in_g")
    dxn2 = _mm_nt(dzu, w["w_up"], k0=D_FF, kw=D_FF, tk=D_FF // 4, add=dxn2, name="d_up_in_u")
    g["w_up"] = jnp.concatenate([_mm_tn(s["xn2"], dzg, name="d_w_up_g"), _mm_tn(s["xn2"], dzu, name="d_w_up_u")], axis=1)
    dh1, g["norm2_g"] = _rmsnorm_bwd(s["h1"], w["norm2_g"], dxn2, dh2)
    dmix = _mm_nt(dh1, w["w_o"], name="d_out_proj_in")
    g["w_o"] = _mm_tn(s["mix"], dh1, name="d_w_o")
    dya, dyb, dyc, dproj, g["gate_b3"] = _merge_bwd(s["proj_r"], w["gate_b3"], s["ya"], s["yb"], s["yc"], dmix)
    doa = _mm_nt(dya, w["w_a_o"], out_dtype=BF16, name="d_branch_a_in")
    g["w_a_o"] = _mm_tn(s["oa"], dya, name="d_w_a_o")
    dzb = _mm_nt(dyb, w["w_b_o"], name="d_branch_b_in")
    g["w_b_o"] = _mm_tn(s["zb"], dyb, name="d_w_b_o")
    dzc = _mm_nt(dyc, w["w_c_o"], name="d_branch_c_in")
    g["w_c_o"] = _mm_tn(s["zc"], dyc, name="d_w_c_o")
    dproj, dlogg, g["gla_norm_g"] = _gla_bwd(s["proj_r"], s["logg"], s["st_all"], s["o_gla"], dzc, w["gla_norm_g"], dproj)
    dproj, g["conv_w"] = _convb_bwd(s["proj_r"], dzb, w["conv_w"], dproj)
    dproj, dcq, dck = _run(hosts, "attn_bwd", g, lambda e: _attn_bwd(
        s["proj_a"], doa, s["oa"], s["lse"], s["cq"], s["ck"], dproj, exch=e))
    dc = jnp.pad((dcq[:, :, 0] + dck[:, 0, :]).T, ((0, 0), (0, BLK - FOX_H)))
    dfa, g["bf128"] = _fox_gate_bwd(s["proj_r"], dc, w["bf128"])
    dproj, g["gla_b_g"], g["wg2p"] = _small_bwd(s["proj_r"], dlogg, w["wg2p"], w["wg2p"].T, w["gla_b_g"], dfa, dproj)
    def pair(out, e):
        return out if e is not None else (out, None)

    g["w_in"], = _run(hosts, "d_w_in", g, lambda e: pair(_mm_tn(s["xn"], dproj, name="d_w_in", exch=e), e))
    dxn, = _run(hosts, "d_in_proj_in", g, lambda e: pair(_mm_nt(dproj, w["w_in"], name="d_in_proj_in", exch=e), e))
    dh0, g["norm1_g"] = _rmsnorm_bwd(s["h"], w["norm1_g"], dxn, dh1)
    return dh0, g


def _local_step(x, tgt, meta, final_g, layers, hosts_fwd=None, hosts_bwd=None):
    h = jnp.concatenate([jnp.zeros((PAD, D), F32), meta, x], axis=0)
    saved = []
    for l, w in enumerate(layers):
        h, s = _layer_fwd(h, w, hosts_fwd[l] if hosts_fwd else None)
        saved.append(s)
    dh, dgf, sq = _final_loss(h, final_g, tgt)
    grads = [None] * len(layers)
    for l in reversed(range(len(layers))):
        dh, grads[l] = _layer_bwd(dh, layers[l], saved[l], hosts_bwd[l](grads) if hosts_bwd else None)
    return sq[0, 0], dh[BLK:], dh[PAD:BLK], dgf, grads


def _w_in_to_kernel(w_nat):
    parts = [w_nat[:, s:s + n] for s, n in _segments()]
    parts.append(jnp.zeros((w_nat.shape[0], SMALL_W - 8 - GLA_R), w_nat.dtype))
    return jnp.concatenate(parts, axis=1)


def _w_in_from_kernel(w_k):
    pieces, off = [], 0
    for s, n in _segments():
        pieces.append((s, w_k[:, off:off + n]))
        off += n
    return jnp.concatenate([p for _, p in sorted(pieces, key=lambda t: t[0])], axis=1)


def _w_in_slots_to_kernel(got):
    per = got.shape[2]
    parts = []
    for s, n in _segments():
        while n > 0:
            d, lo = divmod(s, per)
            take = min(n, per - lo)
            parts.append(got[d, :, lo:lo + take])
            s, n = s + take, n - take
    parts.append(jnp.zeros((got.shape[1], SMALL_W - 8 - GLA_R), got.dtype))
    return jnp.concatenate(parts, axis=1)


def _w_in_kernel_to_slots(w_k):
    per = N_IN // N_DEV
    pieces, off = [], 0
    for s, n in _segments():
        pieces.append((s, n, off))
        off += n
    slots = []
    for d in range(N_DEV):
        lo, hi = d * per, (d + 1) * per
        parts = [w_k[:, off + max(s, lo) - s:off + min(s + n, hi) - s]
                 for s, n, off in sorted(pieces) if max(s, lo) < min(s + n, hi)]
        slots.append(jnp.concatenate(parts, axis=1))
    return jnp.stack(slots)


def _pad_rows_at(a, row0, nrows):
    return jnp.pad(a, ((row0, nrows - row0 - a.shape[0]), (0, 0)))


def _big_to_kernel(name, full):
    return _w_in_to_kernel(full) if name == "w_in" else full


def _layer_weights(big, conv_w, gla_w_g2, mlp_conv_w, norm1_g, fox_b_f, gate_b, gla_b_g, gla_norm_g, norm2_g):
    w = {n: _big_to_kernel(n, a) for n, a in big.items()}
    w.update(
        conv_w=conv_w, mlp_conv_w=mlp_conv_w,
        wg2p=_pad_rows_at(gla_w_g2, 8, BLK).astype(BF16),
        norm1_g=norm1_g[None], norm2_g=norm2_g[None], gla_b_g=gla_b_g[None], gla_norm_g=gla_norm_g[None],
        bf128=jnp.pad(fox_b_f, (0, BLK - FOX_H))[None], gate_b3=gate_b.reshape(3, D))
    return w


def _layer_grads_natural(g):
    return dict(
        w_in=_w_in_from_kernel(g["w_in"]), w_a_o=g["w_a_o"], w_b_o=g["w_b_o"], w_c_o=g["w_c_o"], w_o=g["w_o"],
        w_up=g["w_up"], w_down=g["w_down"], conv_w=g["conv_w"], mlp_conv_w=g["mlp_conv_w"],
        gla_w_g2=g["wg2p"][8:8 + GLA_R], norm1_g=g["norm1_g"][0], norm2_g=g["norm2_g"][0],
        gla_b_g=g["gla_b_g"][0], gla_norm_g=g["gla_norm_g"][0], fox_b_f=g["bf128"][0, :FOX_H],
        gate_b=g["gate_b3"].reshape(3 * D))


def _adamw(recv, w, m, v, layer, prev=None, name="adamw"):
    n_slot, r, c = recv.shape
    lyr = w.shape[0]
    tr = r
    for t in range(16, r, 16):
        if r % t == 0 and t * c <= ADAMW_BLOCK_ELEMS:
            tr = t
    if r * c <= ADAMW_BLOCK_ELEMS:
        tr = r
    bc1, bc2 = 1.0 - ADAM_B1 ** ADAM_STEP, 1.0 - ADAM_B2 ** ADAM_STEP

    def kern(*refs):
        r_ref, w_ref, m_ref, v_ref = refs[:4]
        g_out, d_out, m_out, v_out = refs[-4:]
        g = r_ref[0].astype(F32)
        for sidx in range(1, n_slot):
            g = g + r_ref[sidx].astype(F32)
        m_new = ADAM_B1 * m_ref[...] + (1.0 - ADAM_B1) * g
        v_new = ADAM_B2 * v_ref[...] + (1.0 - ADAM_B2) * (g * g)
        g_out[...] = g
        m_out[...] = m_new
        v_out[...] = v_new
        d_out[...] = -ADAM_LR * ((m_new / bc1) / (jnp.sqrt(v_new / bc2) + ADAM_EPS) + ADAM_WD * w_ref[...])

    lspec = pl.BlockSpec((None, tr, c), lambda i: (layer, i, 0))
    in_specs = [pl.BlockSpec((n_slot, tr, c), lambda i: (0, i, 0)), lspec, lspec, lspec]
    args = [recv, w, m, v]
    aliases = {}
    if prev is not None:
        in_specs += [pl.BlockSpec(memory_space=pl.ANY)] * 4
        args += list(prev)
        aliases = {4: 0, 5: 1, 6: 2, 7: 3}
    return pl.pallas_call(
        kern, grid=(r // tr,), in_specs=in_specs, out_specs=[lspec] * 4,
        out_shape=[jax.ShapeDtypeStruct((lyr, r, c), F32)] * 4, input_output_aliases=aliases,
        compiler_params=_cparams("parallel"), name=name)(*args)


_BIG = ("w_in", "w_a_o", "w_b_o", "w_c_o", "w_o", "w_up", "w_down")
_COL_SHARDED = ("w_in", "w_a_o", "w_b_o", "w_c_o", "w_up", "conv_w", "gla_w_g2", "mlp_conv_w")
_REPL = ("norm1_g", "fox_b_f", "gate_b", "gla_b_g", "gla_norm_g", "norm2_g")


def _cols_from_slots(a):
    return jnp.transpose(a, (1, 0, 2)).reshape(a.shape[1], N_DEV * a.shape[2])


def _cols_to_slots(a):
    r, c8 = a.shape
    return jnp.transpose(a.reshape(r, N_DEV, c8 // N_DEV), (1, 0, 2))


def _rows_to_slots(a):
    return a.reshape(N_DEV, a.shape[0] // N_DEV, a.shape[1])


def kernel(x, meta_tokens, norm1_g, w_in, fox_b_f, gate_b, conv_w, gla_w_g2, gla_b_g, gla_norm_g, w_a_o, w_b_o, w_c_o, w_o, norm2_g, w_up, mlp_conv_w, w_down, final_norm_g, loss_target, m_meta_tokens, m_norm1_g, m_w_in, m_fox_b_f, m_gate_b, m_conv_w, m_gla_w_g2, m_gla_b_g, m_gla_norm_g, m_w_a_o, m_w_b_o, m_w_c_o, m_w_o, m_norm2_g, m_w_up, m_mlp_conv_w, m_w_down, m_final_norm_g, v_meta_tokens, v_norm1_g, v_w_in, v_fox_b_f, v_gate_b, v_conv_w, v_gla_w_g2, v_gla_b_g, v_gla_norm_g, v_w_a_o, v_w_b_o, v_w_c_o, v_w_o, v_norm2_g, v_w_up, v_mlp_conv_w, v_w_down, v_final_norm_g):
    names = ("meta_tokens", "norm1_g", "w_in", "fox_b_f", "gate_b", "conv_w", "gla_w_g2", "gla_b_g", "gla_norm_g",
             "w_a_o", "w_b_o", "w_c_o", "w_o", "norm2_g", "w_up", "mlp_conv_w", "w_down", "final_norm_g")
    wts = dict(zip(names, (meta_tokens, norm1_g, w_in, fox_b_f, gate_b, conv_w, gla_w_g2, gla_b_g, gla_norm_g,
                           w_a_o, w_b_o, w_c_o, w_o, norm2_g, w_up, mlp_conv_w, w_down, final_norm_g)))
    mom = dict(zip(names, (m_meta_tokens, m_norm1_g, m_w_in, m_fox_b_f, m_gate_b, m_conv_w, m_gla_w_g2, m_gla_b_g,
                           m_gla_norm_g, m_w_a_o, m_w_b_o, m_w_c_o, m_w_o, m_norm2_g, m_w_up, m_mlp_conv_w, m_w_down,
                           m_final_norm_g)))
    var = dict(zip(names, (v_meta_tokens, v_norm1_g, v_w_in, v_fox_b_f, v_gate_b, v_conv_w, v_gla_w_g2, v_gla_b_g,
                           v_gla_norm_g, v_w_a_o, v_w_b_o, v_w_c_o, v_w_o, v_norm2_g, v_w_up, v_mlp_conv_w, v_w_down,
                           v_final_norm_g)))

    small = _exchange([conv_w, gla_w_g2, mlp_conv_w, meta_tokens], [True] * 4, "gather_small")
    conv_full = jnp.transpose(small[0], (1, 2, 0, 3)).reshape(DEPTH, 3, CONV_CH)
    g2_full = jnp.transpose(small[1], (1, 2, 0, 3)).reshape(DEPTH, GLA_R, GLA_H * GLA_DK)
    mconv_full = jnp.transpose(small[2], (1, 2, 0, 3)).reshape(DEPTH, 3, 2 * D_FF)
    meta_full = _cols_from_slots(small[3])
    layers = [_layer_weights({}, conv_full[l], g2_full[l], mconv_full[l], norm1_g[l], fox_b_f[l], gate_b[l],
                             gla_b_g[l], gla_norm_g[l], norm2_g[l]) for l in range(DEPTH)]

    wide = ("w_a_o", "w_b_o", "w_c_o", "w_up")

    def gather(l, which):
        def make(_):
            return [wts[n][l].astype(BF16) for n in which], [("wide" if n in wide else True) for n in which]

        def done(got):
            for n, a in zip(which, got):
                if n == "w_in":
                    layers[l][n] = _w_in_slots_to_kernel(a)
                else:
                    layers[l][n] = a if n in wide else a.reshape(-1, a.shape[-1])

        return make, done

    recv_big = [dict() for _ in range(DEPTH)]

    def scatter(l, which, grads_of):
        def make(ctx):
            g = grads_of(ctx)
            send = [_w_in_kernel_to_slots(g[n]) if n == "w_in" else g[n] if n in wide else _rows_to_slots(g[n])
                    for n in which]
            return send, [("cols" if n in wide else False) for n in which]

        def done(got):
            recv_big[l].update(zip(which, got))

        return make, done

    mixers = ("w_o", "w_a_o", "w_b_o", "w_c_o")
    layers[0]["w_in"] = _w_in_slots_to_kernel(_gather_by_chip(w_in[0].astype(BF16), "gather_w_in"))
    hosts_fwd = [
        {"proj_a": gather(0, mixers), "proj_r": gather(0, ("w_up",)), "attn_fwd": gather(1, ("w_in",)),
         "up_proj": gather(0, ("w_down",)), "down_proj": gather(1, mixers)},
        {"attn_fwd": gather(1, ("w_up", "w_down"))}]
    def both(first, n_first, second):
        def make(ctx):
            (a1, b1), (a2, b2) = first[0](ctx), second[0](ctx)
            return a1 + a2, b1 + b2

        def done(got):
            first[1](got[:n_first])
            second[1](got[n_first:])

        return make, done

    hosts_bwd = [
        lambda grads: {"mlp_gate_bwd": scatter(1, ("w_in",), lambda _: grads[1]),
                       "attn_bwd": both(scatter(1, ("w_down",) + mixers, lambda _: grads[1]), 1 + len(mixers),
                                        scatter(0, ("w_down", "w_up"), lambda g: g)),
                       "d_w_in": both(scatter(0, mixers, lambda g: g), len(mixers),
                                      scatter(1, ("w_up",), lambda _: grads[1])),
                       "d_in_proj_in": scatter(0, ("w_in",), lambda g: g)},
        lambda grads: None]

    sq, grad_x, dmeta, dgf, grads_k = _local_step(x[0], loss_target[0], meta_full, final_norm_g[None], layers,
                                                  hosts_fwd, hosts_bwd)
    loss = lax.psum(sq * (0.5 / D), ("x", "y", "c"))
    grads = [_layer_grads_natural(g) for g in grads_k]

    out_g, out_d, out_m, out_v = {}, {}, {}, {}

    def update(name, recv, layer, lyr_shape, prev):
        w3, m3, v3 = (t[name].reshape(lyr_shape) for t in (wts, mom, var))
        return _adamw(recv.reshape((recv.shape[0],) + lyr_shape[1:]), w3, m3, v3, layer, prev, name="adamw_" + name)

    def store(name, res):
        shape = wts[name].shape
        out_g[name], out_d[name], out_m[name], out_v[name] = (t.reshape(shape) for t in res)

    for n in _BIG:
        res = None
        for l in range(DEPTH):
            res = update(n, recv_big[l][n], l, wts[n].shape, res)
        store(n, res)

    def stack_layers(name):
        return jnp.stack([grads[l][name] for l in range(DEPTH)])

    s_conv = jnp.transpose(stack_layers("conv_w").reshape(DEPTH, 3, N_DEV, -1), (2, 0, 1, 3))
    s_g2 = jnp.transpose(stack_layers("gla_w_g2").reshape(DEPTH, GLA_R, N_DEV, -1), (2, 0, 1, 3))
    s_mconv = jnp.transpose(stack_layers("mlp_conv_w").reshape(DEPTH, 3, N_DEV, -1), (2, 0, 1, 3))
    s_meta = _cols_to_slots(dmeta)
    repl = [stack_layers(n) for n in _REPL] + [dgf]
    pack = jnp.concatenate([jnp.pad(a.reshape(-1), (0, (-a.size) % 1024)) for a in repl]).reshape(-1, BLK)
    r_conv, r_g2, r_mconv, r_meta, r_pack = _exchange(
        [s_conv, s_g2, s_mconv, s_meta, pack], [False, False, False, False, True], "scatter_small")
    store("conv_w", update("conv_w", r_conv, 0, (1, DEPTH * 3, CONV_CH // N_DEV), None))
    store("gla_w_g2", update("gla_w_g2", r_g2, 0, (1, DEPTH * GLA_R, GLA_H * GLA_DK // N_DEV), None))
    store("mlp_conv_w", update("mlp_conv_w", r_mconv, 0, (1, DEPTH * 3, 2 * D_FF // N_DEV), None))
    store("meta_tokens", update("meta_tokens", r_meta, 0, (1, N_META, D // N_DEV), None))
    off = 0
    for n, a in zip(_REPL + ("final_norm_g",), repl):
        rows = (a.size + 1023) // 1024 * 8
        part = r_pack[:, off:off + rows].reshape(N_DEV, -1)[:, :a.size]
        off += rows
        shape2 = (1, 1, a.size) if a.size % BLK else (1, a.size // BLK, BLK)
        store(n, update(n, part, 0, shape2, None))

    order = lambda d: [d[n] for n in names]
    return (loss, grad_x[None], *order(out_g), *order(out_d), *order(out_m), *order(out_v))
```

```python
import functools

import jax
import jax.numpy as jnp
from jax import lax
from jax.experimental import pallas as pl
from jax.experimental.pallas import tpu as pltpu

F32 = jnp.float32
BF16 = jnp.bfloat16

D = 2048
DEPTH = 2
N_META = 16
BLK = 128
PAD = BLK - N_META
EPS = 1e-6
NEG = -1e30

FOX_H, FOX_DH = 8, 128
FOX_W = FOX_H * FOX_DH
CONV_CH = 1024
GLA_H, GLA_DK, GLA_DV, GLA_R, GLA_TAU = 4, 128, 256, 16, 16.0
D_FF = 5632
N_IN = 15384
N_DEV = 8

ADAM_LR, ADAM_B1, ADAM_B2, ADAM_EPS, ADAM_WD, ADAM_STEP = 0.001, 0.9, 0.999, 1e-08, 0.01, 10

CONV_TC = 512
GATE_TN = 512
KV0 = 1024
REST0 = 3072
GLA_GRP = 768
SMALL_W = 1024
NP = 16384
NREST = NP - REST0
R_CONV_BLK0 = 0
R_GLA_BLK0 = (6144 - REST0) // GLA_GRP
R_GATE_BLK0 = (9216 - REST0) // (3 * GATE_TN)
R_SMALL_BLK128 = (15360 - REST0) // 128
F_CONV_BLK0 = 3072 // (3 * CONV_TC)
F_GLA_BLK0 = 6144 // GLA_GRP
F_GATE_BLK0 = 9216 // (3 * GATE_TN)
F_SMALL_BLK0 = 15360 // SMALL_W

VMEM_LIMIT = 56 * 1024 * 1024
ADAMW_BLOCK_ELEMS = 128 * 1024


def _segments():
    seg = [(0, 1024)]
    for h in range(FOX_H):
        seg += [(1024 + 128 * h, 128), (2048 + 128 * h, 128)]
    for j in range(CONV_CH // CONV_TC):
        seg += [(3080 + CONV_TC * j, CONV_TC), (4104 + CONV_TC * j, CONV_TC), (5128 + CONV_TC * j, CONV_TC)]
    for h in range(GLA_H):
        seg += [(6152 + 128 * h, 128), (6664 + 128 * h, 128), (7176 + 256 * h, 256), (8200 + 256 * h, 256)]
    for j in range(D // GATE_TN):
        seg += [(9240 + GATE_TN * j, GATE_TN), (11288 + GATE_TN * j, GATE_TN), (13336 + GATE_TN * j, GATE_TN)]
    seg += [(3072, 8), (9224, 16)]
    return seg


def _cparams(*sem):
    return pltpu.CompilerParams(dimension_semantics=sem, vmem_limit_bytes=VMEM_LIMIT)


def _row_tile(n, target):
    best = BLK
    t = BLK
    while t <= min(n, target):
        if n % t == 0:
            best = t
        t += BLK
    return best


def _sigmoid(x):
    return 1.0 / (1.0 + jnp.exp(-x))


def _log_sigmoid(x):
    return jnp.minimum(x, 0.0) - jnp.log(1.0 + jnp.exp(-jnp.abs(x)))


def _valid_rows(row0, n):
    return (row0 + lax.broadcasted_iota(jnp.int32, (n, 1), 0)) >= PAD


def _dot(a, b):
    return jnp.dot(a, b, preferred_element_type=F32)


def _dot_nt(a, b):
    return lax.dot_general(a, b, (((1,), (1,)), ((), ())), preferred_element_type=F32)


def _dot_tn(a, b):
    return lax.dot_general(a, b, (((0,), (0,)), ((), ())), preferred_element_type=F32)


def _exchange_copies(ins, outs, bcast, send_sems, recv_sems, local_sems):
    x, y, c = lax.axis_index("x"), lax.axis_index("y"), lax.axis_index("c")
    me = 4 * x + 2 * y + c

    def src_for(n, dev):
        if bcast[n] is True or bcast[n] == "wide":
            return ins[n]
        if bcast[n] == "cols":
            w = ins[n].shape[1] // N_DEV
            return ins[n].at[:, pl.ds(pl.multiple_of(dev * w, BLK), w)]
        return ins[n].at[dev]

    def dst_of(n, dev):
        if bcast[n] == "wide":
            w = ins[n].shape[1]
            return outs[n].at[:, pl.ds(pl.multiple_of(dev * w, BLK), w)]
        return outs[n].at[dev]

    local, sends, recvs = [], [], []
    for n in range(len(ins)):
        local.append(pltpu.make_async_copy(src_for(n, me), dst_of(n, me), local_sems.at[n]))
    for k in range(1, N_DEV):
        px = 1 - x if (k >> 2) & 1 else x
        py = 1 - y if (k >> 1) & 1 else y
        pc = 1 - c if k & 1 else c
        peer = 4 * px + 2 * py + pc
        for n in range(len(ins)):
            def copy(dst_dev, n=n, k=k, to=(px, py, pc), peer=peer):
                return pltpu.make_async_remote_copy(
                    src_ref=src_for(n, peer), dst_ref=dst_of(n, dst_dev), send_sem=send_sems.at[n, k - 1],
                    recv_sem=recv_sems.at[n, k - 1], device_id=to, device_id_type=pl.DeviceIdType.MESH)

            sends.append(copy(me))
            recvs.append(copy(peer))
    return local, sends, recvs


def _exchange_start(copies):
    local, sends, _ = copies
    for cp in local + sends:
        cp.start()


def _exchange_wait(copies):
    local, sends, recvs = copies
    for cp in recvs:
        cp.wait_recv()
    for cp in sends:
        cp.wait_send()
    for cp in local:
        cp.wait()


def _exchange_shapes(arrays, bcast):
    def shape(a, b):
        if b is True:
            return (N_DEV,) + a.shape
        if b == "wide":
            return (a.shape[0], N_DEV * a.shape[1])
        if b == "cols":
            return (N_DEV, a.shape[0], a.shape[1] // N_DEV)
        return a.shape

    return [jax.ShapeDtypeStruct(shape(a, b), a.dtype) for a, b in zip(arrays, bcast)]


def _exchange_sems(n_arr):
    return [pltpu.SemaphoreType.DMA((n_arr, N_DEV - 1)), pltpu.SemaphoreType.DMA((n_arr, N_DEV - 1)),
            pltpu.SemaphoreType.DMA((n_arr,))]


def _exchange(arrays, bcast, name):
    n_arr = len(arrays)

    def body(*refs):
        copies = _exchange_copies(refs[:n_arr], refs[n_arr:2 * n_arr], bcast, *refs[2 * n_arr:])
        _exchange_start(copies)
        _exchange_wait(copies)

    hbm = pl.BlockSpec(memory_space=pltpu.HBM)
    return pl.pallas_call(
        body, out_shape=_exchange_shapes(arrays, bcast), in_specs=[hbm] * n_arr, out_specs=[hbm] * n_arr,
        scratch_shapes=_exchange_sems(n_arr),
        compiler_params=pltpu.CompilerParams(has_side_effects=True), name=name)(*arrays)


def _gather_by_chip(block, name):
    def body(x_ref, out_ref, send_sems, recv_sems, local_sem):
        x, y, c = lax.axis_index("x"), lax.axis_index("y"), lax.axis_index("c")
        me, sibling = (x, y, c), (x, y, 1 - c)
        chips = [(1 - x, y), (x, 1 - y), (1 - x, 1 - y)]

        def slot(px, py, pc):
            return out_ref.at[4 * px + 2 * py + pc]

        def copy(k, block_of, to, src=None):
            return pltpu.make_async_remote_copy(
                src_ref=slot(*block_of) if src is None else src, dst_ref=slot(*block_of),
                send_sem=send_sems.at[k], recv_sem=recv_sems.at[k], device_id=to,
                device_id_type=pl.DeviceIdType.MESH)

        mine = pltpu.make_async_copy(x_ref, slot(*me), local_sem)
        mine.start()
        first = [copy(0, me, sibling, src=x_ref)]
        first += [copy(1 + j, me, (*chip, c), src=x_ref) for j, chip in enumerate(chips)]
        for cp in first:
            cp.start()
        passed = [copy(4 + j, (*chip, c), sibling) for j, chip in enumerate(chips)]
        for j, chip in enumerate(chips):
            copy(1 + j, (*chip, c), me).wait_recv()
            passed[j].start()
        copy(0, sibling, me).wait_recv()
        for j, chip in enumerate(chips):
            copy(4 + j, (*chip, 1 - c), me).wait_recv()
        for cp in first + passed:
            cp.wait_send()
        mine.wait()

    hbm = pl.BlockSpec(memory_space=pltpu.HBM)
    return pl.pallas_call(
        body, out_shape=jax.ShapeDtypeStruct((N_DEV,) + block.shape, block.dtype), in_specs=[hbm], out_specs=hbm,
        scratch_shapes=[pltpu.SemaphoreType.DMA((N_DEV - 1,)), pltpu.SemaphoreType.DMA((N_DEV - 1,)),
                        pltpu.SemaphoreType.DMA],
        compiler_params=pltpu.CompilerParams(has_side_effects=True), name=name)(block)


def _pcall(kern, *, grid, in_specs, out_specs, out_shape, scratch, sem, name, args, aliases=None, exch=None):
    params = pltpu.CompilerParams(dimension_semantics=sem, vmem_limit_bytes=VMEM_LIMIT,
                                  has_side_effects=exch is not None)
    kw = dict(grid=grid, compiler_params=params, name=name, input_output_aliases=aliases or {})
    if exch is None:
        out = pl.pallas_call(kern, in_specs=in_specs, out_specs=out_specs, out_shape=out_shape,
                             scratch_shapes=scratch, **kw)(*args)
        return out, None
    arrays, bcast = exch
    n_x, n_in, n_out, n_sc = len(arrays), len(in_specs), len(out_specs), len(scratch)

    def hosted(*refs):
        ins, x_in = refs[:n_in], refs[n_in:n_in + n_x]
        outs, x_out = refs[n_in + n_x:n_in + n_x + n_out], refs[n_in + n_x + n_out:n_in + 2 * n_x + n_out]
        sc, sems = refs[n_in + 2 * n_x + n_out:n_in + 2 * n_x + n_out + n_sc], refs[n_in + 2 * n_x + n_out + n_sc:]
        ids = [pl.program_id(d) for d in range(len(grid))]
        first = functools.reduce(jnp.logical_and, [i == 0 for i in ids])
        last = functools.reduce(jnp.logical_and, [i == g - 1 for i, g in zip(ids, grid)])

        @pl.when(first)
        def _():
            _exchange_start(_exchange_copies(x_in, x_out, bcast, *sems))

        kern(*ins, *outs, *sc)

        @pl.when(last)
        def _():
            _exchange_wait(_exchange_copies(x_in, x_out, bcast, *sems))

    hbm = pl.BlockSpec(memory_space=pltpu.HBM)
    out = pl.pallas_call(
        hosted, in_specs=list(in_specs) + [hbm] * n_x, out_specs=list(out_specs) + [hbm] * n_x,
        out_shape=list(out_shape) + _exchange_shapes(arrays, bcast),
        scratch_shapes=list(scratch) + _exchange_sems(n_x), **kw)(*args, *arrays)
    return out[:n_out], out[n_out:]


def _mm_nn(a, b, *, n0=0, n=None, out_dtype=F32, res=None, tm=1664, tn=512, tk=None, name="mm_nn", exch=None):
    m, k = a.shape
    n = b.shape[1] - n0 if n is None else n
    tm = _row_tile(m, tm)
    tk = k if tk is None else tk
    nk = k // tk
    assert k % tk == 0 and n % tn == 0 and n0 % tn == 0
    nb0 = n0 // tn

    def kern(*refs):
        if res is None:
            a_ref, b_ref, o_ref, acc = refs
        else:
            a_ref, b_ref, r_ref, o_ref, acc = refs
        kk = pl.program_id(2)
        row0 = pl.program_id(0) * tm

        def finish(prod):
            if res is None:
                o_ref[...] = prod.astype(out_dtype)
            else:
                o_ref[...] = (r_ref[...] + jnp.where(_valid_rows(row0, tm), prod, 0.0)).astype(out_dtype)

        if nk == 1:
            finish(_dot(a_ref[...].astype(BF16), b_ref[...].astype(BF16)))
            return

        @pl.when(kk == 0)
        def _():
            acc[...] = jnp.zeros_like(acc)

        acc[...] += _dot(a_ref[...].astype(BF16), b_ref[...].astype(BF16))

        @pl.when(kk == nk - 1)
        def _():
            finish(acc[...])

    in_specs = [pl.BlockSpec((tm, tk), lambda i, j, kk: (i, kk)),
                pl.BlockSpec((tk, tn), lambda i, j, kk: (kk, nb0 + j))]
    args = [a, b]
    if res is not None:
        in_specs.append(pl.BlockSpec((tm, tn), lambda i, j, kk: (i, j)))
        args.append(res)
    out, got = _pcall(
        kern, grid=(m // tm, n // tn, nk), in_specs=in_specs,
        out_specs=[pl.BlockSpec((tm, tn), lambda i, j, kk: (i, j))],
        out_shape=[jax.ShapeDtypeStruct((m, n), out_dtype)],
        scratch=[pltpu.VMEM((tm, tn) if nk > 1 else (8, 128), F32)],
        sem=("parallel", "parallel", "arbitrary"), name=name, args=args, exch=exch)
    return out[0] if exch is None else (out[0], got)


def _mm_nt(a, b, *, k0=0, kw=None, out_dtype=F32, add=None, tm=640, tn=None, tk=2048, name="mm_nt", exch=None):
    m = a.shape[0]
    kw = a.shape[1] if kw is None else kw
    nn = b.shape[0]
    tm = _row_tile(m, tm)
    tn = min(nn, 2048) if tn is None else tn
    tk = min(tk, kw)
    assert kw % tk == 0 and k0 % tk == 0 and nn % tn == 0 and a.shape[1] == kw
    nk = kw // tk
    kb0 = k0 // tk

    def kern(*refs):
        if add is None:
            a_ref, b_ref, o_ref, acc = refs
        else:
            a_ref, b_ref, d_ref, o_ref, acc = refs
        kk = pl.program_id(2)

        def finish(prod):
            o_ref[...] = (prod if add is None else prod + d_ref[...]).astype(out_dtype)

        if nk == 1:
            finish(_dot_nt(a_ref[...].astype(BF16), b_ref[...].astype(BF16)))
            return

        @pl.when(kk == 0)
        def _():
            acc[...] = jnp.zeros_like(acc)

        acc[...] += _dot_nt(a_ref[...].astype(BF16), b_ref[...].astype(BF16))

        @pl.when(kk == nk - 1)
        def _():
            finish(acc[...])

    in_specs = [pl.BlockSpec((tm, tk), lambda i, j, kk: (i, kk)),
                pl.BlockSpec((tn, tk), lambda i, j, kk: (j, kb0 + kk))]
    args = [a, b]
    if add is not None:
        in_specs.append(pl.BlockSpec((tm, tn), lambda i, j, kk: (i, j)))
        args.append(add)
    out, got = _pcall(
        kern, grid=(m // tm, nn // tn, nk), in_specs=in_specs,
        out_specs=[pl.BlockSpec((tm, tn), lambda i, j, kk: (i, j))],
        out_shape=[jax.ShapeDtypeStruct((m, nn), out_dtype)],
        scratch=[pltpu.VMEM((tm, tn) if nk > 1 else (8, 128), F32)],
        sem=("parallel", "parallel", "arbitrary"), name=name, args=args, exch=exch)
    return out[0] if exch is None else (out[0], got)


def _mm_tn(a, b, *, out_dtype=BF16, tm=1664, tk=None, tn=None, name="mm_tn", exch=None):
    m, k = a.shape
    n = b.shape[1]
    tm = _row_tile(m, tm)
    tk = k if tk is None else tk
    if tn is None:
        tn = 1024 if n % 1024 == 0 else 512
    assert k % tk == 0 and n % tn == 0
    nm = m // tm

    def kern(a_ref, b_ref, o_ref, acc):
        mm = pl.program_id(2)

        @pl.when(mm == 0)
        def _():
            acc[...] = jnp.zeros_like(acc)

        acc[...] += _dot_tn(a_ref[...].astype(BF16), b_ref[...].astype(BF16))

        @pl.when(mm == nm - 1)
        def _():
            o_ref[...] = acc[...].astype(out_dtype)

    out, got = _pcall(
        kern, grid=(k // tk, n // tn, nm),
        in_specs=[pl.BlockSpec((tm, tk), lambda i, j, mm: (mm, i)),
                  pl.BlockSpec((tm, tn), lambda i, j, mm: (mm, j))],
        out_specs=[pl.BlockSpec((tk, tn), lambda i, j, mm: (i, j))],
        out_shape=[jax.ShapeDtypeStruct((k, n), out_dtype)],
        scratch=[pltpu.VMEM((tk, tn), F32)],
        sem=("parallel", "parallel", "arbitrary"), name=name, args=(a, b), exch=exch)
    return out[0] if exch is None else (out[0], got)


def _rmsnorm_fwd(h, g, tr=640):
    lp = h.shape[0]
    tr = _row_tile(lp, tr)

    def kern(h_ref, g_ref, o_ref):
        x = h_ref[...]
        r = lax.rsqrt(jnp.mean(x * x, axis=-1, keepdims=True) + EPS)
        o_ref[...] = (x * r * g_ref[...]).astype(BF16)

    return pl.pallas_call(
        kern, grid=(lp // tr,),
        in_specs=[pl.BlockSpec((tr, D), lambda i: (i, 0)), pl.BlockSpec((1, D), lambda i: (0, 0))],
        out_specs=pl.BlockSpec((tr, D), lambda i: (i, 0)),
        out_shape=jax.ShapeDtypeStruct((lp, D), BF16),
        compiler_params=_cparams("parallel"), name="rmsnorm_fwd")(h, g)


def _rmsnorm_bwd(h, g, dxn, dres, tr=640):
    lp = h.shape[0]
    tr = _row_tile(lp, tr)

    def kern(h_ref, g_ref, dxn_ref, dres_ref, dh_ref, dg_ref):
        i = pl.program_id(0)
        x = h_ref[...]
        r = lax.rsqrt(jnp.mean(x * x, axis=-1, keepdims=True) + EPS)
        xhat = x * r
        dy = jnp.where(_valid_rows(i * tr, tr), dxn_ref[...], 0.0)

        @pl.when(i == 0)
        def _():
            dg_ref[...] = jnp.zeros_like(dg_ref)

        dg_ref[...] += jnp.sum(dy * xhat, axis=0, keepdims=True)
        dxh = dy * g_ref[...]
        dh_ref[...] = dres_ref[...] + r * (dxh - xhat * jnp.mean(dxh * xhat, axis=-1, keepdims=True))

    row = pl.BlockSpec((tr, D), lambda i: (i, 0))
    vec = pl.BlockSpec((1, D), lambda i: (0, 0))
    return pl.pallas_call(
        kern, grid=(lp // tr,), in_specs=[row, vec, row, row], out_specs=[row, vec],
        out_shape=[jax.ShapeDtypeStruct((lp, D), F32), jax.ShapeDtypeStruct((1, D), F32)],
        compiler_params=_cparams("arbitrary"), name="rmsnorm_bwd")(h, g, dxn, dres)


def _shift_down(xe, k):
    return xe if k == 0 else pltpu.roll(xe, k, 0)


def _shift_up(xe, k):
    return xe if k == 0 else pltpu.roll(xe, xe.shape[0] - k, 0)


def _conv_ext(xe, w_ref):
    return w_ref[2:3, :] * xe + w_ref[1:2, :] * _shift_down(xe, 1) + w_ref[0:1, :] * _shift_down(xe, 2)


def _halo_specs(tr, width, col_of, nrows, rows_first, halo=8):
    r8 = tr // halo
    last8 = nrows // halo - 1
    if rows_first:
        prev = pl.BlockSpec((halo, width), lambda i, j: (jnp.maximum(i * r8 - 1, 0), col_of(j)))
        nxt = pl.BlockSpec((halo, width), lambda i, j: (jnp.minimum((i + 1) * r8, last8), col_of(j)))
    else:
        prev = pl.BlockSpec((halo, width), lambda j, i: (jnp.maximum(i * r8 - 1, 0), col_of(j)))
        nxt = pl.BlockSpec((halo, width), lambda j, i: (jnp.minimum((i + 1) * r8, last8), col_of(j)))
    return prev, nxt


def _convb_fwd(proj_r, conv_w, tr=640):
    lp = proj_r.shape[0]
    tr = _row_tile(lp, tr)
    tc = CONV_TC
    gw = 3 * tc

    def kern(g_ref, gp_ref, w_ref, o_ref):
        i = pl.program_id(0)
        g = g_ref[...]
        p = g[:, tc:2 * tc] * g[:, 2 * tc:]
        gp = gp_ref[...]
        pp = jnp.where(i > 0, gp[:, tc:2 * tc] * gp[:, 2 * tc:], 0.0)
        y = _conv_ext(jnp.concatenate([pp, p], axis=0), w_ref)[8:]
        o_ref[...] = (g[:, :tc] * y).astype(BF16)

    prev, _ = _halo_specs(tr, gw, lambda j: R_CONV_BLK0 + j, lp, True)
    return pl.pallas_call(
        kern, grid=(lp // tr, CONV_CH // tc),
        in_specs=[pl.BlockSpec((tr, gw), lambda i, j: (i, R_CONV_BLK0 + j)), prev,
                  pl.BlockSpec((3, tc), lambda i, j: (0, j))],
        out_specs=pl.BlockSpec((tr, tc), lambda i, j: (i, j)),
        out_shape=jax.ShapeDtypeStruct((lp, CONV_CH), BF16),
        compiler_params=_cparams("parallel", "parallel"), name="convb_fwd")(proj_r, proj_r, conv_w)


def _convb_bwd(proj_r, dzb, conv_w, dproj, tr=640):
    lp = proj_r.shape[0]
    tr = _row_tile(lp, tr)
    nr = lp // tr
    tc = CONV_TC
    gw = 3 * tc

    def kern(g_ref, gp_ref, gn_ref, dz_ref, dzn_ref, w_ref, dp_any, dg_ref, dw_ref):
        del dp_any
        i = pl.program_id(1)
        g = g_ref[...]
        b, c, hh = g[:, :tc], g[:, tc:2 * tc], g[:, 2 * tc:]
        p = c * hh
        gp = gp_ref[...]
        pp = jnp.where(i > 0, gp[:, tc:2 * tc] * gp[:, 2 * tc:], 0.0)
        pe = jnp.concatenate([pp, p], axis=0)
        s1 = _shift_down(pe, 1)[8:]
        s2 = _shift_down(pe, 2)[8:]
        y = w_ref[2:3, :] * p + w_ref[1:2, :] * s1 + w_ref[0:1, :] * s2
        dz = dz_ref[...]
        dy = dz * b
        dyn = jnp.where(i < nr - 1, dzn_ref[...] * gn_ref[...][:, :tc], 0.0)
        dye = jnp.concatenate([dy, dyn], axis=0)
        dp = (w_ref[2:3, :] * dy + w_ref[1:2, :] * _shift_up(dye, 1)[:tr]
              + w_ref[0:1, :] * _shift_up(dye, 2)[:tr])
        valid = _valid_rows(i * tr, tr)
        dg_ref[...] = jnp.where(valid, jnp.concatenate([dz * y, dp * hh, dp * c], axis=1), 0.0).astype(BF16)

        @pl.when(i == 0)
        def _():
            dw_ref[...] = jnp.zeros_like(dw_ref)

        dw_ref[0:1, :] += jnp.sum(dy * s2, axis=0, keepdims=True)
        dw_ref[1:2, :] += jnp.sum(dy * s1, axis=0, keepdims=True)
        dw_ref[2:3, :] += jnp.sum(dy * p, axis=0, keepdims=True)

    gprev, gnext = _halo_specs(tr, gw, lambda j: R_CONV_BLK0 + j, lp, False)
    _, dznext = _halo_specs(tr, tc, lambda j: j, lp, False)
    return pl.pallas_call(
        kern, grid=(CONV_CH // tc, nr),
        in_specs=[pl.BlockSpec((tr, gw), lambda j, i: (i, R_CONV_BLK0 + j)), gprev, gnext,
                  pl.BlockSpec((tr, tc), lambda j, i: (i, j)), dznext,
                  pl.BlockSpec((3, tc), lambda j, i: (0, j)),
                  pl.BlockSpec(memory_space=pl.ANY)],
        out_specs=[pl.BlockSpec((tr, gw), lambda j, i: (i, F_CONV_BLK0 + j)),
                   pl.BlockSpec((3, tc), lambda j, i: (0, j))],
        out_shape=[jax.ShapeDtypeStruct(dproj.shape, BF16), jax.ShapeDtypeStruct((3, CONV_CH), F32)],
        input_output_aliases={6: 0},
        compiler_params=_cparams("parallel", "arbitrary"), name="convb_bwd",
    )(proj_r, proj_r, proj_r, dzb, dzb, conv_w, dproj)


MLP_TC = 256
MLP_HALO = 16


def _mlp_gate_fwd(z, w, tr=640):
    lp = z.shape[0]
    tr = _row_tile(lp, tr)
    tc = 512
    nc = D_FF // tc

    def kern(zg_ref, zgp_ref, zu_ref, zup_ref, wg_ref, wu_ref, o_ref):
        i = pl.program_id(0)

        def ext(m_ref, p_ref):
            return jnp.concatenate([jnp.where(i > 0, p_ref[...].astype(F32), 0.0), m_ref[...].astype(F32)], axis=0)

        ug = _conv_ext(ext(zg_ref, zgp_ref), wg_ref)[MLP_HALO:]
        uu = _conv_ext(ext(zu_ref, zup_ref), wu_ref)[MLP_HALO:]
        o_ref[...] = (ug * _sigmoid(ug) * uu).astype(BF16)

    gprev, _ = _halo_specs(tr, tc, lambda j: j, lp, True, MLP_HALO)
    uprev, _ = _halo_specs(tr, tc, lambda j: nc + j, lp, True, MLP_HALO)
    return pl.pallas_call(
        kern, grid=(lp // tr, nc),
        in_specs=[pl.BlockSpec((tr, tc), lambda i, j: (i, j)), gprev,
                  pl.BlockSpec((tr, tc), lambda i, j: (i, nc + j)), uprev,
                  pl.BlockSpec((3, tc), lambda i, j: (0, j)),
                  pl.BlockSpec((3, tc), lambda i, j: (0, nc + j))],
        out_specs=pl.BlockSpec((tr, tc), lambda i, j: (i, j)),
        out_shape=jax.ShapeDtypeStruct((lp, D_FF), BF16),
        compiler_params=_cparams("parallel", "parallel"), name="mlp_gate_fwd")(z, z, z, z, w, w)


def _mlp_gate_bwd(z, da, w, tr=640, exch=None):
    lp = z.shape[0]
    tr = _row_tile(lp, tr)
    nr = lp // tr
    tc = MLP_TC
    nc = D_FF // tc

    def kern(zg_ref, zgp_ref, zgn_ref, zu_ref, zup_ref, zun_ref, da_ref, dan_ref, wg_ref, wu_ref,
             dzg_ref, dzu_ref, dwg_ref, dwu_ref):
        i = pl.program_id(1)
        first, last = i == 0, i == nr - 1

        hl = MLP_HALO

        def ext(m_ref, p_ref, n_ref):
            return jnp.concatenate([jnp.where(first, 0.0, p_ref[...].astype(F32)), m_ref[...].astype(F32),
                                    jnp.where(last, 0.0, n_ref[...].astype(F32))], axis=0)

        zge, zue = ext(zg_ref, zgp_ref, zgn_ref), ext(zu_ref, zup_ref, zun_ref)
        ug = _conv_ext(zge, wg_ref)[hl:]
        uu = _conv_ext(zue, wu_ref)[hl:]
        dae = jnp.concatenate([da_ref[...].astype(F32), jnp.where(last, 0.0, dan_ref[...].astype(F32))], axis=0)
        sg = _sigmoid(ug)
        dug = dae * uu * (sg * (1.0 + ug * (1.0 - sg)))
        duu = dae * (ug * sg)
        valid = _valid_rows(i * tr, tr)

        @pl.when(first)
        def _():
            dwg_ref[...] = jnp.zeros_like(dwg_ref)
            dwu_ref[...] = jnp.zeros_like(dwu_ref)

        for du, ze, w_ref, dz_ref, dw_ref in ((dug, zge, wg_ref, dzg_ref, dwg_ref),
                                              (duu, zue, wu_ref, dzu_ref, dwu_ref)):
            dz = (w_ref[2:3, :] * du + w_ref[1:2, :] * _shift_up(du, 1) + w_ref[0:1, :] * _shift_up(du, 2))[:tr]
            dz_ref[...] = jnp.where(valid, dz, 0.0).astype(BF16)
            dum = du[:tr]
            for kk in range(3):
                dw_ref[kk:kk + 1, :] += jnp.sum(dum * _shift_down(ze, 2 - kk)[hl:hl + tr], axis=0, keepdims=True)

    gprev, gnext = _halo_specs(tr, tc, lambda j: j, lp, False, MLP_HALO)
    uprev, unext = _halo_specs(tr, tc, lambda j: nc + j, lp, False, MLP_HALO)
    main = pl.BlockSpec((tr, tc), lambda j, i: (i, j))
    wspec = pl.BlockSpec((3, tc), lambda j, i: (0, j))
    out, got = _pcall(
        kern, grid=(nc, nr),
        in_specs=[main, gprev, gnext, pl.BlockSpec((tr, tc), lambda j, i: (i, nc + j)), uprev, unext,
                  main, gnext, wspec, pl.BlockSpec((3, tc), lambda j, i: (0, nc + j))],
        out_specs=[main, main, wspec, wspec],
        out_shape=[jax.ShapeDtypeStruct((lp, D_FF), BF16), jax.ShapeDtypeStruct((lp, D_FF), BF16),
                   jax.ShapeDtypeStruct((3, D_FF), F32), jax.ShapeDtypeStruct((3, D_FF), F32)],
        scratch=[], sem=("parallel", "arbitrary"), name="mlp_gate_bwd",
        args=(z, z, z, z, z, z, da, da, w, w), exch=exch)
    return (*out, got)


def _tri(n, lower):
    r = lax.broadcasted_iota(jnp.int32, (n, n), 0)
    c = lax.broadcasted_iota(jnp.int32, (n, n), 1)
    return jnp.where((c <= r) if lower else (c >= r), 1.0, 0.0).astype(F32)


def _dot_exact(a, b):
    return jnp.dot(a, b, preferred_element_type=F32, precision=lax.Precision.HIGHEST)


def _fox_gate_fwd(proj_r, bf128):
    lp = proj_r.shape[0]
    nb = lp // BLK

    def kern(s_ref, b_ref, c_ref):
        tri = _tri(BLK, True)

        def body(i, carry):
            rows = pl.ds(pl.multiple_of(i * BLK, BLK), BLK)
            lf = jnp.where(_valid_rows(i * BLK, BLK), _log_sigmoid(s_ref[rows, :] + b_ref[...]), 0.0)
            cs = _dot_exact(tri, lf) + carry
            c_ref[rows, :] = cs
            return cs[BLK - 1:BLK, :]

        lax.fori_loop(0, nb, body, jnp.zeros((1, BLK), F32))

    return pl.pallas_call(
        kern, grid=(1,),
        in_specs=[pl.BlockSpec((lp, BLK), lambda i: (0, R_SMALL_BLK128)), pl.BlockSpec((1, BLK), lambda i: (0, 0))],
        out_specs=pl.BlockSpec((lp, BLK), lambda i: (0, 0)),
        out_shape=jax.ShapeDtypeStruct((lp, BLK), F32),
        compiler_params=_cparams("arbitrary"), name="fox_gate_fwd")(proj_r, bf128)


def _fox_gate_bwd(proj_r, dc, bf128):
    lp = proj_r.shape[0]
    nb = lp // BLK

    def kern(s_ref, dc_ref, b_ref, dfa_ref, dbf_ref):
        tri = _tri(BLK, False)

        dbf_ref[...] = jnp.zeros_like(dbf_ref)

        def body(ii, run):
            i = nb - 1 - ii
            rows = pl.ds(pl.multiple_of(i * BLK, BLK), BLK)
            dcb = dc_ref[rows, :]
            suf = _dot_exact(tri, dcb) + run
            dfa = jnp.where(_valid_rows(i * BLK, BLK), suf * _sigmoid(-(s_ref[rows, :] + b_ref[...])), 0.0)
            dfa_ref[rows, :] = dfa
            dbf_ref[...] += jnp.sum(dfa, axis=0, keepdims=True)
            return run + jnp.sum(dcb, axis=0, keepdims=True)

        lax.fori_loop(0, nb, body, jnp.zeros((1, BLK), F32))

    return pl.pallas_call(
        kern, grid=(1,),
        in_specs=[pl.BlockSpec((lp, BLK), lambda i: (0, R_SMALL_BLK128)), pl.BlockSpec((lp, BLK), lambda i: (0, 0)),
                  pl.BlockSpec((1, BLK), lambda i: (0, 0))],
        out_specs=[pl.BlockSpec((lp, BLK), lambda i: (0, 0)), pl.BlockSpec((1, BLK), lambda i: (0, 0))],
        out_shape=[jax.ShapeDtypeStruct((lp, BLK), F32), jax.ShapeDtypeStruct((1, BLK), F32)],
        compiler_params=_cparams("arbitrary"), name="fox_gate_bwd")(proj_r, dc, bf128)


LOG2E = 1.4426950408889634
KEY_PAD_BIAS = 1e30


def _attn_logits2(q, k, ck, diag):
    t = _dot_nt(q, k) * (LOG2E * FOX_DH ** -0.5) - ck * LOG2E
    if diag:
        r = lax.broadcasted_iota(jnp.int32, t.shape, 0)
        c = lax.broadcasted_iota(jnp.int32, t.shape, 1)
        t = jnp.where(c <= r, t, NEG)
    return t


ATTN_HEADS = 2
ATTN_HEADS_FWD = 4


def _head_cols(a):
    return (slice(a * FOX_DH, (a + 1) * FOX_DH), slice(2 * a * FOX_DH, (2 * a + 1) * FOX_DH),
            slice((2 * a + 1) * FOX_DH, (2 * a + 2) * FOX_DH))


def _on_blocks(i, j, step):
    pl.when(j < i)(functools.partial(step, False))
    pl.when(j == i)(functools.partial(step, True))


def _attn_fwd(proj_a, cq, ck, tq=640, exch=None):
    lp = proj_a.shape[0]
    tq = _row_tile(lp, tq)
    tk = tq
    nq = lp // tq

    def kern(q_ref, kv_ref, cq_ref, ck_ref, o_ref, lse_ref, m_sc, l_sc, acc):
        i, j = pl.program_id(1), pl.program_id(2)

        @pl.when(j == 0)
        def _():
            m_sc[...] = jnp.full_like(m_sc, -jnp.inf)
            l_sc[...] = jnp.zeros_like(l_sc)
            acc[...] = jnp.zeros_like(acc)

        def step(diag):
            heads = range(ATTN_HEADS_FWD)
            cols = [_head_cols(a) for a in heads]
            m_old = [m_sc[a] for a in heads]
            l_old = [l_sc[a] for a in heads]
            acc_old = [acc[:, cols[a][0]] for a in heads]
            cq2 = [cq_ref[a] * LOG2E for a in heads]
            t = [_attn_logits2(q_ref[:, cols[a][0]], kv_ref[:, cols[a][1]], ck_ref[a], diag) for a in heads]
            m_new = [jnp.maximum(m_old[a], jnp.max(t[a], axis=-1, keepdims=True) + cq2[a]) for a in heads]
            p = [jnp.exp2(t[a] + (cq2[a] - m_new[a])) for a in heads]
            alpha = [jnp.exp2(m_old[a] - m_new[a]) for a in heads]
            l_new = [alpha[a] * l_old[a] + jnp.sum(p[a], axis=-1, keepdims=True) for a in heads]
            acc_new = [alpha[a] * acc_old[a] + _dot(p[a].astype(BF16), kv_ref[:, cols[a][2]]) for a in heads]
            for a in heads:
                m_sc[a] = m_new[a]
                l_sc[a] = l_new[a]
                acc[:, cols[a][0]] = acc_new[a]

        _on_blocks(i, j, step)

        @pl.when(j == nq - 1)
        def _():
            valid = _valid_rows(i * tq, tq)
            for a in range(ATTN_HEADS_FWD):
                hq, _, _ = _head_cols(a)
                o_ref[:, hq] = jnp.where(valid, acc[:, hq] / l_sc[a], 0.0).astype(BF16)
                lse_ref[a] = m_sc[a] + jnp.log(l_sc[a]) * LOG2E

    hp = ATTN_HEADS_FWD
    out, got = _pcall(
        kern, grid=(FOX_H // hp, nq, nq),
        in_specs=[pl.BlockSpec((tq, hp * FOX_DH), lambda h, i, j: (i, h)),
                  pl.BlockSpec((tk, 2 * hp * FOX_DH), lambda h, i, j: (jnp.minimum(j, i), KV0 // (2 * hp * FOX_DH) + h)),
                  pl.BlockSpec((hp, tq, 1), lambda h, i, j: (h, i, 0)),
                  pl.BlockSpec((hp, 1, tk), lambda h, i, j: (h, 0, jnp.minimum(j, i)))],
        out_specs=[pl.BlockSpec((tq, hp * FOX_DH), lambda h, i, j: (i, h)),
                   pl.BlockSpec((hp, tq, 1), lambda h, i, j: (h, i, 0))],
        out_shape=[jax.ShapeDtypeStruct((lp, FOX_W), BF16), jax.ShapeDtypeStruct((FOX_H, lp, 1), F32)],
        scratch=[pltpu.VMEM((hp, tq, 1), F32), pltpu.VMEM((hp, tq, 1), F32), pltpu.VMEM((tq, hp * FOX_DH), F32)],
        sem=("parallel", "parallel", "arbitrary"), name="attn_fwd", args=(proj_a, proj_a, cq, ck), exch=exch)
    return out[0], out[1], got


def _attn_bwd(proj_a, do, o, lse, cq, ck, dproj, tq=640, exch=None):
    lp = proj_a.shape[0]
    tq = _row_tile(lp, tq)
    tk = tq
    nq = lp // tq
    hp = ATTN_HEADS
    wq = hp * FOX_DH

    def kern(q_ref, kv_ref, do_ref, o_ref, lse_ref, cq_ref, ck_ref, dp_any, dproj_ref, dcq_ref, dck_ref,
             dk_acc, dv_acc, dq_acc, dcq_acc, dq_stage, dkv_stage, sems):
        del dp_any
        h, j, i = pl.program_id(0), pl.program_id(1), pl.program_id(2)
        rows = pl.ds(pl.multiple_of(i * tq, tq), tq)

        @pl.when(i == 0)
        def _():
            dck_ref[...] = jnp.zeros_like(dck_ref)
            dk_acc[...] = jnp.zeros_like(dk_acc)
            dv_acc[...] = jnp.zeros_like(dv_acc)

        @pl.when(jnp.logical_and(i == 0, j == 0))
        def _():
            dq_acc[...] = jnp.zeros_like(dq_acc)
            dcq_acc[...] = jnp.zeros_like(dcq_acc)

        def step(diag):
            heads = range(hp)
            cols = [_head_cols(a) for a in heads]
            dck_old = [dck_ref[a] for a in heads]
            dcq_old = [dcq_acc[a, rows, :] for a in heads]
            dk_old = [dk_acc[:, cols[a][0]] for a in heads]
            dv_old = [dv_acc[:, cols[a][0]] for a in heads]
            dq_old = [dq_acc[rows, cols[a][0]] for a in heads]
            shift = [cq_ref[a] * LOG2E - lse_ref[a] for a in heads]
            do_h = [do_ref[:, cols[a][0]] for a in heads]
            delta = [jnp.sum(do_h[a].astype(F32) * o_ref[:, cols[a][0]].astype(F32), axis=-1, keepdims=True)
                     for a in heads]
            t = [_attn_logits2(q_ref[:, cols[a][0]], kv_ref[:, cols[a][1]], ck_ref[a], diag) for a in heads]
            dp = [_dot_nt(do_h[a], kv_ref[:, cols[a][2]]) for a in heads]
            p = [jnp.exp2(t[a] + shift[a]) for a in heads]
            ds = [p[a] * (dp[a] - delta[a]) for a in heads]
            dsb = [ds[a].astype(BF16) for a in heads]
            dv_new = [dv_old[a] + _dot_tn(p[a].astype(BF16), do_h[a]) for a in heads]
            dck_new = [dck_old[a] - jnp.sum(ds[a], axis=0, keepdims=True) for a in heads]
            dcq_new = [dcq_old[a] + jnp.sum(ds[a], axis=-1, keepdims=True) for a in heads]
            dk_new = [dk_old[a] + _dot_tn(dsb[a], q_ref[:, cols[a][0]]) for a in heads]
            dq_new = [dq_old[a] + _dot(dsb[a], kv_ref[:, cols[a][1]]) for a in heads]
            for a in heads:
                dck_ref[a] = dck_new[a]
                dcq_acc[a, rows, :] = dcq_new[a]
                dk_acc[:, cols[a][0]] = dk_new[a]
                dv_acc[:, cols[a][0]] = dv_new[a]
                dq_acc[rows, cols[a][0]] = dq_new[a]

        _on_blocks(i, j, step)

        @pl.when(i == nq - 1)
        def _():
            parts = []
            for a in range(hp):
                hq, _, _ = _head_cols(a)
                parts += [dk_acc[:, hq] * (FOX_DH ** -0.5), dv_acc[:, hq]]
            dkv_stage[...] = jnp.concatenate(parts, axis=1).astype(BF16)
            out = pltpu.make_async_copy(
                dkv_stage, dproj_ref.at[pl.ds(pl.multiple_of(j * tk, tk), tk),
                                        pl.ds(pl.multiple_of(KV0 + h * 2 * wq, 2 * wq), 2 * wq)], sems.at[0])
            out.start()
            out.wait()

        @pl.when(jnp.logical_and(i == nq - 1, j == nq - 1))
        def _():
            dq_stage[...] = (dq_acc[...] * (FOX_DH ** -0.5)).astype(BF16)
            out = pltpu.make_async_copy(dq_stage, dproj_ref.at[:, pl.ds(pl.multiple_of(h * wq, wq), wq)], sems.at[1])
            rowsums = pltpu.make_async_copy(dcq_acc, dcq_ref.at[pl.ds(h * hp, hp)], sems.at[2])
            out.start()
            rowsums.start()
            out.wait()
            rowsums.wait()

    qspec = pl.BlockSpec((tq, wq), lambda h, j, i: (jnp.maximum(i, j), h))
    col = pl.BlockSpec((hp, tq, 1), lambda h, j, i: (h, jnp.maximum(i, j), 0))
    kvspec = pl.BlockSpec((tk, 2 * wq), lambda h, j, i: (j, KV0 // (2 * wq) + h))
    rowspec = pl.BlockSpec((hp, 1, tk), lambda h, j, i: (h, 0, j))
    out, got = _pcall(
        kern, grid=(FOX_H // hp, nq, nq),
        in_specs=[qspec, kvspec, qspec, qspec, col, col, rowspec, pl.BlockSpec(memory_space=pl.ANY)],
        out_specs=[pl.BlockSpec(memory_space=pl.ANY), pl.BlockSpec(memory_space=pl.ANY), rowspec],
        out_shape=[jax.ShapeDtypeStruct(dproj.shape, BF16), jax.ShapeDtypeStruct((FOX_H, lp, 1), F32),
                   jax.ShapeDtypeStruct((FOX_H, 1, lp), F32)],
        scratch=[pltpu.VMEM((tk, wq), F32), pltpu.VMEM((tk, wq), F32), pltpu.VMEM((lp, wq), F32),
                 pltpu.VMEM((hp, lp, 1), F32), pltpu.VMEM((lp, wq), BF16), pltpu.VMEM((tk, 2 * wq), BF16),
                 pltpu.SemaphoreType.DMA((3,))],
        aliases={7: 0}, sem=("arbitrary", "arbitrary", "arbitrary"), name="attn_bwd",
        args=(proj_a, proj_a, do, o, lse, cq, ck, dproj), exch=exch)
    return out[0], out[1], out[2], got


def _gla_gate_fwd(proj_r, wg2p, bg, tr=640):
    lp = proj_r.shape[0]
    tr = _row_tile(lp, tr)
    w = GLA_H * GLA_DK

    def kern(s_ref, w_ref, b_ref, o_ref):
        zg = _dot(s_ref[...].astype(BF16), w_ref[...]) + b_ref[...]
        o_ref[...] = jnp.where(_valid_rows(pl.program_id(0) * tr, tr), _log_sigmoid(zg) * (1.0 / GLA_TAU), 0.0)

    return pl.pallas_call(
        kern, grid=(lp // tr,),
        in_specs=[pl.BlockSpec((tr, BLK), lambda i: (i, R_SMALL_BLK128)), pl.BlockSpec((BLK, w), lambda i: (0, 0)),
                  pl.BlockSpec((1, w), lambda i: (0, 0))],
        out_specs=pl.BlockSpec((tr, w), lambda i: (i, 0)),
        out_shape=jax.ShapeDtypeStruct((lp, w), F32),
        compiler_params=_cparams("parallel"), name="gla_gate_fwd")(proj_r, wg2p, bg)


def _gla_chunk(grp, g):
    q = grp[:, :GLA_DK] * (GLA_DK ** -0.5)
    k = grp[:, GLA_DK:2 * GLA_DK]
    v = grp[:, 2 * GLA_DK:2 * GLA_DK + GLA_DV]
    r = grp[:, 2 * GLA_DK + GLA_DV:]
    b = _dot_exact(_tri(BLK, True), g)
    bl = b[BLK - 1:BLK, :]
    eb = jnp.exp(b)
    enb = jnp.exp(-b)
    ebl = jnp.exp(bl - b)
    qe, ke, kd = q * eb, k * enb, k * ebl
    causal = lax.broadcasted_iota(jnp.int32, (BLK, BLK), 1) <= lax.broadcasted_iota(jnp.int32, (BLK, BLK), 0)
    att = jnp.where(causal, _dot_nt(qe.astype(BF16), ke.astype(BF16)), 0.0)
    return q, k, v, r, bl, eb, enb, ebl, qe, ke, kd, causal, att


def _gla_fwd(proj_r, logg, gn):
    lp = proj_r.shape[0]
    nc = lp // BLK
    wv = GLA_H * GLA_DV

    def kern(grp_ref, g_ref, gn_ref, o_ref, zc_ref, st_ref, st):
        c = pl.program_id(0)

        @pl.when(c == 0)
        def _():
            st[...] = jnp.zeros_like(st)

        for h in range(GLA_H):
            kcol = slice(h * GLA_DK, (h + 1) * GLA_DK)
            vcol = slice(h * GLA_DV, (h + 1) * GLA_DV)
            q, k, v, r, bl, eb, enb, ebl, qe, ke, kd, causal, att = _gla_chunk(
                grp_ref[:, h * GLA_GRP:(h + 1) * GLA_GRP], g_ref[:, kcol])
            s_t = st[h]
            st_ref[h] = s_t
            vb = v.astype(BF16)
            o = _dot(att.astype(BF16), vb) + _dot_nt(qe.astype(BF16), s_t.astype(BF16))
            st[h] = s_t * jnp.exp(bl) + _dot_tn(vb, kd.astype(BF16))
            o_ref[:, vcol] = o
            rstd = lax.rsqrt(jnp.mean(o * o, axis=-1, keepdims=True) + EPS)
            zc_ref[:, vcol] = (r * _sigmoid(r) * (o * rstd * gn_ref[:, vcol])).astype(BF16)

    vspec = pl.BlockSpec((BLK, wv), lambda c: (c, 0))
    return pl.pallas_call(
        kern, grid=(nc,),
        in_specs=[pl.BlockSpec((BLK, GLA_H * GLA_GRP), lambda c: (c, R_GLA_BLK0 // GLA_H)),
                  pl.BlockSpec((BLK, GLA_H * GLA_DK), lambda c: (c, 0)),
                  pl.BlockSpec((1, wv), lambda c: (0, 0))],
        out_specs=[vspec, vspec, pl.BlockSpec((GLA_H, None, GLA_DV, GLA_DK), lambda c: (0, c, 0, 0))],
        out_shape=[jax.ShapeDtypeStruct((lp, wv), F32), jax.ShapeDtypeStruct((lp, wv), BF16),
                   jax.ShapeDtypeStruct((GLA_H, nc, GLA_DV, GLA_DK), F32)],
        scratch_shapes=[pltpu.VMEM((GLA_H, GLA_DV, GLA_DK), F32)],
        compiler_params=_cparams("arbitrary"), name="gla_fwd")(proj_r, logg, gn)


def _gla_bwd(proj_r, logg, st_all, o_all, dzc, gn, dproj):
    lp = proj_r.shape[0]
    nc = lp // BLK

    def kern(grp_ref, g_ref, st_ref, o_ref, dzc_ref, gn_ref, dp_any, dgrp_ref, dlg_ref, dgn_ref, dst):
        del dp_any
        cc = pl.program_id(0)

        @pl.when(cc == 0)
        def _():
            dst[...] = jnp.zeros_like(dst)
            dgn_ref[...] = jnp.zeros_like(dgn_ref)

        for h in range(GLA_H):
            kcol = slice(h * GLA_DK, (h + 1) * GLA_DK)
            vcol = slice(h * GLA_DV, (h + 1) * GLA_DV)
            q, k, v, r, bl, eb, enb, ebl, qe, ke, kd, causal, att = _gla_chunk(
                grp_ref[:, h * GLA_GRP:(h + 1) * GLA_GRP], g_ref[:, kcol])
            s_t = st_ref[h]
            d_st = dst[h]
            o = o_ref[:, vcol]
            dzc_v = dzc_ref[:, vcol]
            gnv = gn_ref[:, vcol]
            rstd = lax.rsqrt(jnp.mean(o * o, axis=-1, keepdims=True) + EPS)
            xhat = o * rstd
            sr = _sigmoid(r)
            dr = dzc_v * (xhat * gnv) * (sr * (1.0 + r * (1.0 - sr)))
            docn = dzc_v * (r * sr)
            dgn_ref[:, vcol] += jnp.sum(docn * xhat, axis=0, keepdims=True)
            dxh = docn * gnv
            do = rstd * (dxh - xhat * jnp.mean(dxh * xhat, axis=-1, keepdims=True))
            dob, vb = do.astype(BF16), v.astype(BF16)
            qeb, keb, kdb = qe.astype(BF16), ke.astype(BF16), kd.astype(BF16)
            datt = jnp.where(causal, _dot_nt(dob, vb), 0.0).astype(BF16)
            dv = _dot_tn(att.astype(BF16), dob) + _dot_nt(kdb, d_st.astype(BF16))
            dqe = _dot(datt, keb) + _dot(dob, s_t.astype(BF16))
            dke = _dot_tn(datt, qeb)
            dkd = _dot(vb, d_st.astype(BF16))
            dq = dqe * eb * (GLA_DK ** -0.5)
            dk = dke * enb + dkd * ebl
            kd_dkd = dkd * kd
            db = dqe * qe - dke * ke - kd_dkd
            db_last = (jnp.sum(kd_dkd, axis=0, keepdims=True)
                       + jnp.exp(bl) * jnp.sum(s_t * d_st, axis=0, keepdims=True))
            dlg_ref[:, kcol] = _dot_exact(_tri(BLK, False), db) + db_last
            dst[h] = d_st * jnp.exp(bl) + _dot_tn(dob, qeb)
            dgrp_ref[:, h * GLA_GRP:(h + 1) * GLA_GRP] = jnp.concatenate([dq, dk, dv, dr], axis=1).astype(BF16)

    rev = lambda c: nc - 1 - c
    wv = GLA_H * GLA_DV
    vspec = pl.BlockSpec((BLK, wv), lambda c: (rev(c), 0))
    kspec = pl.BlockSpec((BLK, GLA_H * GLA_DK), lambda c: (rev(c), 0))
    return pl.pallas_call(
        kern, grid=(nc,),
        in_specs=[pl.BlockSpec((BLK, GLA_H * GLA_GRP), lambda c: (rev(c), R_GLA_BLK0 // GLA_H)), kspec,
                  pl.BlockSpec((GLA_H, None, GLA_DV, GLA_DK), lambda c: (0, rev(c), 0, 0)),
                  vspec, vspec, pl.BlockSpec((1, wv), lambda c: (0, 0)),
                  pl.BlockSpec(memory_space=pl.ANY)],
        out_specs=[pl.BlockSpec((BLK, GLA_H * GLA_GRP), lambda c: (rev(c), F_GLA_BLK0 // GLA_H)), kspec,
                   pl.BlockSpec((1, wv), lambda c: (0, 0))],
        out_shape=[jax.ShapeDtypeStruct(dproj.shape, BF16), jax.ShapeDtypeStruct((lp, GLA_H * GLA_DK), F32),
                   jax.ShapeDtypeStruct((1, wv), F32)],
        scratch_shapes=[pltpu.VMEM((GLA_H, GLA_DV, GLA_DK), F32)],
        input_output_aliases={6: 0},
        compiler_params=_cparams("arbitrary"), name="gla_bwd",
    )(proj_r, logg, st_all, o_all, dzc, gn, dproj)


def _small_bwd(proj_r, dlogg, wg2p, wg2pt, bg, dfa, dproj, tr=640):
    lp = proj_r.shape[0]
    tr = _row_tile(lp, tr)
    w = GLA_H * GLA_DK

    def kern(s_ref, dlg_ref, w_ref, wt_ref, b_ref, dfa_ref, dp_any, ds_ref, dbg_ref, dw_ref):
        del dp_any
        i = pl.program_id(0)
        sb = s_ref[...].astype(BF16)
        zg = _dot(sb, w_ref[...]) + b_ref[...]
        dzg = jnp.where(_valid_rows(i * tr, tr), dlg_ref[...] * (1.0 / GLA_TAU) * _sigmoid(-zg), 0.0)

        @pl.when(i == 0)
        def _():
            dbg_ref[...] = jnp.zeros_like(dbg_ref)
            dw_ref[...] = jnp.zeros_like(dw_ref)

        dbg_ref[...] += jnp.sum(dzg, axis=0, keepdims=True)
        dzb = dzg.astype(BF16)
        dw_ref[...] += _dot_tn(sb, dzb)
        dsm = _dot(dzb, wt_ref[...]) + dfa_ref[...]
        ds_ref[...] = jnp.concatenate([dsm, jnp.zeros((tr, SMALL_W - BLK), F32)], axis=1).astype(BF16)

    return pl.pallas_call(
        kern, grid=(lp // tr,),
        in_specs=[pl.BlockSpec((tr, BLK), lambda i: (i, R_SMALL_BLK128)), pl.BlockSpec((tr, w), lambda i: (i, 0)),
                  pl.BlockSpec((BLK, w), lambda i: (0, 0)), pl.BlockSpec((w, BLK), lambda i: (0, 0)),
                  pl.BlockSpec((1, w), lambda i: (0, 0)), pl.BlockSpec((tr, BLK), lambda i: (i, 0)),
                  pl.BlockSpec(memory_space=pl.ANY)],
        out_specs=[pl.BlockSpec((tr, SMALL_W), lambda i: (i, F_SMALL_BLK0)), pl.BlockSpec((1, w), lambda i: (0, 0)),
                   pl.BlockSpec((BLK, w), lambda i: (0, 0))],
        out_shape=[jax.ShapeDtypeStruct(dproj.shape, BF16), jax.ShapeDtypeStruct((1, w), F32),
                   jax.ShapeDtypeStruct((BLK, w), F32)],
        input_output_aliases={6: 0},
        compiler_params=_cparams("arbitrary"), name="small_bwd",
    )(proj_r, dlogg, wg2p, wg2pt, bg, dfa, dproj)


def _merge_fwd(proj_r, gate_b3, ya, yb, yc, tr=640):
    lp = proj_r.shape[0]
    tr = _row_tile(lp, tr)
    tn = GATE_TN

    def kern(g_ref, b_ref, ya_ref, yb_ref, yc_ref, o_ref):
        g = g_ref[...]
        mix = (_sigmoid(g[:, :tn] + b_ref[0:1, :]) * ya_ref[...]
               + _sigmoid(g[:, tn:2 * tn] + b_ref[1:2, :]) * yb_ref[...]
               + _sigmoid(g[:, 2 * tn:] + b_ref[2:3, :]) * yc_ref[...])
        o_ref[...] = mix.astype(BF16)

    y = pl.BlockSpec((tr, tn), lambda i, j: (i, j))
    return pl.pallas_call(
        kern, grid=(lp // tr, D // tn),
        in_specs=[pl.BlockSpec((tr, 3 * tn), lambda i, j: (i, R_GATE_BLK0 + j)),
                  pl.BlockSpec((3, tn), lambda i, j: (0, j)), y, y, y],
        out_specs=y, out_shape=jax.ShapeDtypeStruct((lp, D), BF16),
        compiler_params=_cparams("parallel", "parallel"), name="merge_fwd")(proj_r, gate_b3, ya, yb, yc)


def _merge_bwd(proj_r, gate_b3, ya, yb, yc, dmix, tr=640):
    lp = proj_r.shape[0]
    tr = _row_tile(lp, tr)
    tn = GATE_TN

    def kern(g_ref, b_ref, ya_ref, yb_ref, yc_ref, dm_ref, dya_ref, dyb_ref, dyc_ref, dg_ref, db_ref):
        i = pl.program_id(1)
        g = g_ref[...]
        dm = dm_ref[...].astype(F32)

        @pl.when(i == 0)
        def _():
            db_ref[...] = jnp.zeros_like(db_ref)

        dgs = []
        for n, (y_ref, dy_ref) in enumerate(((ya_ref, dya_ref), (yb_ref, dyb_ref), (yc_ref, dyc_ref))):
            s = _sigmoid(g[:, n * tn:(n + 1) * tn] + b_ref[n:n + 1, :])
            dy_ref[...] = (dm * s).astype(BF16)
            dgn = dm * y_ref[...] * (s * (1.0 - s))
            db_ref[n:n + 1, :] += jnp.sum(dgn, axis=0, keepdims=True)
            dgs.append(dgn)
        dg_ref[...] = jnp.concatenate(dgs, axis=1).astype(BF16)

    y = pl.BlockSpec((tr, tn), lambda j, i: (i, j))
    bspec = pl.BlockSpec((3, tn), lambda j, i: (0, j))
    return pl.pallas_call(
        kern, grid=(D // tn, lp // tr),
        in_specs=[pl.BlockSpec((tr, 3 * tn), lambda j, i: (i, R_GATE_BLK0 + j)), bspec, y, y, y, y],
        out_specs=[y, y, y, pl.BlockSpec((tr, 3 * tn), lambda j, i: (i, F_GATE_BLK0 + j)), bspec],
        out_shape=[jax.ShapeDtypeStruct((lp, D), BF16)] * 3
        + [jax.ShapeDtypeStruct((lp, NP), BF16), jax.ShapeDtypeStruct((3, D), F32)],
        compiler_params=_cparams("parallel", "arbitrary"), name="merge_bwd")(proj_r, gate_b3, ya, yb, yc, dmix)


def _final_loss(h, gf, tgt):
    lp = h.shape[0]
    nb = lp // BLK

    def kern(h_ref, g_ref, t_ref, dh_ref, dg_ref, ls_ref):
        i = pl.program_id(0)

        @pl.when(i == 0)
        def _():
            dh_ref[...] = jnp.zeros_like(dh_ref)
            dg_ref[...] = jnp.zeros_like(dg_ref)
            ls_ref[...] = jnp.zeros_like(ls_ref)

        @pl.when(i > 0)
        def _():
            x = h_ref[...]
            r = lax.rsqrt(jnp.mean(x * x, axis=-1, keepdims=True) + EPS)
            xhat = x * r
            err = xhat * g_ref[...] - t_ref[...]
            ls_ref[...] += jnp.sum(jnp.sum(err * err, axis=0, keepdims=True), axis=1, keepdims=True)
            dy = err * (1.0 / D)
            dg_ref[...] += jnp.sum(dy * xhat, axis=0, keepdims=True)
            dxh = dy * g_ref[...]
            dh_ref[...] = r * (dxh - xhat * jnp.mean(dxh * xhat, axis=-1, keepdims=True))

    row = pl.BlockSpec((BLK, D), lambda i: (i, 0))
    vec = pl.BlockSpec((1, D), lambda i: (0, 0))
    return pl.pallas_call(
        kern, grid=(nb,),
        in_specs=[row, vec, pl.BlockSpec((BLK, D), lambda i: (jnp.maximum(i - 1, 0), 0))],
        out_specs=[row, vec, pl.BlockSpec((1, 1), lambda i: (0, 0))],
        out_shape=[jax.ShapeDtypeStruct((lp, D), F32), jax.ShapeDtypeStruct((1, D), F32),
                   jax.ShapeDtypeStruct((1, 1), F32)],
        compiler_params=_cparams("arbitrary"), name="final_loss")(h, gf, tgt)


def _gate_cols(c):
    ct = c[:, :FOX_H].T
    ck = jnp.where(jnp.arange(ct.shape[1]) < PAD, KEY_PAD_BIAS, ct)
    return ct[:, :, None], ck[:, None, :]


def _run(hosts, name, ctx, fn):
    if hosts and name in hosts:
        make, done = hosts[name]
        res = fn(make(ctx))
        done(res[-1])
    else:
        res = fn(None)
    return res[:-1]


def _mm_nn_x(a, b, exch, **kw):
    out = _mm_nn(a, b, exch=exch, **kw)
    return out if exch is not None else (out, None)


def _layer_fwd(h, w, hosts=None):
    xn = _rmsnorm_fwd(h, w["norm1_g"])
    proj_a, = _run(hosts, "proj_a", w,
                   lambda e: _mm_nn_x(xn, w["w_in"], e, n0=0, n=REST0, out_dtype=BF16, name="proj_a"))
    proj_r, = _run(hosts, "proj_r", w, lambda e: _mm_nn_x(xn, w["w_in"], e, n0=REST0, n=NREST, name="proj_r"))
    cq, ck = _gate_cols(_fox_gate_fwd(proj_r, w["bf128"]))
    oa, lse = _run(hosts, "attn_fwd", w, lambda e: _attn_fwd(proj_a, cq, ck, exch=e))
    zb = _convb_fwd(proj_r, w["conv_w"])
    logg = _gla_gate_fwd(proj_r, w["wg2p"], w["gla_b_g"])
    o_gla, zc, st_all = _gla_fwd(proj_r, logg, w["gla_norm_g"])
    ya = _mm_nn(oa, w["w_a_o"], out_dtype=BF16, name="branch_a")
    yb = _mm_nn(zb, w["w_b_o"], out_dtype=BF16, name="branch_b")
    yc = _mm_nn(zc, w["w_c_o"], out_dtype=BF16, name="branch_c")
    mix = _merge_fwd(proj_r, w["gate_b3"], ya, yb, yc)
    h1 = _mm_nn(mix, w["w_o"], res=h, name="out_proj")
    xn2 = _rmsnorm_fwd(h1, w["norm2_g"])
    z, = _run(hosts, "up_proj", w, lambda e: _mm_nn_x(xn2, w["w_up"], e, out_dtype=BF16, name="up_proj"))
    a = _mlp_gate_fwd(z, w["mlp_conv_w"])
    h2, = _run(hosts, "down_proj", w,
               lambda e: _mm_nn_x(a, w["w_down"], e, res=h1, tk=D_FF // 4, name="down_proj"))
    saved = dict(h=h, xn=xn, proj_a=proj_a, proj_r=proj_r, cq=cq, ck=ck, oa=oa, lse=lse, zb=zb, logg=logg,
                 o_gla=o_gla, zc=zc, st_all=st_all, ya=ya, yb=yb, yc=yc, mix=mix, h1=h1, xn2=xn2, z=z, a=a)
    return h2, saved


def _layer_bwd(dh2, w, s, hosts=None):
    g = {}
    da = _mm_nt(dh2, w["w_down"], tn=D_FF // 4, out_dtype=BF16, name="d_down_in")
    g["w_down"] = _mm_tn(s["a"], dh2, tk=D_FF // 4, name="d_w_down")
    dzg, dzu, dmw_g, dmw_u = _run(hosts, "mlp_gate_bwd", g, lambda e: _mlp_gate_bwd(
        s["z"], da, w["mlp_conv_w"], exch=e))
    g["mlp_conv_w"] = jnp.concatenate([dmw_g, dmw_u], axis=1)
    dxn2 = _mm_nt(dzg, w["w_up"], k0=0, kw=D_FF, tk=D_FF // 4, name="d_up_in_g")
    dxn2 = _mm_nt(dzu, w["w_up"], k0=D_FF, kw=D_FF, tk=D_FF // 4, add=dxn2, name="d_up_in_u")
    g["w_up"] = jnp.concatenate([_mm_tn(s["xn2"], dzg, name="d_w_up_g"), _mm_tn(s["xn2"], dzu, name="d_w_up_u")], axis=1)
    dh1, g["norm2_g"] = _rmsnorm_bwd(s["h1"], w["norm2_g"], dxn2, dh2)
    dmix = _mm_nt(dh1, w["w_o"], out_dtype=BF16, name="d_out_proj_in")
    g["w_o"] = _mm_tn(s["mix"], dh1, name="d_w_o")
    dya, dyb, dyc, dproj, g["gate_b3"] = _merge_bwd(s["proj_r"], w["gate_b3"], s["ya"], s["yb"], s["yc"], dmix)
    doa = _mm_nt(dya, w["w_a_o"], out_dtype=BF16, name="d_branch_a_in")
    g["w_a_o"] = _mm_tn(s["oa"], dya, name="d_w_a_o")
    dzb = _mm_nt(dyb, w["w_b_o"], name="d_branch_b_in")
    g["w_b_o"] = _mm_tn(s["zb"], dyb, name="d_w_b_o")
    dzc = _mm_nt(dyc, w["w_c_o"], name="d_branch_c_in")
    g["w_c_o"] = _mm_tn(s["zc"], dyc, name="d_w_c_o")
    dproj, dlogg, g["gla_norm_g"] = _gla_bwd(s["proj_r"], s["logg"], s["st_all"], s["o_gla"], dzc, w["gla_norm_g"], dproj)
    dproj, g["conv_w"] = _convb_bwd(s["proj_r"], dzb, w["conv_w"], dproj)
    dproj, dcq, dck = _run(hosts, "attn_bwd", g, lambda e: _attn_bwd(
        s["proj_a"], doa, s["oa"], s["lse"], s["cq"], s["ck"], dproj, exch=e))
    dc = jnp.pad((dcq[:, :, 0] + dck[:, 0, :]).T, ((0, 0), (0, BLK - FOX_H)))
    dfa, g["bf128"] = _fox_gate_bwd(s["proj_r"], dc, w["bf128"])
    dproj, g["gla_b_g"], g["wg2p"] = _small_bwd(s["proj_r"], dlogg, w["wg2p"], w["wg2p"].T, w["gla_b_g"], dfa, dproj)
    def pair(out, e):
        return out if e is not None else (out, None)

    g["w_in"], = _run(hosts, "d_w_in", g, lambda e: pair(_mm_tn(s["xn"], dproj, name="d_w_in", exch=e), e))
    dxn, = _run(hosts, "d_in_proj_in", g, lambda e: pair(_mm_nt(dproj, w["w_in"], name="d_in_proj_in", exch=e), e))
    dh0, g["norm1_g"] = _rmsnorm_bwd(s["h"], w["norm1_g"], dxn, dh1)
    return dh0, g


def _local_step(x, tgt, meta, final_g, layers, hosts_fwd=None, hosts_bwd=None):
    h = jnp.concatenate([jnp.zeros((PAD, D), F32), meta, x], axis=0)
    saved = []
    for l, w in enumerate(layers):
        h, s = _layer_fwd(h, w, hosts_fwd[l] if hosts_fwd else None)
        saved.append(s)
    dh, dgf, sq = _final_loss(h, final_g, tgt)
    grads = [None] * len(layers)
    for l in reversed(range(len(layers))):
        dh, grads[l] = _layer_bwd(dh, layers[l], saved[l], hosts_bwd[l](grads) if hosts_bwd else None)
    return sq[0, 0], dh[BLK:], dh[PAD:BLK], dgf, grads


def _w_in_to_kernel(w_nat):
    parts = [w_nat[:, s:s + n] for s, n in _segments()]
    parts.append(jnp.zeros((w_nat.shape[0], SMALL_W - 8 - GLA_R), w_nat.dtype))
    return jnp.concatenate(parts, axis=1)


def _w_in_from_kernel(w_k):
    pieces, off = [], 0
    for s, n in _segments():
        pieces.append((s, w_k[:, off:off + n]))
        off += n
    return jnp.concatenate([p for _, p in sorted(pieces, key=lambda t: t[0])], axis=1)


def _w_in_slots_to_kernel(got):
    per = got.shape[2]
    parts = []
    for s, n in _segments():
        while n > 0:
            d, lo = divmod(s, per)
            take = min(n, per - lo)
            parts.append(got[d, :, lo:lo + take])
            s, n = s + take, n - take
    parts.append(jnp.zeros((got.shape[1], SMALL_W - 8 - GLA_R), got.dtype))
    return jnp.concatenate(parts, axis=1)


def _w_in_kernel_to_slots(w_k):
    per = N_IN // N_DEV
    pieces, off = [], 0
    for s, n in _segments():
        pieces.append((s, n, off))
        off += n
    slots = []
    for d in range(N_DEV):
        lo, hi = d * per, (d + 1) * per
        parts = [w_k[:, off + max(s, lo) - s:off + min(s + n, hi) - s]
                 for s, n, off in sorted(pieces) if max(s, lo) < min(s + n, hi)]
        slots.append(jnp.concatenate(parts, axis=1))
    return jnp.stack(slots)


def _pad_rows_at(a, row0, nrows):
    return jnp.pad(a, ((row0, nrows - row0 - a.shape[0]), (0, 0)))


def _big_to_kernel(name, full):
    return _w_in_to_kernel(full) if name == "w_in" else full


def _layer_weights(big, conv_w, gla_w_g2, mlp_conv_w, norm1_g, fox_b_f, gate_b, gla_b_g, gla_norm_g, norm2_g):
    w = {n: _big_to_kernel(n, a) for n, a in big.items()}
    w.update(
        conv_w=conv_w, mlp_conv_w=mlp_conv_w,
        wg2p=_pad_rows_at(gla_w_g2, 8, BLK).astype(BF16),
        norm1_g=norm1_g[None], norm2_g=norm2_g[None], gla_b_g=gla_b_g[None], gla_norm_g=gla_norm_g[None],
        bf128=jnp.pad(fox_b_f, (0, BLK - FOX_H))[None], gate_b3=gate_b.reshape(3, D))
    return w


def _layer_grads_natural(g):
    return dict(
        w_in=_w_in_from_kernel(g["w_in"]), w_a_o=g["w_a_o"], w_b_o=g["w_b_o"], w_c_o=g["w_c_o"], w_o=g["w_o"],
        w_up=g["w_up"], w_down=g["w_down"], conv_w=g["conv_w"], mlp_conv_w=g["mlp_conv_w"],
        gla_w_g2=g["wg2p"][8:8 + GLA_R], norm1_g=g["norm1_g"][0], norm2_g=g["norm2_g"][0],
        gla_b_g=g["gla_b_g"][0], gla_norm_g=g["gla_norm_g"][0], fox_b_f=g["bf128"][0, :FOX_H],
        gate_b=g["gate_b3"].reshape(3 * D))


def _adamw(recv, w, m, v, layer, prev=None, name="adamw"):
    n_slot, r, c = recv.shape
    lyr = w.shape[0]
    tr = r
    for t in range(16, r, 16):
        if r % t == 0 and t * c <= ADAMW_BLOCK_ELEMS:
            tr = t
    if r * c <= ADAMW_BLOCK_ELEMS:
        tr = r
    bc1, bc2 = 1.0 - ADAM_B1 ** ADAM_STEP, 1.0 - ADAM_B2 ** ADAM_STEP

    def kern(*refs):
        r_ref, w_ref, m_ref, v_ref = refs[:4]
        g_out, d_out, m_out, v_out = refs[-4:]
        g = r_ref[0].astype(F32)
        for sidx in range(1, n_slot):
            g = g + r_ref[sidx].astype(F32)
        m_new = ADAM_B1 * m_ref[...] + (1.0 - ADAM_B1) * g
        v_new = ADAM_B2 * v_ref[...] + (1.0 - ADAM_B2) * (g * g)
        g_out[...] = g
        m_out[...] = m_new
        v_out[...] = v_new
        d_out[...] = -ADAM_LR * ((m_new / bc1) / (jnp.sqrt(v_new / bc2) + ADAM_EPS) + ADAM_WD * w_ref[...])

    lspec = pl.BlockSpec((None, tr, c), lambda i: (layer, i, 0))
    in_specs = [pl.BlockSpec((n_slot, tr, c), lambda i: (0, i, 0)), lspec, lspec, lspec]
    args = [recv, w, m, v]
    aliases = {}
    if prev is not None:
        in_specs += [pl.BlockSpec(memory_space=pl.ANY)] * 4
        args += list(prev)
        aliases = {4: 0, 5: 1, 6: 2, 7: 3}
    return pl.pallas_call(
        kern, grid=(r // tr,), in_specs=in_specs, out_specs=[lspec] * 4,
        out_shape=[jax.ShapeDtypeStruct((lyr, r, c), F32)] * 4, input_output_aliases=aliases,
        compiler_params=_cparams("parallel"), name=name)(*args)


_BIG = ("w_in", "w_a_o", "w_b_o", "w_c_o", "w_o", "w_up", "w_down")
_COL_SHARDED = ("w_in", "w_a_o", "w_b_o", "w_c_o", "w_up", "conv_w", "gla_w_g2", "mlp_conv_w")
_REPL = ("norm1_g", "fox_b_f", "gate_b", "gla_b_g", "gla_norm_g", "norm2_g")


def _cols_from_slots(a):
    return jnp.transpose(a, (1, 0, 2)).reshape(a.shape[1], N_DEV * a.shape[2])


def _cols_to_slots(a):
    r, c8 = a.shape
    return jnp.transpose(a.reshape(r, N_DEV, c8 // N_DEV), (1, 0, 2))


def _rows_to_slots(a):
    return a.reshape(N_DEV, a.shape[0] // N_DEV, a.shape[1])


def kernel(x, meta_tokens, norm1_g, w_in, fox_b_f, gate_b, conv_w, gla_w_g2, gla_b_g, gla_norm_g, w_a_o, w_b_o, w_c_o, w_o, norm2_g, w_up, mlp_conv_w, w_down, final_norm_g, loss_target, m_meta_tokens, m_norm1_g, m_w_in, m_fox_b_f, m_gate_b, m_conv_w, m_gla_w_g2, m_gla_b_g, m_gla_norm_g, m_w_a_o, m_w_b_o, m_w_c_o, m_w_o, m_norm2_g, m_w_up, m_mlp_conv_w, m_w_down, m_final_norm_g, v_meta_tokens, v_norm1_g, v_w_in, v_fox_b_f, v_gate_b, v_conv_w, v_gla_w_g2, v_gla_b_g, v_gla_norm_g, v_w_a_o, v_w_b_o, v_w_c_o, v_w_o, v_norm2_g, v_w_up, v_mlp_conv_w, v_w_down, v_final_norm_g):
    names = ("meta_tokens", "norm1_g", "w_in", "fox_b_f", "gate_b", "conv_w", "gla_w_g2", "gla_b_g", "gla_norm_g",
             "w_a_o", "w_b_o", "w_c_o", "w_o", "norm2_g", "w_up", "mlp_conv_w", "w_down", "final_norm_g")
    wts = dict(zip(names, (meta_tokens, norm1_g, w_in, fox_b_f, gate_b, conv_w, gla_w_g2, gla_b_g, gla_norm_g,
                           w_a_o, w_b_o, w_c_o, w_o, norm2_g, w_up, mlp_conv_w, w_down, final_norm_g)))
    mom = dict(zip(names, (m_meta_tokens, m_norm1_g, m_w_in, m_fox_b_f, m_gate_b, m_conv_w, m_gla_w_g2, m_gla_b_g,
                           m_gla_norm_g, m_w_a_o, m_w_b_o, m_w_c_o, m_w_o, m_norm2_g, m_w_up, m_mlp_conv_w, m_w_down,
                           m_final_norm_g)))
    var = dict(zip(names, (v_meta_tokens, v_norm1_g, v_w_in, v_fox_b_f, v_gate_b, v_conv_w, v_gla_w_g2, v_gla_b_g,
                           v_gla_norm_g, v_w_a_o, v_w_b_o, v_w_c_o, v_w_o, v_norm2_g, v_w_up, v_mlp_conv_w, v_w_down,
                           v_final_norm_g)))

    small = _exchange([conv_w, gla_w_g2, mlp_conv_w, meta_tokens], [True] * 4, "gather_small")
    conv_full = jnp.transpose(small[0], (1, 2, 0, 3)).reshape(DEPTH, 3, CONV_CH)
    g2_full = jnp.transpose(small[1], (1, 2, 0, 3)).reshape(DEPTH, GLA_R, GLA_H * GLA_DK)
    mconv_full = jnp.transpose(small[2], (1, 2, 0, 3)).reshape(DEPTH, 3, 2 * D_FF)
    meta_full = _cols_from_slots(small[3])
    layers = [_layer_weights({}, conv_full[l], g2_full[l], mconv_full[l], norm1_g[l], fox_b_f[l], gate_b[l],
                             gla_b_g[l], gla_norm_g[l], norm2_g[l]) for l in range(DEPTH)]

    wide = ("w_a_o", "w_b_o", "w_c_o", "w_up")

    def gather(l, which):
        def make(_):
            return [wts[n][l].astype(BF16) for n in which], [("wide" if n in wide else True) for n in which]

        def done(got):
            for n, a in zip(which, got):
                if n == "w_in":
                    layers[l][n] = _w_in_slots_to_kernel(a)
                else:
                    layers[l][n] = a if n in wide else a.reshape(-1, a.shape[-1])

        return make, done

    recv_big = [dict() for _ in range(DEPTH)]

    def scatter(l, which, grads_of):
        def make(ctx):
            g = grads_of(ctx)
            send = [_w_in_kernel_to_slots(g[n]) if n == "w_in" else g[n] if n in wide else _rows_to_slots(g[n])
                    for n in which]
            return send, [("cols" if n in wide else False) for n in which]

        def done(got):
            recv_big[l].update(zip(which, got))

        return make, done

    mixers = ("w_o", "w_a_o", "w_b_o", "w_c_o")
    layers[0]["w_in"] = _w_in_slots_to_kernel(_gather_by_chip(w_in[0].astype(BF16), "gather_w_in"))
    hosts_fwd = [
        {"proj_a": gather(0, mixers), "proj_r": gather(0, ("w_up",)), "attn_fwd": gather(1, ("w_in",)),
         "up_proj": gather(0, ("w_down",)), "down_proj": gather(1, mixers)},
        {"attn_fwd": gather(1, ("w_up", "w_down"))}]
    def both(first, n_first, second):
        def make(ctx):
            (a1, b1), (a2, b2) = first[0](ctx), second[0](ctx)
            return a1 + a2, b1 + b2

        def done(got):
            first[1](got[:n_first])
            second[1](got[n_first:])

        return make, done

    hosts_bwd = [
        lambda grads: {"mlp_gate_bwd": scatter(1, ("w_in",), lambda _: grads[1]),
                       "attn_bwd": both(scatter(1, ("w_down",) + mixers, lambda _: grads[1]), 1 + len(mixers),
                                        scatter(0, ("w_down", "w_up"), lambda g: g)),
                       "d_w_in": both(scatter(0, mixers, lambda g: g), len(mixers),
                                      scatter(1, ("w_up",), lambda _: grads[1])),
                       "d_in_proj_in": scatter(0, ("w_in",), lambda g: g)},
        lambda grads: None]

    sq, grad_x, dmeta, dgf, grads_k = _local_step(x[0], loss_target[0], meta_full, final_norm_g[None], layers,
                                                  hosts_fwd, hosts_bwd)
    loss = lax.psum(sq * (0.5 / D), ("x", "y", "c"))
    grads = [_layer_grads_natural(g) for g in grads_k]

    out_g, out_d, out_m, out_v = {}, {}, {}, {}

    def update(name, recv, layer, lyr_shape, prev):
        w3, m3, v3 = (t[name].reshape(lyr_shape) for t in (wts, mom, var))
        return _adamw(recv.reshape((recv.shape[0],) + lyr_shape[1:]), w3, m3, v3, layer, prev, name="adamw_" + name)

    def store(name, res):
        shape = wts[name].shape
        out_g[name], out_d[name], out_m[name], out_v[name] = (t.reshape(shape) for t in res)

    for n in _BIG:
        res = None
        for l in range(DEPTH):
            res = update(n, recv_big[l][n], l, wts[n].shape, res)
        store(n, res)

    def stack_layers(name):
        return jnp.stack([grads[l][name] for l in range(DEPTH)])

    s_conv = jnp.transpose(stack_layers("conv_w").reshape(DEPTH, 3, N_DEV, -1), (2, 0, 1, 3))
    s_g2 = jnp.transpose(stack_layers("gla_w_g2").reshape(DEPTH, GLA_R, N_DEV, -1), (2, 0, 1, 3))
    s_mconv = jnp.transpose(stack_layers("mlp_conv_w").reshape(DEPTH, 3, N_DEV, -1), (2, 0, 1, 3))
    s_meta = _cols_to_slots(dmeta)
    repl = [stack_layers(n) for n in _REPL] + [dgf]
    pack = jnp.concatenate([jnp.pad(a.reshape(-1), (0, (-a.size) % 1024)) for a in repl]).reshape(-1, BLK)
    r_conv, r_g2, r_mconv, r_meta, r_pack = _exchange(
        [s_conv, s_g2, s_mconv, s_meta, pack], [False, False, False, False, True], "scatter_small")
    store("conv_w", update("conv_w", r_conv, 0, (1, DEPTH * 3, CONV_CH // N_DEV), None))
    store("gla_w_g2", update("gla_w_g2", r_g2, 0, (1, DEPTH * GLA_R, GLA_H * GLA_DK // N_DEV), None))
    store("mlp_conv_w", update("mlp_conv_w", r_mconv, 0, (1, DEPTH * 3, 2 * D_FF // N_DEV), None))
    store("meta_tokens", update("meta_tokens", r_meta, 0, (1, N_META, D // N_DEV), None))
    off = 0
    for n, a in zip(_REPL + ("final_norm_g",), repl):
        rows = (a.size + 1023) // 1024 * 8
        part = r_pack[:, off:off + rows].reshape(N_DEV, -1)[:, :a.size]
        off += rows
        shape2 = (1, 1, a.size) if a.size % BLK else (1, a.size // BLK, BLK)
        store(n, update(n, part, 0, shape2, None))

    order = lambda d: [d[n] for n in names]
    return (loss, grad_x[None], *order(out_g), *order(out_d), *order(out_m), *order(out_v))
```

```python
import functools

import jax
import jax.numpy as jnp
from jax import lax
from jax.experimental import pallas as pl
from jax.experimental.pallas import tpu as pltpu

F32 = jnp.float32
BF16 = jnp.bfloat16

D = 2048
DEPTH = 2
N_META = 16
BLK = 128
PAD = BLK - N_META
EPS = 1e-6
NEG = -1e30

FOX_H, FOX_DH = 8, 128
FOX_W = FOX_H * FOX_DH
CONV_CH = 1024
GLA_H, GLA_DK, GLA_DV, GLA_R, GLA_TAU = 4, 128, 256, 16, 16.0
D_FF = 5632
N_IN = 15384
N_DEV = 8

ADAM_LR, ADAM_B1, ADAM_B2, ADAM_EPS, ADAM_WD, ADAM_STEP = 0.001, 0.9, 0.999, 1e-08, 0.01, 10

CONV_TC = 512
GATE_TN = 512
KV0 = 1024
REST0 = 3072
GLA_GRP = 768
SMALL_W = 1024
NP = 16384
NREST = NP - REST0
R_CONV_BLK0 = 0
R_GLA_BLK0 = (6144 - REST0) // GLA_GRP
R_GATE_BLK0 = (9216 - REST0) // (3 * GATE_TN)
R_SMALL_BLK128 = (15360 - REST0) // 128
F_CONV_BLK0 = 3072 // (3 * CONV_TC)
F_GLA_BLK0 = 6144 // GLA_GRP
F_GATE_BLK0 = 9216 // (3 * GATE_TN)
F_SMALL_BLK0 = 15360 // SMALL_W

VMEM_LIMIT = 56 * 1024 * 1024
ADAMW_BLOCK_ELEMS = 128 * 1024


def _segments():
    seg = [(0, 1024)]
    for h in range(FOX_H):
        seg += [(1024 + 128 * h, 128), (2048 + 128 * h, 128)]
    for j in range(CONV_CH // CONV_TC):
        seg += [(3080 + CONV_TC * j, CONV_TC), (4104 + CONV_TC * j, CONV_TC), (5128 + CONV_TC * j, CONV_TC)]
    for h in range(GLA_H):
        seg += [(6152 + 128 * h, 128), (6664 + 128 * h, 128), (7176 + 256 * h, 256), (8200 + 256 * h, 256)]
    for j in range(D // GATE_TN):
        seg += [(9240 + GATE_TN * j, GATE_TN), (11288 + GATE_TN * j, GATE_TN), (13336 + GATE_TN * j, GATE_TN)]
    seg += [(3072, 8), (9224, 16)]
    return seg


def _cparams(*sem):
    return pltpu.CompilerParams(dimension_semantics=sem, vmem_limit_bytes=VMEM_LIMIT)


def _row_tile(n, target):
    best = BLK
    t = BLK
    while t <= min(n, target):
        if n % t == 0:
            best = t
        t += BLK
    return best


def _sigmoid(x):
    return 1.0 / (1.0 + jnp.exp(-x))


def _log_sigmoid(x):
    return jnp.minimum(x, 0.0) - jnp.log(1.0 + jnp.exp(-jnp.abs(x)))


def _valid_rows(row0, n):
    return (row0 + lax.broadcasted_iota(jnp.int32, (n, 1), 0)) >= PAD


def _dot(a, b):
    return jnp.dot(a, b, preferred_element_type=F32)


def _dot_nt(a, b):
    return lax.dot_general(a, b, (((1,), (1,)), ((), ())), preferred_element_type=F32)


def _dot_tn(a, b):
    return lax.dot_general(a, b, (((0,), (0,)), ((), ())), preferred_element_type=F32)


def _exchange_copies(ins, outs, bcast, send_sems, recv_sems, local_sems):
    x, y, c = lax.axis_index("x"), lax.axis_index("y"), lax.axis_index("c")
    me = 4 * x + 2 * y + c

    def src_for(n, dev):
        if bcast[n] is True or bcast[n] == "wide":
            return ins[n]
        if bcast[n] == "cols":
            w = ins[n].shape[1] // N_DEV
            return ins[n].at[:, pl.ds(pl.multiple_of(dev * w, BLK), w)]
        return ins[n].at[dev]

    def dst_of(n, dev):
        if bcast[n] == "wide":
            w = ins[n].shape[1]
            return outs[n].at[:, pl.ds(pl.multiple_of(dev * w, BLK), w)]
        return outs[n].at[dev]

    local, sends, recvs = [], [], []
    for n in range(len(ins)):
        local.append(pltpu.make_async_copy(src_for(n, me), dst_of(n, me), local_sems.at[n]))
    for k in range(1, N_DEV):
        px = 1 - x if (k >> 2) & 1 else x
        py = 1 - y if (k >> 1) & 1 else y
        pc = 1 - c if k & 1 else c
        peer = 4 * px + 2 * py + pc
        for n in range(len(ins)):
            def copy(dst_dev, n=n, k=k, to=(px, py, pc), peer=peer):
                return pltpu.make_async_remote_copy(
                    src_ref=src_for(n, peer), dst_ref=dst_of(n, dst_dev), send_sem=send_sems.at[n, k - 1],
                    recv_sem=recv_sems.at[n, k - 1], device_id=to, device_id_type=pl.DeviceIdType.MESH)

            sends.append(copy(me))
            recvs.append(copy(peer))
    return local, sends, recvs


def _exchange_start(copies):
    local, sends, _ = copies
    for cp in local + sends:
        cp.start()


def _exchange_wait(copies):
    local, sends, recvs = copies
    for cp in recvs:
        cp.wait_recv()
    for cp in sends:
        cp.wait_send()
    for cp in local:
        cp.wait()


def _exchange_shapes(arrays, bcast):
    def shape(a, b):
        if b is True:
            return (N_DEV,) + a.shape
        if b == "wide":
            return (a.shape[0], N_DEV * a.shape[1])
        if b == "cols":
            return (N_DEV, a.shape[0], a.shape[1] // N_DEV)
        return a.shape

    return [jax.ShapeDtypeStruct(shape(a, b), a.dtype) for a, b in zip(arrays, bcast)]


def _exchange_sems(n_arr):
    return [pltpu.SemaphoreType.DMA((n_arr, N_DEV - 1)), pltpu.SemaphoreType.DMA((n_arr, N_DEV - 1)),
            pltpu.SemaphoreType.DMA((n_arr,))]


def _exchange(arrays, bcast, name):
    n_arr = len(arrays)

    def body(*refs):
        copies = _exchange_copies(refs[:n_arr], refs[n_arr:2 * n_arr], bcast, *refs[2 * n_arr:])
        _exchange_start(copies)
        _exchange_wait(copies)

    hbm = pl.BlockSpec(memory_space=pltpu.HBM)
    return pl.pallas_call(
        body, out_shape=_exchange_shapes(arrays, bcast), in_specs=[hbm] * n_arr, out_specs=[hbm] * n_arr,
        scratch_shapes=_exchange_sems(n_arr),
        compiler_params=pltpu.CompilerParams(has_side_effects=True), name=name)(*arrays)


def _gather_by_chip(block, name):
    def body(x_ref, out_ref, send_sems, recv_sems, local_sem):
        x, y, c = lax.axis_index("x"), lax.axis_index("y"), lax.axis_index("c")
        me, sibling = (x, y, c), (x, y, 1 - c)
        chips = [(1 - x, y), (x, 1 - y), (1 - x, 1 - y)]

        def slot(px, py, pc):
            return out_ref.at[4 * px + 2 * py + pc]

        def copy(k, block_of, to, src=None):
            return pltpu.make_async_remote_copy(
                src_ref=slot(*block_of) if src is None else src, dst_ref=slot(*block_of),
                send_sem=send_sems.at[k], recv_sem=recv_sems.at[k], device_id=to,
                device_id_type=pl.DeviceIdType.MESH)

        mine = pltpu.make_async_copy(x_ref, slot(*me), local_sem)
        mine.start()
        first = [copy(0, me, sibling, src=x_ref)]
        first += [copy(1 + j, me, (*chip, c), src=x_ref) for j, chip in enumerate(chips)]
        for cp in first:
            cp.start()
        passed = [copy(4 + j, (*chip, c), sibling) for j, chip in enumerate(chips)]
        for j, chip in enumerate(chips):
            copy(1 + j, (*chip, c), me).wait_recv()
            passed[j].start()
        copy(0, sibling, me).wait_recv()
        for j, chip in enumerate(chips):
            copy(4 + j, (*chip, 1 - c), me).wait_recv()
        for cp in first + passed:
            cp.wait_send()
        mine.wait()

    hbm = pl.BlockSpec(memory_space=pltpu.HBM)
    return pl.pallas_call(
        body, out_shape=jax.ShapeDtypeStruct((N_DEV,) + block.shape, block.dtype), in_specs=[hbm], out_specs=hbm,
        scratch_shapes=[pltpu.SemaphoreType.DMA((N_DEV - 1,)), pltpu.SemaphoreType.DMA((N_DEV - 1,)),
                        pltpu.SemaphoreType.DMA],
        compiler_params=pltpu.CompilerParams(has_side_effects=True), name=name)(block)


def _pcall(kern, *, grid, in_specs, out_specs, out_shape, scratch, sem, name, args, aliases=None, exch=None):
    params = pltpu.CompilerParams(dimension_semantics=sem, vmem_limit_bytes=VMEM_LIMIT,
                                  has_side_effects=exch is not None)
    kw = dict(grid=grid, compiler_params=params, name=name, input_output_aliases=aliases or {})
    if exch is None:
        out = pl.pallas_call(kern, in_specs=in_specs, out_specs=out_specs, out_shape=out_shape,
                             scratch_shapes=scratch, **kw)(*args)
        return out, None
    arrays, bcast = exch
    n_x, n_in, n_out, n_sc = len(arrays), len(in_specs), len(out_specs), len(scratch)

    def hosted(*refs):
        ins, x_in = refs[:n_in], refs[n_in:n_in + n_x]
        outs, x_out = refs[n_in + n_x:n_in + n_x + n_out], refs[n_in + n_x + n_out:n_in + 2 * n_x + n_out]
        sc, sems = refs[n_in + 2 * n_x + n_out:n_in + 2 * n_x + n_out + n_sc], refs[n_in + 2 * n_x + n_out + n_sc:]
        ids = [pl.program_id(d) for d in range(len(grid))]
        first = functools.reduce(jnp.logical_and, [i == 0 for i in ids])
        last = functools.reduce(jnp.logical_and, [i == g - 1 for i, g in zip(ids, grid)])

        @pl.when(first)
        def _():
            _exchange_start(_exchange_copies(x_in, x_out, bcast, *sems))

        kern(*ins, *outs, *sc)

        @pl.when(last)
        def _():
            _exchange_wait(_exchange_copies(x_in, x_out, bcast, *sems))

    hbm = pl.BlockSpec(memory_space=pltpu.HBM)
    out = pl.pallas_call(
        hosted, in_specs=list(in_specs) + [hbm] * n_x, out_specs=list(out_specs) + [hbm] * n_x,
        out_shape=list(out_shape) + _exchange_shapes(arrays, bcast),
        scratch_shapes=list(scratch) + _exchange_sems(n_x), **kw)(*args, *arrays)
    return out[:n_out], out[n_out:]


def _mm_nn(a, b, *, n0=0, n=None, out_dtype=F32, res=None, tm=1664, tn=512, tk=None, name="mm_nn", exch=None):
    m, k = a.shape
    n = b.shape[1] - n0 if n is None else n
    tm = _row_tile(m, tm)
    tk = k if tk is None else tk
    nk = k // tk
    assert k % tk == 0 and n % tn == 0 and n0 % tn == 0
    nb0 = n0 // tn

    def kern(*refs):
        if res is None:
            a_ref, b_ref, o_ref, acc = refs
        else:
            a_ref, b_ref, r_ref, o_ref, acc = refs
        kk = pl.program_id(2)
        row0 = pl.program_id(0) * tm

        def finish(prod):
            if res is None:
                o_ref[...] = prod.astype(out_dtype)
            else:
                o_ref[...] = (r_ref[...] + jnp.where(_valid_rows(row0, tm), prod, 0.0)).astype(out_dtype)

        if nk == 1:
            finish(_dot(a_ref[...].astype(BF16), b_ref[...].astype(BF16)))
            return

        @pl.when(kk == 0)
        def _():
            acc[...] = jnp.zeros_like(acc)

        acc[...] += _dot(a_ref[...].astype(BF16), b_ref[...].astype(BF16))

        @pl.when(kk == nk - 1)
        def _():
            finish(acc[...])

    in_specs = [pl.BlockSpec((tm, tk), lambda i, j, kk: (i, kk)),
                pl.BlockSpec((tk, tn), lambda i, j, kk: (kk, nb0 + j))]
    args = [a, b]
    if res is not None:
        in_specs.append(pl.BlockSpec((tm, tn), lambda i, j, kk: (i, j)))
        args.append(res)
    out, got = _pcall(
        kern, grid=(m // tm, n // tn, nk), in_specs=in_specs,
        out_specs=[pl.BlockSpec((tm, tn), lambda i, j, kk: (i, j))],
        out_shape=[jax.ShapeDtypeStruct((m, n), out_dtype)],
        scratch=[pltpu.VMEM((tm, tn) if nk > 1 else (8, 128), F32)],
        sem=("parallel", "parallel", "arbitrary"), name=name, args=args, exch=exch)
    return out[0] if exch is None else (out[0], got)


def _mm_nt(a, b, *, k0=0, kw=None, out_dtype=F32, add=None, tm=640, tn=None, tk=2048, name="mm_nt", exch=None):
    m = a.shape[0]
    kw = a.shape[1] if kw is None else kw
    nn = b.shape[0]
    tm = _row_tile(m, tm)
    tn = min(nn, 2048) if tn is None else tn
    tk = min(tk, kw)
    assert kw % tk == 0 and k0 % tk == 0 and nn % tn == 0 and a.shape[1] == kw
    nk = kw // tk
    kb0 = k0 // tk

    def kern(*refs):
        if add is None:
            a_ref, b_ref, o_ref, acc = refs
        else:
            a_ref, b_ref, d_ref, o_ref, acc = refs
        kk = pl.program_id(2)

        def finish(prod):
            o_ref[...] = (prod if add is None else prod + d_ref[...]).astype(out_dtype)

        if nk == 1:
            finish(_dot_nt(a_ref[...].astype(BF16), b_ref[...].astype(BF16)))
            return

        @pl.when(kk == 0)
        def _():
            acc[...] = jnp.zeros_like(acc)

        acc[...] += _dot_nt(a_ref[...].astype(BF16), b_ref[...].astype(BF16))

        @pl.when(kk == nk - 1)
        def _():
            finish(acc[...])

    in_specs = [pl.BlockSpec((tm, tk), lambda i, j, kk: (i, kk)),
                pl.BlockSpec((tn, tk), lambda i, j, kk: (j, kb0 + kk))]
    args = [a, b]
    if add is not None:
        in_specs.append(pl.BlockSpec((tm, tn), lambda i, j, kk: (i, j)))
        args.append(add)
    out, got = _pcall(
        kern, grid=(m // tm, nn // tn, nk), in_specs=in_specs,
        out_specs=[pl.BlockSpec((tm, tn), lambda i, j, kk: (i, j))],
        out_shape=[jax.ShapeDtypeStruct((m, nn), out_dtype)],
        scratch=[pltpu.VMEM((tm, tn) if nk > 1 else (8, 128), F32)],
        sem=("parallel", "parallel", "arbitrary"), name=name, args=args, exch=exch)
    return out[0] if exch is None else (out[0], got)


def _mm_tn(a, b, *, out_dtype=BF16, tm=1664, tk=None, tn=None, name="mm_tn", exch=None):
    m, k = a.shape
    n = b.shape[1]
    tm = _row_tile(m, tm)
    tk = k if tk is None else tk
    if tn is None:
        tn = 1024 if n % 1024 == 0 else 512
    assert k % tk == 0 and n % tn == 0
    nm = m // tm

    def kern(a_ref, b_ref, o_ref, acc):
        mm = pl.program_id(2)

        @pl.when(mm == 0)
        def _():
            acc[...] = jnp.zeros_like(acc)

        acc[...] += _dot_tn(a_ref[...].astype(BF16), b_ref[...].astype(BF16))

        @pl.when(mm == nm - 1)
        def _():
            o_ref[...] = acc[...].astype(out_dtype)

    out, got = _pcall(
        kern, grid=(k // tk, n // tn, nm),
        in_specs=[pl.BlockSpec((tm, tk), lambda i, j, mm: (mm, i)),
                  pl.BlockSpec((tm, tn), lambda i, j, mm: (mm, j))],
        out_specs=[pl.BlockSpec((tk, tn), lambda i, j, mm: (i, j))],
        out_shape=[jax.ShapeDtypeStruct((k, n), out_dtype)],
        scratch=[pltpu.VMEM((tk, tn), F32)],
        sem=("parallel", "parallel", "arbitrary"), name=name, args=(a, b), exch=exch)
    return out[0] if exch is None else (out[0], got)


def _rmsnorm_fwd(h, g, tr=640):
    lp = h.shape[0]
    tr = _row_tile(lp, tr)

    def kern(h_ref, g_ref, o_ref):
        x = h_ref[...]
        r = lax.rsqrt(jnp.mean(x * x, axis=-1, keepdims=True) + EPS)
        o_ref[...] = (x * r * g_ref[...]).astype(BF16)

    return pl.pallas_call(
        kern, grid=(lp // tr,),
        in_specs=[pl.BlockSpec((tr, D), lambda i: (i, 0)), pl.BlockSpec((1, D), lambda i: (0, 0))],
        out_specs=pl.BlockSpec((tr, D), lambda i: (i, 0)),
        out_shape=jax.ShapeDtypeStruct((lp, D), BF16),
        compiler_params=_cparams("parallel"), name="rmsnorm_fwd")(h, g)


def _rmsnorm_bwd(h, g, dxn, dres, tr=640):
    lp = h.shape[0]
    tr = _row_tile(lp, tr)

    def kern(h_ref, g_ref, dxn_ref, dres_ref, dh_ref, dg_ref):
        i = pl.program_id(0)
        x = h_ref[...]
        r = lax.rsqrt(jnp.mean(x * x, axis=-1, keepdims=True) + EPS)
        xhat = x * r
        dy = jnp.where(_valid_rows(i * tr, tr), dxn_ref[...], 0.0)

        @pl.when(i == 0)
        def _():
            dg_ref[...] = jnp.zeros_like(dg_ref)

        dg_ref[...] += jnp.sum(dy * xhat, axis=0, keepdims=True)
        dxh = dy * g_ref[...]
        dh_ref[...] = dres_ref[...] + r * (dxh - xhat * jnp.mean(dxh * xhat, axis=-1, keepdims=True))

    row = pl.BlockSpec((tr, D), lambda i: (i, 0))
    vec = pl.BlockSpec((1, D), lambda i: (0, 0))
    return pl.pallas_call(
        kern, grid=(lp // tr,), in_specs=[row, vec, row, row], out_specs=[row, vec],
        out_shape=[jax.ShapeDtypeStruct((lp, D), F32), jax.ShapeDtypeStruct((1, D), F32)],
        compiler_params=_cparams("arbitrary"), name="rmsnorm_bwd")(h, g, dxn, dres)


def _shift_down(xe, k):
    return xe if k == 0 else pltpu.roll(xe, k, 0)


def _shift_up(xe, k):
    return xe if k == 0 else pltpu.roll(xe, xe.shape[0] - k, 0)


def _conv_ext(xe, w_ref):
    return w_ref[2:3, :] * xe + w_ref[1:2, :] * _shift_down(xe, 1) + w_ref[0:1, :] * _shift_down(xe, 2)


def _halo_specs(tr, width, col_of, nrows, rows_first, halo=8):
    r8 = tr // halo
    last8 = nrows // halo - 1
    if rows_first:
        prev = pl.BlockSpec((halo, width), lambda i, j: (jnp.maximum(i * r8 - 1, 0), col_of(j)))
        nxt = pl.BlockSpec((halo, width), lambda i, j: (jnp.minimum((i + 1) * r8, last8), col_of(j)))
    else:
        prev = pl.BlockSpec((halo, width), lambda j, i: (jnp.maximum(i * r8 - 1, 0), col_of(j)))
        nxt = pl.BlockSpec((halo, width), lambda j, i: (jnp.minimum((i + 1) * r8, last8), col_of(j)))
    return prev, nxt


def _convb_fwd(proj_r, conv_w, tr=640):
    lp = proj_r.shape[0]
    tr = _row_tile(lp, tr)
    tc = CONV_TC
    gw = 3 * tc

    def kern(g_ref, gp_ref, w_ref, o_ref):
        i = pl.program_id(0)
        g = g_ref[...]
        p = g[:, tc:2 * tc] * g[:, 2 * tc:]
        gp = gp_ref[...]
        pp = jnp.where(i > 0, gp[:, tc:2 * tc] * gp[:, 2 * tc:], 0.0)
        y = _conv_ext(jnp.concatenate([pp, p], axis=0), w_ref)[8:]
        o_ref[...] = (g[:, :tc] * y).astype(BF16)

    prev, _ = _halo_specs(tr, gw, lambda j: R_CONV_BLK0 + j, lp, True)
    return pl.pallas_call(
        kern, grid=(lp // tr, CONV_CH // tc),
        in_specs=[pl.BlockSpec((tr, gw), lambda i, j: (i, R_CONV_BLK0 + j)), prev,
                  pl.BlockSpec((3, tc), lambda i, j: (0, j))],
        out_specs=pl.BlockSpec((tr, tc), lambda i, j: (i, j)),
        out_shape=jax.ShapeDtypeStruct((lp, CONV_CH), BF16),
        compiler_params=_cparams("parallel", "parallel"), name="convb_fwd")(proj_r, proj_r, conv_w)


def _convb_bwd(proj_r, dzb, conv_w, dproj, tr=640):
    lp = proj_r.shape[0]
    tr = _row_tile(lp, tr)
    nr = lp // tr
    tc = CONV_TC
    gw = 3 * tc

    def kern(g_ref, gp_ref, gn_ref, dz_ref, dzn_ref, w_ref, dp_any, dg_ref, dw_ref):
        del dp_any
        i = pl.program_id(1)
        g = g_ref[...]
        b, c, hh = g[:, :tc], g[:, tc:2 * tc], g[:, 2 * tc:]
        p = c * hh
        gp = gp_ref[...]
        pp = jnp.where(i > 0, gp[:, tc:2 * tc] * gp[:, 2 * tc:], 0.0)
        pe = jnp.concatenate([pp, p], axis=0)
        s1 = _shift_down(pe, 1)[8:]
        s2 = _shift_down(pe, 2)[8:]
        y = w_ref[2:3, :] * p + w_ref[1:2, :] * s1 + w_ref[0:1, :] * s2
        dz = dz_ref[...]
        dy = dz * b
        dyn = jnp.where(i < nr - 1, dzn_ref[...] * gn_ref[...][:, :tc], 0.0)
        dye = jnp.concatenate([dy, dyn], axis=0)
        dp = (w_ref[2:3, :] * dy + w_ref[1:2, :] * _shift_up(dye, 1)[:tr]
              + w_ref[0:1, :] * _shift_up(dye, 2)[:tr])
        valid = _valid_rows(i * tr, tr)
        dg_ref[...] = jnp.where(valid, jnp.concatenate([dz * y, dp * hh, dp * c], axis=1), 0.0).astype(BF16)

        @pl.when(i == 0)
        def _():
            dw_ref[...] = jnp.zeros_like(dw_ref)

        dw_ref[0:1, :] += jnp.sum(dy * s2, axis=0, keepdims=True)
        dw_ref[1:2, :] += jnp.sum(dy * s1, axis=0, keepdims=True)
        dw_ref[2:3, :] += jnp.sum(dy * p, axis=0, keepdims=True)

    gprev, gnext = _halo_specs(tr, gw, lambda j: R_CONV_BLK0 + j, lp, False)
    _, dznext = _halo_specs(tr, tc, lambda j: j, lp, False)
    return pl.pallas_call(
        kern, grid=(CONV_CH // tc, nr),
        in_specs=[pl.BlockSpec((tr, gw), lambda j, i: (i, R_CONV_BLK0 + j)), gprev, gnext,
                  pl.BlockSpec((tr, tc), lambda j, i: (i, j)), dznext,
                  pl.BlockSpec((3, tc), lambda j, i: (0, j)),
                  pl.BlockSpec(memory_space=pl.ANY)],
        out_specs=[pl.BlockSpec((tr, gw), lambda j, i: (i, F_CONV_BLK0 + j)),
                   pl.BlockSpec((3, tc), lambda j, i: (0, j))],
        out_shape=[jax.ShapeDtypeStruct(dproj.shape, BF16), jax.ShapeDtypeStruct((3, CONV_CH), F32)],
        input_output_aliases={6: 0},
        compiler_params=_cparams("parallel", "arbitrary"), name="convb_bwd",
    )(proj_r, proj_r, proj_r, dzb, dzb, conv_w, dproj)


MLP_TC = 256
MLP_HALO = 16


def _mlp_gate_fwd(z, w, tr=640):
    lp = z.shape[0]
    tr = _row_tile(lp, tr)
    tc = 512
    nc = D_FF // tc

    def kern(zg_ref, zgp_ref, zu_ref, zup_ref, wg_ref, wu_ref, o_ref):
        i = pl.program_id(0)

        def ext(m_ref, p_ref):
            return jnp.concatenate([jnp.where(i > 0, p_ref[...].astype(F32), 0.0), m_ref[...].astype(F32)], axis=0)

        ug = _conv_ext(ext(zg_ref, zgp_ref), wg_ref)[MLP_HALO:]
        uu = _conv_ext(ext(zu_ref, zup_ref), wu_ref)[MLP_HALO:]
        o_ref[...] = (ug * _sigmoid(ug) * uu).astype(BF16)

    gprev, _ = _halo_specs(tr, tc, lambda j: j, lp, True, MLP_HALO)
    uprev, _ = _halo_specs(tr, tc, lambda j: nc + j, lp, True, MLP_HALO)
    return pl.pallas_call(
        kern, grid=(lp // tr, nc),
        in_specs=[pl.BlockSpec((tr, tc), lambda i, j: (i, j)), gprev,
                  pl.BlockSpec((tr, tc), lambda i, j: (i, nc + j)), uprev,
                  pl.BlockSpec((3, tc), lambda i, j: (0, j)),
                  pl.BlockSpec((3, tc), lambda i, j: (0, nc + j))],
        out_specs=pl.BlockSpec((tr, tc), lambda i, j: (i, j)),
        out_shape=jax.ShapeDtypeStruct((lp, D_FF), BF16),
        compiler_params=_cparams("parallel", "parallel"), name="mlp_gate_fwd")(z, z, z, z, w, w)


def _mlp_gate_bwd(z, da, w, tr=640, exch=None):
    lp = z.shape[0]
    tr = _row_tile(lp, tr)
    nr = lp // tr
    tc = MLP_TC
    nc = D_FF // tc

    def kern(zg_ref, zgp_ref, zgn_ref, zu_ref, zup_ref, zun_ref, da_ref, dan_ref, wg_ref, wu_ref,
             dzg_ref, dzu_ref, dwg_ref, dwu_ref):
        i = pl.program_id(1)
        first, last = i == 0, i == nr - 1

        hl = MLP_HALO

        def ext(m_ref, p_ref, n_ref):
            return jnp.concatenate([jnp.where(first, 0.0, p_ref[...].astype(F32)), m_ref[...].astype(F32),
                                    jnp.where(last, 0.0, n_ref[...].astype(F32))], axis=0)

        zge, zue = ext(zg_ref, zgp_ref, zgn_ref), ext(zu_ref, zup_ref, zun_ref)
        ug = _conv_ext(zge, wg_ref)[hl:]
        uu = _conv_ext(zue, wu_ref)[hl:]
        dae = jnp.concatenate([da_ref[...].astype(F32), jnp.where(last, 0.0, dan_ref[...].astype(F32))], axis=0)
        sg = _sigmoid(ug)
        dug = dae * uu * (sg * (1.0 + ug * (1.0 - sg)))
        duu = dae * (ug * sg)
        valid = _valid_rows(i * tr, tr)

        @pl.when(first)
        def _():
            dwg_ref[...] = jnp.zeros_like(dwg_ref)
            dwu_ref[...] = jnp.zeros_like(dwu_ref)

        for du, ze, w_ref, dz_ref, dw_ref in ((dug, zge, wg_ref, dzg_ref, dwg_ref),
                                              (duu, zue, wu_ref, dzu_ref, dwu_ref)):
            dz = (w_ref[2:3, :] * du + w_ref[1:2, :] * _shift_up(du, 1) + w_ref[0:1, :] * _shift_up(du, 2))[:tr]
            dz_ref[...] = jnp.where(valid, dz, 0.0).astype(BF16)
            dum = du[:tr]
            for kk in range(3):
                dw_ref[kk:kk + 1, :] += jnp.sum(dum * _shift_down(ze, 2 - kk)[hl:hl + tr], axis=0, keepdims=True)

    gprev, gnext = _halo_specs(tr, tc, lambda j: j, lp, False, MLP_HALO)
    uprev, unext = _halo_specs(tr, tc, lambda j: nc + j, lp, False, MLP_HALO)
    main = pl.BlockSpec((tr, tc), lambda j, i: (i, j))
    wspec = pl.BlockSpec((3, tc), lambda j, i: (0, j))
    out, got = _pcall(
        kern, grid=(nc, nr),
        in_specs=[main, gprev, gnext, pl.BlockSpec((tr, tc), lambda j, i: (i, nc + j)), uprev, unext,
                  main, gnext, wspec, pl.BlockSpec((3, tc), lambda j, i: (0, nc + j))],
        out_specs=[main, main, wspec, wspec],
        out_shape=[jax.ShapeDtypeStruct((lp, D_FF), BF16), jax.ShapeDtypeStruct((lp, D_FF), BF16),
                   jax.ShapeDtypeStruct((3, D_FF), F32), jax.ShapeDtypeStruct((3, D_FF), F32)],
        scratch=[], sem=("parallel", "arbitrary"), name="mlp_gate_bwd",
        args=(z, z, z, z, z, z, da, da, w, w), exch=exch)
    return (*out, got)


def _tri(n, lower):
    r = lax.broadcasted_iota(jnp.int32, (n, n), 0)
    c = lax.broadcasted_iota(jnp.int32, (n, n), 1)
    return jnp.where((c <= r) if lower else (c >= r), 1.0, 0.0).astype(F32)


def _dot_exact(a, b):
    return jnp.dot(a, b, preferred_element_type=F32, precision=lax.Precision.HIGHEST)


def _fox_gate_fwd(proj_r, bf128):
    lp = proj_r.shape[0]
    nb = lp // BLK

    def kern(s_ref, b_ref, c_ref):
        tri = _tri(BLK, True)

        def body(i, carry):
            rows = pl.ds(pl.multiple_of(i * BLK, BLK), BLK)
            lf = jnp.where(_valid_rows(i * BLK, BLK), _log_sigmoid(s_ref[rows, :] + b_ref[...]), 0.0)
            cs = _dot_exact(tri, lf) + carry
            c_ref[rows, :] = cs
            return cs[BLK - 1:BLK, :]

        lax.fori_loop(0, nb, body, jnp.zeros((1, BLK), F32))

    return pl.pallas_call(
        kern, grid=(1,),
        in_specs=[pl.BlockSpec((lp, BLK), lambda i: (0, R_SMALL_BLK128)), pl.BlockSpec((1, BLK), lambda i: (0, 0))],
        out_specs=pl.BlockSpec((lp, BLK), lambda i: (0, 0)),
        out_shape=jax.ShapeDtypeStruct((lp, BLK), F32),
        compiler_params=_cparams("arbitrary"), name="fox_gate_fwd")(proj_r, bf128)


def _fox_gate_bwd(proj_r, dc, bf128):
    lp = proj_r.shape[0]
    nb = lp // BLK

    def kern(s_ref, dc_ref, b_ref, dfa_ref, dbf_ref):
        tri = _tri(BLK, False)

        dbf_ref[...] = jnp.zeros_like(dbf_ref)

        def body(ii, run):
            i = nb - 1 - ii
            rows = pl.ds(pl.multiple_of(i * BLK, BLK), BLK)
            dcb = dc_ref[rows, :]
            suf = _dot_exact(tri, dcb) + run
            dfa = jnp.where(_valid_rows(i * BLK, BLK), suf * _sigmoid(-(s_ref[rows, :] + b_ref[...])), 0.0)
            dfa_ref[rows, :] = dfa
            dbf_ref[...] += jnp.sum(dfa, axis=0, keepdims=True)
            return run + jnp.sum(dcb, axis=0, keepdims=True)

        lax.fori_loop(0, nb, body, jnp.zeros((1, BLK), F32))

    return pl.pallas_call(
        kern, grid=(1,),
        in_specs=[pl.BlockSpec((lp, BLK), lambda i: (0, R_SMALL_BLK128)), pl.BlockSpec((lp, BLK), lambda i: (0, 0)),
                  pl.BlockSpec((1, BLK), lambda i: (0, 0))],
        out_specs=[pl.BlockSpec((lp, BLK), lambda i: (0, 0)), pl.BlockSpec((1, BLK), lambda i: (0, 0))],
        out_shape=[jax.ShapeDtypeStruct((lp, BLK), F32), jax.ShapeDtypeStruct((1, BLK), F32)],
        compiler_params=_cparams("arbitrary"), name="fox_gate_bwd")(proj_r, dc, bf128)


LOG2E = 1.4426950408889634
KEY_PAD_BIAS = 1e30


def _attn_logits2(q, k, ck, diag):
    t = _dot_nt(q, k) * (LOG2E * FOX_DH ** -0.5) - ck * LOG2E
    if diag:
        r = lax.broadcasted_iota(jnp.int32, t.shape, 0)
        c = lax.broadcasted_iota(jnp.int32, t.shape, 1)
        t = jnp.where(c <= r, t, NEG)
    return t


ATTN_HEADS = 2
ATTN_HEADS_FWD = 4


def _head_cols(a):
    return (slice(a * FOX_DH, (a + 1) * FOX_DH), slice(2 * a * FOX_DH, (2 * a + 1) * FOX_DH),
            slice((2 * a + 1) * FOX_DH, (2 * a + 2) * FOX_DH))


def _on_blocks(i, j, step):
    pl.when(j < i)(functools.partial(step, False))
    pl.when(j == i)(functools.partial(step, True))


def _attn_fwd(proj_a, cq, ck, tq=640, exch=None):
    lp = proj_a.shape[0]
    tq = _row_tile(lp, tq)
    tk = tq
    nq = lp // tq

    def kern(q_ref, kv_ref, cq_ref, ck_ref, o_ref, lse_ref, m_sc, l_sc, acc):
        i, j = pl.program_id(1), pl.program_id(2)

        @pl.when(j == 0)
        def _():
            m_sc[...] = jnp.full_like(m_sc, -jnp.inf)
            l_sc[...] = jnp.zeros_like(l_sc)
            acc[...] = jnp.zeros_like(acc)

        def step(diag):
            heads = range(ATTN_HEADS_FWD)
            cols = [_head_cols(a) for a in heads]
            m_old = [m_sc[a] for a in heads]
            l_old = [l_sc[a] for a in heads]
            acc_old = [acc[:, cols[a][0]] for a in heads]
            cq2 = [cq_ref[a] * LOG2E for a in heads]
            t = [_attn_logits2(q_ref[:, cols[a][0]], kv_ref[:, cols[a][1]], ck_ref[a], diag) for a in heads]
            m_new = [jnp.maximum(m_old[a], jnp.max(t[a], axis=-1, keepdims=True) + cq2[a]) for a in heads]
            p = [jnp.exp2(t[a] + (cq2[a] - m_new[a])) for a in heads]
            alpha = [jnp.exp2(m_old[a] - m_new[a]) for a in heads]
            l_new = [alpha[a] * l_old[a] + jnp.sum(p[a], axis=-1, keepdims=True) for a in heads]
            acc_new = [alpha[a] * acc_old[a] + _dot(p[a].astype(BF16), kv_ref[:, cols[a][2]]) for a in heads]
            for a in heads:
                m_sc[a] = m_new[a]
                l_sc[a] = l_new[a]
                acc[:, cols[a][0]] = acc_new[a]

        _on_blocks(i, j, step)

        @pl.when(j == nq - 1)
        def _():
            valid = _valid_rows(i * tq, tq)
            for a in range(ATTN_HEADS_FWD):
                hq, _, _ = _head_cols(a)
                o_ref[:, hq] = jnp.where(valid, acc[:, hq] / l_sc[a], 0.0).astype(BF16)
                lse_ref[a] = m_sc[a] + jnp.log(l_sc[a]) * LOG2E

    hp = ATTN_HEADS_FWD
    out, got = _pcall(
        kern, grid=(FOX_H // hp, nq, nq),
        in_specs=[pl.BlockSpec((tq, hp * FOX_DH), lambda h, i, j: (i, h)),
                  pl.BlockSpec((tk, 2 * hp * FOX_DH), lambda h, i, j: (jnp.minimum(j, i), KV0 // (2 * hp * FOX_DH) + h)),
                  pl.BlockSpec((hp, tq, 1), lambda h, i, j: (h, i, 0)),
                  pl.BlockSpec((hp, 1, tk), lambda h, i, j: (h, 0, jnp.minimum(j, i)))],
        out_specs=[pl.BlockSpec((tq, hp * FOX_DH), lambda h, i, j: (i, h)),
                   pl.BlockSpec((hp, tq, 1), lambda h, i, j: (h, i, 0))],
        out_shape=[jax.ShapeDtypeStruct((lp, FOX_W), BF16), jax.ShapeDtypeStruct((FOX_H, lp, 1), F32)],
        scratch=[pltpu.VMEM((hp, tq, 1), F32), pltpu.VMEM((hp, tq, 1), F32), pltpu.VMEM((tq, hp * FOX_DH), F32)],
        sem=("parallel", "parallel", "arbitrary"), name="attn_fwd", args=(proj_a, proj_a, cq, ck), exch=exch)
    return out[0], out[1], got


def _attn_bwd(proj_a, do, o, lse, cq, ck, dproj, tq=640, exch=None):
    lp = proj_a.shape[0]
    tq = _row_tile(lp, tq)
    tk = tq
    nq = lp // tq
    hp = ATTN_HEADS
    wq = hp * FOX_DH

    def kern(q_ref, kv_ref, do_ref, o_ref, lse_ref, cq_ref, ck_ref, dp_any, dproj_ref, dcq_ref, dck_ref,
             dk_acc, dv_acc, dq_acc, dcq_acc, dq_stage, dkv_stage, sems):
        del dp_any
        h, j, i = pl.program_id(0), pl.program_id(1), pl.program_id(2)
        rows = pl.ds(pl.multiple_of(i * tq, tq), tq)

        @pl.when(i == 0)
        def _():
            dck_ref[...] = jnp.zeros_like(dck_ref)
            dk_acc[...] = jnp.zeros_like(dk_acc)
            dv_acc[...] = jnp.zeros_like(dv_acc)

        @pl.when(jnp.logical_and(i == 0, j == 0))
        def _():
            dq_acc[...] = jnp.zeros_like(dq_acc)
            dcq_acc[...] = jnp.zeros_like(dcq_acc)

        def step(diag):
            heads = range(hp)
            cols = [_head_cols(a) for a in heads]
            dck_old = [dck_ref[a] for a in heads]
            dcq_old = [dcq_acc[a, rows, :] for a in heads]
            dk_old = [dk_acc[:, cols[a][0]] for a in heads]
            dv_old = [dv_acc[:, cols[a][0]] for a in heads]
            dq_old = [dq_acc[rows, cols[a][0]] for a in heads]
            shift = [cq_ref[a] * LOG2E - lse_ref[a] for a in heads]
            do_h = [do_ref[:, cols[a][0]] for a in heads]
            delta = [jnp.sum(do_h[a].astype(F32) * o_ref[:, cols[a][0]].astype(F32), axis=-1, keepdims=True)
                     for a in heads]
            t = [_attn_logits2(q_ref[:, cols[a][0]], kv_ref[:, cols[a][1]], ck_ref[a], diag) for a in heads]
            dp = [_dot_nt(do_h[a], kv_ref[:, cols[a][2]]) for a in heads]
            p = [jnp.exp2(t[a] + shift[a]) for a in heads]
            ds = [p[a] * (dp[a] - delta[a]) for a in heads]
            dsb = [ds[a].astype(BF16) for a in heads]
            dv_new = [dv_old[a] + _dot_tn(p[a].astype(BF16), do_h[a]) for a in heads]
            dck_new = [dck_old[a] - jnp.sum(ds[a], axis=0, keepdims=True) for a in heads]
            dcq_new = [dcq_old[a] + jnp.sum(ds[a], axis=-1, keepdims=True) for a in heads]
            dk_new = [dk_old[a] + _dot_tn(dsb[a], q_ref[:, cols[a][0]]) for a in heads]
            dq_new = [dq_old[a] + _dot(dsb[a], kv_ref[:, cols[a][1]]) for a in heads]
            for a in heads:
                dck_ref[a] = dck_new[a]
                dcq_acc[a, rows, :] = dcq_new[a]
                dk_acc[:, cols[a][0]] = dk_new[a]
                dv_acc[:, cols[a][0]] = dv_new[a]
                dq_acc[rows, cols[a][0]] = dq_new[a]

        _on_blocks(i, j, step)

        @pl.when(i == nq - 1)
        def _():
            parts = []
            for a in range(hp):
                hq, _, _ = _head_cols(a)
                parts += [dk_acc[:, hq] * (FOX_DH ** -0.5), dv_acc[:, hq]]
            dkv_stage[...] = jnp.concatenate(parts, axis=1).astype(BF16)
            out = pltpu.make_async_copy(
                dkv_stage, dproj_ref.at[pl.ds(pl.multiple_of(j * tk, tk), tk),
                                        pl.ds(pl.multiple_of(KV0 + h * 2 * wq, 2 * wq), 2 * wq)], sems.at[0])
            out.start()
            out.wait()

        @pl.when(jnp.logical_and(i == nq - 1, j == nq - 1))
        def _():
            dq_stage[...] = (dq_acc[...] * (FOX_DH ** -0.5)).astype(BF16)
            out = pltpu.make_async_copy(dq_stage, dproj_ref.at[:, pl.ds(pl.multiple_of(h * wq, wq), wq)], sems.at[1])
            rowsums = pltpu.make_async_copy(dcq_acc, dcq_ref.at[pl.ds(h * hp, hp)], sems.at[2])
            out.start()
            rowsums.start()
            out.wait()
            rowsums.wait()

    qspec = pl.BlockSpec((tq, wq), lambda h, j, i: (jnp.maximum(i, j), h))
    col = pl.BlockSpec((hp, tq, 1), lambda h, j, i: (h, jnp.maximum(i, j), 0))
    kvspec = pl.BlockSpec((tk, 2 * wq), lambda h, j, i: (j, KV0 // (2 * wq) + h))
    rowspec = pl.BlockSpec((hp, 1, tk), lambda h, j, i: (h, 0, j))
    out, got = _pcall(
        kern, grid=(FOX_H // hp, nq, nq),
        in_specs=[qspec, kvspec, qspec, qspec, col, col, rowspec, pl.BlockSpec(memory_space=pl.ANY)],
        out_specs=[pl.BlockSpec(memory_space=pl.ANY), pl.BlockSpec(memory_space=pl.ANY), rowspec],
        out_shape=[jax.ShapeDtypeStruct(dproj.shape, BF16), jax.ShapeDtypeStruct((FOX_H, lp, 1), F32),
                   jax.ShapeDtypeStruct((FOX_H, 1, lp), F32)],
        scratch=[pltpu.VMEM((tk, wq), F32), pltpu.VMEM((tk, wq), F32), pltpu.VMEM((lp, wq), F32),
                 pltpu.VMEM((hp, lp, 1), F32), pltpu.VMEM((lp, wq), BF16), pltpu.VMEM((tk, 2 * wq), BF16),
                 pltpu.SemaphoreType.DMA((3,))],
        aliases={7: 0}, sem=("arbitrary", "arbitrary", "arbitrary"), name="attn_bwd",
        args=(proj_a, proj_a, do, o, lse, cq, ck, dproj), exch=exch)
    return out[0], out[1], out[2], got


def _gla_gate_fwd(proj_r, wg2p, bg, tr=640):
    lp = proj_r.shape[0]
    tr = _row_tile(lp, tr)
    w = GLA_H * GLA_DK

    def kern(s_ref, w_ref, b_ref, o_ref):
        zg = _dot(s_ref[...].astype(BF16), w_ref[...]) + b_ref[...]
        o_ref[...] = jnp.where(_valid_rows(pl.program_id(0) * tr, tr), _log_sigmoid(zg) * (1.0 / GLA_TAU), 0.0)

    return pl.pallas_call(
        kern, grid=(lp // tr,),
        in_specs=[pl.BlockSpec((tr, BLK), lambda i: (i, R_SMALL_BLK128)), pl.BlockSpec((BLK, w), lambda i: (0, 0)),
                  pl.BlockSpec((1, w), lambda i: (0, 0))],
        out_specs=pl.BlockSpec((tr, w), lambda i: (i, 0)),
        out_shape=jax.ShapeDtypeStruct((lp, w), F32),
        compiler_params=_cparams("parallel"), name="gla_gate_fwd")(proj_r, wg2p, bg)


def _gla_chunk(grp, g):
    q = grp[:, :GLA_DK] * (GLA_DK ** -0.5)
    k = grp[:, GLA_DK:2 * GLA_DK]
    v = grp[:, 2 * GLA_DK:2 * GLA_DK + GLA_DV]
    r = grp[:, 2 * GLA_DK + GLA_DV:]
    b = _dot_exact(_tri(BLK, True), g)
    bl = b[BLK - 1:BLK, :]
    eb = jnp.exp(b)
    enb = jnp.exp(-b)
    ebl = jnp.exp(bl - b)
    qe, ke, kd = q * eb, k * enb, k * ebl
    causal = lax.broadcasted_iota(jnp.int32, (BLK, BLK), 1) <= lax.broadcasted_iota(jnp.int32, (BLK, BLK), 0)
    att = jnp.where(causal, _dot_nt(qe.astype(BF16), ke.astype(BF16)), 0.0)
    return q, k, v, r, bl, eb, enb, ebl, qe, ke, kd, causal, att


def _gla_fwd(proj_r, logg, gn):
    lp = proj_r.shape[0]
    nc = lp // BLK
    wv = GLA_H * GLA_DV

    def kern(grp_ref, g_ref, gn_ref, o_ref, zc_ref, st_ref, st):
        c = pl.program_id(0)

        @pl.when(c == 0)
        def _():
            st[...] = jnp.zeros_like(st)

        for h in range(GLA_H):
            kcol = slice(h * GLA_DK, (h + 1) * GLA_DK)
            vcol = slice(h * GLA_DV, (h + 1) * GLA_DV)
            q, k, v, r, bl, eb, enb, ebl, qe, ke, kd, causal, att = _gla_chunk(
                grp_ref[:, h * GLA_GRP:(h + 1) * GLA_GRP], g_ref[:, kcol])
            s_t = st[h]
            st_ref[h] = s_t
            vb = v.astype(BF16)
            o = _dot(att.astype(BF16), vb) + _dot_nt(qe.astype(BF16), s_t.astype(BF16))
            st[h] = s_t * jnp.exp(bl) + _dot_tn(vb, kd.astype(BF16))
            o_ref[:, vcol] = o
            rstd = lax.rsqrt(jnp.mean(o * o, axis=-1, keepdims=True) + EPS)
            zc_ref[:, vcol] = (r * _sigmoid(r) * (o * rstd * gn_ref[:, vcol])).astype(BF16)

    vspec = pl.BlockSpec((BLK, wv), lambda c: (c, 0))
    return pl.pallas_call(
        kern, grid=(nc,),
        in_specs=[pl.BlockSpec((BLK, GLA_H * GLA_GRP), lambda c: (c, R_GLA_BLK0 // GLA_H)),
                  pl.BlockSpec((BLK, GLA_H * GLA_DK), lambda c: (c, 0)),
                  pl.BlockSpec((1, wv), lambda c: (0, 0))],
        out_specs=[vspec, vspec, pl.BlockSpec((GLA_H, None, GLA_DV, GLA_DK), lambda c: (0, c, 0, 0))],
        out_shape=[jax.ShapeDtypeStruct((lp, wv), F32), jax.ShapeDtypeStruct((lp, wv), BF16),
                   jax.ShapeDtypeStruct((GLA_H, nc, GLA_DV, GLA_DK), F32)],
        scratch_shapes=[pltpu.VMEM((GLA_H, GLA_DV, GLA_DK), F32)],
        compiler_params=_cparams("arbitrary"), name="gla_fwd")(proj_r, logg, gn)


def _gla_bwd(proj_r, logg, st_all, o_all, dzc, gn, dproj):
    lp = proj_r.shape[0]
    nc = lp // BLK

    def kern(grp_ref, g_ref, st_ref, o_ref, dzc_ref, gn_ref, dp_any, dgrp_ref, dlg_ref, dgn_ref, dst):
        del dp_any
        cc = pl.program_id(0)

        @pl.when(cc == 0)
        def _():
            dst[...] = jnp.zeros_like(dst)
            dgn_ref[...] = jnp.zeros_like(dgn_ref)

        for h in range(GLA_H):
            kcol = slice(h * GLA_DK, (h + 1) * GLA_DK)
            vcol = slice(h * GLA_DV, (h + 1) * GLA_DV)
            q, k, v, r, bl, eb, enb, ebl, qe, ke, kd, causal, att = _gla_chunk(
                grp_ref[:, h * GLA_GRP:(h + 1) * GLA_GRP], g_ref[:, kcol])
            s_t = st_ref[h]
            d_st = dst[h]
            o = o_ref[:, vcol]
            dzc_v = dzc_ref[:, vcol]
            gnv = gn_ref[:, vcol]
            rstd = lax.rsqrt(jnp.mean(o * o, axis=-1, keepdims=True) + EPS)
            xhat = o * rstd
            sr = _sigmoid(r)
            dr = dzc_v * (xhat * gnv) * (sr * (1.0 + r * (1.0 - sr)))
            docn = dzc_v * (r * sr)
            dgn_ref[:, vcol] += jnp.sum(docn * xhat, axis=0, keepdims=True)
            dxh = docn * gnv
            do = rstd * (dxh - xhat * jnp.mean(dxh * xhat, axis=-1, keepdims=True))
            dob, vb = do.astype(BF16), v.astype(BF16)
            qeb, keb, kdb = qe.astype(BF16), ke.astype(BF16), kd.astype(BF16)
            datt = jnp.where(causal, _dot_nt(dob, vb), 0.0).astype(BF16)
            dv = _dot_tn(att.astype(BF16), dob) + _dot_nt(kdb, d_st.astype(BF16))
            dqe = _dot(datt, keb) + _dot(dob, s_t.astype(BF16))
            dke = _dot_tn(datt, qeb)
            dkd = _dot(vb, d_st.astype(BF16))
            dq = dqe * eb * (GLA_DK ** -0.5)
            dk = dke * enb + dkd * ebl
            kd_dkd = dkd * kd
            db = dqe * qe - dke * ke - kd_dkd
            db_last = (jnp.sum(kd_dkd, axis=0, keepdims=True)
                       + jnp.exp(bl) * jnp.sum(s_t * d_st, axis=0, keepdims=True))
            dlg_ref[:, kcol] = _dot_exact(_tri(BLK, False), db) + db_last
            dst[h] = d_st * jnp.exp(bl) + _dot_tn(dob, qeb)
            dgrp_ref[:, h * GLA_GRP:(h + 1) * GLA_GRP] = jnp.concatenate([dq, dk, dv, dr], axis=1).astype(BF16)

    rev = lambda c: nc - 1 - c
    wv = GLA_H * GLA_DV
    vspec = pl.BlockSpec((BLK, wv), lambda c: (rev(c), 0))
    kspec = pl.BlockSpec((BLK, GLA_H * GLA_DK), lambda c: (rev(c), 0))
    return pl.pallas_call(
        kern, grid=(nc,),
        in_specs=[pl.BlockSpec((BLK, GLA_H * GLA_GRP), lambda c: (rev(c), R_GLA_BLK0 // GLA_H)), kspec,
                  pl.BlockSpec((GLA_H, None, GLA_DV, GLA_DK), lambda c: (0, rev(c), 0, 0)),
                  vspec, vspec, pl.BlockSpec((1, wv), lambda c: (0, 0)),
                  pl.BlockSpec(memory_space=pl.ANY)],
        out_specs=[pl.BlockSpec((BLK, GLA_H * GLA_GRP), lambda c: (rev(c), F_GLA_BLK0 // GLA_H)), kspec,
                   pl.BlockSpec((1, wv), lambda c: (0, 0))],
        out_shape=[jax.ShapeDtypeStruct(dproj.shape, BF16), jax.ShapeDtypeStruct((lp, GLA_H * GLA_DK), F32),
                   jax.ShapeDtypeStruct((1, wv), F32)],
        scratch_shapes=[pltpu.VMEM((GLA_H, GLA_DV, GLA_DK), F32)],
        input_output_aliases={6: 0},
        compiler_params=_cparams("arbitrary"), name="gla_bwd",
    )(proj_r, logg, st_all, o_all, dzc, gn, dproj)


def _small_bwd(proj_r, dlogg, wg2p, wg2pt, bg, dfa, dproj, tr=640):
    lp = proj_r.shape[0]
    tr = _row_tile(lp, tr)
    w = GLA_H * GLA_DK

    def kern(s_ref, dlg_ref, w_ref, wt_ref, b_ref, dfa_ref, dp_any, ds_ref, dbg_ref, dw_ref):
        del dp_any
        i = pl.program_id(0)
        sb = s_ref[...].astype(BF16)
        zg = _dot(sb, w_ref[...]) + b_ref[...]
        dzg = jnp.where(_valid_rows(i * tr, tr), dlg_ref[...] * (1.0 / GLA_TAU) * _sigmoid(-zg), 0.0)

        @pl.when(i == 0)
        def _():
            dbg_ref[...] = jnp.zeros_like(dbg_ref)
            dw_ref[...] = jnp.zeros_like(dw_ref)

        dbg_ref[...] += jnp.sum(dzg, axis=0, keepdims=True)
        dzb = dzg.astype(BF16)
        dw_ref[...] += _dot_tn(sb, dzb)
        dsm = _dot(dzb, wt_ref[...]) + dfa_ref[...]
        ds_ref[...] = jnp.concatenate([dsm, jnp.zeros((tr, SMALL_W - BLK), F32)], axis=1).astype(BF16)

    return pl.pallas_call(
        kern, grid=(lp // tr,),
        in_specs=[pl.BlockSpec((tr, BLK), lambda i: (i, R_SMALL_BLK128)), pl.BlockSpec((tr, w), lambda i: (i, 0)),
                  pl.BlockSpec((BLK, w), lambda i: (0, 0)), pl.BlockSpec((w, BLK), lambda i: (0, 0)),
                  pl.BlockSpec((1, w), lambda i: (0, 0)), pl.BlockSpec((tr, BLK), lambda i: (i, 0)),
                  pl.BlockSpec(memory_space=pl.ANY)],
        out_specs=[pl.BlockSpec((tr, SMALL_W), lambda i: (i, F_SMALL_BLK0)), pl.BlockSpec((1, w), lambda i: (0, 0)),
                   pl.BlockSpec((BLK, w), lambda i: (0, 0))],
        out_shape=[jax.ShapeDtypeStruct(dproj.shape, BF16), jax.ShapeDtypeStruct((1, w), F32),
                   jax.ShapeDtypeStruct((BLK, w), F32)],
        input_output_aliases={6: 0},
        compiler_params=_cparams("arbitrary"), name="small_bwd",
    )(proj_r, dlogg, wg2p, wg2pt, bg, dfa, dproj)


def _merge_fwd(proj_r, gate_b3, ya, yb, yc, tr=640):
    lp = proj_r.shape[0]
    tr = _row_tile(lp, tr)
    tn = GATE_TN

    def kern(g_ref, b_ref, ya_ref, yb_ref, yc_ref, o_ref):
        g = g_ref[...]
        mix = (_sigmoid(g[:, :tn] + b_ref[0:1, :]) * ya_ref[...]
               + _sigmoid(g[:, tn:2 * tn] + b_ref[1:2, :]) * yb_ref[...]
               + _sigmoid(g[:, 2 * tn:] + b_ref[2:3, :]) * yc_ref[...])
        o_ref[...] = mix.astype(BF16)

    y = pl.BlockSpec((tr, tn), lambda i, j: (i, j))
    return pl.pallas_call(
        kern, grid=(lp // tr, D // tn),
        in_specs=[pl.BlockSpec((tr, 3 * tn), lambda i, j: (i, R_GATE_BLK0 + j)),
                  pl.BlockSpec((3, tn), lambda i, j: (0, j)), y, y, y],
        out_specs=y, out_shape=jax.ShapeDtypeStruct((lp, D), BF16),
        compiler_params=_cparams("parallel", "parallel"), name="merge_fwd")(proj_r, gate_b3, ya, yb, yc)


def _merge_bwd(proj_r, gate_b3, ya, yb, yc, dmix, tr=640):
    lp = proj_r.shape[0]
    tr = _row_tile(lp, tr)
    tn = GATE_TN

    def kern(g_ref, b_ref, ya_ref, yb_ref, yc_ref, dm_ref, dya_ref, dyb_ref, dyc_ref, dg_ref, db_ref):
        i = pl.program_id(1)
        g = g_ref[...]
        dm = dm_ref[...].astype(F32)

        @pl.when(i == 0)
        def _():
            db_ref[...] = jnp.zeros_like(db_ref)

        dgs = []
        for n, (y_ref, dy_ref) in enumerate(((ya_ref, dya_ref), (yb_ref, dyb_ref), (yc_ref, dyc_ref))):
            s = _sigmoid(g[:, n * tn:(n + 1) * tn] + b_ref[n:n + 1, :])
            dy_ref[...] = (dm * s).astype(BF16)
            dgn = dm * y_ref[...] * (s * (1.0 - s))
            db_ref[n:n + 1, :] += jnp.sum(dgn, axis=0, keepdims=True)
            dgs.append(dgn)
        dg_ref[...] = jnp.concatenate(dgs, axis=1).astype(BF16)

    y = pl.BlockSpec((tr, tn), lambda j, i: (i, j))
    bspec = pl.BlockSpec((3, tn), lambda j, i: (0, j))
    return pl.pallas_call(
        kern, grid=(D // tn, lp // tr),
        in_specs=[pl.BlockSpec((tr, 3 * tn), lambda j, i: (i, R_GATE_BLK0 + j)), bspec, y, y, y, y],
        out_specs=[y, y, y, pl.BlockSpec((tr, 3 * tn), lambda j, i: (i, F_GATE_BLK0 + j)), bspec],
        out_shape=[jax.ShapeDtypeStruct((lp, D), BF16)] * 3
        + [jax.ShapeDtypeStruct((lp, NP), BF16), jax.ShapeDtypeStruct((3, D), F32)],
        compiler_params=_cparams("parallel", "arbitrary"), name="merge_bwd")(proj_r, gate_b3, ya, yb, yc, dmix)


def _final_loss(h, gf, tgt):
    lp = h.shape[0]
    nb = lp // BLK

    def kern(h_ref, g_ref, t_ref, dh_ref, dg_ref, ls_ref):
        i = pl.program_id(0)

        @pl.when(i == 0)
        def _():
            dh_ref[...] = jnp.zeros_like(dh_ref)
            dg_ref[...] = jnp.zeros_like(dg_ref)
            ls_ref[...] = jnp.zeros_like(ls_ref)

        @pl.when(i > 0)
        def _():
            x = h_ref[...]
            r = lax.rsqrt(jnp.mean(x * x, axis=-1, keepdims=True) + EPS)
            xhat = x * r
            err = xhat * g_ref[...] - t_ref[...]
            ls_ref[...] += jnp.sum(jnp.sum(err * err, axis=0, keepdims=True), axis=1, keepdims=True)
            dy = err * (1.0 / D)
            dg_ref[...] += jnp.sum(dy * xhat, axis=0, keepdims=True)
            dxh = dy * g_ref[...]
            dh_ref[...] = r * (dxh - xhat * jnp.mean(dxh * xhat, axis=-1, keepdims=True))

    row = pl.BlockSpec((BLK, D), lambda i: (i, 0))
    vec = pl.BlockSpec((1, D), lambda i: (0, 0))
    return pl.pallas_call(
        kern, grid=(nb,),
        in_specs=[row, vec, pl.BlockSpec((BLK, D), lambda i: (jnp.maximum(i - 1, 0), 0))],
        out_specs=[row, vec, pl.BlockSpec((1, 1), lambda i: (0, 0))],
        out_shape=[jax.ShapeDtypeStruct((lp, D), F32), jax.ShapeDtypeStruct((1, D), F32),
                   jax.ShapeDtypeStruct((1, 1), F32)],
        compiler_params=_cparams("arbitrary"), name="final_loss")(h, gf, tgt)


def _gate_cols(c):
    ct = c[:, :FOX_H].T
    ck = jnp.where(jnp.arange(ct.shape[1]) < PAD, KEY_PAD_BIAS, ct)
    return ct[:, :, None], ck[:, None, :]


def _run(hosts, name, ctx, fn):
    if hosts and name in hosts:
        make, done = hosts[name]
        res = fn(make(ctx))
        done(res[-1])
    else:
        res = fn(None)
    return res[:-1]


def _mm_nn_x(a, b, exch, **kw):
    out = _mm_nn(a, b, exch=exch, **kw)
    return out if exch is not None else (out, None)


def _layer_fwd(h, w, hosts=None):
    xn = _rmsnorm_fwd(h, w["norm1_g"])
    proj_a, = _run(hosts, "proj_a", w,
                   lambda e: _mm_nn_x(xn, w["w_in"], e, n0=0, n=REST0, out_dtype=BF16, name="proj_a"))
    proj_r, = _run(hosts, "proj_r", w, lambda e: _mm_nn_x(xn, w["w_in"], e, n0=REST0, n=NREST, name="proj_r"))
    cq, ck = _gate_cols(_fox_gate_fwd(proj_r, w["bf128"]))
    oa, lse = _run(hosts, "attn_fwd", w, lambda e: _attn_fwd(proj_a, cq, ck, exch=e))
    zb = _convb_fwd(proj_r, w["conv_w"])
    logg = _gla_gate_fwd(proj_r, w["wg2p"], w["gla_b_g"])
    o_gla, zc, st_all = _gla_fwd(proj_r, logg, w["gla_norm_g"])
    ya = _mm_nn(oa, w["w_a_o"], out_dtype=BF16, name="branch_a")
    yb = _mm_nn(zb, w["w_b_o"], out_dtype=BF16, name="branch_b")
    yc = _mm_nn(zc, w["w_c_o"], out_dtype=BF16, name="branch_c")
    mix = _merge_fwd(proj_r, w["gate_b3"], ya, yb, yc)
    h1 = _mm_nn(mix, w["w_o"], res=h, name="out_proj")
    xn2 = _rmsnorm_fwd(h1, w["norm2_g"])
    z, = _run(hosts, "up_proj", w, lambda e: _mm_nn_x(xn2, w["w_up"], e, out_dtype=BF16, name="up_proj"))
    a = _mlp_gate_fwd(z, w["mlp_conv_w"])
    h2, = _run(hosts, "down_proj", w,
               lambda e: _mm_nn_x(a, w["w_down"], e, res=h1, tk=D_FF // 4, name="down_proj"))
    saved = dict(h=h, xn=xn, proj_a=proj_a, proj_r=proj_r, cq=cq, ck=ck, oa=oa, lse=lse, zb=zb, logg=logg,
                 o_gla=o_gla, zc=zc, st_all=st_all, ya=ya, yb=yb, yc=yc, mix=mix, h1=h1, xn2=xn2, z=z, a=a)
    return h2, saved


def _layer_bwd(dh2, w, s, hosts=None):
    g = {}
    da = _mm_nt(dh2, w["w_down"], tn=D_FF // 4, out_dtype=BF16, name="d_down_in")
    g["w_down"] = _mm_tn(s["a"], dh2, tk=D_FF // 4, name="d_w_down")
    dzg, dzu, dmw_g, dmw_u = _run(hosts, "mlp_gate_bwd", g, lambda e: _mlp_gate_bwd(
        s["z"], da, w["mlp_conv_w"], exch=e))
    g["mlp_conv_w"] = jnp.concatenate([dmw_g, dmw_u], axis=1)
    dxn2 = _mm_nt(dzg, w["w_up"], k0=0, kw=D_FF, tk=D_FF // 4, name="d_up_in_g")
    dxn2 = _mm_nt(dzu, w["w_up"], k0=D_FF, kw=D_FF, tk=D_FF // 4, add=dxn2, name="d_up_in_u")
    g["w_up"] = jnp.concatenate([_mm_tn(s["xn2"], dzg, name="d_w_up_g"), _mm_tn(s["xn2"], dzu, name="d_w_up_u")], axis=1)
    dh1, g["norm2_g"] = _rmsnorm_bwd(s["h1"], w["norm2_g"], dxn2, dh2)
    dmix = _mm_nt(dh1, w["w_o"], out_dtype=BF16, name="d_out_proj_in")
    g["w_o"] = _mm_tn(s["mix"], dh1, name="d_w_o")
    dya, dyb, dyc, dproj, g["gate_b3"] = _merge_bwd(s["proj_r"], w["gate_b3"], s["ya"], s["yb"], s["yc"], dmix)
    doa = _mm_nt(dya, w["w_a_o"], out_dtype=BF16, name="d_branch_a_in")
    g["w_a_o"] = _mm_tn(s["oa"], dya, name="d_w_a_o")
    dzb = _mm_nt(dyb, w["w_b_o"], name="d_branch_b_in")
    g["w_b_o"] = _mm_tn(s["zb"], dyb, name="d_w_b_o")
    dzc = _mm_nt(dyc, w["w_c_o"], name="d_branch_c_in")
    g["w_c_o"] = _mm_tn(s["zc"], dyc, name="d_w_c_o")
    dproj, dlogg, g["gla_norm_g"] = _gla_bwd(s["proj_r"], s["logg"], s["st_all"], s["o_gla"], dzc, w["gla_norm_g"], dproj)
    dproj, g["conv_w"] = _convb_bwd(s["proj_r"], dzb, w["conv_w"], dproj)
    dproj, dcq, dck = _run(hosts, "attn_bwd", g, lambda e: _attn_bwd(
        s["proj_a"], doa, s["oa"], s["lse"], s["cq"], s["ck"], dproj, exch=e))
    dc = jnp.pad((dcq[:, :, 0] + dck[:, 0, :]).T, ((0, 0), (0, BLK - FOX_H)))
    dfa, g["bf128"] = _fox_gate_bwd(s["proj_r"], dc, w["bf128"])
    dproj, g["gla_b_g"], g["wg2p"] = _small_bwd(s["proj_r"], dlogg, w["wg2p"], w["wg2p"].T, w["gla_b_g"], dfa, dproj)
    def pair(out, e):
        return out if e is not None else (out, None)

    g["w_in"], = _run(hosts, "d_w_in", g, lambda e: pair(_mm_tn(s["xn"], dproj, name="d_w_in", exch=e), e))
    dxn, = _run(hosts, "d_in_proj_in", g, lambda e: pair(_mm_nt(dproj, w["w_in"], name="d_in_proj_in", exch=e), e))
    dh0, g["norm1_g"] = _rmsnorm_bwd(s["h"], w["norm1_g"], dxn, dh1)
    return dh0, g


def _local_step(x, tgt, meta, final_g, layers, hosts_fwd=None, hosts_bwd=None):
    h = jnp.concatenate([jnp.zeros((PAD, D), F32), meta, x], axis=0)
    saved = []
    for l, w in enumerate(layers):
        h, s = _layer_fwd(h, w, hosts_fwd[l] if hosts_fwd else None)
        saved.append(s)
    dh, dgf, sq = _final_loss(h, final_g, tgt)
    grads = [None] * len(layers)
    for l in reversed(range(len(layers))):
        dh, grads[l] = _layer_bwd(dh, layers[l], saved[l], hosts_bwd[l](grads) if hosts_bwd else None)
    return sq[0, 0], dh[BLK:], dh[PAD:BLK], dgf, grads


def _w_in_to_kernel(w_nat):
    parts = [w_nat[:, s:s + n] for s, n in _segments()]
    parts.append(jnp.zeros((w_nat.shape[0], SMALL_W - 8 - GLA_R), w_nat.dtype))
    return jnp.concatenate(parts, axis=1)


def _w_in_from_kernel(w_k):
    pieces, off = [], 0
    for s, n in _segments():
        pieces.append((s, w_k[:, off:off + n]))
        off += n
    return jnp.concatenate([p for _, p in sorted(pieces, key=lambda t: t[0])], axis=1)


def _w_in_slots_to_kernel(got):
    per = got.shape[2]
    parts = []
    for s, n in _segments():
        while n > 0:
            d, lo = divmod(s, per)
            take = min(n, per - lo)
            parts.append(got[d, :, lo:lo + take])
            s, n = s + take, n - take
    parts.append(jnp.zeros((got.shape[1], SMALL_W - 8 - GLA_R), got.dtype))
    return jnp.concatenate(parts, axis=1)


def _w_in_kernel_to_slots(w_k):
    per = N_IN // N_DEV
    pieces, off = [], 0
    for s, n in _segments():
        pieces.append((s, n, off))
        off += n
    slots = []
    for d in range(N_DEV):
        lo, hi = d * per, (d + 1) * per
        parts = [w_k[:, off + max(s, lo) - s:off + min(s + n, hi) - s]
                 for s, n, off in sorted(pieces) if max(s, lo) < min(s + n, hi)]
        slots.append(jnp.concatenate(parts, axis=1))
    return jnp.stack(slots)


def _pad_rows_at(a, row0, nrows):
    return jnp.pad(a, ((row0, nrows - row0 - a.shape[0]), (0, 0)))


def _big_to_kernel(name, full):
    return _w_in_to_kernel(full) if name == "w_in" else full


def _layer_weights(big, conv_w, gla_w_g2, mlp_conv_w, norm1_g, fox_b_f, gate_b, gla_b_g, gla_norm_g, norm2_g):
    w = {n: _big_to_kernel(n, a) for n, a in big.items()}
    w.update(
        conv_w=conv_w, mlp_conv_w=mlp_conv_w,
        wg2p=_pad_rows_at(gla_w_g2, 8, BLK).astype(BF16),
        norm1_g=norm1_g[None], norm2_g=norm2_g[None], gla_b_g=gla_b_g[None], gla_norm_g=gla_norm_g[None],
        bf128=jnp.pad(fox_b_f, (0, BLK - FOX_H))[None], gate_b3=gate_b.reshape(3, D))
    return w


def _layer_grads_natural(g):
    return dict(
        w_in=_w_in_from_kernel(g["w_in"]), w_a_o=g["w_a_o"], w_b_o=g["w_b_o"], w_c_o=g["w_c_o"], w_o=g["w_o"],
        w_up=g["w_up"], w_down=g["w_down"], conv_w=g["conv_w"], mlp_conv_w=g["mlp_conv_w"],
        gla_w_g2=g["wg2p"][8:8 + GLA_R], norm1_g=g["norm1_g"][0], norm2_g=g["norm2_g"][0],
        gla_b_g=g["gla_b_g"][0], gla_norm_g=g["gla_norm_g"][0], fox_b_f=g["bf128"][0, :FOX_H],
        gate_b=g["gate_b3"].reshape(3 * D))


def _adamw(recv, w, m, v, layer, prev=None, name="adamw"):
    n_slot, r, c = recv.shape
    lyr = w.shape[0]
    tr = r
    for t in range(16, r, 16):
        if r % t == 0 and t * c <= ADAMW_BLOCK_ELEMS:
            tr = t
    if r * c <= ADAMW_BLOCK_ELEMS:
        tr = r
    bc1, bc2 = 1.0 - ADAM_B1 ** ADAM_STEP, 1.0 - ADAM_B2 ** ADAM_STEP

    def kern(*refs):
        r_ref, w_ref, m_ref, v_ref = refs[:4]
        g_out, d_out, m_out, v_out = refs[-4:]
        g = r_ref[0].astype(F32)
        for sidx in range(1, n_slot):
            g = g + r_ref[sidx].astype(F32)
        m_new = ADAM_B1 * m_ref[...] + (1.0 - ADAM_B1) * g
        v_new = ADAM_B2 * v_ref[...] + (1.0 - ADAM_B2) * (g * g)
        g_out[...] = g
        m_out[...] = m_new
        v_out[...] = v_new
        d_out[...] = -ADAM_LR * ((m_new / bc1) / (jnp.sqrt(v_new / bc2) + ADAM_EPS) + ADAM_WD * w_ref[...])

    lspec = pl.BlockSpec((None, tr, c), lambda i: (layer, i, 0))
    in_specs = [pl.BlockSpec((n_slot, tr, c), lambda i: (0, i, 0)), lspec, lspec, lspec]
    args = [recv, w, m, v]
    aliases = {}
    if prev is not None:
        in_specs += [pl.BlockSpec(memory_space=pl.ANY)] * 4
        args += list(prev)
        aliases = {4: 0, 5: 1, 6: 2, 7: 3}
    return pl.pallas_call(
        kern, grid=(r // tr,), in_specs=in_specs, out_specs=[lspec] * 4,
        out_shape=[jax.ShapeDtypeStruct((lyr, r, c), F32)] * 4, input_output_aliases=aliases,
        compiler_params=_cparams("parallel"), name=name)(*args)


_BIG = ("w_in", "w_a_o", "w_b_o", "w_c_o", "w_o", "w_up", "w_down")
_COL_SHARDED = ("w_in", "w_a_o", "w_b_o", "w_c_o", "w_up", "conv_w", "gla_w_g2", "mlp_conv_w")
_REPL = ("norm1_g", "fox_b_f", "gate_b", "gla_b_g", "gla_norm_g", "norm2_g")


def _cols_from_slots(a):
    return jnp.transpose(a, (1, 0, 2)).reshape(a.shape[1], N_DEV * a.shape[2])


def _cols_to_slots(a):
    r, c8 = a.shape
    return jnp.transpose(a.reshape(r, N_DEV, c8 // N_DEV), (1, 0, 2))


def _rows_to_slots(a):
    return a.reshape(N_DEV, a.shape[0] // N_DEV, a.shape[1])


def kernel(x, meta_tokens, norm1_g, w_in, fox_b_f, gate_b, conv_w, gla_w_g2, gla_b_g, gla_norm_g, w_a_o, w_b_o, w_c_o, w_o, norm2_g, w_up, mlp_conv_w, w_down, final_norm_g, loss_target, m_meta_tokens, m_norm1_g, m_w_in, m_fox_b_f, m_gate_b, m_conv_w, m_gla_w_g2, m_gla_b_g, m_gla_norm_g, m_w_a_o, m_w_b_o, m_w_c_o, m_w_o, m_norm2_g, m_w_up, m_mlp_conv_w, m_w_down, m_final_norm_g, v_meta_tokens, v_norm1_g, v_w_in, v_fox_b_f, v_gate_b, v_conv_w, v_gla_w_g2, v_gla_b_g, v_gla_norm_g, v_w_a_o, v_w_b_o, v_w_c_o, v_w_o, v_norm2_g, v_w_up, v_mlp_conv_w, v_w_down, v_final_norm_g):
    names = ("meta_tokens", "norm1_g", "w_in", "fox_b_f", "gate_b", "conv_w", "gla_w_g2", "gla_b_g", "gla_norm_g",
             "w_a_o", "w_b_o", "w_c_o", "w_o", "norm2_g", "w_up", "mlp_conv_w", "w_down", "final_norm_g")
    wts = dict(zip(names, (meta_tokens, norm1_g, w_in, fox_b_f, gate_b, conv_w, gla_w_g2, gla_b_g, gla_norm_g,
                           w_a_o, w_b_o, w_c_o, w_o, norm2_g, w_up, mlp_conv_w, w_down, final_norm_g)))
    mom = dict(zip(names, (m_meta_tokens, m_norm1_g, m_w_in, m_fox_b_f, m_gate_b, m_conv_w, m_gla_w_g2, m_gla_b_g,
                           m_gla_norm_g, m_w_a_o, m_w_b_o, m_w_c_o, m_w_o, m_norm2_g, m_w_up, m_mlp_conv_w, m_w_down,
                           m_final_norm_g)))
    var = dict(zip(names, (v_meta_tokens, v_norm1_g, v_w_in, v_fox_b_f, v_gate_b, v_conv_w, v_gla_w_g2, v_gla_b_g,
                           v_gla_norm_g, v_w_a_o, v_w_b_o, v_w_c_o, v_w_o, v_norm2_g, v_w_up, v_mlp_conv_w, v_w_down,
                           v_final_norm_g)))

    small = _exchange([conv_w, gla_w_g2, mlp_conv_w, meta_tokens], [True] * 4, "gather_small")
    conv_full = jnp.transpose(small[0], (1, 2, 0, 3)).reshape(DEPTH, 3, CONV_CH)
    g2_full = jnp.transpose(small[1], (1, 2, 0, 3)).reshape(DEPTH, GLA_R, GLA_H * GLA_DK)
    mconv_full = jnp.transpose(small[2], (1, 2, 0, 3)).reshape(DEPTH, 3, 2 * D_FF)
    meta_full = _cols_from_slots(small[3])
    layers = [_layer_weights({}, conv_full[l], g2_full[l], mconv_full[l], norm1_g[l], fox_b_f[l], gate_b[l],
                             gla_b_g[l], gla_norm_g[l], norm2_g[l]) for l in range(DEPTH)]

    wide = ("w_a_o", "w_b_o", "w_c_o", "w_up")

    def gather(l, which):
        def make(_):
            return [wts[n][l].astype(BF16) for n in which], [("wide" if n in wide else True) for n in which]

        def done(got):
            for n, a in zip(which, got):
                if n == "w_in":
                    layers[l][n] = _w_in_slots_to_kernel(a)
                else:
                    layers[l][n] = a if n in wide else a.reshape(-1, a.shape[-1])

        return make, done

    recv_big = [dict() for _ in range(DEPTH)]

    def scatter(l, which, grads_of):
        def make(ctx):
            g = grads_of(ctx)
            send = [_w_in_kernel_to_slots(g[n]) if n == "w_in" else g[n] if n in wide else _rows_to_slots(g[n])
                    for n in which]
            return send, [("cols" if n in wide else False) for n in which]

        def done(got):
            recv_big[l].update(zip(which, got))

        return make, done

    mixers = ("w_o", "w_a_o", "w_b_o", "w_c_o")
    layers[0]["w_in"] = _w_in_slots_to_kernel(_gather_by_chip(w_in[0].astype(BF16), "gather_w_in"))
    hosts_fwd = [
        {"proj_a": gather(0, mixers[1:]), "proj_r": gather(0, ("w_up", "w_o")), "attn_fwd": gather(1, ("w_in",)),
         "up_proj": gather(0, ("w_down",)), "down_proj": gather(1, mixers)},
        {"attn_fwd": gather(1, ("w_up", "w_down"))}]
    def both(first, n_first, second):
        def make(ctx):
            (a1, b1), (a2, b2) = first[0](ctx), second[0](ctx)
            return a1 + a2, b1 + b2

        def done(got):
            first[1](got[:n_first])
            second[1](got[n_first:])

        return make, done

    hosts_bwd = [
        lambda grads: {"mlp_gate_bwd": scatter(1, ("w_in",), lambda _: grads[1]),
                       "attn_bwd": both(scatter(1, ("w_down",) + mixers, lambda _: grads[1]), 1 + len(mixers),
                                        scatter(0, ("w_down", "w_up"), lambda g: g)),
                       "d_w_in": both(scatter(0, mixers, lambda g: g), len(mixers),
                                      scatter(1, ("w_up",), lambda _: grads[1])),
                       "d_in_proj_in": scatter(0, ("w_in",), lambda g: g)},
        lambda grads: None]

    sq, grad_x, dmeta, dgf, grads_k = _local_step(x[0], loss_target[0], meta_full, final_norm_g[None], layers,
                                                  hosts_fwd, hosts_bwd)
    loss = lax.psum(sq * (0.5 / D), ("x", "y", "c"))
    grads = [_layer_grads_natural(g) for g in grads_k]

    out_g, out_d, out_m, out_v = {}, {}, {}, {}

    def update(name, recv, layer, lyr_shape, prev):
        w3, m3, v3 = (t[name].reshape(lyr_shape) for t in (wts, mom, var))
        return _adamw(recv.reshape((recv.shape[0],) + lyr_shape[1:]), w3, m3, v3, layer, prev, name="adamw_" + name)

    def store(name, res):
        shape = wts[name].shape
        out_g[name], out_d[name], out_m[name], out_v[name] = (t.reshape(shape) for t in res)

    for n in _BIG:
        res = None
        for l in range(DEPTH):
            res = update(n, recv_big[l][n], l, wts[n].shape, res)
        store(n, res)

    def stack_layers(name):
        return jnp.stack([grads[l][name] for l in range(DEPTH)])

    s_conv = jnp.transpose(stack_layers("conv_w").reshape(DEPTH, 3, N_DEV, -1), (2, 0, 1, 3))
    s_g2 = jnp.transpose(stack_layers("gla_w_g2").reshape(DEPTH, GLA_R, N_DEV, -1), (2, 0, 1, 3))
    s_mconv = jnp.transpose(stack_layers("mlp_conv_w").reshape(DEPTH, 3, N_DEV, -1), (2, 0, 1, 3))
    s_meta = _cols_to_slots(dmeta)
    repl = [stack_layers(n) for n in _REPL] + [dgf]
    pack = jnp.concatenate([jnp.pad(a.reshape(-1), (0, (-a.size) % 1024)) for a in repl]).reshape(-1, BLK)
    r_conv, r_g2, r_mconv, r_meta, r_pack = _exchange(
        [s_conv, s_g2, s_mconv, s_meta, pack], [False, False, False, False, True], "scatter_small")
    store("conv_w", update("conv_w", r_conv, 0, (1, DEPTH * 3, CONV_CH // N_DEV), None))
    store("gla_w_g2", update("gla_w_g2", r_g2, 0, (1, DEPTH * GLA_R, GLA_H * GLA_DK // N_DEV), None))
    store("mlp_conv_w", update("mlp_conv_w", r_mconv, 0, (1, DEPTH * 3, 2 * D_FF // N_DEV), None))
    store("meta_tokens", update("meta_tokens", r_meta, 0, (1, N_META, D // N_DEV), None))
    off = 0
    for n, a in zip(_REPL + ("final_norm_g",), repl):
        rows = (a.size + 1023) // 1024 * 8
        part = r_pack[:, off:off + rows].reshape(N_DEV, -1)[:, :a.size]
        off += rows
        shape2 = (1, 1, a.size) if a.size % BLK else (1, a.size // BLK, BLK)
        store(n, update(n, part, 0, shape2, None))

    order = lambda d: [d[n] for n in names]
    return (loss, grad_x[None], *order(out_g), *order(out_d), *order(out_m), *order(out_v))
```
